```python
import jax, jax.numpy as jnp
from jax import lax
import numpy as np

D_MODEL = 2048
BATCH = 8
SEQ = 4096
DEPTH = 1

HEAD_DIM = 128
HEADS_PER_GROUP = 4
DILATED_GROUPS = ((128, 1), (512, 4), (2048, 16))
N_ATT_GROUPS = 3
N_ATT_HEADS = N_ATT_GROUPS * HEADS_PER_GROUP
ATT_WIDTH = N_ATT_HEADS * HEAD_DIM
ATT_OUT_WIDTH = HEADS_PER_GROUP * HEAD_DIM
ROPE_DIM = HEAD_DIM // 4
ROPE_THETA = 500000.0
ATT_BLOCK = 128

SG_CHUNK = 128
SG_GROUPS = 12
SG_GROUP_DIM = 128
SG_WIDTH = SG_GROUPS * SG_GROUP_DIM

D_FF = 5632

NORM_EPS = 1e-6
LN_EPS = 1e-5
IN_WIDTH = 3 * ATT_WIDTH + 2 * SG_WIDTH + 2 * D_MODEL

kernel_name = "dilated_attn_gmlp_gated_macaron_layer"


def rmsnorm(x, g):
    x32 = x.astype(jnp.float32)
    r = x32 * lax.rsqrt(jnp.mean(x32 * x32, axis=-1, keepdims=True) + NORM_EPS)
    return (r * g.astype(jnp.float32)).astype(x.dtype)


def layernorm(x, g, b):
    x32 = x.astype(jnp.float32)
    mu = jnp.mean(x32, axis=-1, keepdims=True)
    var = jnp.mean(jnp.square(x32 - mu), axis=-1, keepdims=True)
    y = (x32 - mu) * lax.rsqrt(var + LN_EPS)
    return (y * g.astype(jnp.float32) + b.astype(jnp.float32)).astype(x.dtype)


def swiglu(x, w_gate, w_up, w_down):
    return (jax.nn.silu(x @ w_gate) * (x @ w_up)) @ w_down


def partial_rope(t, pos):
    half = ROPE_DIM // 2
    inv_freq = ROPE_THETA ** (-jnp.arange(0, ROPE_DIM, 2, dtype=jnp.float32) / ROPE_DIM)
    ang = pos.astype(jnp.float32)[:, None] * inv_freq[None, :]
    ang = ang.reshape((1, t.shape[1]) + (1,) * (t.ndim - 3) + (half,))
    cos = jnp.cos(ang).astype(t.dtype)
    sin = jnp.sin(ang).astype(t.dtype)
    x1 = t[..., :half]
    x2 = t[..., half:ROPE_DIM]
    return jnp.concatenate([x1 * cos - x2 * sin, x2 * cos + x1 * sin, t[..., ROPE_DIM:]], axis=-1)


def dilated_window_attention(q, k, v, window, dilation):
    b, s, h, hd = q.shape
    L = s // dilation
    n_blk = -(-L // ATT_BLOCK)
    Lp = n_blk * ATT_BLOCK
    reach = window // dilation

    def by_residue(t):
        t = t.reshape(b, L, dilation, h, hd).transpose(0, 3, 2, 1, 4)
        t = jnp.pad(t, ((0, 0), (0, 0), (0, 0), (0, Lp - L), (0, 0)))
        return t.reshape(b, h, dilation, n_blk, ATT_BLOCK, hd)

    qb, kb, vb = by_residue(q), by_residue(k), by_residue(v)

    def with_prev(t):
        prev = jnp.pad(t[:, :, :, :-1], ((0, 0), (0, 0), (0, 0), (1, 0), (0, 0), (0, 0)))
        return jnp.concatenate([prev, t], axis=4)

    kw, vw = with_prev(kb), with_prev(vb)
    scores = jnp.einsum('bhrnqd,bhrnkd->bhrnqk', qb, kw).astype(jnp.float32) * (hd ** -0.5)
    qi = jnp.arange(ATT_BLOCK)[None, :, None]
    kj = jnp.arange(2 * ATT_BLOCK)[None, None, :]
    blk = jnp.arange(n_blk)[:, None, None]
    diff = qi + ATT_BLOCK - kj
    key_idx = blk * ATT_BLOCK - ATT_BLOCK + kj
    valid = (diff >= 0) & (diff <= reach) & (key_idx >= 0)
    scores = jnp.where(valid[None, None, None], scores, -jnp.inf)
    m = jnp.max(scores, axis=-1, keepdims=True)
    p = jnp.exp(scores - m)
    l = jnp.sum(p, axis=-1, keepdims=True)
    o = jnp.einsum('bhrnqk,bhrnkd->bhrnqd', p.astype(vw.dtype), vw).astype(jnp.float32) / l

    o = o.reshape(b, h, dilation, Lp, hd)[:, :, :, :L].transpose(0, 3, 2, 1, 4).reshape(b, s, h, hd)

    def stat_back(t):
        t = t.reshape(b, h, dilation, Lp)[:, :, :, :L]
        return t.transpose(0, 3, 2, 1).reshape(b, s, h)

    return o, stat_back(m), stat_back(l)


def hybrid_mixer(h, w_in, sg_ln_g, sg_ln_b, sg_w, sg_b, w_att_out, w_sg_out, w_out):
    b, s, _ = h.shape
    proj = h @ w_in
    splits = np.cumsum([ATT_WIDTH, ATT_WIDTH, ATT_WIDTH, SG_WIDTH, SG_WIDTH, D_MODEL]).tolist()
    q, k, v, u, vs, g_att, g_sg = jnp.split(proj, splits, axis=-1)

    pos = jnp.arange(s)
    q = partial_rope(q.reshape(b, s, N_ATT_GROUPS, HEADS_PER_GROUP, HEAD_DIM), pos)
    k = partial_rope(k.reshape(b, s, N_ATT_GROUPS, HEADS_PER_GROUP, HEAD_DIM), pos)
    v = v.reshape(b, s, N_ATT_GROUPS, HEADS_PER_GROUP, HEAD_DIM)
    outs, maxes, dens = [], [], []
    for gi, (window, dilation) in enumerate(DILATED_GROUPS):
        o_g, m_g, l_g = dilated_window_attention(q[:, :, gi], k[:, :, gi], v[:, :, gi], window, dilation)
        outs.append(o_g)
        maxes.append(m_g)
        dens.append(l_g)
    o_all = jnp.stack(outs)
    m_all = jnp.stack(maxes)
    l_all = jnp.stack(dens)
    w_den = l_all * jnp.exp(m_all - jnp.max(m_all, axis=0, keepdims=True))
    o_att = jnp.sum(w_den[..., None] * o_all, axis=0) / jnp.sum(w_den, axis=0)[..., None]
    y_att = o_att.astype(h.dtype).reshape(b, s, ATT_OUT_WIDTH) @ w_att_out

    u = jax.nn.gelu(u, approximate=False)
    vs = layernorm(jax.nn.gelu(vs, approximate=False), sg_ln_g, sg_ln_b)
    vc = vs.reshape(b, s // SG_CHUNK, SG_CHUNK, SG_GROUPS, SG_GROUP_DIM)
    causal = jnp.tril(jnp.ones((SG_CHUNK, SG_CHUNK), dtype=sg_w.dtype))
    w_sp = sg_w * causal[None]
    spatial = jnp.einsum('gts,bcsgd->bctgd', w_sp, vc) + sg_b.T[None, None, :, :, None]
    y_sg = (u * spatial.reshape(b, s, SG_WIDTH)) @ w_sg_out

    merged = jax.nn.sigmoid(g_att) * y_att + jax.nn.sigmoid(g_sg) * y_sg
    return merged @ w_out


def _fwd_setup_inputs(seed: int = 0) -> dict:
    key = jax.random.key(seed)
    ks = jax.random.split(key, 24)
    f32 = jnp.float32

    def nrm(k, shape, fan_in):
        return jax.random.normal(k, shape, f32) * (fan_in ** -0.5)

    def gain(k, shape):
        return 1.0 + 0.05 * jax.random.normal(k, shape, f32)

    return {
        "x": jax.random.normal(ks[0], (BATCH, SEQ, D_MODEL), f32),
        "ffn1_norm": gain(ks[1], (DEPTH, D_MODEL)),
        "ffn1_w_gate": nrm(ks[2], (DEPTH, D_MODEL, D_FF), D_MODEL),
        "ffn1_w_up": nrm(ks[3], (DEPTH, D_MODEL, D_FF), D_MODEL),
        "ffn1_w_down": nrm(ks[4], (DEPTH, D_FF, D_MODEL), D_FF),
        "mix_norm": gain(ks[5], (DEPTH, D_MODEL)),
        "w_in": nrm(ks[6], (DEPTH, D_MODEL, IN_WIDTH), D_MODEL),
        "sg_ln_g": gain(ks[7], (DEPTH, SG_WIDTH)),
        "sg_ln_b": 0.02 * jax.random.normal(ks[8], (DEPTH, SG_WIDTH), f32),
        "sg_w": nrm(ks[9], (DEPTH, SG_GROUPS, SG_CHUNK, SG_CHUNK), SG_CHUNK),
        "sg_b": gain(ks[10], (DEPTH, SG_GROUPS, SG_CHUNK)),
        "w_att_out": nrm(ks[11], (DEPTH, ATT_OUT_WIDTH, D_MODEL), ATT_OUT_WIDTH),
        "w_sg_out": nrm(ks[12], (DEPTH, SG_WIDTH, D_MODEL), SG_WIDTH),
        "w_out": nrm(ks[13], (DEPTH, D_MODEL, D_MODEL), D_MODEL),
        "ffn2_norm": gain(ks[14], (DEPTH, D_MODEL)),
        "ffn2_w_gate": nrm(ks[15], (DEPTH, D_MODEL, D_FF), D_MODEL),
        "ffn2_w_up": nrm(ks[16], (DEPTH, D_MODEL, D_FF), D_MODEL),
        "ffn2_w_down": nrm(ks[17], (DEPTH, D_FF, D_MODEL), D_FF),
        "final_norm": gain(ks[18], (D_MODEL,)),
    }


def _fwd_reference(x, ffn1_norm, ffn1_w_gate, ffn1_w_up, ffn1_w_down, mix_norm, w_in, sg_ln_g, sg_ln_b,
              sg_w, sg_b, w_att_out, w_sg_out, w_out, ffn2_norm, ffn2_w_gate, ffn2_w_up, ffn2_w_down,
              final_norm):
    for i in range(DEPTH):
        x = x + 0.5 * swiglu(rmsnorm(x, ffn1_norm[i]), ffn1_w_gate[i], ffn1_w_up[i], ffn1_w_down[i])
        x = x + hybrid_mixer(rmsnorm(x, mix_norm[i]), w_in[i], sg_ln_g[i], sg_ln_b[i], sg_w[i], sg_b[i],
                             w_att_out[i], w_sg_out[i], w_out[i])
        x = x + 0.5 * swiglu(rmsnorm(x, ffn2_norm[i]), ffn2_w_gate[i], ffn2_w_up[i], ffn2_w_down[i])
    return rmsnorm(x, final_norm)


import jax as _jax
import jax.numpy as _jnp

TWIN_FORMAT = 'train_step'
FWD_PARAMS = ['x', 'ffn1_norm', 'ffn1_w_gate', 'ffn1_w_up', 'ffn1_w_down', 'mix_norm', 'w_in', 'sg_ln_g', 'sg_ln_b', 'sg_w', 'sg_b', 'w_att_out', 'w_sg_out', 'w_out', 'ffn2_norm', 'ffn2_w_gate', 'ffn2_w_up', 'ffn2_w_down', 'final_norm']
TWIN_WEIGHTS = ['ffn1_norm', 'ffn1_w_gate', 'ffn1_w_up', 'ffn1_w_down', 'mix_norm', 'w_in', 'sg_ln_g', 'sg_ln_b', 'sg_w', 'sg_b', 'w_att_out', 'w_sg_out', 'w_out', 'ffn2_norm', 'ffn2_w_gate', 'ffn2_w_up', 'ffn2_w_down', 'final_norm']
TWIN_DIFF_INPUT = 'x'
TWIN_INPUTS = ['x', 'ffn1_norm', 'ffn1_w_gate', 'ffn1_w_up', 'ffn1_w_down', 'mix_norm', 'w_in', 'sg_ln_g', 'sg_ln_b', 'sg_w', 'sg_b', 'w_att_out', 'w_sg_out', 'w_out', 'ffn2_norm', 'ffn2_w_gate', 'ffn2_w_up', 'ffn2_w_down', 'final_norm', 'loss_target', 'm_ffn1_norm', 'm_ffn1_w_gate', 'm_ffn1_w_up', 'm_ffn1_w_down', 'm_mix_norm', 'm_w_in', 'm_sg_ln_g', 'm_sg_ln_b', 'm_sg_w', 'm_sg_b', 'm_w_att_out', 'm_w_sg_out', 'm_w_out', 'm_ffn2_norm', 'm_ffn2_w_gate', 'm_ffn2_w_up', 'm_ffn2_w_down', 'm_final_norm', 'v_ffn1_norm', 'v_ffn1_w_gate', 'v_ffn1_w_up', 'v_ffn1_w_down', 'v_mix_norm', 'v_w_in', 'v_sg_ln_g', 'v_sg_ln_b', 'v_sg_w', 'v_sg_b', 'v_w_att_out', 'v_w_sg_out', 'v_w_out', 'v_ffn2_norm', 'v_ffn2_w_gate', 'v_ffn2_w_up', 'v_ffn2_w_down', 'v_final_norm']
TWIN_OUTPUTS = ['loss', 'grad_x', 'grad_ffn1_norm', 'grad_ffn1_w_gate', 'grad_ffn1_w_up', 'grad_ffn1_w_down', 'grad_mix_norm', 'grad_w_in', 'grad_sg_ln_g', 'grad_sg_ln_b', 'grad_sg_w', 'grad_sg_b', 'grad_w_att_out', 'grad_w_sg_out', 'grad_w_out', 'grad_ffn2_norm', 'grad_ffn2_w_gate', 'grad_ffn2_w_up', 'grad_ffn2_w_down', 'grad_final_norm', 'delta_ffn1_norm', 'delta_ffn1_w_gate', 'delta_ffn1_w_up', 'delta_ffn1_w_down', 'delta_mix_norm', 'delta_w_in', 'delta_sg_ln_g', 'delta_sg_ln_b', 'delta_sg_w', 'delta_sg_b', 'delta_w_att_out', 'delta_w_sg_out', 'delta_w_out', 'delta_ffn2_norm', 'delta_ffn2_w_gate', 'delta_ffn2_w_up', 'delta_ffn2_w_down', 'delta_final_norm', 'new_m_ffn1_norm', 'new_m_ffn1_w_gate', 'new_m_ffn1_w_up', 'new_m_ffn1_w_down', 'new_m_mix_norm', 'new_m_w_in', 'new_m_sg_ln_g', 'new_m_sg_ln_b', 'new_m_sg_w', 'new_m_sg_b', 'new_m_w_att_out', 'new_m_w_sg_out', 'new_m_w_out', 'new_m_ffn2_norm', 'new_m_ffn2_w_gate', 'new_m_ffn2_w_up', 'new_m_ffn2_w_down', 'new_m_final_norm', 'new_v_ffn1_norm', 'new_v_ffn1_w_gate', 'new_v_ffn1_w_up', 'new_v_ffn1_w_down', 'new_v_mix_norm', 'new_v_w_in', 'new_v_sg_ln_g', 'new_v_sg_ln_b', 'new_v_sg_w', 'new_v_sg_b', 'new_v_w_att_out', 'new_v_w_sg_out', 'new_v_w_out', 'new_v_ffn2_norm', 'new_v_ffn2_w_gate', 'new_v_ffn2_w_up', 'new_v_ffn2_w_down', 'new_v_final_norm']
TWIN_LEAF_KINDS = {'loss': 'loss', 'grad_x': 'grad_x', 'grad_ffn1_norm': 'grad_w', 'grad_ffn1_w_gate': 'grad_w', 'grad_ffn1_w_up': 'grad_w', 'grad_ffn1_w_down': 'grad_w', 'grad_mix_norm': 'grad_w', 'grad_w_in': 'grad_w', 'grad_sg_ln_g': 'grad_w', 'grad_sg_ln_b': 'grad_w', 'grad_sg_w': 'grad_w', 'grad_sg_b': 'grad_w', 'grad_w_att_out': 'grad_w', 'grad_w_sg_out': 'grad_w', 'grad_w_out': 'grad_w', 'grad_ffn2_norm': 'grad_w', 'grad_ffn2_w_gate': 'grad_w', 'grad_ffn2_w_up': 'grad_w', 'grad_ffn2_w_down': 'grad_w', 'grad_final_norm': 'grad_w', 'delta_ffn1_norm': 'delta_w', 'delta_ffn1_w_gate': 'delta_w', 'delta_ffn1_w_up': 'delta_w', 'delta_ffn1_w_down': 'delta_w', 'delta_mix_norm': 'delta_w', 'delta_w_in': 'delta_w', 'delta_sg_ln_g': 'delta_w', 'delta_sg_ln_b': 'delta_w', 'delta_sg_w': 'delta_w', 'delta_sg_b': 'delta_w', 'delta_w_att_out': 'delta_w', 'delta_w_sg_out': 'delta_w', 'delta_w_out': 'delta_w', 'delta_ffn2_norm': 'delta_w', 'delta_ffn2_w_gate': 'delta_w', 'delta_ffn2_w_up': 'delta_w', 'delta_ffn2_w_down': 'delta_w', 'delta_final_norm': 'delta_w', 'new_m_ffn1_norm': 'new_m', 'new_m_ffn1_w_gate': 'new_m', 'new_m_ffn1_w_up': 'new_m', 'new_m_ffn1_w_down': 'new_m', 'new_m_mix_norm': 'new_m', 'new_m_w_in': 'new_m', 'new_m_sg_ln_g': 'new_m', 'new_m_sg_ln_b': 'new_m', 'new_m_sg_w': 'new_m', 'new_m_sg_b': 'new_m', 'new_m_w_att_out': 'new_m', 'new_m_w_sg_out': 'new_m', 'new_m_w_out': 'new_m', 'new_m_ffn2_norm': 'new_m', 'new_m_ffn2_w_gate': 'new_m', 'new_m_ffn2_w_up': 'new_m', 'new_m_ffn2_w_down': 'new_m', 'new_m_final_norm': 'new_m', 'new_v_ffn1_norm': 'new_v', 'new_v_ffn1_w_gate': 'new_v', 'new_v_ffn1_w_up': 'new_v', 'new_v_ffn1_w_down': 'new_v', 'new_v_mix_norm': 'new_v', 'new_v_w_in': 'new_v', 'new_v_sg_ln_g': 'new_v', 'new_v_sg_ln_b': 'new_v', 'new_v_sg_w': 'new_v', 'new_v_sg_b': 'new_v', 'new_v_w_att_out': 'new_v', 'new_v_w_sg_out': 'new_v', 'new_v_w_out': 'new_v', 'new_v_ffn2_norm': 'new_v', 'new_v_ffn2_w_gate': 'new_v', 'new_v_ffn2_w_up': 'new_v', 'new_v_ffn2_w_down': 'new_v', 'new_v_final_norm': 'new_v'}


def _forward(args):
    return _fwd_reference(*[args[k] for k in FWD_PARAMS])


def _output_shape():
    def fwd():
        inp = _fwd_setup_inputs(0)
        return _fwd_reference(*[inp[k] for k in FWD_PARAMS])
    out = _jax.eval_shape(fwd)
    return out.shape, out.dtype

N_MICROBATCH = 1
ADAM_LR = 0.001
ADAM_B1 = 0.9
ADAM_B2 = 0.999
ADAM_EPS = 1e-08
ADAM_WD = 0.01
ADAM_STEP = 10
PER_EXAMPLE_BATCH_AXIS = {'x': 0, 'loss_target': 0}
SHARED_INPUTS = []
_WEIGHT_DTYPES = {'ffn1_norm': _jnp.float32, 'ffn1_w_gate': _jnp.float32, 'ffn1_w_up': _jnp.float32, 'ffn1_w_down': _jnp.float32, 'mix_norm': _jnp.float32, 'w_in': _jnp.float32, 'sg_ln_g': _jnp.float32, 'sg_ln_b': _jnp.float32, 'sg_w': _jnp.float32, 'sg_b': _jnp.float32, 'w_att_out': _jnp.float32, 'w_sg_out': _jnp.float32, 'w_out': _jnp.float32, 'ffn2_norm': _jnp.float32, 'ffn2_w_gate': _jnp.float32, 'ffn2_w_up': _jnp.float32, 'ffn2_w_down': _jnp.float32, 'final_norm': _jnp.float32}
MOMENT_SCALE = {'ffn1_norm': 4.083898e-02, 'ffn1_w_gate': 1.729848e-02, 'ffn1_w_up': 1.675222e-02, 'ffn1_w_down': 2.778053e-02, 'mix_norm': 4.771933e-02, 'w_in': 1.969919e-02, 'sg_ln_g': 2.332696e-02, 'sg_ln_b': 2.334773e-02, 'sg_w': 2.340320e-02, 'sg_b': 3.387150e-02, 'w_att_out': 8.263968e-03, 'w_sg_out': 3.958017e-02, 'w_out': 3.963028e-02, 'ffn2_norm': 3.351464e-02, 'ffn2_w_gate': 1.419163e-02, 'ffn2_w_up': 1.383012e-02, 'ffn2_w_down': 2.293234e-02, 'final_norm': 1.603615e+01}


def _to_microbatches(a, axis):
    t = _jnp.moveaxis(a, axis, 0)
    t = t.reshape((N_MICROBATCH, t.shape[0] // N_MICROBATCH) + t.shape[1:])
    return _jnp.moveaxis(t, 1, axis + 1)


def setup_inputs(seed: int = 0) -> dict:
    inp = _fwd_setup_inputs(seed)
    key = _jax.random.fold_in(_jax.random.key(seed), 7919)
    shape, _ = _output_shape()
    out = dict(inp)
    out["loss_target"] = _jax.random.normal(_jax.random.fold_in(key, 0), shape, _jnp.float32)
    for i, name in enumerate(TWIN_WEIGHTS):
        w = inp[name].astype(_jnp.float32)
        if MOMENT_SCALE is None:
            s = _jnp.sqrt(_jnp.mean(_jnp.square(w)) + 1e-30)
        else:
            s = MOMENT_SCALE[name]
        km, kv = _jax.random.split(_jax.random.fold_in(key, i + 1))
        out[name] = w
        out["m_" + name] = s * _jax.random.normal(km, w.shape, _jnp.float32)
        out["v_" + name] = (s * s) * _jax.random.uniform(kv, w.shape, _jnp.float32, 0.5, 1.5)
    if N_MICROBATCH > 1:
        for name, axis in PER_EXAMPLE_BATCH_AXIS.items():
            out[name] = _to_microbatches(out[name], axis)
    return {'x': out['x'], 'ffn1_norm': out['ffn1_norm'], 'ffn1_w_gate': out['ffn1_w_gate'], 'ffn1_w_up': out['ffn1_w_up'], 'ffn1_w_down': out['ffn1_w_down'], 'mix_norm': out['mix_norm'], 'w_in': out['w_in'], 'sg_ln_g': out['sg_ln_g'], 'sg_ln_b': out['sg_ln_b'], 'sg_w': out['sg_w'], 'sg_b': out['sg_b'], 'w_att_out': out['w_att_out'], 'w_sg_out': out['w_sg_out'], 'w_out': out['w_out'], 'ffn2_norm': out['ffn2_norm'], 'ffn2_w_gate': out['ffn2_w_gate'], 'ffn2_w_up': out['ffn2_w_up'], 'ffn2_w_down': out['ffn2_w_down'], 'final_norm': out['final_norm'], 'loss_target': out['loss_target'], 'm_ffn1_norm': out['m_ffn1_norm'], 'm_ffn1_w_gate': out['m_ffn1_w_gate'], 'm_ffn1_w_up': out['m_ffn1_w_up'], 'm_ffn1_w_down': out['m_ffn1_w_down'], 'm_mix_norm': out['m_mix_norm'], 'm_w_in': out['m_w_in'], 'm_sg_ln_g': out['m_sg_ln_g'], 'm_sg_ln_b': out['m_sg_ln_b'], 'm_sg_w': out['m_sg_w'], 'm_sg_b': out['m_sg_b'], 'm_w_att_out': out['m_w_att_out'], 'm_w_sg_out': out['m_w_sg_out'], 'm_w_out': out['m_w_out'], 'm_ffn2_norm': out['m_ffn2_norm'], 'm_ffn2_w_gate': out['m_ffn2_w_gate'], 'm_ffn2_w_up': out['m_ffn2_w_up'], 'm_ffn2_w_down': out['m_ffn2_w_down'], 'm_final_norm': out['m_final_norm'], 'v_ffn1_norm': out['v_ffn1_norm'], 'v_ffn1_w_gate': out['v_ffn1_w_gate'], 'v_ffn1_w_up': out['v_ffn1_w_up'], 'v_ffn1_w_down': out['v_ffn1_w_down'], 'v_mix_norm': out['v_mix_norm'], 'v_w_in': out['v_w_in'], 'v_sg_ln_g': out['v_sg_ln_g'], 'v_sg_ln_b': out['v_sg_ln_b'], 'v_sg_w': out['v_sg_w'], 'v_sg_b': out['v_sg_b'], 'v_w_att_out': out['v_w_att_out'], 'v_w_sg_out': out['v_w_sg_out'], 'v_w_out': out['v_w_out'], 'v_ffn2_norm': out['v_ffn2_norm'], 'v_ffn2_w_gate': out['v_ffn2_w_gate'], 'v_ffn2_w_up': out['v_ffn2_w_up'], 'v_ffn2_w_down': out['v_ffn2_w_down'], 'v_final_norm': out['v_final_norm']}


def _loss(weights, diff, rest, loss_target):
    with _jax.named_scope("forward"):
        args = {**rest, TWIN_DIFF_INPUT: diff, **{k: w.astype(_WEIGHT_DTYPES[k]) for k, w in weights.items()}}
        y = _forward(args)
    with _jax.named_scope("loss_head"):
        err = _jnp.square(y.astype(_jnp.float32) - loss_target)
        return 0.5 * _jnp.sum(_jnp.mean(err, axis=-1)) if err.ndim else 0.5 * err


def _adamw(w, g, m, v):
    m = ADAM_B1 * m + (1.0 - ADAM_B1) * g
    v = ADAM_B2 * v + (1.0 - ADAM_B2) * _jnp.square(g)
    m_hat = m / (1.0 - ADAM_B1 ** ADAM_STEP)
    v_hat = v / (1.0 - ADAM_B2 ** ADAM_STEP)
    delta = -ADAM_LR * (m_hat / (_jnp.sqrt(v_hat) + ADAM_EPS) + ADAM_WD * w)
    return delta, m, v


def reference(x, ffn1_norm, ffn1_w_gate, ffn1_w_up, ffn1_w_down, mix_norm, w_in, sg_ln_g, sg_ln_b, sg_w, sg_b, w_att_out, w_sg_out, w_out, ffn2_norm, ffn2_w_gate, ffn2_w_up, ffn2_w_down, final_norm, loss_target, m_ffn1_norm, m_ffn1_w_gate, m_ffn1_w_up, m_ffn1_w_down, m_mix_norm, m_w_in, m_sg_ln_g, m_sg_ln_b, m_sg_w, m_sg_b, m_w_att_out, m_w_sg_out, m_w_out, m_ffn2_norm, m_ffn2_w_gate, m_ffn2_w_up, m_ffn2_w_down, m_final_norm, v_ffn1_norm, v_ffn1_w_gate, v_ffn1_w_up, v_ffn1_w_down, v_mix_norm, v_w_in, v_sg_ln_g, v_sg_ln_b, v_sg_w, v_sg_b, v_w_att_out, v_w_sg_out, v_w_out, v_ffn2_norm, v_ffn2_w_gate, v_ffn2_w_up, v_ffn2_w_down, v_final_norm):
    given = dict(x=x, ffn1_norm=ffn1_norm, ffn1_w_gate=ffn1_w_gate, ffn1_w_up=ffn1_w_up, ffn1_w_down=ffn1_w_down, mix_norm=mix_norm, w_in=w_in, sg_ln_g=sg_ln_g, sg_ln_b=sg_ln_b, sg_w=sg_w, sg_b=sg_b, w_att_out=w_att_out, w_sg_out=w_sg_out, w_out=w_out, ffn2_norm=ffn2_norm, ffn2_w_gate=ffn2_w_gate, ffn2_w_up=ffn2_w_up, ffn2_w_down=ffn2_w_down, final_norm=final_norm, loss_target=loss_target, m_ffn1_norm=m_ffn1_norm, m_ffn1_w_gate=m_ffn1_w_gate, m_ffn1_w_up=m_ffn1_w_up, m_ffn1_w_down=m_ffn1_w_down, m_mix_norm=m_mix_norm, m_w_in=m_w_in, m_sg_ln_g=m_sg_ln_g, m_sg_ln_b=m_sg_ln_b, m_sg_w=m_sg_w, m_sg_b=m_sg_b, m_w_att_out=m_w_att_out, m_w_sg_out=m_w_sg_out, m_w_out=m_w_out, m_ffn2_norm=m_ffn2_norm, m_ffn2_w_gate=m_ffn2_w_gate, m_ffn2_w_up=m_ffn2_w_up, m_ffn2_w_down=m_ffn2_w_down, m_final_norm=m_final_norm, v_ffn1_norm=v_ffn1_norm, v_ffn1_w_gate=v_ffn1_w_gate, v_ffn1_w_up=v_ffn1_w_up, v_ffn1_w_down=v_ffn1_w_down, v_mix_norm=v_mix_norm, v_w_in=v_w_in, v_sg_ln_g=v_sg_ln_g, v_sg_ln_b=v_sg_ln_b, v_sg_w=v_sg_w, v_sg_b=v_sg_b, v_w_att_out=v_w_att_out, v_w_sg_out=v_w_sg_out, v_w_out=v_w_out, v_ffn2_norm=v_ffn2_norm, v_ffn2_w_gate=v_ffn2_w_gate, v_ffn2_w_up=v_ffn2_w_up, v_ffn2_w_down=v_ffn2_w_down, v_final_norm=v_final_norm)
    weights = {n: given[n] for n in TWIN_WEIGHTS}
    shared = {n: given[n] for n in SHARED_INPUTS}
    per_example = {n: given[n] for n in ['x']}
    grad_fn = _jax.value_and_grad(_loss, argnums=(0, 1))

    def one_microbatch(ex, loss_target):
        ex = dict(ex)
        diff = ex.pop(TWIN_DIFF_INPUT)
        return grad_fn(weights, diff, {**shared, **ex}, loss_target)

    if N_MICROBATCH == 1:
        loss, (grad_w, grad_x) = one_microbatch(per_example, given["loss_target"])
    else:
        def body(carry, xs):
            loss_sum, grad_sum = carry
            l_k, (gw_k, gx_k) = one_microbatch(xs[0], xs[1])
            with _jax.named_scope("update"):
                return (loss_sum + l_k, _jax.tree.map(_jnp.add, grad_sum, gw_k)), gx_k

        init = (_jnp.zeros((), _jnp.float32), _jax.tree.map(_jnp.zeros_like, weights))
        (loss, grad_w), grad_x = _jax.lax.scan(body, init, (per_example, given["loss_target"]))
    with _jax.named_scope("update"):
        delta_w, new_m, new_v = {}, {}, {}
        for n in TWIN_WEIGHTS:
            delta_w[n], new_m[n], new_v[n] = _adamw(weights[n], grad_w[n], given["m_" + n], given["v_" + n])
    return (loss, grad_x, *[grad_w[n] for n in TWIN_WEIGHTS], *[delta_w[n] for n in TWIN_WEIGHTS],
            *[new_m[n] for n in TWIN_WEIGHTS], *[new_v[n] for n in TWIN_WEIGHTS])
```

```python
import functools

import jax
import jax.numpy as jnp
from jax import lax
from jax.experimental import pallas as pl
from jax.experimental.pallas import tpu as pltpu

F32 = jnp.float32
BF16 = jnp.bfloat16
MESH = pl.DeviceIdType.MESH

NORM_EPS = 1e-6
LN_EPS = 1e-5
HEAD_DIM = 128
HEADS_PER_GROUP = 4
GROUP_WIDTH = HEADS_PER_GROUP * HEAD_DIM
DILATIONS = (1, 4, 16)
N_GROUPS = len(DILATIONS)
ATT_BLOCK = 128
ROPE_DIM = HEAD_DIM // 4
ROPE_THETA = 500000.0
SG_CHUNK = 128
SG_GROUPS = 12
SG_GROUP_DIM = 128
MASKED = -1e30

ADAM_LR = 0.001
ADAM_B1 = 0.9
ADAM_B2 = 0.999
ADAM_EPS = 1e-08
ADAM_WD = 0.01
ADAM_STEP = 10

N_CHIPS = 4
N_DEV = 8
LANES = 128
MIB = 2 ** 20
VMEM_BYTES_V7X = 64 * MIB

WEIGHT_NAMES = ['ffn1_norm', 'ffn1_w_gate', 'ffn1_w_up', 'ffn1_w_down', 'mix_norm', 'w_in', 'sg_ln_g', 'sg_ln_b',
                'sg_w', 'sg_b', 'w_att_out', 'w_sg_out', 'w_out', 'ffn2_norm', 'ffn2_w_gate', 'ffn2_w_up',
                'ffn2_w_down', 'final_norm']
BIG = [('ffn1_w_gate', 1), ('ffn1_w_up', 1), ('ffn1_w_down', 0), ('w_in', 1), ('w_att_out', 1), ('w_sg_out', 1),
       ('w_out', 0), ('ffn2_w_gate', 1), ('ffn2_w_up', 1), ('ffn2_w_down', 0)]
BIG_NAMES = [n for n, _ in BIG]
SMALL_NAMES = [n for n in WEIGHT_NAMES if n not in BIG_NAMES]


def _nbytes(shape, dtype):
    n = jnp.dtype(dtype).itemsize
    for s in shape:
        if s is not None:
            n *= s
    return n


def _cparams(sem, block_bytes, **kw):
    limit = int(min(max(3 * block_bytes, 32 * MIB), VMEM_BYTES_V7X - 8 * MIB))
    return pltpu.CompilerParams(dimension_semantics=sem, vmem_limit_bytes=limit, **kw)


def _tile(dim, cap):
    best = None
    for t in range(LANES, min(dim, cap) + 1, LANES):
        if dim % t == 0:
            best = t
    if best is None:
        assert dim <= cap, (dim, cap)
        return dim
    return best


def _rows(dim, cap):
    best = None
    for t in range(8, min(dim, cap) + 1, 8):
        if dim % t == 0:
            best = t
    assert best is not None, (dim, cap)
    return best


def _matmul(pairs, mode, out_dtype, name, *, scale=1.0, residual=None, b3=False, out3=0, caps=(1024, 1024, 512)):
    a0, b0 = pairs[0]
    if mode == 'nn':
        m, k = a0.shape
        n = b0.shape[0] * b0.shape[2] if b3 else b0.shape[1]
    elif mode == 'nt':
        m = a0.shape[0]
        n, k = (b0.shape[1], b0.shape[0] * b0.shape[2]) if b3 else b0.shape
    else:
        k, m = a0.shape
        n = b0.shape[1]
    tm = _tile(m, caps[0])
    tn = _tile(n, caps[1])
    tk = _tile(k, caps[2])
    if b3 and mode == 'nn':
        tn = b0.shape[2]
    if b3 and mode == 'nt':
        tk = b0.shape[2]
    if out3:
        tn = n // out3
    nk = k // tk
    if mode == 'tn':
        a_spec = pl.BlockSpec((tk, tm), lambda i, j, kk: (kk, i))
        b_spec = pl.BlockSpec((tk, tn), lambda i, j, kk: (kk, j))
        dims = ((0,), (0,))
    elif mode == 'nn':
        a_spec = pl.BlockSpec((tm, tk), lambda i, j, kk: (i, kk))
        b_spec = (pl.BlockSpec((None, tk, tn), lambda i, j, kk: (j, kk, 0)) if b3
                  else pl.BlockSpec((tk, tn), lambda i, j, kk: (kk, j)))
        dims = ((1,), (0,))
    else:
        a_spec = pl.BlockSpec((tm, tk), lambda i, j, kk: (i, kk))
        b_spec = (pl.BlockSpec((None, tn, tk), lambda i, j, kk: (kk, j, 0)) if b3
                  else pl.BlockSpec((tn, tk), lambda i, j, kk: (j, kk)))
        dims = ((1,), (1,))
    in_specs, operands = [], []
    for a, b in pairs:
        in_specs += [a_spec, b_spec]
        operands += [a, b]
    block_bytes = len(pairs) * (_nbytes((tm, tk), a0.dtype) + _nbytes((tk, tn), b0.dtype))
    if residual is not None:
        in_specs.append(pl.BlockSpec((tm, tn), lambda i, j, kk: (i, j)))
        operands.append(residual)
        block_bytes += _nbytes((tm, tn), F32)
    if out3:
        out_spec = pl.BlockSpec((None, tm, tn), lambda i, j, kk: (j, i, 0))
        out_shape = jax.ShapeDtypeStruct((out3, m, tn), out_dtype)
    else:
        out_spec = pl.BlockSpec((tm, tn), lambda i, j, kk: (i, j))
        out_shape = jax.ShapeDtypeStruct((m, n), out_dtype)
    block_bytes += _nbytes((tm, tn), out_dtype) + _nbytes((tm, tn), F32)
    n_pairs = len(pairs)
    has_res = residual is not None

    def body(*refs):
        o_ref, acc = refs[-2], refs[-1]
        kk = pl.program_id(2)

        @pl.when(kk == 0)
        def _():
            acc[...] = jnp.zeros_like(acc)

        part = None
        for p in range(n_pairs):
            d = lax.dot_general(refs[2 * p][...].astype(BF16), refs[2 * p + 1][...].astype(BF16),
                                (dims, ((), ())), preferred_element_type=F32)
            part = d if part is None else part + d
        acc[...] += part

        @pl.when(kk == nk - 1)
        def _():
            r = acc[...]
            if scale != 1.0:
                r = r * scale
            if has_res:
                r = refs[2 * n_pairs][...] + r
            o_ref[...] = r.astype(out_dtype)

    return pl.pallas_call(
        body, out_shape=out_shape, grid=(m // tm, n // tn, nk), in_specs=in_specs, out_specs=out_spec,
        scratch_shapes=[pltpu.VMEM((tm, tn), F32)],
        compiler_params=_cparams(("parallel", "parallel", "arbitrary"), block_bytes), name=name)(*operands)


def _rmsnorm_fwd(x, g, name):
    t, d = x.shape
    tm = _rows(t, 512)

    def body(x_ref, g_ref, o_ref):
        xv = x_ref[...]
        r = lax.rsqrt(jnp.mean(xv * xv, axis=1, keepdims=True) + NORM_EPS)
        o_ref[...] = (xv * r * g_ref[...]).astype(BF16)

    row = pl.BlockSpec((tm, d), lambda i: (i, 0))
    return pl.pallas_call(
        body, out_shape=jax.ShapeDtypeStruct((t, d), BF16), grid=(t // tm,),
        in_specs=[row, pl.BlockSpec((1, d), lambda i: (0, 0))], out_specs=row,
        compiler_params=_cparams(("parallel",), 2 * _nbytes((tm, d), F32)), name=name)(x, g)


def _rms_grad(xv, g, dn, d):
    r = lax.rsqrt(jnp.mean(xv * xv, axis=1, keepdims=True) + NORM_EPS)
    u = dn * g
    s = jnp.sum(xv * u, axis=1, keepdims=True)
    dx = r * u - xv * (r * r * r) * (s * (1.0 / d))
    return dx, dn * xv * r


def _rmsnorm_bwd(x, g, dn, dres, name):
    t, d = x.shape
    tm = _rows(t, 256)

    def body(x_ref, g_ref, dn_ref, dres_ref, dx_ref, dg_ref):
        dx, dg_rows = _rms_grad(x_ref[...], g_ref[...], dn_ref[...].astype(F32), d)
        dx_ref[...] = dres_ref[...] + dx

        @pl.when(pl.program_id(0) == 0)
        def _():
            dg_ref[...] = jnp.zeros_like(dg_ref)

        dg_ref[...] += jnp.sum(dg_rows, axis=0, keepdims=True)

    row = pl.BlockSpec((tm, d), lambda i: (i, 0))
    vec = pl.BlockSpec((1, d), lambda i: (0, 0))
    return pl.pallas_call(
        body, out_shape=(jax.ShapeDtypeStruct((t, d), F32), jax.ShapeDtypeStruct((1, d), F32)), grid=(t // tm,),
        in_specs=[row, vec, row, row], out_specs=(row, vec),
        compiler_params=_cparams(("arbitrary",), 4 * _nbytes((tm, d), F32)), name=name)(x, g, dn, dres)


def _final_loss(x, g, target, name):
    t, d = x.shape
    tm = _rows(t, 256)

    def body(x_ref, g_ref, t_ref, loss_ref, dx_ref, dg_ref):
        xv, gv = x_ref[...], g_ref[...]
        r = lax.rsqrt(jnp.mean(xv * xv, axis=1, keepdims=True) + NORM_EPS)
        err = xv * r * gv - t_ref[...]
        dx, dg_rows = _rms_grad(xv, gv, err * (1.0 / d), d)
        dx_ref[...] = dx

        @pl.when(pl.program_id(0) == 0)
        def _():
            dg_ref[...] = jnp.zeros_like(dg_ref)
            loss_ref[...] = jnp.zeros_like(loss_ref)

        dg_ref[...] += jnp.sum(dg_rows, axis=0, keepdims=True)
        row_loss = jnp.sum(err * err, axis=1, keepdims=True) * (0.5 / d)
        loss_ref[...] += jnp.sum(row_loss, axis=0, keepdims=True)

    row = pl.BlockSpec((tm, d), lambda i: (i, 0))
    vec = pl.BlockSpec((1, d), lambda i: (0, 0))
    return pl.pallas_call(
        body, out_shape=(jax.ShapeDtypeStruct((1, 1), F32), jax.ShapeDtypeStruct((t, d), F32),
                         jax.ShapeDtypeStruct((1, d), F32)),
        grid=(t // tm,), in_specs=[row, vec, row], out_specs=(pl.BlockSpec((1, 1), lambda i: (0, 0)), row, vec),
        compiler_params=_cparams(("arbitrary",), 3 * _nbytes((tm, d), F32)), name=name)(x, g, target)


def _ffn_up(n, wg, wu, name):
    t, d = n.shape
    s, _, f = wg.shape
    tm, tk = _tile(t, 512), _tile(d, 512)
    nk = d // tk

    def body(n_ref, wg_ref, wu_ref, a_ref, b_ref, h_ref, acc_g, acc_u):
        kk = pl.program_id(2)

        @pl.when(kk == 0)
        def _():
            acc_g[...] = jnp.zeros_like(acc_g)
            acc_u[...] = jnp.zeros_like(acc_u)

        nv = n_ref[...]
        acc_g[...] += jnp.dot(nv, wg_ref[...], preferred_element_type=F32)
        acc_u[...] += jnp.dot(nv, wu_ref[...], preferred_element_type=F32)

        @pl.when(kk == nk - 1)
        def _():
            a, b = acc_g[...], acc_u[...]
            a_ref[...] = a.astype(BF16)
            b_ref[...] = b.astype(BF16)
            h_ref[...] = (a * jax.nn.sigmoid(a) * b).astype(BF16)

    w_spec = pl.BlockSpec((None, tk, f), lambda i, j, kk: (j, kk, 0))
    o_spec = pl.BlockSpec((tm, f), lambda i, j, kk: (i, j))
    out = jax.ShapeDtypeStruct((t, s * f), BF16)
    block_bytes = _nbytes((tm, tk), BF16) + 2 * _nbytes((tk, f), BF16) + 3 * _nbytes((tm, f), BF16) + 2 * _nbytes((tm, f), F32)
    return pl.pallas_call(
        body, out_shape=(out, out, out), grid=(t // tm, s, nk),
        in_specs=[pl.BlockSpec((tm, tk), lambda i, j, kk: (i, kk)), w_spec, w_spec], out_specs=(o_spec, o_spec, o_spec),
        scratch_shapes=[pltpu.VMEM((tm, f), F32), pltpu.VMEM((tm, f), F32)],
        compiler_params=_cparams(("parallel", "parallel", "arbitrary"), block_bytes), name=name)(n, wg, wu)


def _ffn_bwd_act(dx, wd, a, b, name):
    t, d = dx.shape
    f = wd.shape[0]
    tm, tn, tk = _tile(t, 512), _tile(f, 1536), _tile(d, 512)
    nk = d // tk

    def body(dx_ref, wd_ref, a_ref, b_ref, da_ref, db_ref, acc):
        kk = pl.program_id(2)

        @pl.when(kk == 0)
        def _():
            acc[...] = jnp.zeros_like(acc)

        acc[...] += lax.dot_general(dx_ref[...].astype(BF16), wd_ref[...], ((((1,), (1,))), ((), ())),
                                    preferred_element_type=F32)

        @pl.when(kk == nk - 1)
        def _():
            dh = 0.5 * acc[...]
            av, bv = a_ref[...].astype(F32), b_ref[...].astype(F32)
            sg = jax.nn.sigmoid(av)
            da_ref[...] = (dh * bv * (sg * (1.0 + av * (1.0 - sg)))).astype(BF16)
            db_ref[...] = (dh * (av * sg)).astype(BF16)

    act = pl.BlockSpec((tm, tn), lambda i, j, kk: (i, j))
    out = jax.ShapeDtypeStruct((t, f), BF16)
    block_bytes = _nbytes((tm, tk), F32) + _nbytes((tn, tk), BF16) + 4 * _nbytes((tm, tn), BF16) + _nbytes((tm, tn), F32)
    return pl.pallas_call(
        body, out_shape=(out, out), grid=(t // tm, f // tn, nk),
        in_specs=[pl.BlockSpec((tm, tk), lambda i, j, kk: (i, kk)), pl.BlockSpec((tn, tk), lambda i, j, kk: (j, kk)),
                  act, act],
        out_specs=(act, act), scratch_shapes=[pltpu.VMEM((tm, tn), F32)],
        compiler_params=_cparams(("parallel", "parallel", "arbitrary"), block_bytes), name=name)(dx, wd, a, b)


def _ffn_forward(x, gain, wg, wu, wd, tag):
    n = _rmsnorm_fwd(x, gain, f"{tag}_norm")
    a, b, h = _ffn_up(n, wg, wu, f"{tag}_up")
    x_next = _matmul([(h, wd)], 'nn', F32, f"{tag}_down", scale=0.5, residual=x)
    return x_next, (n, a, b, h)


def _ffn_backward(x, gain, wg, wu, wd, saved, dx_next, tag):
    n, a, b, h = saved
    s = wg.shape[0]
    da, db = _ffn_bwd_act(dx_next, wd, a, b, f"{tag}_bwd_act")
    g_wd = _matmul([(h, dx_next)], 'tn', BF16, f"{tag}_dwd", scale=0.5)
    g_wg = _matmul([(n, da)], 'tn', BF16, f"{tag}_dwg", out3=s)
    g_wu = _matmul([(n, db)], 'tn', BF16, f"{tag}_dwu", out3=s)
    dn = _matmul([(da, wg), (db, wu)], 'nt', F32, f"{tag}_dn", b3=True)
    dx, g_gain = _rmsnorm_bwd(x, gain, dn, dx_next, f"{tag}_norm_bwd")
    return dx, g_gain, g_wg, g_wu, g_wd


def _rope_tables(seq):
    half = ROPE_DIM // 2
    inv_freq = ROPE_THETA ** (-jnp.arange(0, ROPE_DIM, 2, dtype=F32) / ROPE_DIM)
    ang = jnp.arange(seq).astype(F32)[:, None] * inv_freq[None, :]
    cos, sin = jnp.cos(ang), jnp.sin(ang)
    zeros = lambda w: jnp.zeros((seq, w), F32)
    c = jnp.concatenate([cos, cos, jnp.ones((seq, HEAD_DIM - ROPE_DIM), F32)], axis=1)
    s_up = jnp.concatenate([-sin, zeros(HEAD_DIM - half)], axis=1)
    s_dn = jnp.concatenate([zeros(half), sin, zeros(HEAD_DIM - ROPE_DIM)], axis=1)
    return c, s_up, s_dn


def _rope(x, width, tables, name):
    t = x.shape[0]
    tm = _rows(t, 512)
    half = ROPE_DIM // 2
    c, s_up, s_dn = tables

    def body(x_ref, c_ref, up_ref, dn_ref, o_ref):
        cv, uv, dv = c_ref[...], up_ref[...], dn_ref[...]
        for h in range(GROUP_WIDTH // HEAD_DIM):
            sl = slice(h * HEAD_DIM, (h + 1) * HEAD_DIM)
            xv = x_ref[:, sl].astype(F32)
            o_ref[:, sl] = (xv * cv + pltpu.roll(xv, HEAD_DIM - half, 1) * uv + pltpu.roll(xv, half, 1) * dv).astype(BF16)

    blk = pl.BlockSpec((tm, GROUP_WIDTH), lambda i, j: (i, j))
    tab = pl.BlockSpec((tm, HEAD_DIM), lambda i, j: (i, 0))
    return pl.pallas_call(
        body, out_shape=jax.ShapeDtypeStruct((t, width), BF16), grid=(t // tm, width // GROUP_WIDTH),
        in_specs=[blk, tab, tab, tab], out_specs=blk,
        compiler_params=_cparams(("parallel", "parallel"), 2 * _nbytes((tm, GROUP_WIDTH), F32)), name=name)(x, c, s_up, s_dn)


def _att_masks():
    qi = lax.broadcasted_iota(jnp.int32, (ATT_BLOCK, ATT_BLOCK), 0)
    kj = lax.broadcasted_iota(jnp.int32, (ATT_BLOCK, ATT_BLOCK), 1)
    return kj >= qi, kj <= qi


def _scores(q, k):
    return lax.dot_general(q, k, (((1,), (1,)), ((), ())), preferred_element_type=F32) * (HEAD_DIM ** -0.5)


def _att_fwd(q, k, v, offs, dil, name):
    qo, ko, vo = offs
    length = q.shape[0]
    nb = length // ATT_BLOCK

    def body(q_ref, kp_ref, kc_ref, vp_ref, vc_ref, o_ref, lse_ref):
        has_prev = pl.program_id(1) > 0
        m_prev, m_cur = _att_masks()
        m_prev = jnp.logical_and(m_prev, has_prev)
        for h in range(HEADS_PER_GROUP):
            sl = slice(h * HEAD_DIM, (h + 1) * HEAD_DIM)
            qv = q_ref[:, sl]
            s_p = jnp.where(m_prev, _scores(qv, kp_ref[:, sl]), MASKED)
            s_c = jnp.where(m_cur, _scores(qv, kc_ref[:, sl]), MASKED)
            m = jnp.maximum(jnp.max(s_p, axis=1, keepdims=True), jnp.max(s_c, axis=1, keepdims=True))
            p_p, p_c = jnp.exp(s_p - m), jnp.exp(s_c - m)
            l = jnp.sum(p_p, axis=1, keepdims=True) + jnp.sum(p_c, axis=1, keepdims=True)
            acc = jnp.dot(p_p.astype(BF16), vp_ref[:, sl], preferred_element_type=F32)
            acc += jnp.dot(p_c.astype(BF16), vc_ref[:, sl], preferred_element_type=F32)
            o_ref[:, sl] = acc / l
            lse_ref[:, sl] = jnp.broadcast_to(m + jnp.log(l), (ATT_BLOCK, HEAD_DIM))

    def spec(off, prev):
        if prev:
            return pl.BlockSpec((ATT_BLOCK, GROUP_WIDTH), lambda r, n: (jnp.maximum(n - 1, 0), off + r))
        return pl.BlockSpec((ATT_BLOCK, GROUP_WIDTH), lambda r, n: (n, off + r))

    out = jax.ShapeDtypeStruct((length, dil * GROUP_WIDTH), F32)
    o_spec = pl.BlockSpec((ATT_BLOCK, GROUP_WIDTH), lambda r, n: (n, r))
    return pl.pallas_call(
        body, out_shape=(out, out), grid=(dil, nb),
        in_specs=[spec(qo, False), spec(ko, True), spec(ko, False), spec(vo, True), spec(vo, False)],
        out_specs=(o_spec, o_spec),
        compiler_params=_cparams(("parallel", "parallel"), 8 * _nbytes((ATT_BLOCK, GROUP_WIDTH), F32)), name=name)(q, k, k, v, v)


def _att_combine(outs, lses, name):
    t = outs[0].shape[0]
    tm = _rows(t, 512)

    def body(*refs):
        o_refs, l_refs = refs[:N_GROUPS], refs[N_GROUPS:2 * N_GROUPS]
        ob_ref, of_ref, lse_ref = refs[2 * N_GROUPS:]
        ls = [r[...] for r in l_refs]
        m = functools.reduce(jnp.maximum, ls)
        ws = [jnp.exp(l - m) for l in ls]
        den = functools.reduce(jnp.add, ws)
        num = functools.reduce(jnp.add, [w * r[...] for w, r in zip(ws, o_refs)])
        o = num / den
        ob_ref[...] = o.astype(BF16)
        of_ref[...] = o
        lse_ref[...] = m + jnp.log(den)

    blk = pl.BlockSpec((tm, GROUP_WIDTH), lambda i: (i, 0))
    f32 = jax.ShapeDtypeStruct((t, GROUP_WIDTH), F32)
    return pl.pallas_call(
        body, out_shape=(jax.ShapeDtypeStruct((t, GROUP_WIDTH), BF16), f32, f32), grid=(t // tm,),
        in_specs=[blk] * (2 * N_GROUPS), out_specs=(blk, blk, blk),
        compiler_params=_cparams(("parallel",), 9 * _nbytes((tm, GROUP_WIDTH), F32)), name=name)(*outs, *lses)


def _att_delta(do, o, name):
    t = o.shape[0]
    tm = _rows(t, 512)

    def body(do_ref, o_ref, d_ref):
        for h in range(HEADS_PER_GROUP):
            sl = slice(h * HEAD_DIM, (h + 1) * HEAD_DIM)
            s = jnp.sum(do_ref[:, sl] * o_ref[:, sl], axis=1, keepdims=True)
            d_ref[:, sl] = jnp.broadcast_to(s, (tm, HEAD_DIM))

    blk = pl.BlockSpec((tm, GROUP_WIDTH), lambda i: (i, 0))
    return pl.pallas_call(
        body, out_shape=jax.ShapeDtypeStruct((t, GROUP_WIDTH), F32), grid=(t // tm,), in_specs=[blk, blk], out_specs=blk,
        compiler_params=_cparams(("parallel",), 3 * _nbytes((tm, GROUP_WIDTH), F32)), name=name)(do, o)


def _att_bwd_dq(q, k, v, do, lse, delta, offs, dil, name):
    qo, ko, vo = offs
    length = q.shape[0]
    nb = length // ATT_BLOCK
    scale = HEAD_DIM ** -0.5

    def body(q_ref, kp_ref, kc_ref, vp_ref, vc_ref, do_ref, lse_ref, dl_ref, dq_ref):
        has_prev = pl.program_id(1) > 0
        m_prev, m_cur = _att_masks()
        m_prev = jnp.logical_and(m_prev, has_prev)
        for h in range(HEADS_PER_GROUP):
            sl = slice(h * HEAD_DIM, (h + 1) * HEAD_DIM)
            qv, dov = q_ref[:, sl], do_ref[:, sl].astype(BF16)
            lsev, dlv = lse_ref[:, sl], dl_ref[:, sl]
            dq = None
            for mask, k_ref, v_ref in ((m_prev, kp_ref, vp_ref), (m_cur, kc_ref, vc_ref)):
                kv = k_ref[:, sl]
                p = jnp.exp(jnp.where(mask, _scores(qv, kv), MASKED) - lsev)
                dp = lax.dot_general(dov, v_ref[:, sl], (((1,), (1,)), ((), ())), preferred_element_type=F32)
                ds = (p * (dp - dlv) * scale).astype(BF16)
                part = jnp.dot(ds, kv, preferred_element_type=F32)
                dq = part if dq is None else dq + part
            dq_ref[:, sl] = dq.astype(BF16)

    def spec(off, prev):
        if prev:
            return pl.BlockSpec((ATT_BLOCK, GROUP_WIDTH), lambda r, n: (jnp.maximum(n - 1, 0), off + r))
        return pl.BlockSpec((ATT_BLOCK, GROUP_WIDTH), lambda r, n: (n, off + r))

    own = pl.BlockSpec((ATT_BLOCK, GROUP_WIDTH), lambda r, n: (n, r))
    return pl.pallas_call(
        body, out_shape=jax.ShapeDtypeStruct((length, dil * GROUP_WIDTH), BF16), grid=(dil, nb),
        in_specs=[spec(qo, False), spec(ko, True), spec(ko, False), spec(vo, True), spec(vo, False), own, own, own],
        out_specs=own,
        compiler_params=_cparams(("parallel", "parallel"), 10 * _nbytes((ATT_BLOCK, GROUP_WIDTH), F32)),
        name=name)(q, k, k, v, v, do, lse, delta)


def _att_bwd_dkv(q, k, v, do, lse, delta, offs, dil, name):
    qo, ko, vo = offs
    length = q.shape[0]
    nb = length // ATT_BLOCK
    scale = HEAD_DIM ** -0.5

    def body(k_ref, v_ref, qc_ref, qn_ref, doc_ref, don_ref, lsec_ref, lsen_ref, dlc_ref, dln_ref, dk_ref, dv_ref):
        has_next = pl.program_id(1) < nb - 1
        m_prev, m_cur = _att_masks()
        m_prev = jnp.logical_and(m_prev, has_next)
        for h in range(HEADS_PER_GROUP):
            sl = slice(h * HEAD_DIM, (h + 1) * HEAD_DIM)
            kv, vv = k_ref[:, sl], v_ref[:, sl]
            dk = dv = None
            for mask, q_ref, do_ref, lse_ref, dl_ref in ((m_cur, qc_ref, doc_ref, lsec_ref, dlc_ref),
                                                         (m_prev, qn_ref, don_ref, lsen_ref, dln_ref)):
                qv, dov = q_ref[:, sl], do_ref[:, sl].astype(BF16)
                p = jnp.exp(jnp.where(mask, _scores(qv, kv), MASKED) - lse_ref[:, sl])
                dp = lax.dot_general(dov, vv, (((1,), (1,)), ((), ())), preferred_element_type=F32)
                ds = (p * (dp - dl_ref[:, sl]) * scale).astype(BF16)
                dv_part = lax.dot_general(p.astype(BF16), dov, (((0,), (0,)), ((), ())), preferred_element_type=F32)
                dk_part = lax.dot_general(ds, qv, (((0,), (0,)), ((), ())), preferred_element_type=F32)
                dv = dv_part if dv is None else dv + dv_part
                dk = dk_part if dk is None else dk + dk_part
            dk_ref[:, sl] = dk.astype(BF16)
            dv_ref[:, sl] = dv.astype(BF16)

    def spec(off, nxt):
        if nxt:
            return pl.BlockSpec((ATT_BLOCK, GROUP_WIDTH), lambda r, n: (jnp.minimum(n + 1, nb - 1), off + r))
        return pl.BlockSpec((ATT_BLOCK, GROUP_WIDTH), lambda r, n: (n, off + r))

    own = pl.BlockSpec((ATT_BLOCK, GROUP_WIDTH), lambda r, n: (n, r))
    out = jax.ShapeDtypeStruct((length, dil * GROUP_WIDTH), BF16)
    return pl.pallas_call(
        body, out_shape=(out, out), grid=(dil, nb),
        in_specs=[spec(ko, False), spec(vo, False), spec(qo, False), spec(qo, True), spec(0, False), spec(0, True),
                  spec(0, False), spec(0, True), spec(0, False), spec(0, True)],
        out_specs=(own, own),
        compiler_params=_cparams(("parallel", "parallel"), 12 * _nbytes((ATT_BLOCK, GROUP_WIDTH), F32)),
        name=name)(k, v, q, q, do, do, lse, lse, delta, delta)


def _gelu(x):
    return 0.5 * x * (1.0 + lax.erf(x * (2.0 ** -0.5)))


def _gelu_grad(x):
    return 0.5 * (1.0 + lax.erf(x * (2.0 ** -0.5))) + x * jnp.exp(-0.5 * x * x) * ((2.0 * jnp.pi) ** -0.5)


def _sg_normed(vs, lg, lb):
    gv = _gelu(vs)
    mu = jnp.mean(gv, axis=1, keepdims=True)
    xc = gv - mu
    rstd = lax.rsqrt(jnp.mean(xc * xc, axis=1, keepdims=True) + LN_EPS)
    z = xc * rstd
    return z, rstd, z * lg + lb


def _sg_tril():
    row = lax.broadcasted_iota(jnp.int32, (SG_CHUNK, SG_CHUNK), 0)
    col = lax.broadcasted_iota(jnp.int32, (SG_CHUNK, SG_CHUNK), 1)
    return row >= col


def _sg_fwd(proj, u_blk, vs_blk, lg, lb, sg_w, bias, name):
    t = proj.shape[0]
    width = SG_GROUPS * SG_GROUP_DIM

    def body(u_ref, vs_ref, lg_ref, lb_ref, w_ref, bias_ref, o_ref):
        _, _, vn = _sg_normed(vs_ref[...].astype(F32), lg_ref[...], lb_ref[...])
        vn = vn.astype(BF16)
        tril = _sg_tril()
        for g in range(SG_GROUPS):
            sl = slice(g * SG_GROUP_DIM, (g + 1) * SG_GROUP_DIM)
            w = jnp.where(tril, w_ref[g], 0.0).astype(BF16)
            sp = jnp.dot(w, vn[:, sl], preferred_element_type=F32) + bias_ref[:, sl]
            o_ref[:, sl] = (_gelu(u_ref[:, sl].astype(F32)) * sp).astype(BF16)

    vec = pl.BlockSpec((1, width), lambda i: (0, 0))
    return pl.pallas_call(
        body, out_shape=jax.ShapeDtypeStruct((t, width), BF16), grid=(t // SG_CHUNK,),
        in_specs=[pl.BlockSpec((SG_CHUNK, width), lambda i: (i, u_blk)), pl.BlockSpec((SG_CHUNK, width), lambda i: (i, vs_blk)),
                  vec, vec, pl.BlockSpec((SG_GROUPS, SG_CHUNK, SG_CHUNK), lambda i: (0, 0, 0)),
                  pl.BlockSpec((SG_CHUNK, width), lambda i: (0, 0))],
        out_specs=pl.BlockSpec((SG_CHUNK, width), lambda i: (i, 0)),
        compiler_params=_cparams(("parallel",), 8 * _nbytes((SG_CHUNK, width), F32)), name=name)(proj, proj, lg, lb, sg_w, bias)


def _sg_bwd(proj, u_blk, vs_blk, dsu, lg, lb, sg_w, bias, name):
    t = proj.shape[0]
    width = SG_GROUPS * SG_GROUP_DIM

    def body(u_ref, vs_ref, dsu_ref, lg_ref, lb_ref, w_ref, bias_ref, du_ref, dvs_ref, dw_ref, dbias_ref, dlg_ref, dlb_ref):
        @pl.when(pl.program_id(0) == 0)
        def _():
            dw_ref[...] = jnp.zeros_like(dw_ref)
            dbias_ref[...] = jnp.zeros_like(dbias_ref)
            dlg_ref[...] = jnp.zeros_like(dlg_ref)
            dlb_ref[...] = jnp.zeros_like(dlb_ref)

        vs = vs_ref[...].astype(F32)
        z, rstd, vn = _sg_normed(vs, lg_ref[...], lb_ref[...])
        vn = vn.astype(BF16)
        tril = _sg_tril()
        dvn = []
        for g in range(SG_GROUPS):
            sl = slice(g * SG_GROUP_DIM, (g + 1) * SG_GROUP_DIM)
            w = jnp.where(tril, w_ref[g], 0.0).astype(BF16)
            vg = vn[:, sl]
            sp = jnp.dot(w, vg, preferred_element_type=F32) + bias_ref[:, sl]
            uv = u_ref[:, sl].astype(F32)
            dsu_g = dsu_ref[:, sl].astype(F32)
            du_ref[:, sl] = (dsu_g * sp * _gelu_grad(uv)).astype(BF16)
            dsp = dsu_g * _gelu(uv)
            dsp_b = dsp.astype(BF16)
            dw = lax.dot_general(dsp_b, vg, (((1,), (1,)), ((), ())), preferred_element_type=F32)
            dw_ref[g] += jnp.where(tril, dw, 0.0)
            dbias_ref[:, sl] += jnp.broadcast_to(jnp.sum(dsp, axis=1, keepdims=True), (SG_CHUNK, SG_GROUP_DIM))
            dvn.append(lax.dot_general(w, dsp_b, (((0,), (0,)), ((), ())), preferred_element_type=F32))
        dvn = jnp.concatenate(dvn, axis=1)
        dlg_ref[...] += jnp.sum(dvn * z, axis=0, keepdims=True)
        dlb_ref[...] += jnp.sum(dvn, axis=0, keepdims=True)
        dz = dvn * lg_ref[...]
        dgv = rstd * (dz - jnp.mean(dz, axis=1, keepdims=True) - z * jnp.mean(dz * z, axis=1, keepdims=True))
        dvs_ref[...] = (dgv * _gelu_grad(vs)).astype(BF16)

    vec = pl.BlockSpec((1, width), lambda i: (0, 0))
    row = pl.BlockSpec((SG_CHUNK, width), lambda i: (i, 0))
    fixed = pl.BlockSpec((SG_CHUNK, width), lambda i: (0, 0))
    w_spec = pl.BlockSpec((SG_GROUPS, SG_CHUNK, SG_CHUNK), lambda i: (0, 0, 0))
    act = jax.ShapeDtypeStruct((t, width), BF16)
    return pl.pallas_call(
        body,
        out_shape=(act, act, jax.ShapeDtypeStruct((SG_GROUPS, SG_CHUNK, SG_CHUNK), F32),
                   jax.ShapeDtypeStruct((SG_CHUNK, width), F32), jax.ShapeDtypeStruct((1, width), F32),
                   jax.ShapeDtypeStruct((1, width), F32)),
        grid=(t // SG_CHUNK,),
        in_specs=[pl.BlockSpec((SG_CHUNK, width), lambda i: (i, u_blk)), pl.BlockSpec((SG_CHUNK, width), lambda i: (i, vs_blk)),
                  row, vec, vec, w_spec, fixed],
        out_specs=(row, row, w_spec, fixed, vec, vec),
        compiler_params=_cparams(("arbitrary",), 14 * _nbytes((SG_CHUNK, width), F32)),
        name=name)(proj, proj, dsu, lg, lb, sg_w, bias)


def _gate_fwd(proj, ga_blk, gs_blk, y_att, y_sg, name):
    t, d = y_att.shape
    tm, tn = _rows(t, 512), _tile(d, GROUP_WIDTH)

    def body(ga_ref, gs_ref, ya_ref, ys_ref, o_ref):
        o_ref[...] = (jax.nn.sigmoid(ga_ref[...].astype(F32)) * ya_ref[...].astype(F32)
                      + jax.nn.sigmoid(gs_ref[...].astype(F32)) * ys_ref[...].astype(F32)).astype(BF16)

    own = pl.BlockSpec((tm, tn), lambda i, j: (i, j))
    return pl.pallas_call(
        body, out_shape=jax.ShapeDtypeStruct((t, d), BF16), grid=(t // tm, d // tn),
        in_specs=[pl.BlockSpec((tm, tn), lambda i, j: (i, ga_blk + j)), pl.BlockSpec((tm, tn), lambda i, j: (i, gs_blk + j)),
                  own, own],
        out_specs=own, compiler_params=_cparams(("parallel", "parallel"), 6 * _nbytes((tm, tn), F32)),
        name=name)(proj, proj, y_att, y_sg)


def _gate_bwd(proj, ga_blk, gs_blk, y_att, y_sg, dmerged, name):
    t, d = y_att.shape
    tm, tn = _rows(t, 512), _tile(d, GROUP_WIDTH)

    def body(ga_ref, gs_ref, ya_ref, ys_ref, dm_ref, dya_ref, dys_ref, dga_ref, dgs_ref):
        dm = dm_ref[...].astype(F32)
        for g_ref, y_ref, dy_ref, dg_ref in ((ga_ref, ya_ref, dya_ref, dga_ref), (gs_ref, ys_ref, dys_ref, dgs_ref)):
            sg = jax.nn.sigmoid(g_ref[...].astype(F32))
            dy_ref[...] = (dm * sg).astype(BF16)
            dg_ref[...] = (dm * y_ref[...].astype(F32) * sg * (1.0 - sg)).astype(BF16)

    own = pl.BlockSpec((tm, tn), lambda i, j: (i, j))
    out = jax.ShapeDtypeStruct((t, d), BF16)
    return pl.pallas_call(
        body, out_shape=(out, out, out, out), grid=(t // tm, d // tn),
        in_specs=[pl.BlockSpec((tm, tn), lambda i, j: (i, ga_blk + j)), pl.BlockSpec((tm, tn), lambda i, j: (i, gs_blk + j)),
                  own, own, own],
        out_specs=(own, own, own, own), compiler_params=_cparams(("parallel", "parallel"), 10 * _nbytes((tm, tn), F32)),
        name=name)(proj, proj, y_att, y_sg, dmerged)


def _group_view(arr, col, dil):
    t = arr.shape[0]
    return arr[:, col:col + GROUP_WIDTH].reshape(t // dil, dil * GROUP_WIDTH)


def _mixer_forward(x, w, small):
    t, d = x.shape
    att_w = N_GROUPS * GROUP_WIDTH
    sg_w = SG_GROUPS * SG_GROUP_DIM
    n = _rmsnorm_fwd(x, small['mix_norm'], "mix_norm")
    proj = _matmul([(n, w['w_in'])], 'nn', BF16, "mix_in", b3=True, caps=(512, 1024, 512))
    tables = _rope_tables(t)
    qk = _rope(proj, 2 * att_w, tables, "mix_rope")
    outs, lses = [], []
    for gi, dil in enumerate(DILATIONS):
        if dil == 1:
            args = (qk, qk, proj, (gi, N_GROUPS + gi, 2 * N_GROUPS + gi))
        else:
            args = (_group_view(qk, gi * GROUP_WIDTH, dil), _group_view(qk, att_w + gi * GROUP_WIDTH, dil),
                    _group_view(proj, 2 * att_w + gi * GROUP_WIDTH, dil), (0, 0, 0))
        o, lse = _att_fwd(*args, dil, f"att_fwd{gi}")
        outs.append(o.reshape(t, GROUP_WIDTH))
        lses.append(lse.reshape(t, GROUP_WIDTH))
    o_b, o_f, lse = _att_combine(outs, lses, "att_combine")
    y_att = _matmul([(o_b, w['w_att_out'])], 'nn', BF16, "mix_att_out", b3=True)
    bias = jnp.repeat(small['sg_b'].T, SG_GROUP_DIM, axis=1)
    u_blk, vs_blk = 3 * att_w // sg_w, 3 * att_w // sg_w + 1
    su = _sg_fwd(proj, u_blk, vs_blk, small['sg_ln_g'], small['sg_ln_b'], small['sg_w'], bias, "sg_fwd")
    y_sg = _matmul([(su, w['w_sg_out'])], 'nn', BF16, "mix_sg_out", b3=True)
    ga_blk = (3 * att_w + 2 * sg_w) // _tile(d, GROUP_WIDTH)
    gs_blk = ga_blk + d // _tile(d, GROUP_WIDTH)
    merged = _gate_fwd(proj, ga_blk, gs_blk, y_att, y_sg, "gate_fwd")
    x_next = _matmul([(merged, w['w_out'])], 'nn', F32, "mix_out", residual=x)
    saved = (n, proj, qk, tables, o_b, o_f, lse, y_att, su, y_sg, merged, bias, (u_blk, vs_blk, ga_blk, gs_blk))
    return x_next, saved


def _mixer_backward(x, w, small, saved, dx_next):
    n, proj, qk, tables, o_b, o_f, lse, y_att, su, y_sg, merged, bias, (u_blk, vs_blk, ga_blk, gs_blk) = saved
    t, d = x.shape
    att_w = N_GROUPS * GROUP_WIDTH
    s = N_CHIPS
    g = {}
    dmerged = _matmul([(dx_next, w['w_out'])], 'nt', BF16, "mix_out_dx")
    g['w_out'] = _matmul([(merged, dx_next)], 'tn', BF16, "mix_out_dw")
    dy_att, dy_sg, dg_att, dg_sg = _gate_bwd(proj, ga_blk, gs_blk, y_att, y_sg, dmerged, "gate_bwd")

    g['w_att_out'] = _matmul([(o_b, dy_att)], 'tn', BF16, "mix_att_out_dw", out3=s)
    do = _matmul([(dy_att, w['w_att_out'])], 'nt', F32, "mix_att_out_dx", b3=True)
    delta = _att_delta(do, o_f, "att_delta")
    dqs, dks, dvs_ = [], [], []
    for gi, dil in enumerate(DILATIONS):
        if dil == 1:
            args = (qk, qk, proj, do, lse, delta, (gi, N_GROUPS + gi, 2 * N_GROUPS + gi))
        else:
            args = (_group_view(qk, gi * GROUP_WIDTH, dil), _group_view(qk, att_w + gi * GROUP_WIDTH, dil),
                    _group_view(proj, 2 * att_w + gi * GROUP_WIDTH, dil), _group_view(do, 0, dil),
                    _group_view(lse, 0, dil), _group_view(delta, 0, dil), (0, 0, 0))
        dq = _att_bwd_dq(*args, dil, f"att_bwd_dq{gi}")
        dk, dv = _att_bwd_dkv(*args, dil, f"att_bwd_dkv{gi}")
        dqs.append(dq.reshape(t, GROUP_WIDTH))
        dks.append(dk.reshape(t, GROUP_WIDTH))
        dvs_.append(dv.reshape(t, GROUP_WIDTH))
    c, s_up, s_dn = tables
    dqk = _rope(jnp.concatenate(dqs + dks, axis=1), 2 * att_w, (c, -s_up, -s_dn), "mix_rope_bwd")

    g['w_sg_out'] = _matmul([(su, dy_sg)], 'tn', BF16, "mix_sg_out_dw", out3=s)
    dsu = _matmul([(dy_sg, w['w_sg_out'])], 'nt', BF16, "mix_sg_out_dx", b3=True)
    du, dvs, g_sg_w, g_bias, g_lg, g_lb = _sg_bwd(proj, u_blk, vs_blk, dsu, small['sg_ln_g'], small['sg_ln_b'],
                                                   small['sg_w'], bias, "sg_bwd")
    gs = {'sg_w': g_sg_w, 'sg_b': g_bias[:, ::SG_GROUP_DIM].T, 'sg_ln_g': g_lg, 'sg_ln_b': g_lb}

    dproj = jnp.concatenate([dqk] + dvs_ + [du, dvs, dg_att, dg_sg], axis=1)
    g['w_in'] = _matmul([(n, dproj)], 'tn', BF16, "mix_in_dw", out3=s, caps=(512, 1024, 1024))
    dn = _matmul([(dproj, w['w_in'])], 'nt', F32, "mix_in_dx", b3=True, caps=(512, 512, 512))
    dx, gs['mix_norm'] = _rmsnorm_bwd(x, small['mix_norm'], dn, dx_next, "mix_norm_bwd")
    return dx, g, gs


def _local_step(x, target, w, small):
    x1, s1 = _ffn_forward(x, small['ffn1_norm'], w['ffn1_w_gate'], w['ffn1_w_up'], w['ffn1_w_down'], "ffn1")
    x2, s2 = _mixer_forward(x1, w, small)
    x3, s3 = _ffn_forward(x2, small['ffn2_norm'], w['ffn2_w_gate'], w['ffn2_w_up'], w['ffn2_w_down'], "ffn2")
    loss, dx3, g_final = _final_loss(x3, small['final_norm'], target, "final_loss")
    g, gs = {}, {'final_norm': g_final}
    dx2, gs['ffn2_norm'], g['ffn2_w_gate'], g['ffn2_w_up'], g['ffn2_w_down'] = _ffn_backward(
        x2, small['ffn2_norm'], w['ffn2_w_gate'], w['ffn2_w_up'], w['ffn2_w_down'], s3, dx3, "ffn2")
    dx1, g_mix, gs_mix = _mixer_backward(x1, w, small, s2, dx2)
    g.update(g_mix)
    gs.update(gs_mix)
    dx0, gs['ffn1_norm'], g['ffn1_w_gate'], g['ffn1_w_up'], g['ffn1_w_down'] = _ffn_backward(
        x, small['ffn1_norm'], w['ffn1_w_gate'], w['ffn1_w_up'], w['ffn1_w_down'], s1, dx1, "ffn1")
    return loss, dx0, g, gs


def _place():
    x, y, c = lax.axis_index("x"), lax.axis_index("y"), lax.axis_index("c")
    others = [(1 - x, y), (x, 1 - y), (1 - x, 1 - y)]
    return x, y, c, others


def _cast_bf16(wt, name):
    r, ccols = wt.shape
    tm = _rows(r, 256)
    blk = pl.BlockSpec((tm, ccols), lambda i: (i, 0))

    def body(w_ref, o_ref):
        o_ref[...] = w_ref[...].astype(BF16)

    return pl.pallas_call(body, out_shape=jax.ShapeDtypeStruct((r, ccols), BF16), grid=(r // tm,), in_specs=[blk],
                          out_specs=blk, compiler_params=_cparams(("parallel",), 2 * _nbytes((tm, ccols), F32)), name=name)(wt)


def _all_gather(shards):
    nw = len(shards)

    def body(*refs):
        src, dst = refs[:nw], refs[nw:2 * nw]
        local_sems, send_sems, recv_sems, fwd_send_sems, fwd_recv_sems = refs[2 * nw:]
        x, y, c, others = _place()
        me = 2 * x + y
        local, sends, fwds = [], [], []
        for i in range(nw):
            cp = pltpu.make_async_copy(src[i], dst[i].at[me], local_sems.at[i])
            cp.start()
            local.append(cp)
            for j, (ox, oy) in enumerate(others):
                cp = pltpu.make_async_remote_copy(src[i].at[c], dst[i].at[me, c], send_sems.at[i, j], recv_sems.at[i, j],
                                                  device_id=(ox, oy, c), device_id_type=MESH)
                cp.start()
                sends.append(cp)
        for i in range(nw):
            for j, (ox, oy) in enumerate(others):
                got = dst[i].at[2 * ox + oy, c]
                pltpu.make_async_remote_copy(got, got, send_sems.at[i, j], recv_sems.at[i, j],
                                             device_id=(ox, oy, c), device_id_type=MESH).wait_recv()
                cp = pltpu.make_async_remote_copy(got, got, fwd_send_sems.at[i, j], fwd_recv_sems.at[i, j],
                                                  device_id=(x, y, 1 - c), device_id_type=MESH)
                cp.start()
                fwds.append(cp)
        for i in range(nw):
            for j, (ox, oy) in enumerate(others):
                got = dst[i].at[2 * ox + oy, 1 - c]
                pltpu.make_async_remote_copy(got, got, fwd_send_sems.at[i, j], fwd_recv_sems.at[i, j],
                                             device_id=(x, y, 1 - c), device_id_type=MESH).wait_recv()
        for cp in sends + fwds:
            cp.wait_send()
        for cp in local:
            cp.wait()

    any_spec = pl.BlockSpec(memory_space=pl.ANY)
    return pl.pallas_call(
        body, out_shape=[jax.ShapeDtypeStruct((N_CHIPS,) + s.shape, s.dtype) for s in shards],
        in_specs=[any_spec] * nw, out_specs=[any_spec] * nw,
        scratch_shapes=[pltpu.SemaphoreType.DMA((nw,)), pltpu.SemaphoreType.DMA((nw, 3)), pltpu.SemaphoreType.DMA((nw, 3)),
                        pltpu.SemaphoreType.DMA((nw, 3)), pltpu.SemaphoreType.DMA((nw, 3))],
        compiler_params=pltpu.CompilerParams(has_side_effects=True), name="all_gather_weights")(*shards)


def _sibling_exchange(grads):
    nw = len(grads)

    def body(*refs):
        src, dst = refs[:nw], refs[nw:2 * nw]
        send_sems, recv_sems = refs[2 * nw:]
        x, y, c, _ = _place()
        cps = []
        for i in range(nw):
            cp = pltpu.make_async_remote_copy(src[i].at[:, 1 - c], dst[i], send_sems.at[i], recv_sems.at[i],
                                              device_id=(x, y, 1 - c), device_id_type=MESH)
            cp.start()
            cps.append(cp)
        for cp in cps:
            cp.wait()

    any_spec = pl.BlockSpec(memory_space=pl.ANY)
    return pl.pallas_call(
        body, out_shape=[jax.ShapeDtypeStruct((g.shape[0],) + g.shape[2:], g.dtype) for g in grads],
        in_specs=[any_spec] * nw, out_specs=[any_spec] * nw,
        scratch_shapes=[pltpu.SemaphoreType.DMA((nw,)), pltpu.SemaphoreType.DMA((nw,))],
        compiler_params=pltpu.CompilerParams(has_side_effects=True), name="rs_sibling")(*grads)


def _chip_exchange(parts):
    nw = len(parts)

    def body(*refs):
        src, dst = refs[:nw], refs[nw:2 * nw]
        send_sems, recv_sems = refs[2 * nw:]
        x, y, c, others = _place()
        cps = []
        for i in range(nw):
            for j, (ox, oy) in enumerate(others):
                cp = pltpu.make_async_remote_copy(src[i].at[2 * ox + oy], dst[i].at[j], send_sems.at[i, j], recv_sems.at[i, j],
                                                  device_id=(ox, oy, c), device_id_type=MESH)
                cp.start()
                cps.append(cp)
        for cp in cps:
            cp.wait()

    any_spec = pl.BlockSpec(memory_space=pl.ANY)
    return pl.pallas_call(
        body, out_shape=[jax.ShapeDtypeStruct((3,) + p.shape[1:], p.dtype) for p in parts],
        in_specs=[any_spec] * nw, out_specs=[any_spec] * nw,
        scratch_shapes=[pltpu.SemaphoreType.DMA((nw, 3)), pltpu.SemaphoreType.DMA((nw, 3))],
        compiler_params=pltpu.CompilerParams(has_side_effects=True), name="rs_chips")(*parts)


def _half_exchange(halves):
    nw = len(halves)

    def body(*refs):
        src, dst = refs[:nw], refs[nw:2 * nw]
        local_sems, send_sems, recv_sems = refs[2 * nw:]
        x, y, c, _ = _place()
        cps = []
        for i in range(nw):
            cp = pltpu.make_async_copy(src[i], dst[i].at[c], local_sems.at[i])
            cp.start()
            cps.append(cp)
            cp = pltpu.make_async_remote_copy(src[i], dst[i].at[c], send_sems.at[i], recv_sems.at[i],
                                              device_id=(x, y, 1 - c), device_id_type=MESH)
            cp.start()
            cps.append(cp)
        for cp in cps:
            cp.wait()

    any_spec = pl.BlockSpec(memory_space=pl.ANY)
    return pl.pallas_call(
        body, out_shape=[jax.ShapeDtypeStruct((2,) + h.shape, h.dtype) for h in halves],
        in_specs=[any_spec] * nw, out_specs=[any_spec] * nw,
        scratch_shapes=[pltpu.SemaphoreType.DMA((nw,)), pltpu.SemaphoreType.DMA((nw,)), pltpu.SemaphoreType.DMA((nw,))],
        compiler_params=pltpu.CompilerParams(has_side_effects=True), name="rs_halves")(*halves)


def _sibling_sum(grad, recv, c_idx, name):
    s, _, r, ccols = grad.shape
    tm = _rows(r, 256)

    def body(c_ref, g_ref, r_ref, o_ref):
        o_ref[...] = (g_ref[...].astype(F32) + r_ref[...].astype(F32)).astype(BF16)

    grid_spec = pltpu.PrefetchScalarGridSpec(
        num_scalar_prefetch=1, grid=(s, r // tm),
        in_specs=[pl.BlockSpec((None, None, tm, ccols), lambda q, i, cr: (q, cr[0], i, 0)),
                  pl.BlockSpec((None, tm, ccols), lambda q, i, cr: (q, i, 0))],
        out_specs=pl.BlockSpec((None, tm, ccols), lambda q, i, cr: (q, i, 0)))
    return pl.pallas_call(body, out_shape=jax.ShapeDtypeStruct((s, r, ccols), BF16), grid_spec=grid_spec,
                          compiler_params=_cparams(("parallel", "parallel"), 4 * _nbytes((tm, ccols), F32)), name=name)(c_idx, grad, recv)


def _chip_sum(part, recv, p_idx, name):
    _, r, ccols = part.shape
    tm = _rows(r, 256)

    def body(p_ref, own_ref, r0_ref, r1_ref, r2_ref, o_ref):
        acc = own_ref[...].astype(F32) + r0_ref[...].astype(F32)
        acc = acc + r1_ref[...].astype(F32)
        o_ref[...] = acc + r2_ref[...].astype(F32)

    def slot(j):
        return pl.BlockSpec((None, tm, ccols), lambda i, pr: (j, i, 0))

    grid_spec = pltpu.PrefetchScalarGridSpec(
        num_scalar_prefetch=1, grid=(r // tm,),
        in_specs=[pl.BlockSpec((None, tm, ccols), lambda i, pr: (pr[0], i, 0)), slot(0), slot(1), slot(2)],
        out_specs=pl.BlockSpec((tm, ccols), lambda i, pr: (i, 0)))
    return pl.pallas_call(body, out_shape=jax.ShapeDtypeStruct((r, ccols), F32), grid_spec=grid_spec,
                          compiler_params=_cparams(("parallel",), 6 * _nbytes((tm, ccols), F32)), name=name)(p_idx, part, recv, recv, recv)


def _all_reduce_small(vec):
    r = vec.shape[0]

    def body(v_ref, o_ref, slots, send_sems, recv_sems):
        x, y, c, _ = _place()
        me = 4 * x + 2 * y + c
        slots[me] = v_ref[...]
        cps = []
        for k in range(1, N_DEV):
            peer = (1 - x if k & 4 else x, 1 - y if k & 2 else y, 1 - c if k & 1 else c)
            cp = pltpu.make_async_remote_copy(slots.at[me], slots.at[me], send_sems.at[k - 1], recv_sems.at[k - 1],
                                              device_id=peer, device_id_type=MESH)
            cp.start()
            cps.append(cp)
        for cp in cps:
            cp.wait()
        acc = slots[0]
        for dev in range(1, N_DEV):
            acc = acc + slots[dev]
        o_ref[...] = acc

    vm = pl.BlockSpec(memory_space=pltpu.VMEM)
    return pl.pallas_call(
        body, out_shape=jax.ShapeDtypeStruct((r, LANES), F32), in_specs=[vm], out_specs=vm,
        scratch_shapes=[pltpu.VMEM((N_DEV, r, LANES), F32), pltpu.SemaphoreType.DMA((N_DEV - 1,)),
                        pltpu.SemaphoreType.DMA((N_DEV - 1,))],
        compiler_params=pltpu.CompilerParams(vmem_limit_bytes=int(4 * _nbytes((N_DEV, r, LANES), F32))),
        name="all_reduce_small")(vec)


def _adamw(wt, g, m, v, name):
    r, ccols = wt.shape
    tm = _rows(r, max(8, (MIB // (4 * ccols)) // 8 * 8))
    blk = pl.BlockSpec((tm, ccols), lambda i: (i, 0))

    def body(w_ref, g_ref, m_ref, v_ref, d_ref, mo_ref, vo_ref):
        gv = g_ref[...]
        mv = ADAM_B1 * m_ref[...] + (1.0 - ADAM_B1) * gv
        vv = ADAM_B2 * v_ref[...] + (1.0 - ADAM_B2) * (gv * gv)
        m_hat = mv / (1.0 - ADAM_B1 ** ADAM_STEP)
        v_hat = vv / (1.0 - ADAM_B2 ** ADAM_STEP)
        d_ref[...] = -ADAM_LR * (m_hat / (jnp.sqrt(v_hat) + ADAM_EPS) + ADAM_WD * w_ref[...])
        mo_ref[...] = mv
        vo_ref[...] = vv

    out = jax.ShapeDtypeStruct((r, ccols), F32)
    return pl.pallas_call(body, out_shape=(out, out, out), grid=(r // tm,), in_specs=[blk] * 4, out_specs=(blk, blk, blk),
                          compiler_params=_cparams(("parallel",), 7 * _nbytes((tm, ccols), F32)), name=name)(wt, g, m, v)


def _as_rows(a):
    rows = a.reshape(-1, LANES)
    return jnp.pad(rows, ((0, -rows.shape[0] % 8), (0, 0)))


def kernel(x, ffn1_norm, ffn1_w_gate, ffn1_w_up, ffn1_w_down, mix_norm, w_in, sg_ln_g, sg_ln_b, sg_w, sg_b, w_att_out, w_sg_out, w_out, ffn2_norm, ffn2_w_gate, ffn2_w_up, ffn2_w_down, final_norm, loss_target, m_ffn1_norm, m_ffn1_w_gate, m_ffn1_w_up, m_ffn1_w_down, m_mix_norm, m_w_in, m_sg_ln_g, m_sg_ln_b, m_sg_w, m_sg_b, m_w_att_out, m_w_sg_out, m_w_out, m_ffn2_norm, m_ffn2_w_gate, m_ffn2_w_up, m_ffn2_w_down, m_final_norm, v_ffn1_norm, v_ffn1_w_gate, v_ffn1_w_up, v_ffn1_w_down, v_mix_norm, v_w_in, v_sg_ln_g, v_sg_ln_b, v_sg_w, v_sg_b, v_w_att_out, v_w_sg_out, v_w_out, v_ffn2_norm, v_ffn2_w_gate, v_ffn2_w_up, v_ffn2_w_down, v_final_norm):
    given = dict(locals())
    wts = {n: given[n] for n in WEIGHT_NAMES}
    ms = {n: given["m_" + n] for n in WEIGHT_NAMES}
    vs = {n: given["v_" + n] for n in WEIGHT_NAMES}
    t, d = x.shape[-2], x.shape[-1]
    xc, yc, cc = lax.axis_index("x"), lax.axis_index("y"), lax.axis_index("c")

    shard2d = {n: wts[n].reshape(wts[n].shape[-2:]) for n in BIG_NAMES}
    halves = []
    for n in BIG_NAMES:
        r, ccols = shard2d[n].shape
        halves.append(_cast_bf16(shard2d[n], f"cast_{n}").reshape(2, r // 2, ccols))
    gathered = _all_gather(halves)
    w = {}
    for (n, axis), full in zip(BIG, gathered):
        _, _, r, ccols = full.shape
        w[n] = full.reshape(N_CHIPS, 2 * r, ccols) if axis == 1 else full.reshape(N_CHIPS * 2 * r, ccols)

    small = {n: wts[n].reshape(-1, wts[n].shape[-1]) for n in SMALL_NAMES}
    small['sg_w'] = wts['sg_w'].reshape(wts['sg_w'].shape[-3:])
    loss, dx, g, gs = _local_step(x.reshape(t, d), loss_target.reshape(t, d), w, small)
    loss = lax.psum(loss[0, 0], ("x", "y", "c"))

    g4 = [g[n].reshape(gathered[i].shape) for i, n in enumerate(BIG_NAMES)]
    c_idx = jnp.reshape(cc, (1,)).astype(jnp.int32)
    p_idx = jnp.reshape(2 * xc + yc, (1,)).astype(jnp.int32)
    from_sibling = _sibling_exchange(g4)
    chip_parts = [_sibling_sum(a, b, c_idx, f"rs_sum1_{n}") for a, b, n in zip(g4, from_sibling, BIG_NAMES)]
    from_chips = _chip_exchange(chip_parts)
    my_halves = [_chip_sum(a, b, p_idx, f"rs_sum2_{n}") for a, b, n in zip(chip_parts, from_chips, BIG_NAMES)]
    reduced = _half_exchange(my_halves)
    grads = {n: r.reshape(shard2d[n].shape) for n, r in zip(BIG_NAMES, reduced)}

    def pack(tree):
        return jnp.concatenate([_as_rows(tree[n]) for n in SMALL_NAMES], axis=0)

    packed = _all_reduce_small(pack(gs))

    delta, new_m, new_v = {}, {}, {}
    for n in BIG_NAMES:
        shape = wts[n].shape
        dl, mm, vv = _adamw(shard2d[n], grads[n], ms[n].reshape(shard2d[n].shape), vs[n].reshape(shard2d[n].shape), f"adamw_{n}")
        grads[n], delta[n], new_m[n], new_v[n] = grads[n].reshape(shape), dl.reshape(shape), mm.reshape(shape), vv.reshape(shape)

    small_out = (packed,) + _adamw(pack(wts), packed, pack(ms), pack(vs), "adamw_small")
    row = 0
    for n in SMALL_NAMES:
        shape = wts[n].shape
        sz = wts[n].size // LANES
        grads[n], delta[n], new_m[n], new_v[n] = (a[row:row + sz].reshape(shape) for a in small_out)
        row += sz + -sz % 8

    return (loss, dx.reshape(x.shape), *[grads[n] for n in WEIGHT_NAMES], *[delta[n] for n in WEIGHT_NAMES],
            *[new_m[n] for n in WEIGHT_NAMES], *[new_v[n] for n in WEIGHT_NAMES])
```

```python
import functools

import jax
import jax.numpy as jnp
from jax import lax
from jax.experimental import pallas as pl
from jax.experimental.pallas import tpu as pltpu

F32 = jnp.float32
BF16 = jnp.bfloat16
MESH = pl.DeviceIdType.MESH

NORM_EPS = 1e-6
LN_EPS = 1e-5
HEAD_DIM = 128
HEADS_PER_GROUP = 4
GROUP_WIDTH = HEADS_PER_GROUP * HEAD_DIM
DILATIONS = (1, 4, 16)
N_GROUPS = len(DILATIONS)
ATT_BLOCK = 128
ROPE_DIM = HEAD_DIM // 4
ROPE_THETA = 500000.0
SG_CHUNK = 128
SG_GROUPS = 12
SG_GROUP_DIM = 128
MASKED = -1e30

ADAM_LR = 0.001
ADAM_B1 = 0.9
ADAM_B2 = 0.999
ADAM_EPS = 1e-08
ADAM_WD = 0.01
ADAM_STEP = 10

N_CHIPS = 4
N_DEV = 8
LANES = 128
MIB = 2 ** 20
VMEM_BYTES_V7X = 64 * MIB

WEIGHT_NAMES = ['ffn1_norm', 'ffn1_w_gate', 'ffn1_w_up', 'ffn1_w_down', 'mix_norm', 'w_in', 'sg_ln_g', 'sg_ln_b',
                'sg_w', 'sg_b', 'w_att_out', 'w_sg_out', 'w_out', 'ffn2_norm', 'ffn2_w_gate', 'ffn2_w_up',
                'ffn2_w_down', 'final_norm']
BIG = [('ffn1_w_gate', 1), ('ffn1_w_up', 1), ('ffn1_w_down', 0), ('w_in', 1), ('w_att_out', 1), ('w_sg_out', 1),
       ('w_out', 0), ('ffn2_w_gate', 1), ('ffn2_w_up', 1), ('ffn2_w_down', 0)]
BIG_NAMES = [n for n, _ in BIG]
SMALL_NAMES = [n for n in WEIGHT_NAMES if n not in BIG_NAMES]


def _nbytes(shape, dtype):
    n = jnp.dtype(dtype).itemsize
    for s in shape:
        if s is not None:
            n *= s
    return n


def _cparams(sem, block_bytes, **kw):
    limit = int(min(max(3 * block_bytes, 32 * MIB), VMEM_BYTES_V7X - 8 * MIB))
    return pltpu.CompilerParams(dimension_semantics=sem, vmem_limit_bytes=limit, **kw)


def _tile(dim, cap):
    best = None
    for t in range(LANES, min(dim, cap) + 1, LANES):
        if dim % t == 0:
            best = t
    if best is None:
        assert dim <= cap, (dim, cap)
        return dim
    return best


def _rows(dim, cap):
    best = None
    for t in range(8, min(dim, cap) + 1, 8):
        if dim % t == 0:
            best = t
    assert best is not None, (dim, cap)
    return best


def _matmul(pairs, mode, out_dtype, name, *, scale=1.0, residual=None, b3=False, out3=0, caps=(1024, 1024, 512)):
    a0, b0 = pairs[0]
    if mode == 'nn':
        m, k = a0.shape
        n = b0.shape[0] * b0.shape[2] if b3 else b0.shape[1]
    elif mode == 'nt':
        m = a0.shape[0]
        n, k = (b0.shape[1], b0.shape[0] * b0.shape[2]) if b3 else b0.shape
    else:
        k, m = a0.shape
        n = b0.shape[1]
    tm = _tile(m, caps[0])
    tn = _tile(n, caps[1])
    tk = _tile(k, caps[2])
    if b3 and mode == 'nn':
        tn = b0.shape[2]
    if b3 and mode == 'nt':
        tk = b0.shape[2]
    if out3:
        tn = n // out3
    nk = k // tk
    if mode == 'tn':
        a_spec = pl.BlockSpec((tk, tm), lambda i, j, kk: (kk, i))
        b_spec = pl.BlockSpec((tk, tn), lambda i, j, kk: (kk, j))
        dims = ((0,), (0,))
    elif mode == 'nn':
        a_spec = pl.BlockSpec((tm, tk), lambda i, j, kk: (i, kk))
        b_spec = (pl.BlockSpec((None, tk, tn), lambda i, j, kk: (j, kk, 0)) if b3
                  else pl.BlockSpec((tk, tn), lambda i, j, kk: (kk, j)))
        dims = ((1,), (0,))
    else:
        a_spec = pl.BlockSpec((tm, tk), lambda i, j, kk: (i, kk))
        b_spec = (pl.BlockSpec((None, tn, tk), lambda i, j, kk: (kk, j, 0)) if b3
                  else pl.BlockSpec((tn, tk), lambda i, j, kk: (j, kk)))
        dims = ((1,), (1,))
    in_specs, operands = [], []
    for a, b in pairs:
        in_specs += [a_spec, b_spec]
        operands += [a, b]
    block_bytes = len(pairs) * (_nbytes((tm, tk), a0.dtype) + _nbytes((tk, tn), b0.dtype))
    if residual is not None:
        in_specs.append(pl.BlockSpec((tm, tn), lambda i, j, kk: (i, j)))
        operands.append(residual)
        block_bytes += _nbytes((tm, tn), F32)
    if out3:
        out_spec = pl.BlockSpec((None, tm, tn), lambda i, j, kk: (j, i, 0))
        out_shape = jax.ShapeDtypeStruct((out3, m, tn), out_dtype)
    else:
        out_spec = pl.BlockSpec((tm, tn), lambda i, j, kk: (i, j))
        out_shape = jax.ShapeDtypeStruct((m, n), out_dtype)
    block_bytes += _nbytes((tm, tn), out_dtype) + _nbytes((tm, tn), F32)
    n_pairs = len(pairs)
    has_res = residual is not None

    def body(*refs):
        o_ref, acc = refs[-2], refs[-1]
        kk = pl.program_id(2)

        @pl.when(kk == 0)
        def _():
            acc[...] = jnp.zeros_like(acc)

        part = None
        for p in range(n_pairs):
            d = lax.dot_general(refs[2 * p][...].astype(BF16), refs[2 * p + 1][...].astype(BF16),
                                (dims, ((), ())), preferred_element_type=F32)
            part = d if part is None else part + d
        acc[...] += part

        @pl.when(kk == nk - 1)
        def _():
            r = acc[...]
            if scale != 1.0:
                r = r * scale
            if has_res:
                r = refs[2 * n_pairs][...] + r
            o_ref[...] = r.astype(out_dtype)

    return pl.pallas_call(
        body, out_shape=out_shape, grid=(m // tm, n // tn, nk), in_specs=in_specs, out_specs=out_spec,
        scratch_shapes=[pltpu.VMEM((tm, tn), F32)],
        compiler_params=_cparams(("parallel", "parallel", "arbitrary"), block_bytes), name=name)(*operands)


def _rmsnorm_fwd(x, g, name):
    t, d = x.shape
    tm = _rows(t, 512)

    def body(x_ref, g_ref, o_ref):
        xv = x_ref[...]
        r = lax.rsqrt(jnp.mean(xv * xv, axis=1, keepdims=True) + NORM_EPS)
        o_ref[...] = (xv * r * g_ref[...]).astype(BF16)

    row = pl.BlockSpec((tm, d), lambda i: (i, 0))
    return pl.pallas_call(
        body, out_shape=jax.ShapeDtypeStruct((t, d), BF16), grid=(t // tm,),
        in_specs=[row, pl.BlockSpec((1, d), lambda i: (0, 0))], out_specs=row,
        compiler_params=_cparams(("parallel",), 2 * _nbytes((tm, d), F32)), name=name)(x, g)


def _rms_grad(xv, g, dn, d):
    r = lax.rsqrt(jnp.mean(xv * xv, axis=1, keepdims=True) + NORM_EPS)
    u = dn * g
    s = jnp.sum(xv * u, axis=1, keepdims=True)
    dx = r * u - xv * (r * r * r) * (s * (1.0 / d))
    return dx, dn * xv * r


def _rmsnorm_bwd(x, g, dn, dres, name):
    t, d = x.shape
    tm = _rows(t, 256)

    def body(x_ref, g_ref, dn_ref, dres_ref, dx_ref, dxb_ref, dg_ref):
        dx, dg_rows = _rms_grad(x_ref[...], g_ref[...], dn_ref[...].astype(F32), d)
        dx = dres_ref[...] + dx
        dx_ref[...] = dx
        dxb_ref[...] = dx.astype(BF16)

        @pl.when(pl.program_id(0) == 0)
        def _():
            dg_ref[...] = jnp.zeros_like(dg_ref)

        dg_ref[...] += jnp.sum(dg_rows, axis=0, keepdims=True)

    row = pl.BlockSpec((tm, d), lambda i: (i, 0))
    vec = pl.BlockSpec((1, d), lambda i: (0, 0))
    return pl.pallas_call(
        body, out_shape=(jax.ShapeDtypeStruct((t, d), F32), jax.ShapeDtypeStruct((t, d), BF16), jax.ShapeDtypeStruct((1, d), F32)),
        grid=(t // tm,), in_specs=[row, vec, row, row], out_specs=(row, row, vec),
        compiler_params=_cparams(("arbitrary",), 5 * _nbytes((tm, d), F32)), name=name)(x, g, dn, dres)


def _final_loss(x, g, target, name):
    t, d = x.shape
    tm = _rows(t, 256)

    def body(x_ref, g_ref, t_ref, loss_ref, dx_ref, dxb_ref, dg_ref):
        xv, gv = x_ref[...], g_ref[...]
        r = lax.rsqrt(jnp.mean(xv * xv, axis=1, keepdims=True) + NORM_EPS)
        err = xv * r * gv - t_ref[...]
        dx, dg_rows = _rms_grad(xv, gv, err * (1.0 / d), d)
        dx_ref[...] = dx
        dxb_ref[...] = dx.astype(BF16)

        @pl.when(pl.program_id(0) == 0)
        def _():
            dg_ref[...] = jnp.zeros_like(dg_ref)
            loss_ref[...] = jnp.zeros_like(loss_ref)

        dg_ref[...] += jnp.sum(dg_rows, axis=0, keepdims=True)
        row_loss = jnp.sum(err * err, axis=1, keepdims=True) * (0.5 / d)
        loss_ref[...] += jnp.sum(row_loss, axis=0, keepdims=True)

    row = pl.BlockSpec((tm, d), lambda i: (i, 0))
    vec = pl.BlockSpec((1, d), lambda i: (0, 0))
    return pl.pallas_call(
        body, out_shape=(jax.ShapeDtypeStruct((1, 1), F32), jax.ShapeDtypeStruct((t, d), F32),
                         jax.ShapeDtypeStruct((t, d), BF16), jax.ShapeDtypeStruct((1, d), F32)),
        grid=(t // tm,), in_specs=[row, vec, row], out_specs=(pl.BlockSpec((1, 1), lambda i: (0, 0)), row, row, vec),
        compiler_params=_cparams(("arbitrary",), 4 * _nbytes((tm, d), F32)), name=name)(x, g, target)


def _ffn_up(n, wg, wu, name):
    t, d = n.shape
    s, _, f = wg.shape
    tm, tk = _tile(t, 1024), _tile(d, 512)
    nk = d // tk

    def body(n_ref, wg_ref, wu_ref, a_ref, b_ref, h_ref, acc_g, acc_u):
        kk = pl.program_id(2)

        @pl.when(kk == 0)
        def _():
            acc_g[...] = jnp.zeros_like(acc_g)
            acc_u[...] = jnp.zeros_like(acc_u)

        nv = n_ref[...]
        acc_g[...] += jnp.dot(nv, wg_ref[...], preferred_element_type=F32)
        acc_u[...] += jnp.dot(nv, wu_ref[...], preferred_element_type=F32)

        @pl.when(kk == nk - 1)
        def _():
            a, b = acc_g[...], acc_u[...]
            a_ref[...] = a.astype(BF16)
            b_ref[...] = b.astype(BF16)
            h_ref[...] = (a * jax.nn.sigmoid(a) * b).astype(BF16)

    w_spec = pl.BlockSpec((None, tk, f), lambda i, j, kk: (j, kk, 0))
    o_spec = pl.BlockSpec((tm, f), lambda i, j, kk: (i, j))
    out = jax.ShapeDtypeStruct((t, s * f), BF16)
    block_bytes = _nbytes((tm, tk), BF16) + 2 * _nbytes((tk, f), BF16) + 3 * _nbytes((tm, f), BF16) + 2 * _nbytes((tm, f), F32)
    return pl.pallas_call(
        body, out_shape=(out, out, out), grid=(t // tm, s, nk),
        in_specs=[pl.BlockSpec((tm, tk), lambda i, j, kk: (i, kk)), w_spec, w_spec], out_specs=(o_spec, o_spec, o_spec),
        scratch_shapes=[pltpu.VMEM((tm, f), F32), pltpu.VMEM((tm, f), F32)],
        compiler_params=_cparams(("parallel", "parallel", "arbitrary"), block_bytes), name=name)(n, wg, wu)


def _ffn_bwd_act(dx, wd, a, b, name):
    t, d = dx.shape
    f = wd.shape[0]
    tm, tn, tk = _tile(t, 1024), _tile(f, 1536), _tile(d, 512)
    nk = d // tk

    def body(dx_ref, wd_ref, a_ref, b_ref, da_ref, db_ref, acc):
        kk = pl.program_id(2)

        @pl.when(kk == 0)
        def _():
            acc[...] = jnp.zeros_like(acc)

        acc[...] += lax.dot_general(dx_ref[...].astype(BF16), wd_ref[...], ((((1,), (1,))), ((), ())),
                                    preferred_element_type=F32)

        @pl.when(kk == nk - 1)
        def _():
            dh = 0.5 * acc[...]
            av, bv = a_ref[...].astype(F32), b_ref[...].astype(F32)
            sg = jax.nn.sigmoid(av)
            da_ref[...] = (dh * bv * (sg * (1.0 + av * (1.0 - sg)))).astype(BF16)
            db_ref[...] = (dh * (av * sg)).astype(BF16)

    act = pl.BlockSpec((tm, tn), lambda i, j, kk: (i, j))
    out = jax.ShapeDtypeStruct((t, f), BF16)
    block_bytes = _nbytes((tm, tk), F32) + _nbytes((tn, tk), BF16) + 4 * _nbytes((tm, tn), BF16) + _nbytes((tm, tn), F32)
    return pl.pallas_call(
        body, out_shape=(out, out), grid=(t // tm, f // tn, nk),
        in_specs=[pl.BlockSpec((tm, tk), lambda i, j, kk: (i, kk)), pl.BlockSpec((tn, tk), lambda i, j, kk: (j, kk)),
                  act, act],
        out_specs=(act, act), scratch_shapes=[pltpu.VMEM((tm, tn), F32)],
        compiler_params=_cparams(("parallel", "parallel", "arbitrary"), block_bytes), name=name)(dx, wd, a, b)


def _ffn_forward(x, gain, wg, wu, wd, tag):
    n = _rmsnorm_fwd(x, gain, f"{tag}_norm")
    a, b, h = _ffn_up(n, wg, wu, f"{tag}_up")
    x_next = _matmul([(h, wd)], 'nn', F32, f"{tag}_down", scale=0.5, residual=x, caps=(1024, 1024, 1536))
    return x_next, (n, a, b, h)


def _ffn_backward(x, gain, wg, wu, wd, saved, dx_next, dx_next_b, tag):
    n, a, b, h = saved
    s = wg.shape[0]
    da, db = _ffn_bwd_act(dx_next_b, wd, a, b, f"{tag}_bwd_act")
    g_wd = _matmul([(h, dx_next_b)], 'tn', BF16, f"{tag}_dwd", scale=0.5, caps=(1536, 1024, 1024))
    g_wg = _matmul([(n, da)], 'tn', BF16, f"{tag}_dwg", out3=s, caps=(1024, 1024, 1024))
    g_wu = _matmul([(n, db)], 'tn', BF16, f"{tag}_dwu", out3=s, caps=(1024, 1024, 1024))
    dn = _matmul([(da, wg), (db, wu)], 'nt', F32, f"{tag}_dn", b3=True)
    dx, dx_b, g_gain = _rmsnorm_bwd(x, gain, dn, dx_next, f"{tag}_norm_bwd")
    return dx, dx_b, g_gain, g_wg, g_wu, g_wd


def _rope_tables(seq):
    half = ROPE_DIM // 2
    inv_freq = ROPE_THETA ** (-jnp.arange(0, ROPE_DIM, 2, dtype=F32) / ROPE_DIM)
    ang = jnp.arange(seq).astype(F32)[:, None] * inv_freq[None, :]
    cos, sin = jnp.cos(ang), jnp.sin(ang)
    zeros = lambda w: jnp.zeros((seq, w), F32)
    c = jnp.concatenate([cos, cos, jnp.ones((seq, HEAD_DIM - ROPE_DIM), F32)], axis=1)
    s_up = jnp.concatenate([-sin, zeros(HEAD_DIM - half)], axis=1)
    s_dn = jnp.concatenate([zeros(half), sin, zeros(HEAD_DIM - ROPE_DIM)], axis=1)
    return c, s_up, s_dn


def _rope(x, width, tables, name):
    t = x.shape[0]
    tm = _rows(t, 512)
    half = ROPE_DIM // 2
    c, s_up, s_dn = tables

    def body(x_ref, c_ref, up_ref, dn_ref, o_ref):
        cv, uv, dv = c_ref[...], up_ref[...], dn_ref[...]
        for h in range(GROUP_WIDTH // HEAD_DIM):
            sl = slice(h * HEAD_DIM, (h + 1) * HEAD_DIM)
            xv = x_ref[:, sl].astype(F32)
            o_ref[:, sl] = (xv * cv + pltpu.roll(xv, HEAD_DIM - half, 1) * uv + pltpu.roll(xv, half, 1) * dv).astype(BF16)

    blk = pl.BlockSpec((tm, GROUP_WIDTH), lambda i, j: (i, j))
    tab = pl.BlockSpec((tm, HEAD_DIM), lambda i, j: (i, 0))
    return pl.pallas_call(
        body, out_shape=jax.ShapeDtypeStruct((t, width), BF16), grid=(t // tm, width // GROUP_WIDTH),
        in_specs=[blk, tab, tab, tab], out_specs=blk,
        compiler_params=_cparams(("parallel", "parallel"), 2 * _nbytes((tm, GROUP_WIDTH), F32)), name=name)(x, c, s_up, s_dn)


def _att_masks():
    qi = lax.broadcasted_iota(jnp.int32, (ATT_BLOCK, ATT_BLOCK), 0)
    kj = lax.broadcasted_iota(jnp.int32, (ATT_BLOCK, ATT_BLOCK), 1)
    return kj >= qi, kj <= qi


def _scores(q, k):
    return lax.dot_general(q, k, (((1,), (1,)), ((), ())), preferred_element_type=F32) * (HEAD_DIM ** -0.5)


def _att_fwd(q, k, v, offs, dil, name):
    qo, ko, vo = offs
    length = q.shape[0]
    nb = length // ATT_BLOCK

    def body(q_ref, kp_ref, kc_ref, vp_ref, vc_ref, o_ref, lse_ref):
        has_prev = pl.program_id(1) > 0
        m_prev, m_cur = _att_masks()
        m_prev = jnp.logical_and(m_prev, has_prev)
        for h in range(HEADS_PER_GROUP):
            sl = slice(h * HEAD_DIM, (h + 1) * HEAD_DIM)
            qv = q_ref[:, sl]
            s_p = jnp.where(m_prev, _scores(qv, kp_ref[:, sl]), MASKED)
            s_c = jnp.where(m_cur, _scores(qv, kc_ref[:, sl]), MASKED)
            m = jnp.maximum(jnp.max(s_p, axis=1, keepdims=True), jnp.max(s_c, axis=1, keepdims=True))
            p_p, p_c = jnp.exp(s_p - m), jnp.exp(s_c - m)
            l = jnp.sum(p_p, axis=1, keepdims=True) + jnp.sum(p_c, axis=1, keepdims=True)
            acc = jnp.dot(p_p.astype(BF16), vp_ref[:, sl], preferred_element_type=F32)
            acc += jnp.dot(p_c.astype(BF16), vc_ref[:, sl], preferred_element_type=F32)
            o_ref[:, sl] = acc / l
            lse_ref[:, sl] = jnp.broadcast_to(m + jnp.log(l), (ATT_BLOCK, HEAD_DIM))

    def spec(off, prev):
        if prev:
            return pl.BlockSpec((ATT_BLOCK, GROUP_WIDTH), lambda r, n: (jnp.maximum(n - 1, 0), off + r))
        return pl.BlockSpec((ATT_BLOCK, GROUP_WIDTH), lambda r, n: (n, off + r))

    out = jax.ShapeDtypeStruct((length, dil * GROUP_WIDTH), F32)
    o_spec = pl.BlockSpec((ATT_BLOCK, GROUP_WIDTH), lambda r, n: (n, r))
    return pl.pallas_call(
        body, out_shape=(out, out), grid=(dil, nb),
        in_specs=[spec(qo, False), spec(ko, True), spec(ko, False), spec(vo, True), spec(vo, False)],
        out_specs=(o_spec, o_spec),
        compiler_params=_cparams(("parallel", "parallel"), 8 * _nbytes((ATT_BLOCK, GROUP_WIDTH), F32)), name=name)(q, k, k, v, v)


def _att_combine(outs, lses, name):
    t = outs[0].shape[0]
    tm = _rows(t, 512)

    def body(*refs):
        o_refs, l_refs = refs[:N_GROUPS], refs[N_GROUPS:2 * N_GROUPS]
        ob_ref, of_ref, lse_ref = refs[2 * N_GROUPS:]
        ls = [r[...] for r in l_refs]
        m = functools.reduce(jnp.maximum, ls)
        ws = [jnp.exp(l - m) for l in ls]
        den = functools.reduce(jnp.add, ws)
        num = functools.reduce(jnp.add, [w * r[...] for w, r in zip(ws, o_refs)])
        o = num / den
        ob_ref[...] = o.astype(BF16)
        of_ref[...] = o
        lse_ref[...] = m + jnp.log(den)

    blk = pl.BlockSpec((tm, GROUP_WIDTH), lambda i: (i, 0))
    f32 = jax.ShapeDtypeStruct((t, GROUP_WIDTH), F32)
    return pl.pallas_call(
        body, out_shape=(jax.ShapeDtypeStruct((t, GROUP_WIDTH), BF16), f32, f32), grid=(t // tm,),
        in_specs=[blk] * (2 * N_GROUPS), out_specs=(blk, blk, blk),
        compiler_params=_cparams(("parallel",), 9 * _nbytes((tm, GROUP_WIDTH), F32)), name=name)(*outs, *lses)


def _att_delta(do, o, name):
    t = o.shape[0]
    tm = _rows(t, 512)

    def body(do_ref, o_ref, d_ref):
        for h in range(HEADS_PER_GROUP):
            sl = slice(h * HEAD_DIM, (h + 1) * HEAD_DIM)
            s = jnp.sum(do_ref[:, sl] * o_ref[:, sl], axis=1, keepdims=True)
            d_ref[:, sl] = jnp.broadcast_to(s, (tm, HEAD_DIM))

    blk = pl.BlockSpec((tm, GROUP_WIDTH), lambda i: (i, 0))
    return pl.pallas_call(
        body, out_shape=jax.ShapeDtypeStruct((t, GROUP_WIDTH), F32), grid=(t // tm,), in_specs=[blk, blk], out_specs=blk,
        compiler_params=_cparams(("parallel",), 3 * _nbytes((tm, GROUP_WIDTH), F32)), name=name)(do, o)


def _att_bwd_dq(q, k, v, do, lse, delta, offs, dil, name):
    qo, ko, vo = offs
    length = q.shape[0]
    nb = length // ATT_BLOCK
    scale = HEAD_DIM ** -0.5

    def body(q_ref, kp_ref, kc_ref, vp_ref, vc_ref, do_ref, lse_ref, dl_ref, dq_ref):
        has_prev = pl.program_id(1) > 0
        m_prev, m_cur = _att_masks()
        m_prev = jnp.logical_and(m_prev, has_prev)
        for h in range(HEADS_PER_GROUP):
            sl = slice(h * HEAD_DIM, (h + 1) * HEAD_DIM)
            qv, dov = q_ref[:, sl], do_ref[:, sl].astype(BF16)
            lsev, dlv = lse_ref[:, sl], dl_ref[:, sl]
            dq = None
            for mask, k_ref, v_ref in ((m_prev, kp_ref, vp_ref), (m_cur, kc_ref, vc_ref)):
                kv = k_ref[:, sl]
                p = jnp.exp(jnp.where(mask, _scores(qv, kv), MASKED) - lsev)
                dp = lax.dot_general(dov, v_ref[:, sl], (((1,), (1,)), ((), ())), preferred_element_type=F32)
                ds = (p * (dp - dlv) * scale).astype(BF16)
                part = jnp.dot(ds, kv, preferred_element_type=F32)
                dq = part if dq is None else dq + part
            dq_ref[:, sl] = dq.astype(BF16)

    def spec(off, prev):
        if prev:
            return pl.BlockSpec((ATT_BLOCK, GROUP_WIDTH), lambda r, n: (jnp.maximum(n - 1, 0), off + r))
        return pl.BlockSpec((ATT_BLOCK, GROUP_WIDTH), lambda r, n: (n, off + r))

    own = pl.BlockSpec((ATT_BLOCK, GROUP_WIDTH), lambda r, n: (n, r))
    return pl.pallas_call(
        body, out_shape=jax.ShapeDtypeStruct((length, dil * GROUP_WIDTH), BF16), grid=(dil, nb),
        in_specs=[spec(qo, False), spec(ko, True), spec(ko, False), spec(vo, True), spec(vo, False), own, own, own],
        out_specs=own,
        compiler_params=_cparams(("parallel", "parallel"), 10 * _nbytes((ATT_BLOCK, GROUP_WIDTH), F32)),
        name=name)(q, k, k, v, v, do, lse, delta)


def _att_bwd_dkv(q, k, v, do, lse, delta, offs, dil, name):
    qo, ko, vo = offs
    length = q.shape[0]
    nb = length // ATT_BLOCK
    scale = HEAD_DIM ** -0.5

    def body(k_ref, v_ref, qc_ref, qn_ref, doc_ref, don_ref, lsec_ref, lsen_ref, dlc_ref, dln_ref, dk_ref, dv_ref):
        has_next = pl.program_id(1) < nb - 1
        m_prev, m_cur = _att_masks()
        m_prev = jnp.logical_and(m_prev, has_next)
        for h in range(HEADS_PER_GROUP):
            sl = slice(h * HEAD_DIM, (h + 1) * HEAD_DIM)
            kv, vv = k_ref[:, sl], v_ref[:, sl]
            dk = dv = None
            for mask, q_ref, do_ref, lse_ref, dl_ref in ((m_cur, qc_ref, doc_ref, lsec_ref, dlc_ref),
                                                         (m_prev, qn_ref, don_ref, lsen_ref, dln_ref)):
                qv, dov = q_ref[:, sl], do_ref[:, sl].astype(BF16)
                p = jnp.exp(jnp.where(mask, _scores(qv, kv), MASKED) - lse_ref[:, sl])
                dp = lax.dot_general(dov, vv, (((1,), (1,)), ((), ())), preferred_element_type=F32)
                ds = (p * (dp - dl_ref[:, sl]) * scale).astype(BF16)
                dv_part = lax.dot_general(p.astype(BF16), dov, (((0,), (0,)), ((), ())), preferred_element_type=F32)
                dk_part = lax.dot_general(ds, qv, (((0,), (0,)), ((), ())), preferred_element_type=F32)
                dv = dv_part if dv is None else dv + dv_part
                dk = dk_part if dk is None else dk + dk_part
            dk_ref[:, sl] = dk.astype(BF16)
            dv_ref[:, sl] = dv.astype(BF16)

    def spec(off, nxt):
        if nxt:
            return pl.BlockSpec((ATT_BLOCK, GROUP_WIDTH), lambda r, n: (jnp.minimum(n + 1, nb - 1), off + r))
        return pl.BlockSpec((ATT_BLOCK, GROUP_WIDTH), lambda r, n: (n, off + r))

    own = pl.BlockSpec((ATT_BLOCK, GROUP_WIDTH), lambda r, n: (n, r))
    out = jax.ShapeDtypeStruct((length, dil * GROUP_WIDTH), BF16)
    return pl.pallas_call(
        body, out_shape=(out, out), grid=(dil, nb),
        in_specs=[spec(ko, False), spec(vo, False), spec(qo, False), spec(qo, True), spec(0, False), spec(0, True),
                  spec(0, False), spec(0, True), spec(0, False), spec(0, True)],
        out_specs=(own, own),
        compiler_params=_cparams(("parallel", "parallel"), 12 * _nbytes((ATT_BLOCK, GROUP_WIDTH), F32)),
        name=name)(k, v, q, q, do, do, lse, lse, delta, delta)


def _gelu(x):
    return 0.5 * x * (1.0 + lax.erf(x * (2.0 ** -0.5)))


def _gelu_grad(x):
    return 0.5 * (1.0 + lax.erf(x * (2.0 ** -0.5))) + x * jnp.exp(-0.5 * x * x) * ((2.0 * jnp.pi) ** -0.5)


def _sg_normed(vs, lg, lb):
    gv = _gelu(vs)
    mu = jnp.mean(gv, axis=1, keepdims=True)
    xc = gv - mu
    rstd = lax.rsqrt(jnp.mean(xc * xc, axis=1, keepdims=True) + LN_EPS)
    z = xc * rstd
    return z, rstd, z * lg + lb


def _sg_tril():
    row = lax.broadcasted_iota(jnp.int32, (SG_CHUNK, SG_CHUNK), 0)
    col = lax.broadcasted_iota(jnp.int32, (SG_CHUNK, SG_CHUNK), 1)
    return row >= col


def _sg_fwd(proj, u_blk, vs_blk, lg, lb, sg_w, bias, name):
    t = proj.shape[0]
    width = SG_GROUPS * SG_GROUP_DIM

    def body(u_ref, vs_ref, lg_ref, lb_ref, w_ref, bias_ref, o_ref):
        _, _, vn = _sg_normed(vs_ref[...].astype(F32), lg_ref[...], lb_ref[...])
        vn = vn.astype(BF16)
        tril = _sg_tril()
        for g in range(SG_GROUPS):
            sl = slice(g * SG_GROUP_DIM, (g + 1) * SG_GROUP_DIM)
            w = jnp.where(tril, w_ref[g], 0.0).astype(BF16)
            sp = jnp.dot(w, vn[:, sl], preferred_element_type=F32) + bias_ref[:, sl]
            o_ref[:, sl] = (_gelu(u_ref[:, sl].astype(F32)) * sp).astype(BF16)

    vec = pl.BlockSpec((1, width), lambda i: (0, 0))
    return pl.pallas_call(
        body, out_shape=jax.ShapeDtypeStruct((t, width), BF16), grid=(t // SG_CHUNK,),
        in_specs=[pl.BlockSpec((SG_CHUNK, width), lambda i: (i, u_blk)), pl.BlockSpec((SG_CHUNK, width), lambda i: (i, vs_blk)),
                  vec, vec, pl.BlockSpec((SG_GROUPS, SG_CHUNK, SG_CHUNK), lambda i: (0, 0, 0)),
                  pl.BlockSpec((SG_CHUNK, width), lambda i: (0, 0))],
        out_specs=pl.BlockSpec((SG_CHUNK, width), lambda i: (i, 0)),
        compiler_params=_cparams(("parallel",), 8 * _nbytes((SG_CHUNK, width), F32)), name=name)(proj, proj, lg, lb, sg_w, bias)


def _sg_bwd(proj, u_blk, vs_blk, dsu, lg, lb, sg_w, bias, name):
    t = proj.shape[0]
    width = SG_GROUPS * SG_GROUP_DIM

    def body(u_ref, vs_ref, dsu_ref, lg_ref, lb_ref, w_ref, bias_ref, du_ref, dvs_ref, dw_ref, dbias_ref, dlg_ref, dlb_ref):
        @pl.when(pl.program_id(0) == 0)
        def _():
            dw_ref[...] = jnp.zeros_like(dw_ref)
            dbias_ref[...] = jnp.zeros_like(dbias_ref)
            dlg_ref[...] = jnp.zeros_like(dlg_ref)
            dlb_ref[...] = jnp.zeros_like(dlb_ref)

        vs = vs_ref[...].astype(F32)
        z, rstd, vn = _sg_normed(vs, lg_ref[...], lb_ref[...])
        vn = vn.astype(BF16)
        tril = _sg_tril()
        dvn = []
        for g in range(SG_GROUPS):
            sl = slice(g * SG_GROUP_DIM, (g + 1) * SG_GROUP_DIM)
            w = jnp.where(tril, w_ref[g], 0.0).astype(BF16)
            vg = vn[:, sl]
            sp = jnp.dot(w, vg, preferred_element_type=F32) + bias_ref[:, sl]
            uv = u_ref[:, sl].astype(F32)
            dsu_g = dsu_ref[:, sl].astype(F32)
            du_ref[:, sl] = (dsu_g * sp * _gelu_grad(uv)).astype(BF16)
            dsp = dsu_g * _gelu(uv)
            dsp_b = dsp.astype(BF16)
            dw = lax.dot_general(dsp_b, vg, (((1,), (1,)), ((), ())), preferred_element_type=F32)
            dw_ref[g] += jnp.where(tril, dw, 0.0)
            dbias_ref[:, sl] += jnp.broadcast_to(jnp.sum(dsp, axis=1, keepdims=True), (SG_CHUNK, SG_GROUP_DIM))
            dvn.append(lax.dot_general(w, dsp_b, (((0,), (0,)), ((), ())), preferred_element_type=F32))
        dvn = jnp.concatenate(dvn, axis=1)
        dlg_ref[...] += jnp.sum(dvn * z, axis=0, keepdims=True)
        dlb_ref[...] += jnp.sum(dvn, axis=0, keepdims=True)
        dz = dvn * lg_ref[...]
        dgv = rstd * (dz - jnp.mean(dz, axis=1, keepdims=True) - z * jnp.mean(dz * z, axis=1, keepdims=True))
        dvs_ref[...] = (dgv * _gelu_grad(vs)).astype(BF16)

    vec = pl.BlockSpec((1, width), lambda i: (0, 0))
    row = pl.BlockSpec((SG_CHUNK, width), lambda i: (i, 0))
    fixed = pl.BlockSpec((SG_CHUNK, width), lambda i: (0, 0))
    w_spec = pl.BlockSpec((SG_GROUPS, SG_CHUNK, SG_CHUNK), lambda i: (0, 0, 0))
    act = jax.ShapeDtypeStruct((t, width), BF16)
    return pl.pallas_call(
        body,
        out_shape=(act, act, jax.ShapeDtypeStruct((SG_GROUPS, SG_CHUNK, SG_CHUNK), F32),
                   jax.ShapeDtypeStruct((SG_CHUNK, width), F32), jax.ShapeDtypeStruct((1, width), F32),
                   jax.ShapeDtypeStruct((1, width), F32)),
        grid=(t // SG_CHUNK,),
        in_specs=[pl.BlockSpec((SG_CHUNK, width), lambda i: (i, u_blk)), pl.BlockSpec((SG_CHUNK, width), lambda i: (i, vs_blk)),
                  row, vec, vec, w_spec, fixed],
        out_specs=(row, row, w_spec, fixed, vec, vec),
        compiler_params=_cparams(("arbitrary",), 14 * _nbytes((SG_CHUNK, width), F32)),
        name=name)(proj, proj, dsu, lg, lb, sg_w, bias)


def _gate_fwd(proj, ga_blk, gs_blk, y_att, y_sg, name):
    t, d = y_att.shape
    tm, tn = _rows(t, 512), _tile(d, GROUP_WIDTH)

    def body(ga_ref, gs_ref, ya_ref, ys_ref, o_ref):
        o_ref[...] = (jax.nn.sigmoid(ga_ref[...].astype(F32)) * ya_ref[...].astype(F32)
                      + jax.nn.sigmoid(gs_ref[...].astype(F32)) * ys_ref[...].astype(F32)).astype(BF16)

    own = pl.BlockSpec((tm, tn), lambda i, j: (i, j))
    return pl.pallas_call(
        body, out_shape=jax.ShapeDtypeStruct((t, d), BF16), grid=(t // tm, d // tn),
        in_specs=[pl.BlockSpec((tm, tn), lambda i, j: (i, ga_blk + j)), pl.BlockSpec((tm, tn), lambda i, j: (i, gs_blk + j)),
                  own, own],
        out_specs=own, compiler_params=_cparams(("parallel", "parallel"), 6 * _nbytes((tm, tn), F32)),
        name=name)(proj, proj, y_att, y_sg)


def _gate_bwd(proj, ga_blk, gs_blk, y_att, y_sg, dmerged, name):
    t, d = y_att.shape
    tm, tn = _rows(t, 512), _tile(d, GROUP_WIDTH)

    def body(ga_ref, gs_ref, ya_ref, ys_ref, dm_ref, dya_ref, dys_ref, dga_ref, dgs_ref):
        dm = dm_ref[...].astype(F32)
        for g_ref, y_ref, dy_ref, dg_ref in ((ga_ref, ya_ref, dya_ref, dga_ref), (gs_ref, ys_ref, dys_ref, dgs_ref)):
            sg = jax.nn.sigmoid(g_ref[...].astype(F32))
            dy_ref[...] = (dm * sg).astype(BF16)
            dg_ref[...] = (dm * y_ref[...].astype(F32) * sg * (1.0 - sg)).astype(BF16)

    own = pl.BlockSpec((tm, tn), lambda i, j: (i, j))
    out = jax.ShapeDtypeStruct((t, d), BF16)
    return pl.pallas_call(
        body, out_shape=(out, out, out, out), grid=(t // tm, d // tn),
        in_specs=[pl.BlockSpec((tm, tn), lambda i, j: (i, ga_blk + j)), pl.BlockSpec((tm, tn), lambda i, j: (i, gs_blk + j)),
                  own, own, own],
        out_specs=(own, own, own, own), compiler_params=_cparams(("parallel", "parallel"), 10 * _nbytes((tm, tn), F32)),
        name=name)(proj, proj, y_att, y_sg, dmerged)


def _group_view(arr, col, dil):
    t = arr.shape[0]
    return arr[:, col:col + GROUP_WIDTH].reshape(t // dil, dil * GROUP_WIDTH)


def _mixer_forward(x, w, small):
    t, d = x.shape
    att_w = N_GROUPS * GROUP_WIDTH
    sg_w = SG_GROUPS * SG_GROUP_DIM
    n = _rmsnorm_fwd(x, small['mix_norm'], "mix_norm")
    proj = _matmul([(n, w['w_in'])], 'nn', BF16, "mix_in", b3=True, caps=(1024, 1024, 1024))
    tables = _rope_tables(t)
    qk = _rope(proj, 2 * att_w, tables, "mix_rope")
    outs, lses = [], []
    for gi, dil in enumerate(DILATIONS):
        if dil == 1:
            args = (qk, qk, proj, (gi, N_GROUPS + gi, 2 * N_GROUPS + gi))
        else:
            args = (_group_view(qk, gi * GROUP_WIDTH, dil), _group_view(qk, att_w + gi * GROUP_WIDTH, dil),
                    _group_view(proj, 2 * att_w + gi * GROUP_WIDTH, dil), (0, 0, 0))
        o, lse = _att_fwd(*args, dil, f"att_fwd{gi}")
        outs.append(o.reshape(t, GROUP_WIDTH))
        lses.append(lse.reshape(t, GROUP_WIDTH))
    o_b, o_f, lse = _att_combine(outs, lses, "att_combine")
    y_att = _matmul([(o_b, w['w_att_out'])], 'nn', BF16, "mix_att_out", b3=True)
    bias = jnp.repeat(small['sg_b'].T, SG_GROUP_DIM, axis=1)
    u_blk, vs_blk = 3 * att_w // sg_w, 3 * att_w // sg_w + 1
    su = _sg_fwd(proj, u_blk, vs_blk, small['sg_ln_g'], small['sg_ln_b'], small['sg_w'], bias, "sg_fwd")
    y_sg = _matmul([(su, w['w_sg_out'])], 'nn', BF16, "mix_sg_out", b3=True)
    ga_blk = (3 * att_w + 2 * sg_w) // _tile(d, GROUP_WIDTH)
    gs_blk = ga_blk + d // _tile(d, GROUP_WIDTH)
    merged = _gate_fwd(proj, ga_blk, gs_blk, y_att, y_sg, "gate_fwd")
    x_next = _matmul([(merged, w['w_out'])], 'nn', F32, "mix_out", residual=x)
    saved = (n, proj, qk, tables, o_b, o_f, lse, y_att, su, y_sg, merged, bias, (u_blk, vs_blk, ga_blk, gs_blk))
    return x_next, saved


def _mixer_backward(x, w, small, saved, dx_next, dx_next_b):
    n, proj, qk, tables, o_b, o_f, lse, y_att, su, y_sg, merged, bias, (u_blk, vs_blk, ga_blk, gs_blk) = saved
    t, d = x.shape
    att_w = N_GROUPS * GROUP_WIDTH
    s = N_CHIPS
    g = {}
    dmerged = _matmul([(dx_next_b, w['w_out'])], 'nt', BF16, "mix_out_dx")
    g['w_out'] = _matmul([(merged, dx_next_b)], 'tn', BF16, "mix_out_dw", caps=(1024, 1024, 1024))
    dy_att, dy_sg, dg_att, dg_sg = _gate_bwd(proj, ga_blk, gs_blk, y_att, y_sg, dmerged, "gate_bwd")

    g['w_att_out'] = _matmul([(o_b, dy_att)], 'tn', BF16, "mix_att_out_dw", out3=s)
    do = _matmul([(dy_att, w['w_att_out'])], 'nt', F32, "mix_att_out_dx", b3=True)
    delta = _att_delta(do, o_f, "att_delta")
    dqs, dks, dvs_ = [], [], []
    for gi, dil in enumerate(DILATIONS):
        if dil == 1:
            args = (qk, qk, proj, do, lse, delta, (gi, N_GROUPS + gi, 2 * N_GROUPS + gi))
        else:
            args = (_group_view(qk, gi * GROUP_WIDTH, dil), _group_view(qk, att_w + gi * GROUP_WIDTH, dil),
                    _group_view(proj, 2 * att_w + gi * GROUP_WIDTH, dil), _group_view(do, 0, dil),
                    _group_view(lse, 0, dil), _group_view(delta, 0, dil), (0, 0, 0))
        dq = _att_bwd_dq(*args, dil, f"att_bwd_dq{gi}")
        dk, dv = _att_bwd_dkv(*args, dil, f"att_bwd_dkv{gi}")
        dqs.append(dq.reshape(t, GROUP_WIDTH))
        dks.append(dk.reshape(t, GROUP_WIDTH))
        dvs_.append(dv.reshape(t, GROUP_WIDTH))
    c, s_up, s_dn = tables
    dqk = _rope(jnp.concatenate(dqs + dks, axis=1), 2 * att_w, (c, -s_up, -s_dn), "mix_rope_bwd")

    g['w_sg_out'] = _matmul([(su, dy_sg)], 'tn', BF16, "mix_sg_out_dw", out3=s)
    dsu = _matmul([(dy_sg, w['w_sg_out'])], 'nt', BF16, "mix_sg_out_dx", b3=True)
    du, dvs, g_sg_w, g_bias, g_lg, g_lb = _sg_bwd(proj, u_blk, vs_blk, dsu, small['sg_ln_g'], small['sg_ln_b'],
                                                   small['sg_w'], bias, "sg_bwd")
    gs = {'sg_w': g_sg_w, 'sg_b': g_bias[:, ::SG_GROUP_DIM].T, 'sg_ln_g': g_lg, 'sg_ln_b': g_lb}

    dproj = jnp.concatenate([dqk] + dvs_ + [du, dvs, dg_att, dg_sg], axis=1)
    g['w_in'] = _matmul([(n, dproj)], 'tn', BF16, "mix_in_dw", out3=s, caps=(1024, 1024, 1024))
    dn = _matmul([(dproj, w['w_in'])], 'nt', F32, "mix_in_dx", b3=True, caps=(1024, 1024, 512))
    dx, dx_b, gs['mix_norm'] = _rmsnorm_bwd(x, small['mix_norm'], dn, dx_next, "mix_norm_bwd")
    return dx, dx_b, g, gs


def _local_step(x, target, w, small):
    x1, s1 = _ffn_forward(x, small['ffn1_norm'], w['ffn1_w_gate'], w['ffn1_w_up'], w['ffn1_w_down'], "ffn1")
    x2, s2 = _mixer_forward(x1, w, small)
    x3, s3 = _ffn_forward(x2, small['ffn2_norm'], w['ffn2_w_gate'], w['ffn2_w_up'], w['ffn2_w_down'], "ffn2")
    loss, dx3, dx3_b, g_final = _final_loss(x3, small['final_norm'], target, "final_loss")
    g, gs = {}, {'final_norm': g_final}
    dx2, dx2_b, gs['ffn2_norm'], g['ffn2_w_gate'], g['ffn2_w_up'], g['ffn2_w_down'] = _ffn_backward(
        x2, small['ffn2_norm'], w['ffn2_w_gate'], w['ffn2_w_up'], w['ffn2_w_down'], s3, dx3, dx3_b, "ffn2")
    dx1, dx1_b, g_mix, gs_mix = _mixer_backward(x1, w, small, s2, dx2, dx2_b)
    g.update(g_mix)
    gs.update(gs_mix)
    dx0, _, gs['ffn1_norm'], g['ffn1_w_gate'], g['ffn1_w_up'], g['ffn1_w_down'] = _ffn_backward(
        x, small['ffn1_norm'], w['ffn1_w_gate'], w['ffn1_w_up'], w['ffn1_w_down'], s1, dx1, dx1_b, "ffn1")
    return loss, dx0, g, gs


def _place():
    x, y, c = lax.axis_index("x"), lax.axis_index("y"), lax.axis_index("c")
    others = [(1 - x, y), (x, 1 - y), (1 - x, 1 - y)]
    return x, y, c, others


def _cast_into_gathered(wt, p_idx, name):
    r, ccols = wt.shape[0] // 2, wt.shape[1]
    tm = _rows(r, 256)
    nb = r // tm

    def body(p_ref, w_ref, o_ref):
        o_ref[...] = w_ref[...].astype(BF16)

    grid_spec = pltpu.PrefetchScalarGridSpec(
        num_scalar_prefetch=1, grid=(2, nb),
        in_specs=[pl.BlockSpec((tm, ccols), lambda h, i, pr: (h * nb + i, 0))],
        out_specs=pl.BlockSpec((None, None, tm, ccols), lambda h, i, pr: (pr[0], h, i, 0)))
    return pl.pallas_call(body, out_shape=jax.ShapeDtypeStruct((N_CHIPS, 2, r, ccols), BF16), grid_spec=grid_spec,
                          compiler_params=_cparams(("parallel", "parallel"), 2 * _nbytes((tm, ccols), F32)), name=name)(p_idx, wt)


def _all_gather(bufs):
    nw = len(bufs)

    def body(*refs):
        dst = refs[nw:2 * nw]
        send_sems, recv_sems, fwd_send_sems, fwd_recv_sems = refs[2 * nw:]
        x, y, c, others = _place()
        me = 2 * x + y
        sends, fwds = [], []
        for i in range(nw):
            for j, (ox, oy) in enumerate(others):
                mine = dst[i].at[me, c]
                cp = pltpu.make_async_remote_copy(mine, mine, send_sems.at[i, j], recv_sems.at[i, j],
                                                  device_id=(ox, oy, c), device_id_type=MESH)
                cp.start()
                sends.append(cp)
        for i in range(nw):
            for j, (ox, oy) in enumerate(others):
                got = dst[i].at[2 * ox + oy, c]
                pltpu.make_async_remote_copy(got, got, send_sems.at[i, j], recv_sems.at[i, j],
                                             device_id=(ox, oy, c), device_id_type=MESH).wait_recv()
                cp = pltpu.make_async_remote_copy(got, got, fwd_send_sems.at[i, j], fwd_recv_sems.at[i, j],
                                                  device_id=(x, y, 1 - c), device_id_type=MESH)
                cp.start()
                fwds.append(cp)
        for i in range(nw):
            for j, (ox, oy) in enumerate(others):
                got = dst[i].at[2 * ox + oy, 1 - c]
                pltpu.make_async_remote_copy(got, got, fwd_send_sems.at[i, j], fwd_recv_sems.at[i, j],
                                             device_id=(x, y, 1 - c), device_id_type=MESH).wait_recv()
        for cp in sends + fwds:
            cp.wait_send()

    any_spec = pl.BlockSpec(memory_space=pl.ANY)
    return pl.pallas_call(
        body, out_shape=[jax.ShapeDtypeStruct(b.shape, b.dtype) for b in bufs],
        in_specs=[any_spec] * nw, out_specs=[any_spec] * nw, input_output_aliases={i: i for i in range(nw)},
        scratch_shapes=[pltpu.SemaphoreType.DMA((nw, 3)), pltpu.SemaphoreType.DMA((nw, 3)),
                        pltpu.SemaphoreType.DMA((nw, 3)), pltpu.SemaphoreType.DMA((nw, 3))],
        compiler_params=pltpu.CompilerParams(has_side_effects=True), name="all_gather_weights")(*bufs)


def _sibling_exchange(grads):
    nw = len(grads)

    def body(*refs):
        src, dst = refs[:nw], refs[nw:2 * nw]
        send_sems, recv_sems = refs[2 * nw:]
        x, y, c, _ = _place()
        cps = []
        for i in range(nw):
            cp = pltpu.make_async_remote_copy(src[i].at[:, 1 - c], dst[i], send_sems.at[i], recv_sems.at[i],
                                              device_id=(x, y, 1 - c), device_id_type=MESH)
            cp.start()
            cps.append(cp)
        for cp in cps:
            cp.wait()

    any_spec = pl.BlockSpec(memory_space=pl.ANY)
    return pl.pallas_call(
        body, out_shape=[jax.ShapeDtypeStruct((g.shape[0],) + g.shape[2:], g.dtype) for g in grads],
        in_specs=[any_spec] * nw, out_specs=[any_spec] * nw,
        scratch_shapes=[pltpu.SemaphoreType.DMA((nw,)), pltpu.SemaphoreType.DMA((nw,))],
        compiler_params=pltpu.CompilerParams(has_side_effects=True), name="rs_sibling")(*grads)


def _chip_exchange(parts):
    nw = len(parts)

    def body(*refs):
        src, dst = refs[:nw], refs[nw:2 * nw]
        send_sems, recv_sems = refs[2 * nw:]
        x, y, c, others = _place()
        cps = []
        for i in range(nw):
            for j, (ox, oy) in enumerate(others):
                cp = pltpu.make_async_remote_copy(src[i].at[2 * ox + oy], dst[i].at[j], send_sems.at[i, j], recv_sems.at[i, j],
                                                  device_id=(ox, oy, c), device_id_type=MESH)
                cp.start()
                cps.append(cp)
        for cp in cps:
            cp.wait()

    any_spec = pl.BlockSpec(memory_space=pl.ANY)
    return pl.pallas_call(
        body, out_shape=[jax.ShapeDtypeStruct((3,) + p.shape[1:], p.dtype) for p in parts],
        in_specs=[any_spec] * nw, out_specs=[any_spec] * nw,
        scratch_shapes=[pltpu.SemaphoreType.DMA((nw, 3)), pltpu.SemaphoreType.DMA((nw, 3))],
        compiler_params=pltpu.CompilerParams(has_side_effects=True), name="rs_chips")(*parts)


def _half_exchange(bufs):
    nw = len(bufs)

    def body(*refs):
        dst = refs[nw:2 * nw]
        send_sems, recv_sems = refs[2 * nw:]
        x, y, c, _ = _place()
        cps = []
        for i in range(nw):
            mine = dst[i].at[c]
            cp = pltpu.make_async_remote_copy(mine, mine, send_sems.at[i], recv_sems.at[i],
                                              device_id=(x, y, 1 - c), device_id_type=MESH)
            cp.start()
            cps.append(cp)
        for i, cp in enumerate(cps):
            cp.wait_send()
            theirs = dst[i].at[1 - c]
            pltpu.make_async_remote_copy(theirs, theirs, send_sems.at[i], recv_sems.at[i],
                                         device_id=(x, y, 1 - c), device_id_type=MESH).wait_recv()

    any_spec = pl.BlockSpec(memory_space=pl.ANY)
    return pl.pallas_call(
        body, out_shape=[jax.ShapeDtypeStruct(b.shape, b.dtype) for b in bufs],
        in_specs=[any_spec] * nw, out_specs=[any_spec] * nw, input_output_aliases={i: i for i in range(nw)},
        scratch_shapes=[pltpu.SemaphoreType.DMA((nw,)), pltpu.SemaphoreType.DMA((nw,))],
        compiler_params=pltpu.CompilerParams(has_side_effects=True), name="rs_halves")(*bufs)


def _sibling_sum(grad, recv, c_idx, name):
    s, _, r, ccols = grad.shape
    tm = _rows(r, 256)

    def body(c_ref, g_ref, r_ref, o_ref):
        o_ref[...] = (g_ref[...].astype(F32) + r_ref[...].astype(F32)).astype(BF16)

    grid_spec = pltpu.PrefetchScalarGridSpec(
        num_scalar_prefetch=1, grid=(s, r // tm),
        in_specs=[pl.BlockSpec((None, None, tm, ccols), lambda q, i, cr: (q, cr[0], i, 0)),
                  pl.BlockSpec((None, tm, ccols), lambda q, i, cr: (q, i, 0))],
        out_specs=pl.BlockSpec((None, tm, ccols), lambda q, i, cr: (q, i, 0)))
    return pl.pallas_call(body, out_shape=jax.ShapeDtypeStruct((s, r, ccols), BF16), grid_spec=grid_spec,
                          compiler_params=_cparams(("parallel", "parallel"), 4 * _nbytes((tm, ccols), F32)), name=name)(c_idx, grad, recv)


def _chip_sum(part, recv, pc_idx, name):
    _, r, ccols = part.shape
    tm = _rows(r, 256)

    def body(pc_ref, own_ref, r0_ref, r1_ref, r2_ref, o_ref):
        acc = own_ref[...].astype(F32) + r0_ref[...].astype(F32)
        acc = acc + r1_ref[...].astype(F32)
        o_ref[...] = acc + r2_ref[...].astype(F32)

    def slot(j):
        return pl.BlockSpec((None, tm, ccols), lambda i, pc: (j, i, 0))

    grid_spec = pltpu.PrefetchScalarGridSpec(
        num_scalar_prefetch=1, grid=(r // tm,),
        in_specs=[pl.BlockSpec((None, tm, ccols), lambda i, pc: (pc[0], i, 0)), slot(0), slot(1), slot(2)],
        out_specs=pl.BlockSpec((None, tm, ccols), lambda i, pc: (pc[1], i, 0)))
    return pl.pallas_call(body, out_shape=jax.ShapeDtypeStruct((2, r, ccols), F32), grid_spec=grid_spec,
                          compiler_params=_cparams(("parallel",), 6 * _nbytes((tm, ccols), F32)), name=name)(pc_idx, part, recv, recv, recv)


def _all_reduce_small(vec):
    r = vec.shape[0]

    def body(v_ref, o_ref, slots, send_sems, recv_sems):
        x, y, c, _ = _place()
        me = 4 * x + 2 * y + c
        slots[me] = v_ref[...]
        cps = []
        for k in range(1, N_DEV):
            peer = (1 - x if k & 4 else x, 1 - y if k & 2 else y, 1 - c if k & 1 else c)
            cp = pltpu.make_async_remote_copy(slots.at[me], slots.at[me], send_sems.at[k - 1], recv_sems.at[k - 1],
                                              device_id=peer, device_id_type=MESH)
            cp.start()
            cps.append(cp)
        for cp in cps:
            cp.wait()
        acc = slots[0]
        for dev in range(1, N_DEV):
            acc = acc + slots[dev]
        o_ref[...] = acc

    vm = pl.BlockSpec(memory_space=pltpu.VMEM)
    return pl.pallas_call(
        body, out_shape=jax.ShapeDtypeStruct((r, LANES), F32), in_specs=[vm], out_specs=vm,
        scratch_shapes=[pltpu.VMEM((N_DEV, r, LANES), F32), pltpu.SemaphoreType.DMA((N_DEV - 1,)),
                        pltpu.SemaphoreType.DMA((N_DEV - 1,))],
        compiler_params=pltpu.CompilerParams(vmem_limit_bytes=int(4 * _nbytes((N_DEV, r, LANES), F32))),
        name="all_reduce_small")(vec)


def _adamw(wt, g, m, v, name):
    r, ccols = wt.shape
    tm = _rows(r, max(8, (MIB // (4 * ccols)) // 8 * 8))
    blk = pl.BlockSpec((tm, ccols), lambda i: (i, 0))

    def body(w_ref, g_ref, m_ref, v_ref, d_ref, mo_ref, vo_ref):
        gv = g_ref[...]
        mv = ADAM_B1 * m_ref[...] + (1.0 - ADAM_B1) * gv
        vv = ADAM_B2 * v_ref[...] + (1.0 - ADAM_B2) * (gv * gv)
        m_hat = mv / (1.0 - ADAM_B1 ** ADAM_STEP)
        v_hat = vv / (1.0 - ADAM_B2 ** ADAM_STEP)
        d_ref[...] = -ADAM_LR * (m_hat / (jnp.sqrt(v_hat) + ADAM_EPS) + ADAM_WD * w_ref[...])
        mo_ref[...] = mv
        vo_ref[...] = vv

    out = jax.ShapeDtypeStruct((r, ccols), F32)
    return pl.pallas_call(body, out_shape=(out, out, out), grid=(r // tm,), in_specs=[blk] * 4, out_specs=(blk, blk, blk),
                          compiler_params=_cparams(("parallel",), 7 * _nbytes((tm, ccols), F32)), name=name)(wt, g, m, v)


def _as_rows(a):
    rows = a.reshape(-1, LANES)
    return jnp.pad(rows, ((0, -rows.shape[0] % 8), (0, 0)))


def kernel(x, ffn1_norm, ffn1_w_gate, ffn1_w_up, ffn1_w_down, mix_norm, w_in, sg_ln_g, sg_ln_b, sg_w, sg_b, w_att_out, w_sg_out, w_out, ffn2_norm, ffn2_w_gate, ffn2_w_up, ffn2_w_down, final_norm, loss_target, m_ffn1_norm, m_ffn1_w_gate, m_ffn1_w_up, m_ffn1_w_down, m_mix_norm, m_w_in, m_sg_ln_g, m_sg_ln_b, m_sg_w, m_sg_b, m_w_att_out, m_w_sg_out, m_w_out, m_ffn2_norm, m_ffn2_w_gate, m_ffn2_w_up, m_ffn2_w_down, m_final_norm, v_ffn1_norm, v_ffn1_w_gate, v_ffn1_w_up, v_ffn1_w_down, v_mix_norm, v_w_in, v_sg_ln_g, v_sg_ln_b, v_sg_w, v_sg_b, v_w_att_out, v_w_sg_out, v_w_out, v_ffn2_norm, v_ffn2_w_gate, v_ffn2_w_up, v_ffn2_w_down, v_final_norm):
    given = dict(locals())
    wts = {n: given[n] for n in WEIGHT_NAMES}
    ms = {n: given["m_" + n] for n in WEIGHT_NAMES}
    vs = {n: given["v_" + n] for n in WEIGHT_NAMES}
    t, d = x.shape[-2], x.shape[-1]
    xc, yc, cc = lax.axis_index("x"), lax.axis_index("y"), lax.axis_index("c")

    shard2d = {n: wts[n].reshape(wts[n].shape[-2:]) for n in BIG_NAMES}
    p_idx = jnp.reshape(2 * xc + yc, (1,)).astype(jnp.int32)
    pc_idx = jnp.stack([2 * xc + yc, cc]).astype(jnp.int32)
    gathered = _all_gather([_cast_into_gathered(shard2d[n], p_idx, f"cast_{n}") for n in BIG_NAMES])
    w = {}
    for (n, axis), full in zip(BIG, gathered):
        _, _, r, ccols = full.shape
        w[n] = full.reshape(N_CHIPS, 2 * r, ccols) if axis == 1 else full.reshape(N_CHIPS * 2 * r, ccols)

    small = {n: wts[n].reshape(-1, wts[n].shape[-1]) for n in SMALL_NAMES}
    small['sg_w'] = wts['sg_w'].reshape(wts['sg_w'].shape[-3:])
    loss, dx, g, gs = _local_step(x.reshape(t, d), loss_target.reshape(t, d), w, small)
    loss = lax.psum(loss[0, 0], ("x", "y", "c"))

    g4 = [g[n].reshape(gathered[i].shape) for i, n in enumerate(BIG_NAMES)]
    c_idx = jnp.reshape(cc, (1,)).astype(jnp.int32)
    from_sibling = _sibling_exchange(g4)
    chip_parts = [_sibling_sum(a, b, c_idx, f"rs_sum1_{n}") for a, b, n in zip(g4, from_sibling, BIG_NAMES)]
    from_chips = _chip_exchange(chip_parts)
    my_halves = [_chip_sum(a, b, pc_idx, f"rs_sum2_{n}") for a, b, n in zip(chip_parts, from_chips, BIG_NAMES)]
    reduced = _half_exchange(my_halves)
    grads = {n: r.reshape(shard2d[n].shape) for n, r in zip(BIG_NAMES, reduced)}

    def pack(tree):
        return jnp.concatenate([_as_rows(tree[n]) for n in SMALL_NAMES], axis=0)

    packed = _all_reduce_small(pack(gs))

    delta, new_m, new_v = {}, {}, {}
    for n in BIG_NAMES:
        shape = wts[n].shape
        dl, mm, vv = _adamw(shard2d[n], grads[n], ms[n].reshape(shard2d[n].shape), vs[n].reshape(shard2d[n].shape), f"adamw_{n}")
        grads[n], delta[n], new_m[n], new_v[n] = grads[n].reshape(shape), dl.reshape(shape), mm.reshape(shape), vv.reshape(shape)

    small_out = (packed,) + _adamw(pack(wts), packed, pack(ms), pack(vs), "adamw_small")
    row = 0
    for n in SMALL_NAMES:
        shape = wts[n].shape
        sz = wts[n].size // LANES
        grads[n], delta[n], new_m[n], new_v[n] = (a[row:row + sz].reshape(shape) for a in small_out)
        row += sz + -sz % 8

    return (loss, dx.reshape(x.shape), *[grads[n] for n in WEIGHT_NAMES], *[delta[n] for n in WEIGHT_NAMES],
            *[new_m[n] for n in WEIGHT_NAMES], *[new_v[n] for n in WEIGHT_NAMES])
```

```python
import functools

import jax
import jax.numpy as jnp
from jax import lax
from jax.experimental import pallas as pl
from jax.experimental.pallas import tpu as pltpu

F32 = jnp.float32
BF16 = jnp.bfloat16
MESH = pl.DeviceIdType.MESH

NORM_EPS = 1e-6
LN_EPS = 1e-5
HEAD_DIM = 128
HEADS_PER_GROUP = 4
GROUP_WIDTH = HEADS_PER_GROUP * HEAD_DIM
DILATIONS = (1, 4, 16)
N_GROUPS = len(DILATIONS)
ATT_BLOCK = 128
ROPE_DIM = HEAD_DIM // 4
ROPE_THETA = 500000.0
SG_CHUNK = 128
SG_GROUPS = 12
SG_GROUP_DIM = 128
MASKED = -1e30

ADAM_LR = 0.001
ADAM_B1 = 0.9
ADAM_B2 = 0.999
ADAM_EPS = 1e-08
ADAM_WD = 0.01
ADAM_STEP = 10

N_CHIPS = 4
N_DEV = 8
LANES = 128
MIB = 2 ** 20
VMEM_BYTES_V7X = 64 * MIB

WEIGHT_NAMES = ['ffn1_norm', 'ffn1_w_gate', 'ffn1_w_up', 'ffn1_w_down', 'mix_norm', 'w_in', 'sg_ln_g', 'sg_ln_b',
                'sg_w', 'sg_b', 'w_att_out', 'w_sg_out', 'w_out', 'ffn2_norm', 'ffn2_w_gate', 'ffn2_w_up',
                'ffn2_w_down', 'final_norm']
BIG = [('ffn1_w_gate', 1), ('ffn1_w_up', 1), ('ffn1_w_down', 0), ('w_in', 1), ('w_att_out', 1), ('w_sg_out', 1),
       ('w_out', 0), ('ffn2_w_gate', 1), ('ffn2_w_up', 1), ('ffn2_w_down', 0)]
BIG_NAMES = [n for n, _ in BIG]
SMALL_NAMES = [n for n in WEIGHT_NAMES if n not in BIG_NAMES]


def _nbytes(shape, dtype):
    n = jnp.dtype(dtype).itemsize
    for s in shape:
        if s is not None:
            n *= s
    return n


def _cparams(sem, block_bytes, **kw):
    limit = int(min(max(3 * block_bytes, 32 * MIB), VMEM_BYTES_V7X - 8 * MIB))
    return pltpu.CompilerParams(dimension_semantics=sem, vmem_limit_bytes=limit, **kw)


def _tile(dim, cap):
    best = None
    for t in range(LANES, min(dim, cap) + 1, LANES):
        if dim % t == 0:
            best = t
    if best is None:
        assert dim <= cap, (dim, cap)
        return dim
    return best


def _rows(dim, cap):
    best = None
    for t in range(8, min(dim, cap) + 1, 8):
        if dim % t == 0:
            best = t
    assert best is not None, (dim, cap)
    return best


def _place():
    x, y, c = lax.axis_index("x"), lax.axis_index("y"), lax.axis_index("c")
    others = [(1 - x, y), (x, 1 - y), (1 - x, 1 - y)]
    return x, y, c, others


class _Rider:
    def __init__(self, operands, out_shapes, aliases, sems, start, finish):
        self.operands = operands
        self.out_shapes = out_shapes
        self.aliases = aliases
        self.sems = sems
        self.start = start
        self.finish = finish


def _run(body, *, name, grid, in_specs, out_specs, out_shape, scratch_shapes, operands, block_bytes, rider=None):
    if rider is None:
        sem = ("parallel",) * (len(grid) - 1) + ("arbitrary",)
        return pl.pallas_call(body, out_shape=out_shape, grid=grid, in_specs=in_specs, out_specs=out_specs,
                              scratch_shapes=scratch_shapes, compiler_params=_cparams(sem, block_bytes), name=name)(*operands)
    n_in, n_out, n_scr = len(operands), len(out_shape), len(scratch_shapes)
    r_in, r_out = len(rider.operands), len(rider.out_shapes)
    any_spec = pl.BlockSpec(memory_space=pl.ANY)

    def wrapped(*refs):
        ins, refs = refs[:n_in], refs[n_in:]
        r_ins, refs = refs[:r_in], refs[r_in:]
        outs, refs = refs[:n_out], refs[n_out:]
        r_outs, refs = refs[:r_out], refs[r_out:]
        scr, sems = refs[:n_scr], refs[n_scr:]
        if not grid:
            rider.start(r_ins, r_outs, sems)
            rider.finish(r_ins, r_outs, sems)
            return
        ids = [pl.program_id(a) for a in range(len(grid))]
        first = functools.reduce(jnp.logical_and, [i == 0 for i in ids])
        last = functools.reduce(jnp.logical_and, [i == g - 1 for i, g in zip(ids, grid)])

        @pl.when(first)
        def _():
            rider.start(r_ins, r_outs, sems)

        body(*ins, *outs, *scr)

        @pl.when(last)
        def _():
            rider.finish(r_ins, r_outs, sems)

    results = pl.pallas_call(
        wrapped, out_shape=list(out_shape) + list(rider.out_shapes), grid=grid,
        in_specs=list(in_specs) + [any_spec] * r_in, out_specs=list(out_specs) + [any_spec] * r_out,
        scratch_shapes=list(scratch_shapes) + list(rider.sems),
        input_output_aliases={n_in + k: n_out + v for k, v in rider.aliases.items()},
        compiler_params=_cparams(("arbitrary",) * len(grid) if grid else None, block_bytes, has_side_effects=True),
        name=name)(*operands, *rider.operands)
    return results[:n_out], results[n_out:]


def _exchange(rider, name):
    return _run(None, name=name, grid=(), in_specs=[], out_specs=[], out_shape=[], scratch_shapes=[], operands=[],
                block_bytes=0, rider=rider)[1]


def _gather_rider(items):
    bufs, index = [], []
    for b, r0, r1 in items:
        if not any(b is q for q in bufs):
            bufs.append(b)
        index.append(([k for k, q in enumerate(bufs) if q is b][0], r0, r1))
    n = len(index)

    def piece(refs, k, chip, half):
        bi, r0, r1 = index[k]
        return refs[bi].at[chip, half, pl.ds(r0, r1 - r0)]

    def copy(ref, sem_pair, k, j, to):
        return pltpu.make_async_remote_copy(ref, ref, sem_pair[0].at[k, j], sem_pair[1].at[k, j], device_id=to, device_id_type=MESH)

    def start(r_ins, buf, sems):
        x, y, c, others = _place()
        for k in range(n):
            for j, (ox, oy) in enumerate(others):
                copy(piece(buf, k, 2 * x + y, c), sems[:2], k, j, (ox, oy, c)).start()

    def finish(r_ins, buf, sems):
        x, y, c, others = _place()
        for k in range(n):
            for j, (ox, oy) in enumerate(others):
                got = piece(buf, k, 2 * ox + oy, c)
                copy(got, sems[:2], k, j, (ox, oy, c)).wait_recv()
                copy(got, sems[2:], k, j, (x, y, 1 - c)).start()
        for k in range(n):
            for j, (ox, oy) in enumerate(others):
                copy(piece(buf, k, 2 * ox + oy, 1 - c), sems[2:], k, j, (x, y, 1 - c)).wait_recv()
        for k in range(n):
            for j, (ox, oy) in enumerate(others):
                copy(piece(buf, k, 2 * x + y, c), sems[:2], k, j, (ox, oy, c)).wait_send()
                copy(piece(buf, k, 2 * ox + oy, c), sems[2:], k, j, (x, y, 1 - c)).wait_send()

    return _Rider(bufs, [jax.ShapeDtypeStruct(b.shape, b.dtype) for b in bufs], {i: i for i in range(len(bufs))},
                  [pltpu.SemaphoreType.DMA((n, 3))] * 4, start, finish)


def _scatter_rider(parts):
    n = len(parts)

    def copy(src, dst, sems, i, j, to):
        return pltpu.make_async_remote_copy(src, dst, sems[0].at[i, j], sems[1].at[i, j], device_id=to, device_id_type=MESH)

    def start(src, dst, sems):
        x, y, c, others = _place()
        for i in range(n):
            for j, (ox, oy) in enumerate(others):
                copy(src[i].at[2 * ox + oy], dst[i].at[j], sems, i, j, (ox, oy, c)).start()

    def finish(src, dst, sems):
        x, y, c, others = _place()
        for i in range(n):
            for j, (ox, oy) in enumerate(others):
                copy(src[i].at[2 * ox + oy], dst[i].at[j], sems, i, j, (ox, oy, c)).wait()

    return _Rider(parts, [jax.ShapeDtypeStruct((3,) + p.shape[1:], p.dtype) for p in parts], {},
                  [pltpu.SemaphoreType.DMA((n, 3))] * 2, start, finish)


def _matmul(pairs, mode, out_dtype, name, *, scale=1.0, residual=None, b3=False, out3=0, caps=(1024, 1024, 512), rider=None):
    a0, b0 = pairs[0]
    if mode == 'nn':
        m, k = a0.shape
        n = b0.shape[0] * b0.shape[2] if b3 else b0.shape[1]
    elif mode == 'nt':
        m = a0.shape[0]
        n, k = (b0.shape[1], b0.shape[0] * b0.shape[2]) if b3 else b0.shape
    else:
        k, m = a0.shape
        n = b0.shape[1]
    tm = _tile(m, caps[0])
    tn = _tile(n, caps[1])
    tk = _tile(k, caps[2])
    if b3 and mode == 'nn':
        tn = b0.shape[2]
    if b3 and mode == 'nt':
        tk = b0.shape[2]
    if out3:
        tn = n // out3
    nk = k // tk
    if mode == 'tn':
        a_spec = pl.BlockSpec((tk, tm), lambda i, j, kk: (kk, i))
        b_spec = pl.BlockSpec((tk, tn), lambda i, j, kk: (kk, j))
        dims = ((0,), (0,))
    elif mode == 'nn':
        a_spec = pl.BlockSpec((tm, tk), lambda i, j, kk: (i, kk))
        b_spec = (pl.BlockSpec((None, tk, tn), lambda i, j, kk: (j, kk, 0)) if b3
                  else pl.BlockSpec((tk, tn), lambda i, j, kk: (kk, j)))
        dims = ((1,), (0,))
    else:
        a_spec = pl.BlockSpec((tm, tk), lambda i, j, kk: (i, kk))
        b_spec = (pl.BlockSpec((None, tn, tk), lambda i, j, kk: (kk, j, 0)) if b3
                  else pl.BlockSpec((tn, tk), lambda i, j, kk: (j, kk)))
        dims = ((1,), (1,))
    in_specs, operands = [], []
    for a, b in pairs:
        in_specs += [a_spec, b_spec]
        operands += [a, b]
    block_bytes = len(pairs) * (_nbytes((tm, tk), a0.dtype) + _nbytes((tk, tn), b0.dtype))
    if residual is not None:
        in_specs.append(pl.BlockSpec((tm, tn), lambda i, j, kk: (i, j)))
        operands.append(residual)
        block_bytes += _nbytes((tm, tn), F32)
    if out3:
        out_spec = pl.BlockSpec((None, tm, tn), lambda i, j, kk: (j, i, 0))
        out_shape = jax.ShapeDtypeStruct((out3, m, tn), out_dtype)
    else:
        out_spec = pl.BlockSpec((tm, tn), lambda i, j, kk: (i, j))
        out_shape = jax.ShapeDtypeStruct((m, n), out_dtype)
    block_bytes += _nbytes((tm, tn), out_dtype) + _nbytes((tm, tn), F32)
    n_pairs = len(pairs)
    has_res = residual is not None

    def body(*refs):
        o_ref, acc = refs[-2], refs[-1]
        kk = pl.program_id(2)

        @pl.when(kk == 0)
        def _():
            acc[...] = jnp.zeros_like(acc)

        part = None
        for p in range(n_pairs):
            d = lax.dot_general(refs[2 * p][...].astype(BF16), refs[2 * p + 1][...].astype(BF16),
                                (dims, ((), ())), preferred_element_type=F32)
            part = d if part is None else part + d
        acc[...] += part

        @pl.when(kk == nk - 1)
        def _():
            r = acc[...]
            if scale != 1.0:
                r = r * scale
            if has_res:
                r = refs[2 * n_pairs][...] + r
            o_ref[...] = r.astype(out_dtype)

    res = _run(body, name=name, grid=(m // tm, n // tn, nk), in_specs=in_specs, out_specs=[out_spec], out_shape=[out_shape],
               scratch_shapes=[pltpu.VMEM((tm, tn), F32)], operands=operands, block_bytes=block_bytes, rider=rider)
    return res[0] if rider is None else (res[0][0], res[1])


def _rmsnorm_fwd(x, g, name):
    t, d = x.shape
    tm = _rows(t, 512)

    def body(x_ref, g_ref, o_ref):
        xv = x_ref[...]
        r = lax.rsqrt(jnp.mean(xv * xv, axis=1, keepdims=True) + NORM_EPS)
        o_ref[...] = (xv * r * g_ref[...]).astype(BF16)

    row = pl.BlockSpec((tm, d), lambda i: (i, 0))
    return pl.pallas_call(
        body, out_shape=jax.ShapeDtypeStruct((t, d), BF16), grid=(t // tm,),
        in_specs=[row, pl.BlockSpec((1, d), lambda i: (0, 0))], out_specs=row,
        compiler_params=_cparams(("parallel",), 2 * _nbytes((tm, d), F32)), name=name)(x, g)


def _rms_grad(xv, g, dn, d):
    r = lax.rsqrt(jnp.mean(xv * xv, axis=1, keepdims=True) + NORM_EPS)
    u = dn * g
    s = jnp.sum(xv * u, axis=1, keepdims=True)
    dx = r * u - xv * (r * r * r) * (s * (1.0 / d))
    return dx, dn * xv * r


def _rmsnorm_bwd(x, g, dn, dres, name):
    t, d = x.shape
    tm = _rows(t, 256)

    def body(x_ref, g_ref, dn_ref, dres_ref, dx_ref, dxb_ref, dg_ref):
        dx, dg_rows = _rms_grad(x_ref[...], g_ref[...], dn_ref[...].astype(F32), d)
        dx = dres_ref[...] + dx
        dx_ref[...] = dx
        dxb_ref[...] = dx.astype(BF16)

        @pl.when(pl.program_id(0) == 0)
        def _():
            dg_ref[...] = jnp.zeros_like(dg_ref)

        dg_ref[...] += jnp.sum(dg_rows, axis=0, keepdims=True)

    row = pl.BlockSpec((tm, d), lambda i: (i, 0))
    vec = pl.BlockSpec((1, d), lambda i: (0, 0))
    return pl.pallas_call(
        body, out_shape=(jax.ShapeDtypeStruct((t, d), F32), jax.ShapeDtypeStruct((t, d), BF16), jax.ShapeDtypeStruct((1, d), F32)),
        grid=(t // tm,), in_specs=[row, vec, row, row], out_specs=(row, row, vec),
        compiler_params=_cparams(("arbitrary",), 5 * _nbytes((tm, d), F32)), name=name)(x, g, dn, dres)


def _final_loss(x, g, target, name):
    t, d = x.shape
    tm = _rows(t, 256)

    def body(x_ref, g_ref, t_ref, loss_ref, dx_ref, dxb_ref, dg_ref):
        xv, gv = x_ref[...], g_ref[...]
        r = lax.rsqrt(jnp.mean(xv * xv, axis=1, keepdims=True) + NORM_EPS)
        err = xv * r * gv - t_ref[...]
        dx, dg_rows = _rms_grad(xv, gv, err * (1.0 / d), d)
        dx_ref[...] = dx
        dxb_ref[...] = dx.astype(BF16)

        @pl.when(pl.program_id(0) == 0)
        def _():
            dg_ref[...] = jnp.zeros_like(dg_ref)
            loss_ref[...] = jnp.zeros_like(loss_ref)

        dg_ref[...] += jnp.sum(dg_rows, axis=0, keepdims=True)
        row_loss = jnp.sum(err * err, axis=1, keepdims=True) * (0.5 / d)
        loss_ref[...] += jnp.sum(row_loss, axis=0, keepdims=True)

    row = pl.BlockSpec((tm, d), lambda i: (i, 0))
    vec = pl.BlockSpec((1, d), lambda i: (0, 0))
    return pl.pallas_call(
        body, out_shape=(jax.ShapeDtypeStruct((1, 1), F32), jax.ShapeDtypeStruct((t, d), F32),
                         jax.ShapeDtypeStruct((t, d), BF16), jax.ShapeDtypeStruct((1, d), F32)),
        grid=(t // tm,), in_specs=[row, vec, row], out_specs=(pl.BlockSpec((1, 1), lambda i: (0, 0)), row, row, vec),
        compiler_params=_cparams(("arbitrary",), 4 * _nbytes((tm, d), F32)), name=name)(x, g, target)


def _ffn_up(n, wg, wu, name, rider=None):
    t, d = n.shape
    s, _, f = wg.shape
    tm, tk = _tile(t, 1024), _tile(d, 512)
    nk = d // tk

    def body(n_ref, wg_ref, wu_ref, a_ref, b_ref, h_ref, acc_g, acc_u):
        kk = pl.program_id(2)

        @pl.when(kk == 0)
        def _():
            acc_g[...] = jnp.zeros_like(acc_g)
            acc_u[...] = jnp.zeros_like(acc_u)

        nv = n_ref[...]
        acc_g[...] += jnp.dot(nv, wg_ref[...], preferred_element_type=F32)
        acc_u[...] += jnp.dot(nv, wu_ref[...], preferred_element_type=F32)

        @pl.when(kk == nk - 1)
        def _():
            a, b = acc_g[...], acc_u[...]
            a_ref[...] = a.astype(BF16)
            b_ref[...] = b.astype(BF16)
            h_ref[...] = (a * jax.nn.sigmoid(a) * b).astype(BF16)

    w_spec = pl.BlockSpec((None, tk, f), lambda i, j, kk: (j, kk, 0))
    o_spec = pl.BlockSpec((tm, f), lambda i, j, kk: (i, j))
    out = jax.ShapeDtypeStruct((t, s * f), BF16)
    block_bytes = _nbytes((tm, tk), BF16) + 2 * _nbytes((tk, f), BF16) + 3 * _nbytes((tm, f), BF16) + 2 * _nbytes((tm, f), F32)
    return _run(body, name=name, grid=(t // tm, s, nk),
                in_specs=[pl.BlockSpec((tm, tk), lambda i, j, kk: (i, kk)), w_spec, w_spec], out_specs=[o_spec, o_spec, o_spec],
                out_shape=[out, out, out], scratch_shapes=[pltpu.VMEM((tm, f), F32), pltpu.VMEM((tm, f), F32)],
                operands=[n, wg, wu], block_bytes=block_bytes, rider=rider)


def _ffn_bwd_act(dx, wd, a, b, name):
    t, d = dx.shape
    f = wd.shape[0]
    tm, tn, tk = _tile(t, 1024), _tile(f, 1536), _tile(d, 512)
    nk = d // tk

    def body(dx_ref, wd_ref, a_ref, b_ref, da_ref, db_ref, acc):
        kk = pl.program_id(2)

        @pl.when(kk == 0)
        def _():
            acc[...] = jnp.zeros_like(acc)

        acc[...] += lax.dot_general(dx_ref[...].astype(BF16), wd_ref[...], ((((1,), (1,))), ((), ())),
                                    preferred_element_type=F32)

        @pl.when(kk == nk - 1)
        def _():
            dh = 0.5 * acc[...]
            av, bv = a_ref[...].astype(F32), b_ref[...].astype(F32)
            sg = jax.nn.sigmoid(av)
            da_ref[...] = (dh * bv * (sg * (1.0 + av * (1.0 - sg)))).astype(BF16)
            db_ref[...] = (dh * (av * sg)).astype(BF16)

    act = pl.BlockSpec((tm, tn), lambda i, j, kk: (i, j))
    out = jax.ShapeDtypeStruct((t, f), BF16)
    block_bytes = _nbytes((tm, tk), F32) + _nbytes((tn, tk), BF16) + 4 * _nbytes((tm, tn), BF16) + _nbytes((tm, tn), F32)
    return pl.pallas_call(
        body, out_shape=(out, out), grid=(t // tm, f // tn, nk),
        in_specs=[pl.BlockSpec((tm, tk), lambda i, j, kk: (i, kk)), pl.BlockSpec((tn, tk), lambda i, j, kk: (j, kk)),
                  act, act],
        out_specs=(act, act), scratch_shapes=[pltpu.VMEM((tm, tn), F32)],
        compiler_params=_cparams(("parallel", "parallel", "arbitrary"), block_bytes), name=name)(dx, wd, a, b)


AXIS = dict(BIG)


def _full(wb, n):
    _, _, r, ccols = wb[n].shape
    return wb[n].reshape(N_CHIPS, 2 * r, ccols) if AXIS[n] == 1 else wb[n].reshape(N_CHIPS * 2 * r, ccols)


def _gather(wb, specs):
    items, names = [], []
    for s in specs:
        n, r0, r1 = (s, 0, wb[s].shape[2]) if isinstance(s, str) else s
        items.append((wb[n], r0, r1))
        if n not in names:
            names.append(n)
    return _gather_rider(items), names


def _landed(wb, names, results):
    for n, r in zip(names, results):
        wb[n] = r


def _reduce_first(grads, names, wb, c_idx):
    g4 = [g.reshape(wb[n].shape) for g, n in zip(grads, names)]
    from_sibling = _sibling_exchange(g4, "rs_sibling_" + names[0])
    return [_sibling_sum(a, b, c_idx, f"rs_sum1_{n}") for a, b, n in zip(g4, from_sibling, names)]


def _ffn_forward(x, gain, wb, tag, up_specs, down_specs):
    n = _rmsnorm_fwd(x, gain, f"{tag}_norm")
    rider, names = _gather(wb, up_specs)
    (a, b, h), got = _ffn_up(n, _full(wb, f"{tag}_w_gate"), _full(wb, f"{tag}_w_up"), f"{tag}_up", rider=rider)
    _landed(wb, names, got)
    down = dict(scale=0.5, residual=x, caps=(1024, 1024, 1536))
    if down_specs:
        rider, names = _gather(wb, down_specs)
        x_next, got = _matmul([(h, _full(wb, f"{tag}_w_down"))], 'nn', F32, f"{tag}_down", rider=rider, **down)
        _landed(wb, names, got)
    else:
        x_next = _matmul([(h, _full(wb, f"{tag}_w_down"))], 'nn', F32, f"{tag}_down", **down)
    return x_next, (n, a, b, h)


def _ffn_backward(x, gain, wb, saved, dx_next, dx_next_b, c_idx, tag):
    n, a, b, h = saved
    wg, wu, wd = (f"{tag}_w_gate", f"{tag}_w_up", f"{tag}_w_down")
    da, db = _ffn_bwd_act(dx_next_b, _full(wb, wd), a, b, f"{tag}_bwd_act")
    g_wd = _matmul([(h, dx_next_b)], 'tn', BF16, f"{tag}_dwd", scale=0.5, caps=(1536, 1024, 1024))
    (p_wd,) = _reduce_first([g_wd], [wd], wb, c_idx)
    g_wg, (r_wd,) = _matmul([(n, da)], 'tn', BF16, f"{tag}_dwg", out3=N_CHIPS, caps=(1024, 1024, 1024), rider=_scatter_rider([p_wd]))
    g_wu = _matmul([(n, db)], 'tn', BF16, f"{tag}_dwu", out3=N_CHIPS, caps=(1024, 1024, 1024))
    p_wg, p_wu = _reduce_first([g_wg, g_wu], [wg, wu], wb, c_idx)
    dn, (r_wg, r_wu) = _matmul([(da, _full(wb, wg)), (db, _full(wb, wu))], 'nt', F32, f"{tag}_dn", b3=True,
                               rider=_scatter_rider([p_wg, p_wu]))
    dx, dx_b, g_gain = _rmsnorm_bwd(x, gain, dn, dx_next, f"{tag}_norm_bwd")
    return dx, dx_b, g_gain, {wg: (p_wg, r_wg), wu: (p_wu, r_wu), wd: (p_wd, r_wd)}


def _rope_tables(seq):
    half = ROPE_DIM // 2
    inv_freq = ROPE_THETA ** (-jnp.arange(0, ROPE_DIM, 2, dtype=F32) / ROPE_DIM)
    ang = jnp.arange(seq).astype(F32)[:, None] * inv_freq[None, :]
    cos, sin = jnp.cos(ang), jnp.sin(ang)
    zeros = lambda w: jnp.zeros((seq, w), F32)
    c = jnp.concatenate([cos, cos, jnp.ones((seq, HEAD_DIM - ROPE_DIM), F32)], axis=1)
    s_up = jnp.concatenate([-sin, zeros(HEAD_DIM - half)], axis=1)
    s_dn = jnp.concatenate([zeros(half), sin, zeros(HEAD_DIM - ROPE_DIM)], axis=1)
    return c, s_up, s_dn


def _rope(x, width, tables, name):
    t = x.shape[0]
    tm = _rows(t, 512)
    half = ROPE_DIM // 2
    c, s_up, s_dn = tables

    def body(x_ref, c_ref, up_ref, dn_ref, o_ref):
        cv, uv, dv = c_ref[...], up_ref[...], dn_ref[...]
        for h in range(GROUP_WIDTH // HEAD_DIM):
            sl = slice(h * HEAD_DIM, (h + 1) * HEAD_DIM)
            xv = x_ref[:, sl].astype(F32)
            o_ref[:, sl] = (xv * cv + pltpu.roll(xv, HEAD_DIM - half, 1) * uv + pltpu.roll(xv, half, 1) * dv).astype(BF16)

    blk = pl.BlockSpec((tm, GROUP_WIDTH), lambda i, j: (i, j))
    tab = pl.BlockSpec((tm, HEAD_DIM), lambda i, j: (i, 0))
    return pl.pallas_call(
        body, out_shape=jax.ShapeDtypeStruct((t, width), BF16), grid=(t // tm, width // GROUP_WIDTH),
        in_specs=[blk, tab, tab, tab], out_specs=blk,
        compiler_params=_cparams(("parallel", "parallel"), 2 * _nbytes((tm, GROUP_WIDTH), F32)), name=name)(x, c, s_up, s_dn)


def _att_masks():
    qi = lax.broadcasted_iota(jnp.int32, (ATT_BLOCK, ATT_BLOCK), 0)
    kj = lax.broadcasted_iota(jnp.int32, (ATT_BLOCK, ATT_BLOCK), 1)
    return kj >= qi, kj <= qi


def _scores(q, k):
    return lax.dot_general(q, k, (((1,), (1,)), ((), ())), preferred_element_type=F32) * (HEAD_DIM ** -0.5)


def _att_fwd(q, k, v, offs, dil, name):
    qo, ko, vo = offs
    length = q.shape[0]
    nb = length // ATT_BLOCK

    def body(q_ref, kp_ref, kc_ref, vp_ref, vc_ref, o_ref, lse_ref):
        has_prev = pl.program_id(1) > 0
        m_prev, m_cur = _att_masks()
        m_prev = jnp.logical_and(m_prev, has_prev)
        for h in range(HEADS_PER_GROUP):
            sl = slice(h * HEAD_DIM, (h + 1) * HEAD_DIM)
            qv = q_ref[:, sl]
            s_p = jnp.where(m_prev, _scores(qv, kp_ref[:, sl]), MASKED)
            s_c = jnp.where(m_cur, _scores(qv, kc_ref[:, sl]), MASKED)
            m = jnp.maximum(jnp.max(s_p, axis=1, keepdims=True), jnp.max(s_c, axis=1, keepdims=True))
            p_p, p_c = jnp.exp(s_p - m), jnp.exp(s_c - m)
            l = jnp.sum(p_p, axis=1, keepdims=True) + jnp.sum(p_c, axis=1, keepdims=True)
            acc = jnp.dot(p_p.astype(BF16), vp_ref[:, sl], preferred_element_type=F32)
            acc += jnp.dot(p_c.astype(BF16), vc_ref[:, sl], preferred_element_type=F32)
            o_ref[:, sl] = acc / l
            lse_ref[:, sl] = jnp.broadcast_to(m + jnp.log(l), (ATT_BLOCK, HEAD_DIM))

    def spec(off, prev):
        if prev:
            return pl.BlockSpec((ATT_BLOCK, GROUP_WIDTH), lambda r, n: (jnp.maximum(n - 1, 0), off + r))
        return pl.BlockSpec((ATT_BLOCK, GROUP_WIDTH), lambda r, n: (n, off + r))

    out = jax.ShapeDtypeStruct((length, dil * GROUP_WIDTH), F32)
    o_spec = pl.BlockSpec((ATT_BLOCK, GROUP_WIDTH), lambda r, n: (n, r))
    return pl.pallas_call(
        body, out_shape=(out, out), grid=(dil, nb),
        in_specs=[spec(qo, False), spec(ko, True), spec(ko, False), spec(vo, True), spec(vo, False)],
        out_specs=(o_spec, o_spec),
        compiler_params=_cparams(("parallel", "parallel"), 8 * _nbytes((ATT_BLOCK, GROUP_WIDTH), F32)), name=name)(q, k, k, v, v)


def _att_combine(outs, lses, name):
    t = outs[0].shape[0]
    tm = _rows(t, 512)

    def body(*refs):
        o_refs, l_refs = refs[:N_GROUPS], refs[N_GROUPS:2 * N_GROUPS]
        ob_ref, of_ref, lse_ref = refs[2 * N_GROUPS:]
        ls = [r[...] for r in l_refs]
        m = functools.reduce(jnp.maximum, ls)
        ws = [jnp.exp(l - m) for l in ls]
        den = functools.reduce(jnp.add, ws)
        num = functools.reduce(jnp.add, [w * r[...] for w, r in zip(ws, o_refs)])
        o = num / den
        ob_ref[...] = o.astype(BF16)
        of_ref[...] = o
        lse_ref[...] = m + jnp.log(den)

    blk = pl.BlockSpec((tm, GROUP_WIDTH), lambda i: (i, 0))
    f32 = jax.ShapeDtypeStruct((t, GROUP_WIDTH), F32)
    return pl.pallas_call(
        body, out_shape=(jax.ShapeDtypeStruct((t, GROUP_WIDTH), BF16), f32, f32), grid=(t // tm,),
        in_specs=[blk] * (2 * N_GROUPS), out_specs=(blk, blk, blk),
        compiler_params=_cparams(("parallel",), 9 * _nbytes((tm, GROUP_WIDTH), F32)), name=name)(*outs, *lses)


def _att_delta(do, o, name):
    t = o.shape[0]
    tm = _rows(t, 512)

    def body(do_ref, o_ref, d_ref):
        for h in range(HEADS_PER_GROUP):
            sl = slice(h * HEAD_DIM, (h + 1) * HEAD_DIM)
            s = jnp.sum(do_ref[:, sl] * o_ref[:, sl], axis=1, keepdims=True)
            d_ref[:, sl] = jnp.broadcast_to(s, (tm, HEAD_DIM))

    blk = pl.BlockSpec((tm, GROUP_WIDTH), lambda i: (i, 0))
    return pl.pallas_call(
        body, out_shape=jax.ShapeDtypeStruct((t, GROUP_WIDTH), F32), grid=(t // tm,), in_specs=[blk, blk], out_specs=blk,
        compiler_params=_cparams(("parallel",), 3 * _nbytes((tm, GROUP_WIDTH), F32)), name=name)(do, o)


def _att_bwd_dq(q, k, v, do, lse, delta, offs, dil, name):
    qo, ko, vo = offs
    length = q.shape[0]
    nb = length // ATT_BLOCK
    scale = HEAD_DIM ** -0.5

    def body(q_ref, kp_ref, kc_ref, vp_ref, vc_ref, do_ref, lse_ref, dl_ref, dq_ref):
        has_prev = pl.program_id(1) > 0
        m_prev, m_cur = _att_masks()
        m_prev = jnp.logical_and(m_prev, has_prev)
        for h in range(HEADS_PER_GROUP):
            sl = slice(h * HEAD_DIM, (h + 1) * HEAD_DIM)
            qv, dov = q_ref[:, sl], do_ref[:, sl].astype(BF16)
            lsev, dlv = lse_ref[:, sl], dl_ref[:, sl]
            dq = None
            for mask, k_ref, v_ref in ((m_prev, kp_ref, vp_ref), (m_cur, kc_ref, vc_ref)):
                kv = k_ref[:, sl]
                p = jnp.exp(jnp.where(mask, _scores(qv, kv), MASKED) - lsev)
                dp = lax.dot_general(dov, v_ref[:, sl], (((1,), (1,)), ((), ())), preferred_element_type=F32)
                ds = (p * (dp - dlv) * scale).astype(BF16)
                part = jnp.dot(ds, kv, preferred_element_type=F32)
                dq = part if dq is None else dq + part
            dq_ref[:, sl] = dq.astype(BF16)

    def spec(off, prev):
        if prev:
            return pl.BlockSpec((ATT_BLOCK, GROUP_WIDTH), lambda r, n: (jnp.maximum(n - 1, 0), off + r))
        return pl.BlockSpec((ATT_BLOCK, GROUP_WIDTH), lambda r, n: (n, off + r))

    own = pl.BlockSpec((ATT_BLOCK, GROUP_WIDTH), lambda r, n: (n, r))
    return pl.pallas_call(
        body, out_shape=jax.ShapeDtypeStruct((length, dil * GROUP_WIDTH), BF16), grid=(dil, nb),
        in_specs=[spec(qo, False), spec(ko, True), spec(ko, False), spec(vo, True), spec(vo, False), own, own, own],
        out_specs=own,
        compiler_params=_cparams(("parallel", "parallel"), 10 * _nbytes((ATT_BLOCK, GROUP_WIDTH), F32)),
        name=name)(q, k, k, v, v, do, lse, delta)


def _att_bwd_dkv(q, k, v, do, lse, delta, offs, dil, name):
    qo, ko, vo = offs
    length = q.shape[0]
    nb = length // ATT_BLOCK
    scale = HEAD_DIM ** -0.5

    def body(k_ref, v_ref, qc_ref, qn_ref, doc_ref, don_ref, lsec_ref, lsen_ref, dlc_ref, dln_ref, dk_ref, dv_ref):
        has_next = pl.program_id(1) < nb - 1
        m_prev, m_cur = _att_masks()
        m_prev = jnp.logical_and(m_prev, has_next)
        for h in range(HEADS_PER_GROUP):
            sl = slice(h * HEAD_DIM, (h + 1) * HEAD_DIM)
            kv, vv = k_ref[:, sl], v_ref[:, sl]
            dk = dv = None
            for mask, q_ref, do_ref, lse_ref, dl_ref in ((m_cur, qc_ref, doc_ref, lsec_ref, dlc_ref),
                                                         (m_prev, qn_ref, don_ref, lsen_ref, dln_ref)):
                qv, dov = q_ref[:, sl], do_ref[:, sl].astype(BF16)
                p = jnp.exp(jnp.where(mask, _scores(qv, kv), MASKED) - lse_ref[:, sl])
                dp = lax.dot_general(dov, vv, (((1,), (1,)), ((), ())), preferred_element_type=F32)
                ds = (p * (dp - dl_ref[:, sl]) * scale).astype(BF16)
                dv_part = lax.dot_general(p.astype(BF16), dov, (((0,), (0,)), ((), ())), preferred_element_type=F32)
                dk_part = lax.dot_general(ds, qv, (((0,), (0,)), ((), ())), preferred_element_type=F32)
                dv = dv_part if dv is None else dv + dv_part
                dk = dk_part if dk is None else dk + dk_part
            dk_ref[:, sl] = dk.astype(BF16)
            dv_ref[:, sl] = dv.astype(BF16)

    def spec(off, nxt):
        if nxt:
            return pl.BlockSpec((ATT_BLOCK, GROUP_WIDTH), lambda r, n: (jnp.minimum(n + 1, nb - 1), off + r))
        return pl.BlockSpec((ATT_BLOCK, GROUP_WIDTH), lambda r, n: (n, off + r))

    own = pl.BlockSpec((ATT_BLOCK, GROUP_WIDTH), lambda r, n: (n, r))
    out = jax.ShapeDtypeStruct((length, dil * GROUP_WIDTH), BF16)
    return pl.pallas_call(
        body, out_shape=(out, out), grid=(dil, nb),
        in_specs=[spec(ko, False), spec(vo, False), spec(qo, False), spec(qo, True), spec(0, False), spec(0, True),
                  spec(0, False), spec(0, True), spec(0, False), spec(0, True)],
        out_specs=(own, own),
        compiler_params=_cparams(("parallel", "parallel"), 12 * _nbytes((ATT_BLOCK, GROUP_WIDTH), F32)),
        name=name)(k, v, q, q, do, do, lse, lse, delta, delta)


def _gelu(x):
    return 0.5 * x * (1.0 + lax.erf(x * (2.0 ** -0.5)))


def _gelu_grad(x):
    return 0.5 * (1.0 + lax.erf(x * (2.0 ** -0.5))) + x * jnp.exp(-0.5 * x * x) * ((2.0 * jnp.pi) ** -0.5)


def _sg_normed(vs, lg, lb):
    gv = _gelu(vs)
    mu = jnp.mean(gv, axis=1, keepdims=True)
    xc = gv - mu
    rstd = lax.rsqrt(jnp.mean(xc * xc, axis=1, keepdims=True) + LN_EPS)
    z = xc * rstd
    return z, rstd, z * lg + lb


def _sg_tril():
    row = lax.broadcasted_iota(jnp.int32, (SG_CHUNK, SG_CHUNK), 0)
    col = lax.broadcasted_iota(jnp.int32, (SG_CHUNK, SG_CHUNK), 1)
    return row >= col


def _sg_fwd(proj, u_blk, vs_blk, lg, lb, sg_w, bias, name):
    t = proj.shape[0]
    width = SG_GROUPS * SG_GROUP_DIM

    def body(u_ref, vs_ref, lg_ref, lb_ref, w_ref, bias_ref, o_ref):
        _, _, vn = _sg_normed(vs_ref[...].astype(F32), lg_ref[...], lb_ref[...])
        vn = vn.astype(BF16)
        tril = _sg_tril()
        for g in range(SG_GROUPS):
            sl = slice(g * SG_GROUP_DIM, (g + 1) * SG_GROUP_DIM)
            w = jnp.where(tril, w_ref[g], 0.0).astype(BF16)
            sp = jnp.dot(w, vn[:, sl], preferred_element_type=F32) + bias_ref[:, sl]
            o_ref[:, sl] = (_gelu(u_ref[:, sl].astype(F32)) * sp).astype(BF16)

    vec = pl.BlockSpec((1, width), lambda i: (0, 0))
    return pl.pallas_call(
        body, out_shape=jax.ShapeDtypeStruct((t, width), BF16), grid=(t // SG_CHUNK,),
        in_specs=[pl.BlockSpec((SG_CHUNK, width), lambda i: (i, u_blk)), pl.BlockSpec((SG_CHUNK, width), lambda i: (i, vs_blk)),
                  vec, vec, pl.BlockSpec((SG_GROUPS, SG_CHUNK, SG_CHUNK), lambda i: (0, 0, 0)),
                  pl.BlockSpec((SG_CHUNK, width), lambda i: (0, 0))],
        out_specs=pl.BlockSpec((SG_CHUNK, width), lambda i: (i, 0)),
        compiler_params=_cparams(("parallel",), 8 * _nbytes((SG_CHUNK, width), F32)), name=name)(proj, proj, lg, lb, sg_w, bias)


def _sg_bwd(proj, u_blk, vs_blk, dsu, lg, lb, sg_w, bias, name):
    t = proj.shape[0]
    width = SG_GROUPS * SG_GROUP_DIM

    def body(u_ref, vs_ref, dsu_ref, lg_ref, lb_ref, w_ref, bias_ref, du_ref, dvs_ref, dw_ref, dbias_ref, dlg_ref, dlb_ref):
        @pl.when(pl.program_id(0) == 0)
        def _():
            dw_ref[...] = jnp.zeros_like(dw_ref)
            dbias_ref[...] = jnp.zeros_like(dbias_ref)
            dlg_ref[...] = jnp.zeros_like(dlg_ref)
            dlb_ref[...] = jnp.zeros_like(dlb_ref)

        vs = vs_ref[...].astype(F32)
        z, rstd, vn = _sg_normed(vs, lg_ref[...], lb_ref[...])
        vn = vn.astype(BF16)
        tril = _sg_tril()
        dvn = []
        for g in range(SG_GROUPS):
            sl = slice(g * SG_GROUP_DIM, (g + 1) * SG_GROUP_DIM)
            w = jnp.where(tril, w_ref[g], 0.0).astype(BF16)
            vg = vn[:, sl]
            sp = jnp.dot(w, vg, preferred_element_type=F32) + bias_ref[:, sl]
            uv = u_ref[:, sl].astype(F32)
            dsu_g = dsu_ref[:, sl].astype(F32)
            du_ref[:, sl] = (dsu_g * sp * _gelu_grad(uv)).astype(BF16)
            dsp = dsu_g * _gelu(uv)
            dsp_b = dsp.astype(BF16)
            dw = lax.dot_general(dsp_b, vg, (((1,), (1,)), ((), ())), preferred_element_type=F32)
            dw_ref[g] += jnp.where(tril, dw, 0.0)
            dbias_ref[:, sl] += jnp.broadcast_to(jnp.sum(dsp, axis=1, keepdims=True), (SG_CHUNK, SG_GROUP_DIM))
            dvn.append(lax.dot_general(w, dsp_b, (((0,), (0,)), ((), ())), preferred_element_type=F32))
        dvn = jnp.concatenate(dvn, axis=1)
        dlg_ref[...] += jnp.sum(dvn * z, axis=0, keepdims=True)
        dlb_ref[...] += jnp.sum(dvn, axis=0, keepdims=True)
        dz = dvn * lg_ref[...]
        dgv = rstd * (dz - jnp.mean(dz, axis=1, keepdims=True) - z * jnp.mean(dz * z, axis=1, keepdims=True))
        dvs_ref[...] = (dgv * _gelu_grad(vs)).astype(BF16)

    vec = pl.BlockSpec((1, width), lambda i: (0, 0))
    row = pl.BlockSpec((SG_CHUNK, width), lambda i: (i, 0))
    fixed = pl.BlockSpec((SG_CHUNK, width), lambda i: (0, 0))
    w_spec = pl.BlockSpec((SG_GROUPS, SG_CHUNK, SG_CHUNK), lambda i: (0, 0, 0))
    act = jax.ShapeDtypeStruct((t, width), BF16)
    return pl.pallas_call(
        body,
        out_shape=(act, act, jax.ShapeDtypeStruct((SG_GROUPS, SG_CHUNK, SG_CHUNK), F32),
                   jax.ShapeDtypeStruct((SG_CHUNK, width), F32), jax.ShapeDtypeStruct((1, width), F32),
                   jax.ShapeDtypeStruct((1, width), F32)),
        grid=(t // SG_CHUNK,),
        in_specs=[pl.BlockSpec((SG_CHUNK, width), lambda i: (i, u_blk)), pl.BlockSpec((SG_CHUNK, width), lambda i: (i, vs_blk)),
                  row, vec, vec, w_spec, fixed],
        out_specs=(row, row, w_spec, fixed, vec, vec),
        compiler_params=_cparams(("arbitrary",), 14 * _nbytes((SG_CHUNK, width), F32)),
        name=name)(proj, proj, dsu, lg, lb, sg_w, bias)


def _gate_fwd(proj, ga_blk, gs_blk, y_att, y_sg, name):
    t, d = y_att.shape
    tm, tn = _rows(t, 512), _tile(d, GROUP_WIDTH)

    def body(ga_ref, gs_ref, ya_ref, ys_ref, o_ref):
        o_ref[...] = (jax.nn.sigmoid(ga_ref[...].astype(F32)) * ya_ref[...].astype(F32)
                      + jax.nn.sigmoid(gs_ref[...].astype(F32)) * ys_ref[...].astype(F32)).astype(BF16)

    own = pl.BlockSpec((tm, tn), lambda i, j: (i, j))
    return pl.pallas_call(
        body, out_shape=jax.ShapeDtypeStruct((t, d), BF16), grid=(t // tm, d // tn),
        in_specs=[pl.BlockSpec((tm, tn), lambda i, j: (i, ga_blk + j)), pl.BlockSpec((tm, tn), lambda i, j: (i, gs_blk + j)),
                  own, own],
        out_specs=own, compiler_params=_cparams(("parallel", "parallel"), 6 * _nbytes((tm, tn), F32)),
        name=name)(proj, proj, y_att, y_sg)


def _gate_bwd(proj, ga_blk, gs_blk, y_att, y_sg, dmerged, name):
    t, d = y_att.shape
    tm, tn = _rows(t, 512), _tile(d, GROUP_WIDTH)

    def body(ga_ref, gs_ref, ya_ref, ys_ref, dm_ref, dya_ref, dys_ref, dga_ref, dgs_ref):
        dm = dm_ref[...].astype(F32)
        for g_ref, y_ref, dy_ref, dg_ref in ((ga_ref, ya_ref, dya_ref, dga_ref), (gs_ref, ys_ref, dys_ref, dgs_ref)):
            sg = jax.nn.sigmoid(g_ref[...].astype(F32))
            dy_ref[...] = (dm * sg).astype(BF16)
            dg_ref[...] = (dm * y_ref[...].astype(F32) * sg * (1.0 - sg)).astype(BF16)

    own = pl.BlockSpec((tm, tn), lambda i, j: (i, j))
    out = jax.ShapeDtypeStruct((t, d), BF16)
    return pl.pallas_call(
        body, out_shape=(out, out, out, out), grid=(t // tm, d // tn),
        in_specs=[pl.BlockSpec((tm, tn), lambda i, j: (i, ga_blk + j)), pl.BlockSpec((tm, tn), lambda i, j: (i, gs_blk + j)),
                  own, own, own],
        out_specs=(own, own, own, own), compiler_params=_cparams(("parallel", "parallel"), 10 * _nbytes((tm, tn), F32)),
        name=name)(proj, proj, y_att, y_sg, dmerged)


def _group_view(arr, col, dil):
    t = arr.shape[0]
    return arr[:, col:col + GROUP_WIDTH].reshape(t // dil, dil * GROUP_WIDTH)


def _mixer_forward(x, wb, small, in_specs, out_specs):
    t, d = x.shape
    att_w = N_GROUPS * GROUP_WIDTH
    sg_w = SG_GROUPS * SG_GROUP_DIM
    n = _rmsnorm_fwd(x, small['mix_norm'], "mix_norm")
    rider, names = _gather(wb, in_specs)
    proj, got = _matmul([(n, _full(wb, 'w_in'))], 'nn', BF16, "mix_in", b3=True, caps=(1024, 1024, 1024), rider=rider)
    _landed(wb, names, got)
    tables = _rope_tables(t)
    qk = _rope(proj, 2 * att_w, tables, "mix_rope")
    outs, lses = [], []
    for gi, dil in enumerate(DILATIONS):
        if dil == 1:
            args = (qk, qk, proj, (gi, N_GROUPS + gi, 2 * N_GROUPS + gi))
        else:
            args = (_group_view(qk, gi * GROUP_WIDTH, dil), _group_view(qk, att_w + gi * GROUP_WIDTH, dil),
                    _group_view(proj, 2 * att_w + gi * GROUP_WIDTH, dil), (0, 0, 0))
        o, lse = _att_fwd(*args, dil, f"att_fwd{gi}")
        outs.append(o.reshape(t, GROUP_WIDTH))
        lses.append(lse.reshape(t, GROUP_WIDTH))
    o_b, o_f, lse = _att_combine(outs, lses, "att_combine")
    y_att = _matmul([(o_b, _full(wb, 'w_att_out'))], 'nn', BF16, "mix_att_out", b3=True)
    bias = jnp.repeat(small['sg_b'].T, SG_GROUP_DIM, axis=1)
    u_blk, vs_blk = 3 * att_w // sg_w, 3 * att_w // sg_w + 1
    su = _sg_fwd(proj, u_blk, vs_blk, small['sg_ln_g'], small['sg_ln_b'], small['sg_w'], bias, "sg_fwd")
    y_sg = _matmul([(su, _full(wb, 'w_sg_out'))], 'nn', BF16, "mix_sg_out", b3=True)
    ga_blk = (3 * att_w + 2 * sg_w) // _tile(d, GROUP_WIDTH)
    gs_blk = ga_blk + d // _tile(d, GROUP_WIDTH)
    merged = _gate_fwd(proj, ga_blk, gs_blk, y_att, y_sg, "gate_fwd")
    rider, names = _gather(wb, out_specs)
    x_next, got = _matmul([(merged, _full(wb, 'w_out'))], 'nn', F32, "mix_out", residual=x, rider=rider)
    _landed(wb, names, got)
    saved =(n, proj, qk, tables, o_b, o_f, lse, y_att, su, y_sg, merged, bias, (u_blk, vs_blk, ga_blk, gs_blk))
    return x_next, saved


def _mixer_backward(x, wb, small, saved, dx_next, dx_next_b, c_idx):
    n, proj, qk, tables, o_b, o_f, lse, y_att, su, y_sg, merged, bias, (u_blk, vs_blk, ga_blk, gs_blk) = saved
    t, d = x.shape
    att_w = N_GROUPS * GROUP_WIDTH
    s = N_CHIPS
    dmerged = _matmul([(dx_next_b, _full(wb, 'w_out'))], 'nt', BF16, "mix_out_dx")
    g_w_out = _matmul([(merged, dx_next_b)], 'tn', BF16, "mix_out_dw", caps=(1024, 1024, 1024))
    dy_att, dy_sg, dg_att, dg_sg = _gate_bwd(proj, ga_blk, gs_blk, y_att, y_sg, dmerged, "gate_bwd")

    g_w_att_out = _matmul([(o_b, dy_att)], 'tn', BF16, "mix_att_out_dw", out3=s)
    do = _matmul([(dy_att, _full(wb, 'w_att_out'))], 'nt', F32, "mix_att_out_dx", b3=True)
    delta = _att_delta(do, o_f, "att_delta")
    dqs, dks, dvs_ = [], [], []
    for gi, dil in enumerate(DILATIONS):
        if dil == 1:
            args = (qk, qk, proj, do, lse, delta, (gi, N_GROUPS + gi, 2 * N_GROUPS + gi))
        else:
            args = (_group_view(qk, gi * GROUP_WIDTH, dil), _group_view(qk, att_w + gi * GROUP_WIDTH, dil),
                    _group_view(proj, 2 * att_w + gi * GROUP_WIDTH, dil), _group_view(do, 0, dil),
                    _group_view(lse, 0, dil), _group_view(delta, 0, dil), (0, 0, 0))
        dq = _att_bwd_dq(*args, dil, f"att_bwd_dq{gi}")
        dk, dv = _att_bwd_dkv(*args, dil, f"att_bwd_dkv{gi}")
        dqs.append(dq.reshape(t, GROUP_WIDTH))
        dks.append(dk.reshape(t, GROUP_WIDTH))
        dvs_.append(dv.reshape(t, GROUP_WIDTH))
    c, s_up, s_dn = tables
    dqk = _rope(jnp.concatenate(dqs + dks, axis=1), 2 * att_w, (c, -s_up, -s_dn), "mix_rope_bwd")

    g_w_sg_out = _matmul([(su, dy_sg)], 'tn', BF16, "mix_sg_out_dw", out3=s)
    out_names = ['w_out', 'w_att_out', 'w_sg_out']
    out_parts = _reduce_first([g_w_out, g_w_att_out, g_w_sg_out], out_names, wb, c_idx)
    dsu = _matmul([(dy_sg, _full(wb, 'w_sg_out'))], 'nt', BF16, "mix_sg_out_dx", b3=True)
    du, dvs, g_sg_w, g_bias, g_lg, g_lb = _sg_bwd(proj, u_blk, vs_blk, dsu, small['sg_ln_g'], small['sg_ln_b'],
                                                   small['sg_w'], bias, "sg_bwd")
    gs = {'sg_w': g_sg_w, 'sg_b': g_bias[:, ::SG_GROUP_DIM].T, 'sg_ln_g': g_lg, 'sg_ln_b': g_lb}

    dproj = jnp.concatenate([dqk] + dvs_ + [du, dvs, dg_att, dg_sg], axis=1)
    g_w_in, out_recv = _matmul([(n, dproj)], 'tn', BF16, "mix_in_dw", out3=s, caps=(1024, 1024, 1024),
                               rider=_scatter_rider(out_parts))
    (p_w_in,) = _reduce_first([g_w_in], ['w_in'], wb, c_idx)
    dn, (r_w_in,) = _matmul([(dproj, _full(wb, 'w_in'))], 'nt', F32, "mix_in_dx", b3=True, caps=(1024, 1024, 512),
                            rider=_scatter_rider([p_w_in]))
    dx, dx_b, gs['mix_norm'] = _rmsnorm_bwd(x, small['mix_norm'], dn, dx_next, "mix_norm_bwd")
    g = {nm: (p, r) for nm, p, r in zip(out_names, out_parts, out_recv)}
    g['w_in'] = (p_w_in, r_w_in)
    return dx, dx_b, g, gs


def _step(x, target, wb, small, c_idx):
    wb = dict(wb)
    rider, names = _gather(wb, ['ffn1_w_gate', 'ffn1_w_up'])
    _landed(wb, names, _exchange(rider, "gather_first"))
    half_in = wb['w_in'].shape[2] // 2
    x1, s1 = _ffn_forward(x, small['ffn1_norm'], wb, "ffn1", ['ffn1_w_down', ('w_in', 0, half_in)], [('w_in', half_in, 2 * half_in)])
    x2, s2 = _mixer_forward(x1, wb, small, ['w_att_out', 'w_sg_out', 'w_out', 'ffn2_w_gate'], ['ffn2_w_up'])
    x3, s3 = _ffn_forward(x2, small['ffn2_norm'], wb, "ffn2", ['ffn2_w_down'], None)
    loss, dx3, dx3_b, g_final = _final_loss(x3, small['final_norm'], target, "final_loss")
    gs = {'final_norm': g_final}
    dx2, dx2_b, gs['ffn2_norm'], g = _ffn_backward(x2, small['ffn2_norm'], wb, s3, dx3, dx3_b, c_idx, "ffn2")
    dx1, dx1_b, g_mix, gs_mix = _mixer_backward(x1, wb, small, s2, dx2, dx2_b, c_idx)
    g.update(g_mix)
    gs.update(gs_mix)
    dx0, _, gs['ffn1_norm'], g_ffn1 = _ffn_backward(x, small['ffn1_norm'], wb, s1, dx1, dx1_b, c_idx, "ffn1")
    g.update(g_ffn1)
    return loss, dx0, g, gs


def _cast_into_gathered(wt, p_idx, name):
    r, ccols = wt.shape[0] // 2, wt.shape[1]
    tm = _rows(r, 256)
    nb = r // tm

    def body(p_ref, w_ref, o_ref):
        o_ref[...] = w_ref[...].astype(BF16)

    grid_spec = pltpu.PrefetchScalarGridSpec(
        num_scalar_prefetch=1, grid=(2, nb),
        in_specs=[pl.BlockSpec((tm, ccols), lambda h, i, pr: (h * nb + i, 0))],
        out_specs=pl.BlockSpec((None, None, tm, ccols), lambda h, i, pr: (pr[0], h, i, 0)))
    return pl.pallas_call(body, out_shape=jax.ShapeDtypeStruct((N_CHIPS, 2, r, ccols), BF16), grid_spec=grid_spec,
                          compiler_params=_cparams(("parallel", "parallel"), 2 * _nbytes((tm, ccols), F32)), name=name)(p_idx, wt)


def _sibling_exchange(grads, name):
    nw = len(grads)

    def body(*refs):
        src, dst = refs[:nw], refs[nw:2 * nw]
        send_sems, recv_sems = refs[2 * nw:]
        x, y, c, _ = _place()
        cps = []
        for i in range(nw):
            cp = pltpu.make_async_remote_copy(src[i].at[:, 1 - c], dst[i], send_sems.at[i], recv_sems.at[i],
                                              device_id=(x, y, 1 - c), device_id_type=MESH)
            cp.start()
            cps.append(cp)
        for cp in cps:
            cp.wait()

    any_spec = pl.BlockSpec(memory_space=pl.ANY)
    return pl.pallas_call(
        body, out_shape=[jax.ShapeDtypeStruct((g.shape[0],) + g.shape[2:], g.dtype) for g in grads],
        in_specs=[any_spec] * nw, out_specs=[any_spec] * nw,
        scratch_shapes=[pltpu.SemaphoreType.DMA((nw,)), pltpu.SemaphoreType.DMA((nw,))],
        compiler_params=pltpu.CompilerParams(has_side_effects=True), name=name)(*grads)


def _half_exchange(bufs):
    nw = len(bufs)

    def body(*refs):
        dst = refs[nw:2 * nw]
        send_sems, recv_sems = refs[2 * nw:]
        x, y, c, _ = _place()
        cps = []
        for i in range(nw):
            mine = dst[i].at[c]
            cp = pltpu.make_async_remote_copy(mine, mine, send_sems.at[i], recv_sems.at[i],
                                              device_id=(x, y, 1 - c), device_id_type=MESH)
            cp.start()
            cps.append(cp)
        for i, cp in enumerate(cps):
            cp.wait_send()
            theirs = dst[i].at[1 - c]
            pltpu.make_async_remote_copy(theirs, theirs, send_sems.at[i], recv_sems.at[i],
                                         device_id=(x, y, 1 - c), device_id_type=MESH).wait_recv()

    any_spec = pl.BlockSpec(memory_space=pl.ANY)
    return pl.pallas_call(
        body, out_shape=[jax.ShapeDtypeStruct(b.shape, b.dtype) for b in bufs],
        in_specs=[any_spec] * nw, out_specs=[any_spec] * nw, input_output_aliases={i: i for i in range(nw)},
        scratch_shapes=[pltpu.SemaphoreType.DMA((nw,)), pltpu.SemaphoreType.DMA((nw,))],
        compiler_params=pltpu.CompilerParams(has_side_effects=True), name="rs_halves")(*bufs)


def _sibling_sum(grad, recv, c_idx, name):
    s, _, r, ccols = grad.shape
    tm = _rows(r, 256)

    def body(c_ref, g_ref, r_ref, o_ref):
        o_ref[...] = (g_ref[...].astype(F32) + r_ref[...].astype(F32)).astype(BF16)

    grid_spec = pltpu.PrefetchScalarGridSpec(
        num_scalar_prefetch=1, grid=(s, r // tm),
        in_specs=[pl.BlockSpec((None, None, tm, ccols), lambda q, i, cr: (q, cr[0], i, 0)),
                  pl.BlockSpec((None, tm, ccols), lambda q, i, cr: (q, i, 0))],
        out_specs=pl.BlockSpec((None, tm, ccols), lambda q, i, cr: (q, i, 0)))
    return pl.pallas_call(body, out_shape=jax.ShapeDtypeStruct((s, r, ccols), BF16), grid_spec=grid_spec,
                          compiler_params=_cparams(("parallel", "parallel"), 4 * _nbytes((tm, ccols), F32)), name=name)(c_idx, grad, recv)


def _chip_sum(part, recv, pc_idx, name):
    _, r, ccols = part.shape
    tm = _rows(r, 256)

    def body(pc_ref, own_ref, r0_ref, r1_ref, r2_ref, o_ref):
        acc = own_ref[...].astype(F32) + r0_ref[...].astype(F32)
        acc = acc + r1_ref[...].astype(F32)
        o_ref[...] = acc + r2_ref[...].astype(F32)

    def slot(j):
        return pl.BlockSpec((None, tm, ccols), lambda i, pc: (j, i, 0))

    grid_spec = pltpu.PrefetchScalarGridSpec(
        num_scalar_prefetch=1, grid=(r // tm,),
        in_specs=[pl.BlockSpec((None, tm, ccols), lambda i, pc: (pc[0], i, 0)), slot(0), slot(1), slot(2)],
        out_specs=pl.BlockSpec((None, tm, ccols), lambda i, pc: (pc[1], i, 0)))
    return pl.pallas_call(body, out_shape=jax.ShapeDtypeStruct((2, r, ccols), F32), grid_spec=grid_spec,
                          compiler_params=_cparams(("parallel",), 6 * _nbytes((tm, ccols), F32)), name=name)(pc_idx, part, recv, recv, recv)


def _all_reduce_small(vec):
    r = vec.shape[0]

    def body(v_ref, o_ref, slots, send_sems, recv_sems):
        x, y, c, _ = _place()
        me = 4 * x + 2 * y + c
        slots[me] = v_ref[...]
        cps = []
        for k in range(1, N_DEV):
            peer = (1 - x if k & 4 else x, 1 - y if k & 2 else y, 1 - c if k & 1 else c)
            cp = pltpu.make_async_remote_copy(slots.at[me], slots.at[me], send_sems.at[k - 1], recv_sems.at[k - 1],
                                              device_id=peer, device_id_type=MESH)
            cp.start()
            cps.append(cp)
        for cp in cps:
            cp.wait()
        acc = slots[0]
        for dev in range(1, N_DEV):
            acc = acc + slots[dev]
        o_ref[...] = acc

    vm = pl.BlockSpec(memory_space=pltpu.VMEM)
    return pl.pallas_call(
        body, out_shape=jax.ShapeDtypeStruct((r, LANES), F32), in_specs=[vm], out_specs=vm,
        scratch_shapes=[pltpu.VMEM((N_DEV, r, LANES), F32), pltpu.SemaphoreType.DMA((N_DEV - 1,)),
                        pltpu.SemaphoreType.DMA((N_DEV - 1,))],
        compiler_params=pltpu.CompilerParams(vmem_limit_bytes=int(4 * _nbytes((N_DEV, r, LANES), F32))),
        name="all_reduce_small")(vec)


def _adamw(wt, g, m, v, name):
    r, ccols = wt.shape
    tm = _rows(r, max(8, (MIB // (4 * ccols)) // 8 * 8))
    blk = pl.BlockSpec((tm, ccols), lambda i: (i, 0))

    def body(w_ref, g_ref, m_ref, v_ref, d_ref, mo_ref, vo_ref):
        gv = g_ref[...]
        mv = ADAM_B1 * m_ref[...] + (1.0 - ADAM_B1) * gv
        vv = ADAM_B2 * v_ref[...] + (1.0 - ADAM_B2) * (gv * gv)
        m_hat = mv / (1.0 - ADAM_B1 ** ADAM_STEP)
        v_hat = vv / (1.0 - ADAM_B2 ** ADAM_STEP)
        d_ref[...] = -ADAM_LR * (m_hat / (jnp.sqrt(v_hat) + ADAM_EPS) + ADAM_WD * w_ref[...])
        mo_ref[...] = mv
        vo_ref[...] = vv

    out = jax.ShapeDtypeStruct((r, ccols), F32)
    return pl.pallas_call(body, out_shape=(out, out, out), grid=(r // tm,), in_specs=[blk] * 4, out_specs=(blk, blk, blk),
                          compiler_params=_cparams(("parallel",), 7 * _nbytes((tm, ccols), F32)), name=name)(wt, g, m, v)


def _as_rows(a):
    rows = a.reshape(-1, LANES)
    return jnp.pad(rows, ((0, -rows.shape[0] % 8), (0, 0)))


def kernel(x, ffn1_norm, ffn1_w_gate, ffn1_w_up, ffn1_w_down, mix_norm, w_in, sg_ln_g, sg_ln_b, sg_w, sg_b, w_att_out, w_sg_out, w_out, ffn2_norm, ffn2_w_gate, ffn2_w_up, ffn2_w_down, final_norm, loss_target, m_ffn1_norm, m_ffn1_w_gate, m_ffn1_w_up, m_ffn1_w_down, m_mix_norm, m_w_in, m_sg_ln_g, m_sg_ln_b, m_sg_w, m_sg_b, m_w_att_out, m_w_sg_out, m_w_out, m_ffn2_norm, m_ffn2_w_gate, m_ffn2_w_up, m_ffn2_w_down, m_final_norm, v_ffn1_norm, v_ffn1_w_gate, v_ffn1_w_up, v_ffn1_w_down, v_mix_norm, v_w_in, v_sg_ln_g, v_sg_ln_b, v_sg_w, v_sg_b, v_w_att_out, v_w_sg_out, v_w_out, v_ffn2_norm, v_ffn2_w_gate, v_ffn2_w_up, v_ffn2_w_down, v_final_norm):
    given = dict(locals())
    wts = {n: given[n] for n in WEIGHT_NAMES}
    ms = {n: given["m_" + n] for n in WEIGHT_NAMES}
    vs = {n: given["v_" + n] for n in WEIGHT_NAMES}
    t, d = x.shape[-2], x.shape[-1]
    xc, yc, cc = lax.axis_index("x"), lax.axis_index("y"), lax.axis_index("c")

    shard2d = {n: wts[n].reshape(wts[n].shape[-2:]) for n in BIG_NAMES}
    p_idx = jnp.reshape(2 * xc + yc, (1,)).astype(jnp.int32)
    c_idx = jnp.reshape(cc, (1,)).astype(jnp.int32)
    pc_idx = jnp.stack([2 * xc + yc, cc]).astype(jnp.int32)
    wb = {n: _cast_into_gathered(shard2d[n], p_idx, f"cast_{n}") for n in BIG_NAMES}

    small = {n: wts[n].reshape(-1, wts[n].shape[-1]) for n in SMALL_NAMES}
    small['sg_w'] = wts['sg_w'].reshape(wts['sg_w'].shape[-3:])
    loss, dx, g, gs = _step(x.reshape(t, d), loss_target.reshape(t, d), wb, small, c_idx)
    loss = lax.psum(loss[0, 0], ("x", "y", "c"))

    my_halves = [_chip_sum(*g[n], pc_idx, f"rs_sum2_{n}") for n in BIG_NAMES]
    reduced = _half_exchange(my_halves)
    grads = {n: r.reshape(shard2d[n].shape) for n, r in zip(BIG_NAMES, reduced)}

    def pack(tree):
        return jnp.concatenate([_as_rows(tree[n]) for n in SMALL_NAMES], axis=0)

    packed = _all_reduce_small(pack(gs))

    delta, new_m, new_v = {}, {}, {}
    for n in BIG_NAMES:
        shape = wts[n].shape
        dl, mm, vv = _adamw(shard2d[n], grads[n], ms[n].reshape(shard2d[n].shape), vs[n].reshape(shard2d[n].shape), f"adamw_{n}")
        grads[n], delta[n], new_m[n], new_v[n] = grads[n].reshape(shape), dl.reshape(shape), mm.reshape(shape), vv.reshape(shape)

    small_out = (packed,) + _adamw(pack(wts), packed, pack(ms), pack(vs), "adamw_small")
    row = 0
    for n in SMALL_NAMES:
        shape = wts[n].shape
        sz = wts[n].size // LANES
        grads[n], delta[n], new_m[n], new_v[n] = (a[row:row + sz].reshape(shape) for a in small_out)
        row += sz + -sz % 8

    return (loss, dx.reshape(x.shape), *[grads[n] for n in WEIGHT_NAMES], *[delta[n] for n in WEIGHT_NAMES],
            *[new_m[n] for n in WEIGHT_NAMES], *[new_v[n] for n in WEIGHT_NAMES])
```

```python
import functools

import jax
import jax.numpy as jnp
from jax import lax
from jax.experimental import pallas as pl
from jax.experimental.pallas import tpu as pltpu

F32 = jnp.float32
BF16 = jnp.bfloat16
MESH = pl.DeviceIdType.MESH

NORM_EPS = 1e-6
LN_EPS = 1e-5
HEAD_DIM = 128
HEADS_PER_GROUP = 4
GROUP_WIDTH = HEADS_PER_GROUP * HEAD_DIM
DILATIONS = (1, 4, 16)
N_GROUPS = len(DILATIONS)
ATT_BLOCK = 128
ROPE_DIM = HEAD_DIM // 4
ROPE_THETA = 500000.0
SG_CHUNK = 128
SG_GROUPS = 12
SG_GROUP_DIM = 128
MASKED = -1e30

ADAM_LR = 0.001
ADAM_B1 = 0.9
ADAM_B2 = 0.999
ADAM_EPS = 1e-08
ADAM_WD = 0.01
ADAM_STEP = 10

N_CHIPS = 4
N_DEV = 8
LANES = 128
MIB = 2 ** 20
VMEM_BYTES_V7X = 64 * MIB

WEIGHT_NAMES = ['ffn1_norm', 'ffn1_w_gate', 'ffn1_w_up', 'ffn1_w_down', 'mix_norm', 'w_in', 'sg_ln_g', 'sg_ln_b',
                'sg_w', 'sg_b', 'w_att_out', 'w_sg_out', 'w_out', 'ffn2_norm', 'ffn2_w_gate', 'ffn2_w_up',
                'ffn2_w_down', 'final_norm']
BIG = [('ffn1_w_gate', 1), ('ffn1_w_up', 1), ('ffn1_w_down', 0), ('w_in', 1), ('w_att_out', 1), ('w_sg_out', 1),
       ('w_out', 0), ('ffn2_w_gate', 1), ('ffn2_w_up', 1), ('ffn2_w_down', 0)]
BIG_NAMES = [n for n, _ in BIG]
SMALL_NAMES = [n for n in WEIGHT_NAMES if n not in BIG_NAMES]


def _nbytes(shape, dtype):
    n = jnp.dtype(dtype).itemsize
    for s in shape:
        if s is not None:
            n *= s
    return n


def _cparams(sem, block_bytes, **kw):
    limit = int(min(max(3 * block_bytes, 32 * MIB), VMEM_BYTES_V7X - 8 * MIB))
    return pltpu.CompilerParams(dimension_semantics=sem, vmem_limit_bytes=limit, **kw)


def _tile(dim, cap):
    best = None
    for t in range(LANES, min(dim, cap) + 1, LANES):
        if dim % t == 0:
            best = t
    if best is None:
        assert dim <= cap, (dim, cap)
        return dim
    return best


def _rows(dim, cap):
    best = None
    for t in range(8, min(dim, cap) + 1, 8):
        if dim % t == 0:
            best = t
    assert best is not None, (dim, cap)
    return best


def _place():
    x, y, c = lax.axis_index("x"), lax.axis_index("y"), lax.axis_index("c")
    others = [(1 - x, y), (x, 1 - y), (1 - x, 1 - y)]
    return x, y, c, others


class _Rider:
    def __init__(self, operands, out_shapes, aliases, sems, start, finish):
        self.operands = operands
        self.out_shapes = out_shapes
        self.aliases = aliases
        self.sems = sems
        self.start = start
        self.finish = finish


def _run(body, *, name, grid, in_specs, out_specs, out_shape, scratch_shapes, operands, block_bytes, rider=None):
    operands = [pltpu.with_memory_space_constraint(a, pltpu.HBM) for a in operands]
    if rider is None:
        sem = ("parallel",) * (len(grid) - 1) + ("arbitrary",)
        return pl.pallas_call(body, out_shape=out_shape, grid=grid, in_specs=in_specs, out_specs=out_specs,
                              scratch_shapes=scratch_shapes, compiler_params=_cparams(sem, block_bytes), name=name)(*operands)
    n_in, n_out, n_scr = len(operands), len(out_shape), len(scratch_shapes)
    r_in, r_out = len(rider.operands), len(rider.out_shapes)
    any_spec = pl.BlockSpec(memory_space=pl.ANY)

    def wrapped(*refs):
        ins, refs = refs[:n_in], refs[n_in:]
        r_ins, refs = refs[:r_in], refs[r_in:]
        outs, refs = refs[:n_out], refs[n_out:]
        r_outs, refs = refs[:r_out], refs[r_out:]
        scr, sems = refs[:n_scr], refs[n_scr:]
        if not grid:
            rider.start(r_ins, r_outs, sems)
            rider.finish(r_ins, r_outs, sems)
            return
        ids = [pl.program_id(a) for a in range(len(grid))]
        first = functools.reduce(jnp.logical_and, [i == 0 for i in ids])
        last = functools.reduce(jnp.logical_and, [i == g - 1 for i, g in zip(ids, grid)])

        @pl.when(first)
        def _():
            rider.start(r_ins, r_outs, sems)

        body(*ins, *outs, *scr)

        @pl.when(last)
        def _():
            rider.finish(r_ins, r_outs, sems)

    results = pl.pallas_call(
        wrapped, out_shape=list(out_shape) + list(rider.out_shapes), grid=grid,
        in_specs=list(in_specs) + [any_spec] * r_in, out_specs=list(out_specs) + [any_spec] * r_out,
        scratch_shapes=list(scratch_shapes) + list(rider.sems),
        input_output_aliases={n_in + k: n_out + v for k, v in rider.aliases.items()},
        compiler_params=_cparams(("arbitrary",) * len(grid) if grid else None, block_bytes, has_side_effects=True),
        name=name)(*operands, *rider.operands)
    return results[:n_out], results[n_out:]


def _exchange(rider, name):
    return _run(None, name=name, grid=(), in_specs=[], out_specs=[], out_shape=[], scratch_shapes=[], operands=[],
                block_bytes=0, rider=rider)[1]


def _gather_rider(items):
    bufs, index = [], []
    for b, r0, r1 in items:
        if not any(b is q for q in bufs):
            bufs.append(b)
        index.append(([k for k, q in enumerate(bufs) if q is b][0], r0, r1))
    n = len(index)

    def piece(refs, k, chip, half):
        bi, r0, r1 = index[k]
        return refs[bi].at[chip, half, pl.ds(r0, r1 - r0)]

    def copy(ref, sem_pair, k, j, to):
        return pltpu.make_async_remote_copy(ref, ref, sem_pair[0].at[k, j], sem_pair[1].at[k, j], device_id=to, device_id_type=MESH)

    def start(r_ins, buf, sems):
        x, y, c, others = _place()
        for k in range(n):
            for j, (ox, oy) in enumerate(others):
                copy(piece(buf, k, 2 * x + y, c), sems[:2], k, j, (ox, oy, c)).start()

    def finish(r_ins, buf, sems):
        x, y, c, others = _place()
        for k in range(n):
            for j, (ox, oy) in enumerate(others):
                got = piece(buf, k, 2 * ox + oy, c)
                copy(got, sems[:2], k, j, (ox, oy, c)).wait_recv()
                copy(got, sems[2:], k, j, (x, y, 1 - c)).start()
        for k in range(n):
            for j, (ox, oy) in enumerate(others):
                copy(piece(buf, k, 2 * ox + oy, 1 - c), sems[2:], k, j, (x, y, 1 - c)).wait_recv()
        for k in range(n):
            for j, (ox, oy) in enumerate(others):
                copy(piece(buf, k, 2 * x + y, c), sems[:2], k, j, (ox, oy, c)).wait_send()
                copy(piece(buf, k, 2 * ox + oy, c), sems[2:], k, j, (x, y, 1 - c)).wait_send()

    return _Rider(bufs, [jax.ShapeDtypeStruct(b.shape, b.dtype) for b in bufs], {i: i for i in range(len(bufs))},
                  [pltpu.SemaphoreType.DMA((n, 3))] * 4, start, finish)


def _scatter_rider(parts):
    n = len(parts)

    def copy(src, dst, sems, i, j, to):
        return pltpu.make_async_remote_copy(src, dst, sems[0].at[i, j], sems[1].at[i, j], device_id=to, device_id_type=MESH)

    def start(src, dst, sems):
        x, y, c, others = _place()
        for i in range(n):
            for j, (ox, oy) in enumerate(others):
                copy(src[i].at[2 * ox + oy], dst[i].at[j], sems, i, j, (ox, oy, c)).start()

    def finish(src, dst, sems):
        x, y, c, others = _place()
        for i in range(n):
            for j, (ox, oy) in enumerate(others):
                copy(src[i].at[2 * ox + oy], dst[i].at[j], sems, i, j, (ox, oy, c)).wait()

    return _Rider(parts, [jax.ShapeDtypeStruct((3,) + p.shape[1:], p.dtype) for p in parts], {},
                  [pltpu.SemaphoreType.DMA((n, 3))] * 2, start, finish)


def _matmul(pairs, mode, out_dtype, name, *, scale=1.0, residual=None, b3=False, out3=0, caps=(1024, 1024, 512), rider=None):
    a0, b0 = pairs[0]
    if mode == 'nn':
        m, k = a0.shape
        n = b0.shape[0] * b0.shape[2] if b3 else b0.shape[1]
    elif mode == 'nt':
        m = a0.shape[0]
        n, k = (b0.shape[1], b0.shape[0] * b0.shape[2]) if b3 else b0.shape
    else:
        k, m = a0.shape
        n = b0.shape[1]
    tm = _tile(m, caps[0])
    tn = _tile(n, caps[1])
    tk = _tile(k, caps[2])
    if b3 and mode == 'nn':
        tn = b0.shape[2]
    if b3 and mode == 'nt':
        tk = b0.shape[2]
    if out3:
        tn = n // out3
    nk = k // tk
    if mode == 'tn':
        a_spec = pl.BlockSpec((tk, tm), lambda i, j, kk: (kk, i))
        b_spec = pl.BlockSpec((tk, tn), lambda i, j, kk: (kk, j))
        dims = ((0,), (0,))
    elif mode == 'nn':
        a_spec = pl.BlockSpec((tm, tk), lambda i, j, kk: (i, kk))
        b_spec = (pl.BlockSpec((None, tk, tn), lambda i, j, kk: (j, kk, 0)) if b3
                  else pl.BlockSpec((tk, tn), lambda i, j, kk: (kk, j)))
        dims = ((1,), (0,))
    else:
        a_spec = pl.BlockSpec((tm, tk), lambda i, j, kk: (i, kk))
        b_spec = (pl.BlockSpec((None, tn, tk), lambda i, j, kk: (kk, j, 0)) if b3
                  else pl.BlockSpec((tn, tk), lambda i, j, kk: (j, kk)))
        dims = ((1,), (1,))
    in_specs, operands = [], []
    for a, b in pairs:
        in_specs += [a_spec, b_spec]
        operands += [a, b]
    block_bytes = len(pairs) * (_nbytes((tm, tk), a0.dtype) + _nbytes((tk, tn), b0.dtype))
    if residual is not None:
        in_specs.append(pl.BlockSpec((tm, tn), lambda i, j, kk: (i, j)))
        operands.append(residual)
        block_bytes += _nbytes((tm, tn), F32)
    if out3:
        out_spec = pl.BlockSpec((None, tm, tn), lambda i, j, kk: (j, i, 0))
        out_shape = jax.ShapeDtypeStruct((out3, m, tn), out_dtype)
    else:
        out_spec = pl.BlockSpec((tm, tn), lambda i, j, kk: (i, j))
        out_shape = jax.ShapeDtypeStruct((m, n), out_dtype)
    block_bytes += _nbytes((tm, tn), out_dtype) + _nbytes((tm, tn), F32)
    n_pairs = len(pairs)
    has_res = residual is not None

    def body(*refs):
        o_ref, acc = refs[-2], refs[-1]
        kk = pl.program_id(2)

        def product():
            part = None
            for p in range(n_pairs):
                d = lax.dot_general(refs[2 * p][...].astype(BF16), refs[2 * p + 1][...].astype(BF16),
                                    (dims, ((), ())), preferred_element_type=F32)
                part = d if part is None else part + d
            return part

        def finish(r):
            if scale != 1.0:
                r = r * scale
            if has_res:
                r = refs[2 * n_pairs][...] + r
            o_ref[...] = r.astype(out_dtype)

        if nk == 1:
            finish(product())
            return

        @pl.when(kk == 0)
        def _():
            acc[...] = product()

        if nk > 2:
            @pl.when(jnp.logical_and(kk > 0, kk < nk - 1))
            def _():
                acc[...] += product()

        @pl.when(kk == nk - 1)
        def _():
            finish(acc[...] + product())

    res = _run(body, name=name, grid=(m // tm, n // tn, nk), in_specs=in_specs, out_specs=[out_spec], out_shape=[out_shape],
               scratch_shapes=[pltpu.VMEM((tm, tn), F32)], operands=operands, block_bytes=block_bytes, rider=rider)
    return res[0] if rider is None else (res[0][0], res[1])


def _rmsnorm_fwd(x, g, name):
    t, d = x.shape
    tm = _rows(t, 512)

    def body(x_ref, g_ref, o_ref):
        xv = x_ref[...]
        r = lax.rsqrt(jnp.mean(xv * xv, axis=1, keepdims=True) + NORM_EPS)
        o_ref[...] = (xv * r * g_ref[...]).astype(BF16)

    row = pl.BlockSpec((tm, d), lambda i: (i, 0))
    return pl.pallas_call(
        body, out_shape=jax.ShapeDtypeStruct((t, d), BF16), grid=(t // tm,),
        in_specs=[row, pl.BlockSpec((1, d), lambda i: (0, 0))], out_specs=row,
        compiler_params=_cparams(("parallel",), 2 * _nbytes((tm, d), F32)), name=name)(x, g)


def _rms_grad(xv, g, dn, d):
    r = lax.rsqrt(jnp.mean(xv * xv, axis=1, keepdims=True) + NORM_EPS)
    u = dn * g
    s = jnp.sum(xv * u, axis=1, keepdims=True)
    dx = r * u - xv * (r * r * r) * (s * (1.0 / d))
    return dx, dn * xv * r


def _rmsnorm_bwd(x, g, dn, dres, name):
    t, d = x.shape
    tm = _rows(t, 256)

    def body(x_ref, g_ref, dn_ref, dres_ref, dx_ref, dxb_ref, dg_ref):
        dx, dg_rows = _rms_grad(x_ref[...], g_ref[...], dn_ref[...].astype(F32), d)
        dx = dres_ref[...] + dx
        dx_ref[...] = dx
        dxb_ref[...] = dx.astype(BF16)

        @pl.when(pl.program_id(0) == 0)
        def _():
            dg_ref[...] = jnp.zeros_like(dg_ref)

        dg_ref[...] += jnp.sum(dg_rows, axis=0, keepdims=True)

    row = pl.BlockSpec((tm, d), lambda i: (i, 0))
    vec = pl.BlockSpec((1, d), lambda i: (0, 0))
    return pl.pallas_call(
        body, out_shape=(jax.ShapeDtypeStruct((t, d), F32), jax.ShapeDtypeStruct((t, d), BF16), jax.ShapeDtypeStruct((1, d), F32)),
        grid=(t // tm,), in_specs=[row, vec, row, row], out_specs=(row, row, vec),
        compiler_params=_cparams(("arbitrary",), 5 * _nbytes((tm, d), F32)), name=name)(x, g, dn, dres)


def _final_loss(x, g, target, name):
    t, d = x.shape
    tm = _rows(t, 256)

    def body(x_ref, g_ref, t_ref, loss_ref, dx_ref, dxb_ref, dg_ref):
        xv, gv = x_ref[...], g_ref[...]
        r = lax.rsqrt(jnp.mean(xv * xv, axis=1, keepdims=True) + NORM_EPS)
        err = xv * r * gv - t_ref[...]
        dx, dg_rows = _rms_grad(xv, gv, err * (1.0 / d), d)
        dx_ref[...] = dx
        dxb_ref[...] = dx.astype(BF16)

        @pl.when(pl.program_id(0) == 0)
        def _():
            dg_ref[...] = jnp.zeros_like(dg_ref)
            loss_ref[...] = jnp.zeros_like(loss_ref)

        dg_ref[...] += jnp.sum(dg_rows, axis=0, keepdims=True)
        row_loss = jnp.sum(err * err, axis=1, keepdims=True) * (0.5 / d)
        loss_ref[...] += jnp.sum(row_loss, axis=0, keepdims=True)

    row = pl.BlockSpec((tm, d), lambda i: (i, 0))
    vec = pl.BlockSpec((1, d), lambda i: (0, 0))
    return pl.pallas_call(
        body, out_shape=(jax.ShapeDtypeStruct((1, 1), F32), jax.ShapeDtypeStruct((t, d), F32),
                         jax.ShapeDtypeStruct((t, d), BF16), jax.ShapeDtypeStruct((1, d), F32)),
        grid=(t // tm,), in_specs=[row, vec, row], out_specs=(pl.BlockSpec((1, 1), lambda i: (0, 0)), row, row, vec),
        compiler_params=_cparams(("arbitrary",), 4 * _nbytes((tm, d), F32)), name=name)(x, g, target)


def _ffn_up(n, wg, wu, name, rider=None):
    t, d = n.shape
    s, _, f = wg.shape
    tm, tk = _tile(t, 1024), _tile(d, 1024)
    nk = d // tk

    def body(n_ref, wg_ref, wu_ref, a_ref, b_ref, h_ref, acc_g, acc_u):
        kk = pl.program_id(2)

        def products():
            nv = n_ref[...]
            return jnp.dot(nv, wg_ref[...], preferred_element_type=F32), jnp.dot(nv, wu_ref[...], preferred_element_type=F32)

        def finish(a, b):
            a_ref[...] = a.astype(BF16)
            b_ref[...] = b.astype(BF16)
            h_ref[...] = (a * jax.nn.sigmoid(a) * b).astype(BF16)

        if nk == 1:
            finish(*products())
            return

        @pl.when(kk == 0)
        def _():
            acc_g[...], acc_u[...] = products()

        if nk > 2:
            @pl.when(jnp.logical_and(kk > 0, kk < nk - 1))
            def _():
                pg, pu = products()
                acc_g[...] += pg
                acc_u[...] += pu

        @pl.when(kk == nk - 1)
        def _():
            pg, pu = products()
            finish(acc_g[...] + pg, acc_u[...] + pu)

    w_spec = pl.BlockSpec((None, tk, f), lambda i, j, kk: (j, kk, 0))
    o_spec = pl.BlockSpec((tm, f), lambda i, j, kk: (i, j))
    out = jax.ShapeDtypeStruct((t, s * f), BF16)
    block_bytes = _nbytes((tm, tk), BF16) + 2 * _nbytes((tk, f), BF16) + 3 * _nbytes((tm, f), BF16) + 2 * _nbytes((tm, f), F32)
    return _run(body, name=name, grid=(t // tm, s, nk),
                in_specs=[pl.BlockSpec((tm, tk), lambda i, j, kk: (i, kk)), w_spec, w_spec], out_specs=[o_spec, o_spec, o_spec],
                out_shape=[out, out, out], scratch_shapes=[pltpu.VMEM((tm, f), F32), pltpu.VMEM((tm, f), F32)],
                operands=[n, wg, wu], block_bytes=block_bytes, rider=rider)


def _ffn_bwd_act(dx, wd, a, b, name):
    t, d = dx.shape
    f = wd.shape[0]
    tm, tn, tk = _tile(t, 1024), _tile(f, 1536), _tile(d, 1024)
    nk = d // tk

    def body(dx_ref, wd_ref, a_ref, b_ref, da_ref, db_ref, acc):
        kk = pl.program_id(2)

        def product():
            return lax.dot_general(dx_ref[...], wd_ref[...], (((1,), (1,)), ((), ())), preferred_element_type=F32)

        def finish(r):
            dh = 0.5 * r
            av, bv = a_ref[...].astype(F32), b_ref[...].astype(F32)
            sg = jax.nn.sigmoid(av)
            da_ref[...] = (dh * bv * (sg * (1.0 + av * (1.0 - sg)))).astype(BF16)
            db_ref[...] = (dh * (av * sg)).astype(BF16)

        if nk == 1:
            finish(product())
            return

        @pl.when(kk == 0)
        def _():
            acc[...] = product()

        if nk > 2:
            @pl.when(jnp.logical_and(kk > 0, kk < nk - 1))
            def _():
                acc[...] += product()

        @pl.when(kk == nk - 1)
        def _():
            finish(acc[...] + product())

    act = pl.BlockSpec((tm, tn), lambda i, j, kk: (i, j))
    out = jax.ShapeDtypeStruct((t, f), BF16)
    block_bytes = _nbytes((tm, tk), BF16) + _nbytes((tn, tk), BF16) + 4 * _nbytes((tm, tn), BF16) + _nbytes((tm, tn), F32)
    return pl.pallas_call(
        body, out_shape=(out, out), grid=(t // tm, f // tn, nk),
        in_specs=[pl.BlockSpec((tm, tk), lambda i, j, kk: (i, kk)), pl.BlockSpec((tn, tk), lambda i, j, kk: (j, kk)),
                  act, act],
        out_specs=(act, act), scratch_shapes=[pltpu.VMEM((tm, tn), F32)],
        compiler_params=_cparams(("parallel", "parallel", "arbitrary"), block_bytes), name=name)(dx, wd, a, b)


AXIS = dict(BIG)


def _full(wb, n):
    _, _, r, ccols = wb[n].shape
    return wb[n].reshape(N_CHIPS, 2 * r, ccols) if AXIS[n] == 1 else wb[n].reshape(N_CHIPS * 2 * r, ccols)


def _gather(wb, specs):
    items, names = [], []
    for s in specs:
        n, r0, r1 = (s, 0, wb[s].shape[2]) if isinstance(s, str) else s
        items.append((wb[n], r0, r1))
        if n not in names:
            names.append(n)
    return _gather_rider(items), names


def _landed(wb, names, results):
    for n, r in zip(names, results):
        wb[n] = r


def _reduce_first(grads, names, wb, c_idx):
    g4 = [g.reshape(wb[n].shape) for g, n in zip(grads, names)]
    from_sibling = _sibling_exchange(g4, "rs_sibling_" + names[0])
    return [_sibling_sum(a, b, c_idx, f"rs_sum1_{n}") for a, b, n in zip(g4, from_sibling, names)]


def _ffn_forward(x, gain, wb, tag, up_specs, down_specs):
    n = _rmsnorm_fwd(x, gain, f"{tag}_norm")
    rider, names = _gather(wb, up_specs)
    (a, b, h), got = _ffn_up(n, _full(wb, f"{tag}_w_gate"), _full(wb, f"{tag}_w_up"), f"{tag}_up", rider=rider)
    _landed(wb, names, got)
    down = dict(scale=0.5, residual=x, caps=(1024, 1024, 1536))
    if down_specs:
        rider, names = _gather(wb, down_specs)
        x_next, got = _matmul([(h, _full(wb, f"{tag}_w_down"))], 'nn', F32, f"{tag}_down", rider=rider, **down)
        _landed(wb, names, got)
    else:
        x_next = _matmul([(h, _full(wb, f"{tag}_w_down"))], 'nn', F32, f"{tag}_down", **down)
    return x_next, (n, a, b, h)


def _ffn_backward(x, gain, wb, saved, dx_next, dx_next_b, c_idx, tag):
    n, a, b, h = saved
    wg, wu, wd = (f"{tag}_w_gate", f"{tag}_w_up", f"{tag}_w_down")
    da, db = _ffn_bwd_act(dx_next_b, _full(wb, wd), a, b, f"{tag}_bwd_act")
    g_wd = _matmul([(h, dx_next_b)], 'tn', BF16, f"{tag}_dwd", scale=0.5, caps=(1536, 1024, 1024))
    (p_wd,) = _reduce_first([g_wd], [wd], wb, c_idx)
    g_wg, (r_wd,) = _matmul([(n, da)], 'tn', BF16, f"{tag}_dwg", out3=N_CHIPS, caps=(1024, 1024, 1024), rider=_scatter_rider([p_wd]))
    g_wu = _matmul([(n, db)], 'tn', BF16, f"{tag}_dwu", out3=N_CHIPS, caps=(1024, 1024, 1024))
    p_wg, p_wu = _reduce_first([g_wg, g_wu], [wg, wu], wb, c_idx)
    dn, (r_wg, r_wu) = _matmul([(da, _full(wb, wg)), (db, _full(wb, wu))], 'nt', F32, f"{tag}_dn", b3=True,
                               rider=_scatter_rider([p_wg, p_wu]))
    dx, dx_b, g_gain = _rmsnorm_bwd(x, gain, dn, dx_next, f"{tag}_norm_bwd")
    return dx, dx_b, g_gain, {wg: (p_wg, r_wg), wu: (p_wu, r_wu), wd: (p_wd, r_wd)}


def _rope_tables(seq):
    half = ROPE_DIM // 2
    inv_freq = ROPE_THETA ** (-jnp.arange(0, ROPE_DIM, 2, dtype=F32) / ROPE_DIM)
    ang = jnp.arange(seq).astype(F32)[:, None] * inv_freq[None, :]
    cos, sin = jnp.cos(ang), jnp.sin(ang)
    zeros = lambda w: jnp.zeros((seq, w), F32)
    c = jnp.concatenate([cos, cos, jnp.ones((seq, HEAD_DIM - ROPE_DIM), F32)], axis=1)
    s_up = jnp.concatenate([-sin, zeros(HEAD_DIM - half)], axis=1)
    s_dn = jnp.concatenate([zeros(half), sin, zeros(HEAD_DIM - ROPE_DIM)], axis=1)
    return c, s_up, s_dn


def _rope(x, width, tables, name):
    t = x.shape[0]
    tm = _rows(t, 512)
    half = ROPE_DIM // 2
    c, s_up, s_dn = tables

    def body(x_ref, c_ref, up_ref, dn_ref, o_ref):
        cv, uv, dv = c_ref[...], up_ref[...], dn_ref[...]
        for h in range(GROUP_WIDTH // HEAD_DIM):
            sl = slice(h * HEAD_DIM, (h + 1) * HEAD_DIM)
            xv = x_ref[:, sl].astype(F32)
            o_ref[:, sl] = (xv * cv + pltpu.roll(xv, HEAD_DIM - half, 1) * uv + pltpu.roll(xv, half, 1) * dv).astype(BF16)

    blk = pl.BlockSpec((tm, GROUP_WIDTH), lambda i, j: (i, j))
    tab = pl.BlockSpec((tm, HEAD_DIM), lambda i, j: (i, 0))
    return pl.pallas_call(
        body, out_shape=jax.ShapeDtypeStruct((t, width), BF16), grid=(t // tm, width // GROUP_WIDTH),
        in_specs=[blk, tab, tab, tab], out_specs=blk,
        compiler_params=_cparams(("parallel", "parallel"), 2 * _nbytes((tm, GROUP_WIDTH), F32)), name=name)(x, c, s_up, s_dn)


def _att_masks():
    qi = lax.broadcasted_iota(jnp.int32, (ATT_BLOCK, ATT_BLOCK), 0)
    kj = lax.broadcasted_iota(jnp.int32, (ATT_BLOCK, ATT_BLOCK), 1)
    return kj >= qi, kj <= qi


def _scores(q, k):
    return lax.dot_general(q, k, (((1,), (1,)), ((), ())), preferred_element_type=F32) * (HEAD_DIM ** -0.5)


def _att_fwd(q, k, v, offs, dil, name):
    qo, ko, vo = offs
    length = q.shape[0]
    nb = length // ATT_BLOCK

    def body(q_ref, kp_ref, kc_ref, vp_ref, vc_ref, o_ref, lse_ref):
        has_prev = pl.program_id(1) > 0
        m_prev, m_cur = _att_masks()
        m_prev = jnp.logical_and(m_prev, has_prev)
        for h in range(HEADS_PER_GROUP):
            sl = slice(h * HEAD_DIM, (h + 1) * HEAD_DIM)
            qv = q_ref[:, sl]
            s_p = jnp.where(m_prev, _scores(qv, kp_ref[:, sl]), MASKED)
            s_c = jnp.where(m_cur, _scores(qv, kc_ref[:, sl]), MASKED)
            m = jnp.maximum(jnp.max(s_p, axis=1, keepdims=True), jnp.max(s_c, axis=1, keepdims=True))
            p_p, p_c = jnp.exp(s_p - m), jnp.exp(s_c - m)
            l = jnp.sum(p_p, axis=1, keepdims=True) + jnp.sum(p_c, axis=1, keepdims=True)
            acc = jnp.dot(p_p.astype(BF16), vp_ref[:, sl], preferred_element_type=F32)
            acc += jnp.dot(p_c.astype(BF16), vc_ref[:, sl], preferred_element_type=F32)
            o_ref[:, sl] = acc / l
            lse_ref[:, sl] = jnp.broadcast_to(m + jnp.log(l), (ATT_BLOCK, HEAD_DIM))

    def spec(off, prev):
        if prev:
            return pl.BlockSpec((ATT_BLOCK, GROUP_WIDTH), lambda r, n: (jnp.maximum(n - 1, 0), off + r))
        return pl.BlockSpec((ATT_BLOCK, GROUP_WIDTH), lambda r, n: (n, off + r))

    out = jax.ShapeDtypeStruct((length, dil * GROUP_WIDTH), F32)
    o_spec = pl.BlockSpec((ATT_BLOCK, GROUP_WIDTH), lambda r, n: (n, r))
    return pl.pallas_call(
        body, out_shape=(out, out), grid=(dil, nb),
        in_specs=[spec(qo, False), spec(ko, True), spec(ko, False), spec(vo, True), spec(vo, False)],
        out_specs=(o_spec, o_spec),
        compiler_params=_cparams(("parallel", "parallel"), 8 * _nbytes((ATT_BLOCK, GROUP_WIDTH), F32)), name=name)(q, k, k, v, v)


def _att_combine(outs, lses, name):
    t = outs[0].shape[0]
    tm = _rows(t, 512)

    def body(*refs):
        o_refs, l_refs = refs[:N_GROUPS], refs[N_GROUPS:2 * N_GROUPS]
        ob_ref, of_ref, lse_ref = refs[2 * N_GROUPS:]
        ls = [r[...] for r in l_refs]
        m = functools.reduce(jnp.maximum, ls)
        ws = [jnp.exp(l - m) for l in ls]
        den = functools.reduce(jnp.add, ws)
        num = functools.reduce(jnp.add, [w * r[...] for w, r in zip(ws, o_refs)])
        o = num / den
        ob_ref[...] = o.astype(BF16)
        of_ref[...] = o
        lse_ref[...] = m + jnp.log(den)

    blk = pl.BlockSpec((tm, GROUP_WIDTH), lambda i: (i, 0))
    f32 = jax.ShapeDtypeStruct((t, GROUP_WIDTH), F32)
    return pl.pallas_call(
        body, out_shape=(jax.ShapeDtypeStruct((t, GROUP_WIDTH), BF16), f32, f32), grid=(t // tm,),
        in_specs=[blk] * (2 * N_GROUPS), out_specs=(blk, blk, blk),
        compiler_params=_cparams(("parallel",), 9 * _nbytes((tm, GROUP_WIDTH), F32)), name=name)(*outs, *lses)


def _att_delta(do, o, name):
    t = o.shape[0]
    tm = _rows(t, 512)

    def body(do_ref, o_ref, d_ref):
        for h in range(HEADS_PER_GROUP):
            sl = slice(h * HEAD_DIM, (h + 1) * HEAD_DIM)
            s = jnp.sum(do_ref[:, sl] * o_ref[:, sl], axis=1, keepdims=True)
            d_ref[:, sl] = jnp.broadcast_to(s, (tm, HEAD_DIM))

    blk = pl.BlockSpec((tm, GROUP_WIDTH), lambda i: (i, 0))
    return pl.pallas_call(
        body, out_shape=jax.ShapeDtypeStruct((t, GROUP_WIDTH), F32), grid=(t // tm,), in_specs=[blk, blk], out_specs=blk,
        compiler_params=_cparams(("parallel",), 3 * _nbytes((tm, GROUP_WIDTH), F32)), name=name)(do, o)


def _att_bwd_dq(q, k, v, do, lse, delta, offs, dil, name):
    qo, ko, vo = offs
    length = q.shape[0]
    nb = length // ATT_BLOCK
    scale = HEAD_DIM ** -0.5

    def body(q_ref, kp_ref, kc_ref, vp_ref, vc_ref, do_ref, lse_ref, dl_ref, dq_ref):
        has_prev = pl.program_id(1) > 0
        m_prev, m_cur = _att_masks()
        m_prev = jnp.logical_and(m_prev, has_prev)
        for h in range(HEADS_PER_GROUP):
            sl = slice(h * HEAD_DIM, (h + 1) * HEAD_DIM)
            qv, dov = q_ref[:, sl], do_ref[:, sl].astype(BF16)
            lsev, dlv = lse_ref[:, sl], dl_ref[:, sl]
            dq = None
            for mask, k_ref, v_ref in ((m_prev, kp_ref, vp_ref), (m_cur, kc_ref, vc_ref)):
                kv = k_ref[:, sl]
                p = jnp.exp(jnp.where(mask, _scores(qv, kv), MASKED) - lsev)
                dp = lax.dot_general(dov, v_ref[:, sl], (((1,), (1,)), ((), ())), preferred_element_type=F32)
                ds = (p * (dp - dlv) * scale).astype(BF16)
                part = jnp.dot(ds, kv, preferred_element_type=F32)
                dq = part if dq is None else dq + part
            dq_ref[:, sl] = dq.astype(BF16)

    def spec(off, prev):
        if prev:
            return pl.BlockSpec((ATT_BLOCK, GROUP_WIDTH), lambda r, n: (jnp.maximum(n - 1, 0), off + r))
        return pl.BlockSpec((ATT_BLOCK, GROUP_WIDTH), lambda r, n: (n, off + r))

    own = pl.BlockSpec((ATT_BLOCK, GROUP_WIDTH), lambda r, n: (n, r))
    return pl.pallas_call(
        body, out_shape=jax.ShapeDtypeStruct((length, dil * GROUP_WIDTH), BF16), grid=(dil, nb),
        in_specs=[spec(qo, False), spec(ko, True), spec(ko, False), spec(vo, True), spec(vo, False), own, own, own],
        out_specs=own,
        compiler_params=_cparams(("parallel", "parallel"), 10 * _nbytes((ATT_BLOCK, GROUP_WIDTH), F32)),
        name=name)(q, k, k, v, v, do, lse, delta)


def _att_bwd_dkv(q, k, v, do, lse, delta, offs, dil, name):
    qo, ko, vo = offs
    length = q.shape[0]
    nb = length // ATT_BLOCK
    scale = HEAD_DIM ** -0.5

    def body(k_ref, v_ref, qc_ref, qn_ref, doc_ref, don_ref, lsec_ref, lsen_ref, dlc_ref, dln_ref, dk_ref, dv_ref):
        has_next = pl.program_id(1) < nb - 1
        m_prev, m_cur = _att_masks()
        m_prev = jnp.logical_and(m_prev, has_next)
        for h in range(HEADS_PER_GROUP):
            sl = slice(h * HEAD_DIM, (h + 1) * HEAD_DIM)
            kv, vv = k_ref[:, sl], v_ref[:, sl]
            dk = dv = None
            for mask, q_ref, do_ref, lse_ref, dl_ref in ((m_cur, qc_ref, doc_ref, lsec_ref, dlc_ref),
                                                         (m_prev, qn_ref, don_ref, lsen_ref, dln_ref)):
                qv, dov = q_ref[:, sl], do_ref[:, sl].astype(BF16)
                p = jnp.exp(jnp.where(mask, _scores(qv, kv), MASKED) - lse_ref[:, sl])
                dp = lax.dot_general(dov, vv, (((1,), (1,)), ((), ())), preferred_element_type=F32)
                ds = (p * (dp - dl_ref[:, sl]) * scale).astype(BF16)
                dv_part = lax.dot_general(p.astype(BF16), dov, (((0,), (0,)), ((), ())), preferred_element_type=F32)
                dk_part = lax.dot_general(ds, qv, (((0,), (0,)), ((), ())), preferred_element_type=F32)
                dv = dv_part if dv is None else dv + dv_part
                dk = dk_part if dk is None else dk + dk_part
            dk_ref[:, sl] = dk.astype(BF16)
            dv_ref[:, sl] = dv.astype(BF16)

    def spec(off, nxt):
        if nxt:
            return pl.BlockSpec((ATT_BLOCK, GROUP_WIDTH), lambda r, n: (jnp.minimum(n + 1, nb - 1), off + r))
        return pl.BlockSpec((ATT_BLOCK, GROUP_WIDTH), lambda r, n: (n, off + r))

    own = pl.BlockSpec((ATT_BLOCK, GROUP_WIDTH), lambda r, n: (n, r))
    out = jax.ShapeDtypeStruct((length, dil * GROUP_WIDTH), BF16)
    return pl.pallas_call(
        body, out_shape=(out, out), grid=(dil, nb),
        in_specs=[spec(ko, False), spec(vo, False), spec(qo, False), spec(qo, True), spec(0, False), spec(0, True),
                  spec(0, False), spec(0, True), spec(0, False), spec(0, True)],
        out_specs=(own, own),
        compiler_params=_cparams(("parallel", "parallel"), 12 * _nbytes((ATT_BLOCK, GROUP_WIDTH), F32)),
        name=name)(k, v, q, q, do, do, lse, lse, delta, delta)


def _gelu(x):
    return 0.5 * x * (1.0 + lax.erf(x * (2.0 ** -0.5)))


def _gelu_grad(x):
    return 0.5 * (1.0 + lax.erf(x * (2.0 ** -0.5))) + x * jnp.exp(-0.5 * x * x) * ((2.0 * jnp.pi) ** -0.5)


def _sg_normed(vs, lg, lb):
    gv = _gelu(vs)
    mu = jnp.mean(gv, axis=1, keepdims=True)
    xc = gv - mu
    rstd = lax.rsqrt(jnp.mean(xc * xc, axis=1, keepdims=True) + LN_EPS)
    z = xc * rstd
    return z, rstd, z * lg + lb


def _sg_tril():
    row = lax.broadcasted_iota(jnp.int32, (SG_CHUNK, SG_CHUNK), 0)
    col = lax.broadcasted_iota(jnp.int32, (SG_CHUNK, SG_CHUNK), 1)
    return row >= col


def _sg_fwd(proj, u_blk, vs_blk, lg, lb, sg_w, bias, name):
    t = proj.shape[0]
    width = SG_GROUPS * SG_GROUP_DIM

    def body(u_ref, vs_ref, lg_ref, lb_ref, w_ref, bias_ref, o_ref):
        _, _, vn = _sg_normed(vs_ref[...].astype(F32), lg_ref[...], lb_ref[...])
        vn = vn.astype(BF16)
        tril = _sg_tril()
        for g in range(SG_GROUPS):
            sl = slice(g * SG_GROUP_DIM, (g + 1) * SG_GROUP_DIM)
            w = jnp.where(tril, w_ref[g], 0.0).astype(BF16)
            sp = jnp.dot(w, vn[:, sl], preferred_element_type=F32) + bias_ref[:, sl]
            o_ref[:, sl] = (_gelu(u_ref[:, sl].astype(F32)) * sp).astype(BF16)

    vec = pl.BlockSpec((1, width), lambda i: (0, 0))
    return pl.pallas_call(
        body, out_shape=jax.ShapeDtypeStruct((t, width), BF16), grid=(t // SG_CHUNK,),
        in_specs=[pl.BlockSpec((SG_CHUNK, width), lambda i: (i, u_blk)), pl.BlockSpec((SG_CHUNK, width), lambda i: (i, vs_blk)),
                  vec, vec, pl.BlockSpec((SG_GROUPS, SG_CHUNK, SG_CHUNK), lambda i: (0, 0, 0)),
                  pl.BlockSpec((SG_CHUNK, width), lambda i: (0, 0))],
        out_specs=pl.BlockSpec((SG_CHUNK, width), lambda i: (i, 0)),
        compiler_params=_cparams(("parallel",), 8 * _nbytes((SG_CHUNK, width), F32)), name=name)(proj, proj, lg, lb, sg_w, bias)


def _sg_bwd(proj, u_blk, vs_blk, dsu, lg, lb, sg_w, bias, name):
    t = proj.shape[0]
    width = SG_GROUPS * SG_GROUP_DIM

    def body(u_ref, vs_ref, dsu_ref, lg_ref, lb_ref, w_ref, bias_ref, du_ref, dvs_ref, dw_ref, dbias_ref, dlg_ref, dlb_ref):
        @pl.when(pl.program_id(0) == 0)
        def _():
            dw_ref[...] = jnp.zeros_like(dw_ref)
            dbias_ref[...] = jnp.zeros_like(dbias_ref)
            dlg_ref[...] = jnp.zeros_like(dlg_ref)
            dlb_ref[...] = jnp.zeros_like(dlb_ref)

        vs = vs_ref[...].astype(F32)
        z, rstd, vn = _sg_normed(vs, lg_ref[...], lb_ref[...])
        vn = vn.astype(BF16)
        tril = _sg_tril()
        dvn = []
        for g in range(SG_GROUPS):
            sl = slice(g * SG_GROUP_DIM, (g + 1) * SG_GROUP_DIM)
            w = jnp.where(tril, w_ref[g], 0.0).astype(BF16)
            vg = vn[:, sl]
            sp = jnp.dot(w, vg, preferred_element_type=F32) + bias_ref[:, sl]
            uv = u_ref[:, sl].astype(F32)
            dsu_g = dsu_ref[:, sl].astype(F32)
            du_ref[:, sl] = (dsu_g * sp * _gelu_grad(uv)).astype(BF16)
            dsp = dsu_g * _gelu(uv)
            dsp_b = dsp.astype(BF16)
            dw = lax.dot_general(dsp_b, vg, (((1,), (1,)), ((), ())), preferred_element_type=F32)
            dw_ref[g] += jnp.where(tril, dw, 0.0)
            dbias_ref[:, sl] += jnp.broadcast_to(jnp.sum(dsp, axis=1, keepdims=True), (SG_CHUNK, SG_GROUP_DIM))
            dvn.append(lax.dot_general(w, dsp_b, (((0,), (0,)), ((), ())), preferred_element_type=F32))
        dvn = jnp.concatenate(dvn, axis=1)
        dlg_ref[...] += jnp.sum(dvn * z, axis=0, keepdims=True)
        dlb_ref[...] += jnp.sum(dvn, axis=0, keepdims=True)
        dz = dvn * lg_ref[...]
        dgv = rstd * (dz - jnp.mean(dz, axis=1, keepdims=True) - z * jnp.mean(dz * z, axis=1, keepdims=True))
        dvs_ref[...] = (dgv * _gelu_grad(vs)).astype(BF16)

    vec = pl.BlockSpec((1, width), lambda i: (0, 0))
    row = pl.BlockSpec((SG_CHUNK, width), lambda i: (i, 0))
    fixed = pl.BlockSpec((SG_CHUNK, width), lambda i: (0, 0))
    w_spec = pl.BlockSpec((SG_GROUPS, SG_CHUNK, SG_CHUNK), lambda i: (0, 0, 0))
    act = jax.ShapeDtypeStruct((t, width), BF16)
    return pl.pallas_call(
        body,
        out_shape=(act, act, jax.ShapeDtypeStruct((SG_GROUPS, SG_CHUNK, SG_CHUNK), F32),
                   jax.ShapeDtypeStruct((SG_CHUNK, width), F32), jax.ShapeDtypeStruct((1, width), F32),
                   jax.ShapeDtypeStruct((1, width), F32)),
        grid=(t // SG_CHUNK,),
        in_specs=[pl.BlockSpec((SG_CHUNK, width), lambda i: (i, u_blk)), pl.BlockSpec((SG_CHUNK, width), lambda i: (i, vs_blk)),
                  row, vec, vec, w_spec, fixed],
        out_specs=(row, row, w_spec, fixed, vec, vec),
        compiler_params=_cparams(("arbitrary",), 14 * _nbytes((SG_CHUNK, width), F32)),
        name=name)(proj, proj, dsu, lg, lb, sg_w, bias)


def _gate_fwd(proj, ga_blk, gs_blk, y_att, y_sg, name):
    t, d = y_att.shape
    tm, tn = _rows(t, 512), _tile(d, GROUP_WIDTH)

    def body(ga_ref, gs_ref, ya_ref, ys_ref, o_ref):
        o_ref[...] = (jax.nn.sigmoid(ga_ref[...].astype(F32)) * ya_ref[...].astype(F32)
                      + jax.nn.sigmoid(gs_ref[...].astype(F32)) * ys_ref[...].astype(F32)).astype(BF16)

    own = pl.BlockSpec((tm, tn), lambda i, j: (i, j))
    return pl.pallas_call(
        body, out_shape=jax.ShapeDtypeStruct((t, d), BF16), grid=(t // tm, d // tn),
        in_specs=[pl.BlockSpec((tm, tn), lambda i, j: (i, ga_blk + j)), pl.BlockSpec((tm, tn), lambda i, j: (i, gs_blk + j)),
                  own, own],
        out_specs=own, compiler_params=_cparams(("parallel", "parallel"), 6 * _nbytes((tm, tn), F32)),
        name=name)(proj, proj, y_att, y_sg)


def _gate_bwd(proj, ga_blk, gs_blk, y_att, y_sg, dmerged, name):
    t, d = y_att.shape
    tm, tn = _rows(t, 512), _tile(d, GROUP_WIDTH)

    def body(ga_ref, gs_ref, ya_ref, ys_ref, dm_ref, dya_ref, dys_ref, dga_ref, dgs_ref):
        dm = dm_ref[...].astype(F32)
        for g_ref, y_ref, dy_ref, dg_ref in ((ga_ref, ya_ref, dya_ref, dga_ref), (gs_ref, ys_ref, dys_ref, dgs_ref)):
            sg = jax.nn.sigmoid(g_ref[...].astype(F32))
            dy_ref[...] = (dm * sg).astype(BF16)
            dg_ref[...] = (dm * y_ref[...].astype(F32) * sg * (1.0 - sg)).astype(BF16)

    own = pl.BlockSpec((tm, tn), lambda i, j: (i, j))
    out = jax.ShapeDtypeStruct((t, d), BF16)
    return pl.pallas_call(
        body, out_shape=(out, out, out, out), grid=(t // tm, d // tn),
        in_specs=[pl.BlockSpec((tm, tn), lambda i, j: (i, ga_blk + j)), pl.BlockSpec((tm, tn), lambda i, j: (i, gs_blk + j)),
                  own, own, own],
        out_specs=(own, own, own, own), compiler_params=_cparams(("parallel", "parallel"), 10 * _nbytes((tm, tn), F32)),
        name=name)(proj, proj, y_att, y_sg, dmerged)


def _group_view(arr, col, dil):
    t = arr.shape[0]
    return arr[:, col:col + GROUP_WIDTH].reshape(t // dil, dil * GROUP_WIDTH)


def _mixer_forward(x, wb, small, in_specs, out_specs):
    t, d = x.shape
    att_w = N_GROUPS * GROUP_WIDTH
    sg_w = SG_GROUPS * SG_GROUP_DIM
    n = _rmsnorm_fwd(x, small['mix_norm'], "mix_norm")
    rider, names = _gather(wb, in_specs)
    proj, got = _matmul([(n, _full(wb, 'w_in'))], 'nn', BF16, "mix_in", b3=True, caps=(1024, 1024, 1024), rider=rider)
    _landed(wb, names, got)
    tables = _rope_tables(t)
    qk = _rope(proj, 2 * att_w, tables, "mix_rope")
    outs, lses = [], []
    for gi, dil in enumerate(DILATIONS):
        if dil == 1:
            args = (qk, qk, proj, (gi, N_GROUPS + gi, 2 * N_GROUPS + gi))
        else:
            args = (_group_view(qk, gi * GROUP_WIDTH, dil), _group_view(qk, att_w + gi * GROUP_WIDTH, dil),
                    _group_view(proj, 2 * att_w + gi * GROUP_WIDTH, dil), (0, 0, 0))
        o, lse = _att_fwd(*args, dil, f"att_fwd{gi}")
        outs.append(o.reshape(t, GROUP_WIDTH))
        lses.append(lse.reshape(t, GROUP_WIDTH))
    o_b, o_f, lse = _att_combine(outs, lses, "att_combine")
    y_att = _matmul([(o_b, _full(wb, 'w_att_out'))], 'nn', BF16, "mix_att_out", b3=True)
    bias = jnp.repeat(small['sg_b'].T, SG_GROUP_DIM, axis=1)
    u_blk, vs_blk = 3 * att_w // sg_w, 3 * att_w // sg_w + 1
    su = _sg_fwd(proj, u_blk, vs_blk, small['sg_ln_g'], small['sg_ln_b'], small['sg_w'], bias, "sg_fwd")
    y_sg = _matmul([(su, _full(wb, 'w_sg_out'))], 'nn', BF16, "mix_sg_out", b3=True)
    ga_blk = (3 * att_w + 2 * sg_w) // _tile(d, GROUP_WIDTH)
    gs_blk = ga_blk + d // _tile(d, GROUP_WIDTH)
    merged = _gate_fwd(proj, ga_blk, gs_blk, y_att, y_sg, "gate_fwd")
    rider, names = _gather(wb, out_specs)
    x_next, got = _matmul([(merged, _full(wb, 'w_out'))], 'nn', F32, "mix_out", residual=x, rider=rider)
    _landed(wb, names, got)
    saved =(n, proj, qk, tables, o_b, o_f, lse, y_att, su, y_sg, merged, bias, (u_blk, vs_blk, ga_blk, gs_blk))
    return x_next, saved


def _mixer_backward(x, wb, small, saved, dx_next, dx_next_b, c_idx):
    n, proj, qk, tables, o_b, o_f, lse, y_att, su, y_sg, merged, bias, (u_blk, vs_blk, ga_blk, gs_blk) = saved
    t, d = x.shape
    att_w = N_GROUPS * GROUP_WIDTH
    s = N_CHIPS
    dmerged = _matmul([(dx_next_b, _full(wb, 'w_out'))], 'nt', BF16, "mix_out_dx")
    g_w_out = _matmul([(merged, dx_next_b)], 'tn', BF16, "mix_out_dw", caps=(1024, 1024, 1024))
    dy_att, dy_sg, dg_att, dg_sg = _gate_bwd(proj, ga_blk, gs_blk, y_att, y_sg, dmerged, "gate_bwd")

    g_w_att_out = _matmul([(o_b, dy_att)], 'tn', BF16, "mix_att_out_dw", out3=s)
    do = _matmul([(dy_att, _full(wb, 'w_att_out'))], 'nt', F32, "mix_att_out_dx", b3=True)
    delta = _att_delta(do, o_f, "att_delta")
    dqs, dks, dvs_ = [], [], []
    for gi, dil in enumerate(DILATIONS):
        if dil == 1:
            args = (qk, qk, proj, do, lse, delta, (gi, N_GROUPS + gi, 2 * N_GROUPS + gi))
        else:
            args = (_group_view(qk, gi * GROUP_WIDTH, dil), _group_view(qk, att_w + gi * GROUP_WIDTH, dil),
                    _group_view(proj, 2 * att_w + gi * GROUP_WIDTH, dil), _group_view(do, 0, dil),
                    _group_view(lse, 0, dil), _group_view(delta, 0, dil), (0, 0, 0))
        dq = _att_bwd_dq(*args, dil, f"att_bwd_dq{gi}")
        dk, dv = _att_bwd_dkv(*args, dil, f"att_bwd_dkv{gi}")
        dqs.append(dq.reshape(t, GROUP_WIDTH))
        dks.append(dk.reshape(t, GROUP_WIDTH))
        dvs_.append(dv.reshape(t, GROUP_WIDTH))
    c, s_up, s_dn = tables
    dqk = _rope(jnp.concatenate(dqs + dks, axis=1), 2 * att_w, (c, -s_up, -s_dn), "mix_rope_bwd")

    g_w_sg_out = _matmul([(su, dy_sg)], 'tn', BF16, "mix_sg_out_dw", out3=s)
    out_names = ['w_out', 'w_att_out', 'w_sg_out']
    out_parts = _reduce_first([g_w_out, g_w_att_out, g_w_sg_out], out_names, wb, c_idx)
    dsu = _matmul([(dy_sg, _full(wb, 'w_sg_out'))], 'nt', BF16, "mix_sg_out_dx", b3=True)
    du, dvs, g_sg_w, g_bias, g_lg, g_lb = _sg_bwd(proj, u_blk, vs_blk, dsu, small['sg_ln_g'], small['sg_ln_b'],
                                                   small['sg_w'], bias, "sg_bwd")
    gs = {'sg_w': g_sg_w, 'sg_b': g_bias[:, ::SG_GROUP_DIM].T, 'sg_ln_g': g_lg, 'sg_ln_b': g_lb}

    dproj = jnp.concatenate([dqk] + dvs_ + [du, dvs, dg_att, dg_sg], axis=1)
    g_w_in, out_recv = _matmul([(n, dproj)], 'tn', BF16, "mix_in_dw", out3=s, caps=(1024, 1024, 1024),
                               rider=_scatter_rider(out_parts))
    (p_w_in,) = _reduce_first([g_w_in], ['w_in'], wb, c_idx)
    dn, (r_w_in,) = _matmul([(dproj, _full(wb, 'w_in'))], 'nt', F32, "mix_in_dx", b3=True, caps=(1024, 1024, 512),
                            rider=_scatter_rider([p_w_in]))
    dx, dx_b, gs['mix_norm'] = _rmsnorm_bwd(x, small['mix_norm'], dn, dx_next, "mix_norm_bwd")
    g = {nm: (p, r) for nm, p, r in zip(out_names, out_parts, out_recv)}
    g['w_in'] = (p_w_in, r_w_in)
    return dx, dx_b, g, gs


def _step(x, target, wb, small, c_idx):
    wb = dict(wb)
    rider, names = _gather(wb, ['ffn1_w_gate', 'ffn1_w_up'])
    _landed(wb, names, _exchange(rider, "gather_first"))
    half_in = wb['w_in'].shape[2] // 2
    x1, s1 = _ffn_forward(x, small['ffn1_norm'], wb, "ffn1", ['ffn1_w_down', ('w_in', 0, half_in)], [('w_in', half_in, 2 * half_in)])
    x2, s2 = _mixer_forward(x1, wb, small, ['w_att_out', 'w_sg_out', 'w_out', 'ffn2_w_gate'], ['ffn2_w_up'])
    x3, s3 = _ffn_forward(x2, small['ffn2_norm'], wb, "ffn2", ['ffn2_w_down'], None)
    loss, dx3, dx3_b, g_final = _final_loss(x3, small['final_norm'], target, "final_loss")
    gs = {'final_norm': g_final}
    dx2, dx2_b, gs['ffn2_norm'], g = _ffn_backward(x2, small['ffn2_norm'], wb, s3, dx3, dx3_b, c_idx, "ffn2")
    dx1, dx1_b, g_mix, gs_mix = _mixer_backward(x1, wb, small, s2, dx2, dx2_b, c_idx)
    g.update(g_mix)
    gs.update(gs_mix)
    dx0, _, gs['ffn1_norm'], g_ffn1 = _ffn_backward(x, small['ffn1_norm'], wb, s1, dx1, dx1_b, c_idx, "ffn1")
    g.update(g_ffn1)
    return loss, dx0, g, gs


def _cast_into_gathered(wt, p_idx, name):
    r, ccols = wt.shape[0] // 2, wt.shape[1]
    tm = _rows(r, 256)
    nb = r // tm

    def body(p_ref, w_ref, o_ref):
        o_ref[...] = w_ref[...].astype(BF16)

    grid_spec = pltpu.PrefetchScalarGridSpec(
        num_scalar_prefetch=1, grid=(2, nb),
        in_specs=[pl.BlockSpec((tm, ccols), lambda h, i, pr: (h * nb + i, 0))],
        out_specs=pl.BlockSpec((None, None, tm, ccols), lambda h, i, pr: (pr[0], h, i, 0)))
    return pl.pallas_call(body, out_shape=jax.ShapeDtypeStruct((N_CHIPS, 2, r, ccols), BF16), grid_spec=grid_spec,
                          compiler_params=_cparams(("parallel", "parallel"), 2 * _nbytes((tm, ccols), F32)), name=name)(p_idx, wt)


def _sibling_exchange(grads, name):
    nw = len(grads)

    def body(*refs):
        src, dst = refs[:nw], refs[nw:2 * nw]
        send_sems, recv_sems = refs[2 * nw:]
        x, y, c, _ = _place()
        cps = []
        for i in range(nw):
            cp = pltpu.make_async_remote_copy(src[i].at[:, 1 - c], dst[i], send_sems.at[i], recv_sems.at[i],
                                              device_id=(x, y, 1 - c), device_id_type=MESH)
            cp.start()
            cps.append(cp)
        for cp in cps:
            cp.wait()

    any_spec = pl.BlockSpec(memory_space=pl.ANY)
    return pl.pallas_call(
        body, out_shape=[jax.ShapeDtypeStruct((g.shape[0],) + g.shape[2:], g.dtype) for g in grads],
        in_specs=[any_spec] * nw, out_specs=[any_spec] * nw,
        scratch_shapes=[pltpu.SemaphoreType.DMA((nw,)), pltpu.SemaphoreType.DMA((nw,))],
        compiler_params=pltpu.CompilerParams(has_side_effects=True), name=name)(*grads)


def _half_exchange(bufs):
    nw = len(bufs)

    def body(*refs):
        dst = refs[nw:2 * nw]
        send_sems, recv_sems = refs[2 * nw:]
        x, y, c, _ = _place()
        cps = []
        for i in range(nw):
            mine = dst[i].at[c]
            cp = pltpu.make_async_remote_copy(mine, mine, send_sems.at[i], recv_sems.at[i],
                                              device_id=(x, y, 1 - c), device_id_type=MESH)
            cp.start()
            cps.append(cp)
        for i, cp in enumerate(cps):
            cp.wait_send()
            theirs = dst[i].at[1 - c]
            pltpu.make_async_remote_copy(theirs, theirs, send_sems.at[i], recv_sems.at[i],
                                         device_id=(x, y, 1 - c), device_id_type=MESH).wait_recv()

    any_spec = pl.BlockSpec(memory_space=pl.ANY)
    return pl.pallas_call(
        body, out_shape=[jax.ShapeDtypeStruct(b.shape, b.dtype) for b in bufs],
        in_specs=[any_spec] * nw, out_specs=[any_spec] * nw, input_output_aliases={i: i for i in range(nw)},
        scratch_shapes=[pltpu.SemaphoreType.DMA((nw,)), pltpu.SemaphoreType.DMA((nw,))],
        compiler_params=pltpu.CompilerParams(has_side_effects=True), name="rs_halves")(*bufs)


def _sibling_sum(grad, recv, c_idx, name):
    s, _, r, ccols = grad.shape
    tm = _rows(r, 256)

    def body(c_ref, g_ref, r_ref, o_ref):
        o_ref[...] = (g_ref[...].astype(F32) + r_ref[...].astype(F32)).astype(BF16)

    grid_spec = pltpu.PrefetchScalarGridSpec(
        num_scalar_prefetch=1, grid=(s, r // tm),
        in_specs=[pl.BlockSpec((None, None, tm, ccols), lambda q, i, cr: (q, cr[0], i, 0)),
                  pl.BlockSpec((None, tm, ccols), lambda q, i, cr: (q, i, 0))],
        out_specs=pl.BlockSpec((None, tm, ccols), lambda q, i, cr: (q, i, 0)))
    return pl.pallas_call(body, out_shape=jax.ShapeDtypeStruct((s, r, ccols), BF16), grid_spec=grid_spec,
                          compiler_params=_cparams(("parallel", "parallel"), 4 * _nbytes((tm, ccols), F32)), name=name)(c_idx, grad, recv)


def _chip_sum(part, recv, pc_idx, name):
    _, r, ccols = part.shape
    tm = _rows(r, 256)

    def body(pc_ref, own_ref, r0_ref, r1_ref, r2_ref, o_ref):
        acc = own_ref[...].astype(F32) + r0_ref[...].astype(F32)
        acc = acc + r1_ref[...].astype(F32)
        o_ref[...] = acc + r2_ref[...].astype(F32)

    def slot(j):
        return pl.BlockSpec((None, tm, ccols), lambda i, pc: (j, i, 0))

    grid_spec = pltpu.PrefetchScalarGridSpec(
        num_scalar_prefetch=1, grid=(r // tm,),
        in_specs=[pl.BlockSpec((None, tm, ccols), lambda i, pc: (pc[0], i, 0)), slot(0), slot(1), slot(2)],
        out_specs=pl.BlockSpec((None, tm, ccols), lambda i, pc: (pc[1], i, 0)))
    return pl.pallas_call(body, out_shape=jax.ShapeDtypeStruct((2, r, ccols), F32), grid_spec=grid_spec,
                          compiler_params=_cparams(("parallel",), 6 * _nbytes((tm, ccols), F32)), name=name)(pc_idx, part, recv, recv, recv)


def _all_reduce_small(vec):
    _, r, _ = vec.shape

    def body(v_ref, o_ref, parts, send1, recv1, send2, recv2):
        x, y, c, _ = _place()
        me = 4 * x + 2 * y + c
        peers = []
        for k in range(1, N_DEV):
            px, py, pc = (1 - x if k & 4 else x, 1 - y if k & 2 else y, 1 - c if k & 1 else c)
            peers.append(((px, py, pc), 4 * px + 2 * py + pc))
        parts[me] = v_ref[me]
        cps = []
        for k, (peer, peer_id) in enumerate(peers):
            cp = pltpu.make_async_remote_copy(v_ref.at[peer_id], parts.at[me], send1.at[k], recv1.at[k],
                                              device_id=peer, device_id_type=MESH)
            cp.start()
            cps.append(cp)
        for cp in cps:
            cp.wait()
        acc = parts[0]
        for dev in range(1, N_DEV):
            acc = acc + parts[dev]
        o_ref[me] = acc
        cps = []
        for k, (peer, _) in enumerate(peers):
            cp = pltpu.make_async_remote_copy(o_ref.at[me], o_ref.at[me], send2.at[k], recv2.at[k],
                                              device_id=peer, device_id_type=MESH)
            cp.start()
            cps.append(cp)
        for cp in cps:
            cp.wait()

    vm = pl.BlockSpec(memory_space=pltpu.VMEM)
    sems = pltpu.SemaphoreType.DMA((N_DEV - 1,))
    return pl.pallas_call(
        body, out_shape=jax.ShapeDtypeStruct(vec.shape, F32), in_specs=[vm], out_specs=vm,
        scratch_shapes=[pltpu.VMEM((N_DEV, r, LANES), F32), sems, sems, sems, sems],
        compiler_params=pltpu.CompilerParams(vmem_limit_bytes=int(8 * _nbytes((N_DEV, r, LANES), F32))),
        name="all_reduce_small")(vec)


def _adamw(wt, g, m, v, name):
    r, ccols = wt.shape
    tm = _rows(r, max(8, (MIB // (4 * ccols)) // 8 * 8))
    blk = pl.BlockSpec((tm, ccols), lambda i: (i, 0))

    def body(w_ref, g_ref, m_ref, v_ref, go_ref, d_ref, mo_ref, vo_ref):
        gv = g_ref[...]
        go_ref[...] = gv
        mv = ADAM_B1 * m_ref[...] + (1.0 - ADAM_B1) * gv
        vv = ADAM_B2 * v_ref[...] + (1.0 - ADAM_B2) * (gv * gv)
        m_hat = mv / (1.0 - ADAM_B1 ** ADAM_STEP)
        v_hat = vv / (1.0 - ADAM_B2 ** ADAM_STEP)
        d_ref[...] = -ADAM_LR * (m_hat / (jnp.sqrt(v_hat) + ADAM_EPS) + ADAM_WD * w_ref[...])
        mo_ref[...] = mv
        vo_ref[...] = vv

    out = jax.ShapeDtypeStruct((r, ccols), F32)
    return pl.pallas_call(body, out_shape=(out, out, out, out), grid=(r // tm,), in_specs=[blk] * 4, out_specs=(blk,) * 4,
                          compiler_params=_cparams(("parallel",), 8 * _nbytes((tm, ccols), F32)), name=name)(wt, g, m, v)


def _as_rows(a):
    rows = a.reshape(-1, LANES)
    return jnp.pad(rows, ((0, -rows.shape[0] % 8), (0, 0)))


def kernel(x, ffn1_norm, ffn1_w_gate, ffn1_w_up, ffn1_w_down, mix_norm, w_in, sg_ln_g, sg_ln_b, sg_w, sg_b, w_att_out, w_sg_out, w_out, ffn2_norm, ffn2_w_gate, ffn2_w_up, ffn2_w_down, final_norm, loss_target, m_ffn1_norm, m_ffn1_w_gate, m_ffn1_w_up, m_ffn1_w_down, m_mix_norm, m_w_in, m_sg_ln_g, m_sg_ln_b, m_sg_w, m_sg_b, m_w_att_out, m_w_sg_out, m_w_out, m_ffn2_norm, m_ffn2_w_gate, m_ffn2_w_up, m_ffn2_w_down, m_final_norm, v_ffn1_norm, v_ffn1_w_gate, v_ffn1_w_up, v_ffn1_w_down, v_mix_norm, v_w_in, v_sg_ln_g, v_sg_ln_b, v_sg_w, v_sg_b, v_w_att_out, v_w_sg_out, v_w_out, v_ffn2_norm, v_ffn2_w_gate, v_ffn2_w_up, v_ffn2_w_down, v_final_norm):
    given = dict(locals())
    wts = {n: given[n] for n in WEIGHT_NAMES}
    ms = {n: given["m_" + n] for n in WEIGHT_NAMES}
    vs = {n: given["v_" + n] for n in WEIGHT_NAMES}
    t, d = x.shape[-2], x.shape[-1]
    xc, yc, cc = lax.axis_index("x"), lax.axis_index("y"), lax.axis_index("c")

    shard2d = {n: wts[n].reshape(wts[n].shape[-2:]) for n in BIG_NAMES}
    p_idx = jnp.reshape(2 * xc + yc, (1,)).astype(jnp.int32)
    c_idx = jnp.reshape(cc, (1,)).astype(jnp.int32)
    pc_idx = jnp.stack([2 * xc + yc, cc]).astype(jnp.int32)
    wb = {n: _cast_into_gathered(shard2d[n], p_idx, f"cast_{n}") for n in BIG_NAMES}

    small = {n: wts[n].reshape(-1, wts[n].shape[-1]) for n in SMALL_NAMES}
    small['sg_w'] = wts['sg_w'].reshape(wts['sg_w'].shape[-3:])
    loss, dx, g, gs = _step(x.reshape(t, d), loss_target.reshape(t, d), wb, small, c_idx)
    loss = lax.psum(loss[0, 0], ("x", "y", "c"))

    my_halves = [_chip_sum(*g[n], pc_idx, f"rs_sum2_{n}") for n in BIG_NAMES]
    reduced = _half_exchange(my_halves)
    grads = {n: r.reshape(shard2d[n].shape) for n, r in zip(BIG_NAMES, reduced)}

    def pack(tree):
        rows = jnp.concatenate([_as_rows(tree[n]) for n in SMALL_NAMES], axis=0)
        return jnp.pad(rows, ((0, -rows.shape[0] % (8 * N_DEV)), (0, 0)))

    packed = pack(gs)
    packed = _all_reduce_small(packed.reshape(N_DEV, -1, LANES)).reshape(packed.shape)

    delta, new_m, new_v = {}, {}, {}
    for n in BIG_NAMES:
        shape = wts[n].shape
        out = _adamw(shard2d[n], grads[n], ms[n].reshape(shard2d[n].shape), vs[n].reshape(shard2d[n].shape), f"adamw_{n}")
        grads[n], delta[n], new_m[n], new_v[n] = (a.reshape(shape) for a in out)

    small_out = _adamw(pack(wts), packed, pack(ms), pack(vs), "adamw_small")
    row = 0
    for n in SMALL_NAMES:
        shape = wts[n].shape
        sz = wts[n].size // LANES
        grads[n], delta[n], new_m[n], new_v[n] = (a[row:row + sz].reshape(shape) for a in small_out)
        row += sz + -sz % 8

    return (loss, dx.reshape(x.shape), *[grads[n] for n in WEIGHT_NAMES], *[delta[n] for n in WEIGHT_NAMES],
            *[new_m[n] for n in WEIGHT_NAMES], *[new_v[n] for n in WEIGHT_NAMES])
```

```python
import functools

import jax
import jax.numpy as jnp
from jax import lax
from jax.experimental import pallas as pl
from jax.experimental.pallas import tpu as pltpu

F32 = jnp.float32
BF16 = jnp.bfloat16
MESH = pl.DeviceIdType.MESH

NORM_EPS = 1e-6
LN_EPS = 1e-5
HEAD_DIM = 128
HEADS_PER_GROUP = 4
GROUP_WIDTH = HEADS_PER_GROUP * HEAD_DIM
DILATIONS = (1, 4, 16)
N_GROUPS = len(DILATIONS)
ATT_BLOCK = 128
ROPE_DIM = HEAD_DIM // 4
ROPE_THETA = 500000.0
SG_CHUNK = 128
SG_GROUPS = 12
SG_GROUP_DIM = 128
MASKED = -1e30

ADAM_LR = 0.001
ADAM_B1 = 0.9
ADAM_B2 = 0.999
ADAM_EPS = 1e-08
ADAM_WD = 0.01
ADAM_STEP = 10

N_CHIPS = 4
N_DEV = 8
LANES = 128
MIB = 2 ** 20
VMEM_BYTES_V7X = 64 * MIB

WEIGHT_NAMES = ['ffn1_norm', 'ffn1_w_gate', 'ffn1_w_up', 'ffn1_w_down', 'mix_norm', 'w_in', 'sg_ln_g', 'sg_ln_b',
                'sg_w', 'sg_b', 'w_att_out', 'w_sg_out', 'w_out', 'ffn2_norm', 'ffn2_w_gate', 'ffn2_w_up',
                'ffn2_w_down', 'final_norm']
BIG = [('ffn1_w_gate', 1), ('ffn1_w_up', 1), ('ffn1_w_down', 0), ('w_in', 1), ('w_att_out', 1), ('w_sg_out', 1),
       ('w_out', 0), ('ffn2_w_gate', 1), ('ffn2_w_up', 1), ('ffn2_w_down', 0)]
BIG_NAMES = [n for n, _ in BIG]
SMALL_NAMES = [n for n in WEIGHT_NAMES if n not in BIG_NAMES]


def _nbytes(shape, dtype):
    n = jnp.dtype(dtype).itemsize
    for s in shape:
        if s is not None:
            n *= s
    return n


def _pallas_call(*args, **kw):
    kw['out_shape'] = jax.tree.map(lambda s: pltpu.HBM(s.shape, s.dtype), kw['out_shape'])
    call = pl.pallas_call(*args, **kw)

    def pinned(*operands):
        return call(*[o if jnp.issubdtype(o.dtype, jnp.integer) else pltpu.with_memory_space_constraint(o, pltpu.HBM)
                      for o in operands])

    return pinned


def _cparams(sem, block_bytes, **kw):
    limit = int(min(max(3 * block_bytes, 32 * MIB), VMEM_BYTES_V7X - 8 * MIB))
    return pltpu.CompilerParams(dimension_semantics=sem, vmem_limit_bytes=limit, **kw)


def _tile(dim, cap):
    best = None
    for t in range(LANES, min(dim, cap) + 1, LANES):
        if dim % t == 0:
            best = t
    if best is None:
        assert dim <= cap, (dim, cap)
        return dim
    return best


def _rows(dim, cap):
    best = None
    for t in range(8, min(dim, cap) + 1, 8):
        if dim % t == 0:
            best = t
    assert best is not None, (dim, cap)
    return best


def _place():
    x, y, c = lax.axis_index("x"), lax.axis_index("y"), lax.axis_index("c")
    others = [(1 - x, y), (x, 1 - y), (1 - x, 1 - y)]
    return x, y, c, others


class _Rider:
    def __init__(self, operands, out_shapes, aliases, sems, start, finish):
        self.operands = operands
        self.out_shapes = out_shapes
        self.aliases = aliases
        self.sems = sems
        self.start = start
        self.finish = finish


def _run(body, *, name, grid, in_specs, out_specs, out_shape, scratch_shapes, operands, block_bytes, rider=None):
    if rider is None:
        sem = ("parallel",) * (len(grid) - 1) + ("arbitrary",)
        return _pallas_call(body, out_shape=out_shape, grid=grid, in_specs=in_specs, out_specs=out_specs,
                              scratch_shapes=scratch_shapes, compiler_params=_cparams(sem, block_bytes), name=name)(*operands)
    n_in, n_out, n_scr = len(operands), len(out_shape), len(scratch_shapes)
    r_in, r_out = len(rider.operands), len(rider.out_shapes)
    any_spec = pl.BlockSpec(memory_space=pl.ANY)

    def wrapped(*refs):
        ins, refs = refs[:n_in], refs[n_in:]
        r_ins, refs = refs[:r_in], refs[r_in:]
        outs, refs = refs[:n_out], refs[n_out:]
        r_outs, refs = refs[:r_out], refs[r_out:]
        scr, sems = refs[:n_scr], refs[n_scr:]
        if not grid:
            rider.start(r_ins, r_outs, sems)
            rider.finish(r_ins, r_outs, sems)
            return
        ids = [pl.program_id(a) for a in range(len(grid))]
        first = functools.reduce(jnp.logical_and, [i == 0 for i in ids])
        last = functools.reduce(jnp.logical_and, [i == g - 1 for i, g in zip(ids, grid)])

        @pl.when(first)
        def _():
            rider.start(r_ins, r_outs, sems)

        body(*ins, *outs, *scr)

        @pl.when(last)
        def _():
            rider.finish(r_ins, r_outs, sems)

    results = _pallas_call(
        wrapped, out_shape=list(out_shape) + list(rider.out_shapes), grid=grid,
        in_specs=list(in_specs) + [any_spec] * r_in, out_specs=list(out_specs) + [any_spec] * r_out,
        scratch_shapes=list(scratch_shapes) + list(rider.sems),
        input_output_aliases={n_in + k: n_out + v for k, v in rider.aliases.items()},
        compiler_params=_cparams(("arbitrary",) * len(grid) if grid else None, block_bytes, has_side_effects=True),
        name=name)(*operands, *rider.operands)
    return results[:n_out], results[n_out:]


def _exchange(rider, name):
    return _run(None, name=name, grid=(), in_specs=[], out_specs=[], out_shape=[], scratch_shapes=[], operands=[],
                block_bytes=0, rider=rider)[1]


def _gather_rider(items):
    bufs, index = [], []
    for b, r0, r1 in items:
        if not any(b is q for q in bufs):
            bufs.append(b)
        index.append(([k for k, q in enumerate(bufs) if q is b][0], r0, r1))
    n = len(index)

    def piece(refs, k, chip, half):
        bi, r0, r1 = index[k]
        return refs[bi].at[chip, half, pl.ds(r0, r1 - r0)]

    def copy(ref, sem_pair, k, j, to):
        return pltpu.make_async_remote_copy(ref, ref, sem_pair[0].at[k, j], sem_pair[1].at[k, j], device_id=to, device_id_type=MESH)

    def start(r_ins, buf, sems):
        x, y, c, others = _place()
        for k in range(n):
            for j, (ox, oy) in enumerate(others):
                copy(piece(buf, k, 2 * x + y, c), sems[:2], k, j, (ox, oy, c)).start()

    def finish(r_ins, buf, sems):
        x, y, c, others = _place()
        for k in range(n):
            for j, (ox, oy) in enumerate(others):
                got = piece(buf, k, 2 * ox + oy, c)
                copy(got, sems[:2], k, j, (ox, oy, c)).wait_recv()
                copy(got, sems[2:], k, j, (x, y, 1 - c)).start()
        for k in range(n):
            for j, (ox, oy) in enumerate(others):
                copy(piece(buf, k, 2 * ox + oy, 1 - c), sems[2:], k, j, (x, y, 1 - c)).wait_recv()
        for k in range(n):
            for j, (ox, oy) in enumerate(others):
                copy(piece(buf, k, 2 * x + y, c), sems[:2], k, j, (ox, oy, c)).wait_send()
                copy(piece(buf, k, 2 * ox + oy, c), sems[2:], k, j, (x, y, 1 - c)).wait_send()

    return _Rider(bufs, [jax.ShapeDtypeStruct(b.shape, b.dtype) for b in bufs], {i: i for i in range(len(bufs))},
                  [pltpu.SemaphoreType.DMA((n, 3))] * 4, start, finish)


def _scatter_rider(parts):
    n = len(parts)

    def copy(src, dst, sems, i, j, to):
        return pltpu.make_async_remote_copy(src, dst, sems[0].at[i, j], sems[1].at[i, j], device_id=to, device_id_type=MESH)

    def start(src, dst, sems):
        x, y, c, others = _place()
        for i in range(n):
            for j, (ox, oy) in enumerate(others):
                copy(src[i].at[2 * ox + oy], dst[i].at[j], sems, i, j, (ox, oy, c)).start()

    def finish(src, dst, sems):
        x, y, c, others = _place()
        for i in range(n):
            for j, (ox, oy) in enumerate(others):
                copy(src[i].at[2 * ox + oy], dst[i].at[j], sems, i, j, (ox, oy, c)).wait()

    return _Rider(parts, [jax.ShapeDtypeStruct((3,) + p.shape[1:], p.dtype) for p in parts], {},
                  [pltpu.SemaphoreType.DMA((n, 3))] * 2, start, finish)


def _matmul(pairs, mode, out_dtype, name, *, scale=1.0, residual=None, b3=False, out3=0, caps=(1024, 1024, 512), rider=None):
    a0, b0 = pairs[0]
    if mode == 'nn':
        m, k = a0.shape
        n = b0.shape[0] * b0.shape[2] if b3 else b0.shape[1]
    elif mode == 'nt':
        m = a0.shape[0]
        n, k = (b0.shape[1], b0.shape[0] * b0.shape[2]) if b3 else b0.shape
    else:
        k, m = a0.shape
        n = b0.shape[1]
    tm = _tile(m, caps[0])
    tn = _tile(n, caps[1])
    tk = _tile(k, caps[2])
    if b3 and mode == 'nn':
        tn = b0.shape[2]
    if b3 and mode == 'nt':
        tk = b0.shape[2]
    if out3:
        tn = n // out3
    nk = k // tk
    if mode == 'tn':
        a_spec = pl.BlockSpec((tk, tm), lambda i, j, kk: (kk, i))
        b_spec = pl.BlockSpec((tk, tn), lambda i, j, kk: (kk, j))
        dims = ((0,), (0,))
    elif mode == 'nn':
        a_spec = pl.BlockSpec((tm, tk), lambda i, j, kk: (i, kk))
        b_spec = (pl.BlockSpec((None, tk, tn), lambda i, j, kk: (j, kk, 0)) if b3
                  else pl.BlockSpec((tk, tn), lambda i, j, kk: (kk, j)))
        dims = ((1,), (0,))
    else:
        a_spec = pl.BlockSpec((tm, tk), lambda i, j, kk: (i, kk))
        b_spec = (pl.BlockSpec((None, tn, tk), lambda i, j, kk: (kk, j, 0)) if b3
                  else pl.BlockSpec((tn, tk), lambda i, j, kk: (j, kk)))
        dims = ((1,), (1,))
    in_specs, operands = [], []
    for a, b in pairs:
        in_specs += [a_spec, b_spec]
        operands += [a, b]
    block_bytes = len(pairs) * (_nbytes((tm, tk), a0.dtype) + _nbytes((tk, tn), b0.dtype))
    if residual is not None:
        in_specs.append(pl.BlockSpec((tm, tn), lambda i, j, kk: (i, j)))
        operands.append(residual)
        block_bytes += _nbytes((tm, tn), F32)
    if out3:
        out_spec = pl.BlockSpec((None, tm, tn), lambda i, j, kk: (j, i, 0))
        out_shape = jax.ShapeDtypeStruct((out3, m, tn), out_dtype)
    else:
        out_spec = pl.BlockSpec((tm, tn), lambda i, j, kk: (i, j))
        out_shape = jax.ShapeDtypeStruct((m, n), out_dtype)
    block_bytes += _nbytes((tm, tn), out_dtype) + _nbytes((tm, tn), F32)
    n_pairs = len(pairs)
    has_res = residual is not None

    def body(*refs):
        o_ref, acc = refs[-2], refs[-1]
        kk = pl.program_id(2)

        def product():
            part = None
            for p in range(n_pairs):
                d = lax.dot_general(refs[2 * p][...].astype(BF16), refs[2 * p + 1][...].astype(BF16),
                                    (dims, ((), ())), preferred_element_type=F32)
                part = d if part is None else part + d
            return part

        def finish(r):
            if scale != 1.0:
                r = r * scale
            if has_res:
                r = refs[2 * n_pairs][...] + r
            o_ref[...] = r.astype(out_dtype)

        if nk == 1:
            finish(product())
            return

        @pl.when(kk == 0)
        def _():
            acc[...] = product()

        if nk > 2:
            @pl.when(jnp.logical_and(kk > 0, kk < nk - 1))
            def _():
                acc[...] += product()

        @pl.when(kk == nk - 1)
        def _():
            finish(acc[...] + product())

    res = _run(body, name=name, grid=(m // tm, n // tn, nk), in_specs=in_specs, out_specs=[out_spec], out_shape=[out_shape],
               scratch_shapes=[pltpu.VMEM((tm, tn), F32)], operands=operands, block_bytes=block_bytes, rider=rider)
    return res[0] if rider is None else (res[0][0], res[1])


def _rmsnorm_fwd(x, g, name):
    t, d = x.shape
    tm = _rows(t, 512)

    def body(x_ref, g_ref, o_ref):
        xv = x_ref[...]
        r = lax.rsqrt(jnp.mean(xv * xv, axis=1, keepdims=True) + NORM_EPS)
        o_ref[...] = (xv * r * g_ref[...]).astype(BF16)

    row = pl.BlockSpec((tm, d), lambda i: (i, 0))
    return _pallas_call(
        body, out_shape=jax.ShapeDtypeStruct((t, d), BF16), grid=(t // tm,),
        in_specs=[row, pl.BlockSpec((1, d), lambda i: (0, 0))], out_specs=row,
        compiler_params=_cparams(("parallel",), 2 * _nbytes((tm, d), F32)), name=name)(x, g)


def _rms_grad(xv, g, dn, d):
    r = lax.rsqrt(jnp.mean(xv * xv, axis=1, keepdims=True) + NORM_EPS)
    u = dn * g
    s = jnp.sum(xv * u, axis=1, keepdims=True)
    dx = r * u - xv * (r * r * r) * (s * (1.0 / d))
    return dx, dn * xv * r


def _rmsnorm_bwd(x, g, dn, dres, name):
    t, d = x.shape
    tm = _rows(t, 256)

    def body(x_ref, g_ref, dn_ref, dres_ref, dx_ref, dxb_ref, dg_ref):
        dx, dg_rows = _rms_grad(x_ref[...], g_ref[...], dn_ref[...].astype(F32), d)
        dx = dres_ref[...] + dx
        dx_ref[...] = dx
        dxb_ref[...] = dx.astype(BF16)

        @pl.when(pl.program_id(0) == 0)
        def _():
            dg_ref[...] = jnp.zeros_like(dg_ref)

        dg_ref[...] += jnp.sum(dg_rows, axis=0, keepdims=True)

    row = pl.BlockSpec((tm, d), lambda i: (i, 0))
    vec = pl.BlockSpec((1, d), lambda i: (0, 0))
    return _pallas_call(
        body, out_shape=(jax.ShapeDtypeStruct((t, d), F32), jax.ShapeDtypeStruct((t, d), BF16), jax.ShapeDtypeStruct((1, d), F32)),
        grid=(t // tm,), in_specs=[row, vec, row, row], out_specs=(row, row, vec),
        compiler_params=_cparams(("arbitrary",), 5 * _nbytes((tm, d), F32)), name=name)(x, g, dn, dres)


def _final_loss(x, g, target, name):
    t, d = x.shape
    tm = _rows(t, 256)

    def body(x_ref, g_ref, t_ref, loss_ref, dx_ref, dxb_ref, dg_ref):
        xv, gv = x_ref[...], g_ref[...]
        r = lax.rsqrt(jnp.mean(xv * xv, axis=1, keepdims=True) + NORM_EPS)
        err = xv * r * gv - t_ref[...]
        dx, dg_rows = _rms_grad(xv, gv, err * (1.0 / d), d)
        dx_ref[...] = dx
        dxb_ref[...] = dx.astype(BF16)

        @pl.when(pl.program_id(0) == 0)
        def _():
            dg_ref[...] = jnp.zeros_like(dg_ref)
            loss_ref[...] = jnp.zeros_like(loss_ref)

        dg_ref[...] += jnp.sum(dg_rows, axis=0, keepdims=True)
        row_loss = jnp.sum(err * err, axis=1, keepdims=True) * (0.5 / d)
        loss_ref[...] += jnp.sum(row_loss, axis=0, keepdims=True)

    row = pl.BlockSpec((tm, d), lambda i: (i, 0))
    vec = pl.BlockSpec((1, d), lambda i: (0, 0))
    return _pallas_call(
        body, out_shape=(jax.ShapeDtypeStruct((1, 1), F32), jax.ShapeDtypeStruct((t, d), F32),
                         jax.ShapeDtypeStruct((t, d), BF16), jax.ShapeDtypeStruct((1, d), F32)),
        grid=(t // tm,), in_specs=[row, vec, row], out_specs=(pl.BlockSpec((1, 1), lambda i: (0, 0)), row, row, vec),
        compiler_params=_cparams(("arbitrary",), 4 * _nbytes((tm, d), F32)), name=name)(x, g, target)


def _ffn_up(n, wg, wu, name, rider=None):
    t, d = n.shape
    s, _, f = wg.shape
    tm, tk = _tile(t, 1024), _tile(d, 1024)
    nk = d // tk

    def body(n_ref, wg_ref, wu_ref, a_ref, b_ref, h_ref, acc_g, acc_u):
        kk = pl.program_id(2)

        def products():
            nv = n_ref[...]
            return jnp.dot(nv, wg_ref[...], preferred_element_type=F32), jnp.dot(nv, wu_ref[...], preferred_element_type=F32)

        def finish(a, b):
            a_ref[...] = a.astype(BF16)
            b_ref[...] = b.astype(BF16)
            h_ref[...] = (a * jax.nn.sigmoid(a) * b).astype(BF16)

        if nk == 1:
            finish(*products())
            return

        @pl.when(kk == 0)
        def _():
            acc_g[...], acc_u[...] = products()

        if nk > 2:
            @pl.when(jnp.logical_and(kk > 0, kk < nk - 1))
            def _():
                pg, pu = products()
                acc_g[...] += pg
                acc_u[...] += pu

        @pl.when(kk == nk - 1)
        def _():
            pg, pu = products()
            finish(acc_g[...] + pg, acc_u[...] + pu)

    w_spec = pl.BlockSpec((None, tk, f), lambda i, j, kk: (j, kk, 0))
    o_spec = pl.BlockSpec((tm, f), lambda i, j, kk: (i, j))
    out = jax.ShapeDtypeStruct((t, s * f), BF16)
    block_bytes = _nbytes((tm, tk), BF16) + 2 * _nbytes((tk, f), BF16) + 3 * _nbytes((tm, f), BF16) + 2 * _nbytes((tm, f), F32)
    return _run(body, name=name, grid=(t // tm, s, nk),
                in_specs=[pl.BlockSpec((tm, tk), lambda i, j, kk: (i, kk)), w_spec, w_spec], out_specs=[o_spec, o_spec, o_spec],
                out_shape=[out, out, out], scratch_shapes=[pltpu.VMEM((tm, f), F32), pltpu.VMEM((tm, f), F32)],
                operands=[n, wg, wu], block_bytes=block_bytes, rider=rider)


def _ffn_bwd_act(dx, wd, a, b, name):
    t, d = dx.shape
    f = wd.shape[0]
    tm, tn, tk = _tile(t, 1024), _tile(f, 1536), _tile(d, 1024)
    nk = d // tk

    def body(dx_ref, wd_ref, a_ref, b_ref, da_ref, db_ref, acc):
        kk = pl.program_id(2)

        def product():
            return lax.dot_general(dx_ref[...], wd_ref[...], (((1,), (1,)), ((), ())), preferred_element_type=F32)

        def finish(r):
            dh = 0.5 * r
            av, bv = a_ref[...].astype(F32), b_ref[...].astype(F32)
            sg = jax.nn.sigmoid(av)
            da_ref[...] = (dh * bv * (sg * (1.0 + av * (1.0 - sg)))).astype(BF16)
            db_ref[...] = (dh * (av * sg)).astype(BF16)

        if nk == 1:
            finish(product())
            return

        @pl.when(kk == 0)
        def _():
            acc[...] = product()

        if nk > 2:
            @pl.when(jnp.logical_and(kk > 0, kk < nk - 1))
            def _():
                acc[...] += product()

        @pl.when(kk == nk - 1)
        def _():
            finish(acc[...] + product())

    act = pl.BlockSpec((tm, tn), lambda i, j, kk: (i, j))
    out = jax.ShapeDtypeStruct((t, f), BF16)
    block_bytes = _nbytes((tm, tk), BF16) + _nbytes((tn, tk), BF16) + 4 * _nbytes((tm, tn), BF16) + _nbytes((tm, tn), F32)
    return _pallas_call(
        body, out_shape=(out, out), grid=(t // tm, f // tn, nk),
        in_specs=[pl.BlockSpec((tm, tk), lambda i, j, kk: (i, kk)), pl.BlockSpec((tn, tk), lambda i, j, kk: (j, kk)),
                  act, act],
        out_specs=(act, act), scratch_shapes=[pltpu.VMEM((tm, tn), F32)],
        compiler_params=_cparams(("parallel", "parallel", "arbitrary"), block_bytes), name=name)(dx, wd, a, b)


AXIS = dict(BIG)


def _full(wb, n):
    _, _, r, ccols = wb[n].shape
    return wb[n].reshape(N_CHIPS, 2 * r, ccols) if AXIS[n] == 1 else wb[n].reshape(N_CHIPS * 2 * r, ccols)


def _gather(wb, specs):
    items, names = [], []
    for s in specs:
        n, r0, r1 = (s, 0, wb[s].shape[2]) if isinstance(s, str) else s
        items.append((wb[n], r0, r1))
        if n not in names:
            names.append(n)
    return _gather_rider(items), names


def _landed(wb, names, results):
    for n, r in zip(names, results):
        wb[n] = r


def _reduce_first(grads, names, wb, c_idx):
    g4 = [g.reshape(wb[n].shape) for g, n in zip(grads, names)]
    from_sibling = _sibling_exchange(g4, "rs_sibling_" + names[0])
    return [_sibling_sum(a, b, c_idx, f"rs_sum1_{n}") for a, b, n in zip(g4, from_sibling, names)]


def _ffn_forward(x, gain, wb, tag, up_specs, down_specs):
    n = _rmsnorm_fwd(x, gain, f"{tag}_norm")
    rider, names = _gather(wb, up_specs)
    (a, b, h), got = _ffn_up(n, _full(wb, f"{tag}_w_gate"), _full(wb, f"{tag}_w_up"), f"{tag}_up", rider=rider)
    _landed(wb, names, got)
    down = dict(scale=0.5, residual=x, caps=(1024, 1024, 1536))
    if down_specs:
        rider, names = _gather(wb, down_specs)
        x_next, got = _matmul([(h, _full(wb, f"{tag}_w_down"))], 'nn', F32, f"{tag}_down", rider=rider, **down)
        _landed(wb, names, got)
    else:
        x_next = _matmul([(h, _full(wb, f"{tag}_w_down"))], 'nn', F32, f"{tag}_down", **down)
    return x_next, (n, a, b, h)


def _ffn_backward(x, gain, wb, saved, dx_next, dx_next_b, c_idx, tag):
    n, a, b, h = saved
    wg, wu, wd = (f"{tag}_w_gate", f"{tag}_w_up", f"{tag}_w_down")
    da, db = _ffn_bwd_act(dx_next_b, _full(wb, wd), a, b, f"{tag}_bwd_act")
    g_wd = _matmul([(h, dx_next_b)], 'tn', BF16, f"{tag}_dwd", scale=0.5, caps=(1536, 1024, 1024))
    (p_wd,) = _reduce_first([g_wd], [wd], wb, c_idx)
    g_wg, (r_wd,) = _matmul([(n, da)], 'tn', BF16, f"{tag}_dwg", out3=N_CHIPS, caps=(1024, 1024, 1024), rider=_scatter_rider([p_wd]))
    g_wu = _matmul([(n, db)], 'tn', BF16, f"{tag}_dwu", out3=N_CHIPS, caps=(1024, 1024, 1024))
    p_wg, p_wu = _reduce_first([g_wg, g_wu], [wg, wu], wb, c_idx)
    dn, (r_wg, r_wu) = _matmul([(da, _full(wb, wg)), (db, _full(wb, wu))], 'nt', F32, f"{tag}_dn", b3=True,
                               rider=_scatter_rider([p_wg, p_wu]))
    dx, dx_b, g_gain = _rmsnorm_bwd(x, gain, dn, dx_next, f"{tag}_norm_bwd")
    return dx, dx_b, g_gain, {wg: (p_wg, r_wg), wu: (p_wu, r_wu), wd: (p_wd, r_wd)}


def _rope_tables(seq):
    half = ROPE_DIM // 2
    inv_freq = ROPE_THETA ** (-jnp.arange(0, ROPE_DIM, 2, dtype=F32) / ROPE_DIM)
    ang = jnp.arange(seq).astype(F32)[:, None] * inv_freq[None, :]
    cos, sin = jnp.cos(ang), jnp.sin(ang)
    zeros = lambda w: jnp.zeros((seq, w), F32)
    c = jnp.concatenate([cos, cos, jnp.ones((seq, HEAD_DIM - ROPE_DIM), F32)], axis=1)
    s_up = jnp.concatenate([-sin, zeros(HEAD_DIM - half)], axis=1)
    s_dn = jnp.concatenate([zeros(half), sin, zeros(HEAD_DIM - ROPE_DIM)], axis=1)
    return c, s_up, s_dn


def _rope(x, width, tables, name):
    t = x.shape[0]
    tm = _rows(t, 512)
    half = ROPE_DIM // 2
    c, s_up, s_dn = tables

    def body(x_ref, c_ref, up_ref, dn_ref, o_ref):
        cv, uv, dv = c_ref[...], up_ref[...], dn_ref[...]
        for h in range(GROUP_WIDTH // HEAD_DIM):
            sl = slice(h * HEAD_DIM, (h + 1) * HEAD_DIM)
            xv = x_ref[:, sl].astype(F32)
            o_ref[:, sl] = (xv * cv + pltpu.roll(xv, HEAD_DIM - half, 1) * uv + pltpu.roll(xv, half, 1) * dv).astype(BF16)

    blk = pl.BlockSpec((tm, GROUP_WIDTH), lambda i, j: (i, j))
    tab = pl.BlockSpec((tm, HEAD_DIM), lambda i, j: (i, 0))
    return _pallas_call(
        body, out_shape=jax.ShapeDtypeStruct((t, width), BF16), grid=(t // tm, width // GROUP_WIDTH),
        in_specs=[blk, tab, tab, tab], out_specs=blk,
        compiler_params=_cparams(("parallel", "parallel"), 2 * _nbytes((tm, GROUP_WIDTH), F32)), name=name)(x, c, s_up, s_dn)


def _att_masks():
    qi = lax.broadcasted_iota(jnp.int32, (ATT_BLOCK, ATT_BLOCK), 0)
    kj = lax.broadcasted_iota(jnp.int32, (ATT_BLOCK, ATT_BLOCK), 1)
    return kj >= qi, kj <= qi


def _scores(q, k):
    return lax.dot_general(q, k, (((1,), (1,)), ((), ())), preferred_element_type=F32) * (HEAD_DIM ** -0.5)


def _att_fwd(q, k, v, offs, dil, name):
    qo, ko, vo = offs
    length = q.shape[0]
    nb = length // ATT_BLOCK

    def body(q_ref, kp_ref, kc_ref, vp_ref, vc_ref, o_ref, lse_ref):
        has_prev = pl.program_id(1) > 0
        m_prev, m_cur = _att_masks()
        m_prev = jnp.logical_and(m_prev, has_prev)
        for h in range(HEADS_PER_GROUP):
            sl = slice(h * HEAD_DIM, (h + 1) * HEAD_DIM)
            qv = q_ref[:, sl]
            s_p = jnp.where(m_prev, _scores(qv, kp_ref[:, sl]), MASKED)
            s_c = jnp.where(m_cur, _scores(qv, kc_ref[:, sl]), MASKED)
            m = jnp.maximum(jnp.max(s_p, axis=1, keepdims=True), jnp.max(s_c, axis=1, keepdims=True))
            p_p, p_c = jnp.exp(s_p - m), jnp.exp(s_c - m)
            l = jnp.sum(p_p, axis=1, keepdims=True) + jnp.sum(p_c, axis=1, keepdims=True)
            acc = jnp.dot(p_p.astype(BF16), vp_ref[:, sl], preferred_element_type=F32)
            acc += jnp.dot(p_c.astype(BF16), vc_ref[:, sl], preferred_element_type=F32)
            o_ref[:, sl] = acc / l
            lse_ref[:, sl] = jnp.broadcast_to(m + jnp.log(l), (ATT_BLOCK, HEAD_DIM))

    def spec(off, prev):
        if prev:
            return pl.BlockSpec((ATT_BLOCK, GROUP_WIDTH), lambda r, n: (jnp.maximum(n - 1, 0), off + r))
        return pl.BlockSpec((ATT_BLOCK, GROUP_WIDTH), lambda r, n: (n, off + r))

    out = jax.ShapeDtypeStruct((length, dil * GROUP_WIDTH), F32)
    o_spec = pl.BlockSpec((ATT_BLOCK, GROUP_WIDTH), lambda r, n: (n, r))
    return _pallas_call(
        body, out_shape=(out, out), grid=(dil, nb),
        in_specs=[spec(qo, False), spec(ko, True), spec(ko, False), spec(vo, True), spec(vo, False)],
        out_specs=(o_spec, o_spec),
        compiler_params=_cparams(("parallel", "parallel"), 8 * _nbytes((ATT_BLOCK, GROUP_WIDTH), F32)), name=name)(q, k, k, v, v)


def _att_combine(outs, lses, name):
    t = outs[0].shape[0]
    tm = _rows(t, 512)

    def body(*refs):
        o_refs, l_refs = refs[:N_GROUPS], refs[N_GROUPS:2 * N_GROUPS]
        ob_ref, of_ref, lse_ref = refs[2 * N_GROUPS:]
        ls = [r[...] for r in l_refs]
        m = functools.reduce(jnp.maximum, ls)
        ws = [jnp.exp(l - m) for l in ls]
        den = functools.reduce(jnp.add, ws)
        num = functools.reduce(jnp.add, [w * r[...] for w, r in zip(ws, o_refs)])
        o = num / den
        ob_ref[...] = o.astype(BF16)
        of_ref[...] = o
        lse_ref[...] = m + jnp.log(den)

    blk = pl.BlockSpec((tm, GROUP_WIDTH), lambda i: (i, 0))
    f32 = jax.ShapeDtypeStruct((t, GROUP_WIDTH), F32)
    return _pallas_call(
        body, out_shape=(jax.ShapeDtypeStruct((t, GROUP_WIDTH), BF16), f32, f32), grid=(t // tm,),
        in_specs=[blk] * (2 * N_GROUPS), out_specs=(blk, blk, blk),
        compiler_params=_cparams(("parallel",), 9 * _nbytes((tm, GROUP_WIDTH), F32)), name=name)(*outs, *lses)


def _att_delta(do, o, name):
    t = o.shape[0]
    tm = _rows(t, 512)

    def body(do_ref, o_ref, d_ref):
        for h in range(HEADS_PER_GROUP):
            sl = slice(h * HEAD_DIM, (h + 1) * HEAD_DIM)
            s = jnp.sum(do_ref[:, sl] * o_ref[:, sl], axis=1, keepdims=True)
            d_ref[:, sl] = jnp.broadcast_to(s, (tm, HEAD_DIM))

    blk = pl.BlockSpec((tm, GROUP_WIDTH), lambda i: (i, 0))
    return _pallas_call(
        body, out_shape=jax.ShapeDtypeStruct((t, GROUP_WIDTH), F32), grid=(t // tm,), in_specs=[blk, blk], out_specs=blk,
        compiler_params=_cparams(("parallel",), 3 * _nbytes((tm, GROUP_WIDTH), F32)), name=name)(do, o)


def _att_bwd_dq(q, k, v, do, lse, delta, offs, dil, name):
    qo, ko, vo = offs
    length = q.shape[0]
    nb = length // ATT_BLOCK
    scale = HEAD_DIM ** -0.5

    def body(q_ref, kp_ref, kc_ref, vp_ref, vc_ref, do_ref, lse_ref, dl_ref, dq_ref):
        has_prev = pl.program_id(1) > 0
        m_prev, m_cur = _att_masks()
        m_prev = jnp.logical_and(m_prev, has_prev)
        for h in range(HEADS_PER_GROUP):
            sl = slice(h * HEAD_DIM, (h + 1) * HEAD_DIM)
            qv, dov = q_ref[:, sl], do_ref[:, sl].astype(BF16)
            lsev, dlv = lse_ref[:, sl], dl_ref[:, sl]
            dq = None
            for mask, k_ref, v_ref in ((m_prev, kp_ref, vp_ref), (m_cur, kc_ref, vc_ref)):
                kv = k_ref[:, sl]
                p = jnp.exp(jnp.where(mask, _scores(qv, kv), MASKED) - lsev)
                dp = lax.dot_general(dov, v_ref[:, sl], (((1,), (1,)), ((), ())), preferred_element_type=F32)
                ds = (p * (dp - dlv) * scale).astype(BF16)
                part = jnp.dot(ds, kv, preferred_element_type=F32)
                dq = part if dq is None else dq + part
            dq_ref[:, sl] = dq.astype(BF16)

    def spec(off, prev):
        if prev:
            return pl.BlockSpec((ATT_BLOCK, GROUP_WIDTH), lambda r, n: (jnp.maximum(n - 1, 0), off + r))
        return pl.BlockSpec((ATT_BLOCK, GROUP_WIDTH), lambda r, n: (n, off + r))

    own = pl.BlockSpec((ATT_BLOCK, GROUP_WIDTH), lambda r, n: (n, r))
    return _pallas_call(
        body, out_shape=jax.ShapeDtypeStruct((length, dil * GROUP_WIDTH), BF16), grid=(dil, nb),
        in_specs=[spec(qo, False), spec(ko, True), spec(ko, False), spec(vo, True), spec(vo, False), own, own, own],
        out_specs=own,
        compiler_params=_cparams(("parallel", "parallel"), 10 * _nbytes((ATT_BLOCK, GROUP_WIDTH), F32)),
        name=name)(q, k, k, v, v, do, lse, delta)


def _att_bwd_dkv(q, k, v, do, lse, delta, offs, dil, name):
    qo, ko, vo = offs
    length = q.shape[0]
    nb = length // ATT_BLOCK
    scale = HEAD_DIM ** -0.5

    def body(k_ref, v_ref, qc_ref, qn_ref, doc_ref, don_ref, lsec_ref, lsen_ref, dlc_ref, dln_ref, dk_ref, dv_ref):
        has_next = pl.program_id(1) < nb - 1
        m_prev, m_cur = _att_masks()
        m_prev = jnp.logical_and(m_prev, has_next)
        for h in range(HEADS_PER_GROUP):
            sl = slice(h * HEAD_DIM, (h + 1) * HEAD_DIM)
            kv, vv = k_ref[:, sl], v_ref[:, sl]
            dk = dv = None
            for mask, q_ref, do_ref, lse_ref, dl_ref in ((m_cur, qc_ref, doc_ref, lsec_ref, dlc_ref),
                                                         (m_prev, qn_ref, don_ref, lsen_ref, dln_ref)):
                qv, dov = q_ref[:, sl], do_ref[:, sl].astype(BF16)
                p = jnp.exp(jnp.where(mask, _scores(qv, kv), MASKED) - lse_ref[:, sl])
                dp = lax.dot_general(dov, vv, (((1,), (1,)), ((), ())), preferred_element_type=F32)
                ds = (p * (dp - dl_ref[:, sl]) * scale).astype(BF16)
                dv_part = lax.dot_general(p.astype(BF16), dov, (((0,), (0,)), ((), ())), preferred_element_type=F32)
                dk_part = lax.dot_general(ds, qv, (((0,), (0,)), ((), ())), preferred_element_type=F32)
                dv = dv_part if dv is None else dv + dv_part
                dk = dk_part if dk is None else dk + dk_part
            dk_ref[:, sl] = dk.astype(BF16)
            dv_ref[:, sl] = dv.astype(BF16)

    def spec(off, nxt):
        if nxt:
            return pl.BlockSpec((ATT_BLOCK, GROUP_WIDTH), lambda r, n: (jnp.minimum(n + 1, nb - 1), off + r))
        return pl.BlockSpec((ATT_BLOCK, GROUP_WIDTH), lambda r, n: (n, off + r))

    own = pl.BlockSpec((ATT_BLOCK, GROUP_WIDTH), lambda r, n: (n, r))
    out = jax.ShapeDtypeStruct((length, dil * GROUP_WIDTH), BF16)
    return _pallas_call(
        body, out_shape=(out, out), grid=(dil, nb),
        in_specs=[spec(ko, False), spec(vo, False), spec(qo, False), spec(qo, True), spec(0, False), spec(0, True),
                  spec(0, False), spec(0, True), spec(0, False), spec(0, True)],
        out_specs=(own, own),
        compiler_params=_cparams(("parallel", "parallel"), 12 * _nbytes((ATT_BLOCK, GROUP_WIDTH), F32)),
        name=name)(k, v, q, q, do, do, lse, lse, delta, delta)


def _gelu(x):
    return 0.5 * x * (1.0 + lax.erf(x * (2.0 ** -0.5)))


def _gelu_grad(x):
    return 0.5 * (1.0 + lax.erf(x * (2.0 ** -0.5))) + x * jnp.exp(-0.5 * x * x) * ((2.0 * jnp.pi) ** -0.5)


def _sg_normed(vs, lg, lb):
    gv = _gelu(vs)
    mu = jnp.mean(gv, axis=1, keepdims=True)
    xc = gv - mu
    rstd = lax.rsqrt(jnp.mean(xc * xc, axis=1, keepdims=True) + LN_EPS)
    z = xc * rstd
    return z, rstd, z * lg + lb


def _sg_tril():
    row = lax.broadcasted_iota(jnp.int32, (SG_CHUNK, SG_CHUNK), 0)
    col = lax.broadcasted_iota(jnp.int32, (SG_CHUNK, SG_CHUNK), 1)
    return row >= col


def _sg_fwd(proj, u_blk, vs_blk, lg, lb, sg_w, bias, name):
    t = proj.shape[0]
    width = SG_GROUPS * SG_GROUP_DIM

    def body(u_ref, vs_ref, lg_ref, lb_ref, w_ref, bias_ref, o_ref):
        _, _, vn = _sg_normed(vs_ref[...].astype(F32), lg_ref[...], lb_ref[...])
        vn = vn.astype(BF16)
        tril = _sg_tril()
        for g in range(SG_GROUPS):
            sl = slice(g * SG_GROUP_DIM, (g + 1) * SG_GROUP_DIM)
            w = jnp.where(tril, w_ref[g], 0.0).astype(BF16)
            sp = jnp.dot(w, vn[:, sl], preferred_element_type=F32) + bias_ref[:, sl]
            o_ref[:, sl] = (_gelu(u_ref[:, sl].astype(F32)) * sp).astype(BF16)

    vec = pl.BlockSpec((1, width), lambda i: (0, 0))
    return _pallas_call(
        body, out_shape=jax.ShapeDtypeStruct((t, width), BF16), grid=(t // SG_CHUNK,),
        in_specs=[pl.BlockSpec((SG_CHUNK, width), lambda i: (i, u_blk)), pl.BlockSpec((SG_CHUNK, width), lambda i: (i, vs_blk)),
                  vec, vec, pl.BlockSpec((SG_GROUPS, SG_CHUNK, SG_CHUNK), lambda i: (0, 0, 0)),
                  pl.BlockSpec((SG_CHUNK, width), lambda i: (0, 0))],
        out_specs=pl.BlockSpec((SG_CHUNK, width), lambda i: (i, 0)),
        compiler_params=_cparams(("parallel",), 8 * _nbytes((SG_CHUNK, width), F32)), name=name)(proj, proj, lg, lb, sg_w, bias)


def _sg_bwd(proj, u_blk, vs_blk, dsu, lg, lb, sg_w, bias, name):
    t = proj.shape[0]
    width = SG_GROUPS * SG_GROUP_DIM

    def body(u_ref, vs_ref, dsu_ref, lg_ref, lb_ref, w_ref, bias_ref, du_ref, dvs_ref, dw_ref, dbias_ref, dlg_ref, dlb_ref):
        @pl.when(pl.program_id(0) == 0)
        def _():
            dw_ref[...] = jnp.zeros_like(dw_ref)
            dbias_ref[...] = jnp.zeros_like(dbias_ref)
            dlg_ref[...] = jnp.zeros_like(dlg_ref)
            dlb_ref[...] = jnp.zeros_like(dlb_ref)

        vs = vs_ref[...].astype(F32)
        z, rstd, vn = _sg_normed(vs, lg_ref[...], lb_ref[...])
        vn = vn.astype(BF16)
        tril = _sg_tril()
        dvn = []
        for g in range(SG_GROUPS):
            sl = slice(g * SG_GROUP_DIM, (g + 1) * SG_GROUP_DIM)
            w = jnp.where(tril, w_ref[g], 0.0).astype(BF16)
            vg = vn[:, sl]
            sp = jnp.dot(w, vg, preferred_element_type=F32) + bias_ref[:, sl]
            uv = u_ref[:, sl].astype(F32)
            dsu_g = dsu_ref[:, sl].astype(F32)
            du_ref[:, sl] = (dsu_g * sp * _gelu_grad(uv)).astype(BF16)
            dsp = dsu_g * _gelu(uv)
            dsp_b = dsp.astype(BF16)
            dw = lax.dot_general(dsp_b, vg, (((1,), (1,)), ((), ())), preferred_element_type=F32)
            dw_ref[g] += jnp.where(tril, dw, 0.0)
            dbias_ref[:, sl] += jnp.broadcast_to(jnp.sum(dsp, axis=1, keepdims=True), (SG_CHUNK, SG_GROUP_DIM))
            dvn.append(lax.dot_general(w, dsp_b, (((0,), (0,)), ((), ())), preferred_element_type=F32))
        dvn = jnp.concatenate(dvn, axis=1)
        dlg_ref[...] += jnp.sum(dvn * z, axis=0, keepdims=True)
        dlb_ref[...] += jnp.sum(dvn, axis=0, keepdims=True)
        dz = dvn * lg_ref[...]
        dgv = rstd * (dz - jnp.mean(dz, axis=1, keepdims=True) - z * jnp.mean(dz * z, axis=1, keepdims=True))
        dvs_ref[...] = (dgv * _gelu_grad(vs)).astype(BF16)

    vec = pl.BlockSpec((1, width), lambda i: (0, 0))
    row = pl.BlockSpec((SG_CHUNK, width), lambda i: (i, 0))
    fixed = pl.BlockSpec((SG_CHUNK, width), lambda i: (0, 0))
    w_spec = pl.BlockSpec((SG_GROUPS, SG_CHUNK, SG_CHUNK), lambda i: (0, 0, 0))
    act = jax.ShapeDtypeStruct((t, width), BF16)
    return _pallas_call(
        body,
        out_shape=(act, act, jax.ShapeDtypeStruct((SG_GROUPS, SG_CHUNK, SG_CHUNK), F32),
                   jax.ShapeDtypeStruct((SG_CHUNK, width), F32), jax.ShapeDtypeStruct((1, width), F32),
                   jax.ShapeDtypeStruct((1, width), F32)),
        grid=(t // SG_CHUNK,),
        in_specs=[pl.BlockSpec((SG_CHUNK, width), lambda i: (i, u_blk)), pl.BlockSpec((SG_CHUNK, width), lambda i: (i, vs_blk)),
                  row, vec, vec, w_spec, fixed],
        out_specs=(row, row, w_spec, fixed, vec, vec),
        compiler_params=_cparams(("arbitrary",), 14 * _nbytes((SG_CHUNK, width), F32)),
        name=name)(proj, proj, dsu, lg, lb, sg_w, bias)


def _gate_fwd(proj, ga_blk, gs_blk, y_att, y_sg, name):
    t, d = y_att.shape
    tm, tn = _rows(t, 512), _tile(d, GROUP_WIDTH)

    def body(ga_ref, gs_ref, ya_ref, ys_ref, o_ref):
        o_ref[...] = (jax.nn.sigmoid(ga_ref[...].astype(F32)) * ya_ref[...].astype(F32)
                      + jax.nn.sigmoid(gs_ref[...].astype(F32)) * ys_ref[...].astype(F32)).astype(BF16)

    own = pl.BlockSpec((tm, tn), lambda i, j: (i, j))
    return _pallas_call(
        body, out_shape=jax.ShapeDtypeStruct((t, d), BF16), grid=(t // tm, d // tn),
        in_specs=[pl.BlockSpec((tm, tn), lambda i, j: (i, ga_blk + j)), pl.BlockSpec((tm, tn), lambda i, j: (i, gs_blk + j)),
                  own, own],
        out_specs=own, compiler_params=_cparams(("parallel", "parallel"), 6 * _nbytes((tm, tn), F32)),
        name=name)(proj, proj, y_att, y_sg)


def _gate_bwd(proj, ga_blk, gs_blk, y_att, y_sg, dmerged, name):
    t, d = y_att.shape
    tm, tn = _rows(t, 512), _tile(d, GROUP_WIDTH)

    def body(ga_ref, gs_ref, ya_ref, ys_ref, dm_ref, dya_ref, dys_ref, dga_ref, dgs_ref):
        dm = dm_ref[...].astype(F32)
        for g_ref, y_ref, dy_ref, dg_ref in ((ga_ref, ya_ref, dya_ref, dga_ref), (gs_ref, ys_ref, dys_ref, dgs_ref)):
            sg = jax.nn.sigmoid(g_ref[...].astype(F32))
            dy_ref[...] = (dm * sg).astype(BF16)
            dg_ref[...] = (dm * y_ref[...].astype(F32) * sg * (1.0 - sg)).astype(BF16)

    own = pl.BlockSpec((tm, tn), lambda i, j: (i, j))
    out = jax.ShapeDtypeStruct((t, d), BF16)
    return _pallas_call(
        body, out_shape=(out, out, out, out), grid=(t // tm, d // tn),
        in_specs=[pl.BlockSpec((tm, tn), lambda i, j: (i, ga_blk + j)), pl.BlockSpec((tm, tn), lambda i, j: (i, gs_blk + j)),
                  own, own, own],
        out_specs=(own, own, own, own), compiler_params=_cparams(("parallel", "parallel"), 10 * _nbytes((tm, tn), F32)),
        name=name)(proj, proj, y_att, y_sg, dmerged)


def _group_view(arr, col, dil):
    t = arr.shape[0]
    return arr[:, col:col + GROUP_WIDTH].reshape(t // dil, dil * GROUP_WIDTH)


def _mixer_forward(x, wb, small, in_specs, sg_specs, out_specs):
    t, d = x.shape
    att_w = N_GROUPS * GROUP_WIDTH
    sg_w = SG_GROUPS * SG_GROUP_DIM
    n = _rmsnorm_fwd(x, small['mix_norm'], "mix_norm")
    rider, names = _gather(wb, in_specs)
    proj, got = _matmul([(n, _full(wb, 'w_in'))], 'nn', BF16, "mix_in", b3=True, caps=(1024, 1024, 1024), rider=rider)
    _landed(wb, names, got)
    tables = _rope_tables(t)
    qk = _rope(proj, 2 * att_w, tables, "mix_rope")
    outs, lses = [], []
    for gi, dil in enumerate(DILATIONS):
        if dil == 1:
            args = (qk, qk, proj, (gi, N_GROUPS + gi, 2 * N_GROUPS + gi))
        else:
            args = (_group_view(qk, gi * GROUP_WIDTH, dil), _group_view(qk, att_w + gi * GROUP_WIDTH, dil),
                    _group_view(proj, 2 * att_w + gi * GROUP_WIDTH, dil), (0, 0, 0))
        o, lse = _att_fwd(*args, dil, f"att_fwd{gi}")
        outs.append(o.reshape(t, GROUP_WIDTH))
        lses.append(lse.reshape(t, GROUP_WIDTH))
    o_b, o_f, lse = _att_combine(outs, lses, "att_combine")
    y_att = _matmul([(o_b, _full(wb, 'w_att_out'))], 'nn', BF16, "mix_att_out", b3=True)
    bias = jnp.repeat(small['sg_b'].T, SG_GROUP_DIM, axis=1)
    u_blk, vs_blk = 3 * att_w // sg_w, 3 * att_w // sg_w + 1
    su = _sg_fwd(proj, u_blk, vs_blk, small['sg_ln_g'], small['sg_ln_b'], small['sg_w'], bias, "sg_fwd")
    rider, names = _gather(wb, sg_specs)
    y_sg, got = _matmul([(su, _full(wb, 'w_sg_out'))], 'nn', BF16, "mix_sg_out", b3=True, rider=rider)
    _landed(wb, names, got)
    ga_blk = (3 * att_w + 2 * sg_w) // _tile(d, GROUP_WIDTH)
    gs_blk = ga_blk + d // _tile(d, GROUP_WIDTH)
    merged = _gate_fwd(proj, ga_blk, gs_blk, y_att, y_sg, "gate_fwd")
    rider, names = _gather(wb, out_specs)
    x_next, got = _matmul([(merged, _full(wb, 'w_out'))], 'nn', F32, "mix_out", residual=x, rider=rider)
    _landed(wb, names, got)
    saved =(n, proj, qk, tables, o_b, o_f, lse, y_att, su, y_sg, merged, bias, (u_blk, vs_blk, ga_blk, gs_blk))
    return x_next, saved


def _mixer_backward(x, wb, small, saved, dx_next, dx_next_b, c_idx):
    n, proj, qk, tables, o_b, o_f, lse, y_att, su, y_sg, merged, bias, (u_blk, vs_blk, ga_blk, gs_blk) = saved
    t, d = x.shape
    att_w = N_GROUPS * GROUP_WIDTH
    s = N_CHIPS
    dmerged = _matmul([(dx_next_b, _full(wb, 'w_out'))], 'nt', BF16, "mix_out_dx")
    g_w_out = _matmul([(merged, dx_next_b)], 'tn', BF16, "mix_out_dw", caps=(1024, 1024, 1024))
    dy_att, dy_sg, dg_att, dg_sg = _gate_bwd(proj, ga_blk, gs_blk, y_att, y_sg, dmerged, "gate_bwd")

    g_w_att_out = _matmul([(o_b, dy_att)], 'tn', BF16, "mix_att_out_dw", out3=s)
    do = _matmul([(dy_att, _full(wb, 'w_att_out'))], 'nt', F32, "mix_att_out_dx", b3=True)
    delta = _att_delta(do, o_f, "att_delta")
    dqs, dks, dvs_ = [], [], []
    for gi, dil in enumerate(DILATIONS):
        if dil == 1:
            args = (qk, qk, proj, do, lse, delta, (gi, N_GROUPS + gi, 2 * N_GROUPS + gi))
        else:
            args = (_group_view(qk, gi * GROUP_WIDTH, dil), _group_view(qk, att_w + gi * GROUP_WIDTH, dil),
                    _group_view(proj, 2 * att_w + gi * GROUP_WIDTH, dil), _group_view(do, 0, dil),
                    _group_view(lse, 0, dil), _group_view(delta, 0, dil), (0, 0, 0))
        dq = _att_bwd_dq(*args, dil, f"att_bwd_dq{gi}")
        dk, dv = _att_bwd_dkv(*args, dil, f"att_bwd_dkv{gi}")
        dqs.append(dq.reshape(t, GROUP_WIDTH))
        dks.append(dk.reshape(t, GROUP_WIDTH))
        dvs_.append(dv.reshape(t, GROUP_WIDTH))
    c, s_up, s_dn = tables
    dqk = _rope(jnp.concatenate(dqs + dks, axis=1), 2 * att_w, (c, -s_up, -s_dn), "mix_rope_bwd")

    g_w_sg_out = _matmul([(su, dy_sg)], 'tn', BF16, "mix_sg_out_dw", out3=s)
    out_names = ['w_out', 'w_att_out', 'w_sg_out']
    out_parts = _reduce_first([g_w_out, g_w_att_out, g_w_sg_out], out_names, wb, c_idx)
    dsu = _matmul([(dy_sg, _full(wb, 'w_sg_out'))], 'nt', BF16, "mix_sg_out_dx", b3=True)
    du, dvs, g_sg_w, g_bias, g_lg, g_lb = _sg_bwd(proj, u_blk, vs_blk, dsu, small['sg_ln_g'], small['sg_ln_b'],
                                                   small['sg_w'], bias, "sg_bwd")
    gs = {'sg_w': g_sg_w, 'sg_b': g_bias[:, ::SG_GROUP_DIM].T, 'sg_ln_g': g_lg, 'sg_ln_b': g_lb}

    dproj = jnp.concatenate([dqk] + dvs_ + [du, dvs, dg_att, dg_sg], axis=1)
    g_w_in, out_recv = _matmul([(n, dproj)], 'tn', BF16, "mix_in_dw", out3=s, caps=(1024, 1024, 1024),
                               rider=_scatter_rider(out_parts))
    (p_w_in,) = _reduce_first([g_w_in], ['w_in'], wb, c_idx)
    dn, (r_w_in,) = _matmul([(dproj, _full(wb, 'w_in'))], 'nt', F32, "mix_in_dx", b3=True, caps=(1024, 1024, 512),
                            rider=_scatter_rider([p_w_in]))
    dx, dx_b, gs['mix_norm'] = _rmsnorm_bwd(x, small['mix_norm'], dn, dx_next, "mix_norm_bwd")
    g = {nm: (p, r) for nm, p, r in zip(out_names, out_parts, out_recv)}
    g['w_in'] = (p_w_in, r_w_in)
    return dx, dx_b, g, gs


def _step(x, target, wb, small, c_idx):
    wb = dict(wb)
    rider, names = _gather(wb, ['ffn1_w_gate', 'ffn1_w_up'])
    _landed(wb, names, _exchange(rider, "gather_first"))
    half_in = wb['w_in'].shape[2] // 2
    x1, s1 = _ffn_forward(x, small['ffn1_norm'], wb, "ffn1", ['ffn1_w_down', ('w_in', 0, half_in)], [('w_in', half_in, 2 * half_in)])
    up_rows = wb['ffn2_w_up'].shape[2]
    up_cut = up_rows // 32 * 15
    x2, s2 = _mixer_forward(x1, wb, small, ['w_att_out', 'w_sg_out', 'w_out', 'ffn2_w_gate'],
                            [('ffn2_w_up', 0, up_cut)], [('ffn2_w_up', up_cut, up_rows)])
    x3, s3 = _ffn_forward(x2, small['ffn2_norm'], wb, "ffn2", ['ffn2_w_down'], None)
    loss, dx3, dx3_b, g_final = _final_loss(x3, small['final_norm'], target, "final_loss")
    gs = {'final_norm': g_final}
    dx2, dx2_b, gs['ffn2_norm'], g = _ffn_backward(x2, small['ffn2_norm'], wb, s3, dx3, dx3_b, c_idx, "ffn2")
    dx1, dx1_b, g_mix, gs_mix = _mixer_backward(x1, wb, small, s2, dx2, dx2_b, c_idx)
    g.update(g_mix)
    gs.update(gs_mix)
    dx0, _, gs['ffn1_norm'], g_ffn1 = _ffn_backward(x, small['ffn1_norm'], wb, s1, dx1, dx1_b, c_idx, "ffn1")
    g.update(g_ffn1)
    return loss, dx0, g, gs


def _cast_into_gathered(wt, p_idx, name):
    r, ccols = wt.shape[0] // 2, wt.shape[1]
    tm = _rows(r, 256)
    nb = r // tm

    def body(p_ref, w_ref, o_ref):
        o_ref[...] = w_ref[...].astype(BF16)

    grid_spec = pltpu.PrefetchScalarGridSpec(
        num_scalar_prefetch=1, grid=(2, nb),
        in_specs=[pl.BlockSpec((tm, ccols), lambda h, i, pr: (h * nb + i, 0))],
        out_specs=pl.BlockSpec((None, None, tm, ccols), lambda h, i, pr: (pr[0], h, i, 0)))
    return _pallas_call(body, out_shape=jax.ShapeDtypeStruct((N_CHIPS, 2, r, ccols), BF16), grid_spec=grid_spec,
                          compiler_params=_cparams(("parallel", "parallel"), 2 * _nbytes((tm, ccols), F32)), name=name)(p_idx, wt)


def _sibling_exchange(grads, name):
    nw = len(grads)

    def body(*refs):
        src, dst = refs[:nw], refs[nw:2 * nw]
        send_sems, recv_sems = refs[2 * nw:]
        x, y, c, _ = _place()
        cps = []
        for i in range(nw):
            cp = pltpu.make_async_remote_copy(src[i].at[:, 1 - c], dst[i], send_sems.at[i], recv_sems.at[i],
                                              device_id=(x, y, 1 - c), device_id_type=MESH)
            cp.start()
            cps.append(cp)
        for cp in cps:
            cp.wait()

    any_spec = pl.BlockSpec(memory_space=pl.ANY)
    return _pallas_call(
        body, out_shape=[jax.ShapeDtypeStruct((g.shape[0],) + g.shape[2:], g.dtype) for g in grads],
        in_specs=[any_spec] * nw, out_specs=[any_spec] * nw,
        scratch_shapes=[pltpu.SemaphoreType.DMA((nw,)), pltpu.SemaphoreType.DMA((nw,))],
        compiler_params=pltpu.CompilerParams(has_side_effects=True), name=name)(*grads)


def _half_exchange(bufs):
    nw = len(bufs)

    def body(*refs):
        dst = refs[nw:2 * nw]
        send_sems, recv_sems = refs[2 * nw:]
        x, y, c, _ = _place()
        cps = []
        for i in range(nw):
            mine = dst[i].at[c]
            cp = pltpu.make_async_remote_copy(mine, mine, send_sems.at[i], recv_sems.at[i],
                                              device_id=(x, y, 1 - c), device_id_type=MESH)
            cp.start()
            cps.append(cp)
        for i, cp in enumerate(cps):
            cp.wait_send()
            theirs = dst[i].at[1 - c]
            pltpu.make_async_remote_copy(theirs, theirs, send_sems.at[i], recv_sems.at[i],
                                         device_id=(x, y, 1 - c), device_id_type=MESH).wait_recv()

    any_spec = pl.BlockSpec(memory_space=pl.ANY)
    return _pallas_call(
        body, out_shape=[jax.ShapeDtypeStruct(b.shape, b.dtype) for b in bufs],
        in_specs=[any_spec] * nw, out_specs=[any_spec] * nw, input_output_aliases={i: i for i in range(nw)},
        scratch_shapes=[pltpu.SemaphoreType.DMA((nw,)), pltpu.SemaphoreType.DMA((nw,))],
        compiler_params=pltpu.CompilerParams(has_side_effects=True), name="rs_halves")(*bufs)


def _sibling_sum(grad, recv, c_idx, name):
    s, _, r, ccols = grad.shape
    tm = _rows(r, 256)

    def body(c_ref, g_ref, r_ref, o_ref):
        o_ref[...] = (g_ref[...].astype(F32) + r_ref[...].astype(F32)).astype(BF16)

    grid_spec = pltpu.PrefetchScalarGridSpec(
        num_scalar_prefetch=1, grid=(s, r // tm),
        in_specs=[pl.BlockSpec((None, None, tm, ccols), lambda q, i, cr: (q, cr[0], i, 0)),
                  pl.BlockSpec((None, tm, ccols), lambda q, i, cr: (q, i, 0))],
        out_specs=pl.BlockSpec((None, tm, ccols), lambda q, i, cr: (q, i, 0)))
    return _pallas_call(body, out_shape=jax.ShapeDtypeStruct((s, r, ccols), BF16), grid_spec=grid_spec,
                          compiler_params=_cparams(("parallel", "parallel"), 4 * _nbytes((tm, ccols), F32)), name=name)(c_idx, grad, recv)


def _chip_sum(part, recv, pc_idx, name):
    _, r, ccols = part.shape
    tm = _rows(r, 256)

    def body(pc_ref, own_ref, r0_ref, r1_ref, r2_ref, o_ref):
        acc = own_ref[...].astype(F32) + r0_ref[...].astype(F32)
        acc = acc + r1_ref[...].astype(F32)
        o_ref[...] = acc + r2_ref[...].astype(F32)

    def slot(j):
        return pl.BlockSpec((None, tm, ccols), lambda i, pc: (j, i, 0))

    grid_spec = pltpu.PrefetchScalarGridSpec(
        num_scalar_prefetch=1, grid=(r // tm,),
        in_specs=[pl.BlockSpec((None, tm, ccols), lambda i, pc: (pc[0], i, 0)), slot(0), slot(1), slot(2)],
        out_specs=pl.BlockSpec((None, tm, ccols), lambda i, pc: (pc[1], i, 0)))
    return _pallas_call(body, out_shape=jax.ShapeDtypeStruct((2, r, ccols), F32), grid_spec=grid_spec,
                          compiler_params=_cparams(("parallel",), 6 * _nbytes((tm, ccols), F32)), name=name)(pc_idx, part, recv, recv, recv)


def _all_reduce_small(vec):
    _, r, _ = vec.shape

    def body(v_ref, o_ref, parts, send1, recv1, send2, recv2):
        x, y, c, _ = _place()
        me = 4 * x + 2 * y + c
        peers = []
        for k in range(1, N_DEV):
            px, py, pc = (1 - x if k & 4 else x, 1 - y if k & 2 else y, 1 - c if k & 1 else c)
            peers.append(((px, py, pc), 4 * px + 2 * py + pc))
        parts[me] = v_ref[me]
        cps = []
        for k, (peer, peer_id) in enumerate(peers):
            cp = pltpu.make_async_remote_copy(v_ref.at[peer_id], parts.at[me], send1.at[k], recv1.at[k],
                                              device_id=peer, device_id_type=MESH)
            cp.start()
            cps.append(cp)
        for cp in cps:
            cp.wait()
        acc = parts[0]
        for dev in range(1, N_DEV):
            acc = acc + parts[dev]
        o_ref[me] = acc
        cps = []
        for k, (peer, _) in enumerate(peers):
            cp = pltpu.make_async_remote_copy(o_ref.at[me], o_ref.at[me], send2.at[k], recv2.at[k],
                                              device_id=peer, device_id_type=MESH)
            cp.start()
            cps.append(cp)
        for cp in cps:
            cp.wait()

    vm = pl.BlockSpec(memory_space=pltpu.VMEM)
    sems = pltpu.SemaphoreType.DMA((N_DEV - 1,))
    return pl.pallas_call(
        body, out_shape=jax.ShapeDtypeStruct(vec.shape, F32), in_specs=[vm], out_specs=vm,
        scratch_shapes=[pltpu.VMEM((N_DEV, r, LANES), F32), sems, sems, sems, sems],
        compiler_params=pltpu.CompilerParams(vmem_limit_bytes=int(8 * _nbytes((N_DEV, r, LANES), F32))),
        name="all_reduce_small")(vec)


def _adamw(wt, g, m, v, name):
    r, ccols = wt.shape
    tm = _rows(r, max(8, (MIB // (4 * ccols)) // 8 * 8))
    blk = pl.BlockSpec((tm, ccols), lambda i: (i, 0))

    def body(w_ref, g_ref, m_ref, v_ref, go_ref, d_ref, mo_ref, vo_ref):
        gv = g_ref[...]
        go_ref[...] = gv
        mv = ADAM_B1 * m_ref[...] + (1.0 - ADAM_B1) * gv
        vv = ADAM_B2 * v_ref[...] + (1.0 - ADAM_B2) * (gv * gv)
        m_hat = mv / (1.0 - ADAM_B1 ** ADAM_STEP)
        v_hat = vv / (1.0 - ADAM_B2 ** ADAM_STEP)
        d_ref[...] = -ADAM_LR * (m_hat / (jnp.sqrt(v_hat) + ADAM_EPS) + ADAM_WD * w_ref[...])
        mo_ref[...] = mv
        vo_ref[...] = vv

    out = jax.ShapeDtypeStruct((r, ccols), F32)
    return _pallas_call(body, out_shape=(out, out, out, out), grid=(r // tm,), in_specs=[blk] * 4, out_specs=(blk,) * 4,
                          compiler_params=_cparams(("parallel",), 8 * _nbytes((tm, ccols), F32)), name=name)(wt, g, m, v)


def _as_rows(a):
    rows = a.reshape(-1, LANES)
    return jnp.pad(rows, ((0, -rows.shape[0] % 8), (0, 0)))


def kernel(x, ffn1_norm, ffn1_w_gate, ffn1_w_up, ffn1_w_down, mix_norm, w_in, sg_ln_g, sg_ln_b, sg_w, sg_b, w_att_out, w_sg_out, w_out, ffn2_norm, ffn2_w_gate, ffn2_w_up, ffn2_w_down, final_norm, loss_target, m_ffn1_norm, m_ffn1_w_gate, m_ffn1_w_up, m_ffn1_w_down, m_mix_norm, m_w_in, m_sg_ln_g, m_sg_ln_b, m_sg_w, m_sg_b, m_w_att_out, m_w_sg_out, m_w_out, m_ffn2_norm, m_ffn2_w_gate, m_ffn2_w_up, m_ffn2_w_down, m_final_norm, v_ffn1_norm, v_ffn1_w_gate, v_ffn1_w_up, v_ffn1_w_down, v_mix_norm, v_w_in, v_sg_ln_g, v_sg_ln_b, v_sg_w, v_sg_b, v_w_att_out, v_w_sg_out, v_w_out, v_ffn2_norm, v_ffn2_w_gate, v_ffn2_w_up, v_ffn2_w_down, v_final_norm):
    given = dict(locals())
    wts = {n: given[n] for n in WEIGHT_NAMES}
    ms = {n: given["m_" + n] for n in WEIGHT_NAMES}
    vs = {n: given["v_" + n] for n in WEIGHT_NAMES}
    t, d = x.shape[-2], x.shape[-1]
    xc, yc, cc = lax.axis_index("x"), lax.axis_index("y"), lax.axis_index("c")

    shard2d = {n: wts[n].reshape(wts[n].shape[-2:]) for n in BIG_NAMES}
    p_idx = jnp.reshape(2 * xc + yc, (1,)).astype(jnp.int32)
    c_idx = jnp.reshape(cc, (1,)).astype(jnp.int32)
    pc_idx = jnp.stack([2 * xc + yc, cc]).astype(jnp.int32)
    wb = {n: _cast_into_gathered(shard2d[n], p_idx, f"cast_{n}") for n in BIG_NAMES}

    small = {n: wts[n].reshape(-1, wts[n].shape[-1]) for n in SMALL_NAMES}
    small['sg_w'] = wts['sg_w'].reshape(wts['sg_w'].shape[-3:])
    loss, dx, g, gs = _step(x.reshape(t, d), loss_target.reshape(t, d), wb, small, c_idx)
    loss = lax.psum(loss[0, 0], ("x", "y", "c"))

    my_halves = [_chip_sum(*g[n], pc_idx, f"rs_sum2_{n}") for n in BIG_NAMES]
    reduced = _half_exchange(my_halves)
    grads = {n: r.reshape(shard2d[n].shape) for n, r in zip(BIG_NAMES, reduced)}

    def pack(tree):
        rows = jnp.concatenate([_as_rows(tree[n]) for n in SMALL_NAMES], axis=0)
        return jnp.pad(rows, ((0, -rows.shape[0] % (8 * N_DEV)), (0, 0)))

    packed = pack(gs)
    packed = _all_reduce_small(packed.reshape(N_DEV, -1, LANES)).reshape(packed.shape)

    delta, new_m, new_v = {}, {}, {}
    for n in BIG_NAMES:
        shape = wts[n].shape
        out = _adamw(shard2d[n], grads[n], ms[n].reshape(shard2d[n].shape), vs[n].reshape(shard2d[n].shape), f"adamw_{n}")
        grads[n], delta[n], new_m[n], new_v[n] = (a.reshape(shape) for a in out)

    small_out = _adamw(pack(wts), packed, pack(ms), pack(vs), "adamw_small")
    row = 0
    for n in SMALL_NAMES:
        shape = wts[n].shape
        sz = wts[n].size // LANES
        grads[n], delta[n], new_m[n], new_v[n] = (a[row:row + sz].reshape(shape) for a in small_out)
        row += sz + -sz % 8

    return (loss, dx.reshape(x.shape), *[grads[n] for n in WEIGHT_NAMES], *[delta[n] for n in WEIGHT_NAMES],
            *[new_m[n] for n in WEIGHT_NAMES], *[new_v[n] for n in WEIGHT_NAMES])
```

```python
import functools

import jax
import jax.numpy as jnp
from jax import lax
from jax.experimental import pallas as pl
from jax.experimental.pallas import tpu as pltpu

F32 = jnp.float32
BF16 = jnp.bfloat16
MESH = pl.DeviceIdType.MESH

NORM_EPS = 1e-6
LN_EPS = 1e-5
HEAD_DIM = 128
HEADS_PER_GROUP = 4
GROUP_WIDTH = HEADS_PER_GROUP * HEAD_DIM
DILATIONS = (1, 4, 16)
N_GROUPS = len(DILATIONS)
ATT_BLOCK = 128
ROPE_DIM = HEAD_DIM // 4
ROPE_THETA = 500000.0
SG_CHUNK = 128
SG_GROUPS = 12
SG_GROUP_DIM = 128
MASKED = -1e30

ADAM_LR = 0.001
ADAM_B1 = 0.9
ADAM_B2 = 0.999
ADAM_EPS = 1e-08
ADAM_WD = 0.01
ADAM_STEP = 10

N_CHIPS = 4
N_DEV = 8
LANES = 128
MIB = 2 ** 20
VMEM_BYTES_V7X = 64 * MIB

WEIGHT_NAMES = ['ffn1_norm', 'ffn1_w_gate', 'ffn1_w_up', 'ffn1_w_down', 'mix_norm', 'w_in', 'sg_ln_g', 'sg_ln_b',
                'sg_w', 'sg_b', 'w_att_out', 'w_sg_out', 'w_out', 'ffn2_norm', 'ffn2_w_gate', 'ffn2_w_up',
                'ffn2_w_down', 'final_norm']
BIG = [('ffn1_w_gate', 1), ('ffn1_w_up', 1), ('ffn1_w_down', 0), ('w_in', 1), ('w_att_out', 1), ('w_sg_out', 1),
       ('w_out', 0), ('ffn2_w_gate', 1), ('ffn2_w_up', 1), ('ffn2_w_down', 0)]
BIG_NAMES = [n for n, _ in BIG]
SMALL_NAMES = [n for n in WEIGHT_NAMES if n not in BIG_NAMES]


def _nbytes(shape, dtype):
    n = jnp.dtype(dtype).itemsize
    for s in shape:
        if s is not None:
            n *= s
    return n


def _pallas_call(*args, **kw):
    kw['out_shape'] = jax.tree.map(lambda s: pltpu.HBM(s.shape, s.dtype), kw['out_shape'])
    call = pl.pallas_call(*args, **kw)

    def pinned(*operands):
        return call(*[o if jnp.issubdtype(o.dtype, jnp.integer) else pltpu.with_memory_space_constraint(o, pltpu.HBM)
                      for o in operands])

    return pinned


def _cparams(sem, block_bytes, **kw):
    limit = int(min(max(3 * block_bytes, 32 * MIB), VMEM_BYTES_V7X - 8 * MIB))
    return pltpu.CompilerParams(dimension_semantics=sem, vmem_limit_bytes=limit, **kw)


def _tile(dim, cap):
    best = None
    for t in range(LANES, min(dim, cap) + 1, LANES):
        if dim % t == 0:
            best = t
    if best is None:
        assert dim <= cap, (dim, cap)
        return dim
    return best


def _rows(dim, cap):
    best = None
    for t in range(8, min(dim, cap) + 1, 8):
        if dim % t == 0:
            best = t
    assert best is not None, (dim, cap)
    return best


def _place():
    x, y, c = lax.axis_index("x"), lax.axis_index("y"), lax.axis_index("c")
    others = [(1 - x, y), (x, 1 - y), (1 - x, 1 - y)]
    return x, y, c, others


class _Rider:
    def __init__(self, operands, out_shapes, aliases, sems, start, finish):
        self.operands = operands
        self.out_shapes = out_shapes
        self.aliases = aliases
        self.sems = sems
        self.start = start
        self.finish = finish


def _run(body, *, name, grid, in_specs, out_specs, out_shape, scratch_shapes, operands, block_bytes, rider=None):
    if rider is None:
        sem = ("parallel",) * (len(grid) - 1) + ("arbitrary",)
        return _pallas_call(body, out_shape=out_shape, grid=grid, in_specs=in_specs, out_specs=out_specs,
                              scratch_shapes=scratch_shapes, compiler_params=_cparams(sem, block_bytes), name=name)(*operands)
    n_in, n_out, n_scr = len(operands), len(out_shape), len(scratch_shapes)
    r_in, r_out = len(rider.operands), len(rider.out_shapes)
    any_spec = pl.BlockSpec(memory_space=pl.ANY)

    def wrapped(*refs):
        ins, refs = refs[:n_in], refs[n_in:]
        r_ins, refs = refs[:r_in], refs[r_in:]
        outs, refs = refs[:n_out], refs[n_out:]
        r_outs, refs = refs[:r_out], refs[r_out:]
        scr, sems = refs[:n_scr], refs[n_scr:]
        if not grid:
            rider.start(r_ins, r_outs, sems)
            rider.finish(r_ins, r_outs, sems)
            return
        ids = [pl.program_id(a) for a in range(len(grid))]
        first = functools.reduce(jnp.logical_and, [i == 0 for i in ids])
        last = functools.reduce(jnp.logical_and, [i == g - 1 for i, g in zip(ids, grid)])

        @pl.when(first)
        def _():
            rider.start(r_ins, r_outs, sems)

        body(*ins, *outs, *scr)

        @pl.when(last)
        def _():
            rider.finish(r_ins, r_outs, sems)

    results = _pallas_call(
        wrapped, out_shape=list(out_shape) + list(rider.out_shapes), grid=grid,
        in_specs=list(in_specs) + [any_spec] * r_in, out_specs=list(out_specs) + [any_spec] * r_out,
        scratch_shapes=list(scratch_shapes) + list(rider.sems),
        input_output_aliases={n_in + k: n_out + v for k, v in rider.aliases.items()},
        compiler_params=_cparams(("arbitrary",) * len(grid) if grid else None, block_bytes, has_side_effects=True),
        name=name)(*operands, *rider.operands)
    return results[:n_out], results[n_out:]


def _exchange(rider, name):
    return _run(None, name=name, grid=(), in_specs=[], out_specs=[], out_shape=[], scratch_shapes=[], operands=[],
                block_bytes=0, rider=rider)[1]


def _gather_rider(items):
    bufs, index = [], []
    for b, r0, r1 in items:
        if not any(b is q for q in bufs):
            bufs.append(b)
        index.append(([k for k, q in enumerate(bufs) if q is b][0], r0, r1))
    n = len(index)

    def piece(refs, k, chip, half):
        bi, r0, r1 = index[k]
        return refs[bi].at[chip, half, pl.ds(r0, r1 - r0)]

    def copy(ref, sem_pair, k, j, to):
        return pltpu.make_async_remote_copy(ref, ref, sem_pair[0].at[k, j], sem_pair[1].at[k, j], device_id=to, device_id_type=MESH)

    def start(r_ins, buf, sems):
        x, y, c, others = _place()
        for k in range(n):
            for j, (ox, oy) in enumerate(others):
                copy(piece(buf, k, 2 * x + y, c), sems[:2], k, j, (ox, oy, c)).start()

    def finish(r_ins, buf, sems):
        x, y, c, others = _place()
        for k in range(n):
            for j, (ox, oy) in enumerate(others):
                got = piece(buf, k, 2 * ox + oy, c)
                copy(got, sems[:2], k, j, (ox, oy, c)).wait_recv()
                copy(got, sems[2:], k, j, (x, y, 1 - c)).start()
        for k in range(n):
            for j, (ox, oy) in enumerate(others):
                copy(piece(buf, k, 2 * ox + oy, 1 - c), sems[2:], k, j, (x, y, 1 - c)).wait_recv()
        for k in range(n):
            for j, (ox, oy) in enumerate(others):
                copy(piece(buf, k, 2 * x + y, c), sems[:2], k, j, (ox, oy, c)).wait_send()
                copy(piece(buf, k, 2 * ox + oy, c), sems[2:], k, j, (x, y, 1 - c)).wait_send()

    return _Rider(bufs, [jax.ShapeDtypeStruct(b.shape, b.dtype) for b in bufs], {i: i for i in range(len(bufs))},
                  [pltpu.SemaphoreType.DMA((n, 3))] * 4, start, finish)


def _scatter_rider(parts):
    n = len(parts)

    def copy(src, dst, sems, i, j, to):
        return pltpu.make_async_remote_copy(src, dst, sems[0].at[i, j], sems[1].at[i, j], device_id=to, device_id_type=MESH)

    def start(src, dst, sems):
        x, y, c, others = _place()
        for i in range(n):
            for j, (ox, oy) in enumerate(others):
                copy(src[i].at[2 * ox + oy], dst[i].at[j], sems, i, j, (ox, oy, c)).start()

    def finish(src, dst, sems):
        x, y, c, others = _place()
        for i in range(n):
            for j, (ox, oy) in enumerate(others):
                copy(src[i].at[2 * ox + oy], dst[i].at[j], sems, i, j, (ox, oy, c)).wait()

    return _Rider(parts, [jax.ShapeDtypeStruct((3,) + p.shape[1:], p.dtype) for p in parts], {},
                  [pltpu.SemaphoreType.DMA((n, 3))] * 2, start, finish)


def _matmul(pairs, mode, out_dtype, name, *, scale=1.0, residual=None, b3=False, out3=0, caps=(1024, 1024, 512), rider=None):
    a0, b0 = pairs[0]
    if mode == 'nn':
        m, k = a0.shape
        n = b0.shape[0] * b0.shape[2] if b3 else b0.shape[1]
    elif mode == 'nt':
        m = a0.shape[0]
        n, k = (b0.shape[1], b0.shape[0] * b0.shape[2]) if b3 else b0.shape
    else:
        k, m = a0.shape
        n = b0.shape[1]
    tm = _tile(m, caps[0])
    tn = _tile(n, caps[1])
    tk = _tile(k, caps[2])
    if b3 and mode == 'nn':
        tn = b0.shape[2]
    if b3 and mode == 'nt':
        tk = b0.shape[2]
    if out3:
        tn = n // out3
    nk = k // tk
    if mode == 'tn':
        a_spec = pl.BlockSpec((tk, tm), lambda i, j, kk: (kk, i))
        b_spec = pl.BlockSpec((tk, tn), lambda i, j, kk: (kk, j))
        dims = ((0,), (0,))
    elif mode == 'nn':
        a_spec = pl.BlockSpec((tm, tk), lambda i, j, kk: (i, kk))
        b_spec = (pl.BlockSpec((None, tk, tn), lambda i, j, kk: (j, kk, 0)) if b3
                  else pl.BlockSpec((tk, tn), lambda i, j, kk: (kk, j)))
        dims = ((1,), (0,))
    else:
        a_spec = pl.BlockSpec((tm, tk), lambda i, j, kk: (i, kk))
        b_spec = (pl.BlockSpec((None, tn, tk), lambda i, j, kk: (kk, j, 0)) if b3
                  else pl.BlockSpec((tn, tk), lambda i, j, kk: (j, kk)))
        dims = ((1,), (1,))
    in_specs, operands = [], []
    for a, b in pairs:
        in_specs += [a_spec, b_spec]
        operands += [a, b]
    block_bytes = len(pairs) * (_nbytes((tm, tk), a0.dtype) + _nbytes((tk, tn), b0.dtype))
    if residual is not None:
        in_specs.append(pl.BlockSpec((tm, tn), lambda i, j, kk: (i, j)))
        operands.append(residual)
        block_bytes += _nbytes((tm, tn), F32)
    if out3:
        out_spec = pl.BlockSpec((None, tm, tn), lambda i, j, kk: (j, i, 0))
        out_shape = jax.ShapeDtypeStruct((out3, m, tn), out_dtype)
    else:
        out_spec = pl.BlockSpec((tm, tn), lambda i, j, kk: (i, j))
        out_shape = jax.ShapeDtypeStruct((m, n), out_dtype)
    block_bytes += _nbytes((tm, tn), out_dtype) + _nbytes((tm, tn), F32)
    n_pairs = len(pairs)
    has_res = residual is not None

    def body(*refs):
        o_ref, acc = refs[-2], refs[-1]
        kk = pl.program_id(2)

        def product():
            part = None
            for p in range(n_pairs):
                d = lax.dot_general(refs[2 * p][...].astype(BF16), refs[2 * p + 1][...].astype(BF16),
                                    (dims, ((), ())), preferred_element_type=F32)
                part = d if part is None else part + d
            return part

        def finish(r):
            if scale != 1.0:
                r = r * scale
            if has_res:
                r = refs[2 * n_pairs][...] + r
            o_ref[...] = r.astype(out_dtype)

        if nk == 1:
            finish(product())
            return

        @pl.when(kk == 0)
        def _():
            acc[...] = product()

        if nk > 2:
            @pl.when(jnp.logical_and(kk > 0, kk < nk - 1))
            def _():
                acc[...] += product()

        @pl.when(kk == nk - 1)
        def _():
            finish(acc[...] + product())

    res = _run(body, name=name, grid=(m // tm, n // tn, nk), in_specs=in_specs, out_specs=[out_spec], out_shape=[out_shape],
               scratch_shapes=[pltpu.VMEM((tm, tn), F32)], operands=operands, block_bytes=block_bytes, rider=rider)
    return res[0] if rider is None else (res[0][0], res[1])


def _rmsnorm_fwd(x, g, name):
    t, d = x.shape
    tm = _rows(t, 512)

    def body(x_ref, g_ref, o_ref):
        xv = x_ref[...]
        r = lax.rsqrt(jnp.mean(xv * xv, axis=1, keepdims=True) + NORM_EPS)
        o_ref[...] = (xv * r * g_ref[...]).astype(BF16)

    row = pl.BlockSpec((tm, d), lambda i: (i, 0))
    return _pallas_call(
        body, out_shape=jax.ShapeDtypeStruct((t, d), BF16), grid=(t // tm,),
        in_specs=[row, pl.BlockSpec((1, d), lambda i: (0, 0))], out_specs=row,
        compiler_params=_cparams(("parallel",), 2 * _nbytes((tm, d), F32)), name=name)(x, g)


def _rms_grad(xv, g, dn, d):
    r = lax.rsqrt(jnp.mean(xv * xv, axis=1, keepdims=True) + NORM_EPS)
    u = dn * g
    s = jnp.sum(xv * u, axis=1, keepdims=True)
    dx = r * u - xv * (r * r * r) * (s * (1.0 / d))
    return dx, dn * xv * r


def _rmsnorm_bwd(x, g, dn, dres, name):
    t, d = x.shape
    tm = _rows(t, 256)

    def body(x_ref, g_ref, dn_ref, dres_ref, dx_ref, dxb_ref, dg_ref):
        dx, dg_rows = _rms_grad(x_ref[...], g_ref[...], dn_ref[...].astype(F32), d)
        dx = dres_ref[...] + dx
        dx_ref[...] = dx
        dxb_ref[...] = dx.astype(BF16)

        @pl.when(pl.program_id(0) == 0)
        def _():
            dg_ref[...] = jnp.zeros_like(dg_ref)

        dg_ref[...] += jnp.sum(dg_rows, axis=0, keepdims=True)

    row = pl.BlockSpec((tm, d), lambda i: (i, 0))
    vec = pl.BlockSpec((1, d), lambda i: (0, 0))
    return _pallas_call(
        body, out_shape=(jax.ShapeDtypeStruct((t, d), F32), jax.ShapeDtypeStruct((t, d), BF16), jax.ShapeDtypeStruct((1, d), F32)),
        grid=(t // tm,), in_specs=[row, vec, row, row], out_specs=(row, row, vec),
        compiler_params=_cparams(("arbitrary",), 5 * _nbytes((tm, d), F32)), name=name)(x, g, dn, dres)


def _final_loss(x, g, target, name):
    t, d = x.shape
    tm = _rows(t, 256)

    def body(x_ref, g_ref, t_ref, loss_ref, dx_ref, dxb_ref, dg_ref):
        xv, gv = x_ref[...], g_ref[...]
        r = lax.rsqrt(jnp.mean(xv * xv, axis=1, keepdims=True) + NORM_EPS)
        err = xv * r * gv - t_ref[...]
        dx, dg_rows = _rms_grad(xv, gv, err * (1.0 / d), d)
        dx_ref[...] = dx
        dxb_ref[...] = dx.astype(BF16)

        @pl.when(pl.program_id(0) == 0)
        def _():
            dg_ref[...] = jnp.zeros_like(dg_ref)
            loss_ref[...] = jnp.zeros_like(loss_ref)

        dg_ref[...] += jnp.sum(dg_rows, axis=0, keepdims=True)
        row_loss = jnp.sum(err * err, axis=1, keepdims=True) * (0.5 / d)
        loss_ref[...] += jnp.sum(row_loss, axis=0, keepdims=True)

    row = pl.BlockSpec((tm, d), lambda i: (i, 0))
    vec = pl.BlockSpec((1, d), lambda i: (0, 0))
    return _pallas_call(
        body, out_shape=(jax.ShapeDtypeStruct((1, 1), F32), jax.ShapeDtypeStruct((t, d), F32),
                         jax.ShapeDtypeStruct((t, d), BF16), jax.ShapeDtypeStruct((1, d), F32)),
        grid=(t // tm,), in_specs=[row, vec, row], out_specs=(pl.BlockSpec((1, 1), lambda i: (0, 0)), row, row, vec),
        compiler_params=_cparams(("arbitrary",), 4 * _nbytes((tm, d), F32)), name=name)(x, g, target)


def _ffn_up(n, wg, wu, name, rider=None):
    t, d = n.shape
    s, _, f = wg.shape
    tm, tk = _tile(t, 1024), _tile(d, 1024)
    nk = d // tk

    def body(n_ref, wg_ref, wu_ref, a_ref, b_ref, h_ref, acc_g, acc_u):
        kk = pl.program_id(2)

        def products():
            nv = n_ref[...]
            return jnp.dot(nv, wg_ref[...], preferred_element_type=F32), jnp.dot(nv, wu_ref[...], preferred_element_type=F32)

        def finish(a, b):
            a_ref[...] = a.astype(BF16)
            b_ref[...] = b.astype(BF16)
            h_ref[...] = (a * jax.nn.sigmoid(a) * b).astype(BF16)

        if nk == 1:
            finish(*products())
            return

        @pl.when(kk == 0)
        def _():
            acc_g[...], acc_u[...] = products()

        if nk > 2:
            @pl.when(jnp.logical_and(kk > 0, kk < nk - 1))
            def _():
                pg, pu = products()
                acc_g[...] += pg
                acc_u[...] += pu

        @pl.when(kk == nk - 1)
        def _():
            pg, pu = products()
            finish(acc_g[...] + pg, acc_u[...] + pu)

    w_spec = pl.BlockSpec((None, tk, f), lambda i, j, kk: (j, kk, 0))
    o_spec = pl.BlockSpec((tm, f), lambda i, j, kk: (i, j))
    out = jax.ShapeDtypeStruct((t, s * f), BF16)
    block_bytes = _nbytes((tm, tk), BF16) + 2 * _nbytes((tk, f), BF16) + 3 * _nbytes((tm, f), BF16) + 2 * _nbytes((tm, f), F32)
    return _run(body, name=name, grid=(t // tm, s, nk),
                in_specs=[pl.BlockSpec((tm, tk), lambda i, j, kk: (i, kk)), w_spec, w_spec], out_specs=[o_spec, o_spec, o_spec],
                out_shape=[out, out, out], scratch_shapes=[pltpu.VMEM((tm, f), F32), pltpu.VMEM((tm, f), F32)],
                operands=[n, wg, wu], block_bytes=block_bytes, rider=rider)


def _ffn_bwd_act(dx, wd, a, b, name):
    t, d = dx.shape
    f = wd.shape[0]
    tm, tn, tk = _tile(t, 1024), _tile(f, 1536), _tile(d, 1024)
    nk = d // tk

    def body(dx_ref, wd_ref, a_ref, b_ref, da_ref, db_ref, acc):
        kk = pl.program_id(2)

        def product():
            return lax.dot_general(dx_ref[...], wd_ref[...], (((1,), (1,)), ((), ())), preferred_element_type=F32)

        def finish(r):
            dh = 0.5 * r
            av, bv = a_ref[...].astype(F32), b_ref[...].astype(F32)
            sg = jax.nn.sigmoid(av)
            da_ref[...] = (dh * bv * (sg * (1.0 + av * (1.0 - sg)))).astype(BF16)
            db_ref[...] = (dh * (av * sg)).astype(BF16)

        if nk == 1:
            finish(product())
            return

        @pl.when(kk == 0)
        def _():
            acc[...] = product()

        if nk > 2:
            @pl.when(jnp.logical_and(kk > 0, kk < nk - 1))
            def _():
                acc[...] += product()

        @pl.when(kk == nk - 1)
        def _():
            finish(acc[...] + product())

    act = pl.BlockSpec((tm, tn), lambda i, j, kk: (i, j))
    out = jax.ShapeDtypeStruct((t, f), BF16)
    block_bytes = _nbytes((tm, tk), BF16) + _nbytes((tn, tk), BF16) + 4 * _nbytes((tm, tn), BF16) + _nbytes((tm, tn), F32)
    return _pallas_call(
        body, out_shape=(out, out), grid=(t // tm, f // tn, nk),
        in_specs=[pl.BlockSpec((tm, tk), lambda i, j, kk: (i, kk)), pl.BlockSpec((tn, tk), lambda i, j, kk: (j, kk)),
                  act, act],
        out_specs=(act, act), scratch_shapes=[pltpu.VMEM((tm, tn), F32)],
        compiler_params=_cparams(("parallel", "parallel", "arbitrary"), block_bytes), name=name)(dx, wd, a, b)


AXIS = dict(BIG)


def _full(wb, n):
    _, _, r, ccols = wb[n].shape
    return wb[n].reshape(N_CHIPS, 2 * r, ccols) if AXIS[n] == 1 else wb[n].reshape(N_CHIPS * 2 * r, ccols)


def _gather(wb, specs):
    items, names = [], []
    for s in specs:
        n, r0, r1 = (s, 0, wb[s].shape[2]) if isinstance(s, str) else s
        items.append((wb[n], r0, r1))
        if n not in names:
            names.append(n)
    return _gather_rider(items), names


def _landed(wb, names, results):
    for n, r in zip(names, results):
        wb[n] = r


def _reduce_first(grads, names, wb, c_idx):
    g4 = [g.reshape(wb[n].shape) for g, n in zip(grads, names)]
    from_sibling = _sibling_exchange(g4, "rs_sibling_" + names[0])
    return [_sibling_sum(a, b, c_idx, f"rs_sum1_{n}") for a, b, n in zip(g4, from_sibling, names)]


def _ffn_forward(x, gain, wb, tag, up_specs, down_specs):
    n = _rmsnorm_fwd(x, gain, f"{tag}_norm")
    rider, names = _gather(wb, up_specs)
    (a, b, h), got = _ffn_up(n, _full(wb, f"{tag}_w_gate"), _full(wb, f"{tag}_w_up"), f"{tag}_up", rider=rider)
    _landed(wb, names, got)
    down = dict(scale=0.5, residual=x, caps=(1024, 1024, 1536))
    if down_specs:
        rider, names = _gather(wb, down_specs)
        x_next, got = _matmul([(h, _full(wb, f"{tag}_w_down"))], 'nn', F32, f"{tag}_down", rider=rider, **down)
        _landed(wb, names, got)
    else:
        x_next = _matmul([(h, _full(wb, f"{tag}_w_down"))], 'nn', F32, f"{tag}_down", **down)
    return x_next, (n, a, b, h)


def _ffn_backward(x, gain, wb, saved, dx_next, dx_next_b, c_idx, tag):
    n, a, b, h = saved
    wg, wu, wd = (f"{tag}_w_gate", f"{tag}_w_up", f"{tag}_w_down")
    da, db = _ffn_bwd_act(dx_next_b, _full(wb, wd), a, b, f"{tag}_bwd_act")
    g_wd = _matmul([(h, dx_next_b)], 'tn', BF16, f"{tag}_dwd", scale=0.5, caps=(1536, 1024, 1024))
    (p_wd,) = _reduce_first([g_wd], [wd], wb, c_idx)
    g_wg, (r_wd,) = _matmul([(n, da)], 'tn', BF16, f"{tag}_dwg", out3=N_CHIPS, caps=(1024, 1024, 1024), rider=_scatter_rider([p_wd]))
    g_wu = _matmul([(n, db)], 'tn', BF16, f"{tag}_dwu", out3=N_CHIPS, caps=(1024, 1024, 1024))
    p_wg, p_wu = _reduce_first([g_wg, g_wu], [wg, wu], wb, c_idx)
    dn, (r_wg, r_wu) = _matmul([(da, _full(wb, wg)), (db, _full(wb, wu))], 'nt', F32, f"{tag}_dn", b3=True,
                               rider=_scatter_rider([p_wg, p_wu]))
    dx, dx_b, g_gain = _rmsnorm_bwd(x, gain, dn, dx_next, f"{tag}_norm_bwd")
    return dx, dx_b, g_gain, {wg: (p_wg, r_wg), wu: (p_wu, r_wu), wd: (p_wd, r_wd)}


def _rope_tables(seq):
    half = ROPE_DIM // 2
    inv_freq = ROPE_THETA ** (-jnp.arange(0, ROPE_DIM, 2, dtype=F32) / ROPE_DIM)
    ang = jnp.arange(seq).astype(F32)[:, None] * inv_freq[None, :]
    cos, sin = jnp.cos(ang), jnp.sin(ang)
    zeros = lambda w: jnp.zeros((seq, w), F32)
    c = jnp.concatenate([cos, cos, jnp.ones((seq, HEAD_DIM - ROPE_DIM), F32)], axis=1)
    s_up = jnp.concatenate([-sin, zeros(HEAD_DIM - half)], axis=1)
    s_dn = jnp.concatenate([zeros(half), sin, zeros(HEAD_DIM - ROPE_DIM)], axis=1)
    return c, s_up, s_dn


def _rope(x, width, tables, name):
    t = x.shape[0]
    tm = _rows(t, 512)
    half = ROPE_DIM // 2
    c, s_up, s_dn = tables

    def body(x_ref, c_ref, up_ref, dn_ref, o_ref):
        cv, uv, dv = c_ref[...], up_ref[...], dn_ref[...]
        for h in range(GROUP_WIDTH // HEAD_DIM):
            sl = slice(h * HEAD_DIM, (h + 1) * HEAD_DIM)
            xv = x_ref[:, sl].astype(F32)
            o_ref[:, sl] = (xv * cv + pltpu.roll(xv, HEAD_DIM - half, 1) * uv + pltpu.roll(xv, half, 1) * dv).astype(BF16)

    blk = pl.BlockSpec((tm, GROUP_WIDTH), lambda i, j: (i, j))
    tab = pl.BlockSpec((tm, HEAD_DIM), lambda i, j: (i, 0))
    return _pallas_call(
        body, out_shape=jax.ShapeDtypeStruct((t, width), BF16), grid=(t // tm, width // GROUP_WIDTH),
        in_specs=[blk, tab, tab, tab], out_specs=blk,
        compiler_params=_cparams(("parallel", "parallel"), 2 * _nbytes((tm, GROUP_WIDTH), F32)), name=name)(x, c, s_up, s_dn)


def _query_mask(has_prev):
    qi = lax.broadcasted_iota(jnp.int32, (ATT_BLOCK, 2 * ATT_BLOCK), 0)
    col = lax.broadcasted_iota(jnp.int32, (ATT_BLOCK, 2 * ATT_BLOCK), 1)
    prev = jnp.logical_and(jnp.logical_and(col < ATT_BLOCK, col >= qi), has_prev)
    return jnp.logical_or(prev, jnp.logical_and(col >= ATT_BLOCK, col - ATT_BLOCK <= qi))


def _key_mask(has_next):
    row = lax.broadcasted_iota(jnp.int32, (2 * ATT_BLOCK, ATT_BLOCK), 0)
    kj = lax.broadcasted_iota(jnp.int32, (2 * ATT_BLOCK, ATT_BLOCK), 1)
    nxt = jnp.logical_and(jnp.logical_and(row >= ATT_BLOCK, kj >= row - ATT_BLOCK), has_next)
    return jnp.logical_or(nxt, jnp.logical_and(row < ATT_BLOCK, kj <= row))


def _scores(q, k):
    return lax.dot_general(q, k, (((1,), (1,)), ((), ())), preferred_element_type=F32) * (HEAD_DIM ** -0.5)


def _att_fwd(q, k, v, offs, dil, name):
    qo, ko, vo = offs
    length = q.shape[0]
    nb = length // ATT_BLOCK

    def body(q_ref, kp_ref, kc_ref, vp_ref, vc_ref, o_ref, lse_ref):
        mask = _query_mask(pl.program_id(1) > 0)
        heads = [slice(h * HEAD_DIM, (h + 1) * HEAD_DIM) for h in range(HEADS_PER_GROUP)]
        ks = [jnp.concatenate([kp_ref[:, sl], kc_ref[:, sl]], axis=0) for sl in heads]
        vs = [jnp.concatenate([vp_ref[:, sl], vc_ref[:, sl]], axis=0) for sl in heads]
        ss = [jnp.where(mask, _scores(q_ref[:, sl], kv), MASKED) for sl, kv in zip(heads, ks)]
        ms = [jnp.max(s, axis=1, keepdims=True) for s in ss]
        ps = [jnp.exp(s - m) for s, m in zip(ss, ms)]
        ls = [jnp.sum(p, axis=1, keepdims=True) for p in ps]
        accs = [jnp.dot(p.astype(BF16), vv, preferred_element_type=F32) for p, vv in zip(ps, vs)]
        for sl, acc, m, l in zip(heads, accs, ms, ls):
            o_ref[:, sl] = acc / l
            lse_ref[:, sl] = jnp.broadcast_to(m + jnp.log(l), (ATT_BLOCK, HEAD_DIM))

    def spec(off, prev):
        if prev:
            return pl.BlockSpec((ATT_BLOCK, GROUP_WIDTH), lambda r, n: (jnp.maximum(n - 1, 0), off + r))
        return pl.BlockSpec((ATT_BLOCK, GROUP_WIDTH), lambda r, n: (n, off + r))

    out = jax.ShapeDtypeStruct((length, dil * GROUP_WIDTH), F32)
    o_spec = pl.BlockSpec((ATT_BLOCK, GROUP_WIDTH), lambda r, n: (n, r))
    return _pallas_call(
        body, out_shape=(out, out), grid=(dil, nb),
        in_specs=[spec(qo, False), spec(ko, True), spec(ko, False), spec(vo, True), spec(vo, False)],
        out_specs=(o_spec, o_spec),
        compiler_params=_cparams(("parallel", "parallel"), 8 * _nbytes((ATT_BLOCK, GROUP_WIDTH), F32)), name=name)(q, k, k, v, v)


def _att_combine(outs, lses, name):
    t = outs[0].shape[0]
    tm = _rows(t, 512)

    def body(*refs):
        o_refs, l_refs = refs[:N_GROUPS], refs[N_GROUPS:2 * N_GROUPS]
        ob_ref, of_ref, lse_ref = refs[2 * N_GROUPS:]
        ls = [r[...] for r in l_refs]
        m = functools.reduce(jnp.maximum, ls)
        ws = [jnp.exp(l - m) for l in ls]
        den = functools.reduce(jnp.add, ws)
        num = functools.reduce(jnp.add, [w * r[...] for w, r in zip(ws, o_refs)])
        o = num / den
        ob_ref[...] = o.astype(BF16)
        of_ref[...] = o
        lse_ref[...] = m + jnp.log(den)

    blk = pl.BlockSpec((tm, GROUP_WIDTH), lambda i: (i, 0))
    f32 = jax.ShapeDtypeStruct((t, GROUP_WIDTH), F32)
    return _pallas_call(
        body, out_shape=(jax.ShapeDtypeStruct((t, GROUP_WIDTH), BF16), f32, f32), grid=(t // tm,),
        in_specs=[blk] * (2 * N_GROUPS), out_specs=(blk, blk, blk),
        compiler_params=_cparams(("parallel",), 9 * _nbytes((tm, GROUP_WIDTH), F32)), name=name)(*outs, *lses)


def _att_delta(do, o, name):
    t = o.shape[0]
    tm = _rows(t, 512)

    def body(do_ref, o_ref, d_ref):
        for h in range(HEADS_PER_GROUP):
            sl = slice(h * HEAD_DIM, (h + 1) * HEAD_DIM)
            s = jnp.sum(do_ref[:, sl] * o_ref[:, sl], axis=1, keepdims=True)
            d_ref[:, sl] = jnp.broadcast_to(s, (tm, HEAD_DIM))

    blk = pl.BlockSpec((tm, GROUP_WIDTH), lambda i: (i, 0))
    return _pallas_call(
        body, out_shape=jax.ShapeDtypeStruct((t, GROUP_WIDTH), F32), grid=(t // tm,), in_specs=[blk, blk], out_specs=blk,
        compiler_params=_cparams(("parallel",), 3 * _nbytes((tm, GROUP_WIDTH), F32)), name=name)(do, o)


def _att_bwd_dq(q, k, v, do, lse, delta, offs, dil, name):
    qo, ko, vo = offs
    length = q.shape[0]
    nb = length // ATT_BLOCK
    scale = HEAD_DIM ** -0.5

    def body(q_ref, kp_ref, kc_ref, vp_ref, vc_ref, do_ref, lse_ref, dl_ref, dq_ref):
        mask = _query_mask(pl.program_id(1) > 0)
        heads = [slice(h * HEAD_DIM, (h + 1) * HEAD_DIM) for h in range(HEADS_PER_GROUP)]
        wide = lambda ref, sl: jnp.concatenate([ref[:, sl], ref[:, sl]], axis=1)
        ks = [jnp.concatenate([kp_ref[:, sl], kc_ref[:, sl]], axis=0) for sl in heads]
        vs = [jnp.concatenate([vp_ref[:, sl], vc_ref[:, sl]], axis=0) for sl in heads]
        ps = [jnp.exp(jnp.where(mask, _scores(q_ref[:, sl], kv), MASKED) - wide(lse_ref, sl)) for sl, kv in zip(heads, ks)]
        dps = [lax.dot_general(do_ref[:, sl].astype(BF16), vv, (((1,), (1,)), ((), ())), preferred_element_type=F32)
               for sl, vv in zip(heads, vs)]
        dss = [(p * (dp - wide(dl_ref, sl)) * scale).astype(BF16) for sl, p, dp in zip(heads, ps, dps)]
        dqs = [jnp.dot(ds, kv, preferred_element_type=F32) for ds, kv in zip(dss, ks)]
        for sl, dq in zip(heads, dqs):
            dq_ref[:, sl] = dq.astype(BF16)

    def spec(off, prev):
        if prev:
            return pl.BlockSpec((ATT_BLOCK, GROUP_WIDTH), lambda r, n: (jnp.maximum(n - 1, 0), off + r))
        return pl.BlockSpec((ATT_BLOCK, GROUP_WIDTH), lambda r, n: (n, off + r))

    own = pl.BlockSpec((ATT_BLOCK, GROUP_WIDTH), lambda r, n: (n, r))
    return _pallas_call(
        body, out_shape=jax.ShapeDtypeStruct((length, dil * GROUP_WIDTH), BF16), grid=(dil, nb),
        in_specs=[spec(qo, False), spec(ko, True), spec(ko, False), spec(vo, True), spec(vo, False), own, own, own],
        out_specs=own,
        compiler_params=_cparams(("parallel", "parallel"), 10 * _nbytes((ATT_BLOCK, GROUP_WIDTH), F32)),
        name=name)(q, k, k, v, v, do, lse, delta)


def _att_bwd_dkv(q, k, v, do, lse, delta, offs, dil, name):
    qo, ko, vo = offs
    length = q.shape[0]
    nb = length // ATT_BLOCK
    scale = HEAD_DIM ** -0.5

    def body(k_ref, v_ref, qc_ref, qn_ref, doc_ref, don_ref, lsec_ref, lsen_ref, dlc_ref, dln_ref, dk_ref, dv_ref):
        mask = _key_mask(pl.program_id(1) < nb - 1)
        heads = [slice(h * HEAD_DIM, (h + 1) * HEAD_DIM) for h in range(HEADS_PER_GROUP)]
        both = lambda cur, nxt, sl: jnp.concatenate([cur[:, sl], nxt[:, sl]], axis=0)
        qs = [both(qc_ref, qn_ref, sl) for sl in heads]
        dos = [both(doc_ref, don_ref, sl).astype(BF16) for sl in heads]
        ps = [jnp.exp(jnp.where(mask, _scores(qv, k_ref[:, sl]), MASKED) - both(lsec_ref, lsen_ref, sl)) for sl, qv in zip(heads, qs)]
        dps = [lax.dot_general(dov, v_ref[:, sl], (((1,), (1,)), ((), ())), preferred_element_type=F32) for sl, dov in zip(heads, dos)]
        dss = [(p * (dp - both(dlc_ref, dln_ref, sl)) * scale).astype(BF16) for sl, p, dp in zip(heads, ps, dps)]
        dvs = [lax.dot_general(p.astype(BF16), dov, (((0,), (0,)), ((), ())), preferred_element_type=F32) for p, dov in zip(ps, dos)]
        dks = [lax.dot_general(ds, qv, (((0,), (0,)), ((), ())), preferred_element_type=F32) for ds, qv in zip(dss, qs)]
        for sl, dk, dv in zip(heads, dks, dvs):
            dk_ref[:, sl] = dk.astype(BF16)
            dv_ref[:, sl] = dv.astype(BF16)

    def spec(off, nxt):
        if nxt:
            return pl.BlockSpec((ATT_BLOCK, GROUP_WIDTH), lambda r, n: (jnp.minimum(n + 1, nb - 1), off + r))
        return pl.BlockSpec((ATT_BLOCK, GROUP_WIDTH), lambda r, n: (n, off + r))

    own = pl.BlockSpec((ATT_BLOCK, GROUP_WIDTH), lambda r, n: (n, r))
    out = jax.ShapeDtypeStruct((length, dil * GROUP_WIDTH), BF16)
    return _pallas_call(
        body, out_shape=(out, out), grid=(dil, nb),
        in_specs=[spec(ko, False), spec(vo, False), spec(qo, False), spec(qo, True), spec(0, False), spec(0, True),
                  spec(0, False), spec(0, True), spec(0, False), spec(0, True)],
        out_specs=(own, own),
        compiler_params=_cparams(("parallel", "parallel"), 12 * _nbytes((ATT_BLOCK, GROUP_WIDTH), F32)),
        name=name)(k, v, q, q, do, do, lse, lse, delta, delta)


def _gelu(x):
    return 0.5 * x * (1.0 + lax.erf(x * (2.0 ** -0.5)))


def _gelu_grad(x):
    return 0.5 * (1.0 + lax.erf(x * (2.0 ** -0.5))) + x * jnp.exp(-0.5 * x * x) * ((2.0 * jnp.pi) ** -0.5)


def _sg_normed(vs, lg, lb):
    gv = _gelu(vs)
    mu = jnp.mean(gv, axis=1, keepdims=True)
    xc = gv - mu
    rstd = lax.rsqrt(jnp.mean(xc * xc, axis=1, keepdims=True) + LN_EPS)
    z = xc * rstd
    return z, rstd, z * lg + lb


def _sg_tril():
    row = lax.broadcasted_iota(jnp.int32, (SG_CHUNK, SG_CHUNK), 0)
    col = lax.broadcasted_iota(jnp.int32, (SG_CHUNK, SG_CHUNK), 1)
    return row >= col


def _sg_fwd(proj, u_blk, vs_blk, lg, lb, sg_w, bias, name):
    t = proj.shape[0]
    width = SG_GROUPS * SG_GROUP_DIM

    def body(u_ref, vs_ref, lg_ref, lb_ref, w_ref, bias_ref, o_ref):
        _, _, vn = _sg_normed(vs_ref[...].astype(F32), lg_ref[...], lb_ref[...])
        vn = vn.astype(BF16)
        tril = _sg_tril()
        for g in range(SG_GROUPS):
            sl = slice(g * SG_GROUP_DIM, (g + 1) * SG_GROUP_DIM)
            w = jnp.where(tril, w_ref[g], 0.0).astype(BF16)
            sp = jnp.dot(w, vn[:, sl], preferred_element_type=F32) + bias_ref[:, sl]
            o_ref[:, sl] = (_gelu(u_ref[:, sl].astype(F32)) * sp).astype(BF16)

    vec = pl.BlockSpec((1, width), lambda i: (0, 0))
    return _pallas_call(
        body, out_shape=jax.ShapeDtypeStruct((t, width), BF16), grid=(t // SG_CHUNK,),
        in_specs=[pl.BlockSpec((SG_CHUNK, width), lambda i: (i, u_blk)), pl.BlockSpec((SG_CHUNK, width), lambda i: (i, vs_blk)),
                  vec, vec, pl.BlockSpec((SG_GROUPS, SG_CHUNK, SG_CHUNK), lambda i: (0, 0, 0)),
                  pl.BlockSpec((SG_CHUNK, width), lambda i: (0, 0))],
        out_specs=pl.BlockSpec((SG_CHUNK, width), lambda i: (i, 0)),
        compiler_params=_cparams(("parallel",), 8 * _nbytes((SG_CHUNK, width), F32)), name=name)(proj, proj, lg, lb, sg_w, bias)


def _sg_bwd(proj, u_blk, vs_blk, dsu, lg, lb, sg_w, bias, name):
    t = proj.shape[0]
    width = SG_GROUPS * SG_GROUP_DIM

    def body(u_ref, vs_ref, dsu_ref, lg_ref, lb_ref, w_ref, bias_ref, du_ref, dvs_ref, dw_ref, dbias_ref, dlg_ref, dlb_ref):
        @pl.when(pl.program_id(0) == 0)
        def _():
            dw_ref[...] = jnp.zeros_like(dw_ref)
            dbias_ref[...] = jnp.zeros_like(dbias_ref)
            dlg_ref[...] = jnp.zeros_like(dlg_ref)
            dlb_ref[...] = jnp.zeros_like(dlb_ref)

        vs = vs_ref[...].astype(F32)
        z, rstd, vn = _sg_normed(vs, lg_ref[...], lb_ref[...])
        vn = vn.astype(BF16)
        tril = _sg_tril()
        dvn = []
        for g in range(SG_GROUPS):
            sl = slice(g * SG_GROUP_DIM, (g + 1) * SG_GROUP_DIM)
            w = jnp.where(tril, w_ref[g], 0.0).astype(BF16)
            vg = vn[:, sl]
            sp = jnp.dot(w, vg, preferred_element_type=F32) + bias_ref[:, sl]
            uv = u_ref[:, sl].astype(F32)
            dsu_g = dsu_ref[:, sl].astype(F32)
            du_ref[:, sl] = (dsu_g * sp * _gelu_grad(uv)).astype(BF16)
            dsp = dsu_g * _gelu(uv)
            dsp_b = dsp.astype(BF16)
            dw = lax.dot_general(dsp_b, vg, (((1,), (1,)), ((), ())), preferred_element_type=F32)
            dw_ref[g] += jnp.where(tril, dw, 0.0)
            dbias_ref[:, sl] += jnp.broadcast_to(jnp.sum(dsp, axis=1, keepdims=True), (SG_CHUNK, SG_GROUP_DIM))
            dvn.append(lax.dot_general(w, dsp_b, (((0,), (0,)), ((), ())), preferred_element_type=F32))
        dvn = jnp.concatenate(dvn, axis=1)
        dlg_ref[...] += jnp.sum(dvn * z, axis=0, keepdims=True)
        dlb_ref[...] += jnp.sum(dvn, axis=0, keepdims=True)
        dz = dvn * lg_ref[...]
        dgv = rstd * (dz - jnp.mean(dz, axis=1, keepdims=True) - z * jnp.mean(dz * z, axis=1, keepdims=True))
        dvs_ref[...] = (dgv * _gelu_grad(vs)).astype(BF16)

    vec = pl.BlockSpec((1, width), lambda i: (0, 0))
    row = pl.BlockSpec((SG_CHUNK, width), lambda i: (i, 0))
    fixed = pl.BlockSpec((SG_CHUNK, width), lambda i: (0, 0))
    w_spec = pl.BlockSpec((SG_GROUPS, SG_CHUNK, SG_CHUNK), lambda i: (0, 0, 0))
    act = jax.ShapeDtypeStruct((t, width), BF16)
    return _pallas_call(
        body,
        out_shape=(act, act, jax.ShapeDtypeStruct((SG_GROUPS, SG_CHUNK, SG_CHUNK), F32),
                   jax.ShapeDtypeStruct((SG_CHUNK, width), F32), jax.ShapeDtypeStruct((1, width), F32),
                   jax.ShapeDtypeStruct((1, width), F32)),
        grid=(t // SG_CHUNK,),
        in_specs=[pl.BlockSpec((SG_CHUNK, width), lambda i: (i, u_blk)), pl.BlockSpec((SG_CHUNK, width), lambda i: (i, vs_blk)),
                  row, vec, vec, w_spec, fixed],
        out_specs=(row, row, w_spec, fixed, vec, vec),
        compiler_params=_cparams(("arbitrary",), 14 * _nbytes((SG_CHUNK, width), F32)),
        name=name)(proj, proj, dsu, lg, lb, sg_w, bias)


def _gate_fwd(proj, ga_blk, gs_blk, y_att, y_sg, name):
    t, d = y_att.shape
    tm, tn = _rows(t, 512), _tile(d, GROUP_WIDTH)

    def body(ga_ref, gs_ref, ya_ref, ys_ref, o_ref):
        o_ref[...] = (jax.nn.sigmoid(ga_ref[...].astype(F32)) * ya_ref[...].astype(F32)
                      + jax.nn.sigmoid(gs_ref[...].astype(F32)) * ys_ref[...].astype(F32)).astype(BF16)

    own = pl.BlockSpec((tm, tn), lambda i, j: (i, j))
    return _pallas_call(
        body, out_shape=jax.ShapeDtypeStruct((t, d), BF16), grid=(t // tm, d // tn),
        in_specs=[pl.BlockSpec((tm, tn), lambda i, j: (i, ga_blk + j)), pl.BlockSpec((tm, tn), lambda i, j: (i, gs_blk + j)),
                  own, own],
        out_specs=own, compiler_params=_cparams(("parallel", "parallel"), 6 * _nbytes((tm, tn), F32)),
        name=name)(proj, proj, y_att, y_sg)


def _gate_bwd(proj, ga_blk, gs_blk, y_att, y_sg, dmerged, name):
    t, d = y_att.shape
    tm, tn = _rows(t, 512), _tile(d, GROUP_WIDTH)

    def body(ga_ref, gs_ref, ya_ref, ys_ref, dm_ref, dya_ref, dys_ref, dga_ref, dgs_ref):
        dm = dm_ref[...].astype(F32)
        for g_ref, y_ref, dy_ref, dg_ref in ((ga_ref, ya_ref, dya_ref, dga_ref), (gs_ref, ys_ref, dys_ref, dgs_ref)):
            sg = jax.nn.sigmoid(g_ref[...].astype(F32))
            dy_ref[...] = (dm * sg).astype(BF16)
            dg_ref[...] = (dm * y_ref[...].astype(F32) * sg * (1.0 - sg)).astype(BF16)

    own = pl.BlockSpec((tm, tn), lambda i, j: (i, j))
    out = jax.ShapeDtypeStruct((t, d), BF16)
    return _pallas_call(
        body, out_shape=(out, out, out, out), grid=(t // tm, d // tn),
        in_specs=[pl.BlockSpec((tm, tn), lambda i, j: (i, ga_blk + j)), pl.BlockSpec((tm, tn), lambda i, j: (i, gs_blk + j)),
                  own, own, own],
        out_specs=(own, own, own, own), compiler_params=_cparams(("parallel", "parallel"), 10 * _nbytes((tm, tn), F32)),
        name=name)(proj, proj, y_att, y_sg, dmerged)


def _group_view(arr, col, dil):
    t = arr.shape[0]
    return arr[:, col:col + GROUP_WIDTH].reshape(t // dil, dil * GROUP_WIDTH)


def _mixer_forward(x, wb, small, in_specs, sg_specs, out_specs):
    t, d = x.shape
    att_w = N_GROUPS * GROUP_WIDTH
    sg_w = SG_GROUPS * SG_GROUP_DIM
    n = _rmsnorm_fwd(x, small['mix_norm'], "mix_norm")
    rider, names = _gather(wb, in_specs)
    proj, got = _matmul([(n, _full(wb, 'w_in'))], 'nn', BF16, "mix_in", b3=True, caps=(1024, 1024, 1024), rider=rider)
    _landed(wb, names, got)
    tables = _rope_tables(t)
    qk = _rope(proj, 2 * att_w, tables, "mix_rope")
    outs, lses = [], []
    for gi, dil in enumerate(DILATIONS):
        if dil == 1:
            args = (qk, qk, proj, (gi, N_GROUPS + gi, 2 * N_GROUPS + gi))
        else:
            args = (_group_view(qk, gi * GROUP_WIDTH, dil), _group_view(qk, att_w + gi * GROUP_WIDTH, dil),
                    _group_view(proj, 2 * att_w + gi * GROUP_WIDTH, dil), (0, 0, 0))
        o, lse = _att_fwd(*args, dil, f"att_fwd{gi}")
        outs.append(o.reshape(t, GROUP_WIDTH))
        lses.append(lse.reshape(t, GROUP_WIDTH))
    o_b, o_f, lse = _att_combine(outs, lses, "att_combine")
    y_att = _matmul([(o_b, _full(wb, 'w_att_out'))], 'nn', BF16, "mix_att_out", b3=True)
    bias = jnp.repeat(small['sg_b'].T, SG_GROUP_DIM, axis=1)
    u_blk, vs_blk = 3 * att_w // sg_w, 3 * att_w // sg_w + 1
    su = _sg_fwd(proj, u_blk, vs_blk, small['sg_ln_g'], small['sg_ln_b'], small['sg_w'], bias, "sg_fwd")
    rider, names = _gather(wb, sg_specs)
    y_sg, got = _matmul([(su, _full(wb, 'w_sg_out'))], 'nn', BF16, "mix_sg_out", b3=True, rider=rider)
    _landed(wb, names, got)
    ga_blk = (3 * att_w + 2 * sg_w) // _tile(d, GROUP_WIDTH)
    gs_blk = ga_blk + d // _tile(d, GROUP_WIDTH)
    merged = _gate_fwd(proj, ga_blk, gs_blk, y_att, y_sg, "gate_fwd")
    rider, names = _gather(wb, out_specs)
    x_next, got = _matmul([(merged, _full(wb, 'w_out'))], 'nn', F32, "mix_out", residual=x, rider=rider)
    _landed(wb, names, got)
    saved =(n, proj, qk, tables, o_b, o_f, lse, y_att, su, y_sg, merged, bias, (u_blk, vs_blk, ga_blk, gs_blk))
    return x_next, saved


def _mixer_backward(x, wb, small, saved, dx_next, dx_next_b, c_idx):
    n, proj, qk, tables, o_b, o_f, lse, y_att, su, y_sg, merged, bias, (u_blk, vs_blk, ga_blk, gs_blk) = saved
    t, d = x.shape
    att_w = N_GROUPS * GROUP_WIDTH
    s = N_CHIPS
    dmerged = _matmul([(dx_next_b, _full(wb, 'w_out'))], 'nt', BF16, "mix_out_dx")
    g_w_out = _matmul([(merged, dx_next_b)], 'tn', BF16, "mix_out_dw", caps=(1024, 1024, 1024))
    dy_att, dy_sg, dg_att, dg_sg = _gate_bwd(proj, ga_blk, gs_blk, y_att, y_sg, dmerged, "gate_bwd")

    g_w_att_out = _matmul([(o_b, dy_att)], 'tn', BF16, "mix_att_out_dw", out3=s)
    do = _matmul([(dy_att, _full(wb, 'w_att_out'))], 'nt', F32, "mix_att_out_dx", b3=True)
    delta = _att_delta(do, o_f, "att_delta")
    dqs, dks, dvs_ = [], [], []
    for gi, dil in enumerate(DILATIONS):
        if dil == 1:
            args = (qk, qk, proj, do, lse, delta, (gi, N_GROUPS + gi, 2 * N_GROUPS + gi))
        else:
            args = (_group_view(qk, gi * GROUP_WIDTH, dil), _group_view(qk, att_w + gi * GROUP_WIDTH, dil),
                    _group_view(proj, 2 * att_w + gi * GROUP_WIDTH, dil), _group_view(do, 0, dil),
                    _group_view(lse, 0, dil), _group_view(delta, 0, dil), (0, 0, 0))
        dq = _att_bwd_dq(*args, dil, f"att_bwd_dq{gi}")
        dk, dv = _att_bwd_dkv(*args, dil, f"att_bwd_dkv{gi}")
        dqs.append(dq.reshape(t, GROUP_WIDTH))
        dks.append(dk.reshape(t, GROUP_WIDTH))
        dvs_.append(dv.reshape(t, GROUP_WIDTH))
    c, s_up, s_dn = tables
    dqk = _rope(jnp.concatenate(dqs + dks, axis=1), 2 * att_w, (c, -s_up, -s_dn), "mix_rope_bwd")

    g_w_sg_out = _matmul([(su, dy_sg)], 'tn', BF16, "mix_sg_out_dw", out3=s)
    out_names = ['w_out', 'w_att_out', 'w_sg_out']
    out_parts = _reduce_first([g_w_out, g_w_att_out, g_w_sg_out], out_names, wb, c_idx)
    dsu = _matmul([(dy_sg, _full(wb, 'w_sg_out'))], 'nt', BF16, "mix_sg_out_dx", b3=True)
    du, dvs, g_sg_w, g_bias, g_lg, g_lb = _sg_bwd(proj, u_blk, vs_blk, dsu, small['sg_ln_g'], small['sg_ln_b'],
                                                   small['sg_w'], bias, "sg_bwd")
    gs = {'sg_w': g_sg_w, 'sg_b': g_bias[:, ::SG_GROUP_DIM].T, 'sg_ln_g': g_lg, 'sg_ln_b': g_lb}

    dproj = jnp.concatenate([dqk] + dvs_ + [du, dvs, dg_att, dg_sg], axis=1)
    g_w_in, out_recv = _matmul([(n, dproj)], 'tn', BF16, "mix_in_dw", out3=s, caps=(1024, 1024, 1024),
                               rider=_scatter_rider(out_parts))
    (p_w_in,) = _reduce_first([g_w_in], ['w_in'], wb, c_idx)
    dn, (r_w_in,) = _matmul([(dproj, _full(wb, 'w_in'))], 'nt', F32, "mix_in_dx", b3=True, caps=(1024, 1024, 512),
                            rider=_scatter_rider([p_w_in]))
    dx, dx_b, gs['mix_norm'] = _rmsnorm_bwd(x, small['mix_norm'], dn, dx_next, "mix_norm_bwd")
    g = {nm: (p, r) for nm, p, r in zip(out_names, out_parts, out_recv)}
    g['w_in'] = (p_w_in, r_w_in)
    return dx, dx_b, g, gs


def _step(x, target, wb, small, c_idx):
    wb = dict(wb)
    rider, names = _gather(wb, ['ffn1_w_gate', 'ffn1_w_up'])
    _landed(wb, names, _exchange(rider, "gather_first"))
    half_in = wb['w_in'].shape[2] // 2
    x1, s1 = _ffn_forward(x, small['ffn1_norm'], wb, "ffn1", ['ffn1_w_down', ('w_in', 0, half_in)], [('w_in', half_in, 2 * half_in)])
    up_rows = wb['ffn2_w_up'].shape[2]
    up_cut = up_rows // 32 * 15
    x2, s2 = _mixer_forward(x1, wb, small, ['w_att_out', 'w_sg_out', 'w_out', 'ffn2_w_gate'],
                            [('ffn2_w_up', 0, up_cut)], [('ffn2_w_up', up_cut, up_rows)])
    x3, s3 = _ffn_forward(x2, small['ffn2_norm'], wb, "ffn2", ['ffn2_w_down'], None)
    loss, dx3, dx3_b, g_final = _final_loss(x3, small['final_norm'], target, "final_loss")
    gs = {'final_norm': g_final}
    dx2, dx2_b, gs['ffn2_norm'], g = _ffn_backward(x2, small['ffn2_norm'], wb, s3, dx3, dx3_b, c_idx, "ffn2")
    dx1, dx1_b, g_mix, gs_mix = _mixer_backward(x1, wb, small, s2, dx2, dx2_b, c_idx)
    g.update(g_mix)
    gs.update(gs_mix)
    dx0, _, gs['ffn1_norm'], g_ffn1 = _ffn_backward(x, small['ffn1_norm'], wb, s1, dx1, dx1_b, c_idx, "ffn1")
    g.update(g_ffn1)
    return loss, dx0, g, gs


def _cast_into_gathered(wt, p_idx, name):
    r, ccols = wt.shape[0] // 2, wt.shape[1]
    tm = _rows(r, 256)
    nb = r // tm

    def body(p_ref, w_ref, o_ref):
        o_ref[...] = w_ref[...].astype(BF16)

    grid_spec = pltpu.PrefetchScalarGridSpec(
        num_scalar_prefetch=1, grid=(2, nb),
        in_specs=[pl.BlockSpec((tm, ccols), lambda h, i, pr: (h * nb + i, 0))],
        out_specs=pl.BlockSpec((None, None, tm, ccols), lambda h, i, pr: (pr[0], h, i, 0)))
    return _pallas_call(body, out_shape=jax.ShapeDtypeStruct((N_CHIPS, 2, r, ccols), BF16), grid_spec=grid_spec,
                          compiler_params=_cparams(("parallel", "parallel"), 2 * _nbytes((tm, ccols), F32)), name=name)(p_idx, wt)


def _sibling_exchange(grads, name):
    nw = len(grads)

    def body(*refs):
        src, dst = refs[:nw], refs[nw:2 * nw]
        send_sems, recv_sems = refs[2 * nw:]
        x, y, c, _ = _place()
        cps = []
        for i in range(nw):
            cp = pltpu.make_async_remote_copy(src[i].at[:, 1 - c], dst[i], send_sems.at[i], recv_sems.at[i],
                                              device_id=(x, y, 1 - c), device_id_type=MESH)
            cp.start()
            cps.append(cp)
        for cp in cps:
            cp.wait()

    any_spec = pl.BlockSpec(memory_space=pl.ANY)
    return _pallas_call(
        body, out_shape=[jax.ShapeDtypeStruct((g.shape[0],) + g.shape[2:], g.dtype) for g in grads],
        in_specs=[any_spec] * nw, out_specs=[any_spec] * nw,
        scratch_shapes=[pltpu.SemaphoreType.DMA((nw,)), pltpu.SemaphoreType.DMA((nw,))],
        compiler_params=pltpu.CompilerParams(has_side_effects=True), name=name)(*grads)


def _half_exchange(bufs):
    nw = len(bufs)

    def body(*refs):
        dst = refs[nw:2 * nw]
        send_sems, recv_sems = refs[2 * nw:]
        x, y, c, _ = _place()
        cps = []
        for i in range(nw):
            mine = dst[i].at[c]
            cp = pltpu.make_async_remote_copy(mine, mine, send_sems.at[i], recv_sems.at[i],
                                              device_id=(x, y, 1 - c), device_id_type=MESH)
            cp.start()
            cps.append(cp)
        for i, cp in enumerate(cps):
            cp.wait_send()
            theirs = dst[i].at[1 - c]
            pltpu.make_async_remote_copy(theirs, theirs, send_sems.at[i], recv_sems.at[i],
                                         device_id=(x, y, 1 - c), device_id_type=MESH).wait_recv()

    any_spec = pl.BlockSpec(memory_space=pl.ANY)
    return _pallas_call(
        body, out_shape=[jax.ShapeDtypeStruct(b.shape, b.dtype) for b in bufs],
        in_specs=[any_spec] * nw, out_specs=[any_spec] * nw, input_output_aliases={i: i for i in range(nw)},
        scratch_shapes=[pltpu.SemaphoreType.DMA((nw,)), pltpu.SemaphoreType.DMA((nw,))],
        compiler_params=pltpu.CompilerParams(has_side_effects=True), name="rs_halves")(*bufs)


def _sibling_sum(grad, recv, c_idx, name):
    s, _, r, ccols = grad.shape
    tm = _rows(r, 256)

    def body(c_ref, g_ref, r_ref, o_ref):
        o_ref[...] = (g_ref[...].astype(F32) + r_ref[...].astype(F32)).astype(BF16)

    grid_spec = pltpu.PrefetchScalarGridSpec(
        num_scalar_prefetch=1, grid=(s, r // tm),
        in_specs=[pl.BlockSpec((None, None, tm, ccols), lambda q, i, cr: (q, cr[0], i, 0)),
                  pl.BlockSpec((None, tm, ccols), lambda q, i, cr: (q, i, 0))],
        out_specs=pl.BlockSpec((None, tm, ccols), lambda q, i, cr: (q, i, 0)))
    return _pallas_call(body, out_shape=jax.ShapeDtypeStruct((s, r, ccols), BF16), grid_spec=grid_spec,
                          compiler_params=_cparams(("parallel", "parallel"), 4 * _nbytes((tm, ccols), F32)), name=name)(c_idx, grad, recv)


def _chip_sum(part, recv, pc_idx, name):
    _, r, ccols = part.shape
    tm = _rows(r, 256)

    def body(pc_ref, own_ref, r0_ref, r1_ref, r2_ref, o_ref):
        acc = own_ref[...].astype(F32) + r0_ref[...].astype(F32)
        acc = acc + r1_ref[...].astype(F32)
        o_ref[...] = acc + r2_ref[...].astype(F32)

    def slot(j):
        return pl.BlockSpec((None, tm, ccols), lambda i, pc: (j, i, 0))

    grid_spec = pltpu.PrefetchScalarGridSpec(
        num_scalar_prefetch=1, grid=(r // tm,),
        in_specs=[pl.BlockSpec((None, tm, ccols), lambda i, pc: (pc[0], i, 0)), slot(0), slot(1), slot(2)],
        out_specs=pl.BlockSpec((None, tm, ccols), lambda i, pc: (pc[1], i, 0)))
    return _pallas_call(body, out_shape=jax.ShapeDtypeStruct((2, r, ccols), F32), grid_spec=grid_spec,
                          compiler_params=_cparams(("parallel",), 6 * _nbytes((tm, ccols), F32)), name=name)(pc_idx, part, recv, recv, recv)


def _all_reduce_small(vec):
    _, r, _ = vec.shape

    def body(v_ref, o_ref, parts, send1, recv1, send2, recv2):
        x, y, c, _ = _place()
        me = 4 * x + 2 * y + c
        peers = []
        for k in range(1, N_DEV):
            px, py, pc = (1 - x if k & 4 else x, 1 - y if k & 2 else y, 1 - c if k & 1 else c)
            peers.append(((px, py, pc), 4 * px + 2 * py + pc))
        parts[me] = v_ref[me]
        cps = []
        for k, (peer, peer_id) in enumerate(peers):
            cp = pltpu.make_async_remote_copy(v_ref.at[peer_id], parts.at[me], send1.at[k], recv1.at[k],
                                              device_id=peer, device_id_type=MESH)
            cp.start()
            cps.append(cp)
        for cp in cps:
            cp.wait()
        acc = parts[0]
        for dev in range(1, N_DEV):
            acc = acc + parts[dev]
        o_ref[me] = acc
        cps = []
        for k, (peer, _) in enumerate(peers):
            cp = pltpu.make_async_remote_copy(o_ref.at[me], o_ref.at[me], send2.at[k], recv2.at[k],
                                              device_id=peer, device_id_type=MESH)
            cp.start()
            cps.append(cp)
        for cp in cps:
            cp.wait()

    vm = pl.BlockSpec(memory_space=pltpu.VMEM)
    sems = pltpu.SemaphoreType.DMA((N_DEV - 1,))
    return pl.pallas_call(
        body, out_shape=jax.ShapeDtypeStruct(vec.shape, F32), in_specs=[vm], out_specs=vm,
        scratch_shapes=[pltpu.VMEM((N_DEV, r, LANES), F32), sems, sems, sems, sems],
        compiler_params=pltpu.CompilerParams(vmem_limit_bytes=int(8 * _nbytes((N_DEV, r, LANES), F32))),
        name="all_reduce_small")(vec)


def _adamw(wt, g, m, v, name):
    r, ccols = wt.shape
    tm = _rows(r, max(8, (MIB // (4 * ccols)) // 8 * 8))
    blk = pl.BlockSpec((tm, ccols), lambda i: (i, 0))

    def body(w_ref, g_ref, m_ref, v_ref, go_ref, d_ref, mo_ref, vo_ref):
        gv = g_ref[...]
        go_ref[...] = gv
        mv = ADAM_B1 * m_ref[...] + (1.0 - ADAM_B1) * gv
        vv = ADAM_B2 * v_ref[...] + (1.0 - ADAM_B2) * (gv * gv)
        m_hat = mv / (1.0 - ADAM_B1 ** ADAM_STEP)
        v_hat = vv / (1.0 - ADAM_B2 ** ADAM_STEP)
        d_ref[...] = -ADAM_LR * (m_hat / (jnp.sqrt(v_hat) + ADAM_EPS) + ADAM_WD * w_ref[...])
        mo_ref[...] = mv
        vo_ref[...] = vv

    out = jax.ShapeDtypeStruct((r, ccols), F32)
    return _pallas_call(body, out_shape=(out, out, out, out), grid=(r // tm,), in_specs=[blk] * 4, out_specs=(blk,) * 4,
                          compiler_params=_cparams(("parallel",), 8 * _nbytes((tm, ccols), F32)), name=name)(wt, g, m, v)


def _as_rows(a):
    rows = a.reshape(-1, LANES)
    return jnp.pad(rows, ((0, -rows.shape[0] % 8), (0, 0)))


def kernel(x, ffn1_norm, ffn1_w_gate, ffn1_w_up, ffn1_w_down, mix_norm, w_in, sg_ln_g, sg_ln_b, sg_w, sg_b, w_att_out, w_sg_out, w_out, ffn2_norm, ffn2_w_gate, ffn2_w_up, ffn2_w_down, final_norm, loss_target, m_ffn1_norm, m_ffn1_w_gate, m_ffn1_w_up, m_ffn1_w_down, m_mix_norm, m_w_in, m_sg_ln_g, m_sg_ln_b, m_sg_w, m_sg_b, m_w_att_out, m_w_sg_out, m_w_out, m_ffn2_norm, m_ffn2_w_gate, m_ffn2_w_up, m_ffn2_w_down, m_final_norm, v_ffn1_norm, v_ffn1_w_gate, v_ffn1_w_up, v_ffn1_w_down, v_mix_norm, v_w_in, v_sg_ln_g, v_sg_ln_b, v_sg_w, v_sg_b, v_w_att_out, v_w_sg_out, v_w_out, v_ffn2_norm, v_ffn2_w_gate, v_ffn2_w_up, v_ffn2_w_down, v_final_norm):
    given = dict(locals())
    wts = {n: given[n] for n in WEIGHT_NAMES}
    ms = {n: given["m_" + n] for n in WEIGHT_NAMES}
    vs = {n: given["v_" + n] for n in WEIGHT_NAMES}
    t, d = x.shape[-2], x.shape[-1]
    xc, yc, cc = lax.axis_index("x"), lax.axis_index("y"), lax.axis_index("c")

    shard2d = {n: wts[n].reshape(wts[n].shape[-2:]) for n in BIG_NAMES}
    p_idx = jnp.reshape(2 * xc + yc, (1,)).astype(jnp.int32)
    c_idx = jnp.reshape(cc, (1,)).astype(jnp.int32)
    pc_idx = jnp.stack([2 * xc + yc, cc]).astype(jnp.int32)
    wb = {n: _cast_into_gathered(shard2d[n], p_idx, f"cast_{n}") for n in BIG_NAMES}

    small = {n: wts[n].reshape(-1, wts[n].shape[-1]) for n in SMALL_NAMES}
    small['sg_w'] = wts['sg_w'].reshape(wts['sg_w'].shape[-3:])
    loss, dx, g, gs = _step(x.reshape(t, d), loss_target.reshape(t, d), wb, small, c_idx)
    loss = lax.psum(loss[0, 0], ("x", "y", "c"))

    my_halves = [_chip_sum(*g[n], pc_idx, f"rs_sum2_{n}") for n in BIG_NAMES]
    reduced = _half_exchange(my_halves)
    grads = {n: r.reshape(shard2d[n].shape) for n, r in zip(BIG_NAMES, reduced)}

    def pack(tree):
        rows = jnp.concatenate([_as_rows(tree[n]) for n in SMALL_NAMES], axis=0)
        return jnp.pad(rows, ((0, -rows.shape[0] % (8 * N_DEV)), (0, 0)))

    packed = pack(gs)
    packed = _all_reduce_small(packed.reshape(N_DEV, -1, LANES)).reshape(packed.shape)

    delta, new_m, new_v = {}, {}, {}
    for n in BIG_NAMES:
        shape = wts[n].shape
        out = _adamw(shard2d[n], grads[n], ms[n].reshape(shard2d[n].shape), vs[n].reshape(shard2d[n].shape), f"adamw_{n}")
        grads[n], delta[n], new_m[n], new_v[n] = (a.reshape(shape) for a in out)

    small_out = _adamw(pack(wts), packed, pack(ms), pack(vs), "adamw_small")
    row = 0
    for n in SMALL_NAMES:
        shape = wts[n].shape
        sz = wts[n].size // LANES
        grads[n], delta[n], new_m[n], new_v[n] = (a[row:row + sz].reshape(shape) for a in small_out)
        row += sz + -sz % 8

    return (loss, dx.reshape(x.shape), *[grads[n] for n in WEIGHT_NAMES], *[delta[n] for n in WEIGHT_NAMES],
            *[new_m[n] for n in WEIGHT_NAMES], *[new_v[n] for n in WEIGHT_NAMES])
```

```python
import functools

import jax
import jax.numpy as jnp
from jax import lax
from jax.experimental import pallas as pl
from jax.experimental.pallas import tpu as pltpu

F32 = jnp.float32
BF16 = jnp.bfloat16
MESH = pl.DeviceIdType.MESH

NORM_EPS = 1e-6
LN_EPS = 1e-5
HEAD_DIM = 128
HEADS_PER_GROUP = 4
GROUP_WIDTH = HEADS_PER_GROUP * HEAD_DIM
DILATIONS = (1, 4, 16)
N_GROUPS = len(DILATIONS)
ATT_BLOCK = 128
ROPE_DIM = HEAD_DIM // 4
ROPE_THETA = 500000.0
SG_CHUNK = 128
SG_GROUPS = 12
SG_GROUP_DIM = 128
MASKED = -1e30

ADAM_LR = 0.001
ADAM_B1 = 0.9
ADAM_B2 = 0.999
ADAM_EPS = 1e-08
ADAM_WD = 0.01
ADAM_STEP = 10

N_CHIPS = 4
N_DEV = 8
LANES = 128
MIB = 2 ** 20
VMEM_BYTES_V7X = 64 * MIB

WEIGHT_NAMES = ['ffn1_norm', 'ffn1_w_gate', 'ffn1_w_up', 'ffn1_w_down', 'mix_norm', 'w_in', 'sg_ln_g', 'sg_ln_b',
                'sg_w', 'sg_b', 'w_att_out', 'w_sg_out', 'w_out', 'ffn2_norm', 'ffn2_w_gate', 'ffn2_w_up',
                'ffn2_w_down', 'final_norm']
BIG = [('ffn1_w_gate', 1), ('ffn1_w_up', 1), ('ffn1_w_down', 0), ('w_in', 1), ('w_att_out', 1), ('w_sg_out', 1),
       ('w_out', 0), ('ffn2_w_gate', 1), ('ffn2_w_up', 1), ('ffn2_w_down', 0)]
BIG_NAMES = [n for n, _ in BIG]
SMALL_NAMES = [n for n in WEIGHT_NAMES if n not in BIG_NAMES]


def _nbytes(shape, dtype):
    n = jnp.dtype(dtype).itemsize
    for s in shape:
        if s is not None:
            n *= s
    return n


def _pallas_call(*args, **kw):
    kw['out_shape'] = jax.tree.map(lambda s: pltpu.HBM(s.shape, s.dtype), kw['out_shape'])
    call = pl.pallas_call(*args, **kw)

    def pinned(*operands):
        return call(*[o if jnp.issubdtype(o.dtype, jnp.integer) else pltpu.with_memory_space_constraint(o, pltpu.HBM)
                      for o in operands])

    return pinned


def _cparams(sem, block_bytes, **kw):
    limit = int(min(max(3 * block_bytes, 32 * MIB), VMEM_BYTES_V7X - 8 * MIB))
    return pltpu.CompilerParams(dimension_semantics=sem, vmem_limit_bytes=limit, **kw)


def _tile(dim, cap):
    best = None
    for t in range(LANES, min(dim, cap) + 1, LANES):
        if dim % t == 0:
            best = t
    if best is None:
        assert dim <= cap, (dim, cap)
        return dim
    return best


def _rows(dim, cap):
    best = None
    for t in range(8, min(dim, cap) + 1, 8):
        if dim % t == 0:
            best = t
    assert best is not None, (dim, cap)
    return best


def _place():
    x, y, c = lax.axis_index("x"), lax.axis_index("y"), lax.axis_index("c")
    others = [(1 - x, y), (x, 1 - y), (1 - x, 1 - y)]
    return x, y, c, others


class _Rider:
    def __init__(self, operands, out_shapes, aliases, sems, start, finish):
        self.operands = operands
        self.out_shapes = out_shapes
        self.aliases = aliases
        self.sems = sems
        self.start = start
        self.finish = finish


def _run(body, *, name, grid, in_specs, out_specs, out_shape, scratch_shapes, operands, block_bytes, rider=None):
    if rider is None:
        sem = ("parallel",) * (len(grid) - 1) + ("arbitrary",)
        return _pallas_call(body, out_shape=out_shape, grid=grid, in_specs=in_specs, out_specs=out_specs,
                              scratch_shapes=scratch_shapes, compiler_params=_cparams(sem, block_bytes), name=name)(*operands)
    n_in, n_out, n_scr = len(operands), len(out_shape), len(scratch_shapes)
    r_in, r_out = len(rider.operands), len(rider.out_shapes)
    any_spec = pl.BlockSpec(memory_space=pl.ANY)

    def wrapped(*refs):
        ins, refs = refs[:n_in], refs[n_in:]
        r_ins, refs = refs[:r_in], refs[r_in:]
        outs, refs = refs[:n_out], refs[n_out:]
        r_outs, refs = refs[:r_out], refs[r_out:]
        scr, sems = refs[:n_scr], refs[n_scr:]
        if not grid:
            rider.start(r_ins, r_outs, sems)
            rider.finish(r_ins, r_outs, sems)
            return
        ids = [pl.program_id(a) for a in range(len(grid))]
        first = functools.reduce(jnp.logical_and, [i == 0 for i in ids])
        last = functools.reduce(jnp.logical_and, [i == g - 1 for i, g in zip(ids, grid)])

        @pl.when(first)
        def _():
            rider.start(r_ins, r_outs, sems)

        body(*ins, *outs, *scr)

        @pl.when(last)
        def _():
            rider.finish(r_ins, r_outs, sems)

    results = _pallas_call(
        wrapped, out_shape=list(out_shape) + list(rider.out_shapes), grid=grid,
        in_specs=list(in_specs) + [any_spec] * r_in, out_specs=list(out_specs) + [any_spec] * r_out,
        scratch_shapes=list(scratch_shapes) + list(rider.sems),
        input_output_aliases={n_in + k: n_out + v for k, v in rider.aliases.items()},
        compiler_params=_cparams(("arbitrary",) * len(grid) if grid else None, block_bytes, has_side_effects=True),
        name=name)(*operands, *rider.operands)
    return results[:n_out], results[n_out:]


def _exchange(rider, name):
    return _run(None, name=name, grid=(), in_specs=[], out_specs=[], out_shape=[], scratch_shapes=[], operands=[],
                block_bytes=0, rider=rider)[1]


def _gather_rider(items):
    bufs, index = [], []
    for b, r0, r1 in items:
        if not any(b is q for q in bufs):
            bufs.append(b)
        index.append(([k for k, q in enumerate(bufs) if q is b][0], r0, r1))
    n = len(index)

    def piece(refs, k, chip, half):
        bi, r0, r1 = index[k]
        return refs[bi].at[chip, half, pl.ds(r0, r1 - r0)]

    def copy(ref, sem_pair, k, j, to):
        return pltpu.make_async_remote_copy(ref, ref, sem_pair[0].at[k, j], sem_pair[1].at[k, j], device_id=to, device_id_type=MESH)

    def start(r_ins, buf, sems):
        x, y, c, others = _place()
        for k in range(n):
            for j, (ox, oy) in enumerate(others):
                copy(piece(buf, k, 2 * x + y, c), sems[:2], k, j, (ox, oy, c)).start()

    def finish(r_ins, buf, sems):
        x, y, c, others = _place()
        for k in range(n):
            for j, (ox, oy) in enumerate(others):
                got = piece(buf, k, 2 * ox + oy, c)
                copy(got, sems[:2], k, j, (ox, oy, c)).wait_recv()
                copy(got, sems[2:], k, j, (x, y, 1 - c)).start()
        for k in range(n):
            for j, (ox, oy) in enumerate(others):
                copy(piece(buf, k, 2 * ox + oy, 1 - c), sems[2:], k, j, (x, y, 1 - c)).wait_recv()
        for k in range(n):
            for j, (ox, oy) in enumerate(others):
                copy(piece(buf, k, 2 * x + y, c), sems[:2], k, j, (ox, oy, c)).wait_send()
                copy(piece(buf, k, 2 * ox + oy, c), sems[2:], k, j, (x, y, 1 - c)).wait_send()

    return _Rider(bufs, [jax.ShapeDtypeStruct(b.shape, b.dtype) for b in bufs], {i: i for i in range(len(bufs))},
                  [pltpu.SemaphoreType.DMA((n, 3))] * 4, start, finish)


def _scatter_rider(parts):
    n = len(parts)

    def copy(src, dst, sems, i, j, to):
        return pltpu.make_async_remote_copy(src, dst, sems[0].at[i, j], sems[1].at[i, j], device_id=to, device_id_type=MESH)

    def start(src, dst, sems):
        x, y, c, others = _place()
        for i in range(n):
            for j, (ox, oy) in enumerate(others):
                copy(src[i].at[2 * ox + oy], dst[i].at[j], sems, i, j, (ox, oy, c)).start()

    def finish(src, dst, sems):
        x, y, c, others = _place()
        for i in range(n):
            for j, (ox, oy) in enumerate(others):
                copy(src[i].at[2 * ox + oy], dst[i].at[j], sems, i, j, (ox, oy, c)).wait()

    return _Rider(parts, [jax.ShapeDtypeStruct((3,) + p.shape[1:], p.dtype) for p in parts], {},
                  [pltpu.SemaphoreType.DMA((n, 3))] * 2, start, finish)


def _matmul(pairs, mode, out_dtype, name, *, scale=1.0, residual=None, b3=False, out3=0, caps=(1024, 1024, 512), rider=None):
    a0, b0 = pairs[0]
    if mode == 'nn':
        m, k = a0.shape
        n = b0.shape[0] * b0.shape[2] if b3 else b0.shape[1]
    elif mode == 'nt':
        m = a0.shape[0]
        n, k = (b0.shape[1], b0.shape[0] * b0.shape[2]) if b3 else b0.shape
    else:
        k, m = a0.shape
        n = b0.shape[1]
    tm = _tile(m, caps[0])
    tn = _tile(n, caps[1])
    tk = _tile(k, caps[2])
    if b3 and mode == 'nn':
        tn = b0.shape[2]
    if b3 and mode == 'nt':
        tk = b0.shape[2]
    if out3:
        tn = n // out3
    nk = k // tk
    if mode == 'tn':
        a_spec = pl.BlockSpec((tk, tm), lambda i, j, kk: (kk, i))
        b_spec = pl.BlockSpec((tk, tn), lambda i, j, kk: (kk, j))
        dims = ((0,), (0,))
    elif mode == 'nn':
        a_spec = pl.BlockSpec((tm, tk), lambda i, j, kk: (i, kk))
        b_spec = (pl.BlockSpec((None, tk, tn), lambda i, j, kk: (j, kk, 0)) if b3
                  else pl.BlockSpec((tk, tn), lambda i, j, kk: (kk, j)))
        dims = ((1,), (0,))
    else:
        a_spec = pl.BlockSpec((tm, tk), lambda i, j, kk: (i, kk))
        b_spec = (pl.BlockSpec((None, tn, tk), lambda i, j, kk: (kk, j, 0)) if b3
                  else pl.BlockSpec((tn, tk), lambda i, j, kk: (j, kk)))
        dims = ((1,), (1,))
    in_specs, operands = [], []
    for a, b in pairs:
        in_specs += [a_spec, b_spec]
        operands += [a, b]
    block_bytes = len(pairs) * (_nbytes((tm, tk), a0.dtype) + _nbytes((tk, tn), b0.dtype))
    if residual is not None:
        in_specs.append(pl.BlockSpec((tm, tn), lambda i, j, kk: (i, j)))
        operands.append(residual)
        block_bytes += _nbytes((tm, tn), F32)
    if out3:
        out_spec = pl.BlockSpec((None, tm, tn), lambda i, j, kk: (j, i, 0))
        out_shape = jax.ShapeDtypeStruct((out3, m, tn), out_dtype)
    else:
        out_spec = pl.BlockSpec((tm, tn), lambda i, j, kk: (i, j))
        out_shape = jax.ShapeDtypeStruct((m, n), out_dtype)
    block_bytes += _nbytes((tm, tn), out_dtype) + _nbytes((tm, tn), F32)
    n_pairs = len(pairs)
    has_res = residual is not None

    def body(*refs):
        o_ref, acc = refs[-2], refs[-1]
        kk = pl.program_id(2)

        def product():
            part = None
            for p in range(n_pairs):
                d = lax.dot_general(refs[2 * p][...].astype(BF16), refs[2 * p + 1][...].astype(BF16),
                                    (dims, ((), ())), preferred_element_type=F32)
                part = d if part is None else part + d
            return part

        def finish(r):
            if scale != 1.0:
                r = r * scale
            if has_res:
                r = refs[2 * n_pairs][...] + r
            o_ref[...] = r.astype(out_dtype)

        if nk == 1:
            finish(product())
            return

        @pl.when(kk == 0)
        def _():
            acc[...] = product()

        if nk > 2:
            @pl.when(jnp.logical_and(kk > 0, kk < nk - 1))
            def _():
                acc[...] += product()

        @pl.when(kk == nk - 1)
        def _():
            finish(acc[...] + product())

    res = _run(body, name=name, grid=(m // tm, n // tn, nk), in_specs=in_specs, out_specs=[out_spec], out_shape=[out_shape],
               scratch_shapes=[pltpu.VMEM((tm, tn), F32)], operands=operands, block_bytes=block_bytes, rider=rider)
    return res[0] if rider is None else (res[0][0], res[1])


def _rmsnorm_fwd(x, g, name):
    t, d = x.shape
    tm = _rows(t, 512)

    def body(x_ref, g_ref, o_ref):
        xv = x_ref[...]
        r = lax.rsqrt(jnp.mean(xv * xv, axis=1, keepdims=True) + NORM_EPS)
        o_ref[...] = (xv * r * g_ref[...]).astype(BF16)

    row = pl.BlockSpec((tm, d), lambda i: (i, 0))
    return _pallas_call(
        body, out_shape=jax.ShapeDtypeStruct((t, d), BF16), grid=(t // tm,),
        in_specs=[row, pl.BlockSpec((1, d), lambda i: (0, 0))], out_specs=row,
        compiler_params=_cparams(("parallel",), 2 * _nbytes((tm, d), F32)), name=name)(x, g)


def _rms_grad(xv, g, dn, d):
    r = lax.rsqrt(jnp.mean(xv * xv, axis=1, keepdims=True) + NORM_EPS)
    u = dn * g
    s = jnp.sum(xv * u, axis=1, keepdims=True)
    dx = r * u - xv * (r * r * r) * (s * (1.0 / d))
    return dx, dn * xv * r


def _rmsnorm_bwd(x, g, dn, dres, name):
    t, d = x.shape
    tm = _rows(t, 256)

    def body(x_ref, g_ref, dn_ref, dres_ref, dx_ref, dxb_ref, dg_ref):
        dx, dg_rows = _rms_grad(x_ref[...], g_ref[...], dn_ref[...].astype(F32), d)
        dx = dres_ref[...] + dx
        dx_ref[...] = dx
        dxb_ref[...] = dx.astype(BF16)

        @pl.when(pl.program_id(0) == 0)
        def _():
            dg_ref[...] = jnp.zeros_like(dg_ref)

        dg_ref[...] += jnp.sum(dg_rows, axis=0, keepdims=True)

    row = pl.BlockSpec((tm, d), lambda i: (i, 0))
    vec = pl.BlockSpec((1, d), lambda i: (0, 0))
    return _pallas_call(
        body, out_shape=(jax.ShapeDtypeStruct((t, d), F32), jax.ShapeDtypeStruct((t, d), BF16), jax.ShapeDtypeStruct((1, d), F32)),
        grid=(t // tm,), in_specs=[row, vec, row, row], out_specs=(row, row, vec),
        compiler_params=_cparams(("arbitrary",), 5 * _nbytes((tm, d), F32)), name=name)(x, g, dn, dres)


def _final_loss(x, g, target, name):
    t, d = x.shape
    tm = _rows(t, 256)

    def body(x_ref, g_ref, t_ref, loss_ref, dx_ref, dxb_ref, dg_ref):
        xv, gv = x_ref[...], g_ref[...]
        r = lax.rsqrt(jnp.mean(xv * xv, axis=1, keepdims=True) + NORM_EPS)
        err = xv * r * gv - t_ref[...]
        dx, dg_rows = _rms_grad(xv, gv, err * (1.0 / d), d)
        dx_ref[...] = dx
        dxb_ref[...] = dx.astype(BF16)

        @pl.when(pl.program_id(0) == 0)
        def _():
            dg_ref[...] = jnp.zeros_like(dg_ref)
            loss_ref[...] = jnp.zeros_like(loss_ref)

        dg_ref[...] += jnp.sum(dg_rows, axis=0, keepdims=True)
        row_loss = jnp.sum(err * err, axis=1, keepdims=True) * (0.5 / d)
        loss_ref[...] += jnp.sum(row_loss, axis=0, keepdims=True)

    row = pl.BlockSpec((tm, d), lambda i: (i, 0))
    vec = pl.BlockSpec((1, d), lambda i: (0, 0))
    return _pallas_call(
        body, out_shape=(jax.ShapeDtypeStruct((1, 1), F32), jax.ShapeDtypeStruct((t, d), F32),
                         jax.ShapeDtypeStruct((t, d), BF16), jax.ShapeDtypeStruct((1, d), F32)),
        grid=(t // tm,), in_specs=[row, vec, row], out_specs=(pl.BlockSpec((1, 1), lambda i: (0, 0)), row, row, vec),
        compiler_params=_cparams(("arbitrary",), 4 * _nbytes((tm, d), F32)), name=name)(x, g, target)


def _ffn_up(n, wg, wu, name, rider=None):
    t, d = n.shape
    s, _, f = wg.shape
    tm, tk = _tile(t, 1024), _tile(d, 1024)
    nk = d // tk

    def body(n_ref, wg_ref, wu_ref, a_ref, b_ref, h_ref, acc_g, acc_u):
        kk = pl.program_id(2)

        def products():
            nv = n_ref[...]
            return jnp.dot(nv, wg_ref[...], preferred_element_type=F32), jnp.dot(nv, wu_ref[...], preferred_element_type=F32)

        def finish(a, b):
            a_ref[...] = a.astype(BF16)
            b_ref[...] = b.astype(BF16)
            h_ref[...] = (a * jax.nn.sigmoid(a) * b).astype(BF16)

        if nk == 1:
            finish(*products())
            return

        @pl.when(kk == 0)
        def _():
            acc_g[...], acc_u[...] = products()

        if nk > 2:
            @pl.when(jnp.logical_and(kk > 0, kk < nk - 1))
            def _():
                pg, pu = products()
                acc_g[...] += pg
                acc_u[...] += pu

        @pl.when(kk == nk - 1)
        def _():
            pg, pu = products()
            finish(acc_g[...] + pg, acc_u[...] + pu)

    w_spec = pl.BlockSpec((None, tk, f), lambda i, j, kk: (j, kk, 0))
    o_spec = pl.BlockSpec((tm, f), lambda i, j, kk: (i, j))
    out = jax.ShapeDtypeStruct((t, s * f), BF16)
    block_bytes = _nbytes((tm, tk), BF16) + 2 * _nbytes((tk, f), BF16) + 3 * _nbytes((tm, f), BF16) + 2 * _nbytes((tm, f), F32)
    return _run(body, name=name, grid=(t // tm, s, nk),
                in_specs=[pl.BlockSpec((tm, tk), lambda i, j, kk: (i, kk)), w_spec, w_spec], out_specs=[o_spec, o_spec, o_spec],
                out_shape=[out, out, out], scratch_shapes=[pltpu.VMEM((tm, f), F32), pltpu.VMEM((tm, f), F32)],
                operands=[n, wg, wu], block_bytes=block_bytes, rider=rider)


def _ffn_bwd_act(dx, wd, a, b, name):
    t, d = dx.shape
    f = wd.shape[0]
    tm, tn, tk = _tile(t, 1024), _tile(f, 1536), _tile(d, 1024)
    nk = d // tk

    def body(dx_ref, wd_ref, a_ref, b_ref, da_ref, db_ref, acc):
        kk = pl.program_id(2)

        def product():
            return lax.dot_general(dx_ref[...], wd_ref[...], (((1,), (1,)), ((), ())), preferred_element_type=F32)

        def finish(r):
            dh = 0.5 * r
            av, bv = a_ref[...].astype(F32), b_ref[...].astype(F32)
            sg = jax.nn.sigmoid(av)
            da_ref[...] = (dh * bv * (sg * (1.0 + av * (1.0 - sg)))).astype(BF16)
            db_ref[...] = (dh * (av * sg)).astype(BF16)

        if nk == 1:
            finish(product())
            return

        @pl.when(kk == 0)
        def _():
            acc[...] = product()

        if nk > 2:
            @pl.when(jnp.logical_and(kk > 0, kk < nk - 1))
            def _():
                acc[...] += product()

        @pl.when(kk == nk - 1)
        def _():
            finish(acc[...] + product())

    act = pl.BlockSpec((tm, tn), lambda i, j, kk: (i, j))
    out = jax.ShapeDtypeStruct((t, f), BF16)
    block_bytes = _nbytes((tm, tk), BF16) + _nbytes((tn, tk), BF16) + 4 * _nbytes((tm, tn), BF16) + _nbytes((tm, tn), F32)
    return _pallas_call(
        body, out_shape=(out, out), grid=(t // tm, f // tn, nk),
        in_specs=[pl.BlockSpec((tm, tk), lambda i, j, kk: (i, kk)), pl.BlockSpec((tn, tk), lambda i, j, kk: (j, kk)),
                  act, act],
        out_specs=(act, act), scratch_shapes=[pltpu.VMEM((tm, tn), F32)],
        compiler_params=_cparams(("parallel", "parallel", "arbitrary"), block_bytes), name=name)(dx, wd, a, b)


AXIS = dict(BIG)


def _full(wb, n):
    _, _, r, ccols = wb[n].shape
    return wb[n].reshape(N_CHIPS, 2 * r, ccols) if AXIS[n] == 1 else wb[n].reshape(N_CHIPS * 2 * r, ccols)


def _gather(wb, specs):
    items, names = [], []
    for s in specs:
        n, r0, r1 = (s, 0, wb[s].shape[2]) if isinstance(s, str) else s
        items.append((wb[n], r0, r1))
        if n not in names:
            names.append(n)
    return _gather_rider(items), names


def _landed(wb, names, results):
    for n, r in zip(names, results):
        wb[n] = r


def _reduce_first(grads, names, wb, c_idx):
    g4 = [g.reshape(wb[n].shape) for g, n in zip(grads, names)]
    from_sibling = _sibling_exchange(g4, "rs_sibling_" + names[0])
    return [_sibling_sum(a, b, c_idx, f"rs_sum1_{n}") for a, b, n in zip(g4, from_sibling, names)]


def _ffn_forward(x, gain, wb, tag, up_specs, down_specs):
    n = _rmsnorm_fwd(x, gain, f"{tag}_norm")
    rider, names = _gather(wb, up_specs)
    (a, b, h), got = _ffn_up(n, _full(wb, f"{tag}_w_gate"), _full(wb, f"{tag}_w_up"), f"{tag}_up", rider=rider)
    _landed(wb, names, got)
    down = dict(scale=0.5, residual=x, caps=(1024, 1024, 1536))
    if down_specs:
        rider, names = _gather(wb, down_specs)
        x_next, got = _matmul([(h, _full(wb, f"{tag}_w_down"))], 'nn', F32, f"{tag}_down", rider=rider, **down)
        _landed(wb, names, got)
    else:
        x_next = _matmul([(h, _full(wb, f"{tag}_w_down"))], 'nn', F32, f"{tag}_down", **down)
    return x_next, (n, a, b, h)


def _ffn_backward(x, gain, wb, saved, dx_next, dx_next_b, c_idx, tag):
    n, a, b, h = saved
    wg, wu, wd = (f"{tag}_w_gate", f"{tag}_w_up", f"{tag}_w_down")
    da, db = _ffn_bwd_act(dx_next_b, _full(wb, wd), a, b, f"{tag}_bwd_act")
    g_wd = _matmul([(h, dx_next_b)], 'tn', BF16, f"{tag}_dwd", scale=0.5, caps=(1536, 1024, 1024))
    (p_wd,) = _reduce_first([g_wd], [wd], wb, c_idx)
    g_wg, (r_wd,) = _matmul([(n, da)], 'tn', BF16, f"{tag}_dwg", out3=N_CHIPS, caps=(1024, 1024, 1024), rider=_scatter_rider([p_wd]))
    g_wu = _matmul([(n, db)], 'tn', BF16, f"{tag}_dwu", out3=N_CHIPS, caps=(1024, 1024, 1024))
    p_wg, p_wu = _reduce_first([g_wg, g_wu], [wg, wu], wb, c_idx)
    dn, (r_wg, r_wu) = _matmul([(da, _full(wb, wg)), (db, _full(wb, wu))], 'nt', F32, f"{tag}_dn", b3=True,
                               rider=_scatter_rider([p_wg, p_wu]))
    dx, dx_b, g_gain = _rmsnorm_bwd(x, gain, dn, dx_next, f"{tag}_norm_bwd")
    return dx, dx_b, g_gain, {wg: (p_wg, r_wg), wu: (p_wu, r_wu), wd: (p_wd, r_wd)}


def _rope_tables(seq):
    half = ROPE_DIM // 2
    inv_freq = ROPE_THETA ** (-jnp.arange(0, ROPE_DIM, 2, dtype=F32) / ROPE_DIM)
    ang = jnp.arange(seq).astype(F32)[:, None] * inv_freq[None, :]
    cos, sin = jnp.cos(ang), jnp.sin(ang)
    zeros = lambda w: jnp.zeros((seq, w), F32)
    c = jnp.concatenate([cos, cos, jnp.ones((seq, HEAD_DIM - ROPE_DIM), F32)], axis=1)
    s_up = jnp.concatenate([-sin, zeros(HEAD_DIM - half)], axis=1)
    s_dn = jnp.concatenate([zeros(half), sin, zeros(HEAD_DIM - ROPE_DIM)], axis=1)
    return c, s_up, s_dn


def _rotate(xv, cv, uv, dv):
    half = ROPE_DIM // 2
    return xv * cv + pltpu.roll(xv, HEAD_DIM - half, 1) * uv + pltpu.roll(xv, half, 1) * dv


def _stage(tm):
    return pltpu.VMEM((HEADS_PER_GROUP, tm, HEAD_DIM), F32)


def _to_groups(stage, o_ref, dil):
    rows = stage.shape[1] // dil
    for r in range(dil):
        for h in range(HEADS_PER_GROUP):
            col = r * GROUP_WIDTH + h * HEAD_DIM
            o_ref[:, col:col + HEAD_DIM] = stage[h, pl.ds(r, rows, stride=dil), :].astype(o_ref.dtype)


def _from_groups(g_ref, stage, dil):
    rows = stage.shape[1] // dil
    for r in range(dil):
        for h in range(HEADS_PER_GROUP):
            col = r * GROUP_WIDTH + h * HEAD_DIM
            stage[h, pl.ds(r, rows, stride=dil), :] = g_ref[:, col:col + HEAD_DIM].astype(F32)


def _group_spec(tm, dil):
    return pl.BlockSpec((tm // dil, dil * GROUP_WIDTH), lambda i: (i, 0))


def _group_shape(t, dil, dtype):
    return jax.ShapeDtypeStruct((t // dil, dil * GROUP_WIDTH), dtype)


def _rope_fwd(proj, tables, name):
    t = proj.shape[0]
    tm = _rows(t, 512)
    att_w = N_GROUPS * GROUP_WIDTH
    dilated = [(gi, dil) for gi, dil in enumerate(DILATIONS) if dil > 1]

    def body(x_ref, c_ref, up_ref, dn_ref, qk0_ref, *rest):
        outs, stage = rest[:-1], rest[-1]
        cv, uv, dv = c_ref[...], up_ref[...], dn_ref[...]
        for part in range(2):
            for gi, dil in enumerate(DILATIONS):
                for h in range(HEADS_PER_GROUP):
                    col = part * att_w + gi * GROUP_WIDTH + h * HEAD_DIM
                    y = _rotate(x_ref[:, col:col + HEAD_DIM].astype(F32), cv, uv, dv)
                    if dil == 1:
                        qk0_ref[:, part * GROUP_WIDTH + h * HEAD_DIM:part * GROUP_WIDTH + (h + 1) * HEAD_DIM] = y.astype(BF16)
                    else:
                        stage[h] = y
                if dil > 1:
                    _to_groups(stage, outs[3 * dilated.index((gi, dil)) + part], dil)
        for n, (gi, dil) in enumerate(dilated):
            col = 2 * att_w + gi * GROUP_WIDTH
            for h in range(HEADS_PER_GROUP):
                stage[h] = x_ref[:, col + h * HEAD_DIM:col + (h + 1) * HEAD_DIM].astype(F32)
            _to_groups(stage, outs[3 * n + 2], dil)

    tab = pl.BlockSpec((tm, HEAD_DIM), lambda i: (i, 0))
    out_shape = [jax.ShapeDtypeStruct((t, 2 * GROUP_WIDTH), BF16)]
    out_specs = [pl.BlockSpec((tm, 2 * GROUP_WIDTH), lambda i: (i, 0))]
    for _, dil in dilated:
        out_shape += [_group_shape(t, dil, BF16)] * 3
        out_specs += [_group_spec(tm, dil)] * 3
    res = _pallas_call(
        body, out_shape=out_shape, grid=(t // tm,),
        in_specs=[pl.BlockSpec((tm, 3 * att_w), lambda i: (i, 0)), tab, tab, tab], out_specs=out_specs,
        scratch_shapes=[_stage(tm)],
        compiler_params=_cparams(("parallel",), 4 * _nbytes((tm, 3 * att_w), BF16)), name=name)(proj, *tables)
    return res[0], [tuple(res[1 + 3 * n:4 + 3 * n]) for n in range(len(dilated))]


def _rope_bwd(dq0, dk0, dv0, grouped, tables, name):
    t = dq0.shape[0]
    tm = _rows(t, 512)
    att_w = N_GROUPS * GROUP_WIDTH
    dilated = [(gi, dil) for gi, dil in enumerate(DILATIONS) if dil > 1]
    c, s_up, s_dn = tables

    def body(c_ref, up_ref, dn_ref, dq0_ref, dk0_ref, dv0_ref, *rest):
        g_refs, o_ref, stage = rest[:-2], rest[-2], rest[-1]
        cv, uv, dv = c_ref[...], -up_ref[...], -dn_ref[...]
        for part, first in enumerate((dq0_ref, dk0_ref)):
            for gi, dil in enumerate(DILATIONS):
                if dil > 1:
                    _from_groups(g_refs[3 * dilated.index((gi, dil)) + part], stage, dil)
                for h in range(HEADS_PER_GROUP):
                    sl = slice(h * HEAD_DIM, (h + 1) * HEAD_DIM)
                    xv = first[:, sl].astype(F32) if dil == 1 else stage[h]
                    col = part * att_w + gi * GROUP_WIDTH + h * HEAD_DIM
                    o_ref[:, col:col + HEAD_DIM] = _rotate(xv, cv, uv, dv).astype(BF16)
        for gi, dil in enumerate(DILATIONS):
            col = 2 * att_w + gi * GROUP_WIDTH
            if dil == 1:
                o_ref[:, col:col + GROUP_WIDTH] = dv0_ref[...]
            else:
                _from_groups(g_refs[3 * dilated.index((gi, dil)) + 2], stage, dil)
                for h in range(HEADS_PER_GROUP):
                    o_ref[:, col + h * HEAD_DIM:col + (h + 1) * HEAD_DIM] = stage[h].astype(BF16)

    tab = pl.BlockSpec((tm, HEAD_DIM), lambda i: (i, 0))
    nat = pl.BlockSpec((tm, GROUP_WIDTH), lambda i: (i, 0))
    in_specs, operands = [tab, tab, tab, nat, nat, nat], [c, s_up, s_dn, dq0, dk0, dv0]
    for (_, dil), arrs in zip(dilated, grouped):
        in_specs += [_group_spec(tm, dil)] * 3
        operands += list(arrs)
    return _pallas_call(
        body, out_shape=jax.ShapeDtypeStruct((t, 3 * att_w), BF16), grid=(t // tm,), in_specs=in_specs,
        out_specs=pl.BlockSpec((tm, 3 * att_w), lambda i: (i, 0)), scratch_shapes=[_stage(tm)],
        compiler_params=_cparams(("parallel",), 4 * _nbytes((tm, 3 * att_w), BF16)), name=name)(*operands)


def _regroup(arrs, name):
    t = arrs[0].shape[0]
    tm = _rows(t, 512)
    dilated = [dil for dil in DILATIONS if dil > 1]
    n_in = len(arrs)

    def body(*refs):
        ins, outs, stage = refs[:n_in], refs[n_in:-1], refs[-1]
        for j, x_ref in enumerate(ins):
            for h in range(HEADS_PER_GROUP):
                stage[h] = x_ref[:, h * HEAD_DIM:(h + 1) * HEAD_DIM]
            for n, dil in enumerate(dilated):
                _to_groups(stage, outs[n * n_in + j], dil)

    nat = pl.BlockSpec((tm, GROUP_WIDTH), lambda i: (i, 0))
    res = _pallas_call(
        body, out_shape=[_group_shape(t, dil, F32) for dil in dilated for _ in arrs], grid=(t // tm,),
        in_specs=[nat] * n_in, out_specs=[_group_spec(tm, dil) for dil in dilated for _ in arrs], scratch_shapes=[_stage(tm)],
        compiler_params=_cparams(("parallel",), 3 * n_in * _nbytes((tm, GROUP_WIDTH), F32)), name=name)(*arrs)
    return [tuple(res[n * n_in:(n + 1) * n_in]) for n in range(len(dilated))]


def _query_mask(has_prev):
    qi = lax.broadcasted_iota(jnp.int32, (ATT_BLOCK, 2 * ATT_BLOCK), 0)
    col = lax.broadcasted_iota(jnp.int32, (ATT_BLOCK, 2 * ATT_BLOCK), 1)
    prev = jnp.logical_and(jnp.logical_and(col < ATT_BLOCK, col >= qi), has_prev)
    return jnp.logical_or(prev, jnp.logical_and(col >= ATT_BLOCK, col - ATT_BLOCK <= qi))


def _key_mask(has_next):
    row = lax.broadcasted_iota(jnp.int32, (2 * ATT_BLOCK, ATT_BLOCK), 0)
    kj = lax.broadcasted_iota(jnp.int32, (2 * ATT_BLOCK, ATT_BLOCK), 1)
    nxt = jnp.logical_and(jnp.logical_and(row >= ATT_BLOCK, kj >= row - ATT_BLOCK), has_next)
    return jnp.logical_or(nxt, jnp.logical_and(row < ATT_BLOCK, kj <= row))


def _scores(q, k):
    return lax.dot_general(q, k, (((1,), (1,)), ((), ())), preferred_element_type=F32) * (HEAD_DIM ** -0.5)


def _att_fwd(q, k, v, offs, dil, name):
    qo, ko, vo = offs
    length = q.shape[0]
    nb = length // ATT_BLOCK

    def body(q_ref, kp_ref, kc_ref, vp_ref, vc_ref, o_ref, lse_ref):
        mask = _query_mask(pl.program_id(1) > 0)
        heads = [slice(h * HEAD_DIM, (h + 1) * HEAD_DIM) for h in range(HEADS_PER_GROUP)]
        ks = [jnp.concatenate([kp_ref[:, sl], kc_ref[:, sl]], axis=0) for sl in heads]
        vs = [jnp.concatenate([vp_ref[:, sl], vc_ref[:, sl]], axis=0) for sl in heads]
        ss = [jnp.where(mask, _scores(q_ref[:, sl], kv), MASKED) for sl, kv in zip(heads, ks)]
        ms = [jnp.max(s, axis=1, keepdims=True) for s in ss]
        ps = [jnp.exp(s - m) for s, m in zip(ss, ms)]
        ls = [jnp.sum(p, axis=1, keepdims=True) for p in ps]
        accs = [jnp.dot(p.astype(BF16), vv, preferred_element_type=F32) for p, vv in zip(ps, vs)]
        for sl, acc, m, l in zip(heads, accs, ms, ls):
            o_ref[:, sl] = acc / l
            lse_ref[:, sl] = jnp.broadcast_to(m + jnp.log(l), (ATT_BLOCK, HEAD_DIM))

    def spec(off, prev):
        if prev:
            return pl.BlockSpec((ATT_BLOCK, GROUP_WIDTH), lambda r, n: (jnp.maximum(n - 1, 0), off + r))
        return pl.BlockSpec((ATT_BLOCK, GROUP_WIDTH), lambda r, n: (n, off + r))

    out = jax.ShapeDtypeStruct((length, dil * GROUP_WIDTH), F32)
    o_spec = pl.BlockSpec((ATT_BLOCK, GROUP_WIDTH), lambda r, n: (n, r))
    return _pallas_call(
        body, out_shape=(out, out), grid=(dil, nb),
        in_specs=[spec(qo, False), spec(ko, True), spec(ko, False), spec(vo, True), spec(vo, False)],
        out_specs=(o_spec, o_spec),
        compiler_params=_cparams(("parallel", "parallel"), 8 * _nbytes((ATT_BLOCK, GROUP_WIDTH), F32)), name=name)(q, k, k, v, v)


def _att_combine(outs, lses, name):
    t = outs[0].shape[0] * DILATIONS[0]
    tm = _rows(t, 512)

    def body(*refs):
        o_refs, l_refs = refs[:N_GROUPS], refs[N_GROUPS:2 * N_GROUPS]
        ob_ref, of_ref, lse_ref = refs[2 * N_GROUPS:2 * N_GROUPS + 3]
        stages = list(refs[2 * N_GROUPS + 3:])
        staged = []
        for o_ref, l_ref, dil in zip(o_refs, l_refs, DILATIONS):
            if dil > 1:
                so, sl = stages.pop(), stages.pop()
                _from_groups(o_ref, so, dil)
                _from_groups(l_ref, sl, dil)
                staged.append((so, sl))
            else:
                staged.append(None)
        for h in range(HEADS_PER_GROUP):
            hs = slice(h * HEAD_DIM, (h + 1) * HEAD_DIM)
            os_ = [o_ref[:, hs] if st is None else st[0][h] for o_ref, st in zip(o_refs, staged)]
            ls = [l_ref[:, hs] if st is None else st[1][h] for l_ref, st in zip(l_refs, staged)]
            m = functools.reduce(jnp.maximum, ls)
            ws = [jnp.exp(l - m) for l in ls]
            den = functools.reduce(jnp.add, ws)
            num = functools.reduce(jnp.add, [w * o for w, o in zip(ws, os_)])
            o = num / den
            ob_ref[:, hs] = o.astype(BF16)
            of_ref[:, hs] = o
            lse_ref[:, hs] = m + jnp.log(den)

    blk = pl.BlockSpec((tm, GROUP_WIDTH), lambda i: (i, 0))
    specs = [blk if dil == 1 else _group_spec(tm, dil) for dil in DILATIONS]
    f32 = jax.ShapeDtypeStruct((t, GROUP_WIDTH), F32)
    n_stage = 2 * sum(dil > 1 for dil in DILATIONS)
    return _pallas_call(
        body, out_shape=(jax.ShapeDtypeStruct((t, GROUP_WIDTH), BF16), f32, f32), grid=(t // tm,),
        in_specs=specs * 2, out_specs=(blk, blk, blk), scratch_shapes=[_stage(tm)] * n_stage,
        compiler_params=_cparams(("parallel",), 13 * _nbytes((tm, GROUP_WIDTH), F32)), name=name)(*outs, *lses)


def _att_delta(do, o, name):
    t = o.shape[0]
    tm = _rows(t, 512)

    def body(do_ref, o_ref, d_ref):
        for h in range(HEADS_PER_GROUP):
            sl = slice(h * HEAD_DIM, (h + 1) * HEAD_DIM)
            s = jnp.sum(do_ref[:, sl] * o_ref[:, sl], axis=1, keepdims=True)
            d_ref[:, sl] = jnp.broadcast_to(s, (tm, HEAD_DIM))

    blk = pl.BlockSpec((tm, GROUP_WIDTH), lambda i: (i, 0))
    return _pallas_call(
        body, out_shape=jax.ShapeDtypeStruct((t, GROUP_WIDTH), F32), grid=(t // tm,), in_specs=[blk, blk], out_specs=blk,
        compiler_params=_cparams(("parallel",), 3 * _nbytes((tm, GROUP_WIDTH), F32)), name=name)(do, o)


def _att_bwd_dq(q, k, v, do, lse, delta, offs, dil, name):
    qo, ko, vo = offs
    length = q.shape[0]
    nb = length // ATT_BLOCK
    scale = HEAD_DIM ** -0.5

    def body(q_ref, kp_ref, kc_ref, vp_ref, vc_ref, do_ref, lse_ref, dl_ref, dq_ref):
        mask = _query_mask(pl.program_id(1) > 0)
        heads = [slice(h * HEAD_DIM, (h + 1) * HEAD_DIM) for h in range(HEADS_PER_GROUP)]
        wide = lambda ref, sl: jnp.concatenate([ref[:, sl], ref[:, sl]], axis=1)
        ks = [jnp.concatenate([kp_ref[:, sl], kc_ref[:, sl]], axis=0) for sl in heads]
        vs = [jnp.concatenate([vp_ref[:, sl], vc_ref[:, sl]], axis=0) for sl in heads]
        ps = [jnp.exp(jnp.where(mask, _scores(q_ref[:, sl], kv), MASKED) - wide(lse_ref, sl)) for sl, kv in zip(heads, ks)]
        dps = [lax.dot_general(do_ref[:, sl].astype(BF16), vv, (((1,), (1,)), ((), ())), preferred_element_type=F32)
               for sl, vv in zip(heads, vs)]
        dss = [(p * (dp - wide(dl_ref, sl)) * scale).astype(BF16) for sl, p, dp in zip(heads, ps, dps)]
        dqs = [jnp.dot(ds, kv, preferred_element_type=F32) for ds, kv in zip(dss, ks)]
        for sl, dq in zip(heads, dqs):
            dq_ref[:, sl] = dq.astype(BF16)

    def spec(off, prev):
        if prev:
            return pl.BlockSpec((ATT_BLOCK, GROUP_WIDTH), lambda r, n: (jnp.maximum(n - 1, 0), off + r))
        return pl.BlockSpec((ATT_BLOCK, GROUP_WIDTH), lambda r, n: (n, off + r))

    own = pl.BlockSpec((ATT_BLOCK, GROUP_WIDTH), lambda r, n: (n, r))
    return _pallas_call(
        body, out_shape=jax.ShapeDtypeStruct((length, dil * GROUP_WIDTH), BF16), grid=(dil, nb),
        in_specs=[spec(qo, False), spec(ko, True), spec(ko, False), spec(vo, True), spec(vo, False), own, own, own],
        out_specs=own,
        compiler_params=_cparams(("parallel", "parallel"), 10 * _nbytes((ATT_BLOCK, GROUP_WIDTH), F32)),
        name=name)(q, k, k, v, v, do, lse, delta)


def _att_bwd_dkv(q, k, v, do, lse, delta, offs, dil, name):
    qo, ko, vo = offs
    length = q.shape[0]
    nb = length // ATT_BLOCK
    scale = HEAD_DIM ** -0.5

    def body(k_ref, v_ref, qc_ref, qn_ref, doc_ref, don_ref, lsec_ref, lsen_ref, dlc_ref, dln_ref, dk_ref, dv_ref):
        mask = _key_mask(pl.program_id(1) < nb - 1)
        heads = [slice(h * HEAD_DIM, (h + 1) * HEAD_DIM) for h in range(HEADS_PER_GROUP)]
        both = lambda cur, nxt, sl: jnp.concatenate([cur[:, sl], nxt[:, sl]], axis=0)
        qs = [both(qc_ref, qn_ref, sl) for sl in heads]
        dos = [both(doc_ref, don_ref, sl).astype(BF16) for sl in heads]
        ps = [jnp.exp(jnp.where(mask, _scores(qv, k_ref[:, sl]), MASKED) - both(lsec_ref, lsen_ref, sl)) for sl, qv in zip(heads, qs)]
        dps = [lax.dot_general(dov, v_ref[:, sl], (((1,), (1,)), ((), ())), preferred_element_type=F32) for sl, dov in zip(heads, dos)]
        dss = [(p * (dp - both(dlc_ref, dln_ref, sl)) * scale).astype(BF16) for sl, p, dp in zip(heads, ps, dps)]
        dvs = [lax.dot_general(p.astype(BF16), dov, (((0,), (0,)), ((), ())), preferred_element_type=F32) for p, dov in zip(ps, dos)]
        dks = [lax.dot_general(ds, qv, (((0,), (0,)), ((), ())), preferred_element_type=F32) for ds, qv in zip(dss, qs)]
        for sl, dk, dv in zip(heads, dks, dvs):
            dk_ref[:, sl] = dk.astype(BF16)
            dv_ref[:, sl] = dv.astype(BF16)

    def spec(off, nxt):
        if nxt:
            return pl.BlockSpec((ATT_BLOCK, GROUP_WIDTH), lambda r, n: (jnp.minimum(n + 1, nb - 1), off + r))
        return pl.BlockSpec((ATT_BLOCK, GROUP_WIDTH), lambda r, n: (n, off + r))

    own = pl.BlockSpec((ATT_BLOCK, GROUP_WIDTH), lambda r, n: (n, r))
    out = jax.ShapeDtypeStruct((length, dil * GROUP_WIDTH), BF16)
    return _pallas_call(
        body, out_shape=(out, out), grid=(dil, nb),
        in_specs=[spec(ko, False), spec(vo, False), spec(qo, False), spec(qo, True), spec(0, False), spec(0, True),
                  spec(0, False), spec(0, True), spec(0, False), spec(0, True)],
        out_specs=(own, own),
        compiler_params=_cparams(("parallel", "parallel"), 12 * _nbytes((ATT_BLOCK, GROUP_WIDTH), F32)),
        name=name)(k, v, q, q, do, do, lse, lse, delta, delta)


def _gelu(x):
    return 0.5 * x * (1.0 + lax.erf(x * (2.0 ** -0.5)))


def _gelu_grad(x):
    return 0.5 * (1.0 + lax.erf(x * (2.0 ** -0.5))) + x * jnp.exp(-0.5 * x * x) * ((2.0 * jnp.pi) ** -0.5)


def _sg_normed(vs, lg, lb):
    gv = _gelu(vs)
    mu = jnp.mean(gv, axis=1, keepdims=True)
    xc = gv - mu
    rstd = lax.rsqrt(jnp.mean(xc * xc, axis=1, keepdims=True) + LN_EPS)
    z = xc * rstd
    return z, rstd, z * lg + lb


def _sg_tril():
    row = lax.broadcasted_iota(jnp.int32, (SG_CHUNK, SG_CHUNK), 0)
    col = lax.broadcasted_iota(jnp.int32, (SG_CHUNK, SG_CHUNK), 1)
    return row >= col


def _sg_fwd(proj, u_blk, vs_blk, lg, lb, sg_w, bias, name):
    t = proj.shape[0]
    width = SG_GROUPS * SG_GROUP_DIM

    def body(u_ref, vs_ref, lg_ref, lb_ref, w_ref, bias_ref, o_ref):
        _, _, vn = _sg_normed(vs_ref[...].astype(F32), lg_ref[...], lb_ref[...])
        vn = vn.astype(BF16)
        tril = _sg_tril()
        for g in range(SG_GROUPS):
            sl = slice(g * SG_GROUP_DIM, (g + 1) * SG_GROUP_DIM)
            w = jnp.where(tril, w_ref[g], 0.0).astype(BF16)
            sp = jnp.dot(w, vn[:, sl], preferred_element_type=F32) + bias_ref[:, sl]
            o_ref[:, sl] = (_gelu(u_ref[:, sl].astype(F32)) * sp).astype(BF16)

    vec = pl.BlockSpec((1, width), lambda i: (0, 0))
    return _pallas_call(
        body, out_shape=jax.ShapeDtypeStruct((t, width), BF16), grid=(t // SG_CHUNK,),
        in_specs=[pl.BlockSpec((SG_CHUNK, width), lambda i: (i, u_blk)), pl.BlockSpec((SG_CHUNK, width), lambda i: (i, vs_blk)),
                  vec, vec, pl.BlockSpec((SG_GROUPS, SG_CHUNK, SG_CHUNK), lambda i: (0, 0, 0)),
                  pl.BlockSpec((SG_CHUNK, width), lambda i: (0, 0))],
        out_specs=pl.BlockSpec((SG_CHUNK, width), lambda i: (i, 0)),
        compiler_params=_cparams(("parallel",), 8 * _nbytes((SG_CHUNK, width), F32)), name=name)(proj, proj, lg, lb, sg_w, bias)


def _sg_bwd(proj, u_blk, vs_blk, dsu, lg, lb, sg_w, bias, name):
    t = proj.shape[0]
    width = SG_GROUPS * SG_GROUP_DIM

    def body(u_ref, vs_ref, dsu_ref, lg_ref, lb_ref, w_ref, bias_ref, du_ref, dvs_ref, dw_ref, dbias_ref, dlg_ref, dlb_ref):
        @pl.when(pl.program_id(0) == 0)
        def _():
            dw_ref[...] = jnp.zeros_like(dw_ref)
            dbias_ref[...] = jnp.zeros_like(dbias_ref)
            dlg_ref[...] = jnp.zeros_like(dlg_ref)
            dlb_ref[...] = jnp.zeros_like(dlb_ref)

        vs = vs_ref[...].astype(F32)
        z, rstd, vn = _sg_normed(vs, lg_ref[...], lb_ref[...])
        vn = vn.astype(BF16)
        tril = _sg_tril()
        dvn = []
        for g in range(SG_GROUPS):
            sl = slice(g * SG_GROUP_DIM, (g + 1) * SG_GROUP_DIM)
            w = jnp.where(tril, w_ref[g], 0.0).astype(BF16)
            vg = vn[:, sl]
            sp = jnp.dot(w, vg, preferred_element_type=F32) + bias_ref[:, sl]
            uv = u_ref[:, sl].astype(F32)
            dsu_g = dsu_ref[:, sl].astype(F32)
            du_ref[:, sl] = (dsu_g * sp * _gelu_grad(uv)).astype(BF16)
            dsp = dsu_g * _gelu(uv)
            dsp_b = dsp.astype(BF16)
            dw = lax.dot_general(dsp_b, vg, (((1,), (1,)), ((), ())), preferred_element_type=F32)
            dw_ref[g] += jnp.where(tril, dw, 0.0)
            dbias_ref[:, sl] += jnp.broadcast_to(jnp.sum(dsp, axis=1, keepdims=True), (SG_CHUNK, SG_GROUP_DIM))
            dvn.append(lax.dot_general(w, dsp_b, (((0,), (0,)), ((), ())), preferred_element_type=F32))
        dvn = jnp.concatenate(dvn, axis=1)
        dlg_ref[...] += jnp.sum(dvn * z, axis=0, keepdims=True)
        dlb_ref[...] += jnp.sum(dvn, axis=0, keepdims=True)
        dz = dvn * lg_ref[...]
        dgv = rstd * (dz - jnp.mean(dz, axis=1, keepdims=True) - z * jnp.mean(dz * z, axis=1, keepdims=True))
        dvs_ref[...] = (dgv * _gelu_grad(vs)).astype(BF16)

    vec = pl.BlockSpec((1, width), lambda i: (0, 0))
    row = pl.BlockSpec((SG_CHUNK, width), lambda i: (i, 0))
    fixed = pl.BlockSpec((SG_CHUNK, width), lambda i: (0, 0))
    w_spec = pl.BlockSpec((SG_GROUPS, SG_CHUNK, SG_CHUNK), lambda i: (0, 0, 0))
    act = jax.ShapeDtypeStruct((t, width), BF16)
    return _pallas_call(
        body,
        out_shape=(act, act, jax.ShapeDtypeStruct((SG_GROUPS, SG_CHUNK, SG_CHUNK), F32),
                   jax.ShapeDtypeStruct((SG_CHUNK, width), F32), jax.ShapeDtypeStruct((1, width), F32),
                   jax.ShapeDtypeStruct((1, width), F32)),
        grid=(t // SG_CHUNK,),
        in_specs=[pl.BlockSpec((SG_CHUNK, width), lambda i: (i, u_blk)), pl.BlockSpec((SG_CHUNK, width), lambda i: (i, vs_blk)),
                  row, vec, vec, w_spec, fixed],
        out_specs=(row, row, w_spec, fixed, vec, vec),
        compiler_params=_cparams(("arbitrary",), 14 * _nbytes((SG_CHUNK, width), F32)),
        name=name)(proj, proj, dsu, lg, lb, sg_w, bias)


def _gate_fwd(proj, ga_blk, gs_blk, y_att, y_sg, name):
    t, d = y_att.shape
    tm, tn = _rows(t, 512), _tile(d, GROUP_WIDTH)

    def body(ga_ref, gs_ref, ya_ref, ys_ref, o_ref):
        o_ref[...] = (jax.nn.sigmoid(ga_ref[...].astype(F32)) * ya_ref[...].astype(F32)
                      + jax.nn.sigmoid(gs_ref[...].astype(F32)) * ys_ref[...].astype(F32)).astype(BF16)

    own = pl.BlockSpec((tm, tn), lambda i, j: (i, j))
    return _pallas_call(
        body, out_shape=jax.ShapeDtypeStruct((t, d), BF16), grid=(t // tm, d // tn),
        in_specs=[pl.BlockSpec((tm, tn), lambda i, j: (i, ga_blk + j)), pl.BlockSpec((tm, tn), lambda i, j: (i, gs_blk + j)),
                  own, own],
        out_specs=own, compiler_params=_cparams(("parallel", "parallel"), 6 * _nbytes((tm, tn), F32)),
        name=name)(proj, proj, y_att, y_sg)


def _gate_bwd(proj, ga_blk, gs_blk, y_att, y_sg, dmerged, name):
    t, d = y_att.shape
    tm, tn = _rows(t, 512), _tile(d, GROUP_WIDTH)

    def body(ga_ref, gs_ref, ya_ref, ys_ref, dm_ref, dya_ref, dys_ref, dga_ref, dgs_ref):
        dm = dm_ref[...].astype(F32)
        for g_ref, y_ref, dy_ref, dg_ref in ((ga_ref, ya_ref, dya_ref, dga_ref), (gs_ref, ys_ref, dys_ref, dgs_ref)):
            sg = jax.nn.sigmoid(g_ref[...].astype(F32))
            dy_ref[...] = (dm * sg).astype(BF16)
            dg_ref[...] = (dm * y_ref[...].astype(F32) * sg * (1.0 - sg)).astype(BF16)

    own = pl.BlockSpec((tm, tn), lambda i, j: (i, j))
    out = jax.ShapeDtypeStruct((t, d), BF16)
    return _pallas_call(
        body, out_shape=(out, out, out, out), grid=(t // tm, d // tn),
        in_specs=[pl.BlockSpec((tm, tn), lambda i, j: (i, ga_blk + j)), pl.BlockSpec((tm, tn), lambda i, j: (i, gs_blk + j)),
                  own, own, own],
        out_specs=(own, own, own, own), compiler_params=_cparams(("parallel", "parallel"), 10 * _nbytes((tm, tn), F32)),
        name=name)(proj, proj, y_att, y_sg, dmerged)


def _mixer_forward(x, wb, small, in_specs, sg_specs, out_specs):
    t, d = x.shape
    att_w = N_GROUPS * GROUP_WIDTH
    sg_w = SG_GROUPS * SG_GROUP_DIM
    n = _rmsnorm_fwd(x, small['mix_norm'], "mix_norm")
    rider, names = _gather(wb, in_specs)
    proj, got = _matmul([(n, _full(wb, 'w_in'))], 'nn', BF16, "mix_in", b3=True, caps=(1024, 1024, 1024), rider=rider)
    _landed(wb, names, got)
    tables = _rope_tables(t)
    qk0, grouped = _rope_fwd(proj, tables, "mix_rope")
    qkv = [(qk0, qk0, proj, (0, 1, 2 * N_GROUPS))] + [g + ((0, 0, 0),) for g in grouped]
    outs, lses = zip(*[_att_fwd(*args, dil, f"att_fwd{gi}") for gi, (args, dil) in enumerate(zip(qkv, DILATIONS))])
    o_b, o_f, lse = _att_combine(outs, lses, "att_combine")
    y_att = _matmul([(o_b, _full(wb, 'w_att_out'))], 'nn', BF16, "mix_att_out", b3=True)
    bias = jnp.repeat(small['sg_b'].T, SG_GROUP_DIM, axis=1)
    u_blk, vs_blk = 3 * att_w // sg_w, 3 * att_w // sg_w + 1
    su = _sg_fwd(proj, u_blk, vs_blk, small['sg_ln_g'], small['sg_ln_b'], small['sg_w'], bias, "sg_fwd")
    rider, names = _gather(wb, sg_specs)
    y_sg, got = _matmul([(su, _full(wb, 'w_sg_out'))], 'nn', BF16, "mix_sg_out", b3=True, rider=rider)
    _landed(wb, names, got)
    ga_blk = (3 * att_w + 2 * sg_w) // _tile(d, GROUP_WIDTH)
    gs_blk = ga_blk + d // _tile(d, GROUP_WIDTH)
    merged = _gate_fwd(proj, ga_blk, gs_blk, y_att, y_sg, "gate_fwd")
    rider, names = _gather(wb, out_specs)
    x_next, got = _matmul([(merged, _full(wb, 'w_out'))], 'nn', F32, "mix_out", residual=x, rider=rider)
    _landed(wb, names, got)
    saved = (n, proj, qkv, tables, o_b, o_f, lse, y_att, su, y_sg, merged, bias, (u_blk, vs_blk, ga_blk, gs_blk))
    return x_next, saved


def _mixer_backward(x, wb, small, saved, dx_next, dx_next_b, c_idx):
    n, proj, qkv, tables, o_b, o_f, lse, y_att, su, y_sg, merged, bias, (u_blk, vs_blk, ga_blk, gs_blk) = saved
    s = N_CHIPS
    dmerged = _matmul([(dx_next_b, _full(wb, 'w_out'))], 'nt', BF16, "mix_out_dx")
    g_w_out = _matmul([(merged, dx_next_b)], 'tn', BF16, "mix_out_dw", caps=(1024, 1024, 1024))
    dy_att, dy_sg, dg_att, dg_sg = _gate_bwd(proj, ga_blk, gs_blk, y_att, y_sg, dmerged, "gate_bwd")

    g_w_att_out = _matmul([(o_b, dy_att)], 'tn', BF16, "mix_att_out_dw", out3=s)
    do = _matmul([(dy_att, _full(wb, 'w_att_out'))], 'nt', F32, "mix_att_out_dx", b3=True)
    delta = _att_delta(do, o_f, "att_delta")
    stats = [(do, lse, delta)] + _regroup([do, lse, delta], "att_regroup")
    dqkv = []
    for gi, ((q, k, v, offs), st, dil) in enumerate(zip(qkv, stats, DILATIONS)):
        dq = _att_bwd_dq(q, k, v, *st, offs, dil, f"att_bwd_dq{gi}")
        dk, dv = _att_bwd_dkv(q, k, v, *st, offs, dil, f"att_bwd_dkv{gi}")
        dqkv.append((dq, dk, dv))
    dqkv = _rope_bwd(*dqkv[0], dqkv[1:], tables, "mix_rope_bwd")

    g_w_sg_out = _matmul([(su, dy_sg)], 'tn', BF16, "mix_sg_out_dw", out3=s)
    out_names = ['w_out', 'w_att_out', 'w_sg_out']
    out_parts = _reduce_first([g_w_out, g_w_att_out, g_w_sg_out], out_names, wb, c_idx)
    dsu = _matmul([(dy_sg, _full(wb, 'w_sg_out'))], 'nt', BF16, "mix_sg_out_dx", b3=True)
    du, dvs, g_sg_w, g_bias, g_lg, g_lb = _sg_bwd(proj, u_blk, vs_blk, dsu, small['sg_ln_g'], small['sg_ln_b'],
                                                   small['sg_w'], bias, "sg_bwd")
    gs = {'sg_w': g_sg_w, 'sg_b': g_bias[:, ::SG_GROUP_DIM].T, 'sg_ln_g': g_lg, 'sg_ln_b': g_lb}

    dproj = jnp.concatenate([dqkv, du, dvs, dg_att, dg_sg], axis=1)
    g_w_in, out_recv = _matmul([(n, dproj)], 'tn', BF16, "mix_in_dw", out3=s, caps=(1024, 1024, 1024),
                               rider=_scatter_rider(out_parts))
    (p_w_in,) = _reduce_first([g_w_in], ['w_in'], wb, c_idx)
    dn, (r_w_in,) = _matmul([(dproj, _full(wb, 'w_in'))], 'nt', F32, "mix_in_dx", b3=True, caps=(1024, 1024, 512),
                            rider=_scatter_rider([p_w_in]))
    dx, dx_b, gs['mix_norm'] = _rmsnorm_bwd(x, small['mix_norm'], dn, dx_next, "mix_norm_bwd")
    g = {nm: (p, r) for nm, p, r in zip(out_names, out_parts, out_recv)}
    g['w_in'] = (p_w_in, r_w_in)
    return dx, dx_b, g, gs


def _step(x, target, wb, small, c_idx):
    wb = dict(wb)
    rider, names = _gather(wb, ['ffn1_w_gate', 'ffn1_w_up'])
    _landed(wb, names, _exchange(rider, "gather_first"))
    half_in = wb['w_in'].shape[2] // 2
    x1, s1 = _ffn_forward(x, small['ffn1_norm'], wb, "ffn1", ['ffn1_w_down', ('w_in', 0, half_in)], [('w_in', half_in, 2 * half_in)])
    up_rows = wb['ffn2_w_up'].shape[2]
    up_cut = up_rows // 32 * 15
    x2, s2 = _mixer_forward(x1, wb, small, ['w_att_out', 'w_sg_out', 'w_out', 'ffn2_w_gate'],
                            [('ffn2_w_up', 0, up_cut)], [('ffn2_w_up', up_cut, up_rows)])
    x3, s3 = _ffn_forward(x2, small['ffn2_norm'], wb, "ffn2", ['ffn2_w_down'], None)
    loss, dx3, dx3_b, g_final = _final_loss(x3, small['final_norm'], target, "final_loss")
    gs = {'final_norm': g_final}
    dx2, dx2_b, gs['ffn2_norm'], g = _ffn_backward(x2, small['ffn2_norm'], wb, s3, dx3, dx3_b, c_idx, "ffn2")
    dx1, dx1_b, g_mix, gs_mix = _mixer_backward(x1, wb, small, s2, dx2, dx2_b, c_idx)
    g.update(g_mix)
    gs.update(gs_mix)
    dx0, _, gs['ffn1_norm'], g_ffn1 = _ffn_backward(x, small['ffn1_norm'], wb, s1, dx1, dx1_b, c_idx, "ffn1")
    g.update(g_ffn1)
    return loss, dx0, g, gs


def _cast_into_gathered(wt, p_idx, name):
    r, ccols = wt.shape[0] // 2, wt.shape[1]
    tm = _rows(r, 256)
    nb = r // tm

    def body(p_ref, w_ref, o_ref):
        o_ref[...] = w_ref[...].astype(BF16)

    grid_spec = pltpu.PrefetchScalarGridSpec(
        num_scalar_prefetch=1, grid=(2, nb),
        in_specs=[pl.BlockSpec((tm, ccols), lambda h, i, pr: (h * nb + i, 0))],
        out_specs=pl.BlockSpec((None, None, tm, ccols), lambda h, i, pr: (pr[0], h, i, 0)))
    return _pallas_call(body, out_shape=jax.ShapeDtypeStruct((N_CHIPS, 2, r, ccols), BF16), grid_spec=grid_spec,
                          compiler_params=_cparams(("parallel", "parallel"), 2 * _nbytes((tm, ccols), F32)), name=name)(p_idx, wt)


def _sibling_exchange(grads, name):
    nw = len(grads)

    def body(*refs):
        src, dst = refs[:nw], refs[nw:2 * nw]
        send_sems, recv_sems = refs[2 * nw:]
        x, y, c, _ = _place()
        cps = []
        for i in range(nw):
            cp = pltpu.make_async_remote_copy(src[i].at[:, 1 - c], dst[i], send_sems.at[i], recv_sems.at[i],
                                              device_id=(x, y, 1 - c), device_id_type=MESH)
            cp.start()
            cps.append(cp)
        for cp in cps:
            cp.wait()

    any_spec = pl.BlockSpec(memory_space=pl.ANY)
    return _pallas_call(
        body, out_shape=[jax.ShapeDtypeStruct((g.shape[0],) + g.shape[2:], g.dtype) for g in grads],
        in_specs=[any_spec] * nw, out_specs=[any_spec] * nw,
        scratch_shapes=[pltpu.SemaphoreType.DMA((nw,)), pltpu.SemaphoreType.DMA((nw,))],
        compiler_params=pltpu.CompilerParams(has_side_effects=True), name=name)(*grads)


def _half_exchange(bufs):
    nw = len(bufs)

    def body(*refs):
        dst = refs[nw:2 * nw]
        send_sems, recv_sems = refs[2 * nw:]
        x, y, c, _ = _place()
        cps = []
        for i in range(nw):
            mine = dst[i].at[c]
            cp = pltpu.make_async_remote_copy(mine, mine, send_sems.at[i], recv_sems.at[i],
                                              device_id=(x, y, 1 - c), device_id_type=MESH)
            cp.start()
            cps.append(cp)
        for i, cp in enumerate(cps):
            cp.wait_send()
            theirs = dst[i].at[1 - c]
            pltpu.make_async_remote_copy(theirs, theirs, send_sems.at[i], recv_sems.at[i],
                                         device_id=(x, y, 1 - c), device_id_type=MESH).wait_recv()

    any_spec = pl.BlockSpec(memory_space=pl.ANY)
    return _pallas_call(
        body, out_shape=[jax.ShapeDtypeStruct(b.shape, b.dtype) for b in bufs],
        in_specs=[any_spec] * nw, out_specs=[any_spec] * nw, input_output_aliases={i: i for i in range(nw)},
        scratch_shapes=[pltpu.SemaphoreType.DMA((nw,)), pltpu.SemaphoreType.DMA((nw,))],
        compiler_params=pltpu.CompilerParams(has_side_effects=True), name="rs_halves")(*bufs)


def _sibling_sum(grad, recv, c_idx, name):
    s, _, r, ccols = grad.shape
    tm = _rows(r, 256)

    def body(c_ref, g_ref, r_ref, o_ref):
        o_ref[...] = (g_ref[...].astype(F32) + r_ref[...].astype(F32)).astype(BF16)

    grid_spec = pltpu.PrefetchScalarGridSpec(
        num_scalar_prefetch=1, grid=(s, r // tm),
        in_specs=[pl.BlockSpec((None, None, tm, ccols), lambda q, i, cr: (q, cr[0], i, 0)),
                  pl.BlockSpec((None, tm, ccols), lambda q, i, cr: (q, i, 0))],
        out_specs=pl.BlockSpec((None, tm, ccols), lambda q, i, cr: (q, i, 0)))
    return _pallas_call(body, out_shape=jax.ShapeDtypeStruct((s, r, ccols), BF16), grid_spec=grid_spec,
                          compiler_params=_cparams(("parallel", "parallel"), 4 * _nbytes((tm, ccols), F32)), name=name)(c_idx, grad, recv)


def _chip_sum(part, recv, pc_idx, name):
    _, r, ccols = part.shape
    tm = _rows(r, 256)

    def body(pc_ref, own_ref, r0_ref, r1_ref, r2_ref, o_ref):
        acc = own_ref[...].astype(F32) + r0_ref[...].astype(F32)
        acc = acc + r1_ref[...].astype(F32)
        o_ref[...] = acc + r2_ref[...].astype(F32)

    def slot(j):
        return pl.BlockSpec((None, tm, ccols), lambda i, pc: (j, i, 0))

    grid_spec = pltpu.PrefetchScalarGridSpec(
        num_scalar_prefetch=1, grid=(r // tm,),
        in_specs=[pl.BlockSpec((None, tm, ccols), lambda i, pc: (pc[0], i, 0)), slot(0), slot(1), slot(2)],
        out_specs=pl.BlockSpec((None, tm, ccols), lambda i, pc: (pc[1], i, 0)))
    return _pallas_call(body, out_shape=jax.ShapeDtypeStruct((2, r, ccols), F32), grid_spec=grid_spec,
                          compiler_params=_cparams(("parallel",), 6 * _nbytes((tm, ccols), F32)), name=name)(pc_idx, part, recv, recv, recv)


def _all_reduce_small(vec):
    _, r, _ = vec.shape

    def body(v_ref, o_ref, parts, send1, recv1, send2, recv2):
        x, y, c, _ = _place()
        me = 4 * x + 2 * y + c
        peers = []
        for k in range(1, N_DEV):
            px, py, pc = (1 - x if k & 4 else x, 1 - y if k & 2 else y, 1 - c if k & 1 else c)
            peers.append(((px, py, pc), 4 * px + 2 * py + pc))
        parts[me] = v_ref[me]
        cps = []
        for k, (peer, peer_id) in enumerate(peers):
            cp = pltpu.make_async_remote_copy(v_ref.at[peer_id], parts.at[me], send1.at[k], recv1.at[k],
                                              device_id=peer, device_id_type=MESH)
            cp.start()
            cps.append(cp)
        for cp in cps:
            cp.wait()
        acc = parts[0]
        for dev in range(1, N_DEV):
            acc = acc + parts[dev]
        o_ref[me] = acc
        cps = []
        for k, (peer, _) in enumerate(peers):
            cp = pltpu.make_async_remote_copy(o_ref.at[me], o_ref.at[me], send2.at[k], recv2.at[k],
                                              device_id=peer, device_id_type=MESH)
            cp.start()
            cps.append(cp)
        for cp in cps:
            cp.wait()

    vm = pl.BlockSpec(memory_space=pltpu.VMEM)
    sems = pltpu.SemaphoreType.DMA((N_DEV - 1,))
    return pl.pallas_call(
        body, out_shape=jax.ShapeDtypeStruct(vec.shape, F32), in_specs=[vm], out_specs=vm,
        scratch_shapes=[pltpu.VMEM((N_DEV, r, LANES), F32), sems, sems, sems, sems],
        compiler_params=pltpu.CompilerParams(vmem_limit_bytes=int(8 * _nbytes((N_DEV, r, LANES), F32))),
        name="all_reduce_small")(vec)


def _adamw(wt, g, m, v, name):
    r, ccols = wt.shape
    tm = _rows(r, max(8, (MIB // (4 * ccols)) // 8 * 8))
    blk = pl.BlockSpec((tm, ccols), lambda i: (i, 0))

    def body(w_ref, g_ref, m_ref, v_ref, go_ref, d_ref, mo_ref, vo_ref):
        gv = g_ref[...]
        go_ref[...] = gv
        mv = ADAM_B1 * m_ref[...] + (1.0 - ADAM_B1) * gv
        vv = ADAM_B2 * v_ref[...] + (1.0 - ADAM_B2) * (gv * gv)
        m_hat = mv / (1.0 - ADAM_B1 ** ADAM_STEP)
        v_hat = vv / (1.0 - ADAM_B2 ** ADAM_STEP)
        d_ref[...] = -ADAM_LR * (m_hat / (jnp.sqrt(v_hat) + ADAM_EPS) + ADAM_WD * w_ref[...])
        mo_ref[...] = mv
        vo_ref[...] = vv

    out = jax.ShapeDtypeStruct((r, ccols), F32)
    return _pallas_call(body, out_shape=(out, out, out, out), grid=(r // tm,), in_specs=[blk] * 4, out_specs=(blk,) * 4,
                          compiler_params=_cparams(("parallel",), 8 * _nbytes((tm, ccols), F32)), name=name)(wt, g, m, v)


def _as_rows(a):
    rows = a.reshape(-1, LANES)
    return jnp.pad(rows, ((0, -rows.shape[0] % 8), (0, 0)))


def kernel(x, ffn1_norm, ffn1_w_gate, ffn1_w_up, ffn1_w_down, mix_norm, w_in, sg_ln_g, sg_ln_b, sg_w, sg_b, w_att_out, w_sg_out, w_out, ffn2_norm, ffn2_w_gate, ffn2_w_up, ffn2_w_down, final_norm, loss_target, m_ffn1_norm, m_ffn1_w_gate, m_ffn1_w_up, m_ffn1_w_down, m_mix_norm, m_w_in, m_sg_ln_g, m_sg_ln_b, m_sg_w, m_sg_b, m_w_att_out, m_w_sg_out, m_w_out, m_ffn2_norm, m_ffn2_w_gate, m_ffn2_w_up, m_ffn2_w_down, m_final_norm, v_ffn1_norm, v_ffn1_w_gate, v_ffn1_w_up, v_ffn1_w_down, v_mix_norm, v_w_in, v_sg_ln_g, v_sg_ln_b, v_sg_w, v_sg_b, v_w_att_out, v_w_sg_out, v_w_out, v_ffn2_norm, v_ffn2_w_gate, v_ffn2_w_up, v_ffn2_w_down, v_final_norm):
    given = dict(locals())
    wts = {n: given[n] for n in WEIGHT_NAMES}
    ms = {n: given["m_" + n] for n in WEIGHT_NAMES}
    vs = {n: given["v_" + n] for n in WEIGHT_NAMES}
    t, d = x.shape[-2], x.shape[-1]
    xc, yc, cc = lax.axis_index("x"), lax.axis_index("y"), lax.axis_index("c")

    shard2d = {n: wts[n].reshape(wts[n].shape[-2:]) for n in BIG_NAMES}
    p_idx = jnp.reshape(2 * xc + yc, (1,)).astype(jnp.int32)
    c_idx = jnp.reshape(cc, (1,)).astype(jnp.int32)
    pc_idx = jnp.stack([2 * xc + yc, cc]).astype(jnp.int32)
    wb = {n: _cast_into_gathered(shard2d[n], p_idx, f"cast_{n}") for n in BIG_NAMES}

    small = {n: wts[n].reshape(-1, wts[n].shape[-1]) for n in SMALL_NAMES}
    small['sg_w'] = wts['sg_w'].reshape(wts['sg_w'].shape[-3:])
    loss, dx, g, gs = _step(x.reshape(t, d), loss_target.reshape(t, d), wb, small, c_idx)
    loss = lax.psum(loss[0, 0], ("x", "y", "c"))

    my_halves = [_chip_sum(*g[n], pc_idx, f"rs_sum2_{n}") for n in BIG_NAMES]
    reduced = _half_exchange(my_halves)
    grads = {n: r.reshape(shard2d[n].shape) for n, r in zip(BIG_NAMES, reduced)}

    def pack(tree):
        rows = jnp.concatenate([_as_rows(tree[n]) for n in SMALL_NAMES], axis=0)
        return jnp.pad(rows, ((0, -rows.shape[0] % (8 * N_DEV)), (0, 0)))

    packed = pack(gs)
    packed = _all_reduce_small(packed.reshape(N_DEV, -1, LANES)).reshape(packed.shape)

    delta, new_m, new_v = {}, {}, {}
    for n in BIG_NAMES:
        shape = wts[n].shape
        out = _adamw(shard2d[n], grads[n], ms[n].reshape(shard2d[n].shape), vs[n].reshape(shard2d[n].shape), f"adamw_{n}")
        grads[n], delta[n], new_m[n], new_v[n] = (a.reshape(shape) for a in out)

    small_out = _adamw(pack(wts), packed, pack(ms), pack(vs), "adamw_small")
    row = 0
    for n in SMALL_NAMES:
        shape = wts[n].shape
        sz = wts[n].size // LANES
        grads[n], delta[n], new_m[n], new_v[n] = (a[row:row + sz].reshape(shape) for a in small_out)
        row += sz + -sz % 8

    return (loss, dx.reshape(x.shape), *[grads[n] for n in WEIGHT_NAMES], *[delta[n] for n in WEIGHT_NAMES],
            *[new_m[n] for n in WEIGHT_NAMES], *[new_v[n] for n in WEIGHT_NAMES])
```

```python
import functools

import jax
import jax.numpy as jnp
from jax import lax
from jax.experimental import pallas as pl
from jax.experimental.pallas import tpu as pltpu

F32 = jnp.float32
BF16 = jnp.bfloat16
MESH = pl.DeviceIdType.MESH

NORM_EPS = 1e-6
LN_EPS = 1e-5
HEAD_DIM = 128
HEADS_PER_GROUP = 4
GROUP_WIDTH = HEADS_PER_GROUP * HEAD_DIM
DILATIONS = (1, 4, 16)
N_GROUPS = len(DILATIONS)
ATT_BLOCK = 128
ROPE_DIM = HEAD_DIM // 4
ROPE_THETA = 500000.0
SG_CHUNK = 128
SG_GROUPS = 12
SG_GROUP_DIM = 128
MASKED = -1e30

ADAM_LR = 0.001
ADAM_B1 = 0.9
ADAM_B2 = 0.999
ADAM_EPS = 1e-08
ADAM_WD = 0.01
ADAM_STEP = 10

N_CHIPS = 4
N_DEV = 8
LANES = 128
MIB = 2 ** 20
VMEM_BYTES_V7X = 64 * MIB

WEIGHT_NAMES = ['ffn1_norm', 'ffn1_w_gate', 'ffn1_w_up', 'ffn1_w_down', 'mix_norm', 'w_in', 'sg_ln_g', 'sg_ln_b',
                'sg_w', 'sg_b', 'w_att_out', 'w_sg_out', 'w_out', 'ffn2_norm', 'ffn2_w_gate', 'ffn2_w_up',
                'ffn2_w_down', 'final_norm']
BIG = [('ffn1_w_gate', 1), ('ffn1_w_up', 1), ('ffn1_w_down', 0), ('w_in', 1), ('w_att_out', 1), ('w_sg_out', 1),
       ('w_out', 0), ('ffn2_w_gate', 1), ('ffn2_w_up', 1), ('ffn2_w_down', 0)]
BIG_NAMES = [n for n, _ in BIG]
SMALL_NAMES = [n for n in WEIGHT_NAMES if n not in BIG_NAMES]


def _nbytes(shape, dtype):
    n = jnp.dtype(dtype).itemsize
    for s in shape:
        if s is not None:
            n *= s
    return n


def _pallas_call(*args, **kw):
    kw['out_shape'] = jax.tree.map(lambda s: pltpu.HBM(s.shape, s.dtype), kw['out_shape'])
    call = pl.pallas_call(*args, **kw)

    def pinned(*operands):
        return call(*[o if jnp.issubdtype(o.dtype, jnp.integer) else pltpu.with_memory_space_constraint(o, pltpu.HBM)
                      for o in operands])

    return pinned


def _cparams(sem, block_bytes, **kw):
    limit = int(min(max(3 * block_bytes, 32 * MIB), VMEM_BYTES_V7X - 8 * MIB))
    return pltpu.CompilerParams(dimension_semantics=sem, vmem_limit_bytes=limit, **kw)


def _tile(dim, cap):
    best = None
    for t in range(LANES, min(dim, cap) + 1, LANES):
        if dim % t == 0:
            best = t
    if best is None:
        assert dim <= cap, (dim, cap)
        return dim
    return best


def _rows(dim, cap):
    best = None
    for t in range(8, min(dim, cap) + 1, 8):
        if dim % t == 0:
            best = t
    assert best is not None, (dim, cap)
    return best


def _place():
    x, y, c = lax.axis_index("x"), lax.axis_index("y"), lax.axis_index("c")
    others = [(1 - x, y), (x, 1 - y), (1 - x, 1 - y)]
    return x, y, c, others


class _Rider:
    def __init__(self, operands, out_shapes, aliases, sems, start, finish):
        self.operands = operands
        self.out_shapes = out_shapes
        self.aliases = aliases
        self.sems = sems
        self.start = start
        self.finish = finish


def _run(body, *, name, grid, in_specs, out_specs, out_shape, scratch_shapes, operands, block_bytes, rider=None):
    if rider is None:
        sem = ("parallel",) * (len(grid) - 1) + ("arbitrary",)
        return _pallas_call(body, out_shape=out_shape, grid=grid, in_specs=in_specs, out_specs=out_specs,
                              scratch_shapes=scratch_shapes, compiler_params=_cparams(sem, block_bytes), name=name)(*operands)
    n_in, n_out, n_scr = len(operands), len(out_shape), len(scratch_shapes)
    r_in, r_out = len(rider.operands), len(rider.out_shapes)
    any_spec = pl.BlockSpec(memory_space=pl.ANY)

    def wrapped(*refs):
        ins, refs = refs[:n_in], refs[n_in:]
        r_ins, refs = refs[:r_in], refs[r_in:]
        outs, refs = refs[:n_out], refs[n_out:]
        r_outs, refs = refs[:r_out], refs[r_out:]
        scr, sems = refs[:n_scr], refs[n_scr:]
        if not grid:
            rider.start(r_ins, r_outs, sems)
            rider.finish(r_ins, r_outs, sems)
            return
        ids = [pl.program_id(a) for a in range(len(grid))]
        first = functools.reduce(jnp.logical_and, [i == 0 for i in ids])
        last = functools.reduce(jnp.logical_and, [i == g - 1 for i, g in zip(ids, grid)])

        @pl.when(first)
        def _():
            rider.start(r_ins, r_outs, sems)

        body(*ins, *outs, *scr)

        @pl.when(last)
        def _():
            rider.finish(r_ins, r_outs, sems)

    results = _pallas_call(
        wrapped, out_shape=list(out_shape) + list(rider.out_shapes), grid=grid,
        in_specs=list(in_specs) + [any_spec] * r_in, out_specs=list(out_specs) + [any_spec] * r_out,
        scratch_shapes=list(scratch_shapes) + list(rider.sems),
        input_output_aliases={n_in + k: n_out + v for k, v in rider.aliases.items()},
        compiler_params=_cparams(("arbitrary",) * len(grid) if grid else None, block_bytes, has_side_effects=True),
        name=name)(*operands, *rider.operands)
    return results[:n_out], results[n_out:]


def _exchange(rider, name):
    return _run(None, name=name, grid=(), in_specs=[], out_specs=[], out_shape=[], scratch_shapes=[], operands=[],
                block_bytes=0, rider=rider)[1]


def _gather_rider(items):
    bufs, index = [], []
    for b, r0, r1 in items:
        if not any(b is q for q in bufs):
            bufs.append(b)
        index.append(([k for k, q in enumerate(bufs) if q is b][0], r0, r1))
    n = len(index)

    def piece(refs, k, chip, half):
        bi, r0, r1 = index[k]
        return refs[bi].at[chip, half, pl.ds(r0, r1 - r0)]

    def copy(ref, sem_pair, k, j, to):
        return pltpu.make_async_remote_copy(ref, ref, sem_pair[0].at[k, j], sem_pair[1].at[k, j], device_id=to, device_id_type=MESH)

    def start(r_ins, buf, sems):
        x, y, c, others = _place()
        for k in range(n):
            for j, (ox, oy) in enumerate(others):
                copy(piece(buf, k, 2 * x + y, c), sems[:2], k, j, (ox, oy, c)).start()

    def finish(r_ins, buf, sems):
        x, y, c, others = _place()
        for k in range(n):
            for j, (ox, oy) in enumerate(others):
                got = piece(buf, k, 2 * ox + oy, c)
                copy(got, sems[:2], k, j, (ox, oy, c)).wait_recv()
                copy(got, sems[2:], k, j, (x, y, 1 - c)).start()
        for k in range(n):
            for j, (ox, oy) in enumerate(others):
                copy(piece(buf, k, 2 * ox + oy, 1 - c), sems[2:], k, j, (x, y, 1 - c)).wait_recv()
        for k in range(n):
            for j, (ox, oy) in enumerate(others):
                copy(piece(buf, k, 2 * x + y, c), sems[:2], k, j, (ox, oy, c)).wait_send()
                copy(piece(buf, k, 2 * ox + oy, c), sems[2:], k, j, (x, y, 1 - c)).wait_send()

    return _Rider(bufs, [jax.ShapeDtypeStruct(b.shape, b.dtype) for b in bufs], {i: i for i in range(len(bufs))},
                  [pltpu.SemaphoreType.DMA((n, 3))] * 4, start, finish)


def _scatter_rider(parts):
    n = len(parts)

    def copy(src, dst, sems, i, j, to):
        return pltpu.make_async_remote_copy(src, dst, sems[0].at[i, j], sems[1].at[i, j], device_id=to, device_id_type=MESH)

    def start(src, dst, sems):
        x, y, c, others = _place()
        for i in range(n):
            for j, (ox, oy) in enumerate(others):
                copy(src[i].at[2 * ox + oy], dst[i].at[j], sems, i, j, (ox, oy, c)).start()

    def finish(src, dst, sems):
        x, y, c, others = _place()
        for i in range(n):
            for j, (ox, oy) in enumerate(others):
                copy(src[i].at[2 * ox + oy], dst[i].at[j], sems, i, j, (ox, oy, c)).wait()

    return _Rider(parts, [jax.ShapeDtypeStruct((3,) + p.shape[1:], p.dtype) for p in parts], {},
                  [pltpu.SemaphoreType.DMA((n, 3))] * 2, start, finish)


def _matmul(pairs, mode, out_dtype, name, *, scale=1.0, residual=None, b3=False, out3=0, caps=(1024, 1024, 512), rider=None):
    a0, b0 = pairs[0]
    if mode == 'nn':
        m, k = a0.shape
        n = b0.shape[0] * b0.shape[2] if b3 else b0.shape[1]
    elif mode == 'nt':
        m = a0.shape[0]
        n, k = (b0.shape[1], b0.shape[0] * b0.shape[2]) if b3 else b0.shape
    else:
        k, m = a0.shape
        n = b0.shape[1]
    tm = _tile(m, caps[0])
    tn = _tile(n, caps[1])
    tk = _tile(k, caps[2])
    if b3 and mode == 'nn':
        tn = b0.shape[2]
    if b3 and mode == 'nt':
        tk = b0.shape[2]
    if out3:
        tn = n // out3
    nk = k // tk
    if mode == 'tn':
        a_spec = pl.BlockSpec((tk, tm), lambda i, j, kk: (kk, i))
        b_spec = pl.BlockSpec((tk, tn), lambda i, j, kk: (kk, j))
        dims = ((0,), (0,))
    elif mode == 'nn':
        a_spec = pl.BlockSpec((tm, tk), lambda i, j, kk: (i, kk))
        b_spec = (pl.BlockSpec((None, tk, tn), lambda i, j, kk: (j, kk, 0)) if b3
                  else pl.BlockSpec((tk, tn), lambda i, j, kk: (kk, j)))
        dims = ((1,), (0,))
    else:
        a_spec = pl.BlockSpec((tm, tk), lambda i, j, kk: (i, kk))
        b_spec = (pl.BlockSpec((None, tn, tk), lambda i, j, kk: (kk, j, 0)) if b3
                  else pl.BlockSpec((tn, tk), lambda i, j, kk: (j, kk)))
        dims = ((1,), (1,))
    in_specs, operands = [], []
    for a, b in pairs:
        in_specs += [a_spec, b_spec]
        operands += [a, b]
    block_bytes = len(pairs) * (_nbytes((tm, tk), a0.dtype) + _nbytes((tk, tn), b0.dtype))
    if residual is not None:
        in_specs.append(pl.BlockSpec((tm, tn), lambda i, j, kk: (i, j)))
        operands.append(residual)
        block_bytes += _nbytes((tm, tn), F32)
    if out3:
        out_spec = pl.BlockSpec((None, tm, tn), lambda i, j, kk: (j, i, 0))
        out_shape = jax.ShapeDtypeStruct((out3, m, tn), out_dtype)
    else:
        out_spec = pl.BlockSpec((tm, tn), lambda i, j, kk: (i, j))
        out_shape = jax.ShapeDtypeStruct((m, n), out_dtype)
    block_bytes += _nbytes((tm, tn), out_dtype) + _nbytes((tm, tn), F32)
    n_pairs = len(pairs)
    has_res = residual is not None

    def body(*refs):
        o_ref, acc = refs[-2], refs[-1]
        kk = pl.program_id(2)

        def product():
            part = None
            for p in range(n_pairs):
                d = lax.dot_general(refs[2 * p][...].astype(BF16), refs[2 * p + 1][...].astype(BF16),
                                    (dims, ((), ())), preferred_element_type=F32)
                part = d if part is None else part + d
            return part

        def finish(r):
            if scale != 1.0:
                r = r * scale
            if has_res:
                r = refs[2 * n_pairs][...] + r
            o_ref[...] = r.astype(out_dtype)

        if nk == 1:
            finish(product())
            return

        @pl.when(kk == 0)
        def _():
            acc[...] = product()

        if nk > 2:
            @pl.when(jnp.logical_and(kk > 0, kk < nk - 1))
            def _():
                acc[...] += product()

        @pl.when(kk == nk - 1)
        def _():
            finish(acc[...] + product())

    res = _run(body, name=name, grid=(m // tm, n // tn, nk), in_specs=in_specs, out_specs=[out_spec], out_shape=[out_shape],
               scratch_shapes=[pltpu.VMEM((tm, tn), F32)], operands=operands, block_bytes=block_bytes, rider=rider)
    return res[0] if rider is None else (res[0][0], res[1])


def _rmsnorm_fwd(x, g, name):
    t, d = x.shape
    tm = _rows(t, 512)

    def body(x_ref, g_ref, o_ref):
        xv = x_ref[...]
        r = lax.rsqrt(jnp.mean(xv * xv, axis=1, keepdims=True) + NORM_EPS)
        o_ref[...] = (xv * r * g_ref[...]).astype(BF16)

    row = pl.BlockSpec((tm, d), lambda i: (i, 0))
    return _pallas_call(
        body, out_shape=jax.ShapeDtypeStruct((t, d), BF16), grid=(t // tm,),
        in_specs=[row, pl.BlockSpec((1, d), lambda i: (0, 0))], out_specs=row,
        compiler_params=_cparams(("parallel",), 2 * _nbytes((tm, d), F32)), name=name)(x, g)


def _rms_grad(xv, g, dn, d):
    r = lax.rsqrt(jnp.mean(xv * xv, axis=1, keepdims=True) + NORM_EPS)
    u = dn * g
    s = jnp.sum(xv * u, axis=1, keepdims=True)
    dx = r * u - xv * (r * r * r) * (s * (1.0 / d))
    return dx, dn * xv * r


def _rmsnorm_bwd(x, g, dn, dres, name):
    t, d = x.shape
    tm = _rows(t, 256)

    def body(x_ref, g_ref, dn_ref, dres_ref, dx_ref, dxb_ref, dg_ref):
        dx, dg_rows = _rms_grad(x_ref[...], g_ref[...], dn_ref[...].astype(F32), d)
        dx = dres_ref[...] + dx
        dx_ref[...] = dx
        dxb_ref[...] = dx.astype(BF16)

        @pl.when(pl.program_id(0) == 0)
        def _():
            dg_ref[...] = jnp.zeros_like(dg_ref)

        dg_ref[...] += jnp.sum(dg_rows, axis=0, keepdims=True)

    row = pl.BlockSpec((tm, d), lambda i: (i, 0))
    vec = pl.BlockSpec((1, d), lambda i: (0, 0))
    return _pallas_call(
        body, out_shape=(jax.ShapeDtypeStruct((t, d), F32), jax.ShapeDtypeStruct((t, d), BF16), jax.ShapeDtypeStruct((1, d), F32)),
        grid=(t // tm,), in_specs=[row, vec, row, row], out_specs=(row, row, vec),
        compiler_params=_cparams(("arbitrary",), 5 * _nbytes((tm, d), F32)), name=name)(x, g, dn, dres)


def _final_loss(x, g, target, name):
    t, d = x.shape
    tm = _rows(t, 256)

    def body(x_ref, g_ref, t_ref, loss_ref, dx_ref, dxb_ref, dg_ref):
        xv, gv = x_ref[...], g_ref[...]
        r = lax.rsqrt(jnp.mean(xv * xv, axis=1, keepdims=True) + NORM_EPS)
        err = xv * r * gv - t_ref[...]
        dx, dg_rows = _rms_grad(xv, gv, err * (1.0 / d), d)
        dx_ref[...] = dx
        dxb_ref[...] = dx.astype(BF16)

        @pl.when(pl.program_id(0) == 0)
        def _():
            dg_ref[...] = jnp.zeros_like(dg_ref)
            loss_ref[...] = jnp.zeros_like(loss_ref)

        dg_ref[...] += jnp.sum(dg_rows, axis=0, keepdims=True)
        row_loss = jnp.sum(err * err, axis=1, keepdims=True) * (0.5 / d)
        loss_ref[...] += jnp.sum(row_loss, axis=0, keepdims=True)

    row = pl.BlockSpec((tm, d), lambda i: (i, 0))
    vec = pl.BlockSpec((1, d), lambda i: (0, 0))
    return _pallas_call(
        body, out_shape=(jax.ShapeDtypeStruct((1, 1), F32), jax.ShapeDtypeStruct((t, d), F32),
                         jax.ShapeDtypeStruct((t, d), BF16), jax.ShapeDtypeStruct((1, d), F32)),
        grid=(t // tm,), in_specs=[row, vec, row], out_specs=(pl.BlockSpec((1, 1), lambda i: (0, 0)), row, row, vec),
        compiler_params=_cparams(("arbitrary",), 4 * _nbytes((tm, d), F32)), name=name)(x, g, target)


def _ffn_up(n, wg, wu, name, rider=None):
    t, d = n.shape
    s, _, f = wg.shape
    tm, tk = _tile(t, 1024), _tile(d, 1024)
    nk = d // tk

    def body(n_ref, wg_ref, wu_ref, a_ref, b_ref, h_ref, acc_g, acc_u):
        kk = pl.program_id(2)

        def products():
            nv = n_ref[...]
            return jnp.dot(nv, wg_ref[...], preferred_element_type=F32), jnp.dot(nv, wu_ref[...], preferred_element_type=F32)

        def finish(a, b):
            a_ref[...] = a.astype(BF16)
            b_ref[...] = b.astype(BF16)
            h_ref[...] = (a * jax.nn.sigmoid(a) * b).astype(BF16)

        if nk == 1:
            finish(*products())
            return

        @pl.when(kk == 0)
        def _():
            acc_g[...], acc_u[...] = products()

        if nk > 2:
            @pl.when(jnp.logical_and(kk > 0, kk < nk - 1))
            def _():
                pg, pu = products()
                acc_g[...] += pg
                acc_u[...] += pu

        @pl.when(kk == nk - 1)
        def _():
            pg, pu = products()
            finish(acc_g[...] + pg, acc_u[...] + pu)

    w_spec = pl.BlockSpec((None, tk, f), lambda i, j, kk: (j, kk, 0))
    o_spec = pl.BlockSpec((tm, f), lambda i, j, kk: (i, j))
    out = jax.ShapeDtypeStruct((t, s * f), BF16)
    block_bytes = _nbytes((tm, tk), BF16) + 2 * _nbytes((tk, f), BF16) + 3 * _nbytes((tm, f), BF16) + 2 * _nbytes((tm, f), F32)
    return _run(body, name=name, grid=(t // tm, s, nk),
                in_specs=[pl.BlockSpec((tm, tk), lambda i, j, kk: (i, kk)), w_spec, w_spec], out_specs=[o_spec, o_spec, o_spec],
                out_shape=[out, out, out], scratch_shapes=[pltpu.VMEM((tm, f), F32), pltpu.VMEM((tm, f), F32)],
                operands=[n, wg, wu], block_bytes=block_bytes, rider=rider)


def _ffn_bwd_act(dx, wd, a, b, name):
    t, d = dx.shape
    f = wd.shape[0]
    tm, tn, tk = _tile(t, 1024), _tile(f, 1536), _tile(d, 1024)
    nk = d // tk

    def body(dx_ref, wd_ref, a_ref, b_ref, da_ref, db_ref, acc):
        kk = pl.program_id(2)

        def product():
            return lax.dot_general(dx_ref[...], wd_ref[...], (((1,), (1,)), ((), ())), preferred_element_type=F32)

        def finish(r):
            dh = 0.5 * r
            av, bv = a_ref[...].astype(F32), b_ref[...].astype(F32)
            sg = jax.nn.sigmoid(av)
            da_ref[...] = (dh * bv * (sg * (1.0 + av * (1.0 - sg)))).astype(BF16)
            db_ref[...] = (dh * (av * sg)).astype(BF16)

        if nk == 1:
            finish(product())
            return

        @pl.when(kk == 0)
        def _():
            acc[...] = product()

        if nk > 2:
            @pl.when(jnp.logical_and(kk > 0, kk < nk - 1))
            def _():
                acc[...] += product()

        @pl.when(kk == nk - 1)
        def _():
            finish(acc[...] + product())

    act = pl.BlockSpec((tm, tn), lambda i, j, kk: (i, j))
    out = jax.ShapeDtypeStruct((t, f), BF16)
    block_bytes = _nbytes((tm, tk), BF16) + _nbytes((tn, tk), BF16) + 4 * _nbytes((tm, tn), BF16) + _nbytes((tm, tn), F32)
    return _pallas_call(
        body, out_shape=(out, out), grid=(t // tm, f // tn, nk),
        in_specs=[pl.BlockSpec((tm, tk), lambda i, j, kk: (i, kk)), pl.BlockSpec((tn, tk), lambda i, j, kk: (j, kk)),
                  act, act],
        out_specs=(act, act), scratch_shapes=[pltpu.VMEM((tm, tn), F32)],
        compiler_params=_cparams(("parallel", "parallel", "arbitrary"), block_bytes), name=name)(dx, wd, a, b)


AXIS = dict(BIG)


def _full(wb, n):
    _, _, r, ccols = wb[n].shape
    return wb[n].reshape(N_CHIPS, 2 * r, ccols) if AXIS[n] == 1 else wb[n].reshape(N_CHIPS * 2 * r, ccols)


def _gather(wb, specs):
    items, names = [], []
    for s in specs:
        n, r0, r1 = (s, 0, wb[s].shape[2]) if isinstance(s, str) else s
        items.append((wb[n], r0, r1))
        if n not in names:
            names.append(n)
    return _gather_rider(items), names


def _landed(wb, names, results):
    for n, r in zip(names, results):
        wb[n] = r


def _reduce_first(grads, names, wb, c_idx):
    g4 = [g.reshape(wb[n].shape) for g, n in zip(grads, names)]
    from_sibling = _sibling_exchange(g4, "rs_sibling_" + names[0])
    return [_sibling_sum(a, b, c_idx, f"rs_sum1_{n}") for a, b, n in zip(g4, from_sibling, names)]


def _ffn_forward(x, gain, wb, tag, up_specs, down_specs):
    n = _rmsnorm_fwd(x, gain, f"{tag}_norm")
    rider, names = _gather(wb, up_specs)
    (a, b, h), got = _ffn_up(n, _full(wb, f"{tag}_w_gate"), _full(wb, f"{tag}_w_up"), f"{tag}_up", rider=rider)
    _landed(wb, names, got)
    down = dict(scale=0.5, residual=x, caps=(1024, 1024, 1536))
    if down_specs:
        rider, names = _gather(wb, down_specs)
        x_next, got = _matmul([(h, _full(wb, f"{tag}_w_down"))], 'nn', F32, f"{tag}_down", rider=rider, **down)
        _landed(wb, names, got)
    else:
        x_next = _matmul([(h, _full(wb, f"{tag}_w_down"))], 'nn', F32, f"{tag}_down", **down)
    return x_next, (n, a, b, h)


def _ffn_backward(x, gain, wb, saved, dx_next, dx_next_b, c_idx, tag):
    n, a, b, h = saved
    wg, wu, wd = (f"{tag}_w_gate", f"{tag}_w_up", f"{tag}_w_down")
    da, db = _ffn_bwd_act(dx_next_b, _full(wb, wd), a, b, f"{tag}_bwd_act")
    g_wd = _matmul([(h, dx_next_b)], 'tn', BF16, f"{tag}_dwd", scale=0.5, caps=(1536, 1024, 1024))
    (p_wd,) = _reduce_first([g_wd], [wd], wb, c_idx)
    g_wg, (r_wd,) = _matmul([(n, da)], 'tn', BF16, f"{tag}_dwg", out3=N_CHIPS, caps=(1024, 1024, 1024), rider=_scatter_rider([p_wd]))
    g_wu = _matmul([(n, db)], 'tn', BF16, f"{tag}_dwu", out3=N_CHIPS, caps=(1024, 1024, 1024))
    p_wg, p_wu = _reduce_first([g_wg, g_wu], [wg, wu], wb, c_idx)
    dn, (r_wg, r_wu) = _matmul([(da, _full(wb, wg)), (db, _full(wb, wu))], 'nt', F32, f"{tag}_dn", b3=True,
                               rider=_scatter_rider([p_wg, p_wu]))
    dx, dx_b, g_gain = _rmsnorm_bwd(x, gain, dn, dx_next, f"{tag}_norm_bwd")
    return dx, dx_b, g_gain, {wg: (p_wg, r_wg), wu: (p_wu, r_wu), wd: (p_wd, r_wd)}


def _rope_tables(seq):
    half = ROPE_DIM // 2
    inv_freq = ROPE_THETA ** (-jnp.arange(0, ROPE_DIM, 2, dtype=F32) / ROPE_DIM)
    ang = jnp.arange(seq).astype(F32)[:, None] * inv_freq[None, :]
    cos, sin = jnp.cos(ang), jnp.sin(ang)
    zeros = lambda w: jnp.zeros((seq, w), F32)
    c = jnp.concatenate([cos, cos, jnp.ones((seq, HEAD_DIM - ROPE_DIM), F32)], axis=1)
    s_up = jnp.concatenate([-sin, zeros(HEAD_DIM - half)], axis=1)
    s_dn = jnp.concatenate([zeros(half), sin, zeros(HEAD_DIM - ROPE_DIM)], axis=1)
    return c, s_up, s_dn


def _rotate(xv, cv, uv, dv):
    half = ROPE_DIM // 2
    return xv * cv + pltpu.roll(xv, HEAD_DIM - half, 1) * uv + pltpu.roll(xv, half, 1) * dv


def _stage(tm):
    return pltpu.VMEM((HEADS_PER_GROUP, tm, HEAD_DIM), F32)


def _to_groups(stage, o_ref, dil):
    rows = stage.shape[1] // dil
    for r in range(dil):
        for h in range(HEADS_PER_GROUP):
            col = r * GROUP_WIDTH + h * HEAD_DIM
            o_ref[:, col:col + HEAD_DIM] = stage[h, pl.ds(r, rows, stride=dil), :].astype(o_ref.dtype)


def _from_groups(g_ref, stage, dil):
    rows = stage.shape[1] // dil
    for r in range(dil):
        for h in range(HEADS_PER_GROUP):
            col = r * GROUP_WIDTH + h * HEAD_DIM
            stage[h, pl.ds(r, rows, stride=dil), :] = g_ref[:, col:col + HEAD_DIM].astype(F32)


def _group_spec(tm, dil):
    return pl.BlockSpec((tm // dil, dil * GROUP_WIDTH), lambda i: (i, 0))


def _group_shape(t, dil, dtype):
    return jax.ShapeDtypeStruct((t // dil, dil * GROUP_WIDTH), dtype)


def _rope_fwd(proj, tables, name):
    t = proj.shape[0]
    tm = _rows(t, 512)
    att_w = N_GROUPS * GROUP_WIDTH
    dilated = [(gi, dil) for gi, dil in enumerate(DILATIONS) if dil > 1]

    def body(x_ref, c_ref, up_ref, dn_ref, qk0_ref, *rest):
        outs, stage = rest[:-1], rest[-1]
        cv, uv, dv = c_ref[...], up_ref[...], dn_ref[...]
        for part in range(2):
            for gi, dil in enumerate(DILATIONS):
                for h in range(HEADS_PER_GROUP):
                    col = part * att_w + gi * GROUP_WIDTH + h * HEAD_DIM
                    y = _rotate(x_ref[:, col:col + HEAD_DIM].astype(F32), cv, uv, dv)
                    if dil == 1:
                        qk0_ref[:, part * GROUP_WIDTH + h * HEAD_DIM:part * GROUP_WIDTH + (h + 1) * HEAD_DIM] = y.astype(BF16)
                    else:
                        stage[h] = y
                if dil > 1:
                    _to_groups(stage, outs[3 * dilated.index((gi, dil)) + part], dil)
        for n, (gi, dil) in enumerate(dilated):
            col = 2 * att_w + gi * GROUP_WIDTH
            for h in range(HEADS_PER_GROUP):
                stage[h] = x_ref[:, col + h * HEAD_DIM:col + (h + 1) * HEAD_DIM].astype(F32)
            _to_groups(stage, outs[3 * n + 2], dil)

    tab = pl.BlockSpec((tm, HEAD_DIM), lambda i: (i, 0))
    out_shape = [jax.ShapeDtypeStruct((t, 2 * GROUP_WIDTH), BF16)]
    out_specs = [pl.BlockSpec((tm, 2 * GROUP_WIDTH), lambda i: (i, 0))]
    for _, dil in dilated:
        out_shape += [_group_shape(t, dil, BF16)] * 3
        out_specs += [_group_spec(tm, dil)] * 3
    res = _pallas_call(
        body, out_shape=out_shape, grid=(t // tm,),
        in_specs=[pl.BlockSpec((tm, 3 * att_w), lambda i: (i, 0)), tab, tab, tab], out_specs=out_specs,
        scratch_shapes=[_stage(tm)],
        compiler_params=_cparams(("parallel",), 4 * _nbytes((tm, 3 * att_w), BF16)), name=name)(proj, *tables)
    return res[0], [tuple(res[1 + 3 * n:4 + 3 * n]) for n in range(len(dilated))]


def _rope_bwd(dq0, dk0, dv0, grouped, tables, name):
    t = dq0.shape[0]
    tm = _rows(t, 512)
    att_w = N_GROUPS * GROUP_WIDTH
    dilated = [(gi, dil) for gi, dil in enumerate(DILATIONS) if dil > 1]
    c, s_up, s_dn = tables

    def body(c_ref, up_ref, dn_ref, dq0_ref, dk0_ref, dv0_ref, *rest):
        g_refs, o_ref, stage = rest[:-2], rest[-2], rest[-1]
        cv, uv, dv = c_ref[...], -up_ref[...], -dn_ref[...]
        for part, first in enumerate((dq0_ref, dk0_ref)):
            for gi, dil in enumerate(DILATIONS):
                if dil > 1:
                    _from_groups(g_refs[3 * dilated.index((gi, dil)) + part], stage, dil)
                for h in range(HEADS_PER_GROUP):
                    sl = slice(h * HEAD_DIM, (h + 1) * HEAD_DIM)
                    xv = first[:, sl].astype(F32) if dil == 1 else stage[h]
                    col = part * att_w + gi * GROUP_WIDTH + h * HEAD_DIM
                    o_ref[:, col:col + HEAD_DIM] = _rotate(xv, cv, uv, dv).astype(BF16)
        for gi, dil in enumerate(DILATIONS):
            col = 2 * att_w + gi * GROUP_WIDTH
            if dil == 1:
                o_ref[:, col:col + GROUP_WIDTH] = dv0_ref[...]
            else:
                _from_groups(g_refs[3 * dilated.index((gi, dil)) + 2], stage, dil)
                for h in range(HEADS_PER_GROUP):
                    o_ref[:, col + h * HEAD_DIM:col + (h + 1) * HEAD_DIM] = stage[h].astype(BF16)

    tab = pl.BlockSpec((tm, HEAD_DIM), lambda i: (i, 0))
    nat = pl.BlockSpec((tm, GROUP_WIDTH), lambda i: (i, 0))
    in_specs, operands = [tab, tab, tab, nat, nat, nat], [c, s_up, s_dn, dq0, dk0, dv0]
    for (_, dil), arrs in zip(dilated, grouped):
        in_specs += [_group_spec(tm, dil)] * 3
        operands += list(arrs)
    return _pallas_call(
        body, out_shape=jax.ShapeDtypeStruct((t, 3 * att_w), BF16), grid=(t // tm,), in_specs=in_specs,
        out_specs=pl.BlockSpec((tm, 3 * att_w), lambda i: (i, 0)), scratch_shapes=[_stage(tm)],
        compiler_params=_cparams(("parallel",), 4 * _nbytes((tm, 3 * att_w), BF16)), name=name)(*operands)


def _regroup(arrs, name):
    t = arrs[0].shape[0]
    tm = _rows(t, 512)
    dilated = [dil for dil in DILATIONS if dil > 1]
    n_in = len(arrs)

    def body(*refs):
        ins, outs, stage = refs[:n_in], refs[n_in:-1], refs[-1]
        for j, x_ref in enumerate(ins):
            for h in range(HEADS_PER_GROUP):
                stage[h] = x_ref[:, h * HEAD_DIM:(h + 1) * HEAD_DIM]
            for n, dil in enumerate(dilated):
                _to_groups(stage, outs[n * n_in + j], dil)

    nat = pl.BlockSpec((tm, GROUP_WIDTH), lambda i: (i, 0))
    res = _pallas_call(
        body, out_shape=[_group_shape(t, dil, F32) for dil in dilated for _ in arrs], grid=(t // tm,),
        in_specs=[nat] * n_in, out_specs=[_group_spec(tm, dil) for dil in dilated for _ in arrs], scratch_shapes=[_stage(tm)],
        compiler_params=_cparams(("parallel",), 3 * n_in * _nbytes((tm, GROUP_WIDTH), F32)), name=name)(*arrs)
    return [tuple(res[n * n_in:(n + 1) * n_in]) for n in range(len(dilated))]


def _query_mask(has_prev):
    qi = lax.broadcasted_iota(jnp.int32, (ATT_BLOCK, 2 * ATT_BLOCK), 0)
    col = lax.broadcasted_iota(jnp.int32, (ATT_BLOCK, 2 * ATT_BLOCK), 1)
    prev = jnp.logical_and(jnp.logical_and(col < ATT_BLOCK, col >= qi), has_prev)
    return jnp.logical_or(prev, jnp.logical_and(col >= ATT_BLOCK, col - ATT_BLOCK <= qi))


def _key_mask(has_next):
    row = lax.broadcasted_iota(jnp.int32, (2 * ATT_BLOCK, ATT_BLOCK), 0)
    kj = lax.broadcasted_iota(jnp.int32, (2 * ATT_BLOCK, ATT_BLOCK), 1)
    nxt = jnp.logical_and(jnp.logical_and(row >= ATT_BLOCK, kj >= row - ATT_BLOCK), has_next)
    return jnp.logical_or(nxt, jnp.logical_and(row < ATT_BLOCK, kj <= row))


def _scores(q, k):
    return lax.dot_general(q, k, (((1,), (1,)), ((), ())), preferred_element_type=F32) * (HEAD_DIM ** -0.5)


def _att_fwd(q, k, v, offs, dil, name):
    qo, ko, vo = offs
    length = q.shape[0]
    nb = length // ATT_BLOCK

    def body(q_ref, kp_ref, kc_ref, vp_ref, vc_ref, o_ref, lse_ref):
        mask = _query_mask(pl.program_id(1) > 0)
        heads = [slice(h * HEAD_DIM, (h + 1) * HEAD_DIM) for h in range(HEADS_PER_GROUP)]
        ks = [jnp.concatenate([kp_ref[:, sl], kc_ref[:, sl]], axis=0) for sl in heads]
        vs = [jnp.concatenate([vp_ref[:, sl], vc_ref[:, sl]], axis=0) for sl in heads]
        ss = [jnp.where(mask, _scores(q_ref[:, sl], kv), MASKED) for sl, kv in zip(heads, ks)]
        ms = [jnp.max(s, axis=1, keepdims=True) for s in ss]
        ps = [jnp.exp(s - m) for s, m in zip(ss, ms)]
        ls = [jnp.sum(p, axis=1, keepdims=True) for p in ps]
        accs = [jnp.dot(p.astype(BF16), vv, preferred_element_type=F32) for p, vv in zip(ps, vs)]
        for sl, acc, m, l in zip(heads, accs, ms, ls):
            o_ref[:, sl] = acc / l
            lse_ref[:, sl] = jnp.broadcast_to(m + jnp.log(l), (ATT_BLOCK, HEAD_DIM))

    def spec(off, prev):
        if prev:
            return pl.BlockSpec((ATT_BLOCK, GROUP_WIDTH), lambda r, n: (jnp.maximum(n - 1, 0), off + r))
        return pl.BlockSpec((ATT_BLOCK, GROUP_WIDTH), lambda r, n: (n, off + r))

    out = jax.ShapeDtypeStruct((length, dil * GROUP_WIDTH), F32)
    o_spec = pl.BlockSpec((ATT_BLOCK, GROUP_WIDTH), lambda r, n: (n, r))
    return _pallas_call(
        body, out_shape=(out, out), grid=(dil, nb),
        in_specs=[spec(qo, False), spec(ko, True), spec(ko, False), spec(vo, True), spec(vo, False)],
        out_specs=(o_spec, o_spec),
        compiler_params=_cparams(("parallel", "parallel"), 8 * _nbytes((ATT_BLOCK, GROUP_WIDTH), F32)), name=name)(q, k, k, v, v)


def _att_combine(outs, lses, name):
    t = outs[0].shape[0] * DILATIONS[0]
    tm = _rows(t, 512)

    def body(*refs):
        o_refs, l_refs = refs[:N_GROUPS], refs[N_GROUPS:2 * N_GROUPS]
        ob_ref, of_ref, lse_ref = refs[2 * N_GROUPS:2 * N_GROUPS + 3]
        stages = list(refs[2 * N_GROUPS + 3:])
        staged = []
        for o_ref, l_ref, dil in zip(o_refs, l_refs, DILATIONS):
            if dil > 1:
                so, sl = stages.pop(), stages.pop()
                _from_groups(o_ref, so, dil)
                _from_groups(l_ref, sl, dil)
                staged.append((so, sl))
            else:
                staged.append(None)
        for h in range(HEADS_PER_GROUP):
            hs = slice(h * HEAD_DIM, (h + 1) * HEAD_DIM)
            os_ = [o_ref[:, hs] if st is None else st[0][h] for o_ref, st in zip(o_refs, staged)]
            ls = [l_ref[:, hs] if st is None else st[1][h] for l_ref, st in zip(l_refs, staged)]
            m = functools.reduce(jnp.maximum, ls)
            ws = [jnp.exp(l - m) for l in ls]
            den = functools.reduce(jnp.add, ws)
            num = functools.reduce(jnp.add, [w * o for w, o in zip(ws, os_)])
            o = num / den
            ob_ref[:, hs] = o.astype(BF16)
            of_ref[:, hs] = o
            lse_ref[:, hs] = m + jnp.log(den)

    blk = pl.BlockSpec((tm, GROUP_WIDTH), lambda i: (i, 0))
    specs = [blk if dil == 1 else _group_spec(tm, dil) for dil in DILATIONS]
    f32 = jax.ShapeDtypeStruct((t, GROUP_WIDTH), F32)
    n_stage = 2 * sum(dil > 1 for dil in DILATIONS)
    return _pallas_call(
        body, out_shape=(jax.ShapeDtypeStruct((t, GROUP_WIDTH), BF16), f32, f32), grid=(t // tm,),
        in_specs=specs * 2, out_specs=(blk, blk, blk), scratch_shapes=[_stage(tm)] * n_stage,
        compiler_params=_cparams(("parallel",), 13 * _nbytes((tm, GROUP_WIDTH), F32)), name=name)(*outs, *lses)


def _att_delta(do, o, name):
    t = o.shape[0]
    tm = _rows(t, 512)

    def body(do_ref, o_ref, d_ref):
        for h in range(HEADS_PER_GROUP):
            sl = slice(h * HEAD_DIM, (h + 1) * HEAD_DIM)
            s = jnp.sum(do_ref[:, sl] * o_ref[:, sl], axis=1, keepdims=True)
            d_ref[:, sl] = jnp.broadcast_to(s, (tm, HEAD_DIM))

    blk = pl.BlockSpec((tm, GROUP_WIDTH), lambda i: (i, 0))
    return _pallas_call(
        body, out_shape=jax.ShapeDtypeStruct((t, GROUP_WIDTH), F32), grid=(t // tm,), in_specs=[blk, blk], out_specs=blk,
        compiler_params=_cparams(("parallel",), 3 * _nbytes((tm, GROUP_WIDTH), F32)), name=name)(do, o)


def _att_bwd_dq(q, k, v, do, lse, delta, offs, dil, name):
    qo, ko, vo = offs
    length = q.shape[0]
    nb = length // ATT_BLOCK
    scale = HEAD_DIM ** -0.5

    def body(q_ref, kp_ref, kc_ref, vp_ref, vc_ref, do_ref, lse_ref, dl_ref, dq_ref):
        mask = _query_mask(pl.program_id(1) > 0)
        heads = [slice(h * HEAD_DIM, (h + 1) * HEAD_DIM) for h in range(HEADS_PER_GROUP)]
        wide = lambda ref, sl: jnp.concatenate([ref[:, sl], ref[:, sl]], axis=1)
        ks = [jnp.concatenate([kp_ref[:, sl], kc_ref[:, sl]], axis=0) for sl in heads]
        vs = [jnp.concatenate([vp_ref[:, sl], vc_ref[:, sl]], axis=0) for sl in heads]
        ps = [jnp.exp(jnp.where(mask, _scores(q_ref[:, sl], kv), MASKED) - wide(lse_ref, sl)) for sl, kv in zip(heads, ks)]
        dps = [lax.dot_general(do_ref[:, sl].astype(BF16), vv, (((1,), (1,)), ((), ())), preferred_element_type=F32)
               for sl, vv in zip(heads, vs)]
        dss = [(p * (dp - wide(dl_ref, sl)) * scale).astype(BF16) for sl, p, dp in zip(heads, ps, dps)]
        dqs = [jnp.dot(ds, kv, preferred_element_type=F32) for ds, kv in zip(dss, ks)]
        for sl, dq in zip(heads, dqs):
            dq_ref[:, sl] = dq.astype(BF16)

    def spec(off, prev):
        if prev:
            return pl.BlockSpec((ATT_BLOCK, GROUP_WIDTH), lambda r, n: (jnp.maximum(n - 1, 0), off + r))
        return pl.BlockSpec((ATT_BLOCK, GROUP_WIDTH), lambda r, n: (n, off + r))

    own = pl.BlockSpec((ATT_BLOCK, GROUP_WIDTH), lambda r, n: (n, r))
    return _pallas_call(
        body, out_shape=jax.ShapeDtypeStruct((length, dil * GROUP_WIDTH), BF16), grid=(dil, nb),
        in_specs=[spec(qo, False), spec(ko, True), spec(ko, False), spec(vo, True), spec(vo, False), own, own, own],
        out_specs=own,
        compiler_params=_cparams(("parallel", "parallel"), 10 * _nbytes((ATT_BLOCK, GROUP_WIDTH), F32)),
        name=name)(q, k, k, v, v, do, lse, delta)


def _att_bwd_dkv(q, k, v, do, lse, delta, offs, dil, name):
    qo, ko, vo = offs
    length = q.shape[0]
    nb = length // ATT_BLOCK
    scale = HEAD_DIM ** -0.5

    def body(k_ref, v_ref, qc_ref, qn_ref, doc_ref, don_ref, lsec_ref, lsen_ref, dlc_ref, dln_ref, dk_ref, dv_ref):
        mask = _key_mask(pl.program_id(1) < nb - 1)
        heads = [slice(h * HEAD_DIM, (h + 1) * HEAD_DIM) for h in range(HEADS_PER_GROUP)]
        both = lambda cur, nxt, sl: jnp.concatenate([cur[:, sl], nxt[:, sl]], axis=0)
        qs = [both(qc_ref, qn_ref, sl) for sl in heads]
        dos = [both(doc_ref, don_ref, sl).astype(BF16) for sl in heads]
        ps = [jnp.exp(jnp.where(mask, _scores(qv, k_ref[:, sl]), MASKED) - both(lsec_ref, lsen_ref, sl)) for sl, qv in zip(heads, qs)]
        dps = [lax.dot_general(dov, v_ref[:, sl], (((1,), (1,)), ((), ())), preferred_element_type=F32) for sl, dov in zip(heads, dos)]
        dss = [(p * (dp - both(dlc_ref, dln_ref, sl)) * scale).astype(BF16) for sl, p, dp in zip(heads, ps, dps)]
        dvs = [lax.dot_general(p.astype(BF16), dov, (((0,), (0,)), ((), ())), preferred_element_type=F32) for p, dov in zip(ps, dos)]
        dks = [lax.dot_general(ds, qv, (((0,), (0,)), ((), ())), preferred_element_type=F32) for ds, qv in zip(dss, qs)]
        for sl, dk, dv in zip(heads, dks, dvs):
            dk_ref[:, sl] = dk.astype(BF16)
            dv_ref[:, sl] = dv.astype(BF16)

    def spec(off, nxt):
        if nxt:
            return pl.BlockSpec((ATT_BLOCK, GROUP_WIDTH), lambda r, n: (jnp.minimum(n + 1, nb - 1), off + r))
        return pl.BlockSpec((ATT_BLOCK, GROUP_WIDTH), lambda r, n: (n, off + r))

    own = pl.BlockSpec((ATT_BLOCK, GROUP_WIDTH), lambda r, n: (n, r))
    out = jax.ShapeDtypeStruct((length, dil * GROUP_WIDTH), BF16)
    return _pallas_call(
        body, out_shape=(out, out), grid=(dil, nb),
        in_specs=[spec(ko, False), spec(vo, False), spec(qo, False), spec(qo, True), spec(0, False), spec(0, True),
                  spec(0, False), spec(0, True), spec(0, False), spec(0, True)],
        out_specs=(own, own),
        compiler_params=_cparams(("parallel", "parallel"), 12 * _nbytes((ATT_BLOCK, GROUP_WIDTH), F32)),
        name=name)(k, v, q, q, do, do, lse, lse, delta, delta)


def _gelu(x):
    return 0.5 * x * (1.0 + lax.erf(x * (2.0 ** -0.5)))


def _gelu_grad(x):
    return 0.5 * (1.0 + lax.erf(x * (2.0 ** -0.5))) + x * jnp.exp(-0.5 * x * x) * ((2.0 * jnp.pi) ** -0.5)


def _sg_normed(vs, lg, lb):
    gv = _gelu(vs)
    mu = jnp.mean(gv, axis=1, keepdims=True)
    xc = gv - mu
    rstd = lax.rsqrt(jnp.mean(xc * xc, axis=1, keepdims=True) + LN_EPS)
    z = xc * rstd
    return z, rstd, z * lg + lb


def _sg_tril():
    row = lax.broadcasted_iota(jnp.int32, (SG_CHUNK, SG_CHUNK), 0)
    col = lax.broadcasted_iota(jnp.int32, (SG_CHUNK, SG_CHUNK), 1)
    return row >= col


def _sg_fwd(proj, u_blk, vs_blk, lg, lb, sg_w, bias, name):
    t = proj.shape[0]
    width = SG_GROUPS * SG_GROUP_DIM

    def body(u_ref, vs_ref, lg_ref, lb_ref, w_ref, bias_ref, o_ref):
        _, _, vn = _sg_normed(vs_ref[...].astype(F32), lg_ref[...], lb_ref[...])
        vn = vn.astype(BF16)
        tril = _sg_tril()
        for g in range(SG_GROUPS):
            sl = slice(g * SG_GROUP_DIM, (g + 1) * SG_GROUP_DIM)
            w = jnp.where(tril, w_ref[g], 0.0).astype(BF16)
            sp = jnp.dot(w, vn[:, sl], preferred_element_type=F32) + bias_ref[:, sl]
            o_ref[:, sl] = (_gelu(u_ref[:, sl].astype(F32)) * sp).astype(BF16)

    vec = pl.BlockSpec((1, width), lambda i: (0, 0))
    return _pallas_call(
        body, out_shape=jax.ShapeDtypeStruct((t, width), BF16), grid=(t // SG_CHUNK,),
        in_specs=[pl.BlockSpec((SG_CHUNK, width), lambda i: (i, u_blk)), pl.BlockSpec((SG_CHUNK, width), lambda i: (i, vs_blk)),
                  vec, vec, pl.BlockSpec((SG_GROUPS, SG_CHUNK, SG_CHUNK), lambda i: (0, 0, 0)),
                  pl.BlockSpec((SG_CHUNK, width), lambda i: (0, 0))],
        out_specs=pl.BlockSpec((SG_CHUNK, width), lambda i: (i, 0)),
        compiler_params=_cparams(("parallel",), 8 * _nbytes((SG_CHUNK, width), F32)), name=name)(proj, proj, lg, lb, sg_w, bias)


def _sg_bwd(proj, u_blk, vs_blk, dsu, lg, lb, sg_w, bias, name):
    t = proj.shape[0]
    width = SG_GROUPS * SG_GROUP_DIM

    def body(u_ref, vs_ref, dsu_ref, lg_ref, lb_ref, w_ref, bias_ref, du_ref, dvs_ref, dw_ref, dbias_ref, dlg_ref, dlb_ref):
        @pl.when(pl.program_id(0) == 0)
        def _():
            dw_ref[...] = jnp.zeros_like(dw_ref)
            dbias_ref[...] = jnp.zeros_like(dbias_ref)
            dlg_ref[...] = jnp.zeros_like(dlg_ref)
            dlb_ref[...] = jnp.zeros_like(dlb_ref)

        vs = vs_ref[...].astype(F32)
        z, rstd, vn = _sg_normed(vs, lg_ref[...], lb_ref[...])
        vn = vn.astype(BF16)
        tril = _sg_tril()
        dvn = []
        for g in range(SG_GROUPS):
            sl = slice(g * SG_GROUP_DIM, (g + 1) * SG_GROUP_DIM)
            w = jnp.where(tril, w_ref[g], 0.0).astype(BF16)
            vg = vn[:, sl]
            sp = jnp.dot(w, vg, preferred_element_type=F32) + bias_ref[:, sl]
            uv = u_ref[:, sl].astype(F32)
            dsu_g = dsu_ref[:, sl].astype(F32)
            du_ref[:, sl] = (dsu_g * sp * _gelu_grad(uv)).astype(BF16)
            dsp = dsu_g * _gelu(uv)
            dsp_b = dsp.astype(BF16)
            dw = lax.dot_general(dsp_b, vg, (((1,), (1,)), ((), ())), preferred_element_type=F32)
            dw_ref[g] += jnp.where(tril, dw, 0.0)
            dbias_ref[:, sl] += jnp.broadcast_to(jnp.sum(dsp, axis=1, keepdims=True), (SG_CHUNK, SG_GROUP_DIM))
            dvn.append(lax.dot_general(w, dsp_b, (((0,), (0,)), ((), ())), preferred_element_type=F32))
        dvn = jnp.concatenate(dvn, axis=1)
        dlg_ref[...] += jnp.sum(dvn * z, axis=0, keepdims=True)
        dlb_ref[...] += jnp.sum(dvn, axis=0, keepdims=True)
        dz = dvn * lg_ref[...]
        dgv = rstd * (dz - jnp.mean(dz, axis=1, keepdims=True) - z * jnp.mean(dz * z, axis=1, keepdims=True))
        dvs_ref[...] = (dgv * _gelu_grad(vs)).astype(BF16)

    vec = pl.BlockSpec((1, width), lambda i: (0, 0))
    row = pl.BlockSpec((SG_CHUNK, width), lambda i: (i, 0))
    fixed = pl.BlockSpec((SG_CHUNK, width), lambda i: (0, 0))
    w_spec = pl.BlockSpec((SG_GROUPS, SG_CHUNK, SG_CHUNK), lambda i: (0, 0, 0))
    act = jax.ShapeDtypeStruct((t, width), BF16)
    return _pallas_call(
        body,
        out_shape=(act, act, jax.ShapeDtypeStruct((SG_GROUPS, SG_CHUNK, SG_CHUNK), F32),
                   jax.ShapeDtypeStruct((SG_CHUNK, width), F32), jax.ShapeDtypeStruct((1, width), F32),
                   jax.ShapeDtypeStruct((1, width), F32)),
        grid=(t // SG_CHUNK,),
        in_specs=[pl.BlockSpec((SG_CHUNK, width), lambda i: (i, u_blk)), pl.BlockSpec((SG_CHUNK, width), lambda i: (i, vs_blk)),
                  row, vec, vec, w_spec, fixed],
        out_specs=(row, row, w_spec, fixed, vec, vec),
        compiler_params=_cparams(("arbitrary",), 14 * _nbytes((SG_CHUNK, width), F32)),
        name=name)(proj, proj, dsu, lg, lb, sg_w, bias)


def _gate_fwd(proj, ga_blk, gs_blk, y_att, y_sg, name):
    t, d = y_att.shape
    tm, tn = _rows(t, 512), _tile(d, GROUP_WIDTH)

    def body(ga_ref, gs_ref, ya_ref, ys_ref, o_ref):
        o_ref[...] = (jax.nn.sigmoid(ga_ref[...].astype(F32)) * ya_ref[...].astype(F32)
                      + jax.nn.sigmoid(gs_ref[...].astype(F32)) * ys_ref[...].astype(F32)).astype(BF16)

    own = pl.BlockSpec((tm, tn), lambda i, j: (i, j))
    return _pallas_call(
        body, out_shape=jax.ShapeDtypeStruct((t, d), BF16), grid=(t // tm, d // tn),
        in_specs=[pl.BlockSpec((tm, tn), lambda i, j: (i, ga_blk + j)), pl.BlockSpec((tm, tn), lambda i, j: (i, gs_blk + j)),
                  own, own],
        out_specs=own, compiler_params=_cparams(("parallel", "parallel"), 6 * _nbytes((tm, tn), F32)),
        name=name)(proj, proj, y_att, y_sg)


def _gate_bwd(proj, ga_blk, gs_blk, y_att, y_sg, dmerged, name):
    t, d = y_att.shape
    tm, tn = _rows(t, 512), _tile(d, GROUP_WIDTH)

    def body(ga_ref, gs_ref, ya_ref, ys_ref, dm_ref, dya_ref, dys_ref, dga_ref, dgs_ref):
        dm = dm_ref[...].astype(F32)
        for g_ref, y_ref, dy_ref, dg_ref in ((ga_ref, ya_ref, dya_ref, dga_ref), (gs_ref, ys_ref, dys_ref, dgs_ref)):
            sg = jax.nn.sigmoid(g_ref[...].astype(F32))
            dy_ref[...] = (dm * sg).astype(BF16)
            dg_ref[...] = (dm * y_ref[...].astype(F32) * sg * (1.0 - sg)).astype(BF16)

    own = pl.BlockSpec((tm, tn), lambda i, j: (i, j))
    out = jax.ShapeDtypeStruct((t, d), BF16)
    return _pallas_call(
        body, out_shape=(out, out, out, out), grid=(t // tm, d // tn),
        in_specs=[pl.BlockSpec((tm, tn), lambda i, j: (i, ga_blk + j)), pl.BlockSpec((tm, tn), lambda i, j: (i, gs_blk + j)),
                  own, own, own],
        out_specs=(own, own, own, own), compiler_params=_cparams(("parallel", "parallel"), 10 * _nbytes((tm, tn), F32)),
        name=name)(proj, proj, y_att, y_sg, dmerged)


def _mixer_forward(x, wb, small, in_specs, sg_specs, out_specs):
    t, d = x.shape
    att_w = N_GROUPS * GROUP_WIDTH
    sg_w = SG_GROUPS * SG_GROUP_DIM
    n = _rmsnorm_fwd(x, small['mix_norm'], "mix_norm")
    rider, names = _gather(wb, in_specs)
    proj, got = _matmul([(n, _full(wb, 'w_in'))], 'nn', BF16, "mix_in", b3=True, caps=(1024, 1024, 1024), rider=rider)
    _landed(wb, names, got)
    tables = _rope_tables(t)
    qk0, grouped = _rope_fwd(proj, tables, "mix_rope")
    qkv = [(qk0, qk0, proj, (0, 1, 2 * N_GROUPS))] + [g + ((0, 0, 0),) for g in grouped]
    outs, lses = zip(*[_att_fwd(*args, dil, f"att_fwd{gi}") for gi, (args, dil) in enumerate(zip(qkv, DILATIONS))])
    o_b, o_f, lse = _att_combine(outs, lses, "att_combine")
    y_att = _matmul([(o_b, _full(wb, 'w_att_out'))], 'nn', BF16, "mix_att_out", b3=True)
    bias = jnp.repeat(small['sg_b'].T, SG_GROUP_DIM, axis=1)
    u_blk, vs_blk = 3 * att_w // sg_w, 3 * att_w // sg_w + 1
    su = _sg_fwd(proj, u_blk, vs_blk, small['sg_ln_g'], small['sg_ln_b'], small['sg_w'], bias, "sg_fwd")
    rider, names = _gather(wb, sg_specs)
    y_sg, got = _matmul([(su, _full(wb, 'w_sg_out'))], 'nn', BF16, "mix_sg_out", b3=True, rider=rider)
    _landed(wb, names, got)
    ga_blk = (3 * att_w + 2 * sg_w) // _tile(d, GROUP_WIDTH)
    gs_blk = ga_blk + d // _tile(d, GROUP_WIDTH)
    merged = _gate_fwd(proj, ga_blk, gs_blk, y_att, y_sg, "gate_fwd")
    rider, names = _gather(wb, out_specs)
    x_next, got = _matmul([(merged, _full(wb, 'w_out'))], 'nn', F32, "mix_out", residual=x, rider=rider)
    _landed(wb, names, got)
    saved = (n, proj, qkv, tables, o_b, o_f, lse, y_att, su, y_sg, merged, bias, (u_blk, vs_blk, ga_blk, gs_blk))
    return x_next, saved


def _mixer_backward(x, wb, small, saved, dx_next, dx_next_b, c_idx):
    n, proj, qkv, tables, o_b, o_f, lse, y_att, su, y_sg, merged, bias, (u_blk, vs_blk, ga_blk, gs_blk) = saved
    s = N_CHIPS
    dmerged = _matmul([(dx_next_b, _full(wb, 'w_out'))], 'nt', BF16, "mix_out_dx")
    g_w_out = _matmul([(merged, dx_next_b)], 'tn', BF16, "mix_out_dw", caps=(1024, 1024, 1024))
    dy_att, dy_sg, dg_att, dg_sg = _gate_bwd(proj, ga_blk, gs_blk, y_att, y_sg, dmerged, "gate_bwd")

    g_w_att_out = _matmul([(o_b, dy_att)], 'tn', BF16, "mix_att_out_dw", out3=s)
    do = _matmul([(dy_att, _full(wb, 'w_att_out'))], 'nt', F32, "mix_att_out_dx", b3=True)
    delta = _att_delta(do, o_f, "att_delta")
    stats = [(do, lse, delta)] + _regroup([do, lse, delta], "att_regroup")
    dqkv = []
    for gi, ((q, k, v, offs), st, dil) in enumerate(zip(qkv, stats, DILATIONS)):
        dq = _att_bwd_dq(q, k, v, *st, offs, dil, f"att_bwd_dq{gi}")
        dk, dv = _att_bwd_dkv(q, k, v, *st, offs, dil, f"att_bwd_dkv{gi}")
        dqkv.append((dq, dk, dv))
    dqkv = _rope_bwd(*dqkv[0], dqkv[1:], tables, "mix_rope_bwd")

    g_w_sg_out = _matmul([(su, dy_sg)], 'tn', BF16, "mix_sg_out_dw", out3=s)
    out_names = ['w_out', 'w_att_out', 'w_sg_out']
    out_parts = _reduce_first([g_w_out, g_w_att_out, g_w_sg_out], out_names, wb, c_idx)
    dsu = _matmul([(dy_sg, _full(wb, 'w_sg_out'))], 'nt', BF16, "mix_sg_out_dx", b3=True)
    du, dvs, g_sg_w, g_bias, g_lg, g_lb = _sg_bwd(proj, u_blk, vs_blk, dsu, small['sg_ln_g'], small['sg_ln_b'],
                                                   small['sg_w'], bias, "sg_bwd")
    gs = {'sg_w': g_sg_w, 'sg_b': g_bias[:, ::SG_GROUP_DIM].T, 'sg_ln_g': g_lg, 'sg_ln_b': g_lb}

    dproj = jnp.concatenate([dqkv, du, dvs, dg_att, dg_sg], axis=1)
    g_w_in, out_recv = _matmul([(n, dproj)], 'tn', BF16, "mix_in_dw", out3=s, caps=(1024, 1024, 1024),
                               rider=_scatter_rider(out_parts))
    (p_w_in,) = _reduce_first([g_w_in], ['w_in'], wb, c_idx)
    dn, (r_w_in,) = _matmul([(dproj, _full(wb, 'w_in'))], 'nt', F32, "mix_in_dx", b3=True, caps=(1024, 1024, 512),
                            rider=_scatter_rider([p_w_in]))
    dx, dx_b, gs['mix_norm'] = _rmsnorm_bwd(x, small['mix_norm'], dn, dx_next, "mix_norm_bwd")
    g = {nm: (p, r) for nm, p, r in zip(out_names, out_parts, out_recv)}
    g['w_in'] = (p_w_in, r_w_in)
    return dx, dx_b, g, gs


def _step(x, target, wb, small, c_idx):
    wb = dict(wb)
    rider, names = _gather(wb, ['ffn1_w_gate', 'ffn1_w_up'])
    _landed(wb, names, _exchange(rider, "gather_first"))
    half_in = wb['w_in'].shape[2] // 2
    x1, s1 = _ffn_forward(x, small['ffn1_norm'], wb, "ffn1", ['ffn1_w_down', ('w_in', 0, half_in)], [('w_in', half_in, 2 * half_in)])
    up_rows = wb['ffn2_w_up'].shape[2]
    up_cut = up_rows // 32 * 15
    x2, s2 = _mixer_forward(x1, wb, small, ['w_att_out', 'w_sg_out', 'w_out', 'ffn2_w_gate'],
                            [('ffn2_w_up', 0, up_cut)], [('ffn2_w_up', up_cut, up_rows)])
    x3, s3 = _ffn_forward(x2, small['ffn2_norm'], wb, "ffn2", ['ffn2_w_down'], None)
    loss, dx3, dx3_b, g_final = _final_loss(x3, small['final_norm'], target, "final_loss")
    gs = {'final_norm': g_final}
    dx2, dx2_b, gs['ffn2_norm'], g = _ffn_backward(x2, small['ffn2_norm'], wb, s3, dx3, dx3_b, c_idx, "ffn2")
    dx1, dx1_b, g_mix, gs_mix = _mixer_backward(x1, wb, small, s2, dx2, dx2_b, c_idx)
    g.update(g_mix)
    gs.update(gs_mix)
    dx0, _, gs['ffn1_norm'], g_ffn1 = _ffn_backward(x, small['ffn1_norm'], wb, s1, dx1, dx1_b, c_idx, "ffn1")
    g.update(g_ffn1)
    return loss, dx0, g, gs


def _cast_into_gathered(wt, p_idx, name):
    r, ccols = wt.shape[0] // 2, wt.shape[1]
    tm = _rows(r, 256)
    nb = r // tm

    def body(p_ref, w_ref, o_ref):
        o_ref[...] = w_ref[...].astype(BF16)

    grid_spec = pltpu.PrefetchScalarGridSpec(
        num_scalar_prefetch=1, grid=(2, nb),
        in_specs=[pl.BlockSpec((tm, ccols), lambda h, i, pr: (h * nb + i, 0))],
        out_specs=pl.BlockSpec((None, None, tm, ccols), lambda h, i, pr: (pr[0], h, i, 0)))
    return pl.pallas_call(body, out_shape=jax.ShapeDtypeStruct((N_CHIPS, 2, r, ccols), BF16), grid_spec=grid_spec,
                          compiler_params=_cparams(("parallel", "parallel"), 2 * _nbytes((tm, ccols), F32)), name=name)(p_idx, wt)


def _sibling_exchange(grads, name):
    nw = len(grads)

    def body(*refs):
        src, dst = refs[:nw], refs[nw:2 * nw]
        send_sems, recv_sems = refs[2 * nw:]
        x, y, c, _ = _place()
        cps = []
        for i in range(nw):
            cp = pltpu.make_async_remote_copy(src[i].at[:, 1 - c], dst[i], send_sems.at[i], recv_sems.at[i],
                                              device_id=(x, y, 1 - c), device_id_type=MESH)
            cp.start()
            cps.append(cp)
        for cp in cps:
            cp.wait()

    any_spec = pl.BlockSpec(memory_space=pl.ANY)
    return _pallas_call(
        body, out_shape=[jax.ShapeDtypeStruct((g.shape[0],) + g.shape[2:], g.dtype) for g in grads],
        in_specs=[any_spec] * nw, out_specs=[any_spec] * nw,
        scratch_shapes=[pltpu.SemaphoreType.DMA((nw,)), pltpu.SemaphoreType.DMA((nw,))],
        compiler_params=pltpu.CompilerParams(has_side_effects=True), name=name)(*grads)


def _half_exchange(bufs):
    nw = len(bufs)

    def body(*refs):
        dst = refs[nw:2 * nw]
        send_sems, recv_sems = refs[2 * nw:]
        x, y, c, _ = _place()
        cps = []
        for i in range(nw):
            mine = dst[i].at[c]
            cp = pltpu.make_async_remote_copy(mine, mine, send_sems.at[i], recv_sems.at[i],
                                              device_id=(x, y, 1 - c), device_id_type=MESH)
            cp.start()
            cps.append(cp)
        for i, cp in enumerate(cps):
            cp.wait_send()
            theirs = dst[i].at[1 - c]
            pltpu.make_async_remote_copy(theirs, theirs, send_sems.at[i], recv_sems.at[i],
                                         device_id=(x, y, 1 - c), device_id_type=MESH).wait_recv()

    any_spec = pl.BlockSpec(memory_space=pl.ANY)
    return _pallas_call(
        body, out_shape=[jax.ShapeDtypeStruct(b.shape, b.dtype) for b in bufs],
        in_specs=[any_spec] * nw, out_specs=[any_spec] * nw, input_output_aliases={i: i for i in range(nw)},
        scratch_shapes=[pltpu.SemaphoreType.DMA((nw,)), pltpu.SemaphoreType.DMA((nw,))],
        compiler_params=pltpu.CompilerParams(has_side_effects=True), name="rs_halves")(*bufs)


def _sibling_sum(grad, recv, c_idx, name):
    s, _, r, ccols = grad.shape
    tm = _rows(r, 256)

    def body(c_ref, g_ref, r_ref, o_ref):
        o_ref[...] = (g_ref[...].astype(F32) + r_ref[...].astype(F32)).astype(BF16)

    grid_spec = pltpu.PrefetchScalarGridSpec(
        num_scalar_prefetch=1, grid=(s, r // tm),
        in_specs=[pl.BlockSpec((None, None, tm, ccols), lambda q, i, cr: (q, cr[0], i, 0)),
                  pl.BlockSpec((None, tm, ccols), lambda q, i, cr: (q, i, 0))],
        out_specs=pl.BlockSpec((None, tm, ccols), lambda q, i, cr: (q, i, 0)))
    return pl.pallas_call(body, out_shape=jax.ShapeDtypeStruct((s, r, ccols), BF16), grid_spec=grid_spec,
                          compiler_params=_cparams(("parallel", "parallel"), 4 * _nbytes((tm, ccols), F32)), name=name)(c_idx, grad, recv)


def _chip_sum(part, recv, pc_idx, name):
    _, r, ccols = part.shape
    tm = _rows(r, 256)

    def body(pc_ref, own_ref, r0_ref, r1_ref, r2_ref, o_ref):
        acc = own_ref[...].astype(F32) + r0_ref[...].astype(F32)
        acc = acc + r1_ref[...].astype(F32)
        o_ref[...] = acc + r2_ref[...].astype(F32)

    def slot(j):
        return pl.BlockSpec((None, tm, ccols), lambda i, pc: (j, i, 0))

    grid_spec = pltpu.PrefetchScalarGridSpec(
        num_scalar_prefetch=1, grid=(r // tm,),
        in_specs=[pl.BlockSpec((None, tm, ccols), lambda i, pc: (pc[0], i, 0)), slot(0), slot(1), slot(2)],
        out_specs=pl.BlockSpec((None, tm, ccols), lambda i, pc: (pc[1], i, 0)))
    return pl.pallas_call(body, out_shape=jax.ShapeDtypeStruct((2, r, ccols), F32), grid_spec=grid_spec,
                          compiler_params=_cparams(("parallel",), 6 * _nbytes((tm, ccols), F32)), name=name)(pc_idx, part, recv, recv, recv)


def _all_reduce_small(vec):
    _, r, _ = vec.shape

    def body(v_ref, o_ref, parts, send1, recv1, send2, recv2):
        x, y, c, _ = _place()
        me = 4 * x + 2 * y + c
        peers = []
        for k in range(1, N_DEV):
            px, py, pc = (1 - x if k & 4 else x, 1 - y if k & 2 else y, 1 - c if k & 1 else c)
            peers.append(((px, py, pc), 4 * px + 2 * py + pc))
        parts[me] = v_ref[me]
        cps = []
        for k, (peer, peer_id) in enumerate(peers):
            cp = pltpu.make_async_remote_copy(v_ref.at[peer_id], parts.at[me], send1.at[k], recv1.at[k],
                                              device_id=peer, device_id_type=MESH)
            cp.start()
            cps.append(cp)
        for cp in cps:
            cp.wait()
        acc = parts[0]
        for dev in range(1, N_DEV):
            acc = acc + parts[dev]
        o_ref[me] = acc
        cps = []
        for k, (peer, _) in enumerate(peers):
            cp = pltpu.make_async_remote_copy(o_ref.at[me], o_ref.at[me], send2.at[k], recv2.at[k],
                                              device_id=peer, device_id_type=MESH)
            cp.start()
            cps.append(cp)
        for cp in cps:
            cp.wait()

    vm = pl.BlockSpec(memory_space=pltpu.VMEM)
    sems = pltpu.SemaphoreType.DMA((N_DEV - 1,))
    return pl.pallas_call(
        body, out_shape=jax.ShapeDtypeStruct(vec.shape, F32), in_specs=[vm], out_specs=vm,
        scratch_shapes=[pltpu.VMEM((N_DEV, r, LANES), F32), sems, sems, sems, sems],
        compiler_params=pltpu.CompilerParams(vmem_limit_bytes=int(8 * _nbytes((N_DEV, r, LANES), F32))),
        name="all_reduce_small")(vec)


def _adamw(wt, g, m, v, name):
    r, ccols = wt.shape
    tm = _rows(r, max(8, (MIB // (4 * ccols)) // 8 * 8))
    blk = pl.BlockSpec((tm, ccols), lambda i: (i, 0))

    def body(w_ref, g_ref, m_ref, v_ref, go_ref, d_ref, mo_ref, vo_ref):
        gv = g_ref[...]
        go_ref[...] = gv
        mv = ADAM_B1 * m_ref[...] + (1.0 - ADAM_B1) * gv
        vv = ADAM_B2 * v_ref[...] + (1.0 - ADAM_B2) * (gv * gv)
        m_hat = mv / (1.0 - ADAM_B1 ** ADAM_STEP)
        v_hat = vv / (1.0 - ADAM_B2 ** ADAM_STEP)
        d_ref[...] = -ADAM_LR * (m_hat / (jnp.sqrt(v_hat) + ADAM_EPS) + ADAM_WD * w_ref[...])
        mo_ref[...] = mv
        vo_ref[...] = vv

    out = jax.ShapeDtypeStruct((r, ccols), F32)
    return pl.pallas_call(body, out_shape=(out, out, out, out), grid=(r // tm,), in_specs=[blk] * 4, out_specs=(blk,) * 4,
                          compiler_params=_cparams(("parallel",), 8 * _nbytes((tm, ccols), F32)), name=name)(wt, g, m, v)


def _as_rows(a):
    rows = a.reshape(-1, LANES)
    return jnp.pad(rows, ((0, -rows.shape[0] % 8), (0, 0)))


def kernel(x, ffn1_norm, ffn1_w_gate, ffn1_w_up, ffn1_w_down, mix_norm, w_in, sg_ln_g, sg_ln_b, sg_w, sg_b, w_att_out, w_sg_out, w_out, ffn2_norm, ffn2_w_gate, ffn2_w_up, ffn2_w_down, final_norm, loss_target, m_ffn1_norm, m_ffn1_w_gate, m_ffn1_w_up, m_ffn1_w_down, m_mix_norm, m_w_in, m_sg_ln_g, m_sg_ln_b, m_sg_w, m_sg_b, m_w_att_out, m_w_sg_out, m_w_out, m_ffn2_norm, m_ffn2_w_gate, m_ffn2_w_up, m_ffn2_w_down, m_final_norm, v_ffn1_norm, v_ffn1_w_gate, v_ffn1_w_up, v_ffn1_w_down, v_mix_norm, v_w_in, v_sg_ln_g, v_sg_ln_b, v_sg_w, v_sg_b, v_w_att_out, v_w_sg_out, v_w_out, v_ffn2_norm, v_ffn2_w_gate, v_ffn2_w_up, v_ffn2_w_down, v_final_norm):
    given = dict(locals())
    wts = {n: given[n] for n in WEIGHT_NAMES}
    ms = {n: given["m_" + n] for n in WEIGHT_NAMES}
    vs = {n: given["v_" + n] for n in WEIGHT_NAMES}
    t, d = x.shape[-2], x.shape[-1]
    xc, yc, cc = lax.axis_index("x"), lax.axis_index("y"), lax.axis_index("c")

    shard2d = {n: wts[n].reshape(wts[n].shape[-2:]) for n in BIG_NAMES}
    p_idx = jnp.reshape(2 * xc + yc, (1,)).astype(jnp.int32)
    c_idx = jnp.reshape(cc, (1,)).astype(jnp.int32)
    pc_idx = jnp.stack([2 * xc + yc, cc]).astype(jnp.int32)
    wb = {n: _cast_into_gathered(shard2d[n], p_idx, f"cast_{n}") for n in BIG_NAMES}

    small = {n: wts[n].reshape(-1, wts[n].shape[-1]) for n in SMALL_NAMES}
    small['sg_w'] = wts['sg_w'].reshape(wts['sg_w'].shape[-3:])
    loss, dx, g, gs = _step(x.reshape(t, d), loss_target.reshape(t, d), wb, small, c_idx)
    loss = lax.psum(loss[0, 0], ("x", "y", "c"))

    my_halves = [_chip_sum(*g[n], pc_idx, f"rs_sum2_{n}") for n in BIG_NAMES]
    reduced = _half_exchange(my_halves)
    grads = {n: r.reshape(shard2d[n].shape) for n, r in zip(BIG_NAMES, reduced)}

    def pack(tree):
        rows = jnp.concatenate([_as_rows(tree[n]) for n in SMALL_NAMES], axis=0)
        return jnp.pad(rows, ((0, -rows.shape[0] % (8 * N_DEV)), (0, 0)))

    packed = pack(gs)
    packed = _all_reduce_small(packed.reshape(N_DEV, -1, LANES)).reshape(packed.shape)

    delta, new_m, new_v = {}, {}, {}
    for n in BIG_NAMES:
        shape = wts[n].shape
        out = _adamw(shard2d[n], grads[n], ms[n].reshape(shard2d[n].shape), vs[n].reshape(shard2d[n].shape), f"adamw_{n}")
        grads[n], delta[n], new_m[n], new_v[n] = (a.reshape(shape) for a in out)

    small_out = _adamw(pack(wts), packed, pack(ms), pack(vs), "adamw_small")
    row = 0
    for n in SMALL_NAMES:
        shape = wts[n].shape
        sz = wts[n].size // LANES
        grads[n], delta[n], new_m[n], new_v[n] = (a[row:row + sz].reshape(shape) for a in small_out)
        row += sz + -sz % 8

    return (loss, dx.reshape(x.shape), *[grads[n] for n in WEIGHT_NAMES], *[delta[n] for n in WEIGHT_NAMES],
            *[new_m[n] for n in WEIGHT_NAMES], *[new_v[n] for n in WEIGHT_NAMES])
```

```python
import functools

import jax
import jax.numpy as jnp
from jax import lax
from jax.experimental import pallas as pl
from jax.experimental.pallas import tpu as pltpu

F32 = jnp.float32
BF16 = jnp.bfloat16
MESH = pl.DeviceIdType.MESH

NORM_EPS = 1e-6
LN_EPS = 1e-5
HEAD_DIM = 128
HEADS_PER_GROUP = 4
GROUP_WIDTH = HEADS_PER_GROUP * HEAD_DIM
DILATIONS = (1, 4, 16)
N_GROUPS = len(DILATIONS)
ATT_BLOCK = 128
ROPE_DIM = HEAD_DIM // 4
ROPE_THETA = 500000.0
SG_CHUNK = 128
SG_GROUPS = 12
SG_GROUP_DIM = 128
MASKED = -1e30

ADAM_LR = 0.001
ADAM_B1 = 0.9
ADAM_B2 = 0.999
ADAM_EPS = 1e-08
ADAM_WD = 0.01
ADAM_STEP = 10

N_CHIPS = 4
N_DEV = 8
LANES = 128
MIB = 2 ** 20
VMEM_BYTES_V7X = 64 * MIB

WEIGHT_NAMES = ['ffn1_norm', 'ffn1_w_gate', 'ffn1_w_up', 'ffn1_w_down', 'mix_norm', 'w_in', 'sg_ln_g', 'sg_ln_b',
                'sg_w', 'sg_b', 'w_att_out', 'w_sg_out', 'w_out', 'ffn2_norm', 'ffn2_w_gate', 'ffn2_w_up',
                'ffn2_w_down', 'final_norm']
BIG = [('ffn1_w_gate', 1), ('ffn1_w_up', 1), ('ffn1_w_down', 0), ('w_in', 1), ('w_att_out', 1), ('w_sg_out', 1),
       ('w_out', 0), ('ffn2_w_gate', 1), ('ffn2_w_up', 1), ('ffn2_w_down', 0)]
BIG_NAMES = [n for n, _ in BIG]
SMALL_NAMES = [n for n in WEIGHT_NAMES if n not in BIG_NAMES]


def _nbytes(shape, dtype):
    n = jnp.dtype(dtype).itemsize
    for s in shape:
        if s is not None:
            n *= s
    return n


def _pallas_call(*args, **kw):
    kw['out_shape'] = jax.tree.map(lambda s: pltpu.HBM(s.shape, s.dtype), kw['out_shape'])
    call = pl.pallas_call(*args, **kw)

    def pinned(*operands):
        return call(*[o if jnp.issubdtype(o.dtype, jnp.integer) else pltpu.with_memory_space_constraint(o, pltpu.HBM)
                      for o in operands])

    return pinned


def _cparams(sem, block_bytes, **kw):
    limit = int(min(max(3 * block_bytes, 32 * MIB), VMEM_BYTES_V7X - 8 * MIB))
    return pltpu.CompilerParams(dimension_semantics=sem, vmem_limit_bytes=limit, **kw)


def _tile(dim, cap):
    best = None
    for t in range(LANES, min(dim, cap) + 1, LANES):
        if dim % t == 0:
            best = t
    if best is None:
        assert dim <= cap, (dim, cap)
        return dim
    return best


def _rows(dim, cap):
    best = None
    for t in range(8, min(dim, cap) + 1, 8):
        if dim % t == 0:
            best = t
    assert best is not None, (dim, cap)
    return best


def _place():
    x, y, c = lax.axis_index("x"), lax.axis_index("y"), lax.axis_index("c")
    others = [(1 - x, y), (x, 1 - y), (1 - x, 1 - y)]
    return x, y, c, others


class _Rider:
    def __init__(self, operands, out_shapes, aliases, sems, start, finish):
        self.operands = operands
        self.out_shapes = out_shapes
        self.aliases = aliases
        self.sems = sems
        self.start = start
        self.finish = finish


def _run(body, *, name, grid, in_specs, out_specs, out_shape, scratch_shapes, operands, block_bytes, rider=None):
    if rider is None:
        sem = ("parallel",) * (len(grid) - 1) + ("arbitrary",)
        return _pallas_call(body, out_shape=out_shape, grid=grid, in_specs=in_specs, out_specs=out_specs,
                              scratch_shapes=scratch_shapes, compiler_params=_cparams(sem, block_bytes), name=name)(*operands)
    n_in, n_out, n_scr = len(operands), len(out_shape), len(scratch_shapes)
    r_in, r_out = len(rider.operands), len(rider.out_shapes)
    any_spec = pl.BlockSpec(memory_space=pl.ANY)

    def wrapped(*refs):
        ins, refs = refs[:n_in], refs[n_in:]
        r_ins, refs = refs[:r_in], refs[r_in:]
        outs, refs = refs[:n_out], refs[n_out:]
        r_outs, refs = refs[:r_out], refs[r_out:]
        scr, sems = refs[:n_scr], refs[n_scr:]
        if not grid:
            rider.start(r_ins, r_outs, sems)
            rider.finish(r_ins, r_outs, sems)
            return
        ids = [pl.program_id(a) for a in range(len(grid))]
        first = functools.reduce(jnp.logical_and, [i == 0 for i in ids])
        last = functools.reduce(jnp.logical_and, [i == g - 1 for i, g in zip(ids, grid)])

        @pl.when(first)
        def _():
            rider.start(r_ins, r_outs, sems)

        body(*ins, *outs, *scr)

        @pl.when(last)
        def _():
            rider.finish(r_ins, r_outs, sems)

    results = _pallas_call(
        wrapped, out_shape=list(out_shape) + list(rider.out_shapes), grid=grid,
        in_specs=list(in_specs) + [any_spec] * r_in, out_specs=list(out_specs) + [any_spec] * r_out,
        scratch_shapes=list(scratch_shapes) + list(rider.sems),
        input_output_aliases={n_in + k: n_out + v for k, v in rider.aliases.items()},
        compiler_params=_cparams(("arbitrary",) * len(grid) if grid else None, block_bytes, has_side_effects=True),
        name=name)(*operands, *rider.operands)
    return results[:n_out], results[n_out:]


def _exchange(rider, name):
    return _run(None, name=name, grid=(), in_specs=[], out_specs=[], out_shape=[], scratch_shapes=[], operands=[],
                block_bytes=0, rider=rider)[1]


def _gather_rider(items):
    bufs, index = [], []
    for b, r0, r1 in items:
        if not any(b is q for q in bufs):
            bufs.append(b)
        index.append(([k for k, q in enumerate(bufs) if q is b][0], r0, r1))
    n = len(index)

    def piece(refs, k, chip, half):
        bi, r0, r1 = index[k]
        return refs[bi].at[chip, half, pl.ds(r0, r1 - r0)]

    def copy(ref, sem_pair, k, j, to):
        return pltpu.make_async_remote_copy(ref, ref, sem_pair[0].at[k, j], sem_pair[1].at[k, j], device_id=to, device_id_type=MESH)

    def start(r_ins, buf, sems):
        x, y, c, others = _place()
        for k in range(n):
            for j, (ox, oy) in enumerate(others):
                copy(piece(buf, k, 2 * x + y, c), sems[:2], k, j, (ox, oy, c)).start()

    def finish(r_ins, buf, sems):
        x, y, c, others = _place()
        for k in range(n):
            for j, (ox, oy) in enumerate(others):
                got = piece(buf, k, 2 * ox + oy, c)
                copy(got, sems[:2], k, j, (ox, oy, c)).wait_recv()
                copy(got, sems[2:], k, j, (x, y, 1 - c)).start()
        for k in range(n):
            for j, (ox, oy) in enumerate(others):
                copy(piece(buf, k, 2 * ox + oy, 1 - c), sems[2:], k, j, (x, y, 1 - c)).wait_recv()
        for k in range(n):
            for j, (ox, oy) in enumerate(others):
                copy(piece(buf, k, 2 * x + y, c), sems[:2], k, j, (ox, oy, c)).wait_send()
                copy(piece(buf, k, 2 * ox + oy, c), sems[2:], k, j, (x, y, 1 - c)).wait_send()

    return _Rider(bufs, [jax.ShapeDtypeStruct(b.shape, b.dtype) for b in bufs], {i: i for i in range(len(bufs))},
                  [pltpu.SemaphoreType.DMA((n, 3))] * 4, start, finish)


def _scatter_rider(parts):
    n = len(parts)

    def copy(src, dst, sems, i, j, to):
        return pltpu.make_async_remote_copy(src, dst, sems[0].at[i, j], sems[1].at[i, j], device_id=to, device_id_type=MESH)

    def start(src, dst, sems):
        x, y, c, others = _place()
        for i in range(n):
            for j, (ox, oy) in enumerate(others):
                copy(src[i].at[2 * ox + oy], dst[i].at[j], sems, i, j, (ox, oy, c)).start()

    def finish(src, dst, sems):
        x, y, c, others = _place()
        for i in range(n):
            for j, (ox, oy) in enumerate(others):
                copy(src[i].at[2 * ox + oy], dst[i].at[j], sems, i, j, (ox, oy, c)).wait()

    return _Rider(parts, [jax.ShapeDtypeStruct((3,) + p.shape[1:], p.dtype) for p in parts], {},
                  [pltpu.SemaphoreType.DMA((n, 3))] * 2, start, finish)


def _matmul(pairs, mode, out_dtype, name, *, scale=1.0, residual=None, b3=False, out3=0, caps=(1024, 1024, 512), rider=None):
    a0, b0 = pairs[0]
    if mode == 'nn':
        m, k = a0.shape
        n = b0.shape[0] * b0.shape[2] if b3 else b0.shape[1]
    elif mode == 'nt':
        m = a0.shape[0]
        n, k = (b0.shape[1], b0.shape[0] * b0.shape[2]) if b3 else b0.shape
    else:
        k, m = a0.shape
        n = b0.shape[1]
    tm = _tile(m, caps[0])
    tn = _tile(n, caps[1])
    tk = _tile(k, caps[2])
    if b3 and mode == 'nn':
        tn = b0.shape[2]
    if b3 and mode == 'nt':
        tk = b0.shape[2]
    if out3:
        tn = n // out3
    nk = k // tk
    if mode == 'tn':
        a_spec = pl.BlockSpec((tk, tm), lambda i, j, kk: (kk, i))
        b_spec = pl.BlockSpec((tk, tn), lambda i, j, kk: (kk, j))
        dims = ((0,), (0,))
    elif mode == 'nn':
        a_spec = pl.BlockSpec((tm, tk), lambda i, j, kk: (i, kk))
        b_spec = (pl.BlockSpec((None, tk, tn), lambda i, j, kk: (j, kk, 0)) if b3
                  else pl.BlockSpec((tk, tn), lambda i, j, kk: (kk, j)))
        dims = ((1,), (0,))
    else:
        a_spec = pl.BlockSpec((tm, tk), lambda i, j, kk: (i, kk))
        b_spec = (pl.BlockSpec((None, tn, tk), lambda i, j, kk: (kk, j, 0)) if b3
                  else pl.BlockSpec((tn, tk), lambda i, j, kk: (j, kk)))
        dims = ((1,), (1,))
    in_specs, operands = [], []
    for a, b in pairs:
        in_specs += [a_spec, b_spec]
        operands += [a, b]
    block_bytes = len(pairs) * (_nbytes((tm, tk), a0.dtype) + _nbytes((tk, tn), b0.dtype))
    if residual is not None:
        in_specs.append(pl.BlockSpec((tm, tn), lambda i, j, kk: (i, j)))
        operands.append(residual)
        block_bytes += _nbytes((tm, tn), F32)
    if out3:
        out_spec = pl.BlockSpec((None, tm, tn), lambda i, j, kk: (j, i, 0))
        out_shape = jax.ShapeDtypeStruct((out3, m, tn), out_dtype)
    else:
        out_spec = pl.BlockSpec((tm, tn), lambda i, j, kk: (i, j))
        out_shape = jax.ShapeDtypeStruct((m, n), out_dtype)
    block_bytes += _nbytes((tm, tn), out_dtype) + _nbytes((tm, tn), F32)
    n_pairs = len(pairs)
    has_res = residual is not None

    def body(*refs):
        o_ref, acc = refs[-2], refs[-1]
        kk = pl.program_id(2)

        def product():
            part = None
            for p in range(n_pairs):
                d = lax.dot_general(refs[2 * p][...].astype(BF16), refs[2 * p + 1][...].astype(BF16),
                                    (dims, ((), ())), preferred_element_type=F32)
                part = d if part is None else part + d
            return part

        def finish(r):
            if scale != 1.0:
                r = r * scale
            if has_res:
                r = refs[2 * n_pairs][...] + r
            o_ref[...] = r.astype(out_dtype)

        if nk == 1:
            finish(product())
            return

        @pl.when(kk == 0)
        def _():
            acc[...] = product()

        if nk > 2:
            @pl.when(jnp.logical_and(kk > 0, kk < nk - 1))
            def _():
                acc[...] += product()

        @pl.when(kk == nk - 1)
        def _():
            finish(acc[...] + product())

    res = _run(body, name=name, grid=(m // tm, n // tn, nk), in_specs=in_specs, out_specs=[out_spec], out_shape=[out_shape],
               scratch_shapes=[pltpu.VMEM((tm, tn), F32)], operands=operands, block_bytes=block_bytes, rider=rider)
    return res[0] if rider is None else (res[0][0], res[1])


def _rmsnorm_fwd(x, g, name):
    t, d = x.shape
    tm = _rows(t, 512)

    def body(x_ref, g_ref, o_ref):
        xv = x_ref[...]
        r = lax.rsqrt(jnp.mean(xv * xv, axis=1, keepdims=True) + NORM_EPS)
        o_ref[...] = (xv * r * g_ref[...]).astype(BF16)

    row = pl.BlockSpec((tm, d), lambda i: (i, 0))
    return _pallas_call(
        body, out_shape=jax.ShapeDtypeStruct((t, d), BF16), grid=(t // tm,),
        in_specs=[row, pl.BlockSpec((1, d), lambda i: (0, 0))], out_specs=row,
        compiler_params=_cparams(("parallel",), 2 * _nbytes((tm, d), F32)), name=name)(x, g)


def _rms_grad(xv, g, dn, d):
    r = lax.rsqrt(jnp.mean(xv * xv, axis=1, keepdims=True) + NORM_EPS)
    u = dn * g
    s = jnp.sum(xv * u, axis=1, keepdims=True)
    dx = r * u - xv * (r * r * r) * (s * (1.0 / d))
    return dx, dn * xv * r


def _rmsnorm_bwd(x, g, dn, dres, name):
    t, d = x.shape
    tm = _rows(t, 256)

    def body(x_ref, g_ref, dn_ref, dres_ref, dx_ref, dxb_ref, dg_ref):
        dx, dg_rows = _rms_grad(x_ref[...], g_ref[...], dn_ref[...].astype(F32), d)
        dx = dres_ref[...] + dx
        dx_ref[...] = dx
        dxb_ref[...] = dx.astype(BF16)

        @pl.when(pl.program_id(0) == 0)
        def _():
            dg_ref[...] = jnp.zeros_like(dg_ref)

        dg_ref[...] += jnp.sum(dg_rows, axis=0, keepdims=True)

    row = pl.BlockSpec((tm, d), lambda i: (i, 0))
    vec = pl.BlockSpec((1, d), lambda i: (0, 0))
    return _pallas_call(
        body, out_shape=(jax.ShapeDtypeStruct((t, d), F32), jax.ShapeDtypeStruct((t, d), BF16), jax.ShapeDtypeStruct((1, d), F32)),
        grid=(t // tm,), in_specs=[row, vec, row, row], out_specs=(row, row, vec),
        compiler_params=_cparams(("arbitrary",), 5 * _nbytes((tm, d), F32)), name=name)(x, g, dn, dres)


def _final_loss(x, g, target, name):
    t, d = x.shape
    tm = _rows(t, 256)

    def body(x_ref, g_ref, t_ref, loss_ref, dx_ref, dxb_ref, dg_ref):
        xv, gv = x_ref[...], g_ref[...]
        r = lax.rsqrt(jnp.mean(xv * xv, axis=1, keepdims=True) + NORM_EPS)
        err = xv * r * gv - t_ref[...]
        dx, dg_rows = _rms_grad(xv, gv, err * (1.0 / d), d)
        dx_ref[...] = dx
        dxb_ref[...] = dx.astype(BF16)

        @pl.when(pl.program_id(0) == 0)
        def _():
            dg_ref[...] = jnp.zeros_like(dg_ref)
            loss_ref[...] = jnp.zeros_like(loss_ref)

        dg_ref[...] += jnp.sum(dg_rows, axis=0, keepdims=True)
        row_loss = jnp.sum(err * err, axis=1, keepdims=True) * (0.5 / d)
        loss_ref[...] += jnp.sum(row_loss, axis=0, keepdims=True)

    row = pl.BlockSpec((tm, d), lambda i: (i, 0))
    vec = pl.BlockSpec((1, d), lambda i: (0, 0))
    return _pallas_call(
        body, out_shape=(jax.ShapeDtypeStruct((1, 1), F32), jax.ShapeDtypeStruct((t, d), F32),
                         jax.ShapeDtypeStruct((t, d), BF16), jax.ShapeDtypeStruct((1, d), F32)),
        grid=(t // tm,), in_specs=[row, vec, row], out_specs=(pl.BlockSpec((1, 1), lambda i: (0, 0)), row, row, vec),
        compiler_params=_cparams(("arbitrary",), 4 * _nbytes((tm, d), F32)), name=name)(x, g, target)


def _sigmoid(x):
    return 0.5 * jnp.tanh(0.5 * x) + 0.5


def _ffn_up(n, wg, wu, name, rider=None):
    t, d = n.shape
    s, _, f = wg.shape
    tm, tk = _tile(t, 1024), _tile(d, 1024)
    nk = d // tk

    def body(n_ref, wg_ref, wu_ref, a_ref, b_ref, h_ref, acc_g, acc_u):
        kk = pl.program_id(2)

        def products():
            nv = n_ref[...]
            return jnp.dot(nv, wg_ref[...], preferred_element_type=F32), jnp.dot(nv, wu_ref[...], preferred_element_type=F32)

        def finish(a, b):
            a_ref[...] = a.astype(BF16)
            b_ref[...] = b.astype(BF16)
            h_ref[...] = (a * _sigmoid(a) * b).astype(BF16)

        if nk == 1:
            finish(*products())
            return

        @pl.when(kk == 0)
        def _():
            acc_g[...], acc_u[...] = products()

        if nk > 2:
            @pl.when(jnp.logical_and(kk > 0, kk < nk - 1))
            def _():
                pg, pu = products()
                acc_g[...] += pg
                acc_u[...] += pu

        @pl.when(kk == nk - 1)
        def _():
            pg, pu = products()
            finish(acc_g[...] + pg, acc_u[...] + pu)

    w_spec = pl.BlockSpec((None, tk, f), lambda i, j, kk: (j, kk, 0))
    o_spec = pl.BlockSpec((tm, f), lambda i, j, kk: (i, j))
    out = jax.ShapeDtypeStruct((t, s * f), BF16)
    block_bytes = _nbytes((tm, tk), BF16) + 2 * _nbytes((tk, f), BF16) + 3 * _nbytes((tm, f), BF16) + 2 * _nbytes((tm, f), F32)
    return _run(body, name=name, grid=(t // tm, s, nk),
                in_specs=[pl.BlockSpec((tm, tk), lambda i, j, kk: (i, kk)), w_spec, w_spec], out_specs=[o_spec, o_spec, o_spec],
                out_shape=[out, out, out], scratch_shapes=[pltpu.VMEM((tm, f), F32), pltpu.VMEM((tm, f), F32)],
                operands=[n, wg, wu], block_bytes=block_bytes, rider=rider)


def _ffn_bwd_act(dx, wd, a, b, name):
    t, d = dx.shape
    f = wd.shape[0]
    tm, tn, tk = _tile(t, 1024), _tile(f, 1536), _tile(d, 1024)
    nk = d // tk

    def body(dx_ref, wd_ref, a_ref, b_ref, da_ref, db_ref, acc):
        kk = pl.program_id(2)

        def product():
            return lax.dot_general(dx_ref[...], wd_ref[...], (((1,), (1,)), ((), ())), preferred_element_type=F32)

        def finish(r):
            dh = 0.5 * r
            av, bv = a_ref[...].astype(F32), b_ref[...].astype(F32)
            sg = _sigmoid(av)
            da_ref[...] = (dh * bv * (sg * (1.0 + av * (1.0 - sg)))).astype(BF16)
            db_ref[...] = (dh * (av * sg)).astype(BF16)

        if nk == 1:
            finish(product())
            return

        @pl.when(kk == 0)
        def _():
            acc[...] = product()

        if nk > 2:
            @pl.when(jnp.logical_and(kk > 0, kk < nk - 1))
            def _():
                acc[...] += product()

        @pl.when(kk == nk - 1)
        def _():
            finish(acc[...] + product())

    act = pl.BlockSpec((tm, tn), lambda i, j, kk: (i, j))
    out = jax.ShapeDtypeStruct((t, f), BF16)
    block_bytes = _nbytes((tm, tk), BF16) + _nbytes((tn, tk), BF16) + 4 * _nbytes((tm, tn), BF16) + _nbytes((tm, tn), F32)
    return _pallas_call(
        body, out_shape=(out, out), grid=(t // tm, f // tn, nk),
        in_specs=[pl.BlockSpec((tm, tk), lambda i, j, kk: (i, kk)), pl.BlockSpec((tn, tk), lambda i, j, kk: (j, kk)),
                  act, act],
        out_specs=(act, act), scratch_shapes=[pltpu.VMEM((tm, tn), F32)],
        compiler_params=_cparams(("parallel", "parallel", "arbitrary"), block_bytes), name=name)(dx, wd, a, b)


AXIS = dict(BIG)


def _full(wb, n):
    _, _, r, ccols = wb[n].shape
    return wb[n].reshape(N_CHIPS, 2 * r, ccols) if AXIS[n] == 1 else wb[n].reshape(N_CHIPS * 2 * r, ccols)


def _gather(wb, specs):
    items, names = [], []
    for s in specs:
        n, r0, r1 = (s, 0, wb[s].shape[2]) if isinstance(s, str) else s
        items.append((wb[n], r0, r1))
        if n not in names:
            names.append(n)
    return _gather_rider(items), names


def _landed(wb, names, results):
    for n, r in zip(names, results):
        wb[n] = r


def _reduce_first(grads, names, wb, c_idx):
    g4 = [g.reshape(wb[n].shape) for g, n in zip(grads, names)]
    from_sibling = _sibling_exchange(g4, "rs_sibling_" + names[0])
    return [_sibling_sum(a, b, c_idx, f"rs_sum1_{n}") for a, b, n in zip(g4, from_sibling, names)]


def _ffn_forward(x, gain, wb, tag, up_specs, down_specs):
    n = _rmsnorm_fwd(x, gain, f"{tag}_norm")
    rider, names = _gather(wb, up_specs)
    (a, b, h), got = _ffn_up(n, _full(wb, f"{tag}_w_gate"), _full(wb, f"{tag}_w_up"), f"{tag}_up", rider=rider)
    _landed(wb, names, got)
    down = dict(scale=0.5, residual=x, caps=(1024, 1024, 1536))
    if down_specs:
        rider, names = _gather(wb, down_specs)
        x_next, got = _matmul([(h, _full(wb, f"{tag}_w_down"))], 'nn', F32, f"{tag}_down", rider=rider, **down)
        _landed(wb, names, got)
    else:
        x_next = _matmul([(h, _full(wb, f"{tag}_w_down"))], 'nn', F32, f"{tag}_down", **down)
    return x_next, (n, a, b, h)


def _ffn_backward(x, gain, wb, saved, dx_next, dx_next_b, c_idx, tag):
    n, a, b, h = saved
    wg, wu, wd = (f"{tag}_w_gate", f"{tag}_w_up", f"{tag}_w_down")
    da, db = _ffn_bwd_act(dx_next_b, _full(wb, wd), a, b, f"{tag}_bwd_act")
    g_wd = _matmul([(h, dx_next_b)], 'tn', BF16, f"{tag}_dwd", scale=0.5, caps=(1536, 2048, 1024))
    (p_wd,) = _reduce_first([g_wd], [wd], wb, c_idx)
    g_wg, (r_wd,) = _matmul([(n, da)], 'tn', BF16, f"{tag}_dwg", out3=N_CHIPS, caps=(2048, 1024, 1024), rider=_scatter_rider([p_wd]))
    g_wu = _matmul([(n, db)], 'tn', BF16, f"{tag}_dwu", out3=N_CHIPS, caps=(2048, 1024, 1024))
    p_wg, p_wu = _reduce_first([g_wg, g_wu], [wg, wu], wb, c_idx)
    dn, (r_wg, r_wu) = _matmul([(da, _full(wb, wg)), (db, _full(wb, wu))], 'nt', F32, f"{tag}_dn", b3=True,
                               rider=_scatter_rider([p_wg, p_wu]))
    dx, dx_b, g_gain = _rmsnorm_bwd(x, gain, dn, dx_next, f"{tag}_norm_bwd")
    return dx, dx_b, g_gain, {wg: (p_wg, r_wg), wu: (p_wu, r_wu), wd: (p_wd, r_wd)}


def _rope_tables(seq):
    half = ROPE_DIM // 2
    inv_freq = ROPE_THETA ** (-jnp.arange(0, ROPE_DIM, 2, dtype=F32) / ROPE_DIM)
    ang = jnp.arange(seq).astype(F32)[:, None] * inv_freq[None, :]
    cos, sin = jnp.cos(ang), jnp.sin(ang)
    zeros = lambda w: jnp.zeros((seq, w), F32)
    c = jnp.concatenate([cos, cos, jnp.ones((seq, HEAD_DIM - ROPE_DIM), F32)], axis=1)
    s_up = jnp.concatenate([-sin, zeros(HEAD_DIM - half)], axis=1)
    s_dn = jnp.concatenate([zeros(half), sin, zeros(HEAD_DIM - ROPE_DIM)], axis=1)
    return c, s_up, s_dn


def _rotate(xv, cv, uv, dv):
    half = ROPE_DIM // 2
    return xv * cv + pltpu.roll(xv, HEAD_DIM - half, 1) * uv + pltpu.roll(xv, half, 1) * dv


def _stage(tm):
    return pltpu.VMEM((HEADS_PER_GROUP, tm, HEAD_DIM), F32)


def _to_groups(stage, o_ref, dil):
    rows = stage.shape[1] // dil
    for r in range(dil):
        for h in range(HEADS_PER_GROUP):
            col = r * GROUP_WIDTH + h * HEAD_DIM
            o_ref[:, col:col + HEAD_DIM] = stage[h, pl.ds(r, rows, stride=dil), :].astype(o_ref.dtype)


def _from_groups(g_ref, stage, dil):
    rows = stage.shape[1] // dil
    for r in range(dil):
        for h in range(HEADS_PER_GROUP):
            col = r * GROUP_WIDTH + h * HEAD_DIM
            stage[h, pl.ds(r, rows, stride=dil), :] = g_ref[:, col:col + HEAD_DIM].astype(F32)


def _group_spec(tm, dil):
    return pl.BlockSpec((tm // dil, dil * GROUP_WIDTH), lambda i: (i, 0))


def _group_shape(t, dil, dtype):
    return jax.ShapeDtypeStruct((t // dil, dil * GROUP_WIDTH), dtype)


def _rope_fwd(proj, tables, name):
    t = proj.shape[0]
    tm = _rows(t, 512)
    att_w = N_GROUPS * GROUP_WIDTH
    dilated = [(gi, dil) for gi, dil in enumerate(DILATIONS) if dil > 1]

    def body(x_ref, c_ref, up_ref, dn_ref, qk0_ref, *rest):
        outs, stage = rest[:-1], rest[-1]
        cv, uv, dv = c_ref[...], up_ref[...], dn_ref[...]
        for part in range(2):
            for gi, dil in enumerate(DILATIONS):
                for h in range(HEADS_PER_GROUP):
                    col = part * att_w + gi * GROUP_WIDTH + h * HEAD_DIM
                    y = _rotate(x_ref[:, col:col + HEAD_DIM].astype(F32), cv, uv, dv)
                    if dil == 1:
                        qk0_ref[:, part * GROUP_WIDTH + h * HEAD_DIM:part * GROUP_WIDTH + (h + 1) * HEAD_DIM] = y.astype(BF16)
                    else:
                        stage[h] = y
                if dil > 1:
                    _to_groups(stage, outs[3 * dilated.index((gi, dil)) + part], dil)
        for n, (gi, dil) in enumerate(dilated):
            col = 2 * att_w + gi * GROUP_WIDTH
            for h in range(HEADS_PER_GROUP):
                stage[h] = x_ref[:, col + h * HEAD_DIM:col + (h + 1) * HEAD_DIM].astype(F32)
            _to_groups(stage, outs[3 * n + 2], dil)

    tab = pl.BlockSpec((tm, HEAD_DIM), lambda i: (i, 0))
    out_shape = [jax.ShapeDtypeStruct((t, 2 * GROUP_WIDTH), BF16)]
    out_specs = [pl.BlockSpec((tm, 2 * GROUP_WIDTH), lambda i: (i, 0))]
    for _, dil in dilated:
        out_shape += [_group_shape(t, dil, BF16)] * 3
        out_specs += [_group_spec(tm, dil)] * 3
    res = _pallas_call(
        body, out_shape=out_shape, grid=(t // tm,),
        in_specs=[pl.BlockSpec((tm, 3 * att_w), lambda i: (i, 0)), tab, tab, tab], out_specs=out_specs,
        scratch_shapes=[_stage(tm)],
        compiler_params=_cparams(("parallel",), 4 * _nbytes((tm, 3 * att_w), BF16)), name=name)(proj, *tables)
    return res[0], [tuple(res[1 + 3 * n:4 + 3 * n]) for n in range(len(dilated))]


def _rope_bwd(dq0, dk0, dv0, grouped, tables, name):
    t = dq0.shape[0]
    tm = _rows(t, 512)
    att_w = N_GROUPS * GROUP_WIDTH
    dilated = [(gi, dil) for gi, dil in enumerate(DILATIONS) if dil > 1]
    c, s_up, s_dn = tables

    def body(c_ref, up_ref, dn_ref, dq0_ref, dk0_ref, dv0_ref, *rest):
        g_refs, o_ref, stage = rest[:-2], rest[-2], rest[-1]
        cv, uv, dv = c_ref[...], -up_ref[...], -dn_ref[...]
        for part, first in enumerate((dq0_ref, dk0_ref)):
            for gi, dil in enumerate(DILATIONS):
                if dil > 1:
                    _from_groups(g_refs[3 * dilated.index((gi, dil)) + part], stage, dil)
                for h in range(HEADS_PER_GROUP):
                    sl = slice(h * HEAD_DIM, (h + 1) * HEAD_DIM)
                    xv = first[:, sl].astype(F32) if dil == 1 else stage[h]
                    col = part * att_w + gi * GROUP_WIDTH + h * HEAD_DIM
                    o_ref[:, col:col + HEAD_DIM] = _rotate(xv, cv, uv, dv).astype(BF16)
        for gi, dil in enumerate(DILATIONS):
            col = 2 * att_w + gi * GROUP_WIDTH
            if dil == 1:
                o_ref[:, col:col + GROUP_WIDTH] = dv0_ref[...]
            else:
                _from_groups(g_refs[3 * dilated.index((gi, dil)) + 2], stage, dil)
                for h in range(HEADS_PER_GROUP):
                    o_ref[:, col + h * HEAD_DIM:col + (h + 1) * HEAD_DIM] = stage[h].astype(BF16)

    tab = pl.BlockSpec((tm, HEAD_DIM), lambda i: (i, 0))
    nat = pl.BlockSpec((tm, GROUP_WIDTH), lambda i: (i, 0))
    in_specs, operands = [tab, tab, tab, nat, nat, nat], [c, s_up, s_dn, dq0, dk0, dv0]
    for (_, dil), arrs in zip(dilated, grouped):
        in_specs += [_group_spec(tm, dil)] * 3
        operands += list(arrs)
    return _pallas_call(
        body, out_shape=jax.ShapeDtypeStruct((t, 3 * att_w), BF16), grid=(t // tm,), in_specs=in_specs,
        out_specs=pl.BlockSpec((tm, 3 * att_w), lambda i: (i, 0)), scratch_shapes=[_stage(tm)],
        compiler_params=_cparams(("parallel",), 4 * _nbytes((tm, 3 * att_w), BF16)), name=name)(*operands)


def _regroup(arrs, name):
    t = arrs[0].shape[0]
    tm = _rows(t, 512)
    dilated = [dil for dil in DILATIONS if dil > 1]
    n_in = len(arrs)

    def body(*refs):
        ins, outs, stage = refs[:n_in], refs[n_in:-1], refs[-1]
        for j, x_ref in enumerate(ins):
            for h in range(HEADS_PER_GROUP):
                stage[h] = x_ref[:, h * HEAD_DIM:(h + 1) * HEAD_DIM]
            for n, dil in enumerate(dilated):
                _to_groups(stage, outs[n * n_in + j], dil)

    nat = pl.BlockSpec((tm, GROUP_WIDTH), lambda i: (i, 0))
    res = _pallas_call(
        body, out_shape=[_group_shape(t, dil, F32) for dil in dilated for _ in arrs], grid=(t // tm,),
        in_specs=[nat] * n_in, out_specs=[_group_spec(tm, dil) for dil in dilated for _ in arrs], scratch_shapes=[_stage(tm)],
        compiler_params=_cparams(("parallel",), 3 * n_in * _nbytes((tm, GROUP_WIDTH), F32)), name=name)(*arrs)
    return [tuple(res[n * n_in:(n + 1) * n_in]) for n in range(len(dilated))]


def _query_mask(has_prev):
    qi = lax.broadcasted_iota(jnp.int32, (ATT_BLOCK, 2 * ATT_BLOCK), 0)
    col = lax.broadcasted_iota(jnp.int32, (ATT_BLOCK, 2 * ATT_BLOCK), 1)
    prev = jnp.logical_and(jnp.logical_and(col < ATT_BLOCK, col >= qi), has_prev)
    return jnp.logical_or(prev, jnp.logical_and(col >= ATT_BLOCK, col - ATT_BLOCK <= qi))


def _key_mask(has_next):
    row = lax.broadcasted_iota(jnp.int32, (2 * ATT_BLOCK, ATT_BLOCK), 0)
    kj = lax.broadcasted_iota(jnp.int32, (2 * ATT_BLOCK, ATT_BLOCK), 1)
    nxt = jnp.logical_and(jnp.logical_and(row >= ATT_BLOCK, kj >= row - ATT_BLOCK), has_next)
    return jnp.logical_or(nxt, jnp.logical_and(row < ATT_BLOCK, kj <= row))


def _scores(q, k):
    return lax.dot_general(q, k, (((1,), (1,)), ((), ())), preferred_element_type=F32) * (HEAD_DIM ** -0.5)


def _att_fwd(q, k, v, offs, dil, name):
    qo, ko, vo = offs
    length = q.shape[0]
    nb = length // ATT_BLOCK

    def body(q_ref, kp_ref, kc_ref, vp_ref, vc_ref, o_ref, lse_ref):
        mask = _query_mask(pl.program_id(1) > 0)
        heads = [slice(h * HEAD_DIM, (h + 1) * HEAD_DIM) for h in range(HEADS_PER_GROUP)]
        ks = [jnp.concatenate([kp_ref[:, sl], kc_ref[:, sl]], axis=0) for sl in heads]
        vs = [jnp.concatenate([vp_ref[:, sl], vc_ref[:, sl]], axis=0) for sl in heads]
        ss = [jnp.where(mask, _scores(q_ref[:, sl], kv), MASKED) for sl, kv in zip(heads, ks)]
        ms = [jnp.max(s, axis=1, keepdims=True) for s in ss]
        ps = [jnp.exp(s - m) for s, m in zip(ss, ms)]
        ls = [jnp.sum(p, axis=1, keepdims=True) for p in ps]
        accs = [jnp.dot(p.astype(BF16), vv, preferred_element_type=F32) for p, vv in zip(ps, vs)]
        for sl, acc, m, l in zip(heads, accs, ms, ls):
            o_ref[:, sl] = acc / l
            lse_ref[:, sl] = jnp.broadcast_to(m + jnp.log(l), (ATT_BLOCK, HEAD_DIM))

    def spec(off, prev):
        if prev:
            return pl.BlockSpec((ATT_BLOCK, GROUP_WIDTH), lambda r, n: (jnp.maximum(n - 1, 0), off + r))
        return pl.BlockSpec((ATT_BLOCK, GROUP_WIDTH), lambda r, n: (n, off + r))

    out = jax.ShapeDtypeStruct((length, dil * GROUP_WIDTH), F32)
    o_spec = pl.BlockSpec((ATT_BLOCK, GROUP_WIDTH), lambda r, n: (n, r))
    return _pallas_call(
        body, out_shape=(out, out), grid=(dil, nb),
        in_specs=[spec(qo, False), spec(ko, True), spec(ko, False), spec(vo, True), spec(vo, False)],
        out_specs=(o_spec, o_spec),
        compiler_params=_cparams(("parallel", "parallel"), 8 * _nbytes((ATT_BLOCK, GROUP_WIDTH), F32)), name=name)(q, k, k, v, v)


def _att_combine(outs, lses, name):
    t = outs[0].shape[0] * DILATIONS[0]
    tm = _rows(t, 512)

    def body(*refs):
        o_refs, l_refs = refs[:N_GROUPS], refs[N_GROUPS:2 * N_GROUPS]
        ob_ref, of_ref, lse_ref = refs[2 * N_GROUPS:2 * N_GROUPS + 3]
        stages = list(refs[2 * N_GROUPS + 3:])
        staged = []
        for o_ref, l_ref, dil in zip(o_refs, l_refs, DILATIONS):
            if dil > 1:
                so, sl = stages.pop(), stages.pop()
                _from_groups(o_ref, so, dil)
                _from_groups(l_ref, sl, dil)
                staged.append((so, sl))
            else:
                staged.append(None)
        for h in range(HEADS_PER_GROUP):
            hs = slice(h * HEAD_DIM, (h + 1) * HEAD_DIM)
            os_ = [o_ref[:, hs] if st is None else st[0][h] for o_ref, st in zip(o_refs, staged)]
            ls = [l_ref[:, hs] if st is None else st[1][h] for l_ref, st in zip(l_refs, staged)]
            m = functools.reduce(jnp.maximum, ls)
            ws = [jnp.exp(l - m) for l in ls]
            den = functools.reduce(jnp.add, ws)
            num = functools.reduce(jnp.add, [w * o for w, o in zip(ws, os_)])
            o = num / den
            ob_ref[:, hs] = o.astype(BF16)
            of_ref[:, hs] = o
            lse_ref[:, hs] = m + jnp.log(den)

    blk = pl.BlockSpec((tm, GROUP_WIDTH), lambda i: (i, 0))
    specs = [blk if dil == 1 else _group_spec(tm, dil) for dil in DILATIONS]
    f32 = jax.ShapeDtypeStruct((t, GROUP_WIDTH), F32)
    n_stage = 2 * sum(dil > 1 for dil in DILATIONS)
    return _pallas_call(
        body, out_shape=(jax.ShapeDtypeStruct((t, GROUP_WIDTH), BF16), f32, f32), grid=(t // tm,),
        in_specs=specs * 2, out_specs=(blk, blk, blk), scratch_shapes=[_stage(tm)] * n_stage,
        compiler_params=_cparams(("parallel",), 13 * _nbytes((tm, GROUP_WIDTH), F32)), name=name)(*outs, *lses)


def _att_delta(do, o, name):
    t = o.shape[0]
    tm = _rows(t, 512)

    def body(do_ref, o_ref, d_ref):
        for h in range(HEADS_PER_GROUP):
            sl = slice(h * HEAD_DIM, (h + 1) * HEAD_DIM)
            s = jnp.sum(do_ref[:, sl] * o_ref[:, sl], axis=1, keepdims=True)
            d_ref[:, sl] = jnp.broadcast_to(s, (tm, HEAD_DIM))

    blk = pl.BlockSpec((tm, GROUP_WIDTH), lambda i: (i, 0))
    return _pallas_call(
        body, out_shape=jax.ShapeDtypeStruct((t, GROUP_WIDTH), F32), grid=(t // tm,), in_specs=[blk, blk], out_specs=blk,
        compiler_params=_cparams(("parallel",), 3 * _nbytes((tm, GROUP_WIDTH), F32)), name=name)(do, o)


def _att_bwd_dq(q, k, v, do, lse, delta, offs, dil, name):
    qo, ko, vo = offs
    length = q.shape[0]
    nb = length // ATT_BLOCK
    scale = HEAD_DIM ** -0.5

    def body(q_ref, kp_ref, kc_ref, vp_ref, vc_ref, do_ref, lse_ref, dl_ref, dq_ref):
        mask = _query_mask(pl.program_id(1) > 0)
        heads = [slice(h * HEAD_DIM, (h + 1) * HEAD_DIM) for h in range(HEADS_PER_GROUP)]
        wide = lambda ref, sl: jnp.concatenate([ref[:, sl], ref[:, sl]], axis=1)
        ks = [jnp.concatenate([kp_ref[:, sl], kc_ref[:, sl]], axis=0) for sl in heads]
        vs = [jnp.concatenate([vp_ref[:, sl], vc_ref[:, sl]], axis=0) for sl in heads]
        ps = [jnp.exp(jnp.where(mask, _scores(q_ref[:, sl], kv), MASKED) - wide(lse_ref, sl)) for sl, kv in zip(heads, ks)]
        dps = [lax.dot_general(do_ref[:, sl].astype(BF16), vv, (((1,), (1,)), ((), ())), preferred_element_type=F32)
               for sl, vv in zip(heads, vs)]
        dss = [(p * (dp - wide(dl_ref, sl)) * scale).astype(BF16) for sl, p, dp in zip(heads, ps, dps)]
        dqs = [jnp.dot(ds, kv, preferred_element_type=F32) for ds, kv in zip(dss, ks)]
        for sl, dq in zip(heads, dqs):
            dq_ref[:, sl] = dq.astype(BF16)

    def spec(off, prev):
        if prev:
            return pl.BlockSpec((ATT_BLOCK, GROUP_WIDTH), lambda r, n: (jnp.maximum(n - 1, 0), off + r))
        return pl.BlockSpec((ATT_BLOCK, GROUP_WIDTH), lambda r, n: (n, off + r))

    own = pl.BlockSpec((ATT_BLOCK, GROUP_WIDTH), lambda r, n: (n, r))
    return _pallas_call(
        body, out_shape=jax.ShapeDtypeStruct((length, dil * GROUP_WIDTH), BF16), grid=(dil, nb),
        in_specs=[spec(qo, False), spec(ko, True), spec(ko, False), spec(vo, True), spec(vo, False), own, own, own],
        out_specs=own,
        compiler_params=_cparams(("parallel", "parallel"), 10 * _nbytes((ATT_BLOCK, GROUP_WIDTH), F32)),
        name=name)(q, k, k, v, v, do, lse, delta)


def _att_bwd_dkv(q, k, v, do, lse, delta, offs, dil, name):
    qo, ko, vo = offs
    length = q.shape[0]
    nb = length // ATT_BLOCK
    scale = HEAD_DIM ** -0.5

    def body(k_ref, v_ref, qc_ref, qn_ref, doc_ref, don_ref, lsec_ref, lsen_ref, dlc_ref, dln_ref, dk_ref, dv_ref):
        mask = _key_mask(pl.program_id(1) < nb - 1)
        heads = [slice(h * HEAD_DIM, (h + 1) * HEAD_DIM) for h in range(HEADS_PER_GROUP)]
        both = lambda cur, nxt, sl: jnp.concatenate([cur[:, sl], nxt[:, sl]], axis=0)
        qs = [both(qc_ref, qn_ref, sl) for sl in heads]
        dos = [both(doc_ref, don_ref, sl).astype(BF16) for sl in heads]
        ps = [jnp.exp(jnp.where(mask, _scores(qv, k_ref[:, sl]), MASKED) - both(lsec_ref, lsen_ref, sl)) for sl, qv in zip(heads, qs)]
        dps = [lax.dot_general(dov, v_ref[:, sl], (((1,), (1,)), ((), ())), preferred_element_type=F32) for sl, dov in zip(heads, dos)]
        dss = [(p * (dp - both(dlc_ref, dln_ref, sl)) * scale).astype(BF16) for sl, p, dp in zip(heads, ps, dps)]
        dvs = [lax.dot_general(p.astype(BF16), dov, (((0,), (0,)), ((), ())), preferred_element_type=F32) for p, dov in zip(ps, dos)]
        dks = [lax.dot_general(ds, qv, (((0,), (0,)), ((), ())), preferred_element_type=F32) for ds, qv in zip(dss, qs)]
        for sl, dk, dv in zip(heads, dks, dvs):
            dk_ref[:, sl] = dk.astype(BF16)
            dv_ref[:, sl] = dv.astype(BF16)

    def spec(off, nxt):
        if nxt:
            return pl.BlockSpec((ATT_BLOCK, GROUP_WIDTH), lambda r, n: (jnp.minimum(n + 1, nb - 1), off + r))
        return pl.BlockSpec((ATT_BLOCK, GROUP_WIDTH), lambda r, n: (n, off + r))

    own = pl.BlockSpec((ATT_BLOCK, GROUP_WIDTH), lambda r, n: (n, r))
    out = jax.ShapeDtypeStruct((length, dil * GROUP_WIDTH), BF16)
    return _pallas_call(
        body, out_shape=(out, out), grid=(dil, nb),
        in_specs=[spec(ko, False), spec(vo, False), spec(qo, False), spec(qo, True), spec(0, False), spec(0, True),
                  spec(0, False), spec(0, True), spec(0, False), spec(0, True)],
        out_specs=(own, own),
        compiler_params=_cparams(("parallel", "parallel"), 12 * _nbytes((ATT_BLOCK, GROUP_WIDTH), F32)),
        name=name)(k, v, q, q, do, do, lse, lse, delta, delta)


def _gelu(x):
    return 0.5 * x * (1.0 + lax.erf(x * (2.0 ** -0.5)))


def _gelu_grad(x):
    return 0.5 * (1.0 + lax.erf(x * (2.0 ** -0.5))) + x * jnp.exp(-0.5 * x * x) * ((2.0 * jnp.pi) ** -0.5)


def _sg_normed(vs, lg, lb):
    gv = _gelu(vs)
    mu = jnp.mean(gv, axis=1, keepdims=True)
    xc = gv - mu
    rstd = lax.rsqrt(jnp.mean(xc * xc, axis=1, keepdims=True) + LN_EPS)
    z = xc * rstd
    return z, rstd, z * lg + lb


def _sg_tril():
    row = lax.broadcasted_iota(jnp.int32, (SG_CHUNK, SG_CHUNK), 0)
    col = lax.broadcasted_iota(jnp.int32, (SG_CHUNK, SG_CHUNK), 1)
    return row >= col


def _sg_fwd(proj, u_blk, vs_blk, lg, lb, sg_w, bias, name):
    t = proj.shape[0]
    width = SG_GROUPS * SG_GROUP_DIM

    def body(u_ref, vs_ref, lg_ref, lb_ref, w_ref, bias_ref, o_ref):
        _, _, vn = _sg_normed(vs_ref[...].astype(F32), lg_ref[...], lb_ref[...])
        vn = vn.astype(BF16)
        tril = _sg_tril()
        for g in range(SG_GROUPS):
            sl = slice(g * SG_GROUP_DIM, (g + 1) * SG_GROUP_DIM)
            w = jnp.where(tril, w_ref[g], 0.0).astype(BF16)
            sp = jnp.dot(w, vn[:, sl], preferred_element_type=F32) + bias_ref[:, sl]
            o_ref[:, sl] = (_gelu(u_ref[:, sl].astype(F32)) * sp).astype(BF16)

    vec = pl.BlockSpec((1, width), lambda i: (0, 0))
    return _pallas_call(
        body, out_shape=jax.ShapeDtypeStruct((t, width), BF16), grid=(t // SG_CHUNK,),
        in_specs=[pl.BlockSpec((SG_CHUNK, width), lambda i: (i, u_blk)), pl.BlockSpec((SG_CHUNK, width), lambda i: (i, vs_blk)),
                  vec, vec, pl.BlockSpec((SG_GROUPS, SG_CHUNK, SG_CHUNK), lambda i: (0, 0, 0)),
                  pl.BlockSpec((SG_CHUNK, width), lambda i: (0, 0))],
        out_specs=pl.BlockSpec((SG_CHUNK, width), lambda i: (i, 0)),
        compiler_params=_cparams(("parallel",), 8 * _nbytes((SG_CHUNK, width), F32)), name=name)(proj, proj, lg, lb, sg_w, bias)


def _sg_bwd(proj, u_blk, vs_blk, dsu, lg, lb, sg_w, bias, name):
    t = proj.shape[0]
    width = SG_GROUPS * SG_GROUP_DIM

    def body(u_ref, vs_ref, dsu_ref, lg_ref, lb_ref, w_ref, bias_ref, du_ref, dvs_ref, dw_ref, dbias_ref, dlg_ref, dlb_ref):
        @pl.when(pl.program_id(0) == 0)
        def _():
            dw_ref[...] = jnp.zeros_like(dw_ref)
            dbias_ref[...] = jnp.zeros_like(dbias_ref)
            dlg_ref[...] = jnp.zeros_like(dlg_ref)
            dlb_ref[...] = jnp.zeros_like(dlb_ref)

        vs = vs_ref[...].astype(F32)
        z, rstd, vn = _sg_normed(vs, lg_ref[...], lb_ref[...])
        vn = vn.astype(BF16)
        tril = _sg_tril()
        dvn = []
        for g in range(SG_GROUPS):
            sl = slice(g * SG_GROUP_DIM, (g + 1) * SG_GROUP_DIM)
            w = jnp.where(tril, w_ref[g], 0.0).astype(BF16)
            vg = vn[:, sl]
            sp = jnp.dot(w, vg, preferred_element_type=F32) + bias_ref[:, sl]
            uv = u_ref[:, sl].astype(F32)
            dsu_g = dsu_ref[:, sl].astype(F32)
            du_ref[:, sl] = (dsu_g * sp * _gelu_grad(uv)).astype(BF16)
            dsp = dsu_g * _gelu(uv)
            dsp_b = dsp.astype(BF16)
            dw = lax.dot_general(dsp_b, vg, (((1,), (1,)), ((), ())), preferred_element_type=F32)
            dw_ref[g] += jnp.where(tril, dw, 0.0)
            dbias_ref[:, sl] += jnp.broadcast_to(jnp.sum(dsp, axis=1, keepdims=True), (SG_CHUNK, SG_GROUP_DIM))
            dvn.append(lax.dot_general(w, dsp_b, (((0,), (0,)), ((), ())), preferred_element_type=F32))
        dvn = jnp.concatenate(dvn, axis=1)
        dlg_ref[...] += jnp.sum(dvn * z, axis=0, keepdims=True)
        dlb_ref[...] += jnp.sum(dvn, axis=0, keepdims=True)
        dz = dvn * lg_ref[...]
        dgv = rstd * (dz - jnp.mean(dz, axis=1, keepdims=True) - z * jnp.mean(dz * z, axis=1, keepdims=True))
        dvs_ref[...] = (dgv * _gelu_grad(vs)).astype(BF16)

    vec = pl.BlockSpec((1, width), lambda i: (0, 0))
    row = pl.BlockSpec((SG_CHUNK, width), lambda i: (i, 0))
    fixed = pl.BlockSpec((SG_CHUNK, width), lambda i: (0, 0))
    w_spec = pl.BlockSpec((SG_GROUPS, SG_CHUNK, SG_CHUNK), lambda i: (0, 0, 0))
    act = jax.ShapeDtypeStruct((t, width), BF16)
    return _pallas_call(
        body,
        out_shape=(act, act, jax.ShapeDtypeStruct((SG_GROUPS, SG_CHUNK, SG_CHUNK), F32),
                   jax.ShapeDtypeStruct((SG_CHUNK, width), F32), jax.ShapeDtypeStruct((1, width), F32),
                   jax.ShapeDtypeStruct((1, width), F32)),
        grid=(t // SG_CHUNK,),
        in_specs=[pl.BlockSpec((SG_CHUNK, width), lambda i: (i, u_blk)), pl.BlockSpec((SG_CHUNK, width), lambda i: (i, vs_blk)),
                  row, vec, vec, w_spec, fixed],
        out_specs=(row, row, w_spec, fixed, vec, vec),
        compiler_params=_cparams(("arbitrary",), 14 * _nbytes((SG_CHUNK, width), F32)),
        name=name)(proj, proj, dsu, lg, lb, sg_w, bias)


def _gate_fwd(proj, ga_blk, gs_blk, y_att, y_sg, name):
    t, d = y_att.shape
    tm, tn = _rows(t, 512), _tile(d, GROUP_WIDTH)

    def body(ga_ref, gs_ref, ya_ref, ys_ref, o_ref):
        o_ref[...] = (_sigmoid(ga_ref[...].astype(F32)) * ya_ref[...].astype(F32)
                      + _sigmoid(gs_ref[...].astype(F32)) * ys_ref[...].astype(F32)).astype(BF16)

    own = pl.BlockSpec((tm, tn), lambda i, j: (i, j))
    return _pallas_call(
        body, out_shape=jax.ShapeDtypeStruct((t, d), BF16), grid=(t // tm, d // tn),
        in_specs=[pl.BlockSpec((tm, tn), lambda i, j: (i, ga_blk + j)), pl.BlockSpec((tm, tn), lambda i, j: (i, gs_blk + j)),
                  own, own],
        out_specs=own, compiler_params=_cparams(("parallel", "parallel"), 6 * _nbytes((tm, tn), F32)),
        name=name)(proj, proj, y_att, y_sg)


def _gate_bwd(proj, ga_blk, gs_blk, y_att, y_sg, dmerged, name):
    t, d = y_att.shape
    tm, tn = _rows(t, 512), _tile(d, GROUP_WIDTH)

    def body(ga_ref, gs_ref, ya_ref, ys_ref, dm_ref, dya_ref, dys_ref, dga_ref, dgs_ref):
        dm = dm_ref[...].astype(F32)
        for g_ref, y_ref, dy_ref, dg_ref in ((ga_ref, ya_ref, dya_ref, dga_ref), (gs_ref, ys_ref, dys_ref, dgs_ref)):
            sg = _sigmoid(g_ref[...].astype(F32))
            dy_ref[...] = (dm * sg).astype(BF16)
            dg_ref[...] = (dm * y_ref[...].astype(F32) * sg * (1.0 - sg)).astype(BF16)

    own = pl.BlockSpec((tm, tn), lambda i, j: (i, j))
    out = jax.ShapeDtypeStruct((t, d), BF16)
    return _pallas_call(
        body, out_shape=(out, out, out, out), grid=(t // tm, d // tn),
        in_specs=[pl.BlockSpec((tm, tn), lambda i, j: (i, ga_blk + j)), pl.BlockSpec((tm, tn), lambda i, j: (i, gs_blk + j)),
                  own, own, own],
        out_specs=(own, own, own, own), compiler_params=_cparams(("parallel", "parallel"), 10 * _nbytes((tm, tn), F32)),
        name=name)(proj, proj, y_att, y_sg, dmerged)


def _mixer_forward(x, wb, small, in_specs, sg_specs, out_specs):
    t, d = x.shape
    att_w = N_GROUPS * GROUP_WIDTH
    sg_w = SG_GROUPS * SG_GROUP_DIM
    n = _rmsnorm_fwd(x, small['mix_norm'], "mix_norm")
    rider, names = _gather(wb, in_specs)
    proj, got = _matmul([(n, _full(wb, 'w_in'))], 'nn', BF16, "mix_in", b3=True, caps=(1024, 1024, 1024), rider=rider)
    _landed(wb, names, got)
    tables = _rope_tables(t)
    qk0, grouped = _rope_fwd(proj, tables, "mix_rope")
    qkv = [(qk0, qk0, proj, (0, 1, 2 * N_GROUPS))] + [g + ((0, 0, 0),) for g in grouped]
    outs, lses = zip(*[_att_fwd(*args, dil, f"att_fwd{gi}") for gi, (args, dil) in enumerate(zip(qkv, DILATIONS))])
    o_b, o_f, lse = _att_combine(outs, lses, "att_combine")
    y_att = _matmul([(o_b, _full(wb, 'w_att_out'))], 'nn', BF16, "mix_att_out", b3=True)
    bias = jnp.repeat(small['sg_b'].T, SG_GROUP_DIM, axis=1)
    u_blk, vs_blk = 3 * att_w // sg_w, 3 * att_w // sg_w + 1
    su = _sg_fwd(proj, u_blk, vs_blk, small['sg_ln_g'], small['sg_ln_b'], small['sg_w'], bias, "sg_fwd")
    rider, names = _gather(wb, sg_specs)
    y_sg, got = _matmul([(su, _full(wb, 'w_sg_out'))], 'nn', BF16, "mix_sg_out", b3=True, rider=rider)
    _landed(wb, names, got)
    ga_blk = (3 * att_w + 2 * sg_w) // _tile(d, GROUP_WIDTH)
    gs_blk = ga_blk + d // _tile(d, GROUP_WIDTH)
    merged = _gate_fwd(proj, ga_blk, gs_blk, y_att, y_sg, "gate_fwd")
    rider, names = _gather(wb, out_specs)
    x_next, got = _matmul([(merged, _full(wb, 'w_out'))], 'nn', F32, "mix_out", residual=x, rider=rider)
    _landed(wb, names, got)
    saved = (n, proj, qkv, tables, o_b, o_f, lse, y_att, su, y_sg, merged, bias, (u_blk, vs_blk, ga_blk, gs_blk))
    return x_next, saved


def _mixer_backward(x, wb, small, saved, dx_next, dx_next_b, c_idx):
    n, proj, qkv, tables, o_b, o_f, lse, y_att, su, y_sg, merged, bias, (u_blk, vs_blk, ga_blk, gs_blk) = saved
    s = N_CHIPS
    dmerged = _matmul([(dx_next_b, _full(wb, 'w_out'))], 'nt', BF16, "mix_out_dx")
    g_w_out = _matmul([(merged, dx_next_b)], 'tn', BF16, "mix_out_dw", caps=(1024, 1024, 1024))
    dy_att, dy_sg, dg_att, dg_sg = _gate_bwd(proj, ga_blk, gs_blk, y_att, y_sg, dmerged, "gate_bwd")

    g_w_att_out = _matmul([(o_b, dy_att)], 'tn', BF16, "mix_att_out_dw", out3=s)
    do = _matmul([(dy_att, _full(wb, 'w_att_out'))], 'nt', F32, "mix_att_out_dx", b3=True)
    delta = _att_delta(do, o_f, "att_delta")
    stats = [(do, lse, delta)] + _regroup([do, lse, delta], "att_regroup")
    dqkv = []
    for gi, ((q, k, v, offs), st, dil) in enumerate(zip(qkv, stats, DILATIONS)):
        dq = _att_bwd_dq(q, k, v, *st, offs, dil, f"att_bwd_dq{gi}")
        dk, dv = _att_bwd_dkv(q, k, v, *st, offs, dil, f"att_bwd_dkv{gi}")
        dqkv.append((dq, dk, dv))
    dqkv = _rope_bwd(*dqkv[0], dqkv[1:], tables, "mix_rope_bwd")

    g_w_sg_out = _matmul([(su, dy_sg)], 'tn', BF16, "mix_sg_out_dw", out3=s)
    out_names = ['w_out', 'w_att_out', 'w_sg_out']
    out_parts = _reduce_first([g_w_out, g_w_att_out, g_w_sg_out], out_names, wb, c_idx)
    dsu = _matmul([(dy_sg, _full(wb, 'w_sg_out'))], 'nt', BF16, "mix_sg_out_dx", b3=True)
    du, dvs, g_sg_w, g_bias, g_lg, g_lb = _sg_bwd(proj, u_blk, vs_blk, dsu, small['sg_ln_g'], small['sg_ln_b'],
                                                   small['sg_w'], bias, "sg_bwd")
    gs = {'sg_w': g_sg_w, 'sg_b': g_bias[:, ::SG_GROUP_DIM].T, 'sg_ln_g': g_lg, 'sg_ln_b': g_lb}

    dproj = jnp.concatenate([dqkv, du, dvs, dg_att, dg_sg], axis=1)
    g_w_in, out_recv = _matmul([(n, dproj)], 'tn', BF16, "mix_in_dw", out3=s, caps=(1024, 1024, 1024),
                               rider=_scatter_rider(out_parts))
    (p_w_in,) = _reduce_first([g_w_in], ['w_in'], wb, c_idx)
    dn, (r_w_in,) = _matmul([(dproj, _full(wb, 'w_in'))], 'nt', F32, "mix_in_dx", b3=True, caps=(1024, 1024, 512),
                            rider=_scatter_rider([p_w_in]))
    dx, dx_b, gs['mix_norm'] = _rmsnorm_bwd(x, small['mix_norm'], dn, dx_next, "mix_norm_bwd")
    g = {nm: (p, r) for nm, p, r in zip(out_names, out_parts, out_recv)}
    g['w_in'] = (p_w_in, r_w_in)
    return dx, dx_b, g, gs


def _step(x, target, wb, small, c_idx):
    wb = dict(wb)
    rider, names = _gather(wb, ['ffn1_w_gate', 'ffn1_w_up'])
    _landed(wb, names, _exchange(rider, "gather_first"))
    half_in = wb['w_in'].shape[2] // 2
    x1, s1 = _ffn_forward(x, small['ffn1_norm'], wb, "ffn1", ['ffn1_w_down', ('w_in', 0, half_in)], [('w_in', half_in, 2 * half_in)])
    up_rows = wb['ffn2_w_up'].shape[2]
    up_cut = up_rows // 32 * 15
    x2, s2 = _mixer_forward(x1, wb, small, ['w_att_out', 'w_sg_out', 'w_out', 'ffn2_w_gate'],
                            [('ffn2_w_up', 0, up_cut)], [('ffn2_w_up', up_cut, up_rows)])
    x3, s3 = _ffn_forward(x2, small['ffn2_norm'], wb, "ffn2", ['ffn2_w_down'], None)
    loss, dx3, dx3_b, g_final = _final_loss(x3, small['final_norm'], target, "final_loss")
    gs = {'final_norm': g_final}
    dx2, dx2_b, gs['ffn2_norm'], g = _ffn_backward(x2, small['ffn2_norm'], wb, s3, dx3, dx3_b, c_idx, "ffn2")
    dx1, dx1_b, g_mix, gs_mix = _mixer_backward(x1, wb, small, s2, dx2, dx2_b, c_idx)
    g.update(g_mix)
    gs.update(gs_mix)
    dx0, _, gs['ffn1_norm'], g_ffn1 = _ffn_backward(x, small['ffn1_norm'], wb, s1, dx1, dx1_b, c_idx, "ffn1")
    g.update(g_ffn1)
    return loss, dx0, g, gs


def _cast_into_gathered(wt, p_idx, name):
    r, ccols = wt.shape[0] // 2, wt.shape[1]
    tm = _rows(r, 256)
    nb = r // tm

    def body(p_ref, w_ref, o_ref):
        o_ref[...] = w_ref[...].astype(BF16)

    grid_spec = pltpu.PrefetchScalarGridSpec(
        num_scalar_prefetch=1, grid=(2, nb),
        in_specs=[pl.BlockSpec((tm, ccols), lambda h, i, pr: (h * nb + i, 0))],
        out_specs=pl.BlockSpec((None, None, tm, ccols), lambda h, i, pr: (pr[0], h, i, 0)))
    return pl.pallas_call(body, out_shape=jax.ShapeDtypeStruct((N_CHIPS, 2, r, ccols), BF16), grid_spec=grid_spec,
                          compiler_params=_cparams(("parallel", "parallel"), 2 * _nbytes((tm, ccols), F32)), name=name)(p_idx, wt)


def _sibling_exchange(grads, name):
    nw = len(grads)

    def body(*refs):
        src, dst = refs[:nw], refs[nw:2 * nw]
        send_sems, recv_sems = refs[2 * nw:]
        x, y, c, _ = _place()
        cps = []
        for i in range(nw):
            cp = pltpu.make_async_remote_copy(src[i].at[:, 1 - c], dst[i], send_sems.at[i], recv_sems.at[i],
                                              device_id=(x, y, 1 - c), device_id_type=MESH)
            cp.start()
            cps.append(cp)
        for cp in cps:
            cp.wait()

    any_spec = pl.BlockSpec(memory_space=pl.ANY)
    return _pallas_call(
        body, out_shape=[jax.ShapeDtypeStruct((g.shape[0],) + g.shape[2:], g.dtype) for g in grads],
        in_specs=[any_spec] * nw, out_specs=[any_spec] * nw,
        scratch_shapes=[pltpu.SemaphoreType.DMA((nw,)), pltpu.SemaphoreType.DMA((nw,))],
        compiler_params=pltpu.CompilerParams(has_side_effects=True), name=name)(*grads)


def _half_exchange(bufs):
    nw = len(bufs)

    def body(*refs):
        dst = refs[nw:2 * nw]
        send_sems, recv_sems = refs[2 * nw:]
        x, y, c, _ = _place()
        cps = []
        for i in range(nw):
            mine = dst[i].at[c]
            cp = pltpu.make_async_remote_copy(mine, mine, send_sems.at[i], recv_sems.at[i],
                                              device_id=(x, y, 1 - c), device_id_type=MESH)
            cp.start()
            cps.append(cp)
        for i, cp in enumerate(cps):
            cp.wait_send()
            theirs = dst[i].at[1 - c]
            pltpu.make_async_remote_copy(theirs, theirs, send_sems.at[i], recv_sems.at[i],
                                         device_id=(x, y, 1 - c), device_id_type=MESH).wait_recv()

    any_spec = pl.BlockSpec(memory_space=pl.ANY)
    return _pallas_call(
        body, out_shape=[jax.ShapeDtypeStruct(b.shape, b.dtype) for b in bufs],
        in_specs=[any_spec] * nw, out_specs=[any_spec] * nw, input_output_aliases={i: i for i in range(nw)},
        scratch_shapes=[pltpu.SemaphoreType.DMA((nw,)), pltpu.SemaphoreType.DMA((nw,))],
        compiler_params=pltpu.CompilerParams(has_side_effects=True), name="rs_halves")(*bufs)


def _sibling_sum(grad, recv, c_idx, name):
    s, _, r, ccols = grad.shape
    tm = _rows(r, 256)

    def body(c_ref, g_ref, r_ref, o_ref):
        o_ref[...] = (g_ref[...].astype(F32) + r_ref[...].astype(F32)).astype(BF16)

    grid_spec = pltpu.PrefetchScalarGridSpec(
        num_scalar_prefetch=1, grid=(s, r // tm),
        in_specs=[pl.BlockSpec((None, None, tm, ccols), lambda q, i, cr: (q, cr[0], i, 0)),
                  pl.BlockSpec((None, tm, ccols), lambda q, i, cr: (q, i, 0))],
        out_specs=pl.BlockSpec((None, tm, ccols), lambda q, i, cr: (q, i, 0)))
    return pl.pallas_call(body, out_shape=jax.ShapeDtypeStruct((s, r, ccols), BF16), grid_spec=grid_spec,
                          compiler_params=_cparams(("parallel", "parallel"), 4 * _nbytes((tm, ccols), F32)), name=name)(c_idx, grad, recv)


def _chip_sum(part, recv, pc_idx, name):
    _, r, ccols = part.shape
    tm = _rows(r, 256)

    def body(pc_ref, own_ref, r0_ref, r1_ref, r2_ref, o_ref):
        acc = own_ref[...].astype(F32) + r0_ref[...].astype(F32)
        acc = acc + r1_ref[...].astype(F32)
        o_ref[...] = acc + r2_ref[...].astype(F32)

    def slot(j):
        return pl.BlockSpec((None, tm, ccols), lambda i, pc: (j, i, 0))

    grid_spec = pltpu.PrefetchScalarGridSpec(
        num_scalar_prefetch=1, grid=(r // tm,),
        in_specs=[pl.BlockSpec((None, tm, ccols), lambda i, pc: (pc[0], i, 0)), slot(0), slot(1), slot(2)],
        out_specs=pl.BlockSpec((None, tm, ccols), lambda i, pc: (pc[1], i, 0)))
    return pl.pallas_call(body, out_shape=jax.ShapeDtypeStruct((2, r, ccols), F32), grid_spec=grid_spec,
                          compiler_params=_cparams(("parallel",), 6 * _nbytes((tm, ccols), F32)), name=name)(pc_idx, part, recv, recv, recv)


def _all_reduce_small(vec):
    _, r, _ = vec.shape

    def body(v_ref, o_ref, parts, send1, recv1, send2, recv2):
        x, y, c, _ = _place()
        me = 4 * x + 2 * y + c
        peers = []
        for k in range(1, N_DEV):
            px, py, pc = (1 - x if k & 4 else x, 1 - y if k & 2 else y, 1 - c if k & 1 else c)
            peers.append(((px, py, pc), 4 * px + 2 * py + pc))
        parts[me] = v_ref[me]
        cps = []
        for k, (peer, peer_id) in enumerate(peers):
            cp = pltpu.make_async_remote_copy(v_ref.at[peer_id], parts.at[me], send1.at[k], recv1.at[k],
                                              device_id=peer, device_id_type=MESH)
            cp.start()
            cps.append(cp)
        for cp in cps:
            cp.wait()
        acc = parts[0]
        for dev in range(1, N_DEV):
            acc = acc + parts[dev]
        o_ref[me] = acc
        cps = []
        for k, (peer, _) in enumerate(peers):
            cp = pltpu.make_async_remote_copy(o_ref.at[me], o_ref.at[me], send2.at[k], recv2.at[k],
                                              device_id=peer, device_id_type=MESH)
            cp.start()
            cps.append(cp)
        for cp in cps:
            cp.wait()

    vm = pl.BlockSpec(memory_space=pltpu.VMEM)
    sems = pltpu.SemaphoreType.DMA((N_DEV - 1,))
    return pl.pallas_call(
        body, out_shape=jax.ShapeDtypeStruct(vec.shape, F32), in_specs=[vm], out_specs=vm,
        scratch_shapes=[pltpu.VMEM((N_DEV, r, LANES), F32), sems, sems, sems, sems],
        compiler_params=pltpu.CompilerParams(vmem_limit_bytes=int(8 * _nbytes((N_DEV, r, LANES), F32))),
        name="all_reduce_small")(vec)


def _adamw(wt, g, m, v, name):
    r, ccols = wt.shape
    tm = _rows(r, max(8, (MIB // (4 * ccols)) // 8 * 8))
    blk = pl.BlockSpec((tm, ccols), lambda i: (i, 0))

    def body(w_ref, g_ref, m_ref, v_ref, go_ref, d_ref, mo_ref, vo_ref):
        gv = g_ref[...]
        go_ref[...] = gv
        mv = ADAM_B1 * m_ref[...] + (1.0 - ADAM_B1) * gv
        vv = ADAM_B2 * v_ref[...] + (1.0 - ADAM_B2) * (gv * gv)
        m_hat = mv / (1.0 - ADAM_B1 ** ADAM_STEP)
        v_hat = vv / (1.0 - ADAM_B2 ** ADAM_STEP)
        d_ref[...] = -ADAM_LR * (m_hat / (jnp.sqrt(v_hat) + ADAM_EPS) + ADAM_WD * w_ref[...])
        mo_ref[...] = mv
        vo_ref[...] = vv

    out = jax.ShapeDtypeStruct((r, ccols), F32)
    return pl.pallas_call(body, out_shape=(out, out, out, out), grid=(r // tm,), in_specs=[blk] * 4, out_specs=(blk,) * 4,
                          compiler_params=_cparams(("parallel",), 8 * _nbytes((tm, ccols), F32)), name=name)(wt, g, m, v)


def _as_rows(a):
    rows = a.reshape(-1, LANES)
    return jnp.pad(rows, ((0, -rows.shape[0] % 8), (0, 0)))


def kernel(x, ffn1_norm, ffn1_w_gate, ffn1_w_up, ffn1_w_down, mix_norm, w_in, sg_ln_g, sg_ln_b, sg_w, sg_b, w_att_out, w_sg_out, w_out, ffn2_norm, ffn2_w_gate, ffn2_w_up, ffn2_w_down, final_norm, loss_target, m_ffn1_norm, m_ffn1_w_gate, m_ffn1_w_up, m_ffn1_w_down, m_mix_norm, m_w_in, m_sg_ln_g, m_sg_ln_b, m_sg_w, m_sg_b, m_w_att_out, m_w_sg_out, m_w_out, m_ffn2_norm, m_ffn2_w_gate, m_ffn2_w_up, m_ffn2_w_down, m_final_norm, v_ffn1_norm, v_ffn1_w_gate, v_ffn1_w_up, v_ffn1_w_down, v_mix_norm, v_w_in, v_sg_ln_g, v_sg_ln_b, v_sg_w, v_sg_b, v_w_att_out, v_w_sg_out, v_w_out, v_ffn2_norm, v_ffn2_w_gate, v_ffn2_w_up, v_ffn2_w_down, v_final_norm):
    given = dict(locals())
    wts = {n: given[n] for n in WEIGHT_NAMES}
    ms = {n: given["m_" + n] for n in WEIGHT_NAMES}
    vs = {n: given["v_" + n] for n in WEIGHT_NAMES}
    t, d = x.shape[-2], x.shape[-1]
    xc, yc, cc = lax.axis_index("x"), lax.axis_index("y"), lax.axis_index("c")

    shard2d = {n: wts[n].reshape(wts[n].shape[-2:]) for n in BIG_NAMES}
    p_idx = jnp.reshape(2 * xc + yc, (1,)).astype(jnp.int32)
    c_idx = jnp.reshape(cc, (1,)).astype(jnp.int32)
    pc_idx = jnp.stack([2 * xc + yc, cc]).astype(jnp.int32)
    wb = {n: _cast_into_gathered(shard2d[n], p_idx, f"cast_{n}") for n in BIG_NAMES}

    small = {n: wts[n].reshape(-1, wts[n].shape[-1]) for n in SMALL_NAMES}
    small['sg_w'] = wts['sg_w'].reshape(wts['sg_w'].shape[-3:])
    loss, dx, g, gs = _step(x.reshape(t, d), loss_target.reshape(t, d), wb, small, c_idx)
    loss = lax.psum(loss[0, 0], ("x", "y", "c"))

    my_halves = [_chip_sum(*g[n], pc_idx, f"rs_sum2_{n}") for n in BIG_NAMES]
    reduced = _half_exchange(my_halves)
    grads = {n: r.reshape(shard2d[n].shape) for n, r in zip(BIG_NAMES, reduced)}

    def pack(tree):
        rows = jnp.concatenate([_as_rows(tree[n]) for n in SMALL_NAMES], axis=0)
        return jnp.pad(rows, ((0, -rows.shape[0] % (8 * N_DEV)), (0, 0)))

    packed = pack(gs)
    packed = _all_reduce_small(packed.reshape(N_DEV, -1, LANES)).reshape(packed.shape)

    delta, new_m, new_v = {}, {}, {}
    for n in BIG_NAMES:
        shape = wts[n].shape
        out = _adamw(shard2d[n], grads[n], ms[n].reshape(shard2d[n].shape), vs[n].reshape(shard2d[n].shape), f"adamw_{n}")
        grads[n], delta[n], new_m[n], new_v[n] = (a.reshape(shape) for a in out)

    small_out = _adamw(pack(wts), packed, pack(ms), pack(vs), "adamw_small")
    row = 0
    for n in SMALL_NAMES:
        shape = wts[n].shape
        sz = wts[n].size // LANES
        grads[n], delta[n], new_m[n], new_v[n] = (a[row:row + sz].reshape(shape) for a in small_out)
        row += sz + -sz % 8

    return (loss, dx.reshape(x.shape), *[grads[n] for n in WEIGHT_NAMES], *[delta[n] for n in WEIGHT_NAMES],
            *[new_m[n] for n in WEIGHT_NAMES], *[new_v[n] for n in WEIGHT_NAMES])
```

```python
import functools

import jax
import jax.numpy as jnp
from jax import lax
from jax.experimental import pallas as pl
from jax.experimental.pallas import tpu as pltpu

F32 = jnp.float32
BF16 = jnp.bfloat16
MESH = pl.DeviceIdType.MESH

NORM_EPS = 1e-6
LN_EPS = 1e-5
HEAD_DIM = 128
HEADS_PER_GROUP = 4
GROUP_WIDTH = HEADS_PER_GROUP * HEAD_DIM
DILATIONS = (1, 4, 16)
N_GROUPS = len(DILATIONS)
ATT_BLOCK = 128
ROPE_DIM = HEAD_DIM // 4
ROPE_THETA = 500000.0
SG_CHUNK = 128
SG_GROUPS = 12
SG_GROUP_DIM = 128
MASKED = -1e30

ADAM_LR = 0.001
ADAM_B1 = 0.9
ADAM_B2 = 0.999
ADAM_EPS = 1e-08
ADAM_WD = 0.01
ADAM_STEP = 10

N_CHIPS = 4
N_DEV = 8
LANES = 128
MIB = 2 ** 20
VMEM_BYTES_V7X = 64 * MIB

WEIGHT_NAMES = ['ffn1_norm', 'ffn1_w_gate', 'ffn1_w_up', 'ffn1_w_down', 'mix_norm', 'w_in', 'sg_ln_g', 'sg_ln_b',
                'sg_w', 'sg_b', 'w_att_out', 'w_sg_out', 'w_out', 'ffn2_norm', 'ffn2_w_gate', 'ffn2_w_up',
                'ffn2_w_down', 'final_norm']
BIG = [('ffn1_w_gate', 1), ('ffn1_w_up', 1), ('ffn1_w_down', 0), ('w_in', 1), ('w_att_out', 1), ('w_sg_out', 1),
       ('w_out', 0), ('ffn2_w_gate', 1), ('ffn2_w_up', 1), ('ffn2_w_down', 0)]
BIG_NAMES = [n for n, _ in BIG]
SMALL_NAMES = [n for n in WEIGHT_NAMES if n not in BIG_NAMES]


def _nbytes(shape, dtype):
    n = jnp.dtype(dtype).itemsize
    for s in shape:
        if s is not None:
            n *= s
    return n


def _pallas_call(*args, **kw):
    kw['out_shape'] = jax.tree.map(lambda s: pltpu.HBM(s.shape, s.dtype), kw['out_shape'])
    call = pl.pallas_call(*args, **kw)

    def pinned(*operands):
        return call(*[o if jnp.issubdtype(o.dtype, jnp.integer) else pltpu.with_memory_space_constraint(o, pltpu.HBM)
                      for o in operands])

    return pinned


def _cparams(sem, block_bytes, **kw):
    limit = int(min(max(3 * block_bytes, 32 * MIB), VMEM_BYTES_V7X - 8 * MIB))
    return pltpu.CompilerParams(dimension_semantics=sem, vmem_limit_bytes=limit, **kw)


def _tile(dim, cap):
    best = None
    for t in range(LANES, min(dim, cap) + 1, LANES):
        if dim % t == 0:
            best = t
    if best is None:
        assert dim <= cap, (dim, cap)
        return dim
    return best


def _rows(dim, cap):
    best = None
    for t in range(8, min(dim, cap) + 1, 8):
        if dim % t == 0:
            best = t
    assert best is not None, (dim, cap)
    return best


def _place():
    x, y, c = lax.axis_index("x"), lax.axis_index("y"), lax.axis_index("c")
    others = [(1 - x, y), (x, 1 - y), (1 - x, 1 - y)]
    return x, y, c, others


class _Rider:
    def __init__(self, operands, out_shapes, aliases, sems, start, finish):
        self.operands = operands
        self.out_shapes = out_shapes
        self.aliases = aliases
        self.sems = sems
        self.start = start
        self.finish = finish


def _run(body, *, name, grid, in_specs, out_specs, out_shape, scratch_shapes, operands, block_bytes, rider=None, pinned=True):
    call = _pallas_call if pinned else pl.pallas_call
    if rider is None:
        sem = ("parallel",) * (len(grid) - 1) + ("arbitrary",)
        return call(body, out_shape=out_shape, grid=grid, in_specs=in_specs, out_specs=out_specs,
                    scratch_shapes=scratch_shapes, compiler_params=_cparams(sem, block_bytes), name=name)(*operands)
    n_in, n_out, n_scr = len(operands), len(out_shape), len(scratch_shapes)
    r_in, r_out = len(rider.operands), len(rider.out_shapes)
    any_spec = pl.BlockSpec(memory_space=pl.ANY)

    def wrapped(*refs):
        ins, refs = refs[:n_in], refs[n_in:]
        r_ins, refs = refs[:r_in], refs[r_in:]
        outs, refs = refs[:n_out], refs[n_out:]
        r_outs, refs = refs[:r_out], refs[r_out:]
        scr, sems = refs[:n_scr], refs[n_scr:]
        if not grid:
            rider.start(r_ins, r_outs, sems)
            rider.finish(r_ins, r_outs, sems)
            return
        ids = [pl.program_id(a) for a in range(len(grid))]
        first = functools.reduce(jnp.logical_and, [i == 0 for i in ids])
        last = functools.reduce(jnp.logical_and, [i == g - 1 for i, g in zip(ids, grid)])

        @pl.when(first)
        def _():
            rider.start(r_ins, r_outs, sems)

        body(*ins, *outs, *scr)

        @pl.when(last)
        def _():
            rider.finish(r_ins, r_outs, sems)

    results = call(
        wrapped, out_shape=list(out_shape) + list(rider.out_shapes), grid=grid,
        in_specs=list(in_specs) + [any_spec] * r_in, out_specs=list(out_specs) + [any_spec] * r_out,
        scratch_shapes=list(scratch_shapes) + list(rider.sems),
        input_output_aliases={n_in + k: n_out + v for k, v in rider.aliases.items()},
        compiler_params=_cparams(("arbitrary",) * len(grid) if grid else None, block_bytes, has_side_effects=True),
        name=name)(*operands, *rider.operands)
    return results[:n_out], results[n_out:]


def _exchange(rider, name):
    return _run(None, name=name, grid=(), in_specs=[], out_specs=[], out_shape=[], scratch_shapes=[], operands=[],
                block_bytes=0, rider=rider)[1]


def _gather_rider(items):
    bufs, index = [], []
    for b, r0, r1 in items:
        if not any(b is q for q in bufs):
            bufs.append(b)
        index.append(([k for k, q in enumerate(bufs) if q is b][0], r0, r1))
    n = len(index)

    def piece(refs, k, chip, half):
        bi, r0, r1 = index[k]
        return refs[bi].at[chip, half, pl.ds(r0, r1 - r0)]

    def copy(ref, sem_pair, k, j, to):
        return pltpu.make_async_remote_copy(ref, ref, sem_pair[0].at[k, j], sem_pair[1].at[k, j], device_id=to, device_id_type=MESH)

    def start(r_ins, buf, sems):
        x, y, c, others = _place()
        for k in range(n):
            for j, (ox, oy) in enumerate(others):
                copy(piece(buf, k, 2 * x + y, c), sems[:2], k, j, (ox, oy, c)).start()

    def finish(r_ins, buf, sems):
        x, y, c, others = _place()
        for k in range(n):
            for j, (ox, oy) in enumerate(others):
                got = piece(buf, k, 2 * ox + oy, c)
                copy(got, sems[:2], k, j, (ox, oy, c)).wait_recv()
                copy(got, sems[2:], k, j, (x, y, 1 - c)).start()
        for k in range(n):
            for j, (ox, oy) in enumerate(others):
                copy(piece(buf, k, 2 * ox + oy, 1 - c), sems[2:], k, j, (x, y, 1 - c)).wait_recv()
        for k in range(n):
            for j, (ox, oy) in enumerate(others):
                copy(piece(buf, k, 2 * x + y, c), sems[:2], k, j, (ox, oy, c)).wait_send()
                copy(piece(buf, k, 2 * ox + oy, c), sems[2:], k, j, (x, y, 1 - c)).wait_send()

    return _Rider(bufs, [jax.ShapeDtypeStruct(b.shape, b.dtype) for b in bufs], {i: i for i in range(len(bufs))},
                  [pltpu.SemaphoreType.DMA((n, 3))] * 4, start, finish)


def _scatter_rider(parts):
    n = len(parts)

    def copy(src, dst, sems, i, j, to):
        return pltpu.make_async_remote_copy(src, dst, sems[0].at[i, j], sems[1].at[i, j], device_id=to, device_id_type=MESH)

    def start(src, dst, sems):
        x, y, c, others = _place()
        for i in range(n):
            for j, (ox, oy) in enumerate(others):
                copy(src[i].at[2 * ox + oy], dst[i].at[j], sems, i, j, (ox, oy, c)).start()

    def finish(src, dst, sems):
        x, y, c, others = _place()
        for i in range(n):
            for j, (ox, oy) in enumerate(others):
                copy(src[i].at[2 * ox + oy], dst[i].at[j], sems, i, j, (ox, oy, c)).wait()

    return _Rider(parts, [jax.ShapeDtypeStruct((3,) + p.shape[1:], p.dtype) for p in parts], {},
                  [pltpu.SemaphoreType.DMA((n, 3))] * 2, start, finish)


def _matmul(pairs, mode, out_dtype, name, *, scale=1.0, residual=None, b3=False, out3=0, caps=(1024, 1024, 512), rider=None):
    a0, b0 = pairs[0]
    if mode == 'nn':
        m, k = a0.shape
        n = b0.shape[0] * b0.shape[2] if b3 else b0.shape[1]
    elif mode == 'nt':
        m = a0.shape[0]
        n, k = (b0.shape[1], b0.shape[0] * b0.shape[2]) if b3 else b0.shape
    else:
        k, m = a0.shape
        n = b0.shape[1]
    tm = _tile(m, caps[0])
    tn = _tile(n, caps[1])
    tk = _tile(k, caps[2])
    if b3 and mode == 'nn':
        tn = b0.shape[2]
    if b3 and mode == 'nt':
        tk = b0.shape[2]
    if out3:
        tn = n // out3
    nk = k // tk
    if mode == 'tn':
        a_spec = pl.BlockSpec((tk, tm), lambda i, j, kk: (kk, i))
        b_spec = pl.BlockSpec((tk, tn), lambda i, j, kk: (kk, j))
        dims = ((0,), (0,))
    elif mode == 'nn':
        a_spec = pl.BlockSpec((tm, tk), lambda i, j, kk: (i, kk))
        b_spec = (pl.BlockSpec((None, tk, tn), lambda i, j, kk: (j, kk, 0)) if b3
                  else pl.BlockSpec((tk, tn), lambda i, j, kk: (kk, j)))
        dims = ((1,), (0,))
    else:
        a_spec = pl.BlockSpec((tm, tk), lambda i, j, kk: (i, kk))
        b_spec = (pl.BlockSpec((None, tn, tk), lambda i, j, kk: (kk, j, 0)) if b3
                  else pl.BlockSpec((tn, tk), lambda i, j, kk: (j, kk)))
        dims = ((1,), (1,))
    in_specs, operands = [], []
    for a, b in pairs:
        in_specs += [a_spec, b_spec]
        operands += [a, b]
    block_bytes = len(pairs) * (_nbytes((tm, tk), a0.dtype) + _nbytes((tk, tn), b0.dtype))
    if residual is not None:
        in_specs.append(pl.BlockSpec((tm, tn), lambda i, j, kk: (i, j)))
        operands.append(residual)
        block_bytes += _nbytes((tm, tn), F32)
    if out3:
        out_spec = pl.BlockSpec((None, tm, tn), lambda i, j, kk: (j, i, 0))
        out_shape = jax.ShapeDtypeStruct((out3, m, tn), out_dtype)
    else:
        out_spec = pl.BlockSpec((tm, tn), lambda i, j, kk: (i, j))
        out_shape = jax.ShapeDtypeStruct((m, n), out_dtype)
    block_bytes += _nbytes((tm, tn), out_dtype) + _nbytes((tm, tn), F32)
    n_pairs = len(pairs)
    has_res = residual is not None

    def body(*refs):
        o_ref, acc = refs[-2], refs[-1]
        kk = pl.program_id(2)

        def product():
            part = None
            for p in range(n_pairs):
                d = lax.dot_general(refs[2 * p][...].astype(BF16), refs[2 * p + 1][...].astype(BF16),
                                    (dims, ((), ())), preferred_element_type=F32)
                part = d if part is None else part + d
            return part

        def finish(r):
            if scale != 1.0:
                r = r * scale
            if has_res:
                r = refs[2 * n_pairs][...] + r
            o_ref[...] = r.astype(out_dtype)

        if nk == 1:
            finish(product())
            return

        @pl.when(kk == 0)
        def _():
            acc[...] = product()

        if nk > 2:
            @pl.when(jnp.logical_and(kk > 0, kk < nk - 1))
            def _():
                acc[...] += product()

        @pl.when(kk == nk - 1)
        def _():
            finish(acc[...] + product())

    res = _run(body, name=name, grid=(m // tm, n // tn, nk), in_specs=in_specs, out_specs=[out_spec], out_shape=[out_shape],
               scratch_shapes=[pltpu.VMEM((tm, tn), F32)], operands=operands, block_bytes=block_bytes, rider=rider)
    return res[0] if rider is None else (res[0][0], res[1])


def _rmsnorm_fwd(x, g, name):
    t, d = x.shape
    tm = _rows(t, 512)

    def body(x_ref, g_ref, o_ref):
        xv = x_ref[...]
        r = lax.rsqrt(jnp.mean(xv * xv, axis=1, keepdims=True) + NORM_EPS)
        o_ref[...] = (xv * r * g_ref[...]).astype(BF16)

    row = pl.BlockSpec((tm, d), lambda i: (i, 0))
    return _pallas_call(
        body, out_shape=jax.ShapeDtypeStruct((t, d), BF16), grid=(t // tm,),
        in_specs=[row, pl.BlockSpec((1, d), lambda i: (0, 0))], out_specs=row,
        compiler_params=_cparams(("parallel",), 2 * _nbytes((tm, d), F32)), name=name)(x, g)


def _rms_grad(xv, g, dn, d):
    r = lax.rsqrt(jnp.mean(xv * xv, axis=1, keepdims=True) + NORM_EPS)
    u = dn * g
    s = jnp.sum(xv * u, axis=1, keepdims=True)
    dx = r * u - xv * (r * r * r) * (s * (1.0 / d))
    return dx, dn * xv * r


def _rmsnorm_bwd(x, g, dn, dres, name):
    t, d = x.shape
    tm = _rows(t, 256)

    def body(x_ref, g_ref, dn_ref, dres_ref, dx_ref, dxb_ref, dg_ref):
        dx, dg_rows = _rms_grad(x_ref[...], g_ref[...], dn_ref[...].astype(F32), d)
        dx = dres_ref[...] + dx
        dx_ref[...] = dx
        dxb_ref[...] = dx.astype(BF16)

        @pl.when(pl.program_id(0) == 0)
        def _():
            dg_ref[...] = jnp.zeros_like(dg_ref)

        dg_ref[...] += jnp.sum(dg_rows, axis=0, keepdims=True)

    row = pl.BlockSpec((tm, d), lambda i: (i, 0))
    vec = pl.BlockSpec((1, d), lambda i: (0, 0))
    return _pallas_call(
        body, out_shape=(jax.ShapeDtypeStruct((t, d), F32), jax.ShapeDtypeStruct((t, d), BF16), jax.ShapeDtypeStruct((1, d), F32)),
        grid=(t // tm,), in_specs=[row, vec, row, row], out_specs=(row, row, vec),
        compiler_params=_cparams(("arbitrary",), 5 * _nbytes((tm, d), F32)), name=name)(x, g, dn, dres)


def _final_loss(x, g, target, name):
    t, d = x.shape
    tm = _rows(t, 256)

    def body(x_ref, g_ref, t_ref, loss_ref, dx_ref, dxb_ref, dg_ref):
        xv, gv = x_ref[...], g_ref[...]
        r = lax.rsqrt(jnp.mean(xv * xv, axis=1, keepdims=True) + NORM_EPS)
        err = xv * r * gv - t_ref[...]
        dx, dg_rows = _rms_grad(xv, gv, err * (1.0 / d), d)
        dx_ref[...] = dx
        dxb_ref[...] = dx.astype(BF16)

        @pl.when(pl.program_id(0) == 0)
        def _():
            dg_ref[...] = jnp.zeros_like(dg_ref)
            loss_ref[...] = jnp.zeros_like(loss_ref)

        dg_ref[...] += jnp.sum(dg_rows, axis=0, keepdims=True)
        row_loss = jnp.sum(err * err, axis=1, keepdims=True) * (0.5 / d)
        loss_ref[...] += jnp.sum(row_loss, axis=0, keepdims=True)

    row = pl.BlockSpec((tm, d), lambda i: (i, 0))
    vec = pl.BlockSpec((1, d), lambda i: (0, 0))
    return _pallas_call(
        body, out_shape=(jax.ShapeDtypeStruct((1, 1), F32), jax.ShapeDtypeStruct((t, d), F32),
                         jax.ShapeDtypeStruct((t, d), BF16), jax.ShapeDtypeStruct((1, d), F32)),
        grid=(t // tm,), in_specs=[row, vec, row], out_specs=(pl.BlockSpec((1, 1), lambda i: (0, 0)), row, row, vec),
        compiler_params=_cparams(("arbitrary",), 4 * _nbytes((tm, d), F32)), name=name)(x, g, target)


def _sigmoid(x):
    return 0.5 * jnp.tanh(0.5 * x) + 0.5


def _ffn_up(n, wg, wu, name, rider=None):
    t, d = n.shape
    s, _, f = wg.shape
    tm, tk = _tile(t, 1024), _tile(d, 1024)
    nk = d // tk

    def body(n_ref, wg_ref, wu_ref, a_ref, b_ref, h_ref, acc_g, acc_u):
        kk = pl.program_id(2)

        def products():
            nv = n_ref[...]
            return jnp.dot(nv, wg_ref[...], preferred_element_type=F32), jnp.dot(nv, wu_ref[...], preferred_element_type=F32)

        def finish(a, b):
            a_ref[...] = a.astype(BF16)
            b_ref[...] = b.astype(BF16)
            h_ref[...] = (a * _sigmoid(a) * b).astype(BF16)

        if nk == 1:
            finish(*products())
            return

        @pl.when(kk == 0)
        def _():
            acc_g[...], acc_u[...] = products()

        if nk > 2:
            @pl.when(jnp.logical_and(kk > 0, kk < nk - 1))
            def _():
                pg, pu = products()
                acc_g[...] += pg
                acc_u[...] += pu

        @pl.when(kk == nk - 1)
        def _():
            pg, pu = products()
            finish(acc_g[...] + pg, acc_u[...] + pu)

    w_spec = pl.BlockSpec((None, tk, f), lambda i, j, kk: (j, kk, 0))
    o_spec = pl.BlockSpec((tm, f), lambda i, j, kk: (i, j))
    out = jax.ShapeDtypeStruct((t, s * f), BF16)
    block_bytes = _nbytes((tm, tk), BF16) + 2 * _nbytes((tk, f), BF16) + 3 * _nbytes((tm, f), BF16) + 2 * _nbytes((tm, f), F32)
    return _run(body, name=name, grid=(t // tm, s, nk),
                in_specs=[pl.BlockSpec((tm, tk), lambda i, j, kk: (i, kk)), w_spec, w_spec], out_specs=[o_spec, o_spec, o_spec],
                out_shape=[out, out, out], scratch_shapes=[pltpu.VMEM((tm, f), F32), pltpu.VMEM((tm, f), F32)],
                operands=[n, wg, wu], block_bytes=block_bytes, rider=rider)


def _ffn_bwd_act(dx, wd, a, b, name):
    t, d = dx.shape
    f = wd.shape[0]
    tm, tn, tk = _tile(t, 1024), _tile(f, 1536), _tile(d, 1024)
    nk = d // tk

    def body(dx_ref, wd_ref, a_ref, b_ref, da_ref, db_ref, acc):
        kk = pl.program_id(2)

        def product():
            return lax.dot_general(dx_ref[...], wd_ref[...], (((1,), (1,)), ((), ())), preferred_element_type=F32)

        def finish(r):
            dh = 0.5 * r
            av, bv = a_ref[...].astype(F32), b_ref[...].astype(F32)
            sg = _sigmoid(av)
            da_ref[...] = (dh * bv * (sg * (1.0 + av * (1.0 - sg)))).astype(BF16)
            db_ref[...] = (dh * (av * sg)).astype(BF16)

        if nk == 1:
            finish(product())
            return

        @pl.when(kk == 0)
        def _():
            acc[...] = product()

        if nk > 2:
            @pl.when(jnp.logical_and(kk > 0, kk < nk - 1))
            def _():
                acc[...] += product()

        @pl.when(kk == nk - 1)
        def _():
            finish(acc[...] + product())

    act = pl.BlockSpec((tm, tn), lambda i, j, kk: (i, j))
    out = jax.ShapeDtypeStruct((t, f), BF16)
    block_bytes = _nbytes((tm, tk), BF16) + _nbytes((tn, tk), BF16) + 4 * _nbytes((tm, tn), BF16) + _nbytes((tm, tn), F32)
    return _pallas_call(
        body, out_shape=(out, out), grid=(t // tm, f // tn, nk),
        in_specs=[pl.BlockSpec((tm, tk), lambda i, j, kk: (i, kk)), pl.BlockSpec((tn, tk), lambda i, j, kk: (j, kk)),
                  act, act],
        out_specs=(act, act), scratch_shapes=[pltpu.VMEM((tm, tn), F32)],
        compiler_params=_cparams(("parallel", "parallel", "arbitrary"), block_bytes), name=name)(dx, wd, a, b)


AXIS = dict(BIG)


def _full(wb, n):
    _, _, r, ccols = wb[n].shape
    return wb[n].reshape(N_CHIPS, 2 * r, ccols) if AXIS[n] == 1 else wb[n].reshape(N_CHIPS * 2 * r, ccols)


def _gather(wb, specs):
    items, names = [], []
    for s in specs:
        n, r0, r1 = (s, 0, wb[s].shape[2]) if isinstance(s, str) else s
        items.append((wb[n], r0, r1))
        if n not in names:
            names.append(n)
    return _gather_rider(items), names


def _landed(wb, names, results):
    for n, r in zip(names, results):
        wb[n] = r


def _reduce_first(grads, names, wb, c_idx):
    g4 = [g.reshape(wb[n].shape) for g, n in zip(grads, names)]
    from_sibling = _sibling_exchange(g4, "rs_sibling_" + names[0])
    return [_sibling_sum(a, b, c_idx, f"rs_sum1_{n}") for a, b, n in zip(g4, from_sibling, names)]


def _ffn_forward(x, gain, wb, tag, up_specs, down_specs):
    n = _rmsnorm_fwd(x, gain, f"{tag}_norm")
    rider, names = _gather(wb, up_specs)
    (a, b, h), got = _ffn_up(n, _full(wb, f"{tag}_w_gate"), _full(wb, f"{tag}_w_up"), f"{tag}_up", rider=rider)
    _landed(wb, names, got)
    down = dict(scale=0.5, residual=x, caps=(1024, 1024, 1536))
    if down_specs:
        rider, names = _gather(wb, down_specs)
        x_next, got = _matmul([(h, _full(wb, f"{tag}_w_down"))], 'nn', F32, f"{tag}_down", rider=rider, **down)
        _landed(wb, names, got)
    else:
        x_next = _matmul([(h, _full(wb, f"{tag}_w_down"))], 'nn', F32, f"{tag}_down", **down)
    return x_next, (n, a, b, h)


def _ffn_backward(x, gain, wb, saved, dx_next, dx_next_b, c_idx, tag, dwd_rider=None):
    n, a, b, h = saved
    wg, wu, wd = (f"{tag}_w_gate", f"{tag}_w_up", f"{tag}_w_down")
    da, db = _ffn_bwd_act(dx_next_b, _full(wb, wd), a, b, f"{tag}_bwd_act")
    res = _matmul([(h, dx_next_b)], 'tn', BF16, f"{tag}_dwd", scale=0.5, caps=(1536, 2048, 1024), rider=dwd_rider)
    g_wd, carried = (res, ()) if dwd_rider is None else res
    (p_wd,) = _reduce_first([g_wd], [wd], wb, c_idx)
    g_wg, (r_wd,) = _matmul([(n, da)], 'tn', BF16, f"{tag}_dwg", out3=N_CHIPS, caps=(2048, 1024, 1024), rider=_scatter_rider([p_wd]))
    g_wu = _matmul([(n, db)], 'tn', BF16, f"{tag}_dwu", out3=N_CHIPS, caps=(2048, 1024, 1024))
    p_wg, p_wu = _reduce_first([g_wg, g_wu], [wg, wu], wb, c_idx)
    dn, (r_wg, r_wu) = _matmul([(da, _full(wb, wg)), (db, _full(wb, wu))], 'nt', F32, f"{tag}_dn", b3=True,
                               rider=_scatter_rider([p_wg, p_wu]))
    dx, dx_b, g_gain = _rmsnorm_bwd(x, gain, dn, dx_next, f"{tag}_norm_bwd")
    return dx, dx_b, g_gain, {wg: (p_wg, r_wg), wu: (p_wu, r_wu), wd: (p_wd, r_wd)}, carried


def _rope_tables(seq):
    half = ROPE_DIM // 2
    inv_freq = ROPE_THETA ** (-jnp.arange(0, ROPE_DIM, 2, dtype=F32) / ROPE_DIM)
    ang = jnp.arange(seq).astype(F32)[:, None] * inv_freq[None, :]
    cos, sin = jnp.cos(ang), jnp.sin(ang)
    zeros = lambda w: jnp.zeros((seq, w), F32)
    c = jnp.concatenate([cos, cos, jnp.ones((seq, HEAD_DIM - ROPE_DIM), F32)], axis=1)
    s_up = jnp.concatenate([-sin, zeros(HEAD_DIM - half)], axis=1)
    s_dn = jnp.concatenate([zeros(half), sin, zeros(HEAD_DIM - ROPE_DIM)], axis=1)
    return c, s_up, s_dn


def _rotate(xv, cv, uv, dv):
    half = ROPE_DIM // 2
    return xv * cv + pltpu.roll(xv, HEAD_DIM - half, 1) * uv + pltpu.roll(xv, half, 1) * dv


def _stage(tm):
    return pltpu.VMEM((HEADS_PER_GROUP, tm, HEAD_DIM), F32)


def _to_groups(stage, o_ref, dil):
    rows = stage.shape[1] // dil
    for r in range(dil):
        for h in range(HEADS_PER_GROUP):
            col = r * GROUP_WIDTH + h * HEAD_DIM
            o_ref[:, col:col + HEAD_DIM] = stage[h, pl.ds(r, rows, stride=dil), :].astype(o_ref.dtype)


def _from_groups(g_ref, stage, dil):
    rows = stage.shape[1] // dil
    for r in range(dil):
        for h in range(HEADS_PER_GROUP):
            col = r * GROUP_WIDTH + h * HEAD_DIM
            stage[h, pl.ds(r, rows, stride=dil), :] = g_ref[:, col:col + HEAD_DIM].astype(F32)


def _group_spec(tm, dil):
    return pl.BlockSpec((tm // dil, dil * GROUP_WIDTH), lambda i: (i, 0))


def _group_shape(t, dil, dtype):
    return jax.ShapeDtypeStruct((t // dil, dil * GROUP_WIDTH), dtype)


def _rope_fwd(proj, tables, name):
    t = proj.shape[0]
    tm = _rows(t, 512)
    att_w = N_GROUPS * GROUP_WIDTH
    dilated = [(gi, dil) for gi, dil in enumerate(DILATIONS) if dil > 1]

    def body(x_ref, c_ref, up_ref, dn_ref, qk0_ref, *rest):
        outs, stage = rest[:-1], rest[-1]
        cv, uv, dv = c_ref[...], up_ref[...], dn_ref[...]
        for part in range(2):
            for gi, dil in enumerate(DILATIONS):
                for h in range(HEADS_PER_GROUP):
                    col = part * att_w + gi * GROUP_WIDTH + h * HEAD_DIM
                    y = _rotate(x_ref[:, col:col + HEAD_DIM].astype(F32), cv, uv, dv)
                    if dil == 1:
                        qk0_ref[:, part * GROUP_WIDTH + h * HEAD_DIM:part * GROUP_WIDTH + (h + 1) * HEAD_DIM] = y.astype(BF16)
                    else:
                        stage[h] = y
                if dil > 1:
                    _to_groups(stage, outs[3 * dilated.index((gi, dil)) + part], dil)
        for n, (gi, dil) in enumerate(dilated):
            col = 2 * att_w + gi * GROUP_WIDTH
            for h in range(HEADS_PER_GROUP):
                stage[h] = x_ref[:, col + h * HEAD_DIM:col + (h + 1) * HEAD_DIM].astype(F32)
            _to_groups(stage, outs[3 * n + 2], dil)

    tab = pl.BlockSpec((tm, HEAD_DIM), lambda i: (i, 0))
    out_shape = [jax.ShapeDtypeStruct((t, 2 * GROUP_WIDTH), BF16)]
    out_specs = [pl.BlockSpec((tm, 2 * GROUP_WIDTH), lambda i: (i, 0))]
    for _, dil in dilated:
        out_shape += [_group_shape(t, dil, BF16)] * 3
        out_specs += [_group_spec(tm, dil)] * 3
    res = _pallas_call(
        body, out_shape=out_shape, grid=(t // tm,),
        in_specs=[pl.BlockSpec((tm, 3 * att_w), lambda i: (i, 0)), tab, tab, tab], out_specs=out_specs,
        scratch_shapes=[_stage(tm)],
        compiler_params=_cparams(("parallel",), 4 * _nbytes((tm, 3 * att_w), BF16)), name=name)(proj, *tables)
    return res[0], [tuple(res[1 + 3 * n:4 + 3 * n]) for n in range(len(dilated))]


def _rope_bwd(dq0, dk0, dv0, grouped, tables, name):
    t = dq0.shape[0]
    tm = _rows(t, 512)
    att_w = N_GROUPS * GROUP_WIDTH
    dilated = [(gi, dil) for gi, dil in enumerate(DILATIONS) if dil > 1]
    c, s_up, s_dn = tables

    def body(c_ref, up_ref, dn_ref, dq0_ref, dk0_ref, dv0_ref, *rest):
        g_refs, o_ref, stage = rest[:-2], rest[-2], rest[-1]
        cv, uv, dv = c_ref[...], -up_ref[...], -dn_ref[...]
        for part, first in enumerate((dq0_ref, dk0_ref)):
            for gi, dil in enumerate(DILATIONS):
                if dil > 1:
                    _from_groups(g_refs[3 * dilated.index((gi, dil)) + part], stage, dil)
                for h in range(HEADS_PER_GROUP):
                    sl = slice(h * HEAD_DIM, (h + 1) * HEAD_DIM)
                    xv = first[:, sl].astype(F32) if dil == 1 else stage[h]
                    col = part * att_w + gi * GROUP_WIDTH + h * HEAD_DIM
                    o_ref[:, col:col + HEAD_DIM] = _rotate(xv, cv, uv, dv).astype(BF16)
        for gi, dil in enumerate(DILATIONS):
            col = 2 * att_w + gi * GROUP_WIDTH
            if dil == 1:
                o_ref[:, col:col + GROUP_WIDTH] = dv0_ref[...]
            else:
                _from_groups(g_refs[3 * dilated.index((gi, dil)) + 2], stage, dil)
                for h in range(HEADS_PER_GROUP):
                    o_ref[:, col + h * HEAD_DIM:col + (h + 1) * HEAD_DIM] = stage[h].astype(BF16)

    tab = pl.BlockSpec((tm, HEAD_DIM), lambda i: (i, 0))
    nat = pl.BlockSpec((tm, GROUP_WIDTH), lambda i: (i, 0))
    in_specs, operands = [tab, tab, tab, nat, nat, nat], [c, s_up, s_dn, dq0, dk0, dv0]
    for (_, dil), arrs in zip(dilated, grouped):
        in_specs += [_group_spec(tm, dil)] * 3
        operands += list(arrs)
    return _pallas_call(
        body, out_shape=jax.ShapeDtypeStruct((t, 3 * att_w), BF16), grid=(t // tm,), in_specs=in_specs,
        out_specs=pl.BlockSpec((tm, 3 * att_w), lambda i: (i, 0)), scratch_shapes=[_stage(tm)],
        compiler_params=_cparams(("parallel",), 4 * _nbytes((tm, 3 * att_w), BF16)), name=name)(*operands)


def _regroup(arrs, name):
    t = arrs[0].shape[0]
    tm = _rows(t, 512)
    dilated = [dil for dil in DILATIONS if dil > 1]
    n_in = len(arrs)

    def body(*refs):
        ins, outs, stage = refs[:n_in], refs[n_in:-1], refs[-1]
        for j, x_ref in enumerate(ins):
            for h in range(HEADS_PER_GROUP):
                stage[h] = x_ref[:, h * HEAD_DIM:(h + 1) * HEAD_DIM]
            for n, dil in enumerate(dilated):
                _to_groups(stage, outs[n * n_in + j], dil)

    nat = pl.BlockSpec((tm, GROUP_WIDTH), lambda i: (i, 0))
    res = _pallas_call(
        body, out_shape=[_group_shape(t, dil, F32) for dil in dilated for _ in arrs], grid=(t // tm,),
        in_specs=[nat] * n_in, out_specs=[_group_spec(tm, dil) for dil in dilated for _ in arrs], scratch_shapes=[_stage(tm)],
        compiler_params=_cparams(("parallel",), 3 * n_in * _nbytes((tm, GROUP_WIDTH), F32)), name=name)(*arrs)
    return [tuple(res[n * n_in:(n + 1) * n_in]) for n in range(len(dilated))]


def _query_mask(has_prev):
    qi = lax.broadcasted_iota(jnp.int32, (ATT_BLOCK, 2 * ATT_BLOCK), 0)
    col = lax.broadcasted_iota(jnp.int32, (ATT_BLOCK, 2 * ATT_BLOCK), 1)
    prev = jnp.logical_and(jnp.logical_and(col < ATT_BLOCK, col >= qi), has_prev)
    return jnp.logical_or(prev, jnp.logical_and(col >= ATT_BLOCK, col - ATT_BLOCK <= qi))


def _key_mask(has_next):
    row = lax.broadcasted_iota(jnp.int32, (2 * ATT_BLOCK, ATT_BLOCK), 0)
    kj = lax.broadcasted_iota(jnp.int32, (2 * ATT_BLOCK, ATT_BLOCK), 1)
    nxt = jnp.logical_and(jnp.logical_and(row >= ATT_BLOCK, kj >= row - ATT_BLOCK), has_next)
    return jnp.logical_or(nxt, jnp.logical_and(row < ATT_BLOCK, kj <= row))


def _scores(q, k):
    return lax.dot_general(q, k, (((1,), (1,)), ((), ())), preferred_element_type=F32) * (HEAD_DIM ** -0.5)


def _att_fwd(q, k, v, offs, dil, name):
    qo, ko, vo = offs
    length = q.shape[0]
    nb = length // ATT_BLOCK

    def body(q_ref, kp_ref, kc_ref, vp_ref, vc_ref, o_ref, lse_ref):
        mask = _query_mask(pl.program_id(1) > 0)
        heads = [slice(h * HEAD_DIM, (h + 1) * HEAD_DIM) for h in range(HEADS_PER_GROUP)]
        ks = [jnp.concatenate([kp_ref[:, sl], kc_ref[:, sl]], axis=0) for sl in heads]
        vs = [jnp.concatenate([vp_ref[:, sl], vc_ref[:, sl]], axis=0) for sl in heads]
        ss = [jnp.where(mask, _scores(q_ref[:, sl], kv), MASKED) for sl, kv in zip(heads, ks)]
        ms = [jnp.max(s, axis=1, keepdims=True) for s in ss]
        ps = [jnp.exp(s - m) for s, m in zip(ss, ms)]
        ls = [jnp.sum(p, axis=1, keepdims=True) for p in ps]
        accs = [jnp.dot(p.astype(BF16), vv, preferred_element_type=F32) for p, vv in zip(ps, vs)]
        for sl, acc, m, l in zip(heads, accs, ms, ls):
            o_ref[:, sl] = acc / l
            lse_ref[:, sl] = jnp.broadcast_to(m + jnp.log(l), (ATT_BLOCK, HEAD_DIM))

    def spec(off, prev):
        if prev:
            return pl.BlockSpec((ATT_BLOCK, GROUP_WIDTH), lambda r, n: (jnp.maximum(n - 1, 0), off + r))
        return pl.BlockSpec((ATT_BLOCK, GROUP_WIDTH), lambda r, n: (n, off + r))

    out = jax.ShapeDtypeStruct((length, dil * GROUP_WIDTH), F32)
    o_spec = pl.BlockSpec((ATT_BLOCK, GROUP_WIDTH), lambda r, n: (n, r))
    return _pallas_call(
        body, out_shape=(out, out), grid=(dil, nb),
        in_specs=[spec(qo, False), spec(ko, True), spec(ko, False), spec(vo, True), spec(vo, False)],
        out_specs=(o_spec, o_spec),
        compiler_params=_cparams(("parallel", "parallel"), 8 * _nbytes((ATT_BLOCK, GROUP_WIDTH), F32)), name=name)(q, k, k, v, v)


def _att_combine(outs, lses, name):
    t = outs[0].shape[0] * DILATIONS[0]
    tm = _rows(t, 512)

    def body(*refs):
        o_refs, l_refs = refs[:N_GROUPS], refs[N_GROUPS:2 * N_GROUPS]
        ob_ref, of_ref, lse_ref = refs[2 * N_GROUPS:2 * N_GROUPS + 3]
        stages = list(refs[2 * N_GROUPS + 3:])
        staged = []
        for o_ref, l_ref, dil in zip(o_refs, l_refs, DILATIONS):
            if dil > 1:
                so, sl = stages.pop(), stages.pop()
                _from_groups(o_ref, so, dil)
                _from_groups(l_ref, sl, dil)
                staged.append((so, sl))
            else:
                staged.append(None)
        for h in range(HEADS_PER_GROUP):
            hs = slice(h * HEAD_DIM, (h + 1) * HEAD_DIM)
            os_ = [o_ref[:, hs] if st is None else st[0][h] for o_ref, st in zip(o_refs, staged)]
            ls = [l_ref[:, hs] if st is None else st[1][h] for l_ref, st in zip(l_refs, staged)]
            m = functools.reduce(jnp.maximum, ls)
            ws = [jnp.exp(l - m) for l in ls]
            den = functools.reduce(jnp.add, ws)
            num = functools.reduce(jnp.add, [w * o for w, o in zip(ws, os_)])
            o = num / den
            ob_ref[:, hs] = o.astype(BF16)
            of_ref[:, hs] = o
            lse_ref[:, hs] = m + jnp.log(den)

    blk = pl.BlockSpec((tm, GROUP_WIDTH), lambda i: (i, 0))
    specs = [blk if dil == 1 else _group_spec(tm, dil) for dil in DILATIONS]
    f32 = jax.ShapeDtypeStruct((t, GROUP_WIDTH), F32)
    n_stage = 2 * sum(dil > 1 for dil in DILATIONS)
    return _pallas_call(
        body, out_shape=(jax.ShapeDtypeStruct((t, GROUP_WIDTH), BF16), f32, f32), grid=(t // tm,),
        in_specs=specs * 2, out_specs=(blk, blk, blk), scratch_shapes=[_stage(tm)] * n_stage,
        compiler_params=_cparams(("parallel",), 13 * _nbytes((tm, GROUP_WIDTH), F32)), name=name)(*outs, *lses)


def _att_delta(do, o, name):
    t = o.shape[0]
    tm = _rows(t, 512)

    def body(do_ref, o_ref, d_ref):
        for h in range(HEADS_PER_GROUP):
            sl = slice(h * HEAD_DIM, (h + 1) * HEAD_DIM)
            s = jnp.sum(do_ref[:, sl] * o_ref[:, sl], axis=1, keepdims=True)
            d_ref[:, sl] = jnp.broadcast_to(s, (tm, HEAD_DIM))

    blk = pl.BlockSpec((tm, GROUP_WIDTH), lambda i: (i, 0))
    return _pallas_call(
        body, out_shape=jax.ShapeDtypeStruct((t, GROUP_WIDTH), F32), grid=(t // tm,), in_specs=[blk, blk], out_specs=blk,
        compiler_params=_cparams(("parallel",), 3 * _nbytes((tm, GROUP_WIDTH), F32)), name=name)(do, o)


def _att_bwd_dq(q, k, v, do, lse, delta, offs, dil, name):
    qo, ko, vo = offs
    length = q.shape[0]
    nb = length // ATT_BLOCK
    scale = HEAD_DIM ** -0.5

    def body(q_ref, kp_ref, kc_ref, vp_ref, vc_ref, do_ref, lse_ref, dl_ref, dq_ref):
        mask = _query_mask(pl.program_id(1) > 0)
        heads = [slice(h * HEAD_DIM, (h + 1) * HEAD_DIM) for h in range(HEADS_PER_GROUP)]
        wide = lambda ref, sl: jnp.concatenate([ref[:, sl], ref[:, sl]], axis=1)
        ks = [jnp.concatenate([kp_ref[:, sl], kc_ref[:, sl]], axis=0) for sl in heads]
        vs = [jnp.concatenate([vp_ref[:, sl], vc_ref[:, sl]], axis=0) for sl in heads]
        ps = [jnp.exp(jnp.where(mask, _scores(q_ref[:, sl], kv), MASKED) - wide(lse_ref, sl)) for sl, kv in zip(heads, ks)]
        dps = [lax.dot_general(do_ref[:, sl].astype(BF16), vv, (((1,), (1,)), ((), ())), preferred_element_type=F32)
               for sl, vv in zip(heads, vs)]
        dss = [(p * (dp - wide(dl_ref, sl)) * scale).astype(BF16) for sl, p, dp in zip(heads, ps, dps)]
        dqs = [jnp.dot(ds, kv, preferred_element_type=F32) for ds, kv in zip(dss, ks)]
        for sl, dq in zip(heads, dqs):
            dq_ref[:, sl] = dq.astype(BF16)

    def spec(off, prev):
        if prev:
            return pl.BlockSpec((ATT_BLOCK, GROUP_WIDTH), lambda r, n: (jnp.maximum(n - 1, 0), off + r))
        return pl.BlockSpec((ATT_BLOCK, GROUP_WIDTH), lambda r, n: (n, off + r))

    own = pl.BlockSpec((ATT_BLOCK, GROUP_WIDTH), lambda r, n: (n, r))
    return _pallas_call(
        body, out_shape=jax.ShapeDtypeStruct((length, dil * GROUP_WIDTH), BF16), grid=(dil, nb),
        in_specs=[spec(qo, False), spec(ko, True), spec(ko, False), spec(vo, True), spec(vo, False), own, own, own],
        out_specs=own,
        compiler_params=_cparams(("parallel", "parallel"), 10 * _nbytes((ATT_BLOCK, GROUP_WIDTH), F32)),
        name=name)(q, k, k, v, v, do, lse, delta)


def _att_bwd_dkv(q, k, v, do, lse, delta, offs, dil, name):
    qo, ko, vo = offs
    length = q.shape[0]
    nb = length // ATT_BLOCK
    scale = HEAD_DIM ** -0.5

    def body(k_ref, v_ref, qc_ref, qn_ref, doc_ref, don_ref, lsec_ref, lsen_ref, dlc_ref, dln_ref, dk_ref, dv_ref):
        mask = _key_mask(pl.program_id(1) < nb - 1)
        heads = [slice(h * HEAD_DIM, (h + 1) * HEAD_DIM) for h in range(HEADS_PER_GROUP)]
        both = lambda cur, nxt, sl: jnp.concatenate([cur[:, sl], nxt[:, sl]], axis=0)
        qs = [both(qc_ref, qn_ref, sl) for sl in heads]
        dos = [both(doc_ref, don_ref, sl).astype(BF16) for sl in heads]
        ps = [jnp.exp(jnp.where(mask, _scores(qv, k_ref[:, sl]), MASKED) - both(lsec_ref, lsen_ref, sl)) for sl, qv in zip(heads, qs)]
        dps = [lax.dot_general(dov, v_ref[:, sl], (((1,), (1,)), ((), ())), preferred_element_type=F32) for sl, dov in zip(heads, dos)]
        dss = [(p * (dp - both(dlc_ref, dln_ref, sl)) * scale).astype(BF16) for sl, p, dp in zip(heads, ps, dps)]
        dvs = [lax.dot_general(p.astype(BF16), dov, (((0,), (0,)), ((), ())), preferred_element_type=F32) for p, dov in zip(ps, dos)]
        dks = [lax.dot_general(ds, qv, (((0,), (0,)), ((), ())), preferred_element_type=F32) for ds, qv in zip(dss, qs)]
        for sl, dk, dv in zip(heads, dks, dvs):
            dk_ref[:, sl] = dk.astype(BF16)
            dv_ref[:, sl] = dv.astype(BF16)

    def spec(off, nxt):
        if nxt:
            return pl.BlockSpec((ATT_BLOCK, GROUP_WIDTH), lambda r, n: (jnp.minimum(n + 1, nb - 1), off + r))
        return pl.BlockSpec((ATT_BLOCK, GROUP_WIDTH), lambda r, n: (n, off + r))

    own = pl.BlockSpec((ATT_BLOCK, GROUP_WIDTH), lambda r, n: (n, r))
    out = jax.ShapeDtypeStruct((length, dil * GROUP_WIDTH), BF16)
    return _pallas_call(
        body, out_shape=(out, out), grid=(dil, nb),
        in_specs=[spec(ko, False), spec(vo, False), spec(qo, False), spec(qo, True), spec(0, False), spec(0, True),
                  spec(0, False), spec(0, True), spec(0, False), spec(0, True)],
        out_specs=(own, own),
        compiler_params=_cparams(("parallel", "parallel"), 12 * _nbytes((ATT_BLOCK, GROUP_WIDTH), F32)),
        name=name)(k, v, q, q, do, do, lse, lse, delta, delta)


def _gelu(x):
    return 0.5 * x * (1.0 + lax.erf(x * (2.0 ** -0.5)))


def _gelu_grad(x):
    return 0.5 * (1.0 + lax.erf(x * (2.0 ** -0.5))) + x * jnp.exp(-0.5 * x * x) * ((2.0 * jnp.pi) ** -0.5)


def _sg_normed(vs, lg, lb):
    gv = _gelu(vs)
    mu = jnp.mean(gv, axis=1, keepdims=True)
    xc = gv - mu
    rstd = lax.rsqrt(jnp.mean(xc * xc, axis=1, keepdims=True) + LN_EPS)
    z = xc * rstd
    return z, rstd, z * lg + lb


def _sg_tril():
    row = lax.broadcasted_iota(jnp.int32, (SG_CHUNK, SG_CHUNK), 0)
    col = lax.broadcasted_iota(jnp.int32, (SG_CHUNK, SG_CHUNK), 1)
    return row >= col


def _sg_fwd(proj, u_blk, vs_blk, lg, lb, sg_w, bias, name):
    t = proj.shape[0]
    width = SG_GROUPS * SG_GROUP_DIM

    def body(u_ref, vs_ref, lg_ref, lb_ref, w_ref, bias_ref, o_ref):
        _, _, vn = _sg_normed(vs_ref[...].astype(F32), lg_ref[...], lb_ref[...])
        vn = vn.astype(BF16)
        tril = _sg_tril()
        for g in range(SG_GROUPS):
            sl = slice(g * SG_GROUP_DIM, (g + 1) * SG_GROUP_DIM)
            w = jnp.where(tril, w_ref[g], 0.0).astype(BF16)
            sp = jnp.dot(w, vn[:, sl], preferred_element_type=F32) + bias_ref[:, sl]
            o_ref[:, sl] = (_gelu(u_ref[:, sl].astype(F32)) * sp).astype(BF16)

    vec = pl.BlockSpec((1, width), lambda i: (0, 0))
    return _pallas_call(
        body, out_shape=jax.ShapeDtypeStruct((t, width), BF16), grid=(t // SG_CHUNK,),
        in_specs=[pl.BlockSpec((SG_CHUNK, width), lambda i: (i, u_blk)), pl.BlockSpec((SG_CHUNK, width), lambda i: (i, vs_blk)),
                  vec, vec, pl.BlockSpec((SG_GROUPS, SG_CHUNK, SG_CHUNK), lambda i: (0, 0, 0)),
                  pl.BlockSpec((SG_CHUNK, width), lambda i: (0, 0))],
        out_specs=pl.BlockSpec((SG_CHUNK, width), lambda i: (i, 0)),
        compiler_params=_cparams(("parallel",), 8 * _nbytes((SG_CHUNK, width), F32)), name=name)(proj, proj, lg, lb, sg_w, bias)


def _sg_bwd(proj, u_blk, vs_blk, dsu, lg, lb, sg_w, bias, name):
    t = proj.shape[0]
    width = SG_GROUPS * SG_GROUP_DIM

    def body(u_ref, vs_ref, dsu_ref, lg_ref, lb_ref, w_ref, bias_ref, du_ref, dvs_ref, dw_ref, dbias_ref, dlg_ref, dlb_ref):
        @pl.when(pl.program_id(0) == 0)
        def _():
            dw_ref[...] = jnp.zeros_like(dw_ref)
            dbias_ref[...] = jnp.zeros_like(dbias_ref)
            dlg_ref[...] = jnp.zeros_like(dlg_ref)
            dlb_ref[...] = jnp.zeros_like(dlb_ref)

        vs = vs_ref[...].astype(F32)
        z, rstd, vn = _sg_normed(vs, lg_ref[...], lb_ref[...])
        vn = vn.astype(BF16)
        tril = _sg_tril()
        dvn = []
        for g in range(SG_GROUPS):
            sl = slice(g * SG_GROUP_DIM, (g + 1) * SG_GROUP_DIM)
            w = jnp.where(tril, w_ref[g], 0.0).astype(BF16)
            vg = vn[:, sl]
            sp = jnp.dot(w, vg, preferred_element_type=F32) + bias_ref[:, sl]
            uv = u_ref[:, sl].astype(F32)
            dsu_g = dsu_ref[:, sl].astype(F32)
            du_ref[:, sl] = (dsu_g * sp * _gelu_grad(uv)).astype(BF16)
            dsp = dsu_g * _gelu(uv)
            dsp_b = dsp.astype(BF16)
            dw = lax.dot_general(dsp_b, vg, (((1,), (1,)), ((), ())), preferred_element_type=F32)
            dw_ref[g] += jnp.where(tril, dw, 0.0)
            dbias_ref[:, sl] += jnp.broadcast_to(jnp.sum(dsp, axis=1, keepdims=True), (SG_CHUNK, SG_GROUP_DIM))
            dvn.append(lax.dot_general(w, dsp_b, (((0,), (0,)), ((), ())), preferred_element_type=F32))
        dvn = jnp.concatenate(dvn, axis=1)
        dlg_ref[...] += jnp.sum(dvn * z, axis=0, keepdims=True)
        dlb_ref[...] += jnp.sum(dvn, axis=0, keepdims=True)
        dz = dvn * lg_ref[...]
        dgv = rstd * (dz - jnp.mean(dz, axis=1, keepdims=True) - z * jnp.mean(dz * z, axis=1, keepdims=True))
        dvs_ref[...] = (dgv * _gelu_grad(vs)).astype(BF16)

    vec = pl.BlockSpec((1, width), lambda i: (0, 0))
    row = pl.BlockSpec((SG_CHUNK, width), lambda i: (i, 0))
    fixed = pl.BlockSpec((SG_CHUNK, width), lambda i: (0, 0))
    w_spec = pl.BlockSpec((SG_GROUPS, SG_CHUNK, SG_CHUNK), lambda i: (0, 0, 0))
    act = jax.ShapeDtypeStruct((t, width), BF16)
    return _pallas_call(
        body,
        out_shape=(act, act, jax.ShapeDtypeStruct((SG_GROUPS, SG_CHUNK, SG_CHUNK), F32),
                   jax.ShapeDtypeStruct((SG_CHUNK, width), F32), jax.ShapeDtypeStruct((1, width), F32),
                   jax.ShapeDtypeStruct((1, width), F32)),
        grid=(t // SG_CHUNK,),
        in_specs=[pl.BlockSpec((SG_CHUNK, width), lambda i: (i, u_blk)), pl.BlockSpec((SG_CHUNK, width), lambda i: (i, vs_blk)),
                  row, vec, vec, w_spec, fixed],
        out_specs=(row, row, w_spec, fixed, vec, vec),
        compiler_params=_cparams(("arbitrary",), 14 * _nbytes((SG_CHUNK, width), F32)),
        name=name)(proj, proj, dsu, lg, lb, sg_w, bias)


def _gate_fwd(proj, ga_blk, gs_blk, y_att, y_sg, name):
    t, d = y_att.shape
    tm, tn = _rows(t, 512), _tile(d, GROUP_WIDTH)

    def body(ga_ref, gs_ref, ya_ref, ys_ref, o_ref):
        o_ref[...] = (_sigmoid(ga_ref[...].astype(F32)) * ya_ref[...].astype(F32)
                      + _sigmoid(gs_ref[...].astype(F32)) * ys_ref[...].astype(F32)).astype(BF16)

    own = pl.BlockSpec((tm, tn), lambda i, j: (i, j))
    return _pallas_call(
        body, out_shape=jax.ShapeDtypeStruct((t, d), BF16), grid=(t // tm, d // tn),
        in_specs=[pl.BlockSpec((tm, tn), lambda i, j: (i, ga_blk + j)), pl.BlockSpec((tm, tn), lambda i, j: (i, gs_blk + j)),
                  own, own],
        out_specs=own, compiler_params=_cparams(("parallel", "parallel"), 6 * _nbytes((tm, tn), F32)),
        name=name)(proj, proj, y_att, y_sg)


def _gate_bwd(proj, ga_blk, gs_blk, y_att, y_sg, dmerged, name):
    t, d = y_att.shape
    tm, tn = _rows(t, 512), _tile(d, GROUP_WIDTH)

    def body(ga_ref, gs_ref, ya_ref, ys_ref, dm_ref, dya_ref, dys_ref, dga_ref, dgs_ref):
        dm = dm_ref[...].astype(F32)
        for g_ref, y_ref, dy_ref, dg_ref in ((ga_ref, ya_ref, dya_ref, dga_ref), (gs_ref, ys_ref, dys_ref, dgs_ref)):
            sg = _sigmoid(g_ref[...].astype(F32))
            dy_ref[...] = (dm * sg).astype(BF16)
            dg_ref[...] = (dm * y_ref[...].astype(F32) * sg * (1.0 - sg)).astype(BF16)

    own = pl.BlockSpec((tm, tn), lambda i, j: (i, j))
    out = jax.ShapeDtypeStruct((t, d), BF16)
    return _pallas_call(
        body, out_shape=(out, out, out, out), grid=(t // tm, d // tn),
        in_specs=[pl.BlockSpec((tm, tn), lambda i, j: (i, ga_blk + j)), pl.BlockSpec((tm, tn), lambda i, j: (i, gs_blk + j)),
                  own, own, own],
        out_specs=(own, own, own, own), compiler_params=_cparams(("parallel", "parallel"), 10 * _nbytes((tm, tn), F32)),
        name=name)(proj, proj, y_att, y_sg, dmerged)


def _mixer_forward(x, wb, small, in_specs, sg_specs, out_specs):
    t, d = x.shape
    att_w = N_GROUPS * GROUP_WIDTH
    sg_w = SG_GROUPS * SG_GROUP_DIM
    n = _rmsnorm_fwd(x, small['mix_norm'], "mix_norm")
    rider, names = _gather(wb, in_specs)
    proj, got = _matmul([(n, _full(wb, 'w_in'))], 'nn', BF16, "mix_in", b3=True, caps=(1024, 1024, 1024), rider=rider)
    _landed(wb, names, got)
    tables = _rope_tables(t)
    qk0, grouped = _rope_fwd(proj, tables, "mix_rope")
    qkv = [(qk0, qk0, proj, (0, 1, 2 * N_GROUPS))] + [g + ((0, 0, 0),) for g in grouped]
    outs, lses = zip(*[_att_fwd(*args, dil, f"att_fwd{gi}") for gi, (args, dil) in enumerate(zip(qkv, DILATIONS))])
    o_b, o_f, lse = _att_combine(outs, lses, "att_combine")
    y_att = _matmul([(o_b, _full(wb, 'w_att_out'))], 'nn', BF16, "mix_att_out", b3=True)
    bias = jnp.repeat(small['sg_b'].T, SG_GROUP_DIM, axis=1)
    u_blk, vs_blk = 3 * att_w // sg_w, 3 * att_w // sg_w + 1
    su = _sg_fwd(proj, u_blk, vs_blk, small['sg_ln_g'], small['sg_ln_b'], small['sg_w'], bias, "sg_fwd")
    rider, names = _gather(wb, sg_specs)
    y_sg, got = _matmul([(su, _full(wb, 'w_sg_out'))], 'nn', BF16, "mix_sg_out", b3=True, rider=rider)
    _landed(wb, names, got)
    ga_blk = (3 * att_w + 2 * sg_w) // _tile(d, GROUP_WIDTH)
    gs_blk = ga_blk + d // _tile(d, GROUP_WIDTH)
    merged = _gate_fwd(proj, ga_blk, gs_blk, y_att, y_sg, "gate_fwd")
    rider, names = _gather(wb, out_specs)
    x_next, got = _matmul([(merged, _full(wb, 'w_out'))], 'nn', F32, "mix_out", residual=x, rider=rider)
    _landed(wb, names, got)
    saved = (n, proj, qkv, tables, o_b, o_f, lse, y_att, su, y_sg, merged, bias, (u_blk, vs_blk, ga_blk, gs_blk))
    return x_next, saved


def _mixer_backward(x, wb, small, saved, dx_next, dx_next_b, c_idx, first_rider):
    n, proj, qkv, tables, o_b, o_f, lse, y_att, su, y_sg, merged, bias, (u_blk, vs_blk, ga_blk, gs_blk) = saved
    s = N_CHIPS
    dmerged, carried = _matmul([(dx_next_b, _full(wb, 'w_out'))], 'nt', BF16, "mix_out_dx", rider=first_rider)
    g_w_out = _matmul([(merged, dx_next_b)], 'tn', BF16, "mix_out_dw", caps=(1024, 1024, 1024))
    dy_att, dy_sg, dg_att, dg_sg = _gate_bwd(proj, ga_blk, gs_blk, y_att, y_sg, dmerged, "gate_bwd")

    g_w_att_out = _matmul([(o_b, dy_att)], 'tn', BF16, "mix_att_out_dw", out3=s)
    do = _matmul([(dy_att, _full(wb, 'w_att_out'))], 'nt', F32, "mix_att_out_dx", b3=True)
    delta = _att_delta(do, o_f, "att_delta")
    stats = [(do, lse, delta)] + _regroup([do, lse, delta], "att_regroup")
    dqkv = []
    for gi, ((q, k, v, offs), st, dil) in enumerate(zip(qkv, stats, DILATIONS)):
        dq = _att_bwd_dq(q, k, v, *st, offs, dil, f"att_bwd_dq{gi}")
        dk, dv = _att_bwd_dkv(q, k, v, *st, offs, dil, f"att_bwd_dkv{gi}")
        dqkv.append((dq, dk, dv))
    dqkv = _rope_bwd(*dqkv[0], dqkv[1:], tables, "mix_rope_bwd")

    g_w_sg_out = _matmul([(su, dy_sg)], 'tn', BF16, "mix_sg_out_dw", out3=s)
    out_names = ['w_out', 'w_att_out', 'w_sg_out']
    out_parts = _reduce_first([g_w_out, g_w_att_out, g_w_sg_out], out_names, wb, c_idx)
    dsu = _matmul([(dy_sg, _full(wb, 'w_sg_out'))], 'nt', BF16, "mix_sg_out_dx", b3=True)
    du, dvs, g_sg_w, g_bias, g_lg, g_lb = _sg_bwd(proj, u_blk, vs_blk, dsu, small['sg_ln_g'], small['sg_ln_b'],
                                                   small['sg_w'], bias, "sg_bwd")
    gs = {'sg_w': g_sg_w, 'sg_b': g_bias[:, ::SG_GROUP_DIM].T, 'sg_ln_g': g_lg, 'sg_ln_b': g_lb}

    dproj = jnp.concatenate([dqkv, du, dvs, dg_att, dg_sg], axis=1)
    g_w_in, out_recv = _matmul([(n, dproj)], 'tn', BF16, "mix_in_dw", out3=s, caps=(1024, 1024, 1024),
                               rider=_scatter_rider(out_parts))
    (p_w_in,) = _reduce_first([g_w_in], ['w_in'], wb, c_idx)
    dn, (r_w_in,) = _matmul([(dproj, _full(wb, 'w_in'))], 'nt', F32, "mix_in_dx", b3=True, caps=(1024, 1024, 512),
                            rider=_scatter_rider([p_w_in]))
    dx, dx_b, gs['mix_norm'] = _rmsnorm_bwd(x, small['mix_norm'], dn, dx_next, "mix_norm_bwd")
    g = {nm: (p, r) for nm, p, r in zip(out_names, out_parts, out_recv)}
    g['w_in'] = (p_w_in, r_w_in)
    return dx, dx_b, g, gs, carried


def _step(x, target, wb, small, c_idx, pc_idx):
    def last_stage(g):
        names = list(g)
        return names, _halves_rider([_chip_sum(*g[n], pc_idx, f"rs_sum2_{n}") for n in names])

    wb = dict(wb)
    rider, names = _gather(wb, ['ffn1_w_gate', 'ffn1_w_up'])
    _landed(wb, names, _exchange(rider, "gather_first"))
    half_in = wb['w_in'].shape[2] // 2
    x1, s1 = _ffn_forward(x, small['ffn1_norm'], wb, "ffn1", ['ffn1_w_down', ('w_in', 0, half_in)], [('w_in', half_in, 2 * half_in)])
    up_rows = wb['ffn2_w_up'].shape[2]
    up_cut = up_rows // 32 * 15
    x2, s2 = _mixer_forward(x1, wb, small, ['w_att_out', 'w_sg_out', 'w_out', 'ffn2_w_gate'],
                            [('ffn2_w_up', 0, up_cut)], [('ffn2_w_up', up_cut, up_rows)])
    x3, s3 = _ffn_forward(x2, small['ffn2_norm'], wb, "ffn2", ['ffn2_w_down'], None)
    loss, dx3, dx3_b, g_final = _final_loss(x3, small['final_norm'], target, "final_loss")
    gs = {'final_norm': g_final}
    whole = {}
    dx2, dx2_b, gs['ffn2_norm'], g, _ = _ffn_backward(x2, small['ffn2_norm'], wb, s3, dx3, dx3_b, c_idx, "ffn2")
    names, rider = last_stage(g)
    dx1, dx1_b, g, gs_mix, got = _mixer_backward(x1, wb, small, s2, dx2, dx2_b, c_idx, rider)
    whole.update(zip(names, got))
    gs.update(gs_mix)
    names, rider = last_stage(g)
    dx0, _, gs['ffn1_norm'], g, got = _ffn_backward(x, small['ffn1_norm'], wb, s1, dx1, dx1_b, c_idx, "ffn1", dwd_rider=rider)
    whole.update(zip(names, got))
    names, rider = last_stage(g)
    whole.update(zip(names, _exchange(rider, "rs_halves")))
    return loss, dx0, whole, gs


def _cast_into_gathered(wt, p_idx, name):
    r, ccols = wt.shape[0] // 2, wt.shape[1]
    tm = _rows(r, 256)
    nb = r // tm

    def body(p_ref, w_ref, o_ref):
        o_ref[...] = w_ref[...].astype(BF16)

    grid_spec = pltpu.PrefetchScalarGridSpec(
        num_scalar_prefetch=1, grid=(2, nb),
        in_specs=[pl.BlockSpec((tm, ccols), lambda h, i, pr: (h * nb + i, 0))],
        out_specs=pl.BlockSpec((None, None, tm, ccols), lambda h, i, pr: (pr[0], h, i, 0)))
    return pl.pallas_call(body, out_shape=jax.ShapeDtypeStruct((N_CHIPS, 2, r, ccols), BF16), grid_spec=grid_spec,
                          compiler_params=_cparams(("parallel", "parallel"), 2 * _nbytes((tm, ccols), F32)), name=name)(p_idx, wt)


def _sibling_exchange(grads, name):
    nw = len(grads)

    def body(*refs):
        src, dst = refs[:nw], refs[nw:2 * nw]
        send_sems, recv_sems = refs[2 * nw:]
        x, y, c, _ = _place()
        cps = []
        for i in range(nw):
            cp = pltpu.make_async_remote_copy(src[i].at[:, 1 - c], dst[i], send_sems.at[i], recv_sems.at[i],
                                              device_id=(x, y, 1 - c), device_id_type=MESH)
            cp.start()
            cps.append(cp)
        for cp in cps:
            cp.wait()

    any_spec = pl.BlockSpec(memory_space=pl.ANY)
    return _pallas_call(
        body, out_shape=[jax.ShapeDtypeStruct((g.shape[0],) + g.shape[2:], g.dtype) for g in grads],
        in_specs=[any_spec] * nw, out_specs=[any_spec] * nw,
        scratch_shapes=[pltpu.SemaphoreType.DMA((nw,)), pltpu.SemaphoreType.DMA((nw,))],
        compiler_params=pltpu.CompilerParams(has_side_effects=True), name=name)(*grads)


def _halves_rider(bufs):
    n = len(bufs)

    def copy(ref, sems, i, c, x, y):
        return pltpu.make_async_remote_copy(ref, ref, sems[0].at[i], sems[1].at[i], device_id=(x, y, 1 - c), device_id_type=MESH)

    def start(_, buf, sems):
        x, y, c, _ = _place()
        for i in range(n):
            copy(buf[i].at[c], sems, i, c, x, y).start()

    def finish(_, buf, sems):
        x, y, c, _ = _place()
        for i in range(n):
            copy(buf[i].at[c], sems, i, c, x, y).wait_send()
            copy(buf[i].at[1 - c], sems, i, c, x, y).wait_recv()

    return _Rider(bufs, [jax.ShapeDtypeStruct(b.shape, b.dtype) for b in bufs], {i: i for i in range(n)},
                  [pltpu.SemaphoreType.DMA((n,))] * 2, start, finish)


def _sibling_sum(grad, recv, c_idx, name):
    s, _, r, ccols = grad.shape
    tm = _rows(r, 256)

    def body(c_ref, g_ref, r_ref, o_ref):
        o_ref[...] = (g_ref[...].astype(F32) + r_ref[...].astype(F32)).astype(BF16)

    grid_spec = pltpu.PrefetchScalarGridSpec(
        num_scalar_prefetch=1, grid=(s, r // tm),
        in_specs=[pl.BlockSpec((None, None, tm, ccols), lambda q, i, cr: (q, cr[0], i, 0)),
                  pl.BlockSpec((None, tm, ccols), lambda q, i, cr: (q, i, 0))],
        out_specs=pl.BlockSpec((None, tm, ccols), lambda q, i, cr: (q, i, 0)))
    return pl.pallas_call(body, out_shape=jax.ShapeDtypeStruct((s, r, ccols), BF16), grid_spec=grid_spec,
                          compiler_params=_cparams(("parallel", "parallel"), 4 * _nbytes((tm, ccols), F32)), name=name)(c_idx, grad, recv)


def _chip_sum(part, recv, pc_idx, name):
    _, r, ccols = part.shape
    tm = _rows(r, 256)

    def body(pc_ref, own_ref, r0_ref, r1_ref, r2_ref, o_ref):
        acc = own_ref[...].astype(F32) + r0_ref[...].astype(F32)
        acc = acc + r1_ref[...].astype(F32)
        o_ref[...] = acc + r2_ref[...].astype(F32)

    def slot(j):
        return pl.BlockSpec((None, tm, ccols), lambda i, pc: (j, i, 0))

    grid_spec = pltpu.PrefetchScalarGridSpec(
        num_scalar_prefetch=1, grid=(r // tm,),
        in_specs=[pl.BlockSpec((None, tm, ccols), lambda i, pc: (pc[0], i, 0)), slot(0), slot(1), slot(2)],
        out_specs=pl.BlockSpec((None, tm, ccols), lambda i, pc: (pc[1], i, 0)))
    return pl.pallas_call(body, out_shape=jax.ShapeDtypeStruct((2, r, ccols), F32), grid_spec=grid_spec,
                          compiler_params=_cparams(("parallel",), 6 * _nbytes((tm, ccols), F32)), name=name)(pc_idx, part, recv, recv, recv)


def _all_reduce_small(vec):
    _, r, _ = vec.shape

    def body(v_ref, o_ref, parts, send1, recv1, send2, recv2):
        x, y, c, _ = _place()
        me = 4 * x + 2 * y + c
        peers = []
        for k in range(1, N_DEV):
            px, py, pc = (1 - x if k & 4 else x, 1 - y if k & 2 else y, 1 - c if k & 1 else c)
            peers.append(((px, py, pc), 4 * px + 2 * py + pc))
        parts[me] = v_ref[me]
        cps = []
        for k, (peer, peer_id) in enumerate(peers):
            cp = pltpu.make_async_remote_copy(v_ref.at[peer_id], parts.at[me], send1.at[k], recv1.at[k],
                                              device_id=peer, device_id_type=MESH)
            cp.start()
            cps.append(cp)
        for cp in cps:
            cp.wait()
        acc = parts[0]
        for dev in range(1, N_DEV):
            acc = acc + parts[dev]
        o_ref[me] = acc
        cps = []
        for k, (peer, _) in enumerate(peers):
            cp = pltpu.make_async_remote_copy(o_ref.at[me], o_ref.at[me], send2.at[k], recv2.at[k],
                                              device_id=peer, device_id_type=MESH)
            cp.start()
            cps.append(cp)
        for cp in cps:
            cp.wait()

    vm = pl.BlockSpec(memory_space=pltpu.VMEM)
    sems = pltpu.SemaphoreType.DMA((N_DEV - 1,))
    return pl.pallas_call(
        body, out_shape=jax.ShapeDtypeStruct(vec.shape, F32), in_specs=[vm], out_specs=vm,
        scratch_shapes=[pltpu.VMEM((N_DEV, r, LANES), F32), sems, sems, sems, sems],
        compiler_params=pltpu.CompilerParams(vmem_limit_bytes=int(8 * _nbytes((N_DEV, r, LANES), F32))),
        name="all_reduce_small")(vec)


def _adamw(wt, g, m, v, name, rider=None):
    r, ccols = wt.shape
    tm = _rows(r, max(8, (MIB // (4 * ccols)) // 8 * 8))
    blk = pl.BlockSpec((tm, ccols), lambda i: (i, 0))

    def body(w_ref, g_ref, m_ref, v_ref, go_ref, d_ref, mo_ref, vo_ref):
        gv = g_ref[...]
        go_ref[...] = gv
        mv = ADAM_B1 * m_ref[...] + (1.0 - ADAM_B1) * gv
        vv = ADAM_B2 * v_ref[...] + (1.0 - ADAM_B2) * (gv * gv)
        m_hat = mv / (1.0 - ADAM_B1 ** ADAM_STEP)
        v_hat = vv / (1.0 - ADAM_B2 ** ADAM_STEP)
        d_ref[...] = -ADAM_LR * (m_hat / (jnp.sqrt(v_hat) + ADAM_EPS) + ADAM_WD * w_ref[...])
        mo_ref[...] = mv
        vo_ref[...] = vv

    out = jax.ShapeDtypeStruct((r, ccols), F32)
    return _run(body, name=name, grid=(r // tm,), in_specs=[blk] * 4, out_specs=[blk] * 4, out_shape=[out] * 4, scratch_shapes=[],
                operands=[wt, g, m, v], block_bytes=8 * _nbytes((tm, ccols), F32), rider=rider, pinned=False)


def _as_rows(a):
    rows = a.reshape(-1, LANES)
    return jnp.pad(rows, ((0, -rows.shape[0] % 8), (0, 0)))


def kernel(x, ffn1_norm, ffn1_w_gate, ffn1_w_up, ffn1_w_down, mix_norm, w_in, sg_ln_g, sg_ln_b, sg_w, sg_b, w_att_out, w_sg_out, w_out, ffn2_norm, ffn2_w_gate, ffn2_w_up, ffn2_w_down, final_norm, loss_target, m_ffn1_norm, m_ffn1_w_gate, m_ffn1_w_up, m_ffn1_w_down, m_mix_norm, m_w_in, m_sg_ln_g, m_sg_ln_b, m_sg_w, m_sg_b, m_w_att_out, m_w_sg_out, m_w_out, m_ffn2_norm, m_ffn2_w_gate, m_ffn2_w_up, m_ffn2_w_down, m_final_norm, v_ffn1_norm, v_ffn1_w_gate, v_ffn1_w_up, v_ffn1_w_down, v_mix_norm, v_w_in, v_sg_ln_g, v_sg_ln_b, v_sg_w, v_sg_b, v_w_att_out, v_w_sg_out, v_w_out, v_ffn2_norm, v_ffn2_w_gate, v_ffn2_w_up, v_ffn2_w_down, v_final_norm):
    given = dict(locals())
    wts = {n: given[n] for n in WEIGHT_NAMES}
    ms = {n: given["m_" + n] for n in WEIGHT_NAMES}
    vs = {n: given["v_" + n] for n in WEIGHT_NAMES}
    t, d = x.shape[-2], x.shape[-1]
    xc, yc, cc = lax.axis_index("x"), lax.axis_index("y"), lax.axis_index("c")

    shard2d = {n: wts[n].reshape(wts[n].shape[-2:]) for n in BIG_NAMES}
    p_idx = jnp.reshape(2 * xc + yc, (1,)).astype(jnp.int32)
    c_idx = jnp.reshape(cc, (1,)).astype(jnp.int32)
    pc_idx = jnp.stack([2 * xc + yc, cc]).astype(jnp.int32)
    wb = {n: _cast_into_gathered(shard2d[n], p_idx, f"cast_{n}") for n in BIG_NAMES}

    small = {n: wts[n].reshape(-1, wts[n].shape[-1]) for n in SMALL_NAMES}
    small['sg_w'] = wts['sg_w'].reshape(wts['sg_w'].shape[-3:])
    loss, dx, whole, gs = _step(x.reshape(t, d), loss_target.reshape(t, d), wb, small, c_idx, pc_idx)
    loss = lax.psum(loss[0, 0], ("x", "y", "c"))

    def pack(tree):
        rows = jnp.concatenate([_as_rows(tree[n]) for n in SMALL_NAMES], axis=0)
        return jnp.pad(rows, ((0, -rows.shape[0] % (8 * N_DEV)), (0, 0)))

    packed = pack(gs)
    packed = _all_reduce_small(packed.reshape(N_DEV, -1, LANES)).reshape(packed.shape)

    grads, delta, new_m, new_v = {}, {}, {}, {}
    for n in BIG_NAMES:
        shape, flat = wts[n].shape, shard2d[n].shape
        out = _adamw(shard2d[n], whole[n].reshape(flat), ms[n].reshape(flat), vs[n].reshape(flat), f"adamw_{n}")
        grads[n], delta[n], new_m[n], new_v[n] = (a.reshape(shape) for a in out)

    small_out = _adamw(pack(wts), packed, pack(ms), pack(vs), "adamw_small")
    row = 0
    for n in SMALL_NAMES:
        shape = wts[n].shape
        sz = wts[n].size // LANES
        grads[n], delta[n], new_m[n], new_v[n] = (a[row:row + sz].reshape(shape) for a in small_out)
        row += sz + -sz % 8

    return (loss, dx.reshape(x.shape), *[grads[n] for n in WEIGHT_NAMES], *[delta[n] for n in WEIGHT_NAMES],
            *[new_m[n] for n in WEIGHT_NAMES], *[new_v[n] for n in WEIGHT_NAMES])
```

```python
import functools

import jax
import jax.numpy as jnp
from jax import lax
from jax.experimental import pallas as pl
from jax.experimental.pallas import tpu as pltpu

F32 = jnp.float32
BF16 = jnp.bfloat16
MESH = pl.DeviceIdType.MESH

NORM_EPS = 1e-6
LN_EPS = 1e-5
HEAD_DIM = 128
HEADS_PER_GROUP = 4
GROUP_WIDTH = HEADS_PER_GROUP * HEAD_DIM
DILATIONS = (1, 4, 16)
N_GROUPS = len(DILATIONS)
ATT_BLOCK = 128
ROPE_DIM = HEAD_DIM // 4
ROPE_THETA = 500000.0
SG_CHUNK = 128
SG_GROUPS = 12
SG_GROUP_DIM = 128
MASKED = -1e30

ADAM_LR = 0.001
ADAM_B1 = 0.9
ADAM_B2 = 0.999
ADAM_EPS = 1e-08
ADAM_WD = 0.01
ADAM_STEP = 10

N_CHIPS = 4
N_DEV = 8
LANES = 128
MIB = 2 ** 20
VMEM_BYTES_V7X = 64 * MIB

WEIGHT_NAMES = ['ffn1_norm', 'ffn1_w_gate', 'ffn1_w_up', 'ffn1_w_down', 'mix_norm', 'w_in', 'sg_ln_g', 'sg_ln_b',
                'sg_w', 'sg_b', 'w_att_out', 'w_sg_out', 'w_out', 'ffn2_norm', 'ffn2_w_gate', 'ffn2_w_up',
                'ffn2_w_down', 'final_norm']
BIG = [('ffn1_w_gate', 1), ('ffn1_w_up', 1), ('ffn1_w_down', 0), ('w_in', 1), ('w_att_out', 1), ('w_sg_out', 1),
       ('w_out', 0), ('ffn2_w_gate', 1), ('ffn2_w_up', 1), ('ffn2_w_down', 0)]
BIG_NAMES = [n for n, _ in BIG]
SMALL_NAMES = [n for n in WEIGHT_NAMES if n not in BIG_NAMES]


def _nbytes(shape, dtype):
    n = jnp.dtype(dtype).itemsize
    for s in shape:
        if s is not None:
            n *= s
    return n


def _pallas_call(*args, **kw):
    kw['out_shape'] = jax.tree.map(lambda s: pltpu.HBM(s.shape, s.dtype), kw['out_shape'])
    call = pl.pallas_call(*args, **kw)

    def pinned(*operands):
        return call(*[o if jnp.issubdtype(o.dtype, jnp.integer) else pltpu.with_memory_space_constraint(o, pltpu.HBM)
                      for o in operands])

    return pinned


def _cparams(sem, block_bytes, **kw):
    limit = int(min(max(3 * block_bytes, 32 * MIB), VMEM_BYTES_V7X - 8 * MIB))
    return pltpu.CompilerParams(dimension_semantics=sem, vmem_limit_bytes=limit, **kw)


def _tile(dim, cap):
    best = None
    for t in range(LANES, min(dim, cap) + 1, LANES):
        if dim % t == 0:
            best = t
    if best is None:
        assert dim <= cap, (dim, cap)
        return dim
    return best


def _rows(dim, cap):
    best = None
    for t in range(8, min(dim, cap) + 1, 8):
        if dim % t == 0:
            best = t
    assert best is not None, (dim, cap)
    return best


def _place():
    x, y, c = lax.axis_index("x"), lax.axis_index("y"), lax.axis_index("c")
    others = [(1 - x, y), (x, 1 - y), (1 - x, 1 - y)]
    return x, y, c, others


class _Rider:
    def __init__(self, operands, out_shapes, aliases, sems, start, finish):
        self.operands = operands
        self.out_shapes = out_shapes
        self.aliases = aliases
        self.sems = sems
        self.start = start
        self.finish = finish


def _run(body, *, name, grid, in_specs, out_specs, out_shape, scratch_shapes, operands, block_bytes, rider=None, pinned=True):
    call = _pallas_call if pinned else pl.pallas_call
    if rider is None:
        sem = ("parallel",) * (len(grid) - 1) + ("arbitrary",)
        return call(body, out_shape=out_shape, grid=grid, in_specs=in_specs, out_specs=out_specs,
                    scratch_shapes=scratch_shapes, compiler_params=_cparams(sem, block_bytes), name=name)(*operands)
    n_in, n_out, n_scr = len(operands), len(out_shape), len(scratch_shapes)
    r_in, r_out = len(rider.operands), len(rider.out_shapes)
    any_spec = pl.BlockSpec(memory_space=pl.ANY)

    def wrapped(*refs):
        ins, refs = refs[:n_in], refs[n_in:]
        r_ins, refs = refs[:r_in], refs[r_in:]
        outs, refs = refs[:n_out], refs[n_out:]
        r_outs, refs = refs[:r_out], refs[r_out:]
        scr, sems = refs[:n_scr], refs[n_scr:]
        if not grid:
            rider.start(r_ins, r_outs, sems)
            rider.finish(r_ins, r_outs, sems)
            return
        ids = [pl.program_id(a) for a in range(len(grid))]
        first = functools.reduce(jnp.logical_and, [i == 0 for i in ids])
        last = functools.reduce(jnp.logical_and, [i == g - 1 for i, g in zip(ids, grid)])

        @pl.when(first)
        def _():
            rider.start(r_ins, r_outs, sems)

        body(*ins, *outs, *scr)

        @pl.when(last)
        def _():
            rider.finish(r_ins, r_outs, sems)

    results = call(
        wrapped, out_shape=list(out_shape) + list(rider.out_shapes), grid=grid,
        in_specs=list(in_specs) + [any_spec] * r_in, out_specs=list(out_specs) + [any_spec] * r_out,
        scratch_shapes=list(scratch_shapes) + list(rider.sems),
        input_output_aliases={n_in + k: n_out + v for k, v in rider.aliases.items()},
        compiler_params=_cparams(("arbitrary",) * len(grid) if grid else None, block_bytes, has_side_effects=True),
        name=name)(*operands, *rider.operands)
    return results[:n_out], results[n_out:]


def _exchange(rider, name):
    return _run(None, name=name, grid=(), in_specs=[], out_specs=[], out_shape=[], scratch_shapes=[], operands=[],
                block_bytes=0, rider=rider)[1]


def _gather_rider(items):
    bufs, index = [], []
    for b, r0, r1 in items:
        if not any(b is q for q in bufs):
            bufs.append(b)
        index.append(([k for k, q in enumerate(bufs) if q is b][0], r0, r1))
    n = len(index)

    def piece(refs, k, chip, half):
        bi, r0, r1 = index[k]
        return refs[bi].at[chip, half, pl.ds(r0, r1 - r0)]

    def copy(ref, sem_pair, k, j, to):
        return pltpu.make_async_remote_copy(ref, ref, sem_pair[0].at[k, j], sem_pair[1].at[k, j], device_id=to, device_id_type=MESH)

    def start(r_ins, buf, sems):
        x, y, c, others = _place()
        for k in range(n):
            for j, (ox, oy) in enumerate(others):
                copy(piece(buf, k, 2 * x + y, c), sems[:2], k, j, (ox, oy, c)).start()

    def finish(r_ins, buf, sems):
        x, y, c, others = _place()
        for k in range(n):
            for j, (ox, oy) in enumerate(others):
                got = piece(buf, k, 2 * ox + oy, c)
                copy(got, sems[:2], k, j, (ox, oy, c)).wait_recv()
                copy(got, sems[2:], k, j, (x, y, 1 - c)).start()
        for k in range(n):
            for j, (ox, oy) in enumerate(others):
                copy(piece(buf, k, 2 * ox + oy, 1 - c), sems[2:], k, j, (x, y, 1 - c)).wait_recv()
        for k in range(n):
            for j, (ox, oy) in enumerate(others):
                copy(piece(buf, k, 2 * x + y, c), sems[:2], k, j, (ox, oy, c)).wait_send()
                copy(piece(buf, k, 2 * ox + oy, c), sems[2:], k, j, (x, y, 1 - c)).wait_send()

    return _Rider(bufs, [jax.ShapeDtypeStruct(b.shape, b.dtype) for b in bufs], {i: i for i in range(len(bufs))},
                  [pltpu.SemaphoreType.DMA((n, 3))] * 4, start, finish)


def _scatter_rider(parts):
    n = len(parts)

    def copy(src, dst, sems, i, j, to):
        return pltpu.make_async_remote_copy(src, dst, sems[0].at[i, j], sems[1].at[i, j], device_id=to, device_id_type=MESH)

    def start(src, dst, sems):
        x, y, c, others = _place()
        for i in range(n):
            for j, (ox, oy) in enumerate(others):
                copy(src[i].at[2 * ox + oy], dst[i].at[j], sems, i, j, (ox, oy, c)).start()

    def finish(src, dst, sems):
        x, y, c, others = _place()
        for i in range(n):
            for j, (ox, oy) in enumerate(others):
                copy(src[i].at[2 * ox + oy], dst[i].at[j], sems, i, j, (ox, oy, c)).wait()

    return _Rider(parts, [jax.ShapeDtypeStruct((3,) + p.shape[1:], p.dtype) for p in parts], {},
                  [pltpu.SemaphoreType.DMA((n, 3))] * 2, start, finish)


def _matmul(pairs, mode, out_dtype, name, *, scale=1.0, residual=None, b3=False, out3=0, caps=(1024, 1024, 512), rider=None):
    a0, b0 = pairs[0]
    if mode == 'nn':
        m, k = a0.shape
        n = b0.shape[0] * b0.shape[2] if b3 else b0.shape[1]
    elif mode == 'nt':
        m = a0.shape[0]
        n, k = (b0.shape[1], b0.shape[0] * b0.shape[2]) if b3 else b0.shape
    else:
        k, m = a0.shape
        n = b0.shape[1]
    tm = _tile(m, caps[0])
    tn = _tile(n, caps[1])
    tk = _tile(k, caps[2])
    if b3 and mode == 'nn':
        tn = b0.shape[2]
    if b3 and mode == 'nt':
        tk = b0.shape[2]
    if out3:
        tn = n // out3
    nk = k // tk
    if mode == 'tn':
        a_spec = pl.BlockSpec((tk, tm), lambda i, j, kk: (kk, i))
        b_spec = pl.BlockSpec((tk, tn), lambda i, j, kk: (kk, j))
        dims = ((0,), (0,))
    elif mode == 'nn':
        a_spec = pl.BlockSpec((tm, tk), lambda i, j, kk: (i, kk))
        b_spec = (pl.BlockSpec((None, tk, tn), lambda i, j, kk: (j, kk, 0)) if b3
                  else pl.BlockSpec((tk, tn), lambda i, j, kk: (kk, j)))
        dims = ((1,), (0,))
    else:
        a_spec = pl.BlockSpec((tm, tk), lambda i, j, kk: (i, kk))
        b_spec = (pl.BlockSpec((None, tn, tk), lambda i, j, kk: (kk, j, 0)) if b3
                  else pl.BlockSpec((tn, tk), lambda i, j, kk: (j, kk)))
        dims = ((1,), (1,))
    in_specs, operands = [], []
    for a, b in pairs:
        in_specs += [a_spec, b_spec]
        operands += [a, b]
    block_bytes = len(pairs) * (_nbytes((tm, tk), a0.dtype) + _nbytes((tk, tn), b0.dtype))
    if residual is not None:
        in_specs.append(pl.BlockSpec((tm, tn), lambda i, j, kk: (i, j)))
        operands.append(residual)
        block_bytes += _nbytes((tm, tn), F32)
    if out3:
        out_spec = pl.BlockSpec((None, tm, tn), lambda i, j, kk: (j, i, 0))
        out_shape = jax.ShapeDtypeStruct((out3, m, tn), out_dtype)
    else:
        out_spec = pl.BlockSpec((tm, tn), lambda i, j, kk: (i, j))
        out_shape = jax.ShapeDtypeStruct((m, n), out_dtype)
    block_bytes += _nbytes((tm, tn), out_dtype) + _nbytes((tm, tn), F32)
    n_pairs = len(pairs)
    has_res = residual is not None

    def body(*refs):
        o_ref, acc = refs[-2], refs[-1]
        kk = pl.program_id(2)

        def product():
            part = None
            for p in range(n_pairs):
                d = lax.dot_general(refs[2 * p][...].astype(BF16), refs[2 * p + 1][...].astype(BF16),
                                    (dims, ((), ())), preferred_element_type=F32)
                part = d if part is None else part + d
            return part

        def finish(r):
            if scale != 1.0:
                r = r * scale
            if has_res:
                r = refs[2 * n_pairs][...] + r
            o_ref[...] = r.astype(out_dtype)

        if nk == 1:
            finish(product())
            return

        @pl.when(kk == 0)
        def _():
            acc[...] = product()

        if nk > 2:
            @pl.when(jnp.logical_and(kk > 0, kk < nk - 1))
            def _():
                acc[...] += product()

        @pl.when(kk == nk - 1)
        def _():
            finish(acc[...] + product())

    res = _run(body, name=name, grid=(m // tm, n // tn, nk), in_specs=in_specs, out_specs=[out_spec], out_shape=[out_shape],
               scratch_shapes=[pltpu.VMEM((tm, tn), F32)], operands=operands, block_bytes=block_bytes, rider=rider)
    return res[0] if rider is None else (res[0][0], res[1])


def _rmsnorm_fwd(x, g, name):
    t, d = x.shape
    tm = _rows(t, 512)

    def body(x_ref, g_ref, o_ref):
        xv = x_ref[...]
        r = lax.rsqrt(jnp.mean(xv * xv, axis=1, keepdims=True) + NORM_EPS)
        o_ref[...] = (xv * r * g_ref[...]).astype(BF16)

    row = pl.BlockSpec((tm, d), lambda i: (i, 0))
    return _pallas_call(
        body, out_shape=jax.ShapeDtypeStruct((t, d), BF16), grid=(t // tm,),
        in_specs=[row, pl.BlockSpec((1, d), lambda i: (0, 0))], out_specs=row,
        compiler_params=_cparams(("parallel",), 2 * _nbytes((tm, d), F32)), name=name)(x, g)


def _rms_grad(xv, g, dn, d):
    r = lax.rsqrt(jnp.mean(xv * xv, axis=1, keepdims=True) + NORM_EPS)
    u = dn * g
    s = jnp.sum(xv * u, axis=1, keepdims=True)
    dx = r * u - xv * (r * r * r) * (s * (1.0 / d))
    return dx, dn * xv * r


def _rmsnorm_bwd(x, g, dn, dres, name):
    t, d = x.shape
    tm = _rows(t, 256)

    def body(x_ref, g_ref, dn_ref, dres_ref, dx_ref, dxb_ref, dg_ref):
        dx, dg_rows = _rms_grad(x_ref[...], g_ref[...], dn_ref[...].astype(F32), d)
        dx = dres_ref[...] + dx
        dx_ref[...] = dx
        dxb_ref[...] = dx.astype(BF16)

        @pl.when(pl.program_id(0) == 0)
        def _():
            dg_ref[...] = jnp.zeros_like(dg_ref)

        dg_ref[...] += jnp.sum(dg_rows, axis=0, keepdims=True)

    row = pl.BlockSpec((tm, d), lambda i: (i, 0))
    vec = pl.BlockSpec((1, d), lambda i: (0, 0))
    return _pallas_call(
        body, out_shape=(jax.ShapeDtypeStruct((t, d), F32), jax.ShapeDtypeStruct((t, d), BF16), jax.ShapeDtypeStruct((1, d), F32)),
        grid=(t // tm,), in_specs=[row, vec, row, row], out_specs=(row, row, vec),
        compiler_params=_cparams(("arbitrary",), 5 * _nbytes((tm, d), F32)), name=name)(x, g, dn, dres)


def _final_loss(x, g, target, name):
    t, d = x.shape
    tm = _rows(t, 256)

    def body(x_ref, g_ref, t_ref, loss_ref, dx_ref, dxb_ref, dg_ref):
        xv, gv = x_ref[...], g_ref[...]
        r = lax.rsqrt(jnp.mean(xv * xv, axis=1, keepdims=True) + NORM_EPS)
        err = xv * r * gv - t_ref[...]
        dx, dg_rows = _rms_grad(xv, gv, err * (1.0 / d), d)
        dx_ref[...] = dx
        dxb_ref[...] = dx.astype(BF16)

        @pl.when(pl.program_id(0) == 0)
        def _():
            dg_ref[...] = jnp.zeros_like(dg_ref)
            loss_ref[...] = jnp.zeros_like(loss_ref)

        dg_ref[...] += jnp.sum(dg_rows, axis=0, keepdims=True)
        row_loss = jnp.sum(err * err, axis=1, keepdims=True) * (0.5 / d)
        loss_ref[...] += jnp.sum(row_loss, axis=0, keepdims=True)

    row = pl.BlockSpec((tm, d), lambda i: (i, 0))
    vec = pl.BlockSpec((1, d), lambda i: (0, 0))
    return _pallas_call(
        body, out_shape=(jax.ShapeDtypeStruct((1, 1), F32), jax.ShapeDtypeStruct((t, d), F32),
                         jax.ShapeDtypeStruct((t, d), BF16), jax.ShapeDtypeStruct((1, d), F32)),
        grid=(t // tm,), in_specs=[row, vec, row], out_specs=(pl.BlockSpec((1, 1), lambda i: (0, 0)), row, row, vec),
        compiler_params=_cparams(("arbitrary",), 4 * _nbytes((tm, d), F32)), name=name)(x, g, target)


def _sigmoid(x):
    return 0.5 * jnp.tanh(0.5 * x) + 0.5


def _ffn_up(n, wg, wu, name, rider=None):
    t, d = n.shape
    s, _, f = wg.shape
    tm, tk = _tile(t, 1024), _tile(d, 1024)
    nk = d // tk

    def body(n_ref, wg_ref, wu_ref, a_ref, b_ref, h_ref, acc_g, acc_u):
        kk = pl.program_id(2)

        def products():
            nv = n_ref[...]
            return jnp.dot(nv, wg_ref[...], preferred_element_type=F32), jnp.dot(nv, wu_ref[...], preferred_element_type=F32)

        def finish(a, b):
            a_ref[...] = a.astype(BF16)
            b_ref[...] = b.astype(BF16)
            h_ref[...] = (a * _sigmoid(a) * b).astype(BF16)

        if nk == 1:
            finish(*products())
            return

        @pl.when(kk == 0)
        def _():
            acc_g[...], acc_u[...] = products()

        if nk > 2:
            @pl.when(jnp.logical_and(kk > 0, kk < nk - 1))
            def _():
                pg, pu = products()
                acc_g[...] += pg
                acc_u[...] += pu

        @pl.when(kk == nk - 1)
        def _():
            pg, pu = products()
            finish(acc_g[...] + pg, acc_u[...] + pu)

    w_spec = pl.BlockSpec((None, tk, f), lambda i, j, kk: (j, kk, 0))
    o_spec = pl.BlockSpec((tm, f), lambda i, j, kk: (i, j))
    out = jax.ShapeDtypeStruct((t, s * f), BF16)
    block_bytes = _nbytes((tm, tk), BF16) + 2 * _nbytes((tk, f), BF16) + 3 * _nbytes((tm, f), BF16) + 2 * _nbytes((tm, f), F32)
    return _run(body, name=name, grid=(t // tm, s, nk),
                in_specs=[pl.BlockSpec((tm, tk), lambda i, j, kk: (i, kk)), w_spec, w_spec], out_specs=[o_spec, o_spec, o_spec],
                out_shape=[out, out, out], scratch_shapes=[pltpu.VMEM((tm, f), F32), pltpu.VMEM((tm, f), F32)],
                operands=[n, wg, wu], block_bytes=block_bytes, rider=rider)


def _ffn_bwd_act(dx, wd, a, b, name):
    t, d = dx.shape
    f = wd.shape[0]
    tm, tn, tk = _tile(t, 1024), _tile(f, 1536), _tile(d, 1024)
    nk = d // tk

    def body(dx_ref, wd_ref, a_ref, b_ref, da_ref, db_ref, acc):
        kk = pl.program_id(2)

        def product():
            return lax.dot_general(dx_ref[...], wd_ref[...], (((1,), (1,)), ((), ())), preferred_element_type=F32)

        def finish(r):
            dh = 0.5 * r
            av, bv = a_ref[...].astype(F32), b_ref[...].astype(F32)
            sg = _sigmoid(av)
            da_ref[...] = (dh * bv * (sg * (1.0 + av * (1.0 - sg)))).astype(BF16)
            db_ref[...] = (dh * (av * sg)).astype(BF16)

        if nk == 1:
            finish(product())
            return

        @pl.when(kk == 0)
        def _():
            acc[...] = product()

        if nk > 2:
            @pl.when(jnp.logical_and(kk > 0, kk < nk - 1))
            def _():
                acc[...] += product()

        @pl.when(kk == nk - 1)
        def _():
            finish(acc[...] + product())

    act = pl.BlockSpec((tm, tn), lambda i, j, kk: (i, j))
    out = jax.ShapeDtypeStruct((t, f), BF16)
    block_bytes = _nbytes((tm, tk), BF16) + _nbytes((tn, tk), BF16) + 4 * _nbytes((tm, tn), BF16) + _nbytes((tm, tn), F32)
    return _pallas_call(
        body, out_shape=(out, out), grid=(t // tm, f // tn, nk),
        in_specs=[pl.BlockSpec((tm, tk), lambda i, j, kk: (i, kk)), pl.BlockSpec((tn, tk), lambda i, j, kk: (j, kk)),
                  act, act],
        out_specs=(act, act), scratch_shapes=[pltpu.VMEM((tm, tn), F32)],
        compiler_params=_cparams(("parallel", "parallel", "arbitrary"), block_bytes), name=name)(dx, wd, a, b)


AXIS = dict(BIG)


def _full(wb, n):
    _, _, r, ccols = wb[n].shape
    return wb[n].reshape(N_CHIPS, 2 * r, ccols) if AXIS[n] == 1 else wb[n].reshape(N_CHIPS * 2 * r, ccols)


def _gather(wb, specs):
    items, names = [], []
    for s in specs:
        n, r0, r1 = (s, 0, wb[s].shape[2]) if isinstance(s, str) else s
        items.append((wb[n], r0, r1))
        if n not in names:
            names.append(n)
    return _gather_rider(items), names


def _landed(wb, names, results):
    for n, r in zip(names, results):
        wb[n] = r


def _reduce_first(grads, names, wb, c_idx):
    g4 = [g.reshape(wb[n].shape) for g, n in zip(grads, names)]
    from_sibling = _exchange(_sibling_rider(g4), "rs_sibling_" + names[0])
    return [_sibling_sum(a, b, c_idx, f"rs_sum1_{n}") for a, b, n in zip(g4, from_sibling, names)]


def _ffn_forward(x, gain, wb, tag, up_specs, down_specs):
    n = _rmsnorm_fwd(x, gain, f"{tag}_norm")
    rider, names = _gather(wb, up_specs)
    (a, b, h), got = _ffn_up(n, _full(wb, f"{tag}_w_gate"), _full(wb, f"{tag}_w_up"), f"{tag}_up", rider=rider)
    _landed(wb, names, got)
    down = dict(scale=0.5, residual=x, caps=(1024, 1024, 1536))
    if down_specs:
        rider, names = _gather(wb, down_specs)
        x_next, got = _matmul([(h, _full(wb, f"{tag}_w_down"))], 'nn', F32, f"{tag}_down", rider=rider, **down)
        _landed(wb, names, got)
    else:
        x_next = _matmul([(h, _full(wb, f"{tag}_w_down"))], 'nn', F32, f"{tag}_down", **down)
    return x_next, (n, a, b, h)


def _ffn_backward(x, gain, wb, saved, dx_next, dx_next_b, c_idx, tag, chained, dwd_rider=None):
    n, a, b, h = saved
    wg, wu, wd = (f"{tag}_w_gate", f"{tag}_w_up", f"{tag}_w_down")
    da, db = _ffn_bwd_act(dx_next_b, _full(wb, wd), a, b, f"{tag}_bwd_act")
    res = _matmul([(h, dx_next_b)], 'tn', BF16, f"{tag}_dwd", scale=0.5, caps=(1536, 2048, 1024), rider=dwd_rider)
    g_wd, carried = (res, ()) if dwd_rider is None else res
    grad_mm = dict(out3=N_CHIPS, caps=(2048, 1024, 1024))
    dn_pairs = [(da, _full(wb, wg)), (db, _full(wb, wu))]
    if not chained:
        (p_wd,) = _reduce_first([g_wd], [wd], wb, c_idx)
        g_wg, (r_wd,) = _matmul([(n, da)], 'tn', BF16, f"{tag}_dwg", rider=_scatter_rider([p_wd]), **grad_mm)
        g_wu = _matmul([(n, db)], 'tn', BF16, f"{tag}_dwu", **grad_mm)
        p_wg, p_wu = _reduce_first([g_wg, g_wu], [wg, wu], wb, c_idx)
        dn, (r_wg, r_wu) = _matmul(dn_pairs, 'nt', F32, f"{tag}_dn", b3=True, rider=_scatter_rider([p_wg, p_wu]))
        done, pending = {wg: (p_wg, r_wg), wu: (p_wu, r_wu), wd: (p_wd, r_wd)}, {}
    else:
        g_wd = g_wd.reshape(wb[wd].shape)
        g_wg, (s_wd,) = _matmul([(n, da)], 'tn', BF16, f"{tag}_dwg", rider=_sibling_rider([g_wd]), **grad_mm)
        p_wd = _sibling_sum(g_wd, s_wd, c_idx, f"rs_sum1_{wd}")
        g_wg = g_wg.reshape(wb[wg].shape)
        g_wu, (r_wd, s_wg) = _matmul([(n, db)], 'tn', BF16, f"{tag}_dwu",
                                     rider=_merge_riders(_scatter_rider([p_wd]), _sibling_rider([g_wg])), **grad_mm)
        p_wg = _sibling_sum(g_wg, s_wg, c_idx, f"rs_sum1_{wg}")
        g_wu = g_wu.reshape(wb[wu].shape)
        dn, (r_wg, s_wu) = _matmul(dn_pairs, 'nt', F32, f"{tag}_dn", b3=True,
                                   rider=_merge_riders(_scatter_rider([p_wg]), _sibling_rider([g_wu])))
        p_wu = _sibling_sum(g_wu, s_wu, c_idx, f"rs_sum1_{wu}")
        done, pending = {wg: (p_wg, r_wg), wd: (p_wd, r_wd)}, {wu: p_wu}
    dx, dx_b, g_gain = _rmsnorm_bwd(x, gain, dn, dx_next, f"{tag}_norm_bwd")
    return dx, dx_b, g_gain, done, pending, carried


def _rope_tables(seq):
    half = ROPE_DIM // 2
    inv_freq = ROPE_THETA ** (-jnp.arange(0, ROPE_DIM, 2, dtype=F32) / ROPE_DIM)
    ang = jnp.arange(seq).astype(F32)[:, None] * inv_freq[None, :]
    cos, sin = jnp.cos(ang), jnp.sin(ang)
    zeros = lambda w: jnp.zeros((seq, w), F32)
    c = jnp.concatenate([cos, cos, jnp.ones((seq, HEAD_DIM - ROPE_DIM), F32)], axis=1)
    s_up = jnp.concatenate([-sin, zeros(HEAD_DIM - half)], axis=1)
    s_dn = jnp.concatenate([zeros(half), sin, zeros(HEAD_DIM - ROPE_DIM)], axis=1)
    return c, s_up, s_dn


def _rotate(xv, cv, uv, dv):
    half = ROPE_DIM // 2
    return xv * cv + pltpu.roll(xv, HEAD_DIM - half, 1) * uv + pltpu.roll(xv, half, 1) * dv


def _stage(tm):
    return pltpu.VMEM((HEADS_PER_GROUP, tm, HEAD_DIM), F32)


def _to_groups(stage, o_ref, dil):
    rows = stage.shape[1] // dil
    for r in range(dil):
        for h in range(HEADS_PER_GROUP):
            col = r * GROUP_WIDTH + h * HEAD_DIM
            o_ref[:, col:col + HEAD_DIM] = stage[h, pl.ds(r, rows, stride=dil), :].astype(o_ref.dtype)


def _from_groups(g_ref, stage, dil):
    rows = stage.shape[1] // dil
    for r in range(dil):
        for h in range(HEADS_PER_GROUP):
            col = r * GROUP_WIDTH + h * HEAD_DIM
            stage[h, pl.ds(r, rows, stride=dil), :] = g_ref[:, col:col + HEAD_DIM].astype(F32)


def _group_spec(tm, dil):
    return pl.BlockSpec((tm // dil, dil * GROUP_WIDTH), lambda i: (i, 0))


def _group_shape(t, dil, dtype):
    return jax.ShapeDtypeStruct((t // dil, dil * GROUP_WIDTH), dtype)


def _rope_fwd(proj, tables, name):
    t = proj.shape[0]
    tm = _rows(t, 512)
    att_w = N_GROUPS * GROUP_WIDTH
    dilated = [(gi, dil) for gi, dil in enumerate(DILATIONS) if dil > 1]

    def body(x_ref, c_ref, up_ref, dn_ref, qk0_ref, *rest):
        outs, stage = rest[:-1], rest[-1]
        cv, uv, dv = c_ref[...], up_ref[...], dn_ref[...]
        for part in range(2):
            for gi, dil in enumerate(DILATIONS):
                for h in range(HEADS_PER_GROUP):
                    col = part * att_w + gi * GROUP_WIDTH + h * HEAD_DIM
                    y = _rotate(x_ref[:, col:col + HEAD_DIM].astype(F32), cv, uv, dv)
                    if dil == 1:
                        qk0_ref[:, part * GROUP_WIDTH + h * HEAD_DIM:part * GROUP_WIDTH + (h + 1) * HEAD_DIM] = y.astype(BF16)
                    else:
                        stage[h] = y
                if dil > 1:
                    _to_groups(stage, outs[3 * dilated.index((gi, dil)) + part], dil)
        for n, (gi, dil) in enumerate(dilated):
            col = 2 * att_w + gi * GROUP_WIDTH
            for h in range(HEADS_PER_GROUP):
                stage[h] = x_ref[:, col + h * HEAD_DIM:col + (h + 1) * HEAD_DIM].astype(F32)
            _to_groups(stage, outs[3 * n + 2], dil)

    tab = pl.BlockSpec((tm, HEAD_DIM), lambda i: (i, 0))
    out_shape = [jax.ShapeDtypeStruct((t, 2 * GROUP_WIDTH), BF16)]
    out_specs = [pl.BlockSpec((tm, 2 * GROUP_WIDTH), lambda i: (i, 0))]
    for _, dil in dilated:
        out_shape += [_group_shape(t, dil, BF16)] * 3
        out_specs += [_group_spec(tm, dil)] * 3
    res = _pallas_call(
        body, out_shape=out_shape, grid=(t // tm,),
        in_specs=[pl.BlockSpec((tm, 3 * att_w), lambda i: (i, 0)), tab, tab, tab], out_specs=out_specs,
        scratch_shapes=[_stage(tm)],
        compiler_params=_cparams(("parallel",), 4 * _nbytes((tm, 3 * att_w), BF16)), name=name)(proj, *tables)
    return res[0], [tuple(res[1 + 3 * n:4 + 3 * n]) for n in range(len(dilated))]


def _rope_bwd(dq0, dk0, dv0, grouped, tables, name):
    t = dq0.shape[0]
    tm = _rows(t, 512)
    att_w = N_GROUPS * GROUP_WIDTH
    dilated = [(gi, dil) for gi, dil in enumerate(DILATIONS) if dil > 1]
    c, s_up, s_dn = tables

    def body(c_ref, up_ref, dn_ref, dq0_ref, dk0_ref, dv0_ref, *rest):
        g_refs, o_ref, stage = rest[:-2], rest[-2], rest[-1]
        cv, uv, dv = c_ref[...], -up_ref[...], -dn_ref[...]
        for part, first in enumerate((dq0_ref, dk0_ref)):
            for gi, dil in enumerate(DILATIONS):
                if dil > 1:
                    _from_groups(g_refs[3 * dilated.index((gi, dil)) + part], stage, dil)
                for h in range(HEADS_PER_GROUP):
                    sl = slice(h * HEAD_DIM, (h + 1) * HEAD_DIM)
                    xv = first[:, sl].astype(F32) if dil == 1 else stage[h]
                    col = part * att_w + gi * GROUP_WIDTH + h * HEAD_DIM
                    o_ref[:, col:col + HEAD_DIM] = _rotate(xv, cv, uv, dv).astype(BF16)
        for gi, dil in enumerate(DILATIONS):
            col = 2 * att_w + gi * GROUP_WIDTH
            if dil == 1:
                o_ref[:, col:col + GROUP_WIDTH] = dv0_ref[...]
            else:
                _from_groups(g_refs[3 * dilated.index((gi, dil)) + 2], stage, dil)
                for h in range(HEADS_PER_GROUP):
                    o_ref[:, col + h * HEAD_DIM:col + (h + 1) * HEAD_DIM] = stage[h].astype(BF16)

    tab = pl.BlockSpec((tm, HEAD_DIM), lambda i: (i, 0))
    nat = pl.BlockSpec((tm, GROUP_WIDTH), lambda i: (i, 0))
    in_specs, operands = [tab, tab, tab, nat, nat, nat], [c, s_up, s_dn, dq0, dk0, dv0]
    for (_, dil), arrs in zip(dilated, grouped):
        in_specs += [_group_spec(tm, dil)] * 3
        operands += list(arrs)
    return _pallas_call(
        body, out_shape=jax.ShapeDtypeStruct((t, 3 * att_w), BF16), grid=(t // tm,), in_specs=in_specs,
        out_specs=pl.BlockSpec((tm, 3 * att_w), lambda i: (i, 0)), scratch_shapes=[_stage(tm)],
        compiler_params=_cparams(("parallel",), 4 * _nbytes((tm, 3 * att_w), BF16)), name=name)(*operands)


def _regroup(arrs, name):
    t = arrs[0].shape[0]
    tm = _rows(t, 512)
    dilated = [dil for dil in DILATIONS if dil > 1]
    n_in = len(arrs)

    def body(*refs):
        ins, outs, stage = refs[:n_in], refs[n_in:-1], refs[-1]
        for j, x_ref in enumerate(ins):
            for h in range(HEADS_PER_GROUP):
                stage[h] = x_ref[:, h * HEAD_DIM:(h + 1) * HEAD_DIM]
            for n, dil in enumerate(dilated):
                _to_groups(stage, outs[n * n_in + j], dil)

    nat = pl.BlockSpec((tm, GROUP_WIDTH), lambda i: (i, 0))
    res = _pallas_call(
        body, out_shape=[_group_shape(t, dil, F32) for dil in dilated for _ in arrs], grid=(t // tm,),
        in_specs=[nat] * n_in, out_specs=[_group_spec(tm, dil) for dil in dilated for _ in arrs], scratch_shapes=[_stage(tm)],
        compiler_params=_cparams(("parallel",), 3 * n_in * _nbytes((tm, GROUP_WIDTH), F32)), name=name)(*arrs)
    return [tuple(res[n * n_in:(n + 1) * n_in]) for n in range(len(dilated))]


def _query_mask(has_prev):
    qi = lax.broadcasted_iota(jnp.int32, (ATT_BLOCK, 2 * ATT_BLOCK), 0)
    col = lax.broadcasted_iota(jnp.int32, (ATT_BLOCK, 2 * ATT_BLOCK), 1)
    prev = jnp.logical_and(jnp.logical_and(col < ATT_BLOCK, col >= qi), has_prev)
    return jnp.logical_or(prev, jnp.logical_and(col >= ATT_BLOCK, col - ATT_BLOCK <= qi))


def _key_mask(has_next):
    row = lax.broadcasted_iota(jnp.int32, (2 * ATT_BLOCK, ATT_BLOCK), 0)
    kj = lax.broadcasted_iota(jnp.int32, (2 * ATT_BLOCK, ATT_BLOCK), 1)
    nxt = jnp.logical_and(jnp.logical_and(row >= ATT_BLOCK, kj >= row - ATT_BLOCK), has_next)
    return jnp.logical_or(nxt, jnp.logical_and(row < ATT_BLOCK, kj <= row))


def _scores(q, k):
    return lax.dot_general(q, k, (((1,), (1,)), ((), ())), preferred_element_type=F32) * (HEAD_DIM ** -0.5)


def _att_fwd(q, k, v, offs, dil, name):
    qo, ko, vo = offs
    length = q.shape[0]
    nb = length // ATT_BLOCK

    def body(q_ref, kp_ref, kc_ref, vp_ref, vc_ref, o_ref, lse_ref):
        mask = _query_mask(pl.program_id(1) > 0)
        heads = [slice(h * HEAD_DIM, (h + 1) * HEAD_DIM) for h in range(HEADS_PER_GROUP)]
        ks = [jnp.concatenate([kp_ref[:, sl], kc_ref[:, sl]], axis=0) for sl in heads]
        vs = [jnp.concatenate([vp_ref[:, sl], vc_ref[:, sl]], axis=0) for sl in heads]
        ss = [jnp.where(mask, _scores(q_ref[:, sl], kv), MASKED) for sl, kv in zip(heads, ks)]
        ms = [jnp.max(s, axis=1, keepdims=True) for s in ss]
        ps = [jnp.exp(s - m) for s, m in zip(ss, ms)]
        ls = [jnp.sum(p, axis=1, keepdims=True) for p in ps]
        accs = [jnp.dot(p.astype(BF16), vv, preferred_element_type=F32) for p, vv in zip(ps, vs)]
        for sl, acc, m, l in zip(heads, accs, ms, ls):
            o_ref[:, sl] = acc / l
            lse_ref[:, sl] = jnp.broadcast_to(m + jnp.log(l), (ATT_BLOCK, HEAD_DIM))

    def spec(off, prev):
        if prev:
            return pl.BlockSpec((ATT_BLOCK, GROUP_WIDTH), lambda r, n: (jnp.maximum(n - 1, 0), off + r))
        return pl.BlockSpec((ATT_BLOCK, GROUP_WIDTH), lambda r, n: (n, off + r))

    out = jax.ShapeDtypeStruct((length, dil * GROUP_WIDTH), F32)
    o_spec = pl.BlockSpec((ATT_BLOCK, GROUP_WIDTH), lambda r, n: (n, r))
    return _pallas_call(
        body, out_shape=(out, out), grid=(dil, nb),
        in_specs=[spec(qo, False), spec(ko, True), spec(ko, False), spec(vo, True), spec(vo, False)],
        out_specs=(o_spec, o_spec),
        compiler_params=_cparams(("parallel", "parallel"), 8 * _nbytes((ATT_BLOCK, GROUP_WIDTH), F32)), name=name)(q, k, k, v, v)


def _att_combine(outs, lses, name):
    t = outs[0].shape[0] * DILATIONS[0]
    tm = _rows(t, 512)

    def body(*refs):
        o_refs, l_refs = refs[:N_GROUPS], refs[N_GROUPS:2 * N_GROUPS]
        ob_ref, of_ref, lse_ref = refs[2 * N_GROUPS:2 * N_GROUPS + 3]
        stages = list(refs[2 * N_GROUPS + 3:])
        staged = []
        for o_ref, l_ref, dil in zip(o_refs, l_refs, DILATIONS):
            if dil > 1:
                so, sl = stages.pop(), stages.pop()
                _from_groups(o_ref, so, dil)
                _from_groups(l_ref, sl, dil)
                staged.append((so, sl))
            else:
                staged.append(None)
        for h in range(HEADS_PER_GROUP):
            hs = slice(h * HEAD_DIM, (h + 1) * HEAD_DIM)
            os_ = [o_ref[:, hs] if st is None else st[0][h] for o_ref, st in zip(o_refs, staged)]
            ls = [l_ref[:, hs] if st is None else st[1][h] for l_ref, st in zip(l_refs, staged)]
            m = functools.reduce(jnp.maximum, ls)
            ws = [jnp.exp(l - m) for l in ls]
            den = functools.reduce(jnp.add, ws)
            num = functools.reduce(jnp.add, [w * o for w, o in zip(ws, os_)])
            o = num / den
            ob_ref[:, hs] = o.astype(BF16)
            of_ref[:, hs] = o
            lse_ref[:, hs] = m + jnp.log(den)

    blk = pl.BlockSpec((tm, GROUP_WIDTH), lambda i: (i, 0))
    specs = [blk if dil == 1 else _group_spec(tm, dil) for dil in DILATIONS]
    f32 = jax.ShapeDtypeStruct((t, GROUP_WIDTH), F32)
    n_stage = 2 * sum(dil > 1 for dil in DILATIONS)
    return _pallas_call(
        body, out_shape=(jax.ShapeDtypeStruct((t, GROUP_WIDTH), BF16), f32, f32), grid=(t // tm,),
        in_specs=specs * 2, out_specs=(blk, blk, blk), scratch_shapes=[_stage(tm)] * n_stage,
        compiler_params=_cparams(("parallel",), 13 * _nbytes((tm, GROUP_WIDTH), F32)), name=name)(*outs, *lses)


def _att_delta(do, o, name):
    t = o.shape[0]
    tm = _rows(t, 512)

    def body(do_ref, o_ref, d_ref):
        for h in range(HEADS_PER_GROUP):
            sl = slice(h * HEAD_DIM, (h + 1) * HEAD_DIM)
            s = jnp.sum(do_ref[:, sl] * o_ref[:, sl], axis=1, keepdims=True)
            d_ref[:, sl] = jnp.broadcast_to(s, (tm, HEAD_DIM))

    blk = pl.BlockSpec((tm, GROUP_WIDTH), lambda i: (i, 0))
    return _pallas_call(
        body, out_shape=jax.ShapeDtypeStruct((t, GROUP_WIDTH), F32), grid=(t // tm,), in_specs=[blk, blk], out_specs=blk,
        compiler_params=_cparams(("parallel",), 3 * _nbytes((tm, GROUP_WIDTH), F32)), name=name)(do, o)


def _att_bwd_dq(q, k, v, do, lse, delta, offs, dil, name):
    qo, ko, vo = offs
    length = q.shape[0]
    nb = length // ATT_BLOCK
    scale = HEAD_DIM ** -0.5

    def body(q_ref, kp_ref, kc_ref, vp_ref, vc_ref, do_ref, lse_ref, dl_ref, dq_ref):
        mask = _query_mask(pl.program_id(1) > 0)
        heads = [slice(h * HEAD_DIM, (h + 1) * HEAD_DIM) for h in range(HEADS_PER_GROUP)]
        wide = lambda ref, sl: jnp.concatenate([ref[:, sl], ref[:, sl]], axis=1)
        ks = [jnp.concatenate([kp_ref[:, sl], kc_ref[:, sl]], axis=0) for sl in heads]
        vs = [jnp.concatenate([vp_ref[:, sl], vc_ref[:, sl]], axis=0) for sl in heads]
        ps = [jnp.exp(jnp.where(mask, _scores(q_ref[:, sl], kv), MASKED) - wide(lse_ref, sl)) for sl, kv in zip(heads, ks)]
        dps = [lax.dot_general(do_ref[:, sl].astype(BF16), vv, (((1,), (1,)), ((), ())), preferred_element_type=F32)
               for sl, vv in zip(heads, vs)]
        dss = [(p * (dp - wide(dl_ref, sl)) * scale).astype(BF16) for sl, p, dp in zip(heads, ps, dps)]
        dqs = [jnp.dot(ds, kv, preferred_element_type=F32) for ds, kv in zip(dss, ks)]
        for sl, dq in zip(heads, dqs):
            dq_ref[:, sl] = dq.astype(BF16)

    def spec(off, prev):
        if prev:
            return pl.BlockSpec((ATT_BLOCK, GROUP_WIDTH), lambda r, n: (jnp.maximum(n - 1, 0), off + r))
        return pl.BlockSpec((ATT_BLOCK, GROUP_WIDTH), lambda r, n: (n, off + r))

    own = pl.BlockSpec((ATT_BLOCK, GROUP_WIDTH), lambda r, n: (n, r))
    return _pallas_call(
        body, out_shape=jax.ShapeDtypeStruct((length, dil * GROUP_WIDTH), BF16), grid=(dil, nb),
        in_specs=[spec(qo, False), spec(ko, True), spec(ko, False), spec(vo, True), spec(vo, False), own, own, own],
        out_specs=own,
        compiler_params=_cparams(("parallel", "parallel"), 10 * _nbytes((ATT_BLOCK, GROUP_WIDTH), F32)),
        name=name)(q, k, k, v, v, do, lse, delta)


def _att_bwd_dkv(q, k, v, do, lse, delta, offs, dil, name):
    qo, ko, vo = offs
    length = q.shape[0]
    nb = length // ATT_BLOCK
    scale = HEAD_DIM ** -0.5

    def body(k_ref, v_ref, qc_ref, qn_ref, doc_ref, don_ref, lsec_ref, lsen_ref, dlc_ref, dln_ref, dk_ref, dv_ref):
        mask = _key_mask(pl.program_id(1) < nb - 1)
        heads = [slice(h * HEAD_DIM, (h + 1) * HEAD_DIM) for h in range(HEADS_PER_GROUP)]
        both = lambda cur, nxt, sl: jnp.concatenate([cur[:, sl], nxt[:, sl]], axis=0)
        qs = [both(qc_ref, qn_ref, sl) for sl in heads]
        dos = [both(doc_ref, don_ref, sl).astype(BF16) for sl in heads]
        ps = [jnp.exp(jnp.where(mask, _scores(qv, k_ref[:, sl]), MASKED) - both(lsec_ref, lsen_ref, sl)) for sl, qv in zip(heads, qs)]
        dps = [lax.dot_general(dov, v_ref[:, sl], (((1,), (1,)), ((), ())), preferred_element_type=F32) for sl, dov in zip(heads, dos)]
        dss = [(p * (dp - both(dlc_ref, dln_ref, sl)) * scale).astype(BF16) for sl, p, dp in zip(heads, ps, dps)]
        dvs = [lax.dot_general(p.astype(BF16), dov, (((0,), (0,)), ((), ())), preferred_element_type=F32) for p, dov in zip(ps, dos)]
        dks = [lax.dot_general(ds, qv, (((0,), (0,)), ((), ())), preferred_element_type=F32) for ds, qv in zip(dss, qs)]
        for sl, dk, dv in zip(heads, dks, dvs):
            dk_ref[:, sl] = dk.astype(BF16)
            dv_ref[:, sl] = dv.astype(BF16)

    def spec(off, nxt):
        if nxt:
            return pl.BlockSpec((ATT_BLOCK, GROUP_WIDTH), lambda r, n: (jnp.minimum(n + 1, nb - 1), off + r))
        return pl.BlockSpec((ATT_BLOCK, GROUP_WIDTH), lambda r, n: (n, off + r))

    own = pl.BlockSpec((ATT_BLOCK, GROUP_WIDTH), lambda r, n: (n, r))
    out = jax.ShapeDtypeStruct((length, dil * GROUP_WIDTH), BF16)
    return _pallas_call(
        body, out_shape=(out, out), grid=(dil, nb),
        in_specs=[spec(ko, False), spec(vo, False), spec(qo, False), spec(qo, True), spec(0, False), spec(0, True),
                  spec(0, False), spec(0, True), spec(0, False), spec(0, True)],
        out_specs=(own, own),
        compiler_params=_cparams(("parallel", "parallel"), 12 * _nbytes((ATT_BLOCK, GROUP_WIDTH), F32)),
        name=name)(k, v, q, q, do, do, lse, lse, delta, delta)


def _gelu(x):
    return 0.5 * x * (1.0 + lax.erf(x * (2.0 ** -0.5)))


def _gelu_grad(x):
    return 0.5 * (1.0 + lax.erf(x * (2.0 ** -0.5))) + x * jnp.exp(-0.5 * x * x) * ((2.0 * jnp.pi) ** -0.5)


def _sg_normed(vs, lg, lb):
    gv = _gelu(vs)
    mu = jnp.mean(gv, axis=1, keepdims=True)
    xc = gv - mu
    rstd = lax.rsqrt(jnp.mean(xc * xc, axis=1, keepdims=True) + LN_EPS)
    z = xc * rstd
    return z, rstd, z * lg + lb


def _sg_tril():
    row = lax.broadcasted_iota(jnp.int32, (SG_CHUNK, SG_CHUNK), 0)
    col = lax.broadcasted_iota(jnp.int32, (SG_CHUNK, SG_CHUNK), 1)
    return row >= col


def _sg_fwd(proj, u_blk, vs_blk, lg, lb, sg_w, bias, name):
    t = proj.shape[0]
    width = SG_GROUPS * SG_GROUP_DIM

    def body(u_ref, vs_ref, lg_ref, lb_ref, w_ref, bias_ref, o_ref):
        _, _, vn = _sg_normed(vs_ref[...].astype(F32), lg_ref[...], lb_ref[...])
        vn = vn.astype(BF16)
        tril = _sg_tril()
        for g in range(SG_GROUPS):
            sl = slice(g * SG_GROUP_DIM, (g + 1) * SG_GROUP_DIM)
            w = jnp.where(tril, w_ref[g], 0.0).astype(BF16)
            sp = jnp.dot(w, vn[:, sl], preferred_element_type=F32) + bias_ref[:, sl]
            o_ref[:, sl] = (_gelu(u_ref[:, sl].astype(F32)) * sp).astype(BF16)

    vec = pl.BlockSpec((1, width), lambda i: (0, 0))
    return _pallas_call(
        body, out_shape=jax.ShapeDtypeStruct((t, width), BF16), grid=(t // SG_CHUNK,),
        in_specs=[pl.BlockSpec((SG_CHUNK, width), lambda i: (i, u_blk)), pl.BlockSpec((SG_CHUNK, width), lambda i: (i, vs_blk)),
                  vec, vec, pl.BlockSpec((SG_GROUPS, SG_CHUNK, SG_CHUNK), lambda i: (0, 0, 0)),
                  pl.BlockSpec((SG_CHUNK, width), lambda i: (0, 0))],
        out_specs=pl.BlockSpec((SG_CHUNK, width), lambda i: (i, 0)),
        compiler_params=_cparams(("parallel",), 8 * _nbytes((SG_CHUNK, width), F32)), name=name)(proj, proj, lg, lb, sg_w, bias)


def _sg_bwd(proj, u_blk, vs_blk, dsu, lg, lb, sg_w, bias, name):
    t = proj.shape[0]
    width = SG_GROUPS * SG_GROUP_DIM

    def body(u_ref, vs_ref, dsu_ref, lg_ref, lb_ref, w_ref, bias_ref, du_ref, dvs_ref, dw_ref, dbias_ref, dlg_ref, dlb_ref):
        @pl.when(pl.program_id(0) == 0)
        def _():
            dw_ref[...] = jnp.zeros_like(dw_ref)
            dbias_ref[...] = jnp.zeros_like(dbias_ref)
            dlg_ref[...] = jnp.zeros_like(dlg_ref)
            dlb_ref[...] = jnp.zeros_like(dlb_ref)

        vs = vs_ref[...].astype(F32)
        z, rstd, vn = _sg_normed(vs, lg_ref[...], lb_ref[...])
        vn = vn.astype(BF16)
        tril = _sg_tril()
        dvn = []
        for g in range(SG_GROUPS):
            sl = slice(g * SG_GROUP_DIM, (g + 1) * SG_GROUP_DIM)
            w = jnp.where(tril, w_ref[g], 0.0).astype(BF16)
            vg = vn[:, sl]
            sp = jnp.dot(w, vg, preferred_element_type=F32) + bias_ref[:, sl]
            uv = u_ref[:, sl].astype(F32)
            dsu_g = dsu_ref[:, sl].astype(F32)
            du_ref[:, sl] = (dsu_g * sp * _gelu_grad(uv)).astype(BF16)
            dsp = dsu_g * _gelu(uv)
            dsp_b = dsp.astype(BF16)
            dw = lax.dot_general(dsp_b, vg, (((1,), (1,)), ((), ())), preferred_element_type=F32)
            dw_ref[g] += jnp.where(tril, dw, 0.0)
            dbias_ref[:, sl] += jnp.broadcast_to(jnp.sum(dsp, axis=1, keepdims=True), (SG_CHUNK, SG_GROUP_DIM))
            dvn.append(lax.dot_general(w, dsp_b, (((0,), (0,)), ((), ())), preferred_element_type=F32))
        dvn = jnp.concatenate(dvn, axis=1)
        dlg_ref[...] += jnp.sum(dvn * z, axis=0, keepdims=True)
        dlb_ref[...] += jnp.sum(dvn, axis=0, keepdims=True)
        dz = dvn * lg_ref[...]
        dgv = rstd * (dz - jnp.mean(dz, axis=1, keepdims=True) - z * jnp.mean(dz * z, axis=1, keepdims=True))
        dvs_ref[...] = (dgv * _gelu_grad(vs)).astype(BF16)

    vec = pl.BlockSpec((1, width), lambda i: (0, 0))
    row = pl.BlockSpec((SG_CHUNK, width), lambda i: (i, 0))
    fixed = pl.BlockSpec((SG_CHUNK, width), lambda i: (0, 0))
    w_spec = pl.BlockSpec((SG_GROUPS, SG_CHUNK, SG_CHUNK), lambda i: (0, 0, 0))
    act = jax.ShapeDtypeStruct((t, width), BF16)
    return _pallas_call(
        body,
        out_shape=(act, act, jax.ShapeDtypeStruct((SG_GROUPS, SG_CHUNK, SG_CHUNK), F32),
                   jax.ShapeDtypeStruct((SG_CHUNK, width), F32), jax.ShapeDtypeStruct((1, width), F32),
                   jax.ShapeDtypeStruct((1, width), F32)),
        grid=(t // SG_CHUNK,),
        in_specs=[pl.BlockSpec((SG_CHUNK, width), lambda i: (i, u_blk)), pl.BlockSpec((SG_CHUNK, width), lambda i: (i, vs_blk)),
                  row, vec, vec, w_spec, fixed],
        out_specs=(row, row, w_spec, fixed, vec, vec),
        compiler_params=_cparams(("arbitrary",), 14 * _nbytes((SG_CHUNK, width), F32)),
        name=name)(proj, proj, dsu, lg, lb, sg_w, bias)


def _gate_fwd(proj, ga_blk, gs_blk, y_att, y_sg, name):
    t, d = y_att.shape
    tm, tn = _rows(t, 512), _tile(d, GROUP_WIDTH)

    def body(ga_ref, gs_ref, ya_ref, ys_ref, o_ref):
        o_ref[...] = (_sigmoid(ga_ref[...].astype(F32)) * ya_ref[...].astype(F32)
                      + _sigmoid(gs_ref[...].astype(F32)) * ys_ref[...].astype(F32)).astype(BF16)

    own = pl.BlockSpec((tm, tn), lambda i, j: (i, j))
    return _pallas_call(
        body, out_shape=jax.ShapeDtypeStruct((t, d), BF16), grid=(t // tm, d // tn),
        in_specs=[pl.BlockSpec((tm, tn), lambda i, j: (i, ga_blk + j)), pl.BlockSpec((tm, tn), lambda i, j: (i, gs_blk + j)),
                  own, own],
        out_specs=own, compiler_params=_cparams(("parallel", "parallel"), 6 * _nbytes((tm, tn), F32)),
        name=name)(proj, proj, y_att, y_sg)


def _gate_bwd(proj, ga_blk, gs_blk, y_att, y_sg, dmerged, name):
    t, d = y_att.shape
    tm, tn = _rows(t, 512), _tile(d, GROUP_WIDTH)

    def body(ga_ref, gs_ref, ya_ref, ys_ref, dm_ref, dya_ref, dys_ref, dga_ref, dgs_ref):
        dm = dm_ref[...].astype(F32)
        for g_ref, y_ref, dy_ref, dg_ref in ((ga_ref, ya_ref, dya_ref, dga_ref), (gs_ref, ys_ref, dys_ref, dgs_ref)):
            sg = _sigmoid(g_ref[...].astype(F32))
            dy_ref[...] = (dm * sg).astype(BF16)
            dg_ref[...] = (dm * y_ref[...].astype(F32) * sg * (1.0 - sg)).astype(BF16)

    own = pl.BlockSpec((tm, tn), lambda i, j: (i, j))
    out = jax.ShapeDtypeStruct((t, d), BF16)
    return _pallas_call(
        body, out_shape=(out, out, out, out), grid=(t // tm, d // tn),
        in_specs=[pl.BlockSpec((tm, tn), lambda i, j: (i, ga_blk + j)), pl.BlockSpec((tm, tn), lambda i, j: (i, gs_blk + j)),
                  own, own, own],
        out_specs=(own, own, own, own), compiler_params=_cparams(("parallel", "parallel"), 10 * _nbytes((tm, tn), F32)),
        name=name)(proj, proj, y_att, y_sg, dmerged)


def _mixer_forward(x, wb, small, in_specs, sg_specs, out_specs):
    t, d = x.shape
    att_w = N_GROUPS * GROUP_WIDTH
    sg_w = SG_GROUPS * SG_GROUP_DIM
    n = _rmsnorm_fwd(x, small['mix_norm'], "mix_norm")
    rider, names = _gather(wb, in_specs)
    proj, got = _matmul([(n, _full(wb, 'w_in'))], 'nn', BF16, "mix_in", b3=True, caps=(1024, 1024, 1024), rider=rider)
    _landed(wb, names, got)
    tables = _rope_tables(t)
    qk0, grouped = _rope_fwd(proj, tables, "mix_rope")
    qkv = [(qk0, qk0, proj, (0, 1, 2 * N_GROUPS))] + [g + ((0, 0, 0),) for g in grouped]
    outs, lses = zip(*[_att_fwd(*args, dil, f"att_fwd{gi}") for gi, (args, dil) in enumerate(zip(qkv, DILATIONS))])
    o_b, o_f, lse = _att_combine(outs, lses, "att_combine")
    y_att = _matmul([(o_b, _full(wb, 'w_att_out'))], 'nn', BF16, "mix_att_out", b3=True)
    bias = jnp.repeat(small['sg_b'].T, SG_GROUP_DIM, axis=1)
    u_blk, vs_blk = 3 * att_w // sg_w, 3 * att_w // sg_w + 1
    su = _sg_fwd(proj, u_blk, vs_blk, small['sg_ln_g'], small['sg_ln_b'], small['sg_w'], bias, "sg_fwd")
    rider, names = _gather(wb, sg_specs)
    y_sg, got = _matmul([(su, _full(wb, 'w_sg_out'))], 'nn', BF16, "mix_sg_out", b3=True, rider=rider)
    _landed(wb, names, got)
    ga_blk = (3 * att_w + 2 * sg_w) // _tile(d, GROUP_WIDTH)
    gs_blk = ga_blk + d // _tile(d, GROUP_WIDTH)
    merged = _gate_fwd(proj, ga_blk, gs_blk, y_att, y_sg, "gate_fwd")
    rider, names = _gather(wb, out_specs)
    x_next, got = _matmul([(merged, _full(wb, 'w_out'))], 'nn', F32, "mix_out", residual=x, rider=rider)
    _landed(wb, names, got)
    saved = (n, proj, qkv, tables, o_b, o_f, lse, y_att, su, y_sg, merged, bias, (u_blk, vs_blk, ga_blk, gs_blk))
    return x_next, saved


def _mixer_backward(x, wb, small, saved, dx_next, dx_next_b, c_idx, first_rider, pending):
    n, proj, qkv, tables, o_b, o_f, lse, y_att, su, y_sg, merged, bias, (u_blk, vs_blk, ga_blk, gs_blk) = saved
    s = N_CHIPS
    dmerged, carried = _matmul([(dx_next_b, _full(wb, 'w_out'))], 'nt', BF16, "mix_out_dx", rider=first_rider)
    g_w_out = _matmul([(merged, dx_next_b)], 'tn', BF16, "mix_out_dw", caps=(1024, 1024, 1024))
    dy_att, dy_sg, dg_att, dg_sg = _gate_bwd(proj, ga_blk, gs_blk, y_att, y_sg, dmerged, "gate_bwd")

    g_w_att_out = _matmul([(o_b, dy_att)], 'tn', BF16, "mix_att_out_dw", out3=s)
    do = _matmul([(dy_att, _full(wb, 'w_att_out'))], 'nt', F32, "mix_att_out_dx", b3=True)
    delta = _att_delta(do, o_f, "att_delta")
    stats = [(do, lse, delta)] + _regroup([do, lse, delta], "att_regroup")
    dqkv = []
    for gi, ((q, k, v, offs), st, dil) in enumerate(zip(qkv, stats, DILATIONS)):
        dq = _att_bwd_dq(q, k, v, *st, offs, dil, f"att_bwd_dq{gi}")
        dk, dv = _att_bwd_dkv(q, k, v, *st, offs, dil, f"att_bwd_dkv{gi}")
        dqkv.append((dq, dk, dv))
    dqkv = _rope_bwd(*dqkv[0], dqkv[1:], tables, "mix_rope_bwd")

    g_w_sg_out = _matmul([(su, dy_sg)], 'tn', BF16, "mix_sg_out_dw", out3=s)
    out_names = ['w_out', 'w_att_out', 'w_sg_out']
    out_g4 = [a.reshape(wb[nm].shape) for a, nm in zip([g_w_out, g_w_att_out, g_w_sg_out], out_names)]
    dsu, from_sibling = _matmul([(dy_sg, _full(wb, 'w_sg_out'))], 'nt', BF16, "mix_sg_out_dx", b3=True, rider=_sibling_rider(out_g4))
    out_parts = [_sibling_sum(a, b, c_idx, f"rs_sum1_{nm}") for a, b, nm in zip(out_g4, from_sibling, out_names)]
    out_names, out_parts = out_names + list(pending), out_parts + list(pending.values())
    du, dvs, g_sg_w, g_bias, g_lg, g_lb = _sg_bwd(proj, u_blk, vs_blk, dsu, small['sg_ln_g'], small['sg_ln_b'],
                                                   small['sg_w'], bias, "sg_bwd")
    gs = {'sg_w': g_sg_w, 'sg_b': g_bias[:, ::SG_GROUP_DIM].T, 'sg_ln_g': g_lg, 'sg_ln_b': g_lb}

    dproj = jnp.concatenate([dqkv, du, dvs, dg_att, dg_sg], axis=1)
    g_w_in, out_recv = _matmul([(n, dproj)], 'tn', BF16, "mix_in_dw", out3=s, caps=(1024, 1024, 1024),
                               rider=_scatter_rider(out_parts))
    (p_w_in,) = _reduce_first([g_w_in], ['w_in'], wb, c_idx)
    dn, (r_w_in,) = _matmul([(dproj, _full(wb, 'w_in'))], 'nt', F32, "mix_in_dx", b3=True, caps=(1024, 1024, 512),
                            rider=_scatter_rider([p_w_in]))
    dx, dx_b, gs['mix_norm'] = _rmsnorm_bwd(x, small['mix_norm'], dn, dx_next, "mix_norm_bwd")
    g = {nm: (p, r) for nm, p, r in zip(out_names, out_parts, out_recv)}
    g['w_in'] = (p_w_in, r_w_in)
    return dx, dx_b, g, gs, carried


def _step(x, target, wb, small, c_idx, pc_idx):
    def last_stage(g):
        names = list(g)
        return names, _halves_rider([_chip_sum(*g[n], pc_idx, f"rs_sum2_{n}") for n in names])

    wb = dict(wb)
    rider, names = _gather(wb, ['ffn1_w_gate', 'ffn1_w_up'])
    _landed(wb, names, _exchange(rider, "gather_first"))
    half_in = wb['w_in'].shape[2] // 2
    x1, s1 = _ffn_forward(x, small['ffn1_norm'], wb, "ffn1", ['ffn1_w_down', ('w_in', 0, half_in)], [('w_in', half_in, 2 * half_in)])
    up_rows = wb['ffn2_w_up'].shape[2]
    up_cut = up_rows // 32 * 15
    x2, s2 = _mixer_forward(x1, wb, small, ['w_att_out', 'w_sg_out', 'w_out', 'ffn2_w_gate'],
                            [('ffn2_w_up', 0, up_cut)], [('ffn2_w_up', up_cut, up_rows)])
    x3, s3 = _ffn_forward(x2, small['ffn2_norm'], wb, "ffn2", ['ffn2_w_down'], None)
    loss, dx3, dx3_b, g_final = _final_loss(x3, small['final_norm'], target, "final_loss")
    gs = {'final_norm': g_final}
    whole = {}
    dx2, dx2_b, gs['ffn2_norm'], g, pending, _ = _ffn_backward(x2, small['ffn2_norm'], wb, s3, dx3, dx3_b, c_idx, "ffn2", True)
    names, rider = last_stage(g)
    dx1, dx1_b, g, gs_mix, got = _mixer_backward(x1, wb, small, s2, dx2, dx2_b, c_idx, rider, pending)
    whole.update(zip(names, got))
    gs.update(gs_mix)
    names, rider = last_stage(g)
    dx0, _, gs['ffn1_norm'], g, _, got = _ffn_backward(x, small['ffn1_norm'], wb, s1, dx1, dx1_b, c_idx, "ffn1", False,
                                                       dwd_rider=rider)
    whole.update(zip(names, got))
    names, rider = last_stage(g)
    whole.update(zip(names, _exchange(rider, "rs_halves")))
    return loss, dx0, whole, gs


def _cast_into_gathered(wt, p_idx, name):
    r, ccols = wt.shape[0] // 2, wt.shape[1]
    tm = _rows(r, 256)
    nb = r // tm

    def body(p_ref, w_ref, o_ref):
        o_ref[...] = w_ref[...].astype(BF16)

    grid_spec = pltpu.PrefetchScalarGridSpec(
        num_scalar_prefetch=1, grid=(2, nb),
        in_specs=[pl.BlockSpec((tm, ccols), lambda h, i, pr: (h * nb + i, 0))],
        out_specs=pl.BlockSpec((None, None, tm, ccols), lambda h, i, pr: (pr[0], h, i, 0)))
    return pl.pallas_call(body, out_shape=jax.ShapeDtypeStruct((N_CHIPS, 2, r, ccols), BF16), grid_spec=grid_spec,
                          compiler_params=_cparams(("parallel", "parallel"), 2 * _nbytes((tm, ccols), F32)), name=name)(p_idx, wt)


def _sibling_rider(grads):
    n = len(grads)

    def copy(src, dst, sems, i):
        x, y, c, _ = _place()
        return pltpu.make_async_remote_copy(src[i].at[:, 1 - c], dst[i], sems[0].at[i], sems[1].at[i],
                                            device_id=(x, y, 1 - c), device_id_type=MESH)

    def start(src, dst, sems):
        for i in range(n):
            copy(src, dst, sems, i).start()

    def finish(src, dst, sems):
        for i in range(n):
            copy(src, dst, sems, i).wait()

    return _Rider(grads, [jax.ShapeDtypeStruct((g.shape[0],) + g.shape[2:], g.dtype) for g in grads], {},
                  [pltpu.SemaphoreType.DMA((n,))] * 2, start, finish)


def _merge_riders(a, b):
    n_in, n_out, n_sem = len(a.operands), len(a.out_shapes), len(a.sems)

    def both(which):
        def run(ins, outs, sems):
            getattr(a, which)(ins[:n_in], outs[:n_out], sems[:n_sem])
            getattr(b, which)(ins[n_in:], outs[n_out:], sems[n_sem:])
        return run

    aliases = dict(a.aliases)
    aliases.update({n_in + k: n_out + v for k, v in b.aliases.items()})
    return _Rider(list(a.operands) + list(b.operands), list(a.out_shapes) + list(b.out_shapes), aliases,
                  list(a.sems) + list(b.sems), both('start'), both('finish'))


def _halves_rider(bufs):
    n = len(bufs)

    def copy(ref, sems, i, c, x, y):
        return pltpu.make_async_remote_copy(ref, ref, sems[0].at[i], sems[1].at[i], device_id=(x, y, 1 - c), device_id_type=MESH)

    def start(_, buf, sems):
        x, y, c, _ = _place()
        for i in range(n):
            copy(buf[i].at[c], sems, i, c, x, y).start()

    def finish(_, buf, sems):
        x, y, c, _ = _place()
        for i in range(n):
            copy(buf[i].at[c], sems, i, c, x, y).wait_send()
            copy(buf[i].at[1 - c], sems, i, c, x, y).wait_recv()

    return _Rider(bufs, [jax.ShapeDtypeStruct(b.shape, b.dtype) for b in bufs], {i: i for i in range(n)},
                  [pltpu.SemaphoreType.DMA((n,))] * 2, start, finish)


def _sibling_sum(grad, recv, c_idx, name):
    s, _, r, ccols = grad.shape
    tm = _rows(r, 256)

    def body(c_ref, g_ref, r_ref, o_ref):
        o_ref[...] = (g_ref[...].astype(F32) + r_ref[...].astype(F32)).astype(BF16)

    grid_spec = pltpu.PrefetchScalarGridSpec(
        num_scalar_prefetch=1, grid=(s, r // tm),
        in_specs=[pl.BlockSpec((None, None, tm, ccols), lambda q, i, cr: (q, cr[0], i, 0)),
                  pl.BlockSpec((None, tm, ccols), lambda q, i, cr: (q, i, 0))],
        out_specs=pl.BlockSpec((None, tm, ccols), lambda q, i, cr: (q, i, 0)))
    return pl.pallas_call(body, out_shape=jax.ShapeDtypeStruct((s, r, ccols), BF16), grid_spec=grid_spec,
                          compiler_params=_cparams(("parallel", "parallel"), 4 * _nbytes((tm, ccols), F32)), name=name)(c_idx, grad, recv)


def _chip_sum(part, recv, pc_idx, name):
    _, r, ccols = part.shape
    tm = _rows(r, 256)

    def body(pc_ref, own_ref, r0_ref, r1_ref, r2_ref, o_ref):
        acc = own_ref[...].astype(F32) + r0_ref[...].astype(F32)
        acc = acc + r1_ref[...].astype(F32)
        o_ref[...] = acc + r2_ref[...].astype(F32)

    def slot(j):
        return pl.BlockSpec((None, tm, ccols), lambda i, pc: (j, i, 0))

    grid_spec = pltpu.PrefetchScalarGridSpec(
        num_scalar_prefetch=1, grid=(r // tm,),
        in_specs=[pl.BlockSpec((None, tm, ccols), lambda i, pc: (pc[0], i, 0)), slot(0), slot(1), slot(2)],
        out_specs=pl.BlockSpec((None, tm, ccols), lambda i, pc: (pc[1], i, 0)))
    return pl.pallas_call(body, out_shape=jax.ShapeDtypeStruct((2, r, ccols), F32), grid_spec=grid_spec,
                          compiler_params=_cparams(("parallel",), 6 * _nbytes((tm, ccols), F32)), name=name)(pc_idx, part, recv, recv, recv)


def _all_reduce_small(vec):
    _, r, _ = vec.shape

    def body(v_ref, o_ref, parts, send1, recv1, send2, recv2):
        x, y, c, _ = _place()
        me = 4 * x + 2 * y + c
        peers = []
        for k in range(1, N_DEV):
            px, py, pc = (1 - x if k & 4 else x, 1 - y if k & 2 else y, 1 - c if k & 1 else c)
            peers.append(((px, py, pc), 4 * px + 2 * py + pc))
        parts[me] = v_ref[me]
        cps = []
        for k, (peer, peer_id) in enumerate(peers):
            cp = pltpu.make_async_remote_copy(v_ref.at[peer_id], parts.at[me], send1.at[k], recv1.at[k],
                                              device_id=peer, device_id_type=MESH)
            cp.start()
            cps.append(cp)
        for cp in cps:
            cp.wait()
        acc = parts[0]
        for dev in range(1, N_DEV):
            acc = acc + parts[dev]
        o_ref[me] = acc
        cps = []
        for k, (peer, _) in enumerate(peers):
            cp = pltpu.make_async_remote_copy(o_ref.at[me], o_ref.at[me], send2.at[k], recv2.at[k],
                                              device_id=peer, device_id_type=MESH)
            cp.start()
            cps.append(cp)
        for cp in cps:
            cp.wait()

    vm = pl.BlockSpec(memory_space=pltpu.VMEM)
    sems = pltpu.SemaphoreType.DMA((N_DEV - 1,))
    return pl.pallas_call(
        body, out_shape=jax.ShapeDtypeStruct(vec.shape, F32), in_specs=[vm], out_specs=vm,
        scratch_shapes=[pltpu.VMEM((N_DEV, r, LANES), F32), sems, sems, sems, sems],
        compiler_params=pltpu.CompilerParams(vmem_limit_bytes=int(8 * _nbytes((N_DEV, r, LANES), F32))),
        name="all_reduce_small")(vec)


def _adamw(wt, g, m, v, name, rider=None):
    r, ccols = wt.shape
    tm = _rows(r, max(8, (MIB // (4 * ccols)) // 8 * 8))
    blk = pl.BlockSpec((tm, ccols), lambda i: (i, 0))

    def body(w_ref, g_ref, m_ref, v_ref, go_ref, d_ref, mo_ref, vo_ref):
        gv = g_ref[...]
        go_ref[...] = gv
        mv = ADAM_B1 * m_ref[...] + (1.0 - ADAM_B1) * gv
        vv = ADAM_B2 * v_ref[...] + (1.0 - ADAM_B2) * (gv * gv)
        m_hat = mv / (1.0 - ADAM_B1 ** ADAM_STEP)
        v_hat = vv / (1.0 - ADAM_B2 ** ADAM_STEP)
        d_ref[...] = -ADAM_LR * (m_hat / (jnp.sqrt(v_hat) + ADAM_EPS) + ADAM_WD * w_ref[...])
        mo_ref[...] = mv
        vo_ref[...] = vv

    out = jax.ShapeDtypeStruct((r, ccols), F32)
    return _run(body, name=name, grid=(r // tm,), in_specs=[blk] * 4, out_specs=[blk] * 4, out_shape=[out] * 4, scratch_shapes=[],
                operands=[wt, g, m, v], block_bytes=8 * _nbytes((tm, ccols), F32), rider=rider, pinned=False)


def _as_rows(a):
    rows = a.reshape(-1, LANES)
    return jnp.pad(rows, ((0, -rows.shape[0] % 8), (0, 0)))


def kernel(x, ffn1_norm, ffn1_w_gate, ffn1_w_up, ffn1_w_down, mix_norm, w_in, sg_ln_g, sg_ln_b, sg_w, sg_b, w_att_out, w_sg_out, w_out, ffn2_norm, ffn2_w_gate, ffn2_w_up, ffn2_w_down, final_norm, loss_target, m_ffn1_norm, m_ffn1_w_gate, m_ffn1_w_up, m_ffn1_w_down, m_mix_norm, m_w_in, m_sg_ln_g, m_sg_ln_b, m_sg_w, m_sg_b, m_w_att_out, m_w_sg_out, m_w_out, m_ffn2_norm, m_ffn2_w_gate, m_ffn2_w_up, m_ffn2_w_down, m_final_norm, v_ffn1_norm, v_ffn1_w_gate, v_ffn1_w_up, v_ffn1_w_down, v_mix_norm, v_w_in, v_sg_ln_g, v_sg_ln_b, v_sg_w, v_sg_b, v_w_att_out, v_w_sg_out, v_w_out, v_ffn2_norm, v_ffn2_w_gate, v_ffn2_w_up, v_ffn2_w_down, v_final_norm):
    given = dict(locals())
    wts = {n: given[n] for n in WEIGHT_NAMES}
    ms = {n: given["m_" + n] for n in WEIGHT_NAMES}
    vs = {n: given["v_" + n] for n in WEIGHT_NAMES}
    t, d = x.shape[-2], x.shape[-1]
    xc, yc, cc = lax.axis_index("x"), lax.axis_index("y"), lax.axis_index("c")

    shard2d = {n: wts[n].reshape(wts[n].shape[-2:]) for n in BIG_NAMES}
    p_idx = jnp.reshape(2 * xc + yc, (1,)).astype(jnp.int32)
    c_idx = jnp.reshape(cc, (1,)).astype(jnp.int32)
    pc_idx = jnp.stack([2 * xc + yc, cc]).astype(jnp.int32)
    wb = {n: _cast_into_gathered(shard2d[n], p_idx, f"cast_{n}") for n in BIG_NAMES}

    small = {n: wts[n].reshape(-1, wts[n].shape[-1]) for n in SMALL_NAMES}
    small['sg_w'] = wts['sg_w'].reshape(wts['sg_w'].shape[-3:])
    loss, dx, whole, gs = _step(x.reshape(t, d), loss_target.reshape(t, d), wb, small, c_idx, pc_idx)
    loss = lax.psum(loss[0, 0], ("x", "y", "c"))

    def pack(tree):
        rows = jnp.concatenate([_as_rows(tree[n]) for n in SMALL_NAMES], axis=0)
        return jnp.pad(rows, ((0, -rows.shape[0] % (8 * N_DEV)), (0, 0)))

    packed = pack(gs)
    packed = _all_reduce_small(packed.reshape(N_DEV, -1, LANES)).reshape(packed.shape)

    grads, delta, new_m, new_v = {}, {}, {}, {}
    for n in BIG_NAMES:
        shape, flat = wts[n].shape, shard2d[n].shape
        out = _adamw(shard2d[n], whole[n].reshape(flat), ms[n].reshape(flat), vs[n].reshape(flat), f"adamw_{n}")
        grads[n], delta[n], new_m[n], new_v[n] = (a.reshape(shape) for a in out)

    small_out = _adamw(pack(wts), packed, pack(ms), pack(vs), "adamw_small")
    row = 0
    for n in SMALL_NAMES:
        shape = wts[n].shape
        sz = wts[n].size // LANES
        grads[n], delta[n], new_m[n], new_v[n] = (a[row:row + sz].reshape(shape) for a in small_out)
        row += sz + -sz % 8

    return (loss, dx.reshape(x.shape), *[grads[n] for n in WEIGHT_NAMES], *[delta[n] for n in WEIGHT_NAMES],
            *[new_m[n] for n in WEIGHT_NAMES], *[new_v[n] for n in WEIGHT_NAMES])
```

```python
import functools

import jax
import jax.numpy as jnp
from jax import lax
from jax.experimental import pallas as pl
from jax.experimental.pallas import tpu as pltpu

F32 = jnp.float32
BF16 = jnp.bfloat16
MESH = pl.DeviceIdType.MESH

NORM_EPS = 1e-6
LN_EPS = 1e-5
HEAD_DIM = 128
HEADS_PER_GROUP = 4
GROUP_WIDTH = HEADS_PER_GROUP * HEAD_DIM
DILATIONS = (1, 4, 16)
N_GROUPS = len(DILATIONS)
ATT_BLOCK = 128
ROPE_DIM = HEAD_DIM // 4
ROPE_THETA = 500000.0
SG_CHUNK = 128
SG_GROUPS = 12
SG_GROUP_DIM = 128
MASKED = -1e30

ADAM_LR = 0.001
ADAM_B1 = 0.9
ADAM_B2 = 0.999
ADAM_EPS = 1e-08
ADAM_WD = 0.01
ADAM_STEP = 10

N_CHIPS = 4
N_DEV = 8
LANES = 128
MIB = 2 ** 20
VMEM_BYTES_V7X = 64 * MIB

WEIGHT_NAMES = ['ffn1_norm', 'ffn1_w_gate', 'ffn1_w_up', 'ffn1_w_down', 'mix_norm', 'w_in', 'sg_ln_g', 'sg_ln_b',
                'sg_w', 'sg_b', 'w_att_out', 'w_sg_out', 'w_out', 'ffn2_norm', 'ffn2_w_gate', 'ffn2_w_up',
                'ffn2_w_down', 'final_norm']
BIG = [('ffn1_w_gate', 1), ('ffn1_w_up', 1), ('ffn1_w_down', 0), ('w_in', 1), ('w_att_out', 1), ('w_sg_out', 1),
       ('w_out', 0), ('ffn2_w_gate', 1), ('ffn2_w_up', 1), ('ffn2_w_down', 0)]
BIG_NAMES = [n for n, _ in BIG]
SMALL_NAMES = [n for n in WEIGHT_NAMES if n not in BIG_NAMES]


def _nbytes(shape, dtype):
    n = jnp.dtype(dtype).itemsize
    for s in shape:
        if s is not None:
            n *= s
    return n


def _pallas_call(*args, **kw):
    kw['out_shape'] = jax.tree.map(lambda s: pltpu.HBM(s.shape, s.dtype), kw['out_shape'])
    call = pl.pallas_call(*args, **kw)

    def pinned(*operands):
        return call(*[o if jnp.issubdtype(o.dtype, jnp.integer) else pltpu.with_memory_space_constraint(o, pltpu.HBM)
                      for o in operands])

    return pinned


def _cparams(sem, block_bytes, **kw):
    limit = int(min(max(3 * block_bytes, 32 * MIB), VMEM_BYTES_V7X - 8 * MIB))
    return pltpu.CompilerParams(dimension_semantics=sem, vmem_limit_bytes=limit, **kw)


def _tile(dim, cap):
    best = None
    for t in range(LANES, min(dim, cap) + 1, LANES):
        if dim % t == 0:
            best = t
    if best is None:
        assert dim <= cap, (dim, cap)
        return dim
    return best


def _rows(dim, cap):
    best = None
    for t in range(8, min(dim, cap) + 1, 8):
        if dim % t == 0:
            best = t
    assert best is not None, (dim, cap)
    return best


def _place():
    x, y, c = lax.axis_index("x"), lax.axis_index("y"), lax.axis_index("c")
    others = [(1 - x, y), (x, 1 - y), (1 - x, 1 - y)]
    return x, y, c, others


class _Rider:
    def __init__(self, operands, out_shapes, aliases, sems, start, finish):
        self.operands = operands
        self.out_shapes = out_shapes
        self.aliases = aliases
        self.sems = sems
        self.start = start
        self.finish = finish


def _run(body, *, name, grid, in_specs, out_specs, out_shape, scratch_shapes, operands, block_bytes, rider=None, pinned=True):
    call = _pallas_call if pinned else pl.pallas_call
    if rider is None:
        sem = ("parallel",) * (len(grid) - 1) + ("arbitrary",)
        return call(body, out_shape=out_shape, grid=grid, in_specs=in_specs, out_specs=out_specs,
                    scratch_shapes=scratch_shapes, compiler_params=_cparams(sem, block_bytes), name=name)(*operands)
    n_in, n_out, n_scr = len(operands), len(out_shape), len(scratch_shapes)
    r_in, r_out = len(rider.operands), len(rider.out_shapes)
    any_spec = pl.BlockSpec(memory_space=pl.ANY)

    def wrapped(*refs):
        ins, refs = refs[:n_in], refs[n_in:]
        r_ins, refs = refs[:r_in], refs[r_in:]
        outs, refs = refs[:n_out], refs[n_out:]
        r_outs, refs = refs[:r_out], refs[r_out:]
        scr, sems = refs[:n_scr], refs[n_scr:]
        if not grid:
            rider.start(r_ins, r_outs, sems)
            rider.finish(r_ins, r_outs, sems)
            return
        ids = [pl.program_id(a) for a in range(len(grid))]
        first = functools.reduce(jnp.logical_and, [i == 0 for i in ids])
        last = functools.reduce(jnp.logical_and, [i == g - 1 for i, g in zip(ids, grid)])

        @pl.when(first)
        def _():
            rider.start(r_ins, r_outs, sems)

        body(*ins, *outs, *scr)

        @pl.when(last)
        def _():
            rider.finish(r_ins, r_outs, sems)

    results = call(
        wrapped, out_shape=list(out_shape) + list(rider.out_shapes), grid=grid,
        in_specs=list(in_specs) + [any_spec] * r_in, out_specs=list(out_specs) + [any_spec] * r_out,
        scratch_shapes=list(scratch_shapes) + list(rider.sems),
        input_output_aliases={n_in + k: n_out + v for k, v in rider.aliases.items()},
        compiler_params=_cparams(("arbitrary",) * len(grid) if grid else None, block_bytes, has_side_effects=True),
        name=name)(*operands, *rider.operands)
    return results[:n_out], results[n_out:]


def _exchange(rider, name):
    return _run(None, name=name, grid=(), in_specs=[], out_specs=[], out_shape=[], scratch_shapes=[], operands=[],
                block_bytes=0, rider=rider)[1]


def _gather_rider(items):
    bufs, index = [], []
    for b, r0, r1 in items:
        if not any(b is q for q in bufs):
            bufs.append(b)
        index.append(([k for k, q in enumerate(bufs) if q is b][0], r0, r1))
    n = len(index)

    def piece(refs, k, chip, half):
        bi, r0, r1 = index[k]
        return refs[bi].at[chip, half, pl.ds(r0, r1 - r0)]

    def copy(ref, sem_pair, k, j, to):
        return pltpu.make_async_remote_copy(ref, ref, sem_pair[0].at[k, j], sem_pair[1].at[k, j], device_id=to, device_id_type=MESH)

    def start(r_ins, buf, sems):
        x, y, c, others = _place()
        for k in range(n):
            for j, (ox, oy) in enumerate(others):
                copy(piece(buf, k, 2 * x + y, c), sems[:2], k, j, (ox, oy, c)).start()

    def finish(r_ins, buf, sems):
        x, y, c, others = _place()
        for k in range(n):
            for j, (ox, oy) in enumerate(others):
                got = piece(buf, k, 2 * ox + oy, c)
                copy(got, sems[:2], k, j, (ox, oy, c)).wait_recv()
                copy(got, sems[2:], k, j, (x, y, 1 - c)).start()
        for k in range(n):
            for j, (ox, oy) in enumerate(others):
                copy(piece(buf, k, 2 * ox + oy, 1 - c), sems[2:], k, j, (x, y, 1 - c)).wait_recv()
        for k in range(n):
            for j, (ox, oy) in enumerate(others):
                copy(piece(buf, k, 2 * x + y, c), sems[:2], k, j, (ox, oy, c)).wait_send()
                copy(piece(buf, k, 2 * ox + oy, c), sems[2:], k, j, (x, y, 1 - c)).wait_send()

    return _Rider(bufs, [jax.ShapeDtypeStruct(b.shape, b.dtype) for b in bufs], {i: i for i in range(len(bufs))},
                  [pltpu.SemaphoreType.DMA((n, 3))] * 4, start, finish)


def _scatter_rider(parts):
    n = len(parts)

    def copy(src, dst, sems, i, j, to):
        return pltpu.make_async_remote_copy(src, dst, sems[0].at[i, j], sems[1].at[i, j], device_id=to, device_id_type=MESH)

    def start(src, dst, sems):
        x, y, c, others = _place()
        for i in range(n):
            for j, (ox, oy) in enumerate(others):
                copy(src[i].at[2 * ox + oy], dst[i].at[j], sems, i, j, (ox, oy, c)).start()

    def finish(src, dst, sems):
        x, y, c, others = _place()
        for i in range(n):
            for j, (ox, oy) in enumerate(others):
                copy(src[i].at[2 * ox + oy], dst[i].at[j], sems, i, j, (ox, oy, c)).wait()

    return _Rider(parts, [jax.ShapeDtypeStruct((3,) + p.shape[1:], p.dtype) for p in parts], {},
                  [pltpu.SemaphoreType.DMA((n, 3))] * 2, start, finish)


def _matmul(pairs, mode, out_dtype, name, *, scale=1.0, residual=None, b3=False, out3=0, caps=(1024, 1024, 512), rider=None):
    a0, b0 = pairs[0]
    if mode == 'nn':
        m, k = a0.shape
        n = b0.shape[0] * b0.shape[2] if b3 else b0.shape[1]
    elif mode == 'nt':
        m = a0.shape[0]
        n, k = (b0.shape[1], b0.shape[0] * b0.shape[2]) if b3 else b0.shape
    else:
        k, m = a0.shape
        n = b0.shape[1]
    tm = _tile(m, caps[0])
    tn = _tile(n, caps[1])
    tk = _tile(k, caps[2])
    if b3 and mode == 'nn':
        tn = b0.shape[2]
    if b3 and mode == 'nt':
        tk = b0.shape[2]
    if out3:
        tn = n // out3
    nk = k // tk
    if mode == 'tn':
        a_spec = pl.BlockSpec((tk, tm), lambda i, j, kk: (kk, i))
        b_spec = pl.BlockSpec((tk, tn), lambda i, j, kk: (kk, j))
        dims = ((0,), (0,))
    elif mode == 'nn':
        a_spec = pl.BlockSpec((tm, tk), lambda i, j, kk: (i, kk))
        b_spec = (pl.BlockSpec((None, tk, tn), lambda i, j, kk: (j, kk, 0)) if b3
                  else pl.BlockSpec((tk, tn), lambda i, j, kk: (kk, j)))
        dims = ((1,), (0,))
    else:
        a_spec = pl.BlockSpec((tm, tk), lambda i, j, kk: (i, kk))
        b_spec = (pl.BlockSpec((None, tn, tk), lambda i, j, kk: (kk, j, 0)) if b3
                  else pl.BlockSpec((tn, tk), lambda i, j, kk: (j, kk)))
        dims = ((1,), (1,))
    in_specs, operands = [], []
    for a, b in pairs:
        in_specs += [a_spec, b_spec]
        operands += [a, b]
    block_bytes = len(pairs) * (_nbytes((tm, tk), a0.dtype) + _nbytes((tk, tn), b0.dtype))
    if residual is not None:
        in_specs.append(pl.BlockSpec((tm, tn), lambda i, j, kk: (i, j)))
        operands.append(residual)
        block_bytes += _nbytes((tm, tn), F32)
    if out3:
        out_spec = pl.BlockSpec((None, tm, tn), lambda i, j, kk: (j, i, 0))
        out_shape = jax.ShapeDtypeStruct((out3, m, tn), out_dtype)
    else:
        out_spec = pl.BlockSpec((tm, tn), lambda i, j, kk: (i, j))
        out_shape = jax.ShapeDtypeStruct((m, n), out_dtype)
    block_bytes += _nbytes((tm, tn), out_dtype) + _nbytes((tm, tn), F32)
    n_pairs = len(pairs)
    has_res = residual is not None

    def body(*refs):
        o_ref, acc = refs[-2], refs[-1]
        kk = pl.program_id(2)

        def product():
            part = None
            for p in range(n_pairs):
                d = lax.dot_general(refs[2 * p][...].astype(BF16), refs[2 * p + 1][...].astype(BF16),
                                    (dims, ((), ())), preferred_element_type=F32)
                part = d if part is None else part + d
            return part

        def finish(r):
            if scale != 1.0:
                r = r * scale
            if has_res:
                r = refs[2 * n_pairs][...] + r
            o_ref[...] = r.astype(out_dtype)

        if nk == 1:
            finish(product())
            return

        @pl.when(kk == 0)
        def _():
            acc[...] = product()

        if nk > 2:
            @pl.when(jnp.logical_and(kk > 0, kk < nk - 1))
            def _():
                acc[...] += product()

        @pl.when(kk == nk - 1)
        def _():
            finish(acc[...] + product())

    res = _run(body, name=name, grid=(m // tm, n // tn, nk), in_specs=in_specs, out_specs=[out_spec], out_shape=[out_shape],
               scratch_shapes=[pltpu.VMEM((tm, tn), F32)], operands=operands, block_bytes=block_bytes, rider=rider)
    return res[0] if rider is None else (res[0][0], res[1])


def _rmsnorm_fwd(x, g, name):
    t, d = x.shape
    tm = _rows(t, 512)

    def body(x_ref, g_ref, o_ref):
        xv = x_ref[...]
        r = lax.rsqrt(jnp.mean(xv * xv, axis=1, keepdims=True) + NORM_EPS)
        o_ref[...] = (xv * r * g_ref[...]).astype(BF16)

    row = pl.BlockSpec((tm, d), lambda i: (i, 0))
    return _pallas_call(
        body, out_shape=jax.ShapeDtypeStruct((t, d), BF16), grid=(t // tm,),
        in_specs=[row, pl.BlockSpec((1, d), lambda i: (0, 0))], out_specs=row,
        compiler_params=_cparams(("parallel",), 2 * _nbytes((tm, d), F32)), name=name)(x, g)


def _rms_grad(xv, g, dn, d):
    r = lax.rsqrt(jnp.mean(xv * xv, axis=1, keepdims=True) + NORM_EPS)
    u = dn * g
    s = jnp.sum(xv * u, axis=1, keepdims=True)
    dx = r * u - xv * (r * r * r) * (s * (1.0 / d))
    return dx, dn * xv * r


def _rmsnorm_bwd(x, g, dn, dres, name):
    t, d = x.shape
    tm = _rows(t, 256)

    def body(x_ref, g_ref, dn_ref, dres_ref, dx_ref, dxb_ref, dg_ref):
        dx, dg_rows = _rms_grad(x_ref[...], g_ref[...], dn_ref[...].astype(F32), d)
        dx = dres_ref[...] + dx
        dx_ref[...] = dx
        dxb_ref[...] = dx.astype(BF16)

        @pl.when(pl.program_id(0) == 0)
        def _():
            dg_ref[...] = jnp.zeros_like(dg_ref)

        dg_ref[...] += jnp.sum(dg_rows, axis=0, keepdims=True)

    row = pl.BlockSpec((tm, d), lambda i: (i, 0))
    vec = pl.BlockSpec((1, d), lambda i: (0, 0))
    return _pallas_call(
        body, out_shape=(jax.ShapeDtypeStruct((t, d), F32), jax.ShapeDtypeStruct((t, d), BF16), jax.ShapeDtypeStruct((1, d), F32)),
        grid=(t // tm,), in_specs=[row, vec, row, row], out_specs=(row, row, vec),
        compiler_params=_cparams(("arbitrary",), 5 * _nbytes((tm, d), F32)), name=name)(x, g, dn, dres)


def _final_loss(x, g, target, name):
    t, d = x.shape
    tm = _rows(t, 256)

    def body(x_ref, g_ref, t_ref, loss_ref, dx_ref, dxb_ref, dg_ref):
        xv, gv = x_ref[...], g_ref[...]
        r = lax.rsqrt(jnp.mean(xv * xv, axis=1, keepdims=True) + NORM_EPS)
        err = xv * r * gv - t_ref[...]
        dx, dg_rows = _rms_grad(xv, gv, err * (1.0 / d), d)
        dx_ref[...] = dx
        dxb_ref[...] = dx.astype(BF16)

        @pl.when(pl.program_id(0) == 0)
        def _():
            dg_ref[...] = jnp.zeros_like(dg_ref)
            loss_ref[...] = jnp.zeros_like(loss_ref)

        dg_ref[...] += jnp.sum(dg_rows, axis=0, keepdims=True)
        row_loss = jnp.sum(err * err, axis=1, keepdims=True) * (0.5 / d)
        loss_ref[...] += jnp.sum(row_loss, axis=0, keepdims=True)

    row = pl.BlockSpec((tm, d), lambda i: (i, 0))
    vec = pl.BlockSpec((1, d), lambda i: (0, 0))
    return _pallas_call(
        body, out_shape=(jax.ShapeDtypeStruct((1, 1), F32), jax.ShapeDtypeStruct((t, d), F32),
                         jax.ShapeDtypeStruct((t, d), BF16), jax.ShapeDtypeStruct((1, d), F32)),
        grid=(t // tm,), in_specs=[row, vec, row], out_specs=(pl.BlockSpec((1, 1), lambda i: (0, 0)), row, row, vec),
        compiler_params=_cparams(("arbitrary",), 4 * _nbytes((tm, d), F32)), name=name)(x, g, target)


def _sigmoid(x):
    return 0.5 * jnp.tanh(0.5 * x) + 0.5


def _ffn_up(n, wg, wu, name, rider=None):
    t, d = n.shape
    s, _, f = wg.shape
    tm, tk = _tile(t, 1024), _tile(d, 1024)
    nk = d // tk

    def body(n_ref, wg_ref, wu_ref, a_ref, b_ref, h_ref, acc_g, acc_u):
        kk = pl.program_id(2)

        def products():
            nv = n_ref[...]
            return jnp.dot(nv, wg_ref[...], preferred_element_type=F32), jnp.dot(nv, wu_ref[...], preferred_element_type=F32)

        def finish(a, b):
            a_ref[...] = a.astype(BF16)
            b_ref[...] = b.astype(BF16)
            h_ref[...] = (a * _sigmoid(a) * b).astype(BF16)

        if nk == 1:
            finish(*products())
            return

        @pl.when(kk == 0)
        def _():
            acc_g[...], acc_u[...] = products()

        if nk > 2:
            @pl.when(jnp.logical_and(kk > 0, kk < nk - 1))
            def _():
                pg, pu = products()
                acc_g[...] += pg
                acc_u[...] += pu

        @pl.when(kk == nk - 1)
        def _():
            pg, pu = products()
            finish(acc_g[...] + pg, acc_u[...] + pu)

    w_spec = pl.BlockSpec((None, tk, f), lambda i, j, kk: (j, kk, 0))
    o_spec = pl.BlockSpec((tm, f), lambda i, j, kk: (i, j))
    out = jax.ShapeDtypeStruct((t, s * f), BF16)
    block_bytes = _nbytes((tm, tk), BF16) + 2 * _nbytes((tk, f), BF16) + 3 * _nbytes((tm, f), BF16) + 2 * _nbytes((tm, f), F32)
    return _run(body, name=name, grid=(t // tm, s, nk),
                in_specs=[pl.BlockSpec((tm, tk), lambda i, j, kk: (i, kk)), w_spec, w_spec], out_specs=[o_spec, o_spec, o_spec],
                out_shape=[out, out, out], scratch_shapes=[pltpu.VMEM((tm, f), F32), pltpu.VMEM((tm, f), F32)],
                operands=[n, wg, wu], block_bytes=block_bytes, rider=rider)


def _ffn_bwd_act(dx, wd, a, b, name):
    t, d = dx.shape
    f = wd.shape[0]
    tm, tn, tk = _tile(t, 1024), _tile(f, 1536), _tile(d, 1024)
    nk = d // tk

    def body(dx_ref, wd_ref, a_ref, b_ref, da_ref, db_ref, acc):
        kk = pl.program_id(2)

        def product():
            return lax.dot_general(dx_ref[...], wd_ref[...], (((1,), (1,)), ((), ())), preferred_element_type=F32)

        def finish(r):
            dh = 0.5 * r
            av, bv = a_ref[...].astype(F32), b_ref[...].astype(F32)
            sg = _sigmoid(av)
            da_ref[...] = (dh * bv * (sg * (1.0 + av * (1.0 - sg)))).astype(BF16)
            db_ref[...] = (dh * (av * sg)).astype(BF16)

        if nk == 1:
            finish(product())
            return

        @pl.when(kk == 0)
        def _():
            acc[...] = product()

        if nk > 2:
            @pl.when(jnp.logical_and(kk > 0, kk < nk - 1))
            def _():
                acc[...] += product()

        @pl.when(kk == nk - 1)
        def _():
            finish(acc[...] + product())

    act = pl.BlockSpec((tm, tn), lambda i, j, kk: (i, j))
    out = jax.ShapeDtypeStruct((t, f), BF16)
    block_bytes = _nbytes((tm, tk), BF16) + _nbytes((tn, tk), BF16) + 4 * _nbytes((tm, tn), BF16) + _nbytes((tm, tn), F32)
    return _pallas_call(
        body, out_shape=(out, out), grid=(t // tm, f // tn, nk),
        in_specs=[pl.BlockSpec((tm, tk), lambda i, j, kk: (i, kk)), pl.BlockSpec((tn, tk), lambda i, j, kk: (j, kk)),
                  act, act],
        out_specs=(act, act), scratch_shapes=[pltpu.VMEM((tm, tn), F32)],
        compiler_params=_cparams(("parallel", "parallel", "arbitrary"), block_bytes), name=name)(dx, wd, a, b)


AXIS = dict(BIG)


def _full(wb, n):
    _, _, r, ccols = wb[n].shape
    return wb[n].reshape(N_CHIPS, 2 * r, ccols) if AXIS[n] == 1 else wb[n].reshape(N_CHIPS * 2 * r, ccols)


def _gather(wb, specs):
    items, names = [], []
    for s in specs:
        n, r0, r1 = (s, 0, wb[s].shape[2]) if isinstance(s, str) else s
        items.append((wb[n], r0, r1))
        if n not in names:
            names.append(n)
    return _gather_rider(items), names


def _landed(wb, names, results):
    for n, r in zip(names, results):
        wb[n] = r


def _reduce_first(grads, names, wb, c_idx):
    g4 = [g.reshape(wb[n].shape) for g, n in zip(grads, names)]
    from_sibling = _exchange(_sibling_rider(g4), "rs_sibling_" + names[0])
    return [_sibling_sum(a, b, c_idx, f"rs_sum1_{n}") for a, b, n in zip(g4, from_sibling, names)]


def _ffn_forward(x, gain, wb, tag, up_specs, down_specs):
    n = _rmsnorm_fwd(x, gain, f"{tag}_norm")
    rider, names = _gather(wb, up_specs)
    (a, b, h), got = _ffn_up(n, _full(wb, f"{tag}_w_gate"), _full(wb, f"{tag}_w_up"), f"{tag}_up", rider=rider)
    _landed(wb, names, got)
    down = dict(scale=0.5, residual=x, caps=(1024, 1024, 1536))
    if down_specs:
        rider, names = _gather(wb, down_specs)
        x_next, got = _matmul([(h, _full(wb, f"{tag}_w_down"))], 'nn', F32, f"{tag}_down", rider=rider, **down)
        _landed(wb, names, got)
    else:
        x_next = _matmul([(h, _full(wb, f"{tag}_w_down"))], 'nn', F32, f"{tag}_down", **down)
    return x_next, (n, a, b, h)


def _ffn_backward(x, gain, wb, saved, dx_next, dx_next_b, c_idx, tag, chained, dwd_rider=None):
    n, a, b, h = saved
    wg, wu, wd = (f"{tag}_w_gate", f"{tag}_w_up", f"{tag}_w_down")
    da, db = _ffn_bwd_act(dx_next_b, _full(wb, wd), a, b, f"{tag}_bwd_act")
    res = _matmul([(h, dx_next_b)], 'tn', BF16, f"{tag}_dwd", scale=0.5, caps=(1536, 2048, 1024), rider=dwd_rider)
    g_wd, carried = (res, ()) if dwd_rider is None else res
    grad_mm = dict(out3=N_CHIPS, caps=(2048, 1024, 1024))
    dn_pairs = [(da, _full(wb, wg)), (db, _full(wb, wu))]
    if not chained:
        (p_wd,) = _reduce_first([g_wd], [wd], wb, c_idx)
        g_wg, (r_wd,) = _matmul([(n, da)], 'tn', BF16, f"{tag}_dwg", rider=_scatter_rider([p_wd]), **grad_mm)
        g_wu = _matmul([(n, db)], 'tn', BF16, f"{tag}_dwu", **grad_mm)
        p_wg, p_wu = _reduce_first([g_wg, g_wu], [wg, wu], wb, c_idx)
        dn, (r_wg, r_wu) = _matmul(dn_pairs, 'nt', BF16, f"{tag}_dn", b3=True, rider=_scatter_rider([p_wg, p_wu]))
        done, pending = {wg: (p_wg, r_wg), wu: (p_wu, r_wu), wd: (p_wd, r_wd)}, {}
    else:
        g_wd = g_wd.reshape(wb[wd].shape)
        g_wg, (s_wd,) = _matmul([(n, da)], 'tn', BF16, f"{tag}_dwg", rider=_sibling_rider([g_wd]), **grad_mm)
        p_wd = _sibling_sum(g_wd, s_wd, c_idx, f"rs_sum1_{wd}")
        g_wg = g_wg.reshape(wb[wg].shape)
        g_wu, (r_wd, s_wg) = _matmul([(n, db)], 'tn', BF16, f"{tag}_dwu",
                                     rider=_merge_riders(_scatter_rider([p_wd]), _sibling_rider([g_wg])), **grad_mm)
        p_wg = _sibling_sum(g_wg, s_wg, c_idx, f"rs_sum1_{wg}")
        g_wu = g_wu.reshape(wb[wu].shape)
        dn, (r_wg, s_wu) = _matmul(dn_pairs, 'nt', BF16, f"{tag}_dn", b3=True,
                                   rider=_merge_riders(_scatter_rider([p_wg]), _sibling_rider([g_wu])))
        p_wu = _sibling_sum(g_wu, s_wu, c_idx, f"rs_sum1_{wu}")
        done, pending = {wg: (p_wg, r_wg), wd: (p_wd, r_wd)}, {wu: p_wu}
    dx, dx_b, g_gain = _rmsnorm_bwd(x, gain, dn, dx_next, f"{tag}_norm_bwd")
    return dx, dx_b, g_gain, done, pending, carried


def _rope_tables(seq):
    half = ROPE_DIM // 2
    inv_freq = ROPE_THETA ** (-jnp.arange(0, ROPE_DIM, 2, dtype=F32) / ROPE_DIM)
    ang = jnp.arange(seq).astype(F32)[:, None] * inv_freq[None, :]
    cos, sin = jnp.cos(ang), jnp.sin(ang)
    zeros = lambda w: jnp.zeros((seq, w), F32)
    c = jnp.concatenate([cos, cos, jnp.ones((seq, HEAD_DIM - ROPE_DIM), F32)], axis=1)
    s_up = jnp.concatenate([-sin, zeros(HEAD_DIM - half)], axis=1)
    s_dn = jnp.concatenate([zeros(half), sin, zeros(HEAD_DIM - ROPE_DIM)], axis=1)
    return c, s_up, s_dn


def _rotate(xv, cv, uv, dv):
    half = ROPE_DIM // 2
    return xv * cv + pltpu.roll(xv, HEAD_DIM - half, 1) * uv + pltpu.roll(xv, half, 1) * dv


def _stage(tm):
    return pltpu.VMEM((HEADS_PER_GROUP, tm, HEAD_DIM), F32)


def _to_groups(stage, o_ref, dil):
    rows = stage.shape[1] // dil
    for r in range(dil):
        for h in range(HEADS_PER_GROUP):
            col = r * GROUP_WIDTH + h * HEAD_DIM
            o_ref[:, col:col + HEAD_DIM] = stage[h, pl.ds(r, rows, stride=dil), :].astype(o_ref.dtype)


def _from_groups(g_ref, stage, dil):
    rows = stage.shape[1] // dil
    for r in range(dil):
        for h in range(HEADS_PER_GROUP):
            col = r * GROUP_WIDTH + h * HEAD_DIM
            stage[h, pl.ds(r, rows, stride=dil), :] = g_ref[:, col:col + HEAD_DIM].astype(F32)


def _group_spec(tm, dil):
    return pl.BlockSpec((tm // dil, dil * GROUP_WIDTH), lambda i: (i, 0))


def _group_shape(t, dil, dtype):
    return jax.ShapeDtypeStruct((t // dil, dil * GROUP_WIDTH), dtype)


def _rope_fwd(proj, tables, name):
    t = proj.shape[0]
    tm = _rows(t, 512)
    att_w = N_GROUPS * GROUP_WIDTH
    dilated = [(gi, dil) for gi, dil in enumerate(DILATIONS) if dil > 1]

    def body(x_ref, c_ref, up_ref, dn_ref, qk0_ref, *rest):
        outs, stage = rest[:-1], rest[-1]
        cv, uv, dv = c_ref[...], up_ref[...], dn_ref[...]
        for part in range(2):
            for gi, dil in enumerate(DILATIONS):
                for h in range(HEADS_PER_GROUP):
                    col = part * att_w + gi * GROUP_WIDTH + h * HEAD_DIM
                    y = _rotate(x_ref[:, col:col + HEAD_DIM].astype(F32), cv, uv, dv)
                    if dil == 1:
                        qk0_ref[:, part * GROUP_WIDTH + h * HEAD_DIM:part * GROUP_WIDTH + (h + 1) * HEAD_DIM] = y.astype(BF16)
                    else:
                        stage[h] = y
                if dil > 1:
                    _to_groups(stage, outs[3 * dilated.index((gi, dil)) + part], dil)
        for n, (gi, dil) in enumerate(dilated):
            col = 2 * att_w + gi * GROUP_WIDTH
            for h in range(HEADS_PER_GROUP):
                stage[h] = x_ref[:, col + h * HEAD_DIM:col + (h + 1) * HEAD_DIM].astype(F32)
            _to_groups(stage, outs[3 * n + 2], dil)

    tab = pl.BlockSpec((tm, HEAD_DIM), lambda i: (i, 0))
    out_shape = [jax.ShapeDtypeStruct((t, 2 * GROUP_WIDTH), BF16)]
    out_specs = [pl.BlockSpec((tm, 2 * GROUP_WIDTH), lambda i: (i, 0))]
    for _, dil in dilated:
        out_shape += [_group_shape(t, dil, BF16)] * 3
        out_specs += [_group_spec(tm, dil)] * 3
    res = _pallas_call(
        body, out_shape=out_shape, grid=(t // tm,),
        in_specs=[pl.BlockSpec((tm, 3 * att_w), lambda i: (i, 0)), tab, tab, tab], out_specs=out_specs,
        scratch_shapes=[_stage(tm)],
        compiler_params=_cparams(("parallel",), 4 * _nbytes((tm, 3 * att_w), BF16)), name=name)(proj, *tables)
    return res[0], [tuple(res[1 + 3 * n:4 + 3 * n]) for n in range(len(dilated))]


def _rope_bwd(dq0, dk0, dv0, grouped, tables, name):
    t = dq0.shape[0]
    tm = _rows(t, 512)
    att_w = N_GROUPS * GROUP_WIDTH
    dilated = [(gi, dil) for gi, dil in enumerate(DILATIONS) if dil > 1]
    c, s_up, s_dn = tables

    def body(c_ref, up_ref, dn_ref, dq0_ref, dk0_ref, dv0_ref, *rest):
        g_refs, o_ref, stage = rest[:-2], rest[-2], rest[-1]
        cv, uv, dv = c_ref[...], -up_ref[...], -dn_ref[...]
        for part, first in enumerate((dq0_ref, dk0_ref)):
            for gi, dil in enumerate(DILATIONS):
                if dil > 1:
                    _from_groups(g_refs[3 * dilated.index((gi, dil)) + part], stage, dil)
                for h in range(HEADS_PER_GROUP):
                    sl = slice(h * HEAD_DIM, (h + 1) * HEAD_DIM)
                    xv = first[:, sl].astype(F32) if dil == 1 else stage[h]
                    col = part * att_w + gi * GROUP_WIDTH + h * HEAD_DIM
                    o_ref[:, col:col + HEAD_DIM] = _rotate(xv, cv, uv, dv).astype(BF16)
        for gi, dil in enumerate(DILATIONS):
            col = 2 * att_w + gi * GROUP_WIDTH
            if dil == 1:
                o_ref[:, col:col + GROUP_WIDTH] = dv0_ref[...]
            else:
                _from_groups(g_refs[3 * dilated.index((gi, dil)) + 2], stage, dil)
                for h in range(HEADS_PER_GROUP):
                    o_ref[:, col + h * HEAD_DIM:col + (h + 1) * HEAD_DIM] = stage[h].astype(BF16)

    tab = pl.BlockSpec((tm, HEAD_DIM), lambda i: (i, 0))
    nat = pl.BlockSpec((tm, GROUP_WIDTH), lambda i: (i, 0))
    in_specs, operands = [tab, tab, tab, nat, nat, nat], [c, s_up, s_dn, dq0, dk0, dv0]
    for (_, dil), arrs in zip(dilated, grouped):
        in_specs += [_group_spec(tm, dil)] * 3
        operands += list(arrs)
    return _pallas_call(
        body, out_shape=jax.ShapeDtypeStruct((t, 3 * att_w), BF16), grid=(t // tm,), in_specs=in_specs,
        out_specs=pl.BlockSpec((tm, 3 * att_w), lambda i: (i, 0)), scratch_shapes=[_stage(tm)],
        compiler_params=_cparams(("parallel",), 4 * _nbytes((tm, 3 * att_w), BF16)), name=name)(*operands)


def _regroup(arrs, name):
    t = arrs[0].shape[0]
    tm = _rows(t, 512)
    dilated = [dil for dil in DILATIONS if dil > 1]
    n_in = len(arrs)

    def body(*refs):
        ins, outs, stage = refs[:n_in], refs[n_in:-1], refs[-1]
        for j, x_ref in enumerate(ins):
            for h in range(HEADS_PER_GROUP):
                stage[h] = x_ref[:, h * HEAD_DIM:(h + 1) * HEAD_DIM]
            for n, dil in enumerate(dilated):
                _to_groups(stage, outs[n * n_in + j], dil)

    nat = pl.BlockSpec((tm, GROUP_WIDTH), lambda i: (i, 0))
    res = _pallas_call(
        body, out_shape=[_group_shape(t, dil, F32) for dil in dilated for _ in arrs], grid=(t // tm,),
        in_specs=[nat] * n_in, out_specs=[_group_spec(tm, dil) for dil in dilated for _ in arrs], scratch_shapes=[_stage(tm)],
        compiler_params=_cparams(("parallel",), 3 * n_in * _nbytes((tm, GROUP_WIDTH), F32)), name=name)(*arrs)
    return [tuple(res[n * n_in:(n + 1) * n_in]) for n in range(len(dilated))]


def _query_mask(has_prev):
    qi = lax.broadcasted_iota(jnp.int32, (ATT_BLOCK, 2 * ATT_BLOCK), 0)
    col = lax.broadcasted_iota(jnp.int32, (ATT_BLOCK, 2 * ATT_BLOCK), 1)
    prev = jnp.logical_and(jnp.logical_and(col < ATT_BLOCK, col >= qi), has_prev)
    return jnp.logical_or(prev, jnp.logical_and(col >= ATT_BLOCK, col - ATT_BLOCK <= qi))


def _key_mask(has_next):
    row = lax.broadcasted_iota(jnp.int32, (2 * ATT_BLOCK, ATT_BLOCK), 0)
    kj = lax.broadcasted_iota(jnp.int32, (2 * ATT_BLOCK, ATT_BLOCK), 1)
    nxt = jnp.logical_and(jnp.logical_and(row >= ATT_BLOCK, kj >= row - ATT_BLOCK), has_next)
    return jnp.logical_or(nxt, jnp.logical_and(row < ATT_BLOCK, kj <= row))


def _scores(q, k):
    return lax.dot_general(q, k, (((1,), (1,)), ((), ())), preferred_element_type=F32) * (HEAD_DIM ** -0.5)


def _att_fwd(q, k, v, offs, dil, name):
    qo, ko, vo = offs
    length = q.shape[0]
    nb = length // ATT_BLOCK

    def body(q_ref, kp_ref, kc_ref, vp_ref, vc_ref, o_ref, lse_ref):
        mask = _query_mask(pl.program_id(1) > 0)
        heads = [slice(h * HEAD_DIM, (h + 1) * HEAD_DIM) for h in range(HEADS_PER_GROUP)]
        ks = [jnp.concatenate([kp_ref[:, sl], kc_ref[:, sl]], axis=0) for sl in heads]
        vs = [jnp.concatenate([vp_ref[:, sl], vc_ref[:, sl]], axis=0) for sl in heads]
        ss = [jnp.where(mask, _scores(q_ref[:, sl], kv), MASKED) for sl, kv in zip(heads, ks)]
        ms = [jnp.max(s, axis=1, keepdims=True) for s in ss]
        ps = [jnp.exp(s - m) for s, m in zip(ss, ms)]
        ls = [jnp.sum(p, axis=1, keepdims=True) for p in ps]
        accs = [jnp.dot(p.astype(BF16), vv, preferred_element_type=F32) for p, vv in zip(ps, vs)]
        for sl, acc, m, l in zip(heads, accs, ms, ls):
            o_ref[:, sl] = acc / l
            lse_ref[:, sl] = jnp.broadcast_to(m + jnp.log(l), (ATT_BLOCK, HEAD_DIM))

    def spec(off, prev):
        if prev:
            return pl.BlockSpec((ATT_BLOCK, GROUP_WIDTH), lambda r, n: (jnp.maximum(n - 1, 0), off + r))
        return pl.BlockSpec((ATT_BLOCK, GROUP_WIDTH), lambda r, n: (n, off + r))

    out = jax.ShapeDtypeStruct((length, dil * GROUP_WIDTH), F32)
    o_spec = pl.BlockSpec((ATT_BLOCK, GROUP_WIDTH), lambda r, n: (n, r))
    return _pallas_call(
        body, out_shape=(out, out), grid=(dil, nb),
        in_specs=[spec(qo, False), spec(ko, True), spec(ko, False), spec(vo, True), spec(vo, False)],
        out_specs=(o_spec, o_spec),
        compiler_params=_cparams(("parallel", "parallel"), 8 * _nbytes((ATT_BLOCK, GROUP_WIDTH), F32)), name=name)(q, k, k, v, v)


def _att_combine(outs, lses, name):
    t = outs[0].shape[0] * DILATIONS[0]
    tm = _rows(t, 512)

    def body(*refs):
        o_refs, l_refs = refs[:N_GROUPS], refs[N_GROUPS:2 * N_GROUPS]
        ob_ref, of_ref, lse_ref = refs[2 * N_GROUPS:2 * N_GROUPS + 3]
        stages = list(refs[2 * N_GROUPS + 3:])
        staged = []
        for o_ref, l_ref, dil in zip(o_refs, l_refs, DILATIONS):
            if dil > 1:
                so, sl = stages.pop(), stages.pop()
                _from_groups(o_ref, so, dil)
                _from_groups(l_ref, sl, dil)
                staged.append((so, sl))
            else:
                staged.append(None)
        for h in range(HEADS_PER_GROUP):
            hs = slice(h * HEAD_DIM, (h + 1) * HEAD_DIM)
            os_ = [o_ref[:, hs] if st is None else st[0][h] for o_ref, st in zip(o_refs, staged)]
            ls = [l_ref[:, hs] if st is None else st[1][h] for l_ref, st in zip(l_refs, staged)]
            m = functools.reduce(jnp.maximum, ls)
            ws = [jnp.exp(l - m) for l in ls]
            den = functools.reduce(jnp.add, ws)
            num = functools.reduce(jnp.add, [w * o for w, o in zip(ws, os_)])
            o = num / den
            ob_ref[:, hs] = o.astype(BF16)
            of_ref[:, hs] = o
            lse_ref[:, hs] = m + jnp.log(den)

    blk = pl.BlockSpec((tm, GROUP_WIDTH), lambda i: (i, 0))
    specs = [blk if dil == 1 else _group_spec(tm, dil) for dil in DILATIONS]
    f32 = jax.ShapeDtypeStruct((t, GROUP_WIDTH), F32)
    n_stage = 2 * sum(dil > 1 for dil in DILATIONS)
    return _pallas_call(
        body, out_shape=(jax.ShapeDtypeStruct((t, GROUP_WIDTH), BF16), f32, f32), grid=(t // tm,),
        in_specs=specs * 2, out_specs=(blk, blk, blk), scratch_shapes=[_stage(tm)] * n_stage,
        compiler_params=_cparams(("parallel",), 13 * _nbytes((tm, GROUP_WIDTH), F32)), name=name)(*outs, *lses)


def _att_delta(do, o, name):
    t = o.shape[0]
    tm = _rows(t, 512)

    def body(do_ref, o_ref, d_ref):
        for h in range(HEADS_PER_GROUP):
            sl = slice(h * HEAD_DIM, (h + 1) * HEAD_DIM)
            s = jnp.sum(do_ref[:, sl] * o_ref[:, sl], axis=1, keepdims=True)
            d_ref[:, sl] = jnp.broadcast_to(s, (tm, HEAD_DIM))

    blk = pl.BlockSpec((tm, GROUP_WIDTH), lambda i: (i, 0))
    return _pallas_call(
        body, out_shape=jax.ShapeDtypeStruct((t, GROUP_WIDTH), F32), grid=(t // tm,), in_specs=[blk, blk], out_specs=blk,
        compiler_params=_cparams(("parallel",), 3 * _nbytes((tm, GROUP_WIDTH), F32)), name=name)(do, o)


def _att_bwd_dq(q, k, v, do, lse, delta, offs, dil, name):
    qo, ko, vo = offs
    length = q.shape[0]
    nb = length // ATT_BLOCK
    scale = HEAD_DIM ** -0.5

    def body(q_ref, kp_ref, kc_ref, vp_ref, vc_ref, do_ref, lse_ref, dl_ref, dq_ref):
        mask = _query_mask(pl.program_id(1) > 0)
        heads = [slice(h * HEAD_DIM, (h + 1) * HEAD_DIM) for h in range(HEADS_PER_GROUP)]
        wide = lambda ref, sl: jnp.concatenate([ref[:, sl], ref[:, sl]], axis=1)
        ks = [jnp.concatenate([kp_ref[:, sl], kc_ref[:, sl]], axis=0) for sl in heads]
        vs = [jnp.concatenate([vp_ref[:, sl], vc_ref[:, sl]], axis=0) for sl in heads]
        ps = [jnp.exp(jnp.where(mask, _scores(q_ref[:, sl], kv), MASKED) - wide(lse_ref, sl)) for sl, kv in zip(heads, ks)]
        dps = [lax.dot_general(do_ref[:, sl].astype(BF16), vv, (((1,), (1,)), ((), ())), preferred_element_type=F32)
               for sl, vv in zip(heads, vs)]
        dss = [(p * (dp - wide(dl_ref, sl)) * scale).astype(BF16) for sl, p, dp in zip(heads, ps, dps)]
        dqs = [jnp.dot(ds, kv, preferred_element_type=F32) for ds, kv in zip(dss, ks)]
        for sl, dq in zip(heads, dqs):
            dq_ref[:, sl] = dq.astype(BF16)

    def spec(off, prev):
        if prev:
            return pl.BlockSpec((ATT_BLOCK, GROUP_WIDTH), lambda r, n: (jnp.maximum(n - 1, 0), off + r))
        return pl.BlockSpec((ATT_BLOCK, GROUP_WIDTH), lambda r, n: (n, off + r))

    own = pl.BlockSpec((ATT_BLOCK, GROUP_WIDTH), lambda r, n: (n, r))
    return _pallas_call(
        body, out_shape=jax.ShapeDtypeStruct((length, dil * GROUP_WIDTH), BF16), grid=(dil, nb),
        in_specs=[spec(qo, False), spec(ko, True), spec(ko, False), spec(vo, True), spec(vo, False), own, own, own],
        out_specs=own,
        compiler_params=_cparams(("parallel", "parallel"), 10 * _nbytes((ATT_BLOCK, GROUP_WIDTH), F32)),
        name=name)(q, k, k, v, v, do, lse, delta)


def _att_bwd_dkv(q, k, v, do, lse, delta, offs, dil, name):
    qo, ko, vo = offs
    length = q.shape[0]
    nb = length // ATT_BLOCK
    scale = HEAD_DIM ** -0.5

    def body(k_ref, v_ref, qc_ref, qn_ref, doc_ref, don_ref, lsec_ref, lsen_ref, dlc_ref, dln_ref, dk_ref, dv_ref):
        mask = _key_mask(pl.program_id(1) < nb - 1)
        heads = [slice(h * HEAD_DIM, (h + 1) * HEAD_DIM) for h in range(HEADS_PER_GROUP)]
        both = lambda cur, nxt, sl: jnp.concatenate([cur[:, sl], nxt[:, sl]], axis=0)
        qs = [both(qc_ref, qn_ref, sl) for sl in heads]
        dos = [both(doc_ref, don_ref, sl).astype(BF16) for sl in heads]
        ps = [jnp.exp(jnp.where(mask, _scores(qv, k_ref[:, sl]), MASKED) - both(lsec_ref, lsen_ref, sl)) for sl, qv in zip(heads, qs)]
        dps = [lax.dot_general(dov, v_ref[:, sl], (((1,), (1,)), ((), ())), preferred_element_type=F32) for sl, dov in zip(heads, dos)]
        dss = [(p * (dp - both(dlc_ref, dln_ref, sl)) * scale).astype(BF16) for sl, p, dp in zip(heads, ps, dps)]
        dvs = [lax.dot_general(p.astype(BF16), dov, (((0,), (0,)), ((), ())), preferred_element_type=F32) for p, dov in zip(ps, dos)]
        dks = [lax.dot_general(ds, qv, (((0,), (0,)), ((), ())), preferred_element_type=F32) for ds, qv in zip(dss, qs)]
        for sl, dk, dv in zip(heads, dks, dvs):
            dk_ref[:, sl] = dk.astype(BF16)
            dv_ref[:, sl] = dv.astype(BF16)

    def spec(off, nxt):
        if nxt:
            return pl.BlockSpec((ATT_BLOCK, GROUP_WIDTH), lambda r, n: (jnp.minimum(n + 1, nb - 1), off + r))
        return pl.BlockSpec((ATT_BLOCK, GROUP_WIDTH), lambda r, n: (n, off + r))

    own = pl.BlockSpec((ATT_BLOCK, GROUP_WIDTH), lambda r, n: (n, r))
    out = jax.ShapeDtypeStruct((length, dil * GROUP_WIDTH), BF16)
    return _pallas_call(
        body, out_shape=(out, out), grid=(dil, nb),
        in_specs=[spec(ko, False), spec(vo, False), spec(qo, False), spec(qo, True), spec(0, False), spec(0, True),
                  spec(0, False), spec(0, True), spec(0, False), spec(0, True)],
        out_specs=(own, own),
        compiler_params=_cparams(("parallel", "parallel"), 12 * _nbytes((ATT_BLOCK, GROUP_WIDTH), F32)),
        name=name)(k, v, q, q, do, do, lse, lse, delta, delta)


def _gelu(x):
    return 0.5 * x * (1.0 + lax.erf(x * (2.0 ** -0.5)))


def _gelu_grad(x):
    return 0.5 * (1.0 + lax.erf(x * (2.0 ** -0.5))) + x * jnp.exp(-0.5 * x * x) * ((2.0 * jnp.pi) ** -0.5)


def _sg_normed(vs, lg, lb):
    gv = _gelu(vs)
    mu = jnp.mean(gv, axis=1, keepdims=True)
    xc = gv - mu
    rstd = lax.rsqrt(jnp.mean(xc * xc, axis=1, keepdims=True) + LN_EPS)
    z = xc * rstd
    return z, rstd, z * lg + lb


def _sg_tril():
    row = lax.broadcasted_iota(jnp.int32, (SG_CHUNK, SG_CHUNK), 0)
    col = lax.broadcasted_iota(jnp.int32, (SG_CHUNK, SG_CHUNK), 1)
    return row >= col


def _sg_fwd(proj, u_blk, vs_blk, lg, lb, sg_w, bias, name):
    t = proj.shape[0]
    width = SG_GROUPS * SG_GROUP_DIM

    def body(u_ref, vs_ref, lg_ref, lb_ref, w_ref, bias_ref, o_ref):
        _, _, vn = _sg_normed(vs_ref[...].astype(F32), lg_ref[...], lb_ref[...])
        vn = vn.astype(BF16)
        tril = _sg_tril()
        for g in range(SG_GROUPS):
            sl = slice(g * SG_GROUP_DIM, (g + 1) * SG_GROUP_DIM)
            w = jnp.where(tril, w_ref[g], 0.0).astype(BF16)
            sp = jnp.dot(w, vn[:, sl], preferred_element_type=F32) + bias_ref[:, sl]
            o_ref[:, sl] = (_gelu(u_ref[:, sl].astype(F32)) * sp).astype(BF16)

    vec = pl.BlockSpec((1, width), lambda i: (0, 0))
    return _pallas_call(
        body, out_shape=jax.ShapeDtypeStruct((t, width), BF16), grid=(t // SG_CHUNK,),
        in_specs=[pl.BlockSpec((SG_CHUNK, width), lambda i: (i, u_blk)), pl.BlockSpec((SG_CHUNK, width), lambda i: (i, vs_blk)),
                  vec, vec, pl.BlockSpec((SG_GROUPS, SG_CHUNK, SG_CHUNK), lambda i: (0, 0, 0)),
                  pl.BlockSpec((SG_CHUNK, width), lambda i: (0, 0))],
        out_specs=pl.BlockSpec((SG_CHUNK, width), lambda i: (i, 0)),
        compiler_params=_cparams(("parallel",), 8 * _nbytes((SG_CHUNK, width), F32)), name=name)(proj, proj, lg, lb, sg_w, bias)


def _sg_bwd(proj, u_blk, vs_blk, dsu, lg, lb, sg_w, bias, name):
    t = proj.shape[0]
    width = SG_GROUPS * SG_GROUP_DIM

    def body(u_ref, vs_ref, dsu_ref, lg_ref, lb_ref, w_ref, bias_ref, du_ref, dvs_ref, dw_ref, dbias_ref, dlg_ref, dlb_ref):
        @pl.when(pl.program_id(0) == 0)
        def _():
            dw_ref[...] = jnp.zeros_like(dw_ref)
            dbias_ref[...] = jnp.zeros_like(dbias_ref)
            dlg_ref[...] = jnp.zeros_like(dlg_ref)
            dlb_ref[...] = jnp.zeros_like(dlb_ref)

        vs = vs_ref[...].astype(F32)
        z, rstd, vn = _sg_normed(vs, lg_ref[...], lb_ref[...])
        vn = vn.astype(BF16)
        tril = _sg_tril()
        dvn = []
        for g in range(SG_GROUPS):
            sl = slice(g * SG_GROUP_DIM, (g + 1) * SG_GROUP_DIM)
            w = jnp.where(tril, w_ref[g], 0.0).astype(BF16)
            vg = vn[:, sl]
            sp = jnp.dot(w, vg, preferred_element_type=F32) + bias_ref[:, sl]
            uv = u_ref[:, sl].astype(F32)
            dsu_g = dsu_ref[:, sl].astype(F32)
            du_ref[:, sl] = (dsu_g * sp * _gelu_grad(uv)).astype(BF16)
            dsp = dsu_g * _gelu(uv)
            dsp_b = dsp.astype(BF16)
            dw = lax.dot_general(dsp_b, vg, (((1,), (1,)), ((), ())), preferred_element_type=F32)
            dw_ref[g] += jnp.where(tril, dw, 0.0)
            dbias_ref[:, sl] += jnp.broadcast_to(jnp.sum(dsp, axis=1, keepdims=True), (SG_CHUNK, SG_GROUP_DIM))
            dvn.append(lax.dot_general(w, dsp_b, (((0,), (0,)), ((), ())), preferred_element_type=F32))
        dvn = jnp.concatenate(dvn, axis=1)
        dlg_ref[...] += jnp.sum(dvn * z, axis=0, keepdims=True)
        dlb_ref[...] += jnp.sum(dvn, axis=0, keepdims=True)
        dz = dvn * lg_ref[...]
        dgv = rstd * (dz - jnp.mean(dz, axis=1, keepdims=True) - z * jnp.mean(dz * z, axis=1, keepdims=True))
        dvs_ref[...] = (dgv * _gelu_grad(vs)).astype(BF16)

    vec = pl.BlockSpec((1, width), lambda i: (0, 0))
    row = pl.BlockSpec((SG_CHUNK, width), lambda i: (i, 0))
    fixed = pl.BlockSpec((SG_CHUNK, width), lambda i: (0, 0))
    w_spec = pl.BlockSpec((SG_GROUPS, SG_CHUNK, SG_CHUNK), lambda i: (0, 0, 0))
    act = jax.ShapeDtypeStruct((t, width), BF16)
    return _pallas_call(
        body,
        out_shape=(act, act, jax.ShapeDtypeStruct((SG_GROUPS, SG_CHUNK, SG_CHUNK), F32),
                   jax.ShapeDtypeStruct((SG_CHUNK, width), F32), jax.ShapeDtypeStruct((1, width), F32),
                   jax.ShapeDtypeStruct((1, width), F32)),
        grid=(t // SG_CHUNK,),
        in_specs=[pl.BlockSpec((SG_CHUNK, width), lambda i: (i, u_blk)), pl.BlockSpec((SG_CHUNK, width), lambda i: (i, vs_blk)),
                  row, vec, vec, w_spec, fixed],
        out_specs=(row, row, w_spec, fixed, vec, vec),
        compiler_params=_cparams(("arbitrary",), 14 * _nbytes((SG_CHUNK, width), F32)),
        name=name)(proj, proj, dsu, lg, lb, sg_w, bias)


def _gate_fwd(proj, ga_blk, gs_blk, y_att, y_sg, name):
    t, d = y_att.shape
    tm, tn = _rows(t, 512), _tile(d, GROUP_WIDTH)

    def body(ga_ref, gs_ref, ya_ref, ys_ref, o_ref):
        o_ref[...] = (_sigmoid(ga_ref[...].astype(F32)) * ya_ref[...].astype(F32)
                      + _sigmoid(gs_ref[...].astype(F32)) * ys_ref[...].astype(F32)).astype(BF16)

    own = pl.BlockSpec((tm, tn), lambda i, j: (i, j))
    return _pallas_call(
        body, out_shape=jax.ShapeDtypeStruct((t, d), BF16), grid=(t // tm, d // tn),
        in_specs=[pl.BlockSpec((tm, tn), lambda i, j: (i, ga_blk + j)), pl.BlockSpec((tm, tn), lambda i, j: (i, gs_blk + j)),
                  own, own],
        out_specs=own, compiler_params=_cparams(("parallel", "parallel"), 6 * _nbytes((tm, tn), F32)),
        name=name)(proj, proj, y_att, y_sg)


def _gate_bwd(proj, ga_blk, gs_blk, y_att, y_sg, dmerged, name):
    t, d = y_att.shape
    tm, tn = _rows(t, 512), _tile(d, GROUP_WIDTH)

    def body(ga_ref, gs_ref, ya_ref, ys_ref, dm_ref, dya_ref, dys_ref, dga_ref, dgs_ref):
        dm = dm_ref[...].astype(F32)
        for g_ref, y_ref, dy_ref, dg_ref in ((ga_ref, ya_ref, dya_ref, dga_ref), (gs_ref, ys_ref, dys_ref, dgs_ref)):
            sg = _sigmoid(g_ref[...].astype(F32))
            dy_ref[...] = (dm * sg).astype(BF16)
            dg_ref[...] = (dm * y_ref[...].astype(F32) * sg * (1.0 - sg)).astype(BF16)

    own = pl.BlockSpec((tm, tn), lambda i, j: (i, j))
    out = jax.ShapeDtypeStruct((t, d), BF16)
    return _pallas_call(
        body, out_shape=(out, out, out, out), grid=(t // tm, d // tn),
        in_specs=[pl.BlockSpec((tm, tn), lambda i, j: (i, ga_blk + j)), pl.BlockSpec((tm, tn), lambda i, j: (i, gs_blk + j)),
                  own, own, own],
        out_specs=(own, own, own, own), compiler_params=_cparams(("parallel", "parallel"), 10 * _nbytes((tm, tn), F32)),
        name=name)(proj, proj, y_att, y_sg, dmerged)


def _mixer_forward(x, wb, small, in_specs, sg_specs, out_specs):
    t, d = x.shape
    att_w = N_GROUPS * GROUP_WIDTH
    sg_w = SG_GROUPS * SG_GROUP_DIM
    n = _rmsnorm_fwd(x, small['mix_norm'], "mix_norm")
    rider, names = _gather(wb, in_specs)
    proj, got = _matmul([(n, _full(wb, 'w_in'))], 'nn', BF16, "mix_in", b3=True, caps=(1024, 1024, 1024), rider=rider)
    _landed(wb, names, got)
    tables = _rope_tables(t)
    qk0, grouped = _rope_fwd(proj, tables, "mix_rope")
    qkv = [(qk0, qk0, proj, (0, 1, 2 * N_GROUPS))] + [g + ((0, 0, 0),) for g in grouped]
    outs, lses = zip(*[_att_fwd(*args, dil, f"att_fwd{gi}") for gi, (args, dil) in enumerate(zip(qkv, DILATIONS))])
    o_b, o_f, lse = _att_combine(outs, lses, "att_combine")
    y_att = _matmul([(o_b, _full(wb, 'w_att_out'))], 'nn', BF16, "mix_att_out", b3=True)
    bias = jnp.repeat(small['sg_b'].T, SG_GROUP_DIM, axis=1)
    u_blk, vs_blk = 3 * att_w // sg_w, 3 * att_w // sg_w + 1
    su = _sg_fwd(proj, u_blk, vs_blk, small['sg_ln_g'], small['sg_ln_b'], small['sg_w'], bias, "sg_fwd")
    rider, names = _gather(wb, sg_specs)
    y_sg, got = _matmul([(su, _full(wb, 'w_sg_out'))], 'nn', BF16, "mix_sg_out", b3=True, rider=rider)
    _landed(wb, names, got)
    ga_blk = (3 * att_w + 2 * sg_w) // _tile(d, GROUP_WIDTH)
    gs_blk = ga_blk + d // _tile(d, GROUP_WIDTH)
    merged = _gate_fwd(proj, ga_blk, gs_blk, y_att, y_sg, "gate_fwd")
    rider, names = _gather(wb, out_specs)
    x_next, got = _matmul([(merged, _full(wb, 'w_out'))], 'nn', F32, "mix_out", residual=x, rider=rider)
    _landed(wb, names, got)
    saved = (n, proj, qkv, tables, o_b, o_f, lse, y_att, su, y_sg, merged, bias, (u_blk, vs_blk, ga_blk, gs_blk))
    return x_next, saved


def _mixer_backward(x, wb, small, saved, dx_next, dx_next_b, c_idx, first_rider, pending):
    n, proj, qkv, tables, o_b, o_f, lse, y_att, su, y_sg, merged, bias, (u_blk, vs_blk, ga_blk, gs_blk) = saved
    s = N_CHIPS
    dmerged, carried = _matmul([(dx_next_b, _full(wb, 'w_out'))], 'nt', BF16, "mix_out_dx", rider=first_rider)
    g_w_out = _matmul([(merged, dx_next_b)], 'tn', BF16, "mix_out_dw", caps=(1024, 1024, 1024))
    dy_att, dy_sg, dg_att, dg_sg = _gate_bwd(proj, ga_blk, gs_blk, y_att, y_sg, dmerged, "gate_bwd")

    g_w_att_out = _matmul([(o_b, dy_att)], 'tn', BF16, "mix_att_out_dw", out3=s)
    do = _matmul([(dy_att, _full(wb, 'w_att_out'))], 'nt', F32, "mix_att_out_dx", b3=True)
    delta = _att_delta(do, o_f, "att_delta")
    stats = [(do, lse, delta)] + _regroup([do, lse, delta], "att_regroup")
    dqkv = []
    for gi, ((q, k, v, offs), st, dil) in enumerate(zip(qkv, stats, DILATIONS)):
        dq = _att_bwd_dq(q, k, v, *st, offs, dil, f"att_bwd_dq{gi}")
        dk, dv = _att_bwd_dkv(q, k, v, *st, offs, dil, f"att_bwd_dkv{gi}")
        dqkv.append((dq, dk, dv))
    dqkv = _rope_bwd(*dqkv[0], dqkv[1:], tables, "mix_rope_bwd")

    g_w_sg_out = _matmul([(su, dy_sg)], 'tn', BF16, "mix_sg_out_dw", out3=s)
    out_names = ['w_out', 'w_att_out', 'w_sg_out']
    out_g4 = [a.reshape(wb[nm].shape) for a, nm in zip([g_w_out, g_w_att_out, g_w_sg_out], out_names)]
    dsu, from_sibling = _matmul([(dy_sg, _full(wb, 'w_sg_out'))], 'nt', BF16, "mix_sg_out_dx", b3=True, rider=_sibling_rider(out_g4))
    out_parts = [_sibling_sum(a, b, c_idx, f"rs_sum1_{nm}") for a, b, nm in zip(out_g4, from_sibling, out_names)]
    out_names, out_parts = out_names + list(pending), out_parts + list(pending.values())
    du, dvs, g_sg_w, g_bias, g_lg, g_lb = _sg_bwd(proj, u_blk, vs_blk, dsu, small['sg_ln_g'], small['sg_ln_b'],
                                                   small['sg_w'], bias, "sg_bwd")
    gs = {'sg_w': g_sg_w, 'sg_b': g_bias[:, ::SG_GROUP_DIM].T, 'sg_ln_g': g_lg, 'sg_ln_b': g_lb}

    dproj = jnp.concatenate([dqkv, du, dvs, dg_att, dg_sg], axis=1)
    g_w_in, out_recv = _matmul([(n, dproj)], 'tn', BF16, "mix_in_dw", out3=s, caps=(1024, 1024, 1024),
                               rider=_scatter_rider(out_parts))
    (p_w_in,) = _reduce_first([g_w_in], ['w_in'], wb, c_idx)
    dn, (r_w_in,) = _matmul([(dproj, _full(wb, 'w_in'))], 'nt', BF16, "mix_in_dx", b3=True, caps=(1024, 1024, 512),
                            rider=_scatter_rider([p_w_in]))
    dx, dx_b, gs['mix_norm'] = _rmsnorm_bwd(x, small['mix_norm'], dn, dx_next, "mix_norm_bwd")
    g = {nm: (p, r) for nm, p, r in zip(out_names, out_parts, out_recv)}
    g['w_in'] = (p_w_in, r_w_in)
    return dx, dx_b, g, gs, carried


def _step(x, target, wb, small, c_idx, pc_idx):
    def last_stage(g):
        names = list(g)
        return names, _halves_rider([_chip_sum(*g[n], pc_idx, f"rs_sum2_{n}") for n in names])

    wb = dict(wb)
    rider, names = _gather(wb, ['ffn1_w_gate', 'ffn1_w_up'])
    _landed(wb, names, _exchange(rider, "gather_first"))
    half_in = wb['w_in'].shape[2] // 2
    x1, s1 = _ffn_forward(x, small['ffn1_norm'], wb, "ffn1", ['ffn1_w_down', ('w_in', 0, half_in)], [('w_in', half_in, 2 * half_in)])
    up_rows = wb['ffn2_w_up'].shape[2]
    up_cut = up_rows // 32 * 15
    x2, s2 = _mixer_forward(x1, wb, small, ['w_att_out', 'w_sg_out', 'w_out', 'ffn2_w_gate'],
                            [('ffn2_w_up', 0, up_cut)], [('ffn2_w_up', up_cut, up_rows)])
    x3, s3 = _ffn_forward(x2, small['ffn2_norm'], wb, "ffn2", ['ffn2_w_down'], None)
    loss, dx3, dx3_b, g_final = _final_loss(x3, small['final_norm'], target, "final_loss")
    gs = {'final_norm': g_final}
    whole = {}
    dx2, dx2_b, gs['ffn2_norm'], g, pending, _ = _ffn_backward(x2, small['ffn2_norm'], wb, s3, dx3, dx3_b, c_idx, "ffn2", True)
    names, rider = last_stage(g)
    dx1, dx1_b, g, gs_mix, got = _mixer_backward(x1, wb, small, s2, dx2, dx2_b, c_idx, rider, pending)
    whole.update(zip(names, got))
    gs.update(gs_mix)
    names, rider = last_stage(g)
    dx0, _, gs['ffn1_norm'], g, _, got = _ffn_backward(x, small['ffn1_norm'], wb, s1, dx1, dx1_b, c_idx, "ffn1", False,
                                                       dwd_rider=rider)
    whole.update(zip(names, got))
    names, rider = last_stage(g)
    whole.update(zip(names, _exchange(rider, "rs_halves")))
    return loss, dx0, whole, gs


def _cast_into_gathered(wt, p_idx, name):
    r, ccols = wt.shape[0] // 2, wt.shape[1]
    tm = _rows(r, 256)
    nb = r // tm

    def body(p_ref, w_ref, o_ref):
        o_ref[...] = w_ref[...].astype(BF16)

    grid_spec = pltpu.PrefetchScalarGridSpec(
        num_scalar_prefetch=1, grid=(2, nb),
        in_specs=[pl.BlockSpec((tm, ccols), lambda h, i, pr: (h * nb + i, 0))],
        out_specs=pl.BlockSpec((None, None, tm, ccols), lambda h, i, pr: (pr[0], h, i, 0)))
    return pl.pallas_call(body, out_shape=jax.ShapeDtypeStruct((N_CHIPS, 2, r, ccols), BF16), grid_spec=grid_spec,
                          compiler_params=_cparams(("parallel", "parallel"), 2 * _nbytes((tm, ccols), F32)), name=name)(p_idx, wt)


def _sibling_rider(grads):
    n = len(grads)

    def copy(src, dst, sems, i):
        x, y, c, _ = _place()
        return pltpu.make_async_remote_copy(src[i].at[:, 1 - c], dst[i], sems[0].at[i], sems[1].at[i],
                                            device_id=(x, y, 1 - c), device_id_type=MESH)

    def start(src, dst, sems):
        for i in range(n):
            copy(src, dst, sems, i).start()

    def finish(src, dst, sems):
        for i in range(n):
            copy(src, dst, sems, i).wait()

    return _Rider(grads, [jax.ShapeDtypeStruct((g.shape[0],) + g.shape[2:], g.dtype) for g in grads], {},
                  [pltpu.SemaphoreType.DMA((n,))] * 2, start, finish)


def _merge_riders(a, b):
    n_in, n_out, n_sem = len(a.operands), len(a.out_shapes), len(a.sems)

    def both(which):
        def run(ins, outs, sems):
            getattr(a, which)(ins[:n_in], outs[:n_out], sems[:n_sem])
            getattr(b, which)(ins[n_in:], outs[n_out:], sems[n_sem:])
        return run

    aliases = dict(a.aliases)
    aliases.update({n_in + k: n_out + v for k, v in b.aliases.items()})
    return _Rider(list(a.operands) + list(b.operands), list(a.out_shapes) + list(b.out_shapes), aliases,
                  list(a.sems) + list(b.sems), both('start'), both('finish'))


def _halves_rider(bufs):
    n = len(bufs)

    def copy(ref, sems, i, c, x, y):
        return pltpu.make_async_remote_copy(ref, ref, sems[0].at[i], sems[1].at[i], device_id=(x, y, 1 - c), device_id_type=MESH)

    def start(_, buf, sems):
        x, y, c, _ = _place()
        for i in range(n):
            copy(buf[i].at[c], sems, i, c, x, y).start()

    def finish(_, buf, sems):
        x, y, c, _ = _place()
        for i in range(n):
            copy(buf[i].at[c], sems, i, c, x, y).wait_send()
            copy(buf[i].at[1 - c], sems, i, c, x, y).wait_recv()

    return _Rider(bufs, [jax.ShapeDtypeStruct(b.shape, b.dtype) for b in bufs], {i: i for i in range(n)},
                  [pltpu.SemaphoreType.DMA((n,))] * 2, start, finish)


def _sibling_sum(grad, recv, c_idx, name):
    s, _, r, ccols = grad.shape
    tm = _rows(r, 256)

    def body(c_ref, g_ref, r_ref, o_ref):
        o_ref[...] = (g_ref[...].astype(F32) + r_ref[...].astype(F32)).astype(BF16)

    grid_spec = pltpu.PrefetchScalarGridSpec(
        num_scalar_prefetch=1, grid=(s, r // tm),
        in_specs=[pl.BlockSpec((None, None, tm, ccols), lambda q, i, cr: (q, cr[0], i, 0)),
                  pl.BlockSpec((None, tm, ccols), lambda q, i, cr: (q, i, 0))],
        out_specs=pl.BlockSpec((None, tm, ccols), lambda q, i, cr: (q, i, 0)))
    return pl.pallas_call(body, out_shape=jax.ShapeDtypeStruct((s, r, ccols), BF16), grid_spec=grid_spec,
                          compiler_params=_cparams(("parallel", "parallel"), 4 * _nbytes((tm, ccols), F32)), name=name)(c_idx, grad, recv)


def _chip_sum(part, recv, pc_idx, name):
    _, r, ccols = part.shape
    tm = _rows(r, 256)

    def body(pc_ref, own_ref, r0_ref, r1_ref, r2_ref, o_ref):
        acc = own_ref[...].astype(F32) + r0_ref[...].astype(F32)
        acc = acc + r1_ref[...].astype(F32)
        o_ref[...] = acc + r2_ref[...].astype(F32)

    def slot(j):
        return pl.BlockSpec((None, tm, ccols), lambda i, pc: (j, i, 0))

    grid_spec = pltpu.PrefetchScalarGridSpec(
        num_scalar_prefetch=1, grid=(r // tm,),
        in_specs=[pl.BlockSpec((None, tm, ccols), lambda i, pc: (pc[0], i, 0)), slot(0), slot(1), slot(2)],
        out_specs=pl.BlockSpec((None, tm, ccols), lambda i, pc: (pc[1], i, 0)))
    return pl.pallas_call(body, out_shape=jax.ShapeDtypeStruct((2, r, ccols), F32), grid_spec=grid_spec,
                          compiler_params=_cparams(("parallel",), 6 * _nbytes((tm, ccols), F32)), name=name)(pc_idx, part, recv, recv, recv)


def _all_reduce_small(vec):
    _, r, _ = vec.shape

    def body(v_ref, o_ref, parts, send1, recv1, send2, recv2):
        x, y, c, _ = _place()
        me = 4 * x + 2 * y + c
        peers = []
        for k in range(1, N_DEV):
            px, py, pc = (1 - x if k & 4 else x, 1 - y if k & 2 else y, 1 - c if k & 1 else c)
            peers.append(((px, py, pc), 4 * px + 2 * py + pc))
        parts[me] = v_ref[me]
        cps = []
        for k, (peer, peer_id) in enumerate(peers):
            cp = pltpu.make_async_remote_copy(v_ref.at[peer_id], parts.at[me], send1.at[k], recv1.at[k],
                                              device_id=peer, device_id_type=MESH)
            cp.start()
            cps.append(cp)
        for cp in cps:
            cp.wait()
        acc = parts[0]
        for dev in range(1, N_DEV):
            acc = acc + parts[dev]
        o_ref[me] = acc
        cps = []
        for k, (peer, _) in enumerate(peers):
            cp = pltpu.make_async_remote_copy(o_ref.at[me], o_ref.at[me], send2.at[k], recv2.at[k],
                                              device_id=peer, device_id_type=MESH)
            cp.start()
            cps.append(cp)
        for cp in cps:
            cp.wait()

    vm = pl.BlockSpec(memory_space=pltpu.VMEM)
    sems = pltpu.SemaphoreType.DMA((N_DEV - 1,))
    return pl.pallas_call(
        body, out_shape=jax.ShapeDtypeStruct(vec.shape, F32), in_specs=[vm], out_specs=vm,
        scratch_shapes=[pltpu.VMEM((N_DEV, r, LANES), F32), sems, sems, sems, sems],
        compiler_params=pltpu.CompilerParams(vmem_limit_bytes=int(8 * _nbytes((N_DEV, r, LANES), F32))),
        name="all_reduce_small")(vec)


def _adamw(wt, g, m, v, name, rider=None):
    r, ccols = wt.shape
    tm = _rows(r, max(8, (MIB // (4 * ccols)) // 8 * 8))
    blk = pl.BlockSpec((tm, ccols), lambda i: (i, 0))

    def body(w_ref, g_ref, m_ref, v_ref, go_ref, d_ref, mo_ref, vo_ref):
        gv = g_ref[...]
        go_ref[...] = gv
        mv = ADAM_B1 * m_ref[...] + (1.0 - ADAM_B1) * gv
        vv = ADAM_B2 * v_ref[...] + (1.0 - ADAM_B2) * (gv * gv)
        m_hat = mv / (1.0 - ADAM_B1 ** ADAM_STEP)
        v_hat = vv / (1.0 - ADAM_B2 ** ADAM_STEP)
        d_ref[...] = -ADAM_LR * (m_hat / (jnp.sqrt(v_hat) + ADAM_EPS) + ADAM_WD * w_ref[...])
        mo_ref[...] = mv
        vo_ref[...] = vv

    out = jax.ShapeDtypeStruct((r, ccols), F32)
    return _run(body, name=name, grid=(r // tm,), in_specs=[blk] * 4, out_specs=[blk] * 4, out_shape=[out] * 4, scratch_shapes=[],
                operands=[wt, g, m, v], block_bytes=8 * _nbytes((tm, ccols), F32), rider=rider, pinned=False)


def _as_rows(a):
    rows = a.reshape(-1, LANES)
    return jnp.pad(rows, ((0, -rows.shape[0] % 8), (0, 0)))


def kernel(x, ffn1_norm, ffn1_w_gate, ffn1_w_up, ffn1_w_down, mix_norm, w_in, sg_ln_g, sg_ln_b, sg_w, sg_b, w_att_out, w_sg_out, w_out, ffn2_norm, ffn2_w_gate, ffn2_w_up, ffn2_w_down, final_norm, loss_target, m_ffn1_norm, m_ffn1_w_gate, m_ffn1_w_up, m_ffn1_w_down, m_mix_norm, m_w_in, m_sg_ln_g, m_sg_ln_b, m_sg_w, m_sg_b, m_w_att_out, m_w_sg_out, m_w_out, m_ffn2_norm, m_ffn2_w_gate, m_ffn2_w_up, m_ffn2_w_down, m_final_norm, v_ffn1_norm, v_ffn1_w_gate, v_ffn1_w_up, v_ffn1_w_down, v_mix_norm, v_w_in, v_sg_ln_g, v_sg_ln_b, v_sg_w, v_sg_b, v_w_att_out, v_w_sg_out, v_w_out, v_ffn2_norm, v_ffn2_w_gate, v_ffn2_w_up, v_ffn2_w_down, v_final_norm):
    given = dict(locals())
    wts = {n: given[n] for n in WEIGHT_NAMES}
    ms = {n: given["m_" + n] for n in WEIGHT_NAMES}
    vs = {n: given["v_" + n] for n in WEIGHT_NAMES}
    t, d = x.shape[-2], x.shape[-1]
    xc, yc, cc = lax.axis_index("x"), lax.axis_index("y"), lax.axis_index("c")

    shard2d = {n: wts[n].reshape(wts[n].shape[-2:]) for n in BIG_NAMES}
    p_idx = jnp.reshape(2 * xc + yc, (1,)).astype(jnp.int32)
    c_idx = jnp.reshape(cc, (1,)).astype(jnp.int32)
    pc_idx = jnp.stack([2 * xc + yc, cc]).astype(jnp.int32)
    wb = {n: _cast_into_gathered(shard2d[n], p_idx, f"cast_{n}") for n in BIG_NAMES}

    small = {n: wts[n].reshape(-1, wts[n].shape[-1]) for n in SMALL_NAMES}
    small['sg_w'] = wts['sg_w'].reshape(wts['sg_w'].shape[-3:])
    loss, dx, whole, gs = _step(x.reshape(t, d), loss_target.reshape(t, d), wb, small, c_idx, pc_idx)
    loss = lax.psum(loss[0, 0], ("x", "y", "c"))

    def pack(tree):
        rows = jnp.concatenate([_as_rows(tree[n]) for n in SMALL_NAMES], axis=0)
        return jnp.pad(rows, ((0, -rows.shape[0] % (8 * N_DEV)), (0, 0)))

    packed = pack(gs)
    packed = _all_reduce_small(packed.reshape(N_DEV, -1, LANES)).reshape(packed.shape)

    grads, delta, new_m, new_v = {}, {}, {}, {}
    for n in BIG_NAMES:
        shape, flat = wts[n].shape, shard2d[n].shape
        out = _adamw(shard2d[n], whole[n].reshape(flat), ms[n].reshape(flat), vs[n].reshape(flat), f"adamw_{n}")
        grads[n], delta[n], new_m[n], new_v[n] = (a.reshape(shape) for a in out)

    small_out = _adamw(pack(wts), packed, pack(ms), pack(vs), "adamw_small")
    row = 0
    for n in SMALL_NAMES:
        shape = wts[n].shape
        sz = wts[n].size // LANES
        grads[n], delta[n], new_m[n], new_v[n] = (a[row:row + sz].reshape(shape) for a in small_out)
        row += sz + -sz % 8

    return (loss, dx.reshape(x.shape), *[grads[n] for n in WEIGHT_NAMES], *[delta[n] for n in WEIGHT_NAMES],
            *[new_m[n] for n in WEIGHT_NAMES], *[new_v[n] for n in WEIGHT_NAMES])
```

```python
import functools

import jax
import jax.numpy as jnp
from jax import lax
from jax.experimental import pallas as pl
from jax.experimental.pallas import tpu as pltpu

F32 = jnp.float32
BF16 = jnp.bfloat16
MESH = pl.DeviceIdType.MESH

NORM_EPS = 1e-6
LN_EPS = 1e-5
HEAD_DIM = 128
HEADS_PER_GROUP = 4
GROUP_WIDTH = HEADS_PER_GROUP * HEAD_DIM
DILATIONS = (1, 4, 16)
N_GROUPS = len(DILATIONS)
ATT_BLOCK = 128
ROPE_DIM = HEAD_DIM // 4
ROPE_THETA = 500000.0
SG_CHUNK = 128
SG_GROUPS = 12
SG_GROUP_DIM = 128
MASKED = -1e30

ADAM_LR = 0.001
ADAM_B1 = 0.9
ADAM_B2 = 0.999
ADAM_EPS = 1e-08
ADAM_WD = 0.01
ADAM_STEP = 10

N_CHIPS = 4
N_DEV = 8
LANES = 128
MIB = 2 ** 20
VMEM_BYTES_V7X = 64 * MIB

WEIGHT_NAMES = ['ffn1_norm', 'ffn1_w_gate', 'ffn1_w_up', 'ffn1_w_down', 'mix_norm', 'w_in', 'sg_ln_g', 'sg_ln_b',
                'sg_w', 'sg_b', 'w_att_out', 'w_sg_out', 'w_out', 'ffn2_norm', 'ffn2_w_gate', 'ffn2_w_up',
                'ffn2_w_down', 'final_norm']
BIG = [('ffn1_w_gate', 1), ('ffn1_w_up', 1), ('ffn1_w_down', 0), ('w_in', 1), ('w_att_out', 1), ('w_sg_out', 1),
       ('w_out', 0), ('ffn2_w_gate', 1), ('ffn2_w_up', 1), ('ffn2_w_down', 0)]
BIG_NAMES = [n for n, _ in BIG]
SMALL_NAMES = [n for n in WEIGHT_NAMES if n not in BIG_NAMES]


def _nbytes(shape, dtype):
    n = jnp.dtype(dtype).itemsize
    for s in shape:
        if s is not None:
            n *= s
    return n


def _pallas_call(*args, **kw):
    kw['out_shape'] = jax.tree.map(lambda s: pltpu.HBM(s.shape, s.dtype), kw['out_shape'])
    call = pl.pallas_call(*args, **kw)

    def pinned(*operands):
        return call(*[o if jnp.issubdtype(o.dtype, jnp.integer) else pltpu.with_memory_space_constraint(o, pltpu.HBM)
                      for o in operands])

    return pinned


def _cparams(sem, block_bytes, **kw):
    limit = int(min(max(3 * block_bytes, 32 * MIB), VMEM_BYTES_V7X - 8 * MIB))
    return pltpu.CompilerParams(dimension_semantics=sem, vmem_limit_bytes=limit, **kw)


def _tile(dim, cap):
    best = None
    for t in range(LANES, min(dim, cap) + 1, LANES):
        if dim % t == 0:
            best = t
    if best is None:
        assert dim <= cap, (dim, cap)
        return dim
    return best


def _rows(dim, cap):
    best = None
    for t in range(8, min(dim, cap) + 1, 8):
        if dim % t == 0:
            best = t
    assert best is not None, (dim, cap)
    return best


def _place():
    x, y, c = lax.axis_index("x"), lax.axis_index("y"), lax.axis_index("c")
    others = [(1 - x, y), (x, 1 - y), (1 - x, 1 - y)]
    return x, y, c, others


class _Rider:
    def __init__(self, operands, out_shapes, aliases, sems, start, finish):
        self.operands = operands
        self.out_shapes = out_shapes
        self.aliases = aliases
        self.sems = sems
        self.start = start
        self.finish = finish


def _run(body, *, name, grid, in_specs, out_specs, out_shape, scratch_shapes, operands, block_bytes, rider=None, pinned=True):
    call = _pallas_call if pinned else pl.pallas_call
    if rider is None:
        sem = ("parallel",) * (len(grid) - 1) + ("arbitrary",)
        return call(body, out_shape=out_shape, grid=grid, in_specs=in_specs, out_specs=out_specs,
                    scratch_shapes=scratch_shapes, compiler_params=_cparams(sem, block_bytes), name=name)(*operands)
    n_in, n_out, n_scr = len(operands), len(out_shape), len(scratch_shapes)
    r_in, r_out = len(rider.operands), len(rider.out_shapes)
    any_spec = pl.BlockSpec(memory_space=pl.ANY)

    def wrapped(*refs):
        ins, refs = refs[:n_in], refs[n_in:]
        r_ins, refs = refs[:r_in], refs[r_in:]
        outs, refs = refs[:n_out], refs[n_out:]
        r_outs, refs = refs[:r_out], refs[r_out:]
        scr, sems = refs[:n_scr], refs[n_scr:]
        if not grid:
            rider.start(r_ins, r_outs, sems)
            rider.finish(r_ins, r_outs, sems)
            return
        ids = [pl.program_id(a) for a in range(len(grid))]
        first = functools.reduce(jnp.logical_and, [i == 0 for i in ids])
        last = functools.reduce(jnp.logical_and, [i == g - 1 for i, g in zip(ids, grid)])

        @pl.when(first)
        def _():
            rider.start(r_ins, r_outs, sems)

        body(*ins, *outs, *scr)

        @pl.when(last)
        def _():
            rider.finish(r_ins, r_outs, sems)

    results = call(
        wrapped, out_shape=list(out_shape) + list(rider.out_shapes), grid=grid,
        in_specs=list(in_specs) + [any_spec] * r_in, out_specs=list(out_specs) + [any_spec] * r_out,
        scratch_shapes=list(scratch_shapes) + list(rider.sems),
        input_output_aliases={n_in + k: n_out + v for k, v in rider.aliases.items()},
        compiler_params=_cparams(("arbitrary",) * len(grid) if grid else None, block_bytes, has_side_effects=True),
        name=name)(*operands, *rider.operands)
    return results[:n_out], results[n_out:]


def _exchange(rider, name):
    return _run(None, name=name, grid=(), in_specs=[], out_specs=[], out_shape=[], scratch_shapes=[], operands=[],
                block_bytes=0, rider=rider)[1]


def _gather_rider(items):
    bufs, index = [], []
    for b, r0, r1 in items:
        if not any(b is q for q in bufs):
            bufs.append(b)
        index.append(([k for k, q in enumerate(bufs) if q is b][0], r0, r1))
    n = len(index)

    def piece(refs, k, chip, half):
        bi, r0, r1 = index[k]
        return refs[bi].at[chip, half, pl.ds(r0, r1 - r0)]

    def copy(ref, sem_pair, k, j, to):
        return pltpu.make_async_remote_copy(ref, ref, sem_pair[0].at[k, j], sem_pair[1].at[k, j], device_id=to, device_id_type=MESH)

    def start(r_ins, buf, sems):
        x, y, c, others = _place()
        for k in range(n):
            for j, (ox, oy) in enumerate(others):
                copy(piece(buf, k, 2 * x + y, c), sems[:2], k, j, (ox, oy, c)).start()

    def finish(r_ins, buf, sems):
        x, y, c, others = _place()
        for k in range(n):
            for j, (ox, oy) in enumerate(others):
                got = piece(buf, k, 2 * ox + oy, c)
                copy(got, sems[:2], k, j, (ox, oy, c)).wait_recv()
                copy(got, sems[2:], k, j, (x, y, 1 - c)).start()
        for k in range(n):
            for j, (ox, oy) in enumerate(others):
                copy(piece(buf, k, 2 * ox + oy, 1 - c), sems[2:], k, j, (x, y, 1 - c)).wait_recv()
        for k in range(n):
            for j, (ox, oy) in enumerate(others):
                copy(piece(buf, k, 2 * x + y, c), sems[:2], k, j, (ox, oy, c)).wait_send()
                copy(piece(buf, k, 2 * ox + oy, c), sems[2:], k, j, (x, y, 1 - c)).wait_send()

    return _Rider(bufs, [jax.ShapeDtypeStruct(b.shape, b.dtype) for b in bufs], {i: i for i in range(len(bufs))},
                  [pltpu.SemaphoreType.DMA((n, 3))] * 4, start, finish)


def _scatter_rider(parts):
    n = len(parts)

    def copy(src, dst, sems, i, j, to):
        return pltpu.make_async_remote_copy(src, dst, sems[0].at[i, j], sems[1].at[i, j], device_id=to, device_id_type=MESH)

    def start(src, dst, sems):
        x, y, c, others = _place()
        for i in range(n):
            for j, (ox, oy) in enumerate(others):
                copy(src[i].at[2 * ox + oy], dst[i].at[j], sems, i, j, (ox, oy, c)).start()

    def finish(src, dst, sems):
        x, y, c, others = _place()
        for i in range(n):
            for j, (ox, oy) in enumerate(others):
                copy(src[i].at[2 * ox + oy], dst[i].at[j], sems, i, j, (ox, oy, c)).wait()

    return _Rider(parts, [jax.ShapeDtypeStruct((3,) + p.shape[1:], p.dtype) for p in parts], {},
                  [pltpu.SemaphoreType.DMA((n, 3))] * 2, start, finish)


def _matmul(pairs, mode, out_dtype, name, *, scale=1.0, residual=None, b3=False, out3=0, caps=(1024, 1024, 512), rider=None):
    a0, b0 = pairs[0]
    if mode == 'nn':
        m, k = a0.shape
        n = b0.shape[0] * b0.shape[2] if b3 else b0.shape[1]
    elif mode == 'nt':
        m = a0.shape[0]
        n, k = (b0.shape[1], b0.shape[0] * b0.shape[2]) if b3 else b0.shape
    else:
        k, m = a0.shape
        n = b0.shape[1]
    tm = _tile(m, caps[0])
    tn = _tile(n, caps[1])
    tk = _tile(k, caps[2])
    if b3 and mode == 'nn':
        tn = b0.shape[2]
    if b3 and mode == 'nt':
        tk = b0.shape[2]
    if out3:
        tn = n // out3
    nk = k // tk
    if mode == 'tn':
        a_spec = pl.BlockSpec((tk, tm), lambda i, j, kk: (kk, i))
        b_spec = pl.BlockSpec((tk, tn), lambda i, j, kk: (kk, j))
        dims = ((0,), (0,))
    elif mode == 'nn':
        a_spec = pl.BlockSpec((tm, tk), lambda i, j, kk: (i, kk))
        b_spec = (pl.BlockSpec((None, tk, tn), lambda i, j, kk: (j, kk, 0)) if b3
                  else pl.BlockSpec((tk, tn), lambda i, j, kk: (kk, j)))
        dims = ((1,), (0,))
    else:
        a_spec = pl.BlockSpec((tm, tk), lambda i, j, kk: (i, kk))
        b_spec = (pl.BlockSpec((None, tn, tk), lambda i, j, kk: (kk, j, 0)) if b3
                  else pl.BlockSpec((tn, tk), lambda i, j, kk: (j, kk)))
        dims = ((1,), (1,))
    in_specs, operands = [], []
    for a, b in pairs:
        in_specs += [a_spec, b_spec]
        operands += [a, b]
    block_bytes = len(pairs) * (_nbytes((tm, tk), a0.dtype) + _nbytes((tk, tn), b0.dtype))
    if residual is not None:
        in_specs.append(pl.BlockSpec((tm, tn), lambda i, j, kk: (i, j)))
        operands.append(residual)
        block_bytes += _nbytes((tm, tn), F32)
    if out3:
        out_spec = pl.BlockSpec((None, tm, tn), lambda i, j, kk: (j, i, 0))
        out_shape = jax.ShapeDtypeStruct((out3, m, tn), out_dtype)
    else:
        out_spec = pl.BlockSpec((tm, tn), lambda i, j, kk: (i, j))
        out_shape = jax.ShapeDtypeStruct((m, n), out_dtype)
    block_bytes += _nbytes((tm, tn), out_dtype) + _nbytes((tm, tn), F32)
    n_pairs = len(pairs)
    has_res = residual is not None

    def body(*refs):
        o_ref, acc = refs[-2], refs[-1]
        kk = pl.program_id(2)

        def product():
            part = None
            for p in range(n_pairs):
                d = lax.dot_general(refs[2 * p][...].astype(BF16), refs[2 * p + 1][...].astype(BF16),
                                    (dims, ((), ())), preferred_element_type=F32)
                part = d if part is None else part + d
            return part

        def finish(r):
            if scale != 1.0:
                r = r * scale
            if has_res:
                r = refs[2 * n_pairs][...] + r
            o_ref[...] = r.astype(out_dtype)

        if nk == 1:
            finish(product())
            return

        @pl.when(kk == 0)
        def _():
            acc[...] = product()

        if nk > 2:
            @pl.when(jnp.logical_and(kk > 0, kk < nk - 1))
            def _():
                acc[...] += product()

        @pl.when(kk == nk - 1)
        def _():
            finish(acc[...] + product())

    res = _run(body, name=name, grid=(m // tm, n // tn, nk), in_specs=in_specs, out_specs=[out_spec], out_shape=[out_shape],
               scratch_shapes=[pltpu.VMEM((tm, tn), F32)], operands=operands, block_bytes=block_bytes, rider=rider)
    return res[0] if rider is None else (res[0][0], res[1])


def _rmsnorm_fwd(x, g, name):
    t, d = x.shape
    tm = _rows(t, 512)

    def body(x_ref, g_ref, o_ref):
        xv = x_ref[...]
        r = lax.rsqrt(jnp.mean(xv * xv, axis=1, keepdims=True) + NORM_EPS)
        o_ref[...] = (xv * r * g_ref[...]).astype(BF16)

    row = pl.BlockSpec((tm, d), lambda i: (i, 0))
    return _pallas_call(
        body, out_shape=jax.ShapeDtypeStruct((t, d), BF16), grid=(t // tm,),
        in_specs=[row, pl.BlockSpec((1, d), lambda i: (0, 0))], out_specs=row,
        compiler_params=_cparams(("parallel",), 2 * _nbytes((tm, d), F32)), name=name)(x, g)


def _rms_grad(xv, g, dn, d):
    r = lax.rsqrt(jnp.mean(xv * xv, axis=1, keepdims=True) + NORM_EPS)
    u = dn * g
    s = jnp.sum(xv * u, axis=1, keepdims=True)
    dx = r * u - xv * (r * r * r) * (s * (1.0 / d))
    return dx, dn * xv * r


def _rmsnorm_bwd(x, g, dn, dres, name):
    t, d = x.shape
    tm = _rows(t, 256)

    def body(x_ref, g_ref, dn_ref, dres_ref, dx_ref, dxb_ref, dg_ref):
        dx, dg_rows = _rms_grad(x_ref[...], g_ref[...], dn_ref[...].astype(F32), d)
        dx = dres_ref[...] + dx
        dx_ref[...] = dx
        dxb_ref[...] = dx.astype(BF16)

        @pl.when(pl.program_id(0) == 0)
        def _():
            dg_ref[...] = jnp.zeros_like(dg_ref)

        dg_ref[...] += jnp.sum(dg_rows, axis=0, keepdims=True)

    row = pl.BlockSpec((tm, d), lambda i: (i, 0))
    vec = pl.BlockSpec((1, d), lambda i: (0, 0))
    return _pallas_call(
        body, out_shape=(jax.ShapeDtypeStruct((t, d), F32), jax.ShapeDtypeStruct((t, d), BF16), jax.ShapeDtypeStruct((1, d), F32)),
        grid=(t // tm,), in_specs=[row, vec, row, row], out_specs=(row, row, vec),
        compiler_params=_cparams(("arbitrary",), 5 * _nbytes((tm, d), F32)), name=name)(x, g, dn, dres)


def _final_loss(x, g, target, name):
    t, d = x.shape
    tm = _rows(t, 256)

    def body(x_ref, g_ref, t_ref, loss_ref, dx_ref, dxb_ref, dg_ref):
        xv, gv = x_ref[...], g_ref[...]
        r = lax.rsqrt(jnp.mean(xv * xv, axis=1, keepdims=True) + NORM_EPS)
        err = xv * r * gv - t_ref[...]
        dx, dg_rows = _rms_grad(xv, gv, err * (1.0 / d), d)
        dx_ref[...] = dx
        dxb_ref[...] = dx.astype(BF16)

        @pl.when(pl.program_id(0) == 0)
        def _():
            dg_ref[...] = jnp.zeros_like(dg_ref)
            loss_ref[...] = jnp.zeros_like(loss_ref)

        dg_ref[...] += jnp.sum(dg_rows, axis=0, keepdims=True)
        row_loss = jnp.sum(err * err, axis=1, keepdims=True) * (0.5 / d)
        loss_ref[...] += jnp.sum(row_loss, axis=0, keepdims=True)

    row = pl.BlockSpec((tm, d), lambda i: (i, 0))
    vec = pl.BlockSpec((1, d), lambda i: (0, 0))
    return _pallas_call(
        body, out_shape=(jax.ShapeDtypeStruct((1, 1), F32), jax.ShapeDtypeStruct((t, d), F32),
                         jax.ShapeDtypeStruct((t, d), BF16), jax.ShapeDtypeStruct((1, d), F32)),
        grid=(t // tm,), in_specs=[row, vec, row], out_specs=(pl.BlockSpec((1, 1), lambda i: (0, 0)), row, row, vec),
        compiler_params=_cparams(("arbitrary",), 4 * _nbytes((tm, d), F32)), name=name)(x, g, target)


def _sigmoid(x):
    return 0.5 * jnp.tanh(0.5 * x) + 0.5


def _ffn_up(n, wg, wu, name, rider=None):
    t, d = n.shape
    s, _, f = wg.shape
    tm, tk = _tile(t, 1024), _tile(d, 1024)
    nk = d // tk

    def body(n_ref, wg_ref, wu_ref, a_ref, b_ref, h_ref, acc_g, acc_u):
        kk = pl.program_id(2)

        def products():
            nv = n_ref[...]
            return jnp.dot(nv, wg_ref[...], preferred_element_type=F32), jnp.dot(nv, wu_ref[...], preferred_element_type=F32)

        def finish(a, b):
            a_ref[...] = a.astype(BF16)
            b_ref[...] = b.astype(BF16)
            h_ref[...] = (a * _sigmoid(a) * b).astype(BF16)

        if nk == 1:
            finish(*products())
            return

        @pl.when(kk == 0)
        def _():
            acc_g[...], acc_u[...] = products()

        if nk > 2:
            @pl.when(jnp.logical_and(kk > 0, kk < nk - 1))
            def _():
                pg, pu = products()
                acc_g[...] += pg
                acc_u[...] += pu

        @pl.when(kk == nk - 1)
        def _():
            pg, pu = products()
            finish(acc_g[...] + pg, acc_u[...] + pu)

    w_spec = pl.BlockSpec((None, tk, f), lambda i, j, kk: (j, kk, 0))
    o_spec = pl.BlockSpec((tm, f), lambda i, j, kk: (i, j))
    out = jax.ShapeDtypeStruct((t, s * f), BF16)
    block_bytes = _nbytes((tm, tk), BF16) + 2 * _nbytes((tk, f), BF16) + 3 * _nbytes((tm, f), BF16) + 2 * _nbytes((tm, f), F32)
    return _run(body, name=name, grid=(t // tm, s, nk),
                in_specs=[pl.BlockSpec((tm, tk), lambda i, j, kk: (i, kk)), w_spec, w_spec], out_specs=[o_spec, o_spec, o_spec],
                out_shape=[out, out, out], scratch_shapes=[pltpu.VMEM((tm, f), F32), pltpu.VMEM((tm, f), F32)],
                operands=[n, wg, wu], block_bytes=block_bytes, rider=rider)


def _ffn_bwd_act(dx, wd, a, b, name):
    t, d = dx.shape
    f = wd.shape[0]
    tm, tn, tk = _tile(t, 1024), _tile(f, 1536), _tile(d, 1024)
    nk = d // tk

    def body(dx_ref, wd_ref, a_ref, b_ref, da_ref, db_ref, acc):
        kk = pl.program_id(2)

        def product():
            return lax.dot_general(dx_ref[...], wd_ref[...], (((1,), (1,)), ((), ())), preferred_element_type=F32)

        def finish(r):
            dh = 0.5 * r
            av, bv = a_ref[...].astype(F32), b_ref[...].astype(F32)
            sg = _sigmoid(av)
            da_ref[...] = (dh * bv * (sg * (1.0 + av * (1.0 - sg)))).astype(BF16)
            db_ref[...] = (dh * (av * sg)).astype(BF16)

        if nk == 1:
            finish(product())
            return

        @pl.when(kk == 0)
        def _():
            acc[...] = product()

        if nk > 2:
            @pl.when(jnp.logical_and(kk > 0, kk < nk - 1))
            def _():
                acc[...] += product()

        @pl.when(kk == nk - 1)
        def _():
            finish(acc[...] + product())

    act = pl.BlockSpec((tm, tn), lambda i, j, kk: (i, j))
    out = jax.ShapeDtypeStruct((t, f), BF16)
    block_bytes = _nbytes((tm, tk), BF16) + _nbytes((tn, tk), BF16) + 4 * _nbytes((tm, tn), BF16) + _nbytes((tm, tn), F32)
    return _pallas_call(
        body, out_shape=(out, out), grid=(t // tm, f // tn, nk),
        in_specs=[pl.BlockSpec((tm, tk), lambda i, j, kk: (i, kk)), pl.BlockSpec((tn, tk), lambda i, j, kk: (j, kk)),
                  act, act],
        out_specs=(act, act), scratch_shapes=[pltpu.VMEM((tm, tn), F32)],
        compiler_params=_cparams(("parallel", "parallel", "arbitrary"), block_bytes), name=name)(dx, wd, a, b)


AXIS = dict(BIG)


def _full(wb, n):
    _, _, r, ccols = wb[n].shape
    return wb[n].reshape(N_CHIPS, 2 * r, ccols) if AXIS[n] == 1 else wb[n].reshape(N_CHIPS * 2 * r, ccols)


def _gather(wb, specs):
    items, names = [], []
    for s in specs:
        n, r0, r1 = (s, 0, wb[s].shape[2]) if isinstance(s, str) else s
        items.append((wb[n], r0, r1))
        if n not in names:
            names.append(n)
    return _gather_rider(items), names


def _landed(wb, names, results):
    for n, r in zip(names, results):
        wb[n] = r


def _reduce_first(grads, names, wb, c_idx):
    g4 = [g.reshape(wb[n].shape) for g, n in zip(grads, names)]
    from_sibling = _exchange(_sibling_rider(g4), "rs_sibling_" + names[0])
    return [_sibling_sum(a, b, c_idx, f"rs_sum1_{n}") for a, b, n in zip(g4, from_sibling, names)]


def _ffn_forward(x, gain, wb, tag, up_specs, down_specs):
    n = _rmsnorm_fwd(x, gain, f"{tag}_norm")
    rider, names = _gather(wb, up_specs)
    (a, b, h), got = _ffn_up(n, _full(wb, f"{tag}_w_gate"), _full(wb, f"{tag}_w_up"), f"{tag}_up", rider=rider)
    _landed(wb, names, got)
    down = dict(scale=0.5, residual=x, caps=(1024, 1024, 1536))
    if down_specs:
        rider, names = _gather(wb, down_specs)
        x_next, got = _matmul([(h, _full(wb, f"{tag}_w_down"))], 'nn', F32, f"{tag}_down", rider=rider, **down)
        _landed(wb, names, got)
    else:
        x_next = _matmul([(h, _full(wb, f"{tag}_w_down"))], 'nn', F32, f"{tag}_down", **down)
    return x_next, (n, a, b, h)


def _ffn_backward(x, gain, wb, saved, dx_next, dx_next_b, c_idx, tag, chained, dwd_rider=None):
    n, a, b, h = saved
    wg, wu, wd = (f"{tag}_w_gate", f"{tag}_w_up", f"{tag}_w_down")
    da, db = _ffn_bwd_act(dx_next_b, _full(wb, wd), a, b, f"{tag}_bwd_act")
    res = _matmul([(h, dx_next_b)], 'tn', BF16, f"{tag}_dwd", scale=0.5, caps=(1536, 2048, 1024), rider=dwd_rider)
    g_wd, carried = (res, ()) if dwd_rider is None else res
    grad_mm = dict(out3=N_CHIPS, caps=(2048, 1024, 1024))
    dn_pairs = [(da, _full(wb, wg)), (db, _full(wb, wu))]
    if not chained:
        (p_wd,) = _reduce_first([g_wd], [wd], wb, c_idx)
        g_wg, (r_wd,) = _matmul([(n, da)], 'tn', BF16, f"{tag}_dwg", rider=_scatter_rider([p_wd]), **grad_mm)
        g_wu = _matmul([(n, db)], 'tn', BF16, f"{tag}_dwu", **grad_mm)
        p_wg, p_wu = _reduce_first([g_wg, g_wu], [wg, wu], wb, c_idx)
        dn, (r_wg, r_wu) = _matmul(dn_pairs, 'nt', BF16, f"{tag}_dn", b3=True, rider=_scatter_rider([p_wg, p_wu]))
        done, pending = {wg: (p_wg, r_wg), wu: (p_wu, r_wu), wd: (p_wd, r_wd)}, {}
    else:
        g_wd = g_wd.reshape(wb[wd].shape)
        g_wg, (s_wd,) = _matmul([(n, da)], 'tn', BF16, f"{tag}_dwg", rider=_sibling_rider([g_wd]), **grad_mm)
        p_wd = _sibling_sum(g_wd, s_wd, c_idx, f"rs_sum1_{wd}")
        g_wg = g_wg.reshape(wb[wg].shape)
        g_wu, (r_wd, s_wg) = _matmul([(n, db)], 'tn', BF16, f"{tag}_dwu",
                                     rider=_merge_riders(_scatter_rider([p_wd]), _sibling_rider([g_wg])), **grad_mm)
        p_wg = _sibling_sum(g_wg, s_wg, c_idx, f"rs_sum1_{wg}")
        g_wu = g_wu.reshape(wb[wu].shape)
        dn, (r_wg, s_wu) = _matmul(dn_pairs, 'nt', BF16, f"{tag}_dn", b3=True,
                                   rider=_merge_riders(_scatter_rider([p_wg]), _sibling_rider([g_wu])))
        p_wu = _sibling_sum(g_wu, s_wu, c_idx, f"rs_sum1_{wu}")
        done, pending = {wg: (p_wg, r_wg), wd: (p_wd, r_wd)}, {wu: p_wu}
    dx, dx_b, g_gain = _rmsnorm_bwd(x, gain, dn, dx_next, f"{tag}_norm_bwd")
    return dx, dx_b, g_gain, done, pending, carried


def _rope_tables(seq):
    half = ROPE_DIM // 2
    inv_freq = ROPE_THETA ** (-jnp.arange(0, ROPE_DIM, 2, dtype=F32) / ROPE_DIM)
    ang = jnp.arange(seq).astype(F32)[:, None] * inv_freq[None, :]
    cos, sin = jnp.cos(ang), jnp.sin(ang)
    zeros = lambda w: jnp.zeros((seq, w), F32)
    c = jnp.concatenate([cos, cos, jnp.ones((seq, HEAD_DIM - ROPE_DIM), F32)], axis=1)
    s_up = jnp.concatenate([-sin, zeros(HEAD_DIM - half)], axis=1)
    s_dn = jnp.concatenate([zeros(half), sin, zeros(HEAD_DIM - ROPE_DIM)], axis=1)
    return c, s_up, s_dn


def _rotate(xv, cv, uv, dv):
    half = ROPE_DIM // 2
    return xv * cv + pltpu.roll(xv, HEAD_DIM - half, 1) * uv + pltpu.roll(xv, half, 1) * dv


def _stage(tm):
    return pltpu.VMEM((HEADS_PER_GROUP, tm, HEAD_DIM), F32)


def _to_groups(stage, o_ref, dil):
    rows = stage.shape[1] // dil
    for r in range(dil):
        for h in range(HEADS_PER_GROUP):
            col = r * GROUP_WIDTH + h * HEAD_DIM
            o_ref[:, col:col + HEAD_DIM] = stage[h, pl.ds(r, rows, stride=dil), :].astype(o_ref.dtype)


def _from_groups(g_ref, stage, dil):
    rows = stage.shape[1] // dil
    for r in range(dil):
        for h in range(HEADS_PER_GROUP):
            col = r * GROUP_WIDTH + h * HEAD_DIM
            stage[h, pl.ds(r, rows, stride=dil), :] = g_ref[:, col:col + HEAD_DIM].astype(F32)


def _group_spec(tm, dil):
    return pl.BlockSpec((tm // dil, dil * GROUP_WIDTH), lambda i: (i, 0))


def _group_shape(t, dil, dtype):
    return jax.ShapeDtypeStruct((t // dil, dil * GROUP_WIDTH), dtype)


def _rope_fwd(proj, tables, name):
    t = proj.shape[0]
    tm = _rows(t, 512)
    att_w = N_GROUPS * GROUP_WIDTH
    dilated = [(gi, dil) for gi, dil in enumerate(DILATIONS) if dil > 1]

    def body(x_ref, c_ref, up_ref, dn_ref, qk0_ref, *rest):
        outs, stage = rest[:-1], rest[-1]
        cv, uv, dv = c_ref[...], up_ref[...], dn_ref[...]
        for part in range(2):
            for gi, dil in enumerate(DILATIONS):
                for h in range(HEADS_PER_GROUP):
                    col = part * att_w + gi * GROUP_WIDTH + h * HEAD_DIM
                    y = _rotate(x_ref[:, col:col + HEAD_DIM].astype(F32), cv, uv, dv)
                    if dil == 1:
                        qk0_ref[:, part * GROUP_WIDTH + h * HEAD_DIM:part * GROUP_WIDTH + (h + 1) * HEAD_DIM] = y.astype(BF16)
                    else:
                        stage[h] = y
                if dil > 1:
                    _to_groups(stage, outs[3 * dilated.index((gi, dil)) + part], dil)
        for n, (gi, dil) in enumerate(dilated):
            col = 2 * att_w + gi * GROUP_WIDTH
            for h in range(HEADS_PER_GROUP):
                stage[h] = x_ref[:, col + h * HEAD_DIM:col + (h + 1) * HEAD_DIM].astype(F32)
            _to_groups(stage, outs[3 * n + 2], dil)

    tab = pl.BlockSpec((tm, HEAD_DIM), lambda i: (i, 0))
    out_shape = [jax.ShapeDtypeStruct((t, 2 * GROUP_WIDTH), BF16)]
    out_specs = [pl.BlockSpec((tm, 2 * GROUP_WIDTH), lambda i: (i, 0))]
    for _, dil in dilated:
        out_shape += [_group_shape(t, dil, BF16)] * 3
        out_specs += [_group_spec(tm, dil)] * 3
    res = _pallas_call(
        body, out_shape=out_shape, grid=(t // tm,),
        in_specs=[pl.BlockSpec((tm, 3 * att_w), lambda i: (i, 0)), tab, tab, tab], out_specs=out_specs,
        scratch_shapes=[_stage(tm)],
        compiler_params=_cparams(("parallel",), 4 * _nbytes((tm, 3 * att_w), BF16)), name=name)(proj, *tables)
    return res[0], [tuple(res[1 + 3 * n:4 + 3 * n]) for n in range(len(dilated))]


def _rope_bwd(dq0, dk0, dv0, grouped, tables, name):
    t = dq0.shape[0]
    tm = _rows(t, 512)
    att_w = N_GROUPS * GROUP_WIDTH
    dilated = [(gi, dil) for gi, dil in enumerate(DILATIONS) if dil > 1]
    c, s_up, s_dn = tables

    def body(c_ref, up_ref, dn_ref, dq0_ref, dk0_ref, dv0_ref, *rest):
        g_refs, o_ref, stage = rest[:-2], rest[-2], rest[-1]
        cv, uv, dv = c_ref[...], -up_ref[...], -dn_ref[...]
        for part, first in enumerate((dq0_ref, dk0_ref)):
            for gi, dil in enumerate(DILATIONS):
                if dil > 1:
                    _from_groups(g_refs[3 * dilated.index((gi, dil)) + part], stage, dil)
                for h in range(HEADS_PER_GROUP):
                    sl = slice(h * HEAD_DIM, (h + 1) * HEAD_DIM)
                    xv = first[:, sl].astype(F32) if dil == 1 else stage[h]
                    col = part * att_w + gi * GROUP_WIDTH + h * HEAD_DIM
                    o_ref[:, col:col + HEAD_DIM] = _rotate(xv, cv, uv, dv).astype(BF16)
        for gi, dil in enumerate(DILATIONS):
            col = 2 * att_w + gi * GROUP_WIDTH
            if dil == 1:
                o_ref[:, col:col + GROUP_WIDTH] = dv0_ref[...]
            else:
                _from_groups(g_refs[3 * dilated.index((gi, dil)) + 2], stage, dil)
                for h in range(HEADS_PER_GROUP):
                    o_ref[:, col + h * HEAD_DIM:col + (h + 1) * HEAD_DIM] = stage[h].astype(BF16)

    tab = pl.BlockSpec((tm, HEAD_DIM), lambda i: (i, 0))
    nat = pl.BlockSpec((tm, GROUP_WIDTH), lambda i: (i, 0))
    in_specs, operands = [tab, tab, tab, nat, nat, nat], [c, s_up, s_dn, dq0, dk0, dv0]
    for (_, dil), arrs in zip(dilated, grouped):
        in_specs += [_group_spec(tm, dil)] * 3
        operands += list(arrs)
    return _pallas_call(
        body, out_shape=jax.ShapeDtypeStruct((t, 3 * att_w), BF16), grid=(t // tm,), in_specs=in_specs,
        out_specs=pl.BlockSpec((tm, 3 * att_w), lambda i: (i, 0)), scratch_shapes=[_stage(tm)],
        compiler_params=_cparams(("parallel",), 4 * _nbytes((tm, 3 * att_w), BF16)), name=name)(*operands)


def _regroup(arrs, name):
    t = arrs[0].shape[0]
    tm = _rows(t, 512)
    dilated = [dil for dil in DILATIONS if dil > 1]
    n_in = len(arrs)

    def body(*refs):
        ins, outs, stage = refs[:n_in], refs[n_in:-1], refs[-1]
        for j, x_ref in enumerate(ins):
            for h in range(HEADS_PER_GROUP):
                stage[h] = x_ref[:, h * HEAD_DIM:(h + 1) * HEAD_DIM]
            for n, dil in enumerate(dilated):
                _to_groups(stage, outs[n * n_in + j], dil)

    nat = pl.BlockSpec((tm, GROUP_WIDTH), lambda i: (i, 0))
    res = _pallas_call(
        body, out_shape=[_group_shape(t, dil, F32) for dil in dilated for _ in arrs], grid=(t // tm,),
        in_specs=[nat] * n_in, out_specs=[_group_spec(tm, dil) for dil in dilated for _ in arrs], scratch_shapes=[_stage(tm)],
        compiler_params=_cparams(("parallel",), 3 * n_in * _nbytes((tm, GROUP_WIDTH), F32)), name=name)(*arrs)
    return [tuple(res[n * n_in:(n + 1) * n_in]) for n in range(len(dilated))]


def _query_mask(has_prev):
    qi = lax.broadcasted_iota(jnp.int32, (ATT_BLOCK, 2 * ATT_BLOCK), 0)
    col = lax.broadcasted_iota(jnp.int32, (ATT_BLOCK, 2 * ATT_BLOCK), 1)
    prev = jnp.logical_and(jnp.logical_and(col < ATT_BLOCK, col >= qi), has_prev)
    return jnp.logical_or(prev, jnp.logical_and(col >= ATT_BLOCK, col - ATT_BLOCK <= qi))


def _key_mask(has_next):
    row = lax.broadcasted_iota(jnp.int32, (2 * ATT_BLOCK, ATT_BLOCK), 0)
    kj = lax.broadcasted_iota(jnp.int32, (2 * ATT_BLOCK, ATT_BLOCK), 1)
    nxt = jnp.logical_and(jnp.logical_and(row >= ATT_BLOCK, kj >= row - ATT_BLOCK), has_next)
    return jnp.logical_or(nxt, jnp.logical_and(row < ATT_BLOCK, kj <= row))


def _scores(q, k):
    return lax.dot_general(q, k, (((1,), (1,)), ((), ())), preferred_element_type=F32) * (HEAD_DIM ** -0.5)


def _att_fwd(q, k, v, offs, dil, name):
    qo, ko, vo = offs
    length = q.shape[0]
    nb = length // ATT_BLOCK

    def body(q_ref, kp_ref, kc_ref, vp_ref, vc_ref, o_ref, lse_ref):
        mask = _query_mask(pl.program_id(1) > 0)
        heads = [slice(h * HEAD_DIM, (h + 1) * HEAD_DIM) for h in range(HEADS_PER_GROUP)]
        ks = [jnp.concatenate([kp_ref[:, sl], kc_ref[:, sl]], axis=0) for sl in heads]
        vs = [jnp.concatenate([vp_ref[:, sl], vc_ref[:, sl]], axis=0) for sl in heads]
        ss = [jnp.where(mask, _scores(q_ref[:, sl], kv), MASKED) for sl, kv in zip(heads, ks)]
        ms = [jnp.max(s, axis=1, keepdims=True) for s in ss]
        ps = [jnp.exp(s - m) for s, m in zip(ss, ms)]
        ls = [jnp.sum(p, axis=1, keepdims=True) for p in ps]
        accs = [jnp.dot(p.astype(BF16), vv, preferred_element_type=F32) for p, vv in zip(ps, vs)]
        for sl, acc, m, l in zip(heads, accs, ms, ls):
            o_ref[:, sl] = acc / l
            lse_ref[:, sl] = jnp.broadcast_to(m + jnp.log(l), (ATT_BLOCK, HEAD_DIM))

    def spec(off, prev):
        if prev:
            return pl.BlockSpec((ATT_BLOCK, GROUP_WIDTH), lambda r, n: (jnp.maximum(n - 1, 0), off + r))
        return pl.BlockSpec((ATT_BLOCK, GROUP_WIDTH), lambda r, n: (n, off + r))

    out = jax.ShapeDtypeStruct((length, dil * GROUP_WIDTH), F32)
    o_spec = pl.BlockSpec((ATT_BLOCK, GROUP_WIDTH), lambda r, n: (n, r))
    return _pallas_call(
        body, out_shape=(out, out), grid=(dil, nb),
        in_specs=[spec(qo, False), spec(ko, True), spec(ko, False), spec(vo, True), spec(vo, False)],
        out_specs=(o_spec, o_spec),
        compiler_params=_cparams(("parallel", "parallel"), 8 * _nbytes((ATT_BLOCK, GROUP_WIDTH), F32)), name=name)(q, k, k, v, v)


def _att_combine(outs, lses, name):
    t = outs[0].shape[0] * DILATIONS[0]
    tm = _rows(t, 512)

    def body(*refs):
        o_refs, l_refs = refs[:N_GROUPS], refs[N_GROUPS:2 * N_GROUPS]
        ob_ref, of_ref, lse_ref = refs[2 * N_GROUPS:2 * N_GROUPS + 3]
        stages = list(refs[2 * N_GROUPS + 3:])
        staged = []
        for o_ref, l_ref, dil in zip(o_refs, l_refs, DILATIONS):
            if dil > 1:
                so, sl = stages.pop(), stages.pop()
                _from_groups(o_ref, so, dil)
                _from_groups(l_ref, sl, dil)
                staged.append((so, sl))
            else:
                staged.append(None)
        for h in range(HEADS_PER_GROUP):
            hs = slice(h * HEAD_DIM, (h + 1) * HEAD_DIM)
            os_ = [o_ref[:, hs] if st is None else st[0][h] for o_ref, st in zip(o_refs, staged)]
            ls = [l_ref[:, hs] if st is None else st[1][h] for l_ref, st in zip(l_refs, staged)]
            m = functools.reduce(jnp.maximum, ls)
            ws = [jnp.exp(l - m) for l in ls]
            den = functools.reduce(jnp.add, ws)
            num = functools.reduce(jnp.add, [w * o for w, o in zip(ws, os_)])
            o = num / den
            ob_ref[:, hs] = o.astype(BF16)
            of_ref[:, hs] = o
            lse_ref[:, hs] = m + jnp.log(den)

    blk = pl.BlockSpec((tm, GROUP_WIDTH), lambda i: (i, 0))
    specs = [blk if dil == 1 else _group_spec(tm, dil) for dil in DILATIONS]
    f32 = jax.ShapeDtypeStruct((t, GROUP_WIDTH), F32)
    n_stage = 2 * sum(dil > 1 for dil in DILATIONS)
    return _pallas_call(
        body, out_shape=(jax.ShapeDtypeStruct((t, GROUP_WIDTH), BF16), f32, f32), grid=(t // tm,),
        in_specs=specs * 2, out_specs=(blk, blk, blk), scratch_shapes=[_stage(tm)] * n_stage,
        compiler_params=_cparams(("parallel",), 13 * _nbytes((tm, GROUP_WIDTH), F32)), name=name)(*outs, *lses)


def _att_delta(do, o, name):
    t = o.shape[0]
    tm = _rows(t, 512)

    def body(do_ref, o_ref, d_ref):
        for h in range(HEADS_PER_GROUP):
            sl = slice(h * HEAD_DIM, (h + 1) * HEAD_DIM)
            s = jnp.sum(do_ref[:, sl] * o_ref[:, sl], axis=1, keepdims=True)
            d_ref[:, sl] = jnp.broadcast_to(s, (tm, HEAD_DIM))

    blk = pl.BlockSpec((tm, GROUP_WIDTH), lambda i: (i, 0))
    return _pallas_call(
        body, out_shape=jax.ShapeDtypeStruct((t, GROUP_WIDTH), F32), grid=(t // tm,), in_specs=[blk, blk], out_specs=blk,
        compiler_params=_cparams(("parallel",), 3 * _nbytes((tm, GROUP_WIDTH), F32)), name=name)(do, o)


def _att_bwd_dq(q, k, v, do, lse, delta, offs, dil, name):
    qo, ko, vo = offs
    length = q.shape[0]
    nb = length // ATT_BLOCK
    scale = HEAD_DIM ** -0.5

    def body(q_ref, kp_ref, kc_ref, vp_ref, vc_ref, do_ref, lse_ref, dl_ref, dq_ref):
        mask = _query_mask(pl.program_id(1) > 0)
        heads = [slice(h * HEAD_DIM, (h + 1) * HEAD_DIM) for h in range(HEADS_PER_GROUP)]
        wide = lambda ref, sl: jnp.concatenate([ref[:, sl], ref[:, sl]], axis=1)
        ks = [jnp.concatenate([kp_ref[:, sl], kc_ref[:, sl]], axis=0) for sl in heads]
        vs = [jnp.concatenate([vp_ref[:, sl], vc_ref[:, sl]], axis=0) for sl in heads]
        ps = [jnp.exp(jnp.where(mask, _scores(q_ref[:, sl], kv), MASKED) - wide(lse_ref, sl)) for sl, kv in zip(heads, ks)]
        dps = [lax.dot_general(do_ref[:, sl].astype(BF16), vv, (((1,), (1,)), ((), ())), preferred_element_type=F32)
               for sl, vv in zip(heads, vs)]
        dss = [(p * (dp - wide(dl_ref, sl)) * scale).astype(BF16) for sl, p, dp in zip(heads, ps, dps)]
        dqs = [jnp.dot(ds, kv, preferred_element_type=F32) for ds, kv in zip(dss, ks)]
        for sl, dq in zip(heads, dqs):
            dq_ref[:, sl] = dq.astype(BF16)

    def spec(off, prev):
        if prev:
            return pl.BlockSpec((ATT_BLOCK, GROUP_WIDTH), lambda r, n: (jnp.maximum(n - 1, 0), off + r))
        return pl.BlockSpec((ATT_BLOCK, GROUP_WIDTH), lambda r, n: (n, off + r))

    own = pl.BlockSpec((ATT_BLOCK, GROUP_WIDTH), lambda r, n: (n, r))
    return _pallas_call(
        body, out_shape=jax.ShapeDtypeStruct((length, dil * GROUP_WIDTH), BF16), grid=(dil, nb),
        in_specs=[spec(qo, False), spec(ko, True), spec(ko, False), spec(vo, True), spec(vo, False), own, own, own],
        out_specs=own,
        compiler_params=_cparams(("parallel", "parallel"), 10 * _nbytes((ATT_BLOCK, GROUP_WIDTH), F32)),
        name=name)(q, k, k, v, v, do, lse, delta)


def _att_bwd_dkv(q, k, v, do, lse, delta, offs, dil, name):
    qo, ko, vo = offs
    length = q.shape[0]
    nb = length // ATT_BLOCK
    scale = HEAD_DIM ** -0.5

    def body(k_ref, v_ref, qc_ref, qn_ref, doc_ref, don_ref, lsec_ref, lsen_ref, dlc_ref, dln_ref, dk_ref, dv_ref):
        mask = _key_mask(pl.program_id(1) < nb - 1)
        heads = [slice(h * HEAD_DIM, (h + 1) * HEAD_DIM) for h in range(HEADS_PER_GROUP)]
        both = lambda cur, nxt, sl: jnp.concatenate([cur[:, sl], nxt[:, sl]], axis=0)
        qs = [both(qc_ref, qn_ref, sl) for sl in heads]
        dos = [both(doc_ref, don_ref, sl).astype(BF16) for sl in heads]
        ps = [jnp.exp(jnp.where(mask, _scores(qv, k_ref[:, sl]), MASKED) - both(lsec_ref, lsen_ref, sl)) for sl, qv in zip(heads, qs)]
        dps = [lax.dot_general(dov, v_ref[:, sl], (((1,), (1,)), ((), ())), preferred_element_type=F32) for sl, dov in zip(heads, dos)]
        dss = [(p * (dp - both(dlc_ref, dln_ref, sl)) * scale).astype(BF16) for sl, p, dp in zip(heads, ps, dps)]
        dvs = [lax.dot_general(p.astype(BF16), dov, (((0,), (0,)), ((), ())), preferred_element_type=F32) for p, dov in zip(ps, dos)]
        dks = [lax.dot_general(ds, qv, (((0,), (0,)), ((), ())), preferred_element_type=F32) for ds, qv in zip(dss, qs)]
        for sl, dk, dv in zip(heads, dks, dvs):
            dk_ref[:, sl] = dk.astype(BF16)
            dv_ref[:, sl] = dv.astype(BF16)

    def spec(off, nxt):
        if nxt:
            return pl.BlockSpec((ATT_BLOCK, GROUP_WIDTH), lambda r, n: (jnp.minimum(n + 1, nb - 1), off + r))
        return pl.BlockSpec((ATT_BLOCK, GROUP_WIDTH), lambda r, n: (n, off + r))

    own = pl.BlockSpec((ATT_BLOCK, GROUP_WIDTH), lambda r, n: (n, r))
    out = jax.ShapeDtypeStruct((length, dil * GROUP_WIDTH), BF16)
    return _pallas_call(
        body, out_shape=(out, out), grid=(dil, nb),
        in_specs=[spec(ko, False), spec(vo, False), spec(qo, False), spec(qo, True), spec(0, False), spec(0, True),
                  spec(0, False), spec(0, True), spec(0, False), spec(0, True)],
        out_specs=(own, own),
        compiler_params=_cparams(("parallel", "parallel"), 12 * _nbytes((ATT_BLOCK, GROUP_WIDTH), F32)),
        name=name)(k, v, q, q, do, do, lse, lse, delta, delta)


def _gelu(x):
    return 0.5 * x * (1.0 + lax.erf(x * (2.0 ** -0.5)))


def _gelu_grad(x):
    return 0.5 * (1.0 + lax.erf(x * (2.0 ** -0.5))) + x * jnp.exp(-0.5 * x * x) * ((2.0 * jnp.pi) ** -0.5)


def _sg_normed(vs, lg, lb):
    gv = _gelu(vs)
    mu = jnp.mean(gv, axis=1, keepdims=True)
    xc = gv - mu
    rstd = lax.rsqrt(jnp.mean(xc * xc, axis=1, keepdims=True) + LN_EPS)
    z = xc * rstd
    return z, rstd, z * lg + lb


def _sg_tril():
    row = lax.broadcasted_iota(jnp.int32, (SG_CHUNK, SG_CHUNK), 0)
    col = lax.broadcasted_iota(jnp.int32, (SG_CHUNK, SG_CHUNK), 1)
    return row >= col


def _sg_fwd(proj, u_blk, vs_blk, lg, lb, sg_w, bias, name):
    t = proj.shape[0]
    width = SG_GROUPS * SG_GROUP_DIM

    def body(u_ref, vs_ref, lg_ref, lb_ref, w_ref, bias_ref, o_ref):
        _, _, vn = _sg_normed(vs_ref[...].astype(F32), lg_ref[...], lb_ref[...])
        vn = vn.astype(BF16)
        tril = _sg_tril()
        for g in range(SG_GROUPS):
            sl = slice(g * SG_GROUP_DIM, (g + 1) * SG_GROUP_DIM)
            w = jnp.where(tril, w_ref[g], 0.0).astype(BF16)
            sp = jnp.dot(w, vn[:, sl], preferred_element_type=F32) + bias_ref[:, sl]
            o_ref[:, sl] = (_gelu(u_ref[:, sl].astype(F32)) * sp).astype(BF16)

    vec = pl.BlockSpec((1, width), lambda i: (0, 0))
    return _pallas_call(
        body, out_shape=jax.ShapeDtypeStruct((t, width), BF16), grid=(t // SG_CHUNK,),
        in_specs=[pl.BlockSpec((SG_CHUNK, width), lambda i: (i, u_blk)), pl.BlockSpec((SG_CHUNK, width), lambda i: (i, vs_blk)),
                  vec, vec, pl.BlockSpec((SG_GROUPS, SG_CHUNK, SG_CHUNK), lambda i: (0, 0, 0)),
                  pl.BlockSpec((SG_CHUNK, width), lambda i: (0, 0))],
        out_specs=pl.BlockSpec((SG_CHUNK, width), lambda i: (i, 0)),
        compiler_params=_cparams(("parallel",), 8 * _nbytes((SG_CHUNK, width), F32)), name=name)(proj, proj, lg, lb, sg_w, bias)


def _sg_bwd(proj, u_blk, vs_blk, dsu, lg, lb, sg_w, bias, name):
    t = proj.shape[0]
    width = SG_GROUPS * SG_GROUP_DIM

    def body(u_ref, vs_ref, dsu_ref, lg_ref, lb_ref, w_ref, bias_ref, du_ref, dvs_ref, dw_ref, dbias_ref, dlg_ref, dlb_ref):
        @pl.when(pl.program_id(0) == 0)
        def _():
            dw_ref[...] = jnp.zeros_like(dw_ref)
            dbias_ref[...] = jnp.zeros_like(dbias_ref)
            dlg_ref[...] = jnp.zeros_like(dlg_ref)
            dlb_ref[...] = jnp.zeros_like(dlb_ref)

        vs = vs_ref[...].astype(F32)
        z, rstd, vn = _sg_normed(vs, lg_ref[...], lb_ref[...])
        vn = vn.astype(BF16)
        tril = _sg_tril()
        dvn = []
        for g in range(SG_GROUPS):
            sl = slice(g * SG_GROUP_DIM, (g + 1) * SG_GROUP_DIM)
            w = jnp.where(tril, w_ref[g], 0.0).astype(BF16)
            vg = vn[:, sl]
            sp = jnp.dot(w, vg, preferred_element_type=F32) + bias_ref[:, sl]
            uv = u_ref[:, sl].astype(F32)
            dsu_g = dsu_ref[:, sl].astype(F32)
            du_ref[:, sl] = (dsu_g * sp * _gelu_grad(uv)).astype(BF16)
            dsp = dsu_g * _gelu(uv)
            dsp_b = dsp.astype(BF16)
            dw = lax.dot_general(dsp_b, vg, (((1,), (1,)), ((), ())), preferred_element_type=F32)
            dw_ref[g] += jnp.where(tril, dw, 0.0)
            dbias_ref[:, sl] += jnp.broadcast_to(jnp.sum(dsp, axis=1, keepdims=True), (SG_CHUNK, SG_GROUP_DIM))
            dvn.append(lax.dot_general(w, dsp_b, (((0,), (0,)), ((), ())), preferred_element_type=F32))
        dvn = jnp.concatenate(dvn, axis=1)
        dlg_ref[...] += jnp.sum(dvn * z, axis=0, keepdims=True)
        dlb_ref[...] += jnp.sum(dvn, axis=0, keepdims=True)
        dz = dvn * lg_ref[...]
        dgv = rstd * (dz - jnp.mean(dz, axis=1, keepdims=True) - z * jnp.mean(dz * z, axis=1, keepdims=True))
        dvs_ref[...] = (dgv * _gelu_grad(vs)).astype(BF16)

    vec = pl.BlockSpec((1, width), lambda i: (0, 0))
    row = pl.BlockSpec((SG_CHUNK, width), lambda i: (i, 0))
    fixed = pl.BlockSpec((SG_CHUNK, width), lambda i: (0, 0))
    w_spec = pl.BlockSpec((SG_GROUPS, SG_CHUNK, SG_CHUNK), lambda i: (0, 0, 0))
    act = jax.ShapeDtypeStruct((t, width), BF16)
    return _pallas_call(
        body,
        out_shape=(act, act, jax.ShapeDtypeStruct((SG_GROUPS, SG_CHUNK, SG_CHUNK), F32),
                   jax.ShapeDtypeStruct((SG_CHUNK, width), F32), jax.ShapeDtypeStruct((1, width), F32),
                   jax.ShapeDtypeStruct((1, width), F32)),
        grid=(t // SG_CHUNK,),
        in_specs=[pl.BlockSpec((SG_CHUNK, width), lambda i: (i, u_blk)), pl.BlockSpec((SG_CHUNK, width), lambda i: (i, vs_blk)),
                  row, vec, vec, w_spec, fixed],
        out_specs=(row, row, w_spec, fixed, vec, vec),
        compiler_params=_cparams(("arbitrary",), 14 * _nbytes((SG_CHUNK, width), F32)),
        name=name)(proj, proj, dsu, lg, lb, sg_w, bias)


def _gate_fwd(proj, ga_blk, gs_blk, y_att, y_sg, name):
    t, d = y_att.shape
    tm, tn = _rows(t, 512), _tile(d, GROUP_WIDTH)

    def body(ga_ref, gs_ref, ya_ref, ys_ref, o_ref):
        o_ref[...] = (_sigmoid(ga_ref[...].astype(F32)) * ya_ref[...].astype(F32)
                      + _sigmoid(gs_ref[...].astype(F32)) * ys_ref[...].astype(F32)).astype(BF16)

    own = pl.BlockSpec((tm, tn), lambda i, j: (i, j))
    return _pallas_call(
        body, out_shape=jax.ShapeDtypeStruct((t, d), BF16), grid=(t // tm, d // tn),
        in_specs=[pl.BlockSpec((tm, tn), lambda i, j: (i, ga_blk + j)), pl.BlockSpec((tm, tn), lambda i, j: (i, gs_blk + j)),
                  own, own],
        out_specs=own, compiler_params=_cparams(("parallel", "parallel"), 6 * _nbytes((tm, tn), F32)),
        name=name)(proj, proj, y_att, y_sg)


def _gate_bwd(proj, ga_blk, gs_blk, y_att, y_sg, dmerged, name):
    t, d = y_att.shape
    tm, tn = _rows(t, 512), _tile(d, GROUP_WIDTH)

    def body(ga_ref, gs_ref, ya_ref, ys_ref, dm_ref, dya_ref, dys_ref, dga_ref, dgs_ref):
        dm = dm_ref[...].astype(F32)
        for g_ref, y_ref, dy_ref, dg_ref in ((ga_ref, ya_ref, dya_ref, dga_ref), (gs_ref, ys_ref, dys_ref, dgs_ref)):
            sg = _sigmoid(g_ref[...].astype(F32))
            dy_ref[...] = (dm * sg).astype(BF16)
            dg_ref[...] = (dm * y_ref[...].astype(F32) * sg * (1.0 - sg)).astype(BF16)

    own = pl.BlockSpec((tm, tn), lambda i, j: (i, j))
    out = jax.ShapeDtypeStruct((t, d), BF16)
    return _pallas_call(
        body, out_shape=(out, out, out, out), grid=(t // tm, d // tn),
        in_specs=[pl.BlockSpec((tm, tn), lambda i, j: (i, ga_blk + j)), pl.BlockSpec((tm, tn), lambda i, j: (i, gs_blk + j)),
                  own, own, own],
        out_specs=(own, own, own, own), compiler_params=_cparams(("parallel", "parallel"), 10 * _nbytes((tm, tn), F32)),
        name=name)(proj, proj, y_att, y_sg, dmerged)


def _mixer_forward(x, wb, small, in_specs, sg_specs, out_specs):
    t, d = x.shape
    att_w = N_GROUPS * GROUP_WIDTH
    sg_w = SG_GROUPS * SG_GROUP_DIM
    n = _rmsnorm_fwd(x, small['mix_norm'], "mix_norm")
    rider, names = _gather(wb, in_specs)
    proj, got = _matmul([(n, _full(wb, 'w_in'))], 'nn', BF16, "mix_in", b3=True, caps=(1024, 1024, 1024), rider=rider)
    _landed(wb, names, got)
    tables = _rope_tables(t)
    qk0, grouped = _rope_fwd(proj, tables, "mix_rope")
    qkv = [(qk0, qk0, proj, (0, 1, 2 * N_GROUPS))] + [g + ((0, 0, 0),) for g in grouped]
    outs, lses = zip(*[_att_fwd(*args, dil, f"att_fwd{gi}") for gi, (args, dil) in enumerate(zip(qkv, DILATIONS))])
    o_b, o_f, lse = _att_combine(outs, lses, "att_combine")
    y_att = _matmul([(o_b, _full(wb, 'w_att_out'))], 'nn', BF16, "mix_att_out", b3=True)
    bias = jnp.repeat(small['sg_b'].T, SG_GROUP_DIM, axis=1)
    u_blk, vs_blk = 3 * att_w // sg_w, 3 * att_w // sg_w + 1
    su = _sg_fwd(proj, u_blk, vs_blk, small['sg_ln_g'], small['sg_ln_b'], small['sg_w'], bias, "sg_fwd")
    rider, names = _gather(wb, sg_specs)
    y_sg, got = _matmul([(su, _full(wb, 'w_sg_out'))], 'nn', BF16, "mix_sg_out", b3=True, rider=rider)
    _landed(wb, names, got)
    ga_blk = (3 * att_w + 2 * sg_w) // _tile(d, GROUP_WIDTH)
    gs_blk = ga_blk + d // _tile(d, GROUP_WIDTH)
    merged = _gate_fwd(proj, ga_blk, gs_blk, y_att, y_sg, "gate_fwd")
    rider, names = _gather(wb, out_specs)
    x_next, got = _matmul([(merged, _full(wb, 'w_out'))], 'nn', F32, "mix_out", residual=x, rider=rider)
    _landed(wb, names, got)
    saved = (n, proj, qkv, tables, o_b, o_f, lse, y_att, su, y_sg, merged, bias, (u_blk, vs_blk, ga_blk, gs_blk))
    return x_next, saved


def _mixer_backward(x, wb, small, saved, dx_next, dx_next_b, c_idx, first_rider, pending):
    n, proj, qkv, tables, o_b, o_f, lse, y_att, su, y_sg, merged, bias, (u_blk, vs_blk, ga_blk, gs_blk) = saved
    s = N_CHIPS
    dmerged, carried = _matmul([(dx_next_b, _full(wb, 'w_out'))], 'nt', BF16, "mix_out_dx", rider=first_rider)
    g_w_out = _matmul([(merged, dx_next_b)], 'tn', BF16, "mix_out_dw", caps=(1024, 1024, 1024))
    dy_att, dy_sg, dg_att, dg_sg = _gate_bwd(proj, ga_blk, gs_blk, y_att, y_sg, dmerged, "gate_bwd")

    g_w_att_out = _matmul([(o_b, dy_att)], 'tn', BF16, "mix_att_out_dw", out3=s)
    do = _matmul([(dy_att, _full(wb, 'w_att_out'))], 'nt', F32, "mix_att_out_dx", b3=True)
    delta = _att_delta(do, o_f, "att_delta")
    stats = [(do, lse, delta)] + _regroup([do, lse, delta], "att_regroup")
    dqkv = []
    for gi, ((q, k, v, offs), st, dil) in enumerate(zip(qkv, stats, DILATIONS)):
        dq = _att_bwd_dq(q, k, v, *st, offs, dil, f"att_bwd_dq{gi}")
        dk, dv = _att_bwd_dkv(q, k, v, *st, offs, dil, f"att_bwd_dkv{gi}")
        dqkv.append((dq, dk, dv))
    dqkv = _rope_bwd(*dqkv[0], dqkv[1:], tables, "mix_rope_bwd")

    g_w_sg_out = _matmul([(su, dy_sg)], 'tn', BF16, "mix_sg_out_dw", out3=s)
    out_names = ['w_out', 'w_att_out', 'w_sg_out']
    out_g4 = [a.reshape(wb[nm].shape) for a, nm in zip([g_w_out, g_w_att_out, g_w_sg_out], out_names)]
    dsu, from_sibling = _matmul([(dy_sg, _full(wb, 'w_sg_out'))], 'nt', BF16, "mix_sg_out_dx", b3=True, rider=_sibling_rider(out_g4))
    out_parts = [_sibling_sum(a, b, c_idx, f"rs_sum1_{nm}") for a, b, nm in zip(out_g4, from_sibling, out_names)]
    out_names, out_parts = out_names + list(pending), out_parts + list(pending.values())
    du, dvs, g_sg_w, g_bias, g_lg, g_lb = _sg_bwd(proj, u_blk, vs_blk, dsu, small['sg_ln_g'], small['sg_ln_b'],
                                                   small['sg_w'], bias, "sg_bwd")
    gs = {'sg_w': g_sg_w, 'sg_b': g_bias[:, ::SG_GROUP_DIM].T, 'sg_ln_g': g_lg, 'sg_ln_b': g_lb}

    dproj = jnp.concatenate([dqkv, du, dvs, dg_att, dg_sg], axis=1)
    g_w_in, out_recv = _matmul([(n, dproj)], 'tn', BF16, "mix_in_dw", out3=s, caps=(1024, 1024, 1024),
                               rider=_scatter_rider(out_parts))
    (p_w_in,) = _reduce_first([g_w_in], ['w_in'], wb, c_idx)
    dn, (r_w_in,) = _matmul([(dproj, _full(wb, 'w_in'))], 'nt', BF16, "mix_in_dx", b3=True, caps=(1024, 1024, 512),
                            rider=_scatter_rider([p_w_in]))
    dx, dx_b, gs['mix_norm'] = _rmsnorm_bwd(x, small['mix_norm'], dn, dx_next, "mix_norm_bwd")
    g = {nm: (p, r) for nm, p, r in zip(out_names, out_parts, out_recv)}
    g['w_in'] = (p_w_in, r_w_in)
    return dx, dx_b, g, gs, carried


def _step(x, target, wb, small, c_idx, pc_idx):
    def last_stage(g):
        names = list(g)
        return names, _halves_rider([_chip_sum(*g[n], pc_idx, f"rs_sum2_{n}") for n in names])

    wb = dict(wb)
    rider, names = _gather(wb, ['ffn1_w_gate', 'ffn1_w_up'])
    _landed(wb, names, _exchange(rider, "gather_first"))
    half_in = wb['w_in'].shape[2] // 2
    x1, s1 = _ffn_forward(x, small['ffn1_norm'], wb, "ffn1", ['ffn1_w_down', ('w_in', 0, half_in)], [('w_in', half_in, 2 * half_in)])
    up_rows = wb['ffn2_w_up'].shape[2]
    up_cut = up_rows // 32 * 15
    x2, s2 = _mixer_forward(x1, wb, small, ['w_att_out', 'w_sg_out', 'w_out', 'ffn2_w_gate'],
                            [('ffn2_w_up', 0, up_cut)], [('ffn2_w_up', up_cut, up_rows)])
    x3, s3 = _ffn_forward(x2, small['ffn2_norm'], wb, "ffn2", ['ffn2_w_down'], None)
    loss, dx3, dx3_b, g_final = _final_loss(x3, small['final_norm'], target, "final_loss")
    gs = {'final_norm': g_final}
    whole = {}
    dx2, dx2_b, gs['ffn2_norm'], g, pending, _ = _ffn_backward(x2, small['ffn2_norm'], wb, s3, dx3, dx3_b, c_idx, "ffn2", True)
    names, rider = last_stage(g)
    dx1, dx1_b, g, gs_mix, got = _mixer_backward(x1, wb, small, s2, dx2, dx2_b, c_idx, rider, pending)
    whole.update(zip(names, got))
    gs.update(gs_mix)
    names, rider = last_stage(g)
    dx0, _, gs['ffn1_norm'], g, _, got = _ffn_backward(x, small['ffn1_norm'], wb, s1, dx1, dx1_b, c_idx, "ffn1", False,
                                                       dwd_rider=rider)
    whole.update(zip(names, got))
    names, rider = last_stage(g)
    whole.update(zip(names, _exchange(rider, "rs_halves")))
    return loss, dx0, whole, gs


def _cast_into_gathered(wt, p_idx, name):
    r, ccols = wt.shape[0] // 2, wt.shape[1]
    tm = _rows(r, 512)
    nb = r // tm

    def body(p_ref, w_ref, o_ref):
        o_ref[...] = w_ref[...].astype(BF16)

    grid_spec = pltpu.PrefetchScalarGridSpec(
        num_scalar_prefetch=1, grid=(2, nb),
        in_specs=[pl.BlockSpec((tm, ccols), lambda h, i, pr: (h * nb + i, 0))],
        out_specs=pl.BlockSpec((None, None, tm, ccols), lambda h, i, pr: (pr[0], h, i, 0)))
    return pl.pallas_call(body, out_shape=jax.ShapeDtypeStruct((N_CHIPS, 2, r, ccols), BF16), grid_spec=grid_spec,
                          compiler_params=_cparams(("parallel", "parallel"), 2 * _nbytes((tm, ccols), F32)), name=name)(p_idx, wt)


def _sibling_rider(grads):
    n = len(grads)

    def copy(src, dst, sems, i):
        x, y, c, _ = _place()
        return pltpu.make_async_remote_copy(src[i].at[:, 1 - c], dst[i], sems[0].at[i], sems[1].at[i],
                                            device_id=(x, y, 1 - c), device_id_type=MESH)

    def start(src, dst, sems):
        for i in range(n):
            copy(src, dst, sems, i).start()

    def finish(src, dst, sems):
        for i in range(n):
            copy(src, dst, sems, i).wait()

    return _Rider(grads, [jax.ShapeDtypeStruct((g.shape[0],) + g.shape[2:], g.dtype) for g in grads], {},
                  [pltpu.SemaphoreType.DMA((n,))] * 2, start, finish)


def _merge_riders(a, b):
    n_in, n_out, n_sem = len(a.operands), len(a.out_shapes), len(a.sems)

    def both(which):
        def run(ins, outs, sems):
            getattr(a, which)(ins[:n_in], outs[:n_out], sems[:n_sem])
            getattr(b, which)(ins[n_in:], outs[n_out:], sems[n_sem:])
        return run

    aliases = dict(a.aliases)
    aliases.update({n_in + k: n_out + v for k, v in b.aliases.items()})
    return _Rider(list(a.operands) + list(b.operands), list(a.out_shapes) + list(b.out_shapes), aliases,
                  list(a.sems) + list(b.sems), both('start'), both('finish'))


def _halves_rider(bufs):
    n = len(bufs)

    def copy(ref, sems, i, c, x, y):
        return pltpu.make_async_remote_copy(ref, ref, sems[0].at[i], sems[1].at[i], device_id=(x, y, 1 - c), device_id_type=MESH)

    def start(_, buf, sems):
        x, y, c, _ = _place()
        for i in range(n):
            copy(buf[i].at[c], sems, i, c, x, y).start()

    def finish(_, buf, sems):
        x, y, c, _ = _place()
        for i in range(n):
            copy(buf[i].at[c], sems, i, c, x, y).wait_send()
            copy(buf[i].at[1 - c], sems, i, c, x, y).wait_recv()

    return _Rider(bufs, [jax.ShapeDtypeStruct(b.shape, b.dtype) for b in bufs], {i: i for i in range(n)},
                  [pltpu.SemaphoreType.DMA((n,))] * 2, start, finish)


def _sibling_sum(grad, recv, c_idx, name):
    s, _, r, ccols = grad.shape
    tm = _rows(r, 512)

    def body(c_ref, g_ref, r_ref, o_ref):
        o_ref[...] = (g_ref[...].astype(F32) + r_ref[...].astype(F32)).astype(BF16)

    grid_spec = pltpu.PrefetchScalarGridSpec(
        num_scalar_prefetch=1, grid=(s, r // tm),
        in_specs=[pl.BlockSpec((None, None, tm, ccols), lambda q, i, cr: (q, cr[0], i, 0)),
                  pl.BlockSpec((None, tm, ccols), lambda q, i, cr: (q, i, 0))],
        out_specs=pl.BlockSpec((None, tm, ccols), lambda q, i, cr: (q, i, 0)))
    return pl.pallas_call(body, out_shape=jax.ShapeDtypeStruct((s, r, ccols), BF16), grid_spec=grid_spec,
                          compiler_params=_cparams(("parallel", "parallel"), 4 * _nbytes((tm, ccols), F32)), name=name)(c_idx, grad, recv)


def _chip_sum(part, recv, pc_idx, name):
    _, r, ccols = part.shape
    tm = _rows(r, 512)

    def body(pc_ref, own_ref, r0_ref, r1_ref, r2_ref, o_ref):
        acc = own_ref[...].astype(F32) + r0_ref[...].astype(F32)
        acc = acc + r1_ref[...].astype(F32)
        o_ref[...] = acc + r2_ref[...].astype(F32)

    def slot(j):
        return pl.BlockSpec((None, tm, ccols), lambda i, pc: (j, i, 0))

    grid_spec = pltpu.PrefetchScalarGridSpec(
        num_scalar_prefetch=1, grid=(r // tm,),
        in_specs=[pl.BlockSpec((None, tm, ccols), lambda i, pc: (pc[0], i, 0)), slot(0), slot(1), slot(2)],
        out_specs=pl.BlockSpec((None, tm, ccols), lambda i, pc: (pc[1], i, 0)))
    return pl.pallas_call(body, out_shape=jax.ShapeDtypeStruct((2, r, ccols), F32), grid_spec=grid_spec,
                          compiler_params=_cparams(("parallel",), 6 * _nbytes((tm, ccols), F32)), name=name)(pc_idx, part, recv, recv, recv)


def _all_reduce_small(vec):
    _, r, _ = vec.shape

    def body(v_ref, o_ref, parts, send1, recv1, send2, recv2):
        x, y, c, _ = _place()
        me = 4 * x + 2 * y + c
        peers = []
        for k in range(1, N_DEV):
            px, py, pc = (1 - x if k & 4 else x, 1 - y if k & 2 else y, 1 - c if k & 1 else c)
            peers.append(((px, py, pc), 4 * px + 2 * py + pc))
        parts[me] = v_ref[me]
        cps = []
        for k, (peer, peer_id) in enumerate(peers):
            cp = pltpu.make_async_remote_copy(v_ref.at[peer_id], parts.at[me], send1.at[k], recv1.at[k],
                                              device_id=peer, device_id_type=MESH)
            cp.start()
            cps.append(cp)
        for cp in cps:
            cp.wait()
        acc = parts[0]
        for dev in range(1, N_DEV):
            acc = acc + parts[dev]
        o_ref[me] = acc
        cps = []
        for k, (peer, _) in enumerate(peers):
            cp = pltpu.make_async_remote_copy(o_ref.at[me], o_ref.at[me], send2.at[k], recv2.at[k],
                                              device_id=peer, device_id_type=MESH)
            cp.start()
            cps.append(cp)
        for cp in cps:
            cp.wait()

    vm = pl.BlockSpec(memory_space=pltpu.VMEM)
    sems = pltpu.SemaphoreType.DMA((N_DEV - 1,))
    return pl.pallas_call(
        body, out_shape=jax.ShapeDtypeStruct(vec.shape, F32), in_specs=[vm], out_specs=vm,
        scratch_shapes=[pltpu.VMEM((N_DEV, r, LANES), F32), sems, sems, sems, sems],
        compiler_params=pltpu.CompilerParams(vmem_limit_bytes=int(8 * _nbytes((N_DEV, r, LANES), F32))),
        name="all_reduce_small")(vec)


def _adamw(wt, g, m, v, name, rider=None):
    r, ccols = wt.shape
    tm = _rows(r, max(8, (MIB // (4 * ccols)) // 8 * 8))
    blk = pl.BlockSpec((tm, ccols), lambda i: (i, 0))

    def body(w_ref, g_ref, m_ref, v_ref, go_ref, d_ref, mo_ref, vo_ref):
        gv = g_ref[...]
        go_ref[...] = gv
        mv = ADAM_B1 * m_ref[...] + (1.0 - ADAM_B1) * gv
        vv = ADAM_B2 * v_ref[...] + (1.0 - ADAM_B2) * (gv * gv)
        m_hat = mv / (1.0 - ADAM_B1 ** ADAM_STEP)
        v_hat = vv / (1.0 - ADAM_B2 ** ADAM_STEP)
        d_ref[...] = -ADAM_LR * (m_hat / (jnp.sqrt(v_hat) + ADAM_EPS) + ADAM_WD * w_ref[...])
        mo_ref[...] = mv
        vo_ref[...] = vv

    out = jax.ShapeDtypeStruct((r, ccols), F32)
    return _run(body, name=name, grid=(r // tm,), in_specs=[blk] * 4, out_specs=[blk] * 4, out_shape=[out] * 4, scratch_shapes=[],
                operands=[wt, g, m, v], block_bytes=8 * _nbytes((tm, ccols), F32), rider=rider, pinned=False)


def _as_rows(a):
    rows = a.reshape(-1, LANES)
    return jnp.pad(rows, ((0, -rows.shape[0] % 8), (0, 0)))


def kernel(x, ffn1_norm, ffn1_w_gate, ffn1_w_up, ffn1_w_down, mix_norm, w_in, sg_ln_g, sg_ln_b, sg_w, sg_b, w_att_out, w_sg_out, w_out, ffn2_norm, ffn2_w_gate, ffn2_w_up, ffn2_w_down, final_norm, loss_target, m_ffn1_norm, m_ffn1_w_gate, m_ffn1_w_up, m_ffn1_w_down, m_mix_norm, m_w_in, m_sg_ln_g, m_sg_ln_b, m_sg_w, m_sg_b, m_w_att_out, m_w_sg_out, m_w_out, m_ffn2_norm, m_ffn2_w_gate, m_ffn2_w_up, m_ffn2_w_down, m_final_norm, v_ffn1_norm, v_ffn1_w_gate, v_ffn1_w_up, v_ffn1_w_down, v_mix_norm, v_w_in, v_sg_ln_g, v_sg_ln_b, v_sg_w, v_sg_b, v_w_att_out, v_w_sg_out, v_w_out, v_ffn2_norm, v_ffn2_w_gate, v_ffn2_w_up, v_ffn2_w_down, v_final_norm):
    given = dict(locals())
    wts = {n: given[n] for n in WEIGHT_NAMES}
    ms = {n: given["m_" + n] for n in WEIGHT_NAMES}
    vs = {n: given["v_" + n] for n in WEIGHT_NAMES}
    t, d = x.shape[-2], x.shape[-1]
    xc, yc, cc = lax.axis_index("x"), lax.axis_index("y"), lax.axis_index("c")

    shard2d = {n: wts[n].reshape(wts[n].shape[-2:]) for n in BIG_NAMES}
    p_idx = jnp.reshape(2 * xc + yc, (1,)).astype(jnp.int32)
    c_idx = jnp.reshape(cc, (1,)).astype(jnp.int32)
    pc_idx = jnp.stack([2 * xc + yc, cc]).astype(jnp.int32)
    wb = {n: _cast_into_gathered(shard2d[n], p_idx, f"cast_{n}") for n in BIG_NAMES}

    small = {n: wts[n].reshape(-1, wts[n].shape[-1]) for n in SMALL_NAMES}
    small['sg_w'] = wts['sg_w'].reshape(wts['sg_w'].shape[-3:])
    loss, dx, whole, gs = _step(x.reshape(t, d), loss_target.reshape(t, d), wb, small, c_idx, pc_idx)
    loss = lax.psum(loss[0, 0], ("x", "y", "c"))

    def pack(tree):
        rows = jnp.concatenate([_as_rows(tree[n]) for n in SMALL_NAMES], axis=0)
        return jnp.pad(rows, ((0, -rows.shape[0] % (8 * N_DEV)), (0, 0)))

    packed = pack(gs)
    packed = _all_reduce_small(packed.reshape(N_DEV, -1, LANES)).reshape(packed.shape)

    grads, delta, new_m, new_v = {}, {}, {}, {}
    for n in BIG_NAMES:
        shape, flat = wts[n].shape, shard2d[n].shape
        out = _adamw(shard2d[n], whole[n].reshape(flat), ms[n].reshape(flat), vs[n].reshape(flat), f"adamw_{n}")
        grads[n], delta[n], new_m[n], new_v[n] = (a.reshape(shape) for a in out)

    small_out = _adamw(pack(wts), packed, pack(ms), pack(vs), "adamw_small")
    row = 0
    for n in SMALL_NAMES:
        shape = wts[n].shape
        sz = wts[n].size // LANES
        grads[n], delta[n], new_m[n], new_v[n] = (a[row:row + sz].reshape(shape) for a in small_out)
        row += sz + -sz % 8

    return (loss, dx.reshape(x.shape), *[grads[n] for n in WEIGHT_NAMES], *[delta[n] for n in WEIGHT_NAMES],
            *[new_m[n] for n in WEIGHT_NAMES], *[new_v[n] for n in WEIGHT_NAMES])
```

```python
import functools

import jax
import jax.numpy as jnp
from jax import lax
from jax.experimental import pallas as pl
from jax.experimental.pallas import tpu as pltpu

F32 = jnp.float32
BF16 = jnp.bfloat16
MESH = pl.DeviceIdType.MESH

NORM_EPS = 1e-6
LN_EPS = 1e-5
HEAD_DIM = 128
HEADS_PER_GROUP = 4
GROUP_WIDTH = HEADS_PER_GROUP * HEAD_DIM
DILATIONS = (1, 4, 16)
N_GROUPS = len(DILATIONS)
ATT_BLOCK = 128
ROPE_DIM = HEAD_DIM // 4
ROPE_THETA = 500000.0
SG_CHUNK = 128
SG_GROUPS = 12
SG_GROUP_DIM = 128
MASKED = -1e30

ADAM_LR = 0.001
ADAM_B1 = 0.9
ADAM_B2 = 0.999
ADAM_EPS = 1e-08
ADAM_WD = 0.01
ADAM_STEP = 10

N_CHIPS = 4
N_DEV = 8
LANES = 128
MIB = 2 ** 20
VMEM_BYTES_V7X = 64 * MIB

WEIGHT_NAMES = ['ffn1_norm', 'ffn1_w_gate', 'ffn1_w_up', 'ffn1_w_down', 'mix_norm', 'w_in', 'sg_ln_g', 'sg_ln_b',
                'sg_w', 'sg_b', 'w_att_out', 'w_sg_out', 'w_out', 'ffn2_norm', 'ffn2_w_gate', 'ffn2_w_up',
                'ffn2_w_down', 'final_norm']
BIG = [('ffn1_w_gate', 1), ('ffn1_w_up', 1), ('ffn1_w_down', 0), ('w_in', 1), ('w_att_out', 1), ('w_sg_out', 1),
       ('w_out', 0), ('ffn2_w_gate', 1), ('ffn2_w_up', 1), ('ffn2_w_down', 0)]
BIG_NAMES = [n for n, _ in BIG]
SMALL_NAMES = [n for n in WEIGHT_NAMES if n not in BIG_NAMES]


def _nbytes(shape, dtype):
    n = jnp.dtype(dtype).itemsize
    for s in shape:
        if s is not None:
            n *= s
    return n


def _pallas_call(*args, **kw):
    kw['out_shape'] = jax.tree.map(lambda s: pltpu.HBM(s.shape, s.dtype), kw['out_shape'])
    call = pl.pallas_call(*args, **kw)

    def pinned(*operands):
        return call(*[o if jnp.issubdtype(o.dtype, jnp.integer) else pltpu.with_memory_space_constraint(o, pltpu.HBM)
                      for o in operands])

    return pinned


def _cparams(sem, block_bytes, **kw):
    limit = int(min(max(3 * block_bytes, 32 * MIB), VMEM_BYTES_V7X - 8 * MIB))
    return pltpu.CompilerParams(dimension_semantics=sem, vmem_limit_bytes=limit, **kw)


def _tile(dim, cap):
    best = None
    for t in range(LANES, min(dim, cap) + 1, LANES):
        if dim % t == 0:
            best = t
    if best is None:
        assert dim <= cap, (dim, cap)
        return dim
    return best


def _rows(dim, cap):
    best = None
    for t in range(8, min(dim, cap) + 1, 8):
        if dim % t == 0:
            best = t
    assert best is not None, (dim, cap)
    return best


def _place():
    x, y, c = lax.axis_index("x"), lax.axis_index("y"), lax.axis_index("c")
    others = [(1 - x, y), (x, 1 - y), (1 - x, 1 - y)]
    return x, y, c, others


class _Rider:
    def __init__(self, operands, out_shapes, aliases, sems, start, finish):
        self.operands = operands
        self.out_shapes = out_shapes
        self.aliases = aliases
        self.sems = sems
        self.start = start
        self.finish = finish


def _run(body, *, name, grid, in_specs, out_specs, out_shape, scratch_shapes, operands, block_bytes, rider=None, pinned=True):
    call = _pallas_call if pinned else pl.pallas_call
    if rider is None:
        sem = ("parallel",) * (len(grid) - 1) + ("arbitrary",)
        return call(body, out_shape=out_shape, grid=grid, in_specs=in_specs, out_specs=out_specs,
                    scratch_shapes=scratch_shapes, compiler_params=_cparams(sem, block_bytes), name=name)(*operands)
    n_in, n_out, n_scr = len(operands), len(out_shape), len(scratch_shapes)
    r_in, r_out = len(rider.operands), len(rider.out_shapes)
    any_spec = pl.BlockSpec(memory_space=pl.ANY)

    def wrapped(*refs):
        ins, refs = refs[:n_in], refs[n_in:]
        r_ins, refs = refs[:r_in], refs[r_in:]
        outs, refs = refs[:n_out], refs[n_out:]
        r_outs, refs = refs[:r_out], refs[r_out:]
        scr, sems = refs[:n_scr], refs[n_scr:]
        if not grid:
            rider.start(r_ins, r_outs, sems)
            rider.finish(r_ins, r_outs, sems)
            return
        ids = [pl.program_id(a) for a in range(len(grid))]
        first = functools.reduce(jnp.logical_and, [i == 0 for i in ids])
        last = functools.reduce(jnp.logical_and, [i == g - 1 for i, g in zip(ids, grid)])

        @pl.when(first)
        def _():
            rider.start(r_ins, r_outs, sems)

        body(*ins, *outs, *scr)

        @pl.when(last)
        def _():
            rider.finish(r_ins, r_outs, sems)

    results = call(
        wrapped, out_shape=list(out_shape) + list(rider.out_shapes), grid=grid,
        in_specs=list(in_specs) + [any_spec] * r_in, out_specs=list(out_specs) + [any_spec] * r_out,
        scratch_shapes=list(scratch_shapes) + list(rider.sems),
        input_output_aliases={n_in + k: n_out + v for k, v in rider.aliases.items()},
        compiler_params=_cparams(("arbitrary",) * len(grid) if grid else None, block_bytes, has_side_effects=True),
        name=name)(*operands, *rider.operands)
    return results[:n_out], results[n_out:]


def _exchange(rider, name):
    return _run(None, name=name, grid=(), in_specs=[], out_specs=[], out_shape=[], scratch_shapes=[], operands=[],
                block_bytes=0, rider=rider)[1]


def _gather_rider(items):
    bufs, index = [], []
    for b, r0, r1 in items:
        if not any(b is q for q in bufs):
            bufs.append(b)
        index.append(([k for k, q in enumerate(bufs) if q is b][0], r0, r1))
    n = len(index)

    def piece(refs, k, chip, half):
        bi, r0, r1 = index[k]
        return refs[bi].at[chip, half, pl.ds(r0, r1 - r0)]

    def copy(ref, sem_pair, k, j, to):
        return pltpu.make_async_remote_copy(ref, ref, sem_pair[0].at[k, j], sem_pair[1].at[k, j], device_id=to, device_id_type=MESH)

    def start(r_ins, buf, sems):
        x, y, c, others = _place()
        for k in range(n):
            for j, (ox, oy) in enumerate(others):
                copy(piece(buf, k, 2 * x + y, c), sems[:2], k, j, (ox, oy, c)).start()

    def finish(r_ins, buf, sems):
        x, y, c, others = _place()
        for k in range(n):
            for j, (ox, oy) in enumerate(others):
                got = piece(buf, k, 2 * ox + oy, c)
                copy(got, sems[:2], k, j, (ox, oy, c)).wait_recv()
                copy(got, sems[2:], k, j, (x, y, 1 - c)).start()
        for k in range(n):
            for j, (ox, oy) in enumerate(others):
                copy(piece(buf, k, 2 * ox + oy, 1 - c), sems[2:], k, j, (x, y, 1 - c)).wait_recv()
        for k in range(n):
            for j, (ox, oy) in enumerate(others):
                copy(piece(buf, k, 2 * x + y, c), sems[:2], k, j, (ox, oy, c)).wait_send()
                copy(piece(buf, k, 2 * ox + oy, c), sems[2:], k, j, (x, y, 1 - c)).wait_send()

    return _Rider(bufs, [jax.ShapeDtypeStruct(b.shape, b.dtype) for b in bufs], {i: i for i in range(len(bufs))},
                  [pltpu.SemaphoreType.DMA((n, 3))] * 4, start, finish)


def _scatter_rider(parts):
    n = len(parts)

    def copy(src, dst, sems, i, j, to):
        return pltpu.make_async_remote_copy(src, dst, sems[0].at[i, j], sems[1].at[i, j], device_id=to, device_id_type=MESH)

    def start(src, dst, sems):
        x, y, c, others = _place()
        for i in range(n):
            for j, (ox, oy) in enumerate(others):
                copy(src[i].at[2 * ox + oy], dst[i].at[j], sems, i, j, (ox, oy, c)).start()

    def finish(src, dst, sems):
        x, y, c, others = _place()
        for i in range(n):
            for j, (ox, oy) in enumerate(others):
                copy(src[i].at[2 * ox + oy], dst[i].at[j], sems, i, j, (ox, oy, c)).wait()

    return _Rider(parts, [jax.ShapeDtypeStruct((3,) + p.shape[1:], p.dtype) for p in parts], {},
                  [pltpu.SemaphoreType.DMA((n, 3))] * 2, start, finish)


def _matmul(pairs, mode, out_dtype, name, *, scale=1.0, residual=None, b3=False, out3=0, caps=(1024, 1024, 512), rider=None):
    a0, b0 = pairs[0]
    if mode == 'nn':
        m, k = a0.shape
        n = b0.shape[0] * b0.shape[2] if b3 else b0.shape[1]
    elif mode == 'nt':
        m = a0.shape[0]
        n, k = (b0.shape[1], b0.shape[0] * b0.shape[2]) if b3 else b0.shape
    else:
        k, m = a0.shape
        n = b0.shape[1]
    tm = _tile(m, caps[0])
    tn = _tile(n, caps[1])
    tk = _tile(k, caps[2])
    if b3 and mode == 'nn':
        tn = b0.shape[2]
    if b3 and mode == 'nt':
        tk = b0.shape[2]
    if out3:
        tn = n // out3
    nk = k // tk
    if mode == 'tn':
        a_spec = pl.BlockSpec((tk, tm), lambda i, j, kk: (kk, i))
        b_spec = pl.BlockSpec((tk, tn), lambda i, j, kk: (kk, j))
        dims = ((0,), (0,))
    elif mode == 'nn':
        a_spec = pl.BlockSpec((tm, tk), lambda i, j, kk: (i, kk))
        b_spec = (pl.BlockSpec((None, tk, tn), lambda i, j, kk: (j, kk, 0)) if b3
                  else pl.BlockSpec((tk, tn), lambda i, j, kk: (kk, j)))
        dims = ((1,), (0,))
    else:
        a_spec = pl.BlockSpec((tm, tk), lambda i, j, kk: (i, kk))
        b_spec = (pl.BlockSpec((None, tn, tk), lambda i, j, kk: (kk, j, 0)) if b3
                  else pl.BlockSpec((tn, tk), lambda i, j, kk: (j, kk)))
        dims = ((1,), (1,))
    in_specs, operands = [], []
    for a, b in pairs:
        in_specs += [a_spec, b_spec]
        operands += [a, b]
    block_bytes = len(pairs) * (_nbytes((tm, tk), a0.dtype) + _nbytes((tk, tn), b0.dtype))
    if residual is not None:
        in_specs.append(pl.BlockSpec((tm, tn), lambda i, j, kk: (i, j)))
        operands.append(residual)
        block_bytes += _nbytes((tm, tn), F32)
    if out3:
        out_spec = pl.BlockSpec((None, tm, tn), lambda i, j, kk: (j, i, 0))
        out_shape = jax.ShapeDtypeStruct((out3, m, tn), out_dtype)
    else:
        out_spec = pl.BlockSpec((tm, tn), lambda i, j, kk: (i, j))
        out_shape = jax.ShapeDtypeStruct((m, n), out_dtype)
    block_bytes += _nbytes((tm, tn), out_dtype) + _nbytes((tm, tn), F32)
    n_pairs = len(pairs)
    has_res = residual is not None

    def body(*refs):
        o_ref, acc = refs[-2], refs[-1]
        kk = pl.program_id(2)

        def product():
            part = None
            for p in range(n_pairs):
                d = lax.dot_general(refs[2 * p][...].astype(BF16), refs[2 * p + 1][...].astype(BF16),
                                    (dims, ((), ())), preferred_element_type=F32)
                part = d if part is None else part + d
            return part

        def finish(r):
            if scale != 1.0:
                r = r * scale
            if has_res:
                r = refs[2 * n_pairs][...] + r
            o_ref[...] = r.astype(out_dtype)

        if nk == 1:
            finish(product())
            return

        @pl.when(kk == 0)
        def _():
            acc[...] = product()

        if nk > 2:
            @pl.when(jnp.logical_and(kk > 0, kk < nk - 1))
            def _():
                acc[...] += product()

        @pl.when(kk == nk - 1)
        def _():
            finish(acc[...] + product())

    res = _run(body, name=name, grid=(m // tm, n // tn, nk), in_specs=in_specs, out_specs=[out_spec], out_shape=[out_shape],
               scratch_shapes=[pltpu.VMEM((tm, tn), F32)], operands=operands, block_bytes=block_bytes, rider=rider)
    return res[0] if rider is None else (res[0][0], res[1])


def _rmsnorm_fwd(x, g, name):
    t, d = x.shape
    tm = _rows(t, 512)

    def body(x_ref, g_ref, o_ref):
        xv = x_ref[...]
        r = lax.rsqrt(jnp.mean(xv * xv, axis=1, keepdims=True) + NORM_EPS)
        o_ref[...] = (xv * r * g_ref[...]).astype(BF16)

    row = pl.BlockSpec((tm, d), lambda i: (i, 0))
    return _pallas_call(
        body, out_shape=jax.ShapeDtypeStruct((t, d), BF16), grid=(t // tm,),
        in_specs=[row, pl.BlockSpec((1, d), lambda i: (0, 0))], out_specs=row,
        compiler_params=_cparams(("parallel",), 2 * _nbytes((tm, d), F32)), name=name)(x, g)


def _rms_grad(xv, g, dn, d):
    r = lax.rsqrt(jnp.mean(xv * xv, axis=1, keepdims=True) + NORM_EPS)
    u = dn * g
    s = jnp.sum(xv * u, axis=1, keepdims=True)
    dx = r * u - xv * (r * r * r) * (s * (1.0 / d))
    return dx, dn * xv * r


def _rmsnorm_bwd(x, g, dn, dres, name):
    t, d = x.shape
    tm = _rows(t, 256)

    def body(x_ref, g_ref, dn_ref, dres_ref, dx_ref, dxb_ref, dg_ref):
        dx, dg_rows = _rms_grad(x_ref[...], g_ref[...], dn_ref[...].astype(F32), d)
        dx = dres_ref[...] + dx
        dx_ref[...] = dx
        dxb_ref[...] = dx.astype(BF16)

        @pl.when(pl.program_id(0) == 0)
        def _():
            dg_ref[...] = jnp.zeros_like(dg_ref)

        dg_ref[...] += jnp.sum(dg_rows, axis=0, keepdims=True)

    row = pl.BlockSpec((tm, d), lambda i: (i, 0))
    vec = pl.BlockSpec((1, d), lambda i: (0, 0))
    return _pallas_call(
        body, out_shape=(jax.ShapeDtypeStruct((t, d), F32), jax.ShapeDtypeStruct((t, d), BF16), jax.ShapeDtypeStruct((1, d), F32)),
        grid=(t // tm,), in_specs=[row, vec, row, row], out_specs=(row, row, vec),
        compiler_params=_cparams(("arbitrary",), 5 * _nbytes((tm, d), F32)), name=name)(x, g, dn, dres)


def _final_loss(x, g, target, name):
    t, d = x.shape
    tm = _rows(t, 256)

    def body(x_ref, g_ref, t_ref, loss_ref, dx_ref, dxb_ref, dg_ref):
        xv, gv = x_ref[...], g_ref[...]
        r = lax.rsqrt(jnp.mean(xv * xv, axis=1, keepdims=True) + NORM_EPS)
        err = xv * r * gv - t_ref[...]
        dx, dg_rows = _rms_grad(xv, gv, err * (1.0 / d), d)
        dx_ref[...] = dx
        dxb_ref[...] = dx.astype(BF16)

        @pl.when(pl.program_id(0) == 0)
        def _():
            dg_ref[...] = jnp.zeros_like(dg_ref)
            loss_ref[...] = jnp.zeros_like(loss_ref)

        dg_ref[...] += jnp.sum(dg_rows, axis=0, keepdims=True)
        row_loss = jnp.sum(err * err, axis=1, keepdims=True) * (0.5 / d)
        loss_ref[...] += jnp.sum(row_loss, axis=0, keepdims=True)

    row = pl.BlockSpec((tm, d), lambda i: (i, 0))
    vec = pl.BlockSpec((1, d), lambda i: (0, 0))
    return _pallas_call(
        body, out_shape=(jax.ShapeDtypeStruct((1, 1), F32), jax.ShapeDtypeStruct((t, d), F32),
                         jax.ShapeDtypeStruct((t, d), BF16), jax.ShapeDtypeStruct((1, d), F32)),
        grid=(t // tm,), in_specs=[row, vec, row], out_specs=(pl.BlockSpec((1, 1), lambda i: (0, 0)), row, row, vec),
        compiler_params=_cparams(("arbitrary",), 4 * _nbytes((tm, d), F32)), name=name)(x, g, target)


def _sigmoid(x):
    return 0.5 * jnp.tanh(0.5 * x) + 0.5


def _ffn_up(n, wg, wu, name, rider=None):
    t, d = n.shape
    s, _, f = wg.shape
    tm, tk = _tile(t, 1024), _tile(d, 1024)
    nk = d // tk

    def body(n_ref, wg_ref, wu_ref, a_ref, b_ref, h_ref, acc_g, acc_u):
        kk = pl.program_id(2)

        def products():
            nv = n_ref[...]
            return jnp.dot(nv, wg_ref[...], preferred_element_type=F32), jnp.dot(nv, wu_ref[...], preferred_element_type=F32)

        def finish(a, b):
            a_ref[...] = a.astype(BF16)
            b_ref[...] = b.astype(BF16)
            h_ref[...] = (a * _sigmoid(a) * b).astype(BF16)

        if nk == 1:
            finish(*products())
            return

        @pl.when(kk == 0)
        def _():
            acc_g[...], acc_u[...] = products()

        if nk > 2:
            @pl.when(jnp.logical_and(kk > 0, kk < nk - 1))
            def _():
                pg, pu = products()
                acc_g[...] += pg
                acc_u[...] += pu

        @pl.when(kk == nk - 1)
        def _():
            pg, pu = products()
            finish(acc_g[...] + pg, acc_u[...] + pu)

    w_spec = pl.BlockSpec((None, tk, f), lambda i, j, kk: (j, kk, 0))
    o_spec = pl.BlockSpec((tm, f), lambda i, j, kk: (i, j))
    out = jax.ShapeDtypeStruct((t, s * f), BF16)
    block_bytes = _nbytes((tm, tk), BF16) + 2 * _nbytes((tk, f), BF16) + 3 * _nbytes((tm, f), BF16) + 2 * _nbytes((tm, f), F32)
    return _run(body, name=name, grid=(t // tm, s, nk),
                in_specs=[pl.BlockSpec((tm, tk), lambda i, j, kk: (i, kk)), w_spec, w_spec], out_specs=[o_spec, o_spec, o_spec],
                out_shape=[out, out, out], scratch_shapes=[pltpu.VMEM((tm, f), F32), pltpu.VMEM((tm, f), F32)],
                operands=[n, wg, wu], block_bytes=block_bytes, rider=rider)


def _ffn_bwd_act(dx, wd, a, b, name):
    t, d = dx.shape
    f = wd.shape[0]
    tm, tn, tk = _tile(t, 1024), _tile(f, 1536), _tile(d, 1024)
    nk = d // tk

    def body(dx_ref, wd_ref, a_ref, b_ref, da_ref, db_ref, acc):
        kk = pl.program_id(2)

        def product():
            return lax.dot_general(dx_ref[...], wd_ref[...], (((1,), (1,)), ((), ())), preferred_element_type=F32)

        def finish(r):
            dh = 0.5 * r
            av, bv = a_ref[...].astype(F32), b_ref[...].astype(F32)
            sg = _sigmoid(av)
            da_ref[...] = (dh * bv * (sg * (1.0 + av * (1.0 - sg)))).astype(BF16)
            db_ref[...] = (dh * (av * sg)).astype(BF16)

        if nk == 1:
            finish(product())
            return

        @pl.when(kk == 0)
        def _():
            acc[...] = product()

        if nk > 2:
            @pl.when(jnp.logical_and(kk > 0, kk < nk - 1))
            def _():
                acc[...] += product()

        @pl.when(kk == nk - 1)
        def _():
            finish(acc[...] + product())

    act = pl.BlockSpec((tm, tn), lambda i, j, kk: (i, j))
    out = jax.ShapeDtypeStruct((t, f), BF16)
    block_bytes = _nbytes((tm, tk), BF16) + _nbytes((tn, tk), BF16) + 4 * _nbytes((tm, tn), BF16) + _nbytes((tm, tn), F32)
    return _pallas_call(
        body, out_shape=(out, out), grid=(t // tm, f // tn, nk),
        in_specs=[pl.BlockSpec((tm, tk), lambda i, j, kk: (i, kk)), pl.BlockSpec((tn, tk), lambda i, j, kk: (j, kk)),
                  act, act],
        out_specs=(act, act), scratch_shapes=[pltpu.VMEM((tm, tn), F32)],
        compiler_params=_cparams(("parallel", "parallel", "arbitrary"), block_bytes), name=name)(dx, wd, a, b)


AXIS = dict(BIG)


def _full(wb, n):
    _, _, r, ccols = wb[n].shape
    return wb[n].reshape(N_CHIPS, 2 * r, ccols) if AXIS[n] == 1 else wb[n].reshape(N_CHIPS * 2 * r, ccols)


def _gather(wb, specs):
    items, names = [], []
    for s in specs:
        n, r0, r1 = (s, 0, wb[s].shape[2]) if isinstance(s, str) else s
        items.append((wb[n], r0, r1))
        if n not in names:
            names.append(n)
    return _gather_rider(items), names


def _landed(wb, names, results):
    for n, r in zip(names, results):
        wb[n] = r


def _reduce_first(grads, names, wb, c_idx):
    g4 = [g.reshape(wb[n].shape) for g, n in zip(grads, names)]
    from_sibling = _exchange(_sibling_rider(g4), "rs_sibling_" + names[0])
    return [_sibling_sum(a, b, c_idx, f"rs_sum1_{n}") for a, b, n in zip(g4, from_sibling, names)]


def _ffn_forward(x, gain, wb, tag, up_specs, down_specs):
    n = _rmsnorm_fwd(x, gain, f"{tag}_norm")
    rider, names = _gather(wb, up_specs)
    (a, b, h), got = _ffn_up(n, _full(wb, f"{tag}_w_gate"), _full(wb, f"{tag}_w_up"), f"{tag}_up", rider=rider)
    _landed(wb, names, got)
    down = dict(scale=0.5, residual=x, caps=(1024, 1024, 1536))
    if down_specs:
        rider, names = _gather(wb, down_specs)
        x_next, got = _matmul([(h, _full(wb, f"{tag}_w_down"))], 'nn', F32, f"{tag}_down", rider=rider, **down)
        _landed(wb, names, got)
    else:
        x_next = _matmul([(h, _full(wb, f"{tag}_w_down"))], 'nn', F32, f"{tag}_down", **down)
    return x_next, (n, a, b, h)


def _ffn_backward(x, gain, wb, saved, dx_next, dx_next_b, c_idx, tag, chained, dwd_rider=None):
    n, a, b, h = saved
    wg, wu, wd = (f"{tag}_w_gate", f"{tag}_w_up", f"{tag}_w_down")
    da, db = _ffn_bwd_act(dx_next_b, _full(wb, wd), a, b, f"{tag}_bwd_act")
    res = _matmul([(h, dx_next_b)], 'tn', BF16, f"{tag}_dwd", scale=0.5, caps=(1536, 2048, 1024), rider=dwd_rider)
    g_wd, carried = (res, ()) if dwd_rider is None else res
    grad_mm = dict(out3=N_CHIPS, caps=(2048, 1024, 1024))
    dn_pairs = [(da, _full(wb, wg)), (db, _full(wb, wu))]
    if not chained:
        (p_wd,) = _reduce_first([g_wd], [wd], wb, c_idx)
        g_wg, (r_wd,) = _matmul([(n, da)], 'tn', BF16, f"{tag}_dwg", rider=_scatter_rider([p_wd]), **grad_mm)
        g_wu = _matmul([(n, db)], 'tn', BF16, f"{tag}_dwu", **grad_mm)
        p_wg, p_wu = _reduce_first([g_wg, g_wu], [wg, wu], wb, c_idx)
        dn, (r_wg, r_wu) = _matmul(dn_pairs, 'nt', BF16, f"{tag}_dn", b3=True, rider=_scatter_rider([p_wg, p_wu]))
        done, pending = {wg: (p_wg, r_wg), wu: (p_wu, r_wu), wd: (p_wd, r_wd)}, {}
    else:
        g_wd = g_wd.reshape(wb[wd].shape)
        g_wg, (s_wd,) = _matmul([(n, da)], 'tn', BF16, f"{tag}_dwg", rider=_sibling_rider([g_wd]), **grad_mm)
        p_wd = _sibling_sum(g_wd, s_wd, c_idx, f"rs_sum1_{wd}")
        g_wg = g_wg.reshape(wb[wg].shape)
        g_wu, (r_wd, s_wg) = _matmul([(n, db)], 'tn', BF16, f"{tag}_dwu",
                                     rider=_merge_riders(_scatter_rider([p_wd]), _sibling_rider([g_wg])), **grad_mm)
        p_wg = _sibling_sum(g_wg, s_wg, c_idx, f"rs_sum1_{wg}")
        g_wu = g_wu.reshape(wb[wu].shape)
        dn, (r_wg, s_wu) = _matmul(dn_pairs, 'nt', BF16, f"{tag}_dn", b3=True,
                                   rider=_merge_riders(_scatter_rider([p_wg]), _sibling_rider([g_wu])))
        p_wu = _sibling_sum(g_wu, s_wu, c_idx, f"rs_sum1_{wu}")
        done, pending = {wg: (p_wg, r_wg), wd: (p_wd, r_wd)}, {wu: p_wu}
    dx, dx_b, g_gain = _rmsnorm_bwd(x, gain, dn, dx_next, f"{tag}_norm_bwd")
    return dx, dx_b, g_gain, done, pending, carried


def _rope_tables(seq):
    half = ROPE_DIM // 2
    inv_freq = ROPE_THETA ** (-jnp.arange(0, ROPE_DIM, 2, dtype=F32) / ROPE_DIM)
    ang = jnp.arange(seq).astype(F32)[:, None] * inv_freq[None, :]
    cos, sin = jnp.cos(ang), jnp.sin(ang)
    zeros = lambda w: jnp.zeros((seq, w), F32)
    c = jnp.concatenate([cos, cos, jnp.ones((seq, HEAD_DIM - ROPE_DIM), F32)], axis=1)
    s_up = jnp.concatenate([-sin, zeros(HEAD_DIM - half)], axis=1)
    s_dn = jnp.concatenate([zeros(half), sin, zeros(HEAD_DIM - ROPE_DIM)], axis=1)
    return c, s_up, s_dn


def _rotate(xv, cv, uv, dv):
    half = ROPE_DIM // 2
    return xv * cv + pltpu.roll(xv, HEAD_DIM - half, 1) * uv + pltpu.roll(xv, half, 1) * dv


def _stage(tm):
    return pltpu.VMEM((HEADS_PER_GROUP, tm, HEAD_DIM), F32)


def _to_groups(stage, o_ref, dil):
    rows = stage.shape[1] // dil
    for r in range(dil):
        for h in range(HEADS_PER_GROUP):
            col = r * GROUP_WIDTH + h * HEAD_DIM
            o_ref[:, col:col + HEAD_DIM] = stage[h, pl.ds(r, rows, stride=dil), :].astype(o_ref.dtype)


def _from_groups(g_ref, stage, dil):
    rows = stage.shape[1] // dil
    for r in range(dil):
        for h in range(HEADS_PER_GROUP):
            col = r * GROUP_WIDTH + h * HEAD_DIM
            stage[h, pl.ds(r, rows, stride=dil), :] = g_ref[:, col:col + HEAD_DIM].astype(F32)


def _group_spec(tm, dil):
    return pl.BlockSpec((tm // dil, dil * GROUP_WIDTH), lambda i: (i, 0))


def _group_shape(t, dil, dtype):
    return jax.ShapeDtypeStruct((t // dil, dil * GROUP_WIDTH), dtype)


def _rope_fwd(proj, tables, name):
    t = proj.shape[0]
    tm = _rows(t, 512)
    att_w = N_GROUPS * GROUP_WIDTH
    dilated = [(gi, dil) for gi, dil in enumerate(DILATIONS) if dil > 1]

    def body(x_ref, c_ref, up_ref, dn_ref, qk0_ref, *rest):
        outs, stage = rest[:-1], rest[-1]
        cv, uv, dv = c_ref[...], up_ref[...], dn_ref[...]
        for part in range(2):
            for gi, dil in enumerate(DILATIONS):
                for h in range(HEADS_PER_GROUP):
                    col = part * att_w + gi * GROUP_WIDTH + h * HEAD_DIM
                    y = _rotate(x_ref[:, col:col + HEAD_DIM].astype(F32), cv, uv, dv)
                    if dil == 1:
                        qk0_ref[:, part * GROUP_WIDTH + h * HEAD_DIM:part * GROUP_WIDTH + (h + 1) * HEAD_DIM] = y.astype(BF16)
                    else:
                        stage[h] = y
                if dil > 1:
                    _to_groups(stage, outs[3 * dilated.index((gi, dil)) + part], dil)
        for n, (gi, dil) in enumerate(dilated):
            col = 2 * att_w + gi * GROUP_WIDTH
            for h in range(HEADS_PER_GROUP):
                stage[h] = x_ref[:, col + h * HEAD_DIM:col + (h + 1) * HEAD_DIM].astype(F32)
            _to_groups(stage, outs[3 * n + 2], dil)

    tab = pl.BlockSpec((tm, HEAD_DIM), lambda i: (i, 0))
    out_shape = [jax.ShapeDtypeStruct((t, 2 * GROUP_WIDTH), BF16)]
    out_specs = [pl.BlockSpec((tm, 2 * GROUP_WIDTH), lambda i: (i, 0))]
    for _, dil in dilated:
        out_shape += [_group_shape(t, dil, BF16)] * 3
        out_specs += [_group_spec(tm, dil)] * 3
    res = _pallas_call(
        body, out_shape=out_shape, grid=(t // tm,),
        in_specs=[pl.BlockSpec((tm, 3 * att_w), lambda i: (i, 0)), tab, tab, tab], out_specs=out_specs,
        scratch_shapes=[_stage(tm)],
        compiler_params=_cparams(("parallel",), 4 * _nbytes((tm, 3 * att_w), BF16)), name=name)(proj, *tables)
    return res[0], [tuple(res[1 + 3 * n:4 + 3 * n]) for n in range(len(dilated))]


def _rope_bwd(dq0, dk0, dv0, grouped, tables, name):
    t = dq0.shape[0]
    tm = _rows(t, 512)
    att_w = N_GROUPS * GROUP_WIDTH
    dilated = [(gi, dil) for gi, dil in enumerate(DILATIONS) if dil > 1]
    c, s_up, s_dn = tables

    def body(c_ref, up_ref, dn_ref, dq0_ref, dk0_ref, dv0_ref, *rest):
        g_refs, o_ref, stage = rest[:-2], rest[-2], rest[-1]
        cv, uv, dv = c_ref[...], -up_ref[...], -dn_ref[...]
        for part, first in enumerate((dq0_ref, dk0_ref)):
            for gi, dil in enumerate(DILATIONS):
                if dil > 1:
                    _from_groups(g_refs[3 * dilated.index((gi, dil)) + part], stage, dil)
                for h in range(HEADS_PER_GROUP):
                    sl = slice(h * HEAD_DIM, (h + 1) * HEAD_DIM)
                    xv = first[:, sl].astype(F32) if dil == 1 else stage[h]
                    col = part * att_w + gi * GROUP_WIDTH + h * HEAD_DIM
                    o_ref[:, col:col + HEAD_DIM] = _rotate(xv, cv, uv, dv).astype(BF16)
        for gi, dil in enumerate(DILATIONS):
            col = 2 * att_w + gi * GROUP_WIDTH
            if dil == 1:
                o_ref[:, col:col + GROUP_WIDTH] = dv0_ref[...]
            else:
                _from_groups(g_refs[3 * dilated.index((gi, dil)) + 2], stage, dil)
                for h in range(HEADS_PER_GROUP):
                    o_ref[:, col + h * HEAD_DIM:col + (h + 1) * HEAD_DIM] = stage[h].astype(BF16)

    tab = pl.BlockSpec((tm, HEAD_DIM), lambda i: (i, 0))
    nat = pl.BlockSpec((tm, GROUP_WIDTH), lambda i: (i, 0))
    in_specs, operands = [tab, tab, tab, nat, nat, nat], [c, s_up, s_dn, dq0, dk0, dv0]
    for (_, dil), arrs in zip(dilated, grouped):
        in_specs += [_group_spec(tm, dil)] * 3
        operands += list(arrs)
    return _pallas_call(
        body, out_shape=jax.ShapeDtypeStruct((t, 3 * att_w), BF16), grid=(t // tm,), in_specs=in_specs,
        out_specs=pl.BlockSpec((tm, 3 * att_w), lambda i: (i, 0)), scratch_shapes=[_stage(tm)],
        compiler_params=_cparams(("parallel",), 4 * _nbytes((tm, 3 * att_w), BF16)), name=name)(*operands)


def _regroup(arrs, name):
    t = arrs[0].shape[0]
    tm = _rows(t, 512)
    dilated = [dil for dil in DILATIONS if dil > 1]
    n_in = len(arrs)

    def body(*refs):
        ins, outs, stage = refs[:n_in], refs[n_in:-1], refs[-1]
        for j, x_ref in enumerate(ins):
            for h in range(HEADS_PER_GROUP):
                stage[h] = x_ref[:, h * HEAD_DIM:(h + 1) * HEAD_DIM]
            for n, dil in enumerate(dilated):
                _to_groups(stage, outs[n * n_in + j], dil)

    nat = pl.BlockSpec((tm, GROUP_WIDTH), lambda i: (i, 0))
    res = _pallas_call(
        body, out_shape=[_group_shape(t, dil, F32) for dil in dilated for _ in arrs], grid=(t // tm,),
        in_specs=[nat] * n_in, out_specs=[_group_spec(tm, dil) for dil in dilated for _ in arrs], scratch_shapes=[_stage(tm)],
        compiler_params=_cparams(("parallel",), 3 * n_in * _nbytes((tm, GROUP_WIDTH), F32)), name=name)(*arrs)
    return [tuple(res[n * n_in:(n + 1) * n_in]) for n in range(len(dilated))]


def _query_mask(has_prev):
    qi = lax.broadcasted_iota(jnp.int32, (ATT_BLOCK, 2 * ATT_BLOCK), 0)
    col = lax.broadcasted_iota(jnp.int32, (ATT_BLOCK, 2 * ATT_BLOCK), 1)
    prev = jnp.logical_and(jnp.logical_and(col < ATT_BLOCK, col >= qi), has_prev)
    return jnp.logical_or(prev, jnp.logical_and(col >= ATT_BLOCK, col - ATT_BLOCK <= qi))


def _key_mask(has_next):
    row = lax.broadcasted_iota(jnp.int32, (2 * ATT_BLOCK, ATT_BLOCK), 0)
    kj = lax.broadcasted_iota(jnp.int32, (2 * ATT_BLOCK, ATT_BLOCK), 1)
    nxt = jnp.logical_and(jnp.logical_and(row >= ATT_BLOCK, kj >= row - ATT_BLOCK), has_next)
    return jnp.logical_or(nxt, jnp.logical_and(row < ATT_BLOCK, kj <= row))


def _scores(q, k):
    return lax.dot_general(q, k, (((1,), (1,)), ((), ())), preferred_element_type=F32) * (HEAD_DIM ** -0.5)


def _att_fwd(q, k, v, offs, dil, name):
    qo, ko, vo = offs
    length = q.shape[0]
    nb = length // ATT_BLOCK

    def body(q_ref, kp_ref, kc_ref, vp_ref, vc_ref, o_ref, lse_ref):
        mask = _query_mask(pl.program_id(1) > 0)
        heads = [slice(h * HEAD_DIM, (h + 1) * HEAD_DIM) for h in range(HEADS_PER_GROUP)]
        ks = [jnp.concatenate([kp_ref[:, sl], kc_ref[:, sl]], axis=0) for sl in heads]
        vs = [jnp.concatenate([vp_ref[:, sl], vc_ref[:, sl]], axis=0) for sl in heads]
        ss = [jnp.where(mask, _scores(q_ref[:, sl], kv), MASKED) for sl, kv in zip(heads, ks)]
        ms = [jnp.max(s, axis=1, keepdims=True) for s in ss]
        ps = [jnp.exp(s - m) for s, m in zip(ss, ms)]
        ls = [jnp.sum(p, axis=1, keepdims=True) for p in ps]
        accs = [jnp.dot(p.astype(BF16), vv, preferred_element_type=F32) for p, vv in zip(ps, vs)]
        for sl, acc, m, l in zip(heads, accs, ms, ls):
            o_ref[:, sl] = acc / l
            lse_ref[:, sl] = jnp.broadcast_to(m + jnp.log(l), (ATT_BLOCK, HEAD_DIM))

    def spec(off, prev):
        if prev:
            return pl.BlockSpec((ATT_BLOCK, GROUP_WIDTH), lambda r, n: (jnp.maximum(n - 1, 0), off + r))
        return pl.BlockSpec((ATT_BLOCK, GROUP_WIDTH), lambda r, n: (n, off + r))

    out = jax.ShapeDtypeStruct((length, dil * GROUP_WIDTH), F32)
    o_spec = pl.BlockSpec((ATT_BLOCK, GROUP_WIDTH), lambda r, n: (n, r))
    return _pallas_call(
        body, out_shape=(out, out), grid=(dil, nb),
        in_specs=[spec(qo, False), spec(ko, True), spec(ko, False), spec(vo, True), spec(vo, False)],
        out_specs=(o_spec, o_spec),
        compiler_params=_cparams(("parallel", "parallel"), 8 * _nbytes((ATT_BLOCK, GROUP_WIDTH), F32)), name=name)(q, k, k, v, v)


def _att_combine(outs, lses, name):
    t = outs[0].shape[0] * DILATIONS[0]
    tm = _rows(t, 512)

    def body(*refs):
        o_refs, l_refs = refs[:N_GROUPS], refs[N_GROUPS:2 * N_GROUPS]
        ob_ref, of_ref, lse_ref = refs[2 * N_GROUPS:2 * N_GROUPS + 3]
        stages = list(refs[2 * N_GROUPS + 3:])
        staged = []
        for o_ref, l_ref, dil in zip(o_refs, l_refs, DILATIONS):
            if dil > 1:
                so, sl = stages.pop(), stages.pop()
                _from_groups(o_ref, so, dil)
                _from_groups(l_ref, sl, dil)
                staged.append((so, sl))
            else:
                staged.append(None)
        for h in range(HEADS_PER_GROUP):
            hs = slice(h * HEAD_DIM, (h + 1) * HEAD_DIM)
            os_ = [o_ref[:, hs] if st is None else st[0][h] for o_ref, st in zip(o_refs, staged)]
            ls = [l_ref[:, hs] if st is None else st[1][h] for l_ref, st in zip(l_refs, staged)]
            m = functools.reduce(jnp.maximum, ls)
            ws = [jnp.exp(l - m) for l in ls]
            den = functools.reduce(jnp.add, ws)
            num = functools.reduce(jnp.add, [w * o for w, o in zip(ws, os_)])
            o = num / den
            ob_ref[:, hs] = o.astype(BF16)
            of_ref[:, hs] = o
            lse_ref[:, hs] = m + jnp.log(den)

    blk = pl.BlockSpec((tm, GROUP_WIDTH), lambda i: (i, 0))
    specs = [blk if dil == 1 else _group_spec(tm, dil) for dil in DILATIONS]
    f32 = jax.ShapeDtypeStruct((t, GROUP_WIDTH), F32)
    n_stage = 2 * sum(dil > 1 for dil in DILATIONS)
    return _pallas_call(
        body, out_shape=(jax.ShapeDtypeStruct((t, GROUP_WIDTH), BF16), f32, f32), grid=(t // tm,),
        in_specs=specs * 2, out_specs=(blk, blk, blk), scratch_shapes=[_stage(tm)] * n_stage,
        compiler_params=_cparams(("parallel",), 13 * _nbytes((tm, GROUP_WIDTH), F32)), name=name)(*outs, *lses)


def _att_delta(do, o, name):
    t = o.shape[0]
    tm = _rows(t, 512)

    def body(do_ref, o_ref, d_ref):
        for h in range(HEADS_PER_GROUP):
            sl = slice(h * HEAD_DIM, (h + 1) * HEAD_DIM)
            s = jnp.sum(do_ref[:, sl] * o_ref[:, sl], axis=1, keepdims=True)
            d_ref[:, sl] = jnp.broadcast_to(s, (tm, HEAD_DIM))

    blk = pl.BlockSpec((tm, GROUP_WIDTH), lambda i: (i, 0))
    return _pallas_call(
        body, out_shape=jax.ShapeDtypeStruct((t, GROUP_WIDTH), F32), grid=(t // tm,), in_specs=[blk, blk], out_specs=blk,
        compiler_params=_cparams(("parallel",), 3 * _nbytes((tm, GROUP_WIDTH), F32)), name=name)(do, o)


def _att_bwd_dq(q, k, v, do, lse, delta, offs, dil, name):
    qo, ko, vo = offs
    length = q.shape[0]
    nb = length // ATT_BLOCK
    scale = HEAD_DIM ** -0.5

    def body(q_ref, kp_ref, kc_ref, vp_ref, vc_ref, do_ref, lse_ref, dl_ref, dq_ref):
        mask = _query_mask(pl.program_id(1) > 0)
        heads = [slice(h * HEAD_DIM, (h + 1) * HEAD_DIM) for h in range(HEADS_PER_GROUP)]
        wide = lambda ref, sl: jnp.concatenate([ref[:, sl], ref[:, sl]], axis=1)
        ks = [jnp.concatenate([kp_ref[:, sl], kc_ref[:, sl]], axis=0) for sl in heads]
        vs = [jnp.concatenate([vp_ref[:, sl], vc_ref[:, sl]], axis=0) for sl in heads]
        ps = [jnp.exp(jnp.where(mask, _scores(q_ref[:, sl], kv), MASKED) - wide(lse_ref, sl)) for sl, kv in zip(heads, ks)]
        dps = [lax.dot_general(do_ref[:, sl].astype(BF16), vv, (((1,), (1,)), ((), ())), preferred_element_type=F32)
               for sl, vv in zip(heads, vs)]
        dss = [(p * (dp - wide(dl_ref, sl)) * scale).astype(BF16) for sl, p, dp in zip(heads, ps, dps)]
        dqs = [jnp.dot(ds, kv, preferred_element_type=F32) for ds, kv in zip(dss, ks)]
        for sl, dq in zip(heads, dqs):
            dq_ref[:, sl] = dq.astype(BF16)

    def spec(off, prev):
        if prev:
            return pl.BlockSpec((ATT_BLOCK, GROUP_WIDTH), lambda r, n: (jnp.maximum(n - 1, 0), off + r))
        return pl.BlockSpec((ATT_BLOCK, GROUP_WIDTH), lambda r, n: (n, off + r))

    own = pl.BlockSpec((ATT_BLOCK, GROUP_WIDTH), lambda r, n: (n, r))
    return _pallas_call(
        body, out_shape=jax.ShapeDtypeStruct((length, dil * GROUP_WIDTH), BF16), grid=(dil, nb),
        in_specs=[spec(qo, False), spec(ko, True), spec(ko, False), spec(vo, True), spec(vo, False), own, own, own],
        out_specs=own,
        compiler_params=_cparams(("parallel", "parallel"), 10 * _nbytes((ATT_BLOCK, GROUP_WIDTH), F32)),
        name=name)(q, k, k, v, v, do, lse, delta)


def _att_bwd_dkv(q, k, v, do, lse, delta, offs, dil, name):
    qo, ko, vo = offs
    length = q.shape[0]
    nb = length // ATT_BLOCK
    scale = HEAD_DIM ** -0.5

    def body(k_ref, v_ref, qc_ref, qn_ref, doc_ref, don_ref, lsec_ref, lsen_ref, dlc_ref, dln_ref, dk_ref, dv_ref):
        mask = _key_mask(pl.program_id(1) < nb - 1)
        heads = [slice(h * HEAD_DIM, (h + 1) * HEAD_DIM) for h in range(HEADS_PER_GROUP)]
        both = lambda cur, nxt, sl: jnp.concatenate([cur[:, sl], nxt[:, sl]], axis=0)
        qs = [both(qc_ref, qn_ref, sl) for sl in heads]
        dos = [both(doc_ref, don_ref, sl).astype(BF16) for sl in heads]
        ps = [jnp.exp(jnp.where(mask, _scores(qv, k_ref[:, sl]), MASKED) - both(lsec_ref, lsen_ref, sl)) for sl, qv in zip(heads, qs)]
        dps = [lax.dot_general(dov, v_ref[:, sl], (((1,), (1,)), ((), ())), preferred_element_type=F32) for sl, dov in zip(heads, dos)]
        dss = [(p * (dp - both(dlc_ref, dln_ref, sl)) * scale).astype(BF16) for sl, p, dp in zip(heads, ps, dps)]
        dvs = [lax.dot_general(p.astype(BF16), dov, (((0,), (0,)), ((), ())), preferred_element_type=F32) for p, dov in zip(ps, dos)]
        dks = [lax.dot_general(ds, qv, (((0,), (0,)), ((), ())), preferred_element_type=F32) for ds, qv in zip(dss, qs)]
        for sl, dk, dv in zip(heads, dks, dvs):
            dk_ref[:, sl] = dk.astype(BF16)
            dv_ref[:, sl] = dv.astype(BF16)

    def spec(off, nxt):
        if nxt:
            return pl.BlockSpec((ATT_BLOCK, GROUP_WIDTH), lambda r, n: (jnp.minimum(n + 1, nb - 1), off + r))
        return pl.BlockSpec((ATT_BLOCK, GROUP_WIDTH), lambda r, n: (n, off + r))

    own = pl.BlockSpec((ATT_BLOCK, GROUP_WIDTH), lambda r, n: (n, r))
    out = jax.ShapeDtypeStruct((length, dil * GROUP_WIDTH), BF16)
    return _pallas_call(
        body, out_shape=(out, out), grid=(dil, nb),
        in_specs=[spec(ko, False), spec(vo, False), spec(qo, False), spec(qo, True), spec(0, False), spec(0, True),
                  spec(0, False), spec(0, True), spec(0, False), spec(0, True)],
        out_specs=(own, own),
        compiler_params=_cparams(("parallel", "parallel"), 12 * _nbytes((ATT_BLOCK, GROUP_WIDTH), F32)),
        name=name)(k, v, q, q, do, do, lse, lse, delta, delta)


def _gelu(x):
    return 0.5 * x * (1.0 + lax.erf(x * (2.0 ** -0.5)))


def _gelu_grad(x):
    return 0.5 * (1.0 + lax.erf(x * (2.0 ** -0.5))) + x * jnp.exp(-0.5 * x * x) * ((2.0 * jnp.pi) ** -0.5)


def _sg_normed(vs, lg, lb):
    gv = _gelu(vs)
    mu = jnp.mean(gv, axis=1, keepdims=True)
    xc = gv - mu
    rstd = lax.rsqrt(jnp.mean(xc * xc, axis=1, keepdims=True) + LN_EPS)
    z = xc * rstd
    return z, rstd, z * lg + lb


def _sg_tril():
    row = lax.broadcasted_iota(jnp.int32, (SG_CHUNK, SG_CHUNK), 0)
    col = lax.broadcasted_iota(jnp.int32, (SG_CHUNK, SG_CHUNK), 1)
    return row >= col


def _sg_fwd(proj, u_blk, vs_blk, lg, lb, sg_w, bias, name):
    t = proj.shape[0]
    width = SG_GROUPS * SG_GROUP_DIM

    def body(u_ref, vs_ref, lg_ref, lb_ref, w_ref, bias_ref, o_ref):
        _, _, vn = _sg_normed(vs_ref[...].astype(F32), lg_ref[...], lb_ref[...])
        vn = vn.astype(BF16)
        tril = _sg_tril()
        for g in range(SG_GROUPS):
            sl = slice(g * SG_GROUP_DIM, (g + 1) * SG_GROUP_DIM)
            w = jnp.where(tril, w_ref[g], 0.0).astype(BF16)
            sp = jnp.dot(w, vn[:, sl], preferred_element_type=F32) + bias_ref[:, sl]
            o_ref[:, sl] = (_gelu(u_ref[:, sl].astype(F32)) * sp).astype(BF16)

    vec = pl.BlockSpec((1, width), lambda i: (0, 0))
    return _pallas_call(
        body, out_shape=jax.ShapeDtypeStruct((t, width), BF16), grid=(t // SG_CHUNK,),
        in_specs=[pl.BlockSpec((SG_CHUNK, width), lambda i: (i, u_blk)), pl.BlockSpec((SG_CHUNK, width), lambda i: (i, vs_blk)),
                  vec, vec, pl.BlockSpec((SG_GROUPS, SG_CHUNK, SG_CHUNK), lambda i: (0, 0, 0)),
                  pl.BlockSpec((SG_CHUNK, width), lambda i: (0, 0))],
        out_specs=pl.BlockSpec((SG_CHUNK, width), lambda i: (i, 0)),
        compiler_params=_cparams(("parallel",), 8 * _nbytes((SG_CHUNK, width), F32)), name=name)(proj, proj, lg, lb, sg_w, bias)


def _sg_bwd(proj, u_blk, vs_blk, dsu, lg, lb, sg_w, bias, name):
    t = proj.shape[0]
    width = SG_GROUPS * SG_GROUP_DIM

    def body(u_ref, vs_ref, dsu_ref, lg_ref, lb_ref, w_ref, bias_ref, du_ref, dvs_ref, dw_ref, dbias_ref, dlg_ref, dlb_ref):
        @pl.when(pl.program_id(0) == 0)
        def _():
            dw_ref[...] = jnp.zeros_like(dw_ref)
            dbias_ref[...] = jnp.zeros_like(dbias_ref)
            dlg_ref[...] = jnp.zeros_like(dlg_ref)
            dlb_ref[...] = jnp.zeros_like(dlb_ref)

        vs = vs_ref[...].astype(F32)
        z, rstd, vn = _sg_normed(vs, lg_ref[...], lb_ref[...])
        vn = vn.astype(BF16)
        tril = _sg_tril()
        dvn = []
        for g in range(SG_GROUPS):
            sl = slice(g * SG_GROUP_DIM, (g + 1) * SG_GROUP_DIM)
            w = jnp.where(tril, w_ref[g], 0.0).astype(BF16)
            vg = vn[:, sl]
            sp = jnp.dot(w, vg, preferred_element_type=F32) + bias_ref[:, sl]
            uv = u_ref[:, sl].astype(F32)
            dsu_g = dsu_ref[:, sl].astype(F32)
            du_ref[:, sl] = (dsu_g * sp * _gelu_grad(uv)).astype(BF16)
            dsp = dsu_g * _gelu(uv)
            dsp_b = dsp.astype(BF16)
            dw = lax.dot_general(dsp_b, vg, (((1,), (1,)), ((), ())), preferred_element_type=F32)
            dw_ref[g] += jnp.where(tril, dw, 0.0)
            dbias_ref[:, sl] += jnp.broadcast_to(jnp.sum(dsp, axis=1, keepdims=True), (SG_CHUNK, SG_GROUP_DIM))
            dvn.append(lax.dot_general(w, dsp_b, (((0,), (0,)), ((), ())), preferred_element_type=F32))
        dvn = jnp.concatenate(dvn, axis=1)
        dlg_ref[...] += jnp.sum(dvn * z, axis=0, keepdims=True)
        dlb_ref[...] += jnp.sum(dvn, axis=0, keepdims=True)
        dz = dvn * lg_ref[...]
        dgv = rstd * (dz - jnp.mean(dz, axis=1, keepdims=True) - z * jnp.mean(dz * z, axis=1, keepdims=True))
        dvs_ref[...] = (dgv * _gelu_grad(vs)).astype(BF16)

    vec = pl.BlockSpec((1, width), lambda i: (0, 0))
    row = pl.BlockSpec((SG_CHUNK, width), lambda i: (i, 0))
    fixed = pl.BlockSpec((SG_CHUNK, width), lambda i: (0, 0))
    w_spec = pl.BlockSpec((SG_GROUPS, SG_CHUNK, SG_CHUNK), lambda i: (0, 0, 0))
    act = jax.ShapeDtypeStruct((t, width), BF16)
    return _pallas_call(
        body,
        out_shape=(act, act, jax.ShapeDtypeStruct((SG_GROUPS, SG_CHUNK, SG_CHUNK), F32),
                   jax.ShapeDtypeStruct((SG_CHUNK, width), F32), jax.ShapeDtypeStruct((1, width), F32),
                   jax.ShapeDtypeStruct((1, width), F32)),
        grid=(t // SG_CHUNK,),
        in_specs=[pl.BlockSpec((SG_CHUNK, width), lambda i: (i, u_blk)), pl.BlockSpec((SG_CHUNK, width), lambda i: (i, vs_blk)),
                  row, vec, vec, w_spec, fixed],
        out_specs=(row, row, w_spec, fixed, vec, vec),
        compiler_params=_cparams(("arbitrary",), 14 * _nbytes((SG_CHUNK, width), F32)),
        name=name)(proj, proj, dsu, lg, lb, sg_w, bias)


def _gate_fwd(proj, ga_blk, gs_blk, y_att, y_sg, name):
    t, d = y_att.shape
    tm, tn = _rows(t, 512), _tile(d, GROUP_WIDTH)

    def body(ga_ref, gs_ref, ya_ref, ys_ref, o_ref):
        o_ref[...] = (_sigmoid(ga_ref[...].astype(F32)) * ya_ref[...].astype(F32)
                      + _sigmoid(gs_ref[...].astype(F32)) * ys_ref[...].astype(F32)).astype(BF16)

    own = pl.BlockSpec((tm, tn), lambda i, j: (i, j))
    return _pallas_call(
        body, out_shape=jax.ShapeDtypeStruct((t, d), BF16), grid=(t // tm, d // tn),
        in_specs=[pl.BlockSpec((tm, tn), lambda i, j: (i, ga_blk + j)), pl.BlockSpec((tm, tn), lambda i, j: (i, gs_blk + j)),
                  own, own],
        out_specs=own, compiler_params=_cparams(("parallel", "parallel"), 6 * _nbytes((tm, tn), F32)),
        name=name)(proj, proj, y_att, y_sg)


def _gate_bwd(proj, ga_blk, gs_blk, y_att, y_sg, dmerged, name):
    t, d = y_att.shape
    tm, tn = _rows(t, 512), _tile(d, GROUP_WIDTH)

    def body(ga_ref, gs_ref, ya_ref, ys_ref, dm_ref, dya_ref, dys_ref, dga_ref, dgs_ref):
        dm = dm_ref[...].astype(F32)
        for g_ref, y_ref, dy_ref, dg_ref in ((ga_ref, ya_ref, dya_ref, dga_ref), (gs_ref, ys_ref, dys_ref, dgs_ref)):
            sg = _sigmoid(g_ref[...].astype(F32))
            dy_ref[...] = (dm * sg).astype(BF16)
            dg_ref[...] = (dm * y_ref[...].astype(F32) * sg * (1.0 - sg)).astype(BF16)

    own = pl.BlockSpec((tm, tn), lambda i, j: (i, j))
    out = jax.ShapeDtypeStruct((t, d), BF16)
    return _pallas_call(
        body, out_shape=(out, out, out, out), grid=(t // tm, d // tn),
        in_specs=[pl.BlockSpec((tm, tn), lambda i, j: (i, ga_blk + j)), pl.BlockSpec((tm, tn), lambda i, j: (i, gs_blk + j)),
                  own, own, own],
        out_specs=(own, own, own, own), compiler_params=_cparams(("parallel", "parallel"), 10 * _nbytes((tm, tn), F32)),
        name=name)(proj, proj, y_att, y_sg, dmerged)


def _mixer_forward(x, wb, small, in_specs, sg_specs, out_specs):
    t, d = x.shape
    att_w = N_GROUPS * GROUP_WIDTH
    sg_w = SG_GROUPS * SG_GROUP_DIM
    n = _rmsnorm_fwd(x, small['mix_norm'], "mix_norm")
    rider, names = _gather(wb, in_specs)
    proj, got = _matmul([(n, _full(wb, 'w_in'))], 'nn', BF16, "mix_in", b3=True, caps=(1024, 1024, 1024), rider=rider)
    _landed(wb, names, got)
    tables = _rope_tables(t)
    qk0, grouped = _rope_fwd(proj, tables, "mix_rope")
    qkv = [(qk0, qk0, proj, (0, 1, 2 * N_GROUPS))] + [g + ((0, 0, 0),) for g in grouped]
    outs, lses = zip(*[_att_fwd(*args, dil, f"att_fwd{gi}") for gi, (args, dil) in enumerate(zip(qkv, DILATIONS))])
    o_b, o_f, lse = _att_combine(outs, lses, "att_combine")
    y_att = _matmul([(o_b, _full(wb, 'w_att_out'))], 'nn', BF16, "mix_att_out", b3=True)
    bias = jnp.repeat(small['sg_b'].T, SG_GROUP_DIM, axis=1)
    u_blk, vs_blk = 3 * att_w // sg_w, 3 * att_w // sg_w + 1
    su = _sg_fwd(proj, u_blk, vs_blk, small['sg_ln_g'], small['sg_ln_b'], small['sg_w'], bias, "sg_fwd")
    rider, names = _gather(wb, sg_specs)
    y_sg, got = _matmul([(su, _full(wb, 'w_sg_out'))], 'nn', BF16, "mix_sg_out", b3=True, rider=rider)
    _landed(wb, names, got)
    ga_blk = (3 * att_w + 2 * sg_w) // _tile(d, GROUP_WIDTH)
    gs_blk = ga_blk + d // _tile(d, GROUP_WIDTH)
    merged = _gate_fwd(proj, ga_blk, gs_blk, y_att, y_sg, "gate_fwd")
    rider, names = _gather(wb, out_specs)
    x_next, got = _matmul([(merged, _full(wb, 'w_out'))], 'nn', F32, "mix_out", residual=x, rider=rider)
    _landed(wb, names, got)
    saved = (n, proj, qkv, tables, o_b, o_f, lse, y_att, su, y_sg, merged, bias, (u_blk, vs_blk, ga_blk, gs_blk))
    return x_next, saved


def _mixer_backward(x, wb, small, saved, dx_next, dx_next_b, c_idx, first_rider, pending):
    n, proj, qkv, tables, o_b, o_f, lse, y_att, su, y_sg, merged, bias, (u_blk, vs_blk, ga_blk, gs_blk) = saved
    s = N_CHIPS
    dmerged, carried = _matmul([(dx_next_b, _full(wb, 'w_out'))], 'nt', BF16, "mix_out_dx", rider=first_rider)
    g_w_out = _matmul([(merged, dx_next_b)], 'tn', BF16, "mix_out_dw", caps=(1024, 1024, 1024))
    dy_att, dy_sg, dg_att, dg_sg = _gate_bwd(proj, ga_blk, gs_blk, y_att, y_sg, dmerged, "gate_bwd")

    g_w_att_out = _matmul([(o_b, dy_att)], 'tn', BF16, "mix_att_out_dw", out3=s)
    do = _matmul([(dy_att, _full(wb, 'w_att_out'))], 'nt', F32, "mix_att_out_dx", b3=True)
    delta = _att_delta(do, o_f, "att_delta")
    stats = [(do, lse, delta)] + _regroup([do, lse, delta], "att_regroup")
    dqkv = []
    for gi, ((q, k, v, offs), st, dil) in enumerate(zip(qkv, stats, DILATIONS)):
        dq = _att_bwd_dq(q, k, v, *st, offs, dil, f"att_bwd_dq{gi}")
        dk, dv = _att_bwd_dkv(q, k, v, *st, offs, dil, f"att_bwd_dkv{gi}")
        dqkv.append((dq, dk, dv))
    dqkv = _rope_bwd(*dqkv[0], dqkv[1:], tables, "mix_rope_bwd")

    g_w_sg_out = _matmul([(su, dy_sg)], 'tn', BF16, "mix_sg_out_dw", out3=s)
    out_names = ['w_out', 'w_att_out', 'w_sg_out']
    out_g4 = [a.reshape(wb[nm].shape) for a, nm in zip([g_w_out, g_w_att_out, g_w_sg_out], out_names)]
    dsu, from_sibling = _matmul([(dy_sg, _full(wb, 'w_sg_out'))], 'nt', BF16, "mix_sg_out_dx", b3=True, rider=_sibling_rider(out_g4))
    out_parts = [_sibling_sum(a, b, c_idx, f"rs_sum1_{nm}") for a, b, nm in zip(out_g4, from_sibling, out_names)]
    out_names, out_parts = out_names + list(pending), out_parts + list(pending.values())
    du, dvs, g_sg_w, g_bias, g_lg, g_lb = _sg_bwd(proj, u_blk, vs_blk, dsu, small['sg_ln_g'], small['sg_ln_b'],
                                                   small['sg_w'], bias, "sg_bwd")
    gs = {'sg_w': g_sg_w, 'sg_b': g_bias[:, ::SG_GROUP_DIM].T, 'sg_ln_g': g_lg, 'sg_ln_b': g_lb}

    dproj = jnp.concatenate([dqkv, du, dvs, dg_att, dg_sg], axis=1)
    g_w_in, out_recv = _matmul([(n, dproj)], 'tn', BF16, "mix_in_dw", out3=s, caps=(1024, 1024, 1024),
                               rider=_scatter_rider(out_parts))
    (p_w_in,) = _reduce_first([g_w_in], ['w_in'], wb, c_idx)
    dn, (r_w_in,) = _matmul([(dproj, _full(wb, 'w_in'))], 'nt', BF16, "mix_in_dx", b3=True, caps=(1024, 1024, 512),
                            rider=_scatter_rider([p_w_in]))
    dx, dx_b, gs['mix_norm'] = _rmsnorm_bwd(x, small['mix_norm'], dn, dx_next, "mix_norm_bwd")
    g = {nm: (p, r) for nm, p, r in zip(out_names, out_parts, out_recv)}
    g['w_in'] = (p_w_in, r_w_in)
    return dx, dx_b, g, gs, carried


def _step(x, target, wb, small, c_idx, pc_idx):
    def last_stage(g):
        names = list(g)
        return names, _halves_rider([_chip_sum(*g[n], pc_idx, f"rs_sum2_{n}") for n in names])

    wb = dict(wb)
    rider, names = _gather(wb, ['ffn1_w_gate', 'ffn1_w_up'])
    _landed(wb, names, _exchange(rider, "gather_first"))
    half_in = wb['w_in'].shape[2] // 2
    x1, s1 = _ffn_forward(x, small['ffn1_norm'], wb, "ffn1", ['ffn1_w_down', ('w_in', 0, half_in)], [('w_in', half_in, 2 * half_in)])
    up_rows = wb['ffn2_w_up'].shape[2]
    up_cut = up_rows // 32 * 15
    x2, s2 = _mixer_forward(x1, wb, small, ['w_att_out', 'w_sg_out', 'w_out', 'ffn2_w_gate'],
                            [('ffn2_w_up', 0, up_cut)], [('ffn2_w_up', up_cut, up_rows)])
    x3, s3 = _ffn_forward(x2, small['ffn2_norm'], wb, "ffn2", ['ffn2_w_down'], None)
    loss, dx3, dx3_b, g_final = _final_loss(x3, small['final_norm'], target, "final_loss")
    gs = {'final_norm': g_final}
    whole = {}
    dx2, dx2_b, gs['ffn2_norm'], g, pending, _ = _ffn_backward(x2, small['ffn2_norm'], wb, s3, dx3, dx3_b, c_idx, "ffn2", True)
    names, rider = last_stage(g)
    dx1, dx1_b, g, gs_mix, got = _mixer_backward(x1, wb, small, s2, dx2, dx2_b, c_idx, rider, pending)
    whole.update(zip(names, got))
    gs.update(gs_mix)
    names, rider = last_stage(g)
    dx0, _, gs['ffn1_norm'], g, _, got = _ffn_backward(x, small['ffn1_norm'], wb, s1, dx1, dx1_b, c_idx, "ffn1", False,
                                                       dwd_rider=rider)
    whole.update(zip(names, got))
    names, rider = last_stage(g)
    whole.update(zip(names, _exchange(rider, "rs_halves")))
    return loss, dx0, whole, gs


def _cast_into_gathered(wt, p_idx, name):
    r, ccols = wt.shape[0] // 2, wt.shape[1]
    tm = _rows(r, 512)
    nb = r // tm

    def body(p_ref, w_ref, o_ref):
        o_ref[...] = w_ref[...].astype(BF16)

    grid_spec = pltpu.PrefetchScalarGridSpec(
        num_scalar_prefetch=1, grid=(2, nb),
        in_specs=[pl.BlockSpec((tm, ccols), lambda h, i, pr: (h * nb + i, 0))],
        out_specs=pl.BlockSpec((None, None, tm, ccols), lambda h, i, pr: (pr[0], h, i, 0)))
    return pl.pallas_call(body, out_shape=jax.ShapeDtypeStruct((N_CHIPS, 2, r, ccols), BF16), grid_spec=grid_spec,
                          compiler_params=_cparams(("parallel", "parallel"), 2 * _nbytes((tm, ccols), F32)), name=name)(p_idx, wt)


def _sibling_rider(grads):
    n = len(grads)

    def copy(src, dst, sems, i):
        x, y, c, _ = _place()
        return pltpu.make_async_remote_copy(src[i].at[:, 1 - c], dst[i], sems[0].at[i], sems[1].at[i],
                                            device_id=(x, y, 1 - c), device_id_type=MESH)

    def start(src, dst, sems):
        for i in range(n):
            copy(src, dst, sems, i).start()

    def finish(src, dst, sems):
        for i in range(n):
            copy(src, dst, sems, i).wait()

    return _Rider(grads, [jax.ShapeDtypeStruct((g.shape[0],) + g.shape[2:], g.dtype) for g in grads], {},
                  [pltpu.SemaphoreType.DMA((n,))] * 2, start, finish)


def _merge_riders(a, b):
    n_in, n_out, n_sem = len(a.operands), len(a.out_shapes), len(a.sems)

    def both(which):
        def run(ins, outs, sems):
            getattr(a, which)(ins[:n_in], outs[:n_out], sems[:n_sem])
            getattr(b, which)(ins[n_in:], outs[n_out:], sems[n_sem:])
        return run

    aliases = dict(a.aliases)
    aliases.update({n_in + k: n_out + v for k, v in b.aliases.items()})
    return _Rider(list(a.operands) + list(b.operands), list(a.out_shapes) + list(b.out_shapes), aliases,
                  list(a.sems) + list(b.sems), both('start'), both('finish'))


def _halves_rider(bufs):
    n = len(bufs)

    def copy(ref, sems, i, c, x, y):
        return pltpu.make_async_remote_copy(ref, ref, sems[0].at[i], sems[1].at[i], device_id=(x, y, 1 - c), device_id_type=MESH)

    def start(_, buf, sems):
        x, y, c, _ = _place()
        for i in range(n):
            copy(buf[i].at[c], sems, i, c, x, y).start()

    def finish(_, buf, sems):
        x, y, c, _ = _place()
        for i in range(n):
            copy(buf[i].at[c], sems, i, c, x, y).wait_send()
            copy(buf[i].at[1 - c], sems, i, c, x, y).wait_recv()

    return _Rider(bufs, [jax.ShapeDtypeStruct(b.shape, b.dtype) for b in bufs], {i: i for i in range(n)},
                  [pltpu.SemaphoreType.DMA((n,))] * 2, start, finish)


def _sibling_sum(grad, recv, c_idx, name):
    s, _, r, ccols = grad.shape
    tm = _rows(r, 512)

    def body(c_ref, g_ref, r_ref, o_ref):
        o_ref[...] = (g_ref[...].astype(F32) + r_ref[...].astype(F32)).astype(BF16)

    grid_spec = pltpu.PrefetchScalarGridSpec(
        num_scalar_prefetch=1, grid=(s, r // tm),
        in_specs=[pl.BlockSpec((None, None, tm, ccols), lambda q, i, cr: (q, cr[0], i, 0)),
                  pl.BlockSpec((None, tm, ccols), lambda q, i, cr: (q, i, 0))],
        out_specs=pl.BlockSpec((None, tm, ccols), lambda q, i, cr: (q, i, 0)))
    return pl.pallas_call(body, out_shape=jax.ShapeDtypeStruct((s, r, ccols), BF16), grid_spec=grid_spec,
                          compiler_params=_cparams(("parallel", "parallel"), 4 * _nbytes((tm, ccols), F32)), name=name)(c_idx, grad, recv)


def _chip_sum(part, recv, pc_idx, name):
    _, r, ccols = part.shape
    tm = _rows(r, 512)

    def body(pc_ref, own_ref, r0_ref, r1_ref, r2_ref, o_ref):
        acc = own_ref[...].astype(F32) + r0_ref[...].astype(F32)
        acc = acc + r1_ref[...].astype(F32)
        o_ref[...] = acc + r2_ref[...].astype(F32)

    def slot(j):
        return pl.BlockSpec((None, tm, ccols), lambda i, pc: (j, i, 0))

    grid_spec = pltpu.PrefetchScalarGridSpec(
        num_scalar_prefetch=1, grid=(r // tm,),
        in_specs=[pl.BlockSpec((None, tm, ccols), lambda i, pc: (pc[0], i, 0)), slot(0), slot(1), slot(2)],
        out_specs=pl.BlockSpec((None, tm, ccols), lambda i, pc: (pc[1], i, 0)))
    return pl.pallas_call(body, out_shape=jax.ShapeDtypeStruct((2, r, ccols), F32), grid_spec=grid_spec,
                          compiler_params=_cparams(("parallel",), 6 * _nbytes((tm, ccols), F32)), name=name)(pc_idx, part, recv, recv, recv)


def _all_reduce_small(vec):
    _, r, _ = vec.shape

    def body(v_ref, o_ref, parts, send1, recv1, send2, recv2):
        x, y, c, _ = _place()
        me = 4 * x + 2 * y + c
        peers = []
        for k in range(1, N_DEV):
            px, py, pc = (1 - x if k & 4 else x, 1 - y if k & 2 else y, 1 - c if k & 1 else c)
            peers.append(((px, py, pc), 4 * px + 2 * py + pc))
        parts[me] = v_ref[me]
        cps = []
        for k, (peer, peer_id) in enumerate(peers):
            cp = pltpu.make_async_remote_copy(v_ref.at[peer_id], parts.at[me], send1.at[k], recv1.at[k],
                                              device_id=peer, device_id_type=MESH)
            cp.start()
            cps.append(cp)
        for cp in cps:
            cp.wait()
        acc = parts[0]
        for dev in range(1, N_DEV):
            acc = acc + parts[dev]
        o_ref[me] = acc
        cps = []
        for k, (peer, _) in enumerate(peers):
            cp = pltpu.make_async_remote_copy(o_ref.at[me], o_ref.at[me], send2.at[k], recv2.at[k],
                                              device_id=peer, device_id_type=MESH)
            cp.start()
            cps.append(cp)
        for cp in cps:
            cp.wait()

    vm = pl.BlockSpec(memory_space=pltpu.VMEM)
    sems = pltpu.SemaphoreType.DMA((N_DEV - 1,))
    return pl.pallas_call(
        body, out_shape=jax.ShapeDtypeStruct(vec.shape, F32), in_specs=[vm], out_specs=vm,
        scratch_shapes=[pltpu.VMEM((N_DEV, r, LANES), F32), sems, sems, sems, sems],
        compiler_params=pltpu.CompilerParams(vmem_limit_bytes=int(8 * _nbytes((N_DEV, r, LANES), F32))),
        name="all_reduce_small")(vec)


def _adamw(wt, g, m, v, name, rider=None):
    r, ccols = wt.shape
    tm = _rows(r, max(8, (2 * MIB // (4 * ccols)) // 8 * 8))
    blk = pl.BlockSpec((tm, ccols), lambda i: (i, 0))

    def body(w_ref, g_ref, m_ref, v_ref, go_ref, d_ref, mo_ref, vo_ref):
        gv = g_ref[...]
        go_ref[...] = gv
        mv = ADAM_B1 * m_ref[...] + (1.0 - ADAM_B1) * gv
        vv = ADAM_B2 * v_ref[...] + (1.0 - ADAM_B2) * (gv * gv)
        m_hat = mv / (1.0 - ADAM_B1 ** ADAM_STEP)
        v_hat = vv / (1.0 - ADAM_B2 ** ADAM_STEP)
        d_ref[...] = -ADAM_LR * (m_hat / (jnp.sqrt(v_hat) + ADAM_EPS) + ADAM_WD * w_ref[...])
        mo_ref[...] = mv
        vo_ref[...] = vv

    out = jax.ShapeDtypeStruct((r, ccols), F32)
    return _run(body, name=name, grid=(r // tm,), in_specs=[blk] * 4, out_specs=[blk] * 4, out_shape=[out] * 4, scratch_shapes=[],
                operands=[wt, g, m, v], block_bytes=8 * _nbytes((tm, ccols), F32), rider=rider, pinned=False)


def _as_rows(a):
    rows = a.reshape(-1, LANES)
    return jnp.pad(rows, ((0, -rows.shape[0] % 8), (0, 0)))


def kernel(x, ffn1_norm, ffn1_w_gate, ffn1_w_up, ffn1_w_down, mix_norm, w_in, sg_ln_g, sg_ln_b, sg_w, sg_b, w_att_out, w_sg_out, w_out, ffn2_norm, ffn2_w_gate, ffn2_w_up, ffn2_w_down, final_norm, loss_target, m_ffn1_norm, m_ffn1_w_gate, m_ffn1_w_up, m_ffn1_w_down, m_mix_norm, m_w_in, m_sg_ln_g, m_sg_ln_b, m_sg_w, m_sg_b, m_w_att_out, m_w_sg_out, m_w_out, m_ffn2_norm, m_ffn2_w_gate, m_ffn2_w_up, m_ffn2_w_down, m_final_norm, v_ffn1_norm, v_ffn1_w_gate, v_ffn1_w_up, v_ffn1_w_down, v_mix_norm, v_w_in, v_sg_ln_g, v_sg_ln_b, v_sg_w, v_sg_b, v_w_att_out, v_w_sg_out, v_w_out, v_ffn2_norm, v_ffn2_w_gate, v_ffn2_w_up, v_ffn2_w_down, v_final_norm):
    given = dict(locals())
    wts = {n: given[n] for n in WEIGHT_NAMES}
    ms = {n: given["m_" + n] for n in WEIGHT_NAMES}
    vs = {n: given["v_" + n] for n in WEIGHT_NAMES}
    t, d = x.shape[-2], x.shape[-1]
    xc, yc, cc = lax.axis_index("x"), lax.axis_index("y"), lax.axis_index("c")

    shard2d = {n: wts[n].reshape(wts[n].shape[-2:]) for n in BIG_NAMES}
    p_idx = jnp.reshape(2 * xc + yc, (1,)).astype(jnp.int32)
    c_idx = jnp.reshape(cc, (1,)).astype(jnp.int32)
    pc_idx = jnp.stack([2 * xc + yc, cc]).astype(jnp.int32)
    wb = {n: _cast_into_gathered(shard2d[n], p_idx, f"cast_{n}") for n in BIG_NAMES}

    small = {n: wts[n].reshape(-1, wts[n].shape[-1]) for n in SMALL_NAMES}
    small['sg_w'] = wts['sg_w'].reshape(wts['sg_w'].shape[-3:])
    loss, dx, whole, gs = _step(x.reshape(t, d), loss_target.reshape(t, d), wb, small, c_idx, pc_idx)
    loss = lax.psum(loss[0, 0], ("x", "y", "c"))

    def pack(tree):
        rows = jnp.concatenate([_as_rows(tree[n]) for n in SMALL_NAMES], axis=0)
        return jnp.pad(rows, ((0, -rows.shape[0] % (8 * N_DEV)), (0, 0)))

    packed = pack(gs)
    packed = _all_reduce_small(packed.reshape(N_DEV, -1, LANES)).reshape(packed.shape)

    grads, delta, new_m, new_v = {}, {}, {}, {}
    for n in BIG_NAMES:
        shape, flat = wts[n].shape, shard2d[n].shape
        out = _adamw(shard2d[n], whole[n].reshape(flat), ms[n].reshape(flat), vs[n].reshape(flat), f"adamw_{n}")
        grads[n], delta[n], new_m[n], new_v[n] = (a.reshape(shape) for a in out)

    small_out = _adamw(pack(wts), packed, pack(ms), pack(vs), "adamw_small")
    row = 0
    for n in SMALL_NAMES:
        shape = wts[n].shape
        sz = wts[n].size // LANES
        grads[n], delta[n], new_m[n], new_v[n] = (a[row:row + sz].reshape(shape) for a in small_out)
        row += sz + -sz % 8

    return (loss, dx.reshape(x.shape), *[grads[n] for n in WEIGHT_NAMES], *[delta[n] for n in WEIGHT_NAMES],
            *[new_m[n] for n in WEIGHT_NAMES], *[new_v[n] for n in WEIGHT_NAMES])
```

```python
import functools

import jax
import jax.numpy as jnp
from jax import lax
from jax.experimental import pallas as pl
from jax.experimental.pallas import tpu as pltpu

F32 = jnp.float32
BF16 = jnp.bfloat16
MESH = pl.DeviceIdType.MESH

NORM_EPS = 1e-6
LN_EPS = 1e-5
HEAD_DIM = 128
HEADS_PER_GROUP = 4
GROUP_WIDTH = HEADS_PER_GROUP * HEAD_DIM
DILATIONS = (1, 4, 16)
N_GROUPS = len(DILATIONS)
ATT_BLOCK = 128
ROPE_DIM = HEAD_DIM // 4
ROPE_THETA = 500000.0
SG_CHUNK = 128
SG_GROUPS = 12
SG_GROUP_DIM = 128
MASKED = -1e30

ADAM_LR = 0.001
ADAM_B1 = 0.9
ADAM_B2 = 0.999
ADAM_EPS = 1e-08
ADAM_WD = 0.01
ADAM_STEP = 10

N_CHIPS = 4
N_DEV = 8
LANES = 128
MIB = 2 ** 20
VMEM_BYTES_V7X = 64 * MIB

WEIGHT_NAMES = ['ffn1_norm', 'ffn1_w_gate', 'ffn1_w_up', 'ffn1_w_down', 'mix_norm', 'w_in', 'sg_ln_g', 'sg_ln_b',
                'sg_w', 'sg_b', 'w_att_out', 'w_sg_out', 'w_out', 'ffn2_norm', 'ffn2_w_gate', 'ffn2_w_up',
                'ffn2_w_down', 'final_norm']
BIG = [('ffn1_w_gate', 1), ('ffn1_w_up', 1), ('ffn1_w_down', 0), ('w_in', 1), ('w_att_out', 1), ('w_sg_out', 1),
       ('w_out', 0), ('ffn2_w_gate', 1), ('ffn2_w_up', 1), ('ffn2_w_down', 0)]
BIG_NAMES = [n for n, _ in BIG]
SMALL_NAMES = [n for n in WEIGHT_NAMES if n not in BIG_NAMES]


def _nbytes(shape, dtype):
    n = jnp.dtype(dtype).itemsize
    for s in shape:
        if s is not None:
            n *= s
    return n


def _pallas_call(*args, **kw):
    kw['out_shape'] = jax.tree.map(lambda s: pltpu.HBM(s.shape, s.dtype), kw['out_shape'])
    call = pl.pallas_call(*args, **kw)

    def pinned(*operands):
        return call(*[o if jnp.issubdtype(o.dtype, jnp.integer) else pltpu.with_memory_space_constraint(o, pltpu.HBM)
                      for o in operands])

    return pinned


def _cparams(sem, block_bytes, **kw):
    limit = int(min(max(3 * block_bytes, 32 * MIB), VMEM_BYTES_V7X - 8 * MIB))
    return pltpu.CompilerParams(dimension_semantics=sem, vmem_limit_bytes=limit, **kw)


def _tile(dim, cap):
    best = None
    for t in range(LANES, min(dim, cap) + 1, LANES):
        if dim % t == 0:
            best = t
    if best is None:
        assert dim <= cap, (dim, cap)
        return dim
    return best


def _rows(dim, cap):
    best = None
    for t in range(8, min(dim, cap) + 1, 8):
        if dim % t == 0:
            best = t
    assert best is not None, (dim, cap)
    return best


def _place():
    x, y, c = lax.axis_index("x"), lax.axis_index("y"), lax.axis_index("c")
    others = [(1 - x, y), (x, 1 - y), (1 - x, 1 - y)]
    return x, y, c, others


class _Rider:
    def __init__(self, operands, out_shapes, aliases, sems, start, finish):
        self.operands = operands
        self.out_shapes = out_shapes
        self.aliases = aliases
        self.sems = sems
        self.start = start
        self.finish = finish


def _run(body, *, name, grid, in_specs, out_specs, out_shape, scratch_shapes, operands, block_bytes, rider=None, pinned=True):
    call = _pallas_call if pinned else pl.pallas_call
    if rider is None:
        sem = ("parallel",) * (len(grid) - 1) + ("arbitrary",)
        return call(body, out_shape=out_shape, grid=grid, in_specs=in_specs, out_specs=out_specs,
                    scratch_shapes=scratch_shapes, compiler_params=_cparams(sem, block_bytes), name=name)(*operands)
    n_in, n_out, n_scr = len(operands), len(out_shape), len(scratch_shapes)
    r_in, r_out = len(rider.operands), len(rider.out_shapes)
    any_spec = pl.BlockSpec(memory_space=pl.ANY)

    def wrapped(*refs):
        ins, refs = refs[:n_in], refs[n_in:]
        r_ins, refs = refs[:r_in], refs[r_in:]
        outs, refs = refs[:n_out], refs[n_out:]
        r_outs, refs = refs[:r_out], refs[r_out:]
        scr, sems = refs[:n_scr], refs[n_scr:]
        if not grid:
            rider.start(r_ins, r_outs, sems)
            rider.finish(r_ins, r_outs, sems)
            return
        ids = [pl.program_id(a) for a in range(len(grid))]
        first = functools.reduce(jnp.logical_and, [i == 0 for i in ids])
        last = functools.reduce(jnp.logical_and, [i == g - 1 for i, g in zip(ids, grid)])

        @pl.when(first)
        def _():
            rider.start(r_ins, r_outs, sems)

        body(*ins, *outs, *scr)

        @pl.when(last)
        def _():
            rider.finish(r_ins, r_outs, sems)

    results = call(
        wrapped, out_shape=list(out_shape) + list(rider.out_shapes), grid=grid,
        in_specs=list(in_specs) + [any_spec] * r_in, out_specs=list(out_specs) + [any_spec] * r_out,
        scratch_shapes=list(scratch_shapes) + list(rider.sems),
        input_output_aliases={n_in + k: n_out + v for k, v in rider.aliases.items()},
        compiler_params=_cparams(("arbitrary",) * len(grid) if grid else None, block_bytes, has_side_effects=True),
        name=name)(*operands, *rider.operands)
    return results[:n_out], results[n_out:]


def _exchange(rider, name):
    return _run(None, name=name, grid=(), in_specs=[], out_specs=[], out_shape=[], scratch_shapes=[], operands=[],
                block_bytes=0, rider=rider)[1]


def _gather_rider(items):
    bufs, index = [], []
    for b, r0, r1 in items:
        if not any(b is q for q in bufs):
            bufs.append(b)
        index.append(([k for k, q in enumerate(bufs) if q is b][0], r0, r1))
    n = len(index)

    def piece(refs, k, chip, half):
        bi, r0, r1 = index[k]
        return refs[bi].at[chip, half, pl.ds(r0, r1 - r0)]

    def copy(ref, sem_pair, k, j, to):
        return pltpu.make_async_remote_copy(ref, ref, sem_pair[0].at[k, j], sem_pair[1].at[k, j], device_id=to, device_id_type=MESH)

    def start(r_ins, buf, sems):
        x, y, c, others = _place()
        for k in range(n):
            for j, (ox, oy) in enumerate(others):
                copy(piece(buf, k, 2 * x + y, c), sems[:2], k, j, (ox, oy, c)).start()

    def finish(r_ins, buf, sems):
        x, y, c, others = _place()
        for k in range(n):
            for j, (ox, oy) in enumerate(others):
                got = piece(buf, k, 2 * ox + oy, c)
                copy(got, sems[:2], k, j, (ox, oy, c)).wait_recv()
                copy(got, sems[2:], k, j, (x, y, 1 - c)).start()
        for k in range(n):
            for j, (ox, oy) in enumerate(others):
                copy(piece(buf, k, 2 * ox + oy, 1 - c), sems[2:], k, j, (x, y, 1 - c)).wait_recv()
        for k in range(n):
            for j, (ox, oy) in enumerate(others):
                copy(piece(buf, k, 2 * x + y, c), sems[:2], k, j, (ox, oy, c)).wait_send()
                copy(piece(buf, k, 2 * ox + oy, c), sems[2:], k, j, (x, y, 1 - c)).wait_send()

    return _Rider(bufs, [jax.ShapeDtypeStruct(b.shape, b.dtype) for b in bufs], {i: i for i in range(len(bufs))},
                  [pltpu.SemaphoreType.DMA((n, 3))] * 4, start, finish)


def _scatter_rider(parts):
    n = len(parts)

    def copy(src, dst, sems, i, j, to):
        return pltpu.make_async_remote_copy(src, dst, sems[0].at[i, j], sems[1].at[i, j], device_id=to, device_id_type=MESH)

    def start(src, dst, sems):
        x, y, c, others = _place()
        for i in range(n):
            for j, (ox, oy) in enumerate(others):
                copy(src[i].at[2 * ox + oy], dst[i].at[j], sems, i, j, (ox, oy, c)).start()

    def finish(src, dst, sems):
        x, y, c, others = _place()
        for i in range(n):
            for j, (ox, oy) in enumerate(others):
                copy(src[i].at[2 * ox + oy], dst[i].at[j], sems, i, j, (ox, oy, c)).wait()

    return _Rider(parts, [jax.ShapeDtypeStruct((3,) + p.shape[1:], p.dtype) for p in parts], {},
                  [pltpu.SemaphoreType.DMA((n, 3))] * 2, start, finish)


def _matmul(pairs, mode, out_dtype, name, *, scale=1.0, residual=None, b3=False, out3=0, caps=(1024, 1024, 512), rider=None):
    a0, b0 = pairs[0]
    if mode == 'nn':
        m, k = a0.shape
        n = b0.shape[0] * b0.shape[2] if b3 else b0.shape[1]
    elif mode == 'nt':
        m = a0.shape[0]
        n, k = (b0.shape[1], b0.shape[0] * b0.shape[2]) if b3 else b0.shape
    else:
        k, m = a0.shape
        n = b0.shape[1]
    tm = _tile(m, caps[0])
    tn = _tile(n, caps[1])
    tk = _tile(k, caps[2])
    if b3 and mode == 'nn':
        tn = b0.shape[2]
    if b3 and mode == 'nt':
        tk = b0.shape[2]
    if out3:
        tn = n // out3
    nk = k // tk
    if mode == 'tn':
        a_spec = pl.BlockSpec((tk, tm), lambda i, j, kk: (kk, i))
        b_spec = pl.BlockSpec((tk, tn), lambda i, j, kk: (kk, j))
        dims = ((0,), (0,))
    elif mode == 'nn':
        a_spec = pl.BlockSpec((tm, tk), lambda i, j, kk: (i, kk))
        b_spec = (pl.BlockSpec((None, tk, tn), lambda i, j, kk: (j, kk, 0)) if b3
                  else pl.BlockSpec((tk, tn), lambda i, j, kk: (kk, j)))
        dims = ((1,), (0,))
    else:
        a_spec = pl.BlockSpec((tm, tk), lambda i, j, kk: (i, kk))
        b_spec = (pl.BlockSpec((None, tn, tk), lambda i, j, kk: (kk, j, 0)) if b3
                  else pl.BlockSpec((tn, tk), lambda i, j, kk: (j, kk)))
        dims = ((1,), (1,))
    in_specs, operands = [], []
    for a, b in pairs:
        in_specs += [a_spec, b_spec]
        operands += [a, b]
    block_bytes = len(pairs) * (_nbytes((tm, tk), a0.dtype) + _nbytes((tk, tn), b0.dtype))
    if residual is not None:
        in_specs.append(pl.BlockSpec((tm, tn), lambda i, j, kk: (i, j)))
        operands.append(residual)
        block_bytes += _nbytes((tm, tn), F32)
    if out3:
        out_spec = pl.BlockSpec((None, tm, tn), lambda i, j, kk: (j, i, 0))
        out_shape = jax.ShapeDtypeStruct((out3, m, tn), out_dtype)
    else:
        out_spec = pl.BlockSpec((tm, tn), lambda i, j, kk: (i, j))
        out_shape = jax.ShapeDtypeStruct((m, n), out_dtype)
    block_bytes += _nbytes((tm, tn), out_dtype) + _nbytes((tm, tn), F32)
    n_pairs = len(pairs)
    has_res = residual is not None

    def body(*refs):
        o_ref, acc = refs[-2], refs[-1]
        kk = pl.program_id(2)

        def product():
            part = None
            for p in range(n_pairs):
                d = lax.dot_general(refs[2 * p][...].astype(BF16), refs[2 * p + 1][...].astype(BF16),
                                    (dims, ((), ())), preferred_element_type=F32)
                part = d if part is None else part + d
            return part

        def finish(r):
            if scale != 1.0:
                r = r * scale
            if has_res:
                r = refs[2 * n_pairs][...] + r
            o_ref[...] = r.astype(out_dtype)

        if nk == 1:
            finish(product())
            return

        @pl.when(kk == 0)
        def _():
            acc[...] = product()

        if nk > 2:
            @pl.when(jnp.logical_and(kk > 0, kk < nk - 1))
            def _():
                acc[...] += product()

        @pl.when(kk == nk - 1)
        def _():
            finish(acc[...] + product())

    res = _run(body, name=name, grid=(m // tm, n // tn, nk), in_specs=in_specs, out_specs=[out_spec], out_shape=[out_shape],
               scratch_shapes=[pltpu.VMEM((tm, tn), F32)], operands=operands, block_bytes=block_bytes, rider=rider)
    return res[0] if rider is None else (res[0][0], res[1])


def _rmsnorm_fwd(x, g, name):
    t, d = x.shape
    tm = _rows(t, 512)

    def body(x_ref, g_ref, o_ref):
        xv = x_ref[...]
        r = lax.rsqrt(jnp.mean(xv * xv, axis=1, keepdims=True) + NORM_EPS)
        o_ref[...] = (xv * r * g_ref[...]).astype(BF16)

    row = pl.BlockSpec((tm, d), lambda i: (i, 0))
    return _pallas_call(
        body, out_shape=jax.ShapeDtypeStruct((t, d), BF16), grid=(t // tm,),
        in_specs=[row, pl.BlockSpec((1, d), lambda i: (0, 0))], out_specs=row,
        compiler_params=_cparams(("parallel",), 2 * _nbytes((tm, d), F32)), name=name)(x, g)


def _rms_grad(xv, g, dn, d):
    r = lax.rsqrt(jnp.mean(xv * xv, axis=1, keepdims=True) + NORM_EPS)
    u = dn * g
    s = jnp.sum(xv * u, axis=1, keepdims=True)
    dx = r * u - xv * (r * r * r) * (s * (1.0 / d))
    return dx, dn * xv * r


def _rmsnorm_bwd(x, g, dn, dres, name):
    t, d = x.shape
    tm = _rows(t, 256)

    def body(x_ref, g_ref, dn_ref, dres_ref, dx_ref, dxb_ref, dg_ref):
        dx, dg_rows = _rms_grad(x_ref[...], g_ref[...], dn_ref[...].astype(F32), d)
        dx = dres_ref[...] + dx
        dx_ref[...] = dx
        dxb_ref[...] = dx.astype(BF16)

        @pl.when(pl.program_id(0) == 0)
        def _():
            dg_ref[...] = jnp.zeros_like(dg_ref)

        dg_ref[...] += jnp.sum(dg_rows, axis=0, keepdims=True)

    row = pl.BlockSpec((tm, d), lambda i: (i, 0))
    vec = pl.BlockSpec((1, d), lambda i: (0, 0))
    return _pallas_call(
        body, out_shape=(jax.ShapeDtypeStruct((t, d), F32), jax.ShapeDtypeStruct((t, d), BF16), jax.ShapeDtypeStruct((1, d), F32)),
        grid=(t // tm,), in_specs=[row, vec, row, row], out_specs=(row, row, vec),
        compiler_params=_cparams(("arbitrary",), 5 * _nbytes((tm, d), F32)), name=name)(x, g, dn, dres)


def _final_loss(x, g, target, name):
    t, d = x.shape
    tm = _rows(t, 256)

    def body(x_ref, g_ref, t_ref, loss_ref, dx_ref, dxb_ref, dg_ref):
        xv, gv = x_ref[...], g_ref[...]
        r = lax.rsqrt(jnp.mean(xv * xv, axis=1, keepdims=True) + NORM_EPS)
        err = xv * r * gv - t_ref[...]
        dx, dg_rows = _rms_grad(xv, gv, err * (1.0 / d), d)
        dx_ref[...] = dx
        dxb_ref[...] = dx.astype(BF16)

        @pl.when(pl.program_id(0) == 0)
        def _():
            dg_ref[...] = jnp.zeros_like(dg_ref)
            loss_ref[...] = jnp.zeros_like(loss_ref)

        dg_ref[...] += jnp.sum(dg_rows, axis=0, keepdims=True)
        row_loss = jnp.sum(err * err, axis=1, keepdims=True) * (0.5 / d)
        loss_ref[...] += jnp.sum(row_loss, axis=0, keepdims=True)

    row = pl.BlockSpec((tm, d), lambda i: (i, 0))
    vec = pl.BlockSpec((1, d), lambda i: (0, 0))
    return _pallas_call(
        body, out_shape=(jax.ShapeDtypeStruct((1, 1), F32), jax.ShapeDtypeStruct((t, d), F32),
                         jax.ShapeDtypeStruct((t, d), BF16), jax.ShapeDtypeStruct((1, d), F32)),
        grid=(t // tm,), in_specs=[row, vec, row], out_specs=(pl.BlockSpec((1, 1), lambda i: (0, 0)), row, row, vec),
        compiler_params=_cparams(("arbitrary",), 4 * _nbytes((tm, d), F32)), name=name)(x, g, target)


def _sigmoid(x):
    return 0.5 * jnp.tanh(0.5 * x) + 0.5


def _ffn_up(n, wg, wu, name, rider=None):
    t, d = n.shape
    s, _, f = wg.shape
    tm, tk = _tile(t, 1024), _tile(d, 1024)
    nk = d // tk

    def body(n_ref, wg_ref, wu_ref, a_ref, b_ref, h_ref, acc_g, acc_u):
        kk = pl.program_id(2)

        def products():
            nv = n_ref[...]
            return jnp.dot(nv, wg_ref[...], preferred_element_type=F32), jnp.dot(nv, wu_ref[...], preferred_element_type=F32)

        def finish(a, b):
            a_ref[...] = a.astype(BF16)
            b_ref[...] = b.astype(BF16)
            h_ref[...] = (a * _sigmoid(a) * b).astype(BF16)

        if nk == 1:
            finish(*products())
            return

        @pl.when(kk == 0)
        def _():
            acc_g[...], acc_u[...] = products()

        if nk > 2:
            @pl.when(jnp.logical_and(kk > 0, kk < nk - 1))
            def _():
                pg, pu = products()
                acc_g[...] += pg
                acc_u[...] += pu

        @pl.when(kk == nk - 1)
        def _():
            pg, pu = products()
            finish(acc_g[...] + pg, acc_u[...] + pu)

    w_spec = pl.BlockSpec((None, tk, f), lambda i, j, kk: (j, kk, 0))
    o_spec = pl.BlockSpec((tm, f), lambda i, j, kk: (i, j))
    out = jax.ShapeDtypeStruct((t, s * f), BF16)
    block_bytes = _nbytes((tm, tk), BF16) + 2 * _nbytes((tk, f), BF16) + 3 * _nbytes((tm, f), BF16) + 2 * _nbytes((tm, f), F32)
    return _run(body, name=name, grid=(t // tm, s, nk),
                in_specs=[pl.BlockSpec((tm, tk), lambda i, j, kk: (i, kk)), w_spec, w_spec], out_specs=[o_spec, o_spec, o_spec],
                out_shape=[out, out, out], scratch_shapes=[pltpu.VMEM((tm, f), F32), pltpu.VMEM((tm, f), F32)],
                operands=[n, wg, wu], block_bytes=block_bytes, rider=rider)


def _ffn_bwd_act(dx, wd, a, b, name):
    t, d = dx.shape
    f = wd.shape[0]
    tm, tn, tk = _tile(t, 1024), _tile(f, 1536), _tile(d, 1024)
    nk = d // tk

    def body(dx_ref, wd_ref, a_ref, b_ref, da_ref, db_ref, acc):
        kk = pl.program_id(2)

        def product():
            return lax.dot_general(dx_ref[...], wd_ref[...], (((1,), (1,)), ((), ())), preferred_element_type=F32)

        def finish(r):
            dh = 0.5 * r
            av, bv = a_ref[...].astype(F32), b_ref[...].astype(F32)
            sg = _sigmoid(av)
            da_ref[...] = (dh * bv * (sg * (1.0 + av * (1.0 - sg)))).astype(BF16)
            db_ref[...] = (dh * (av * sg)).astype(BF16)

        if nk == 1:
            finish(product())
            return

        @pl.when(kk == 0)
        def _():
            acc[...] = product()

        if nk > 2:
            @pl.when(jnp.logical_and(kk > 0, kk < nk - 1))
            def _():
                acc[...] += product()

        @pl.when(kk == nk - 1)
        def _():
            finish(acc[...] + product())

    act = pl.BlockSpec((tm, tn), lambda i, j, kk: (i, j))
    out = jax.ShapeDtypeStruct((t, f), BF16)
    block_bytes = _nbytes((tm, tk), BF16) + _nbytes((tn, tk), BF16) + 4 * _nbytes((tm, tn), BF16) + _nbytes((tm, tn), F32)
    return _pallas_call(
        body, out_shape=(out, out), grid=(t // tm, f // tn, nk),
        in_specs=[pl.BlockSpec((tm, tk), lambda i, j, kk: (i, kk)), pl.BlockSpec((tn, tk), lambda i, j, kk: (j, kk)),
                  act, act],
        out_specs=(act, act), scratch_shapes=[pltpu.VMEM((tm, tn), F32)],
        compiler_params=_cparams(("parallel", "parallel", "arbitrary"), block_bytes), name=name)(dx, wd, a, b)


AXIS = dict(BIG)


def _full(wb, n):
    _, _, r, ccols = wb[n].shape
    return wb[n].reshape(N_CHIPS, 2 * r, ccols) if AXIS[n] == 1 else wb[n].reshape(N_CHIPS * 2 * r, ccols)


def _gather(wb, specs):
    items, names = [], []
    for s in specs:
        n, r0, r1 = (s, 0, wb[s].shape[2]) if isinstance(s, str) else s
        items.append((wb[n], r0, r1))
        if n not in names:
            names.append(n)
    return _gather_rider(items), names


def _landed(wb, names, results):
    for n, r in zip(names, results):
        wb[n] = r


def _reduce_first(grads, names, wb, c_idx):
    g4 = [g.reshape(wb[n].shape) for g, n in zip(grads, names)]
    from_sibling = _exchange(_sibling_rider(g4), "rs_sibling_" + names[0])
    return [_sibling_sum(a, b, c_idx, f"rs_sum1_{n}") for a, b, n in zip(g4, from_sibling, names)]


def _ffn_forward(x, gain, wb, tag, up_specs, down_specs):
    n = _rmsnorm_fwd(x, gain, f"{tag}_norm")
    rider, names = _gather(wb, up_specs)
    (a, b, h), got = _ffn_up(n, _full(wb, f"{tag}_w_gate"), _full(wb, f"{tag}_w_up"), f"{tag}_up", rider=rider)
    _landed(wb, names, got)
    down = dict(scale=0.5, residual=x, caps=(1024, 1024, 1536))
    if down_specs:
        rider, names = _gather(wb, down_specs)
        x_next, got = _matmul([(h, _full(wb, f"{tag}_w_down"))], 'nn', F32, f"{tag}_down", rider=rider, **down)
        _landed(wb, names, got)
    else:
        x_next = _matmul([(h, _full(wb, f"{tag}_w_down"))], 'nn', F32, f"{tag}_down", **down)
    return x_next, (n, a, b, h)


def _ffn_backward(x, gain, wb, saved, dx_next, dx_next_b, c_idx, tag, chained, dwd_rider=None):
    n, a, b, h = saved
    wg, wu, wd = (f"{tag}_w_gate", f"{tag}_w_up", f"{tag}_w_down")
    da, db = _ffn_bwd_act(dx_next_b, _full(wb, wd), a, b, f"{tag}_bwd_act")
    res = _matmul([(h, dx_next_b)], 'tn', BF16, f"{tag}_dwd", scale=0.5, caps=(1536, 2048, 1024), rider=dwd_rider)
    g_wd, carried = (res, ()) if dwd_rider is None else res
    grad_mm = dict(out3=N_CHIPS, caps=(2048, 1024, 1024))
    dn_pairs = [(da, _full(wb, wg)), (db, _full(wb, wu))]
    if not chained:
        g_wd = g_wd.reshape(wb[wd].shape)
        g_wg, (s_wd,) = _matmul([(n, da)], 'tn', BF16, f"{tag}_dwg", rider=_sibling_rider([g_wd]), **grad_mm)
        p_wd = _sibling_sum(g_wd, s_wd, c_idx, f"rs_sum1_{wd}")
        g_wg = g_wg.reshape(wb[wg].shape)
        g_wu, (r_wd, s_wg) = _matmul([(n, db)], 'tn', BF16, f"{tag}_dwu",
                                     rider=_merge_riders(_scatter_rider([p_wd]), _sibling_rider([g_wg])), **grad_mm)
        p_wg = _sibling_sum(g_wg, s_wg, c_idx, f"rs_sum1_{wg}")
        (p_wu,) = _reduce_first([g_wu], [wu], wb, c_idx)
        dn, (r_wg, r_wu) = _matmul(dn_pairs, 'nt', BF16, f"{tag}_dn", b3=True, rider=_scatter_rider([p_wg, p_wu]))
        done, pending = {wg: (p_wg, r_wg), wu: (p_wu, r_wu), wd: (p_wd, r_wd)}, {}
    else:
        g_wd = g_wd.reshape(wb[wd].shape)
        g_wg, (s_wd,) = _matmul([(n, da)], 'tn', BF16, f"{tag}_dwg", rider=_sibling_rider([g_wd]), **grad_mm)
        p_wd = _sibling_sum(g_wd, s_wd, c_idx, f"rs_sum1_{wd}")
        g_wg = g_wg.reshape(wb[wg].shape)
        g_wu, (r_wd, s_wg) = _matmul([(n, db)], 'tn', BF16, f"{tag}_dwu",
                                     rider=_merge_riders(_scatter_rider([p_wd]), _sibling_rider([g_wg])), **grad_mm)
        p_wg = _sibling_sum(g_wg, s_wg, c_idx, f"rs_sum1_{wg}")
        g_wu = g_wu.reshape(wb[wu].shape)
        dn, (r_wg, s_wu) = _matmul(dn_pairs, 'nt', BF16, f"{tag}_dn", b3=True,
                                   rider=_merge_riders(_scatter_rider([p_wg]), _sibling_rider([g_wu])))
        p_wu = _sibling_sum(g_wu, s_wu, c_idx, f"rs_sum1_{wu}")
        done, pending = {wg: (p_wg, r_wg), wd: (p_wd, r_wd)}, {wu: p_wu}
    dx, dx_b, g_gain = _rmsnorm_bwd(x, gain, dn, dx_next, f"{tag}_norm_bwd")
    return dx, dx_b, g_gain, done, pending, carried


def _rope_tables(seq):
    half = ROPE_DIM // 2
    inv_freq = ROPE_THETA ** (-jnp.arange(0, ROPE_DIM, 2, dtype=F32) / ROPE_DIM)
    ang = jnp.arange(seq).astype(F32)[:, None] * inv_freq[None, :]
    cos, sin = jnp.cos(ang), jnp.sin(ang)
    zeros = lambda w: jnp.zeros((seq, w), F32)
    c = jnp.concatenate([cos, cos, jnp.ones((seq, HEAD_DIM - ROPE_DIM), F32)], axis=1)
    s_up = jnp.concatenate([-sin, zeros(HEAD_DIM - half)], axis=1)
    s_dn = jnp.concatenate([zeros(half), sin, zeros(HEAD_DIM - ROPE_DIM)], axis=1)
    return c, s_up, s_dn


def _rotate(xv, cv, uv, dv):
    half = ROPE_DIM // 2
    return xv * cv + pltpu.roll(xv, HEAD_DIM - half, 1) * uv + pltpu.roll(xv, half, 1) * dv


def _stage(tm):
    return pltpu.VMEM((HEADS_PER_GROUP, tm, HEAD_DIM), F32)


def _to_groups(stage, o_ref, dil):
    rows = stage.shape[1] // dil
    for r in range(dil):
        for h in range(HEADS_PER_GROUP):
            col = r * GROUP_WIDTH + h * HEAD_DIM
            o_ref[:, col:col + HEAD_DIM] = stage[h, pl.ds(r, rows, stride=dil), :].astype(o_ref.dtype)


def _from_groups(g_ref, stage, dil):
    rows = stage.shape[1] // dil
    for r in range(dil):
        for h in range(HEADS_PER_GROUP):
            col = r * GROUP_WIDTH + h * HEAD_DIM
            stage[h, pl.ds(r, rows, stride=dil), :] = g_ref[:, col:col + HEAD_DIM].astype(F32)


def _group_spec(tm, dil):
    return pl.BlockSpec((tm // dil, dil * GROUP_WIDTH), lambda i: (i, 0))


def _group_shape(t, dil, dtype):
    return jax.ShapeDtypeStruct((t // dil, dil * GROUP_WIDTH), dtype)


def _rope_fwd(proj, tables, name):
    t = proj.shape[0]
    tm = _rows(t, 512)
    att_w = N_GROUPS * GROUP_WIDTH
    dilated = [(gi, dil) for gi, dil in enumerate(DILATIONS) if dil > 1]

    def body(x_ref, c_ref, up_ref, dn_ref, qk0_ref, *rest):
        outs, stage = rest[:-1], rest[-1]
        cv, uv, dv = c_ref[...], up_ref[...], dn_ref[...]
        for part in range(2):
            for gi, dil in enumerate(DILATIONS):
                for h in range(HEADS_PER_GROUP):
                    col = part * att_w + gi * GROUP_WIDTH + h * HEAD_DIM
                    y = _rotate(x_ref[:, col:col + HEAD_DIM].astype(F32), cv, uv, dv)
                    if dil == 1:
                        qk0_ref[:, part * GROUP_WIDTH + h * HEAD_DIM:part * GROUP_WIDTH + (h + 1) * HEAD_DIM] = y.astype(BF16)
                    else:
                        stage[h] = y
                if dil > 1:
                    _to_groups(stage, outs[3 * dilated.index((gi, dil)) + part], dil)
        for n, (gi, dil) in enumerate(dilated):
            col = 2 * att_w + gi * GROUP_WIDTH
            for h in range(HEADS_PER_GROUP):
                stage[h] = x_ref[:, col + h * HEAD_DIM:col + (h + 1) * HEAD_DIM].astype(F32)
            _to_groups(stage, outs[3 * n + 2], dil)

    tab = pl.BlockSpec((tm, HEAD_DIM), lambda i: (i, 0))
    out_shape = [jax.ShapeDtypeStruct((t, 2 * GROUP_WIDTH), BF16)]
    out_specs = [pl.BlockSpec((tm, 2 * GROUP_WIDTH), lambda i: (i, 0))]
    for _, dil in dilated:
        out_shape += [_group_shape(t, dil, BF16)] * 3
        out_specs += [_group_spec(tm, dil)] * 3
    res = _pallas_call(
        body, out_shape=out_shape, grid=(t // tm,),
        in_specs=[pl.BlockSpec((tm, 3 * att_w), lambda i: (i, 0)), tab, tab, tab], out_specs=out_specs,
        scratch_shapes=[_stage(tm)],
        compiler_params=_cparams(("parallel",), 4 * _nbytes((tm, 3 * att_w), BF16)), name=name)(proj, *tables)
    return res[0], [tuple(res[1 + 3 * n:4 + 3 * n]) for n in range(len(dilated))]


def _rope_bwd(dq0, dk0, dv0, grouped, tables, name):
    t = dq0.shape[0]
    tm = _rows(t, 512)
    att_w = N_GROUPS * GROUP_WIDTH
    dilated = [(gi, dil) for gi, dil in enumerate(DILATIONS) if dil > 1]
    c, s_up, s_dn = tables

    def body(c_ref, up_ref, dn_ref, dq0_ref, dk0_ref, dv0_ref, *rest):
        g_refs, o_ref, stage = rest[:-2], rest[-2], rest[-1]
        cv, uv, dv = c_ref[...], -up_ref[...], -dn_ref[...]
        for part, first in enumerate((dq0_ref, dk0_ref)):
            for gi, dil in enumerate(DILATIONS):
                if dil > 1:
                    _from_groups(g_refs[3 * dilated.index((gi, dil)) + part], stage, dil)
                for h in range(HEADS_PER_GROUP):
                    sl = slice(h * HEAD_DIM, (h + 1) * HEAD_DIM)
                    xv = first[:, sl].astype(F32) if dil == 1 else stage[h]
                    col = part * att_w + gi * GROUP_WIDTH + h * HEAD_DIM
                    o_ref[:, col:col + HEAD_DIM] = _rotate(xv, cv, uv, dv).astype(BF16)
        for gi, dil in enumerate(DILATIONS):
            col = 2 * att_w + gi * GROUP_WIDTH
            if dil == 1:
                o_ref[:, col:col + GROUP_WIDTH] = dv0_ref[...]
            else:
                _from_groups(g_refs[3 * dilated.index((gi, dil)) + 2], stage, dil)
                for h in range(HEADS_PER_GROUP):
                    o_ref[:, col + h * HEAD_DIM:col + (h + 1) * HEAD_DIM] = stage[h].astype(BF16)

    tab = pl.BlockSpec((tm, HEAD_DIM), lambda i: (i, 0))
    nat = pl.BlockSpec((tm, GROUP_WIDTH), lambda i: (i, 0))
    in_specs, operands = [tab, tab, tab, nat, nat, nat], [c, s_up, s_dn, dq0, dk0, dv0]
    for (_, dil), arrs in zip(dilated, grouped):
        in_specs += [_group_spec(tm, dil)] * 3
        operands += list(arrs)
    return _pallas_call(
        body, out_shape=jax.ShapeDtypeStruct((t, 3 * att_w), BF16), grid=(t // tm,), in_specs=in_specs,
        out_specs=pl.BlockSpec((tm, 3 * att_w), lambda i: (i, 0)), scratch_shapes=[_stage(tm)],
        compiler_params=_cparams(("parallel",), 4 * _nbytes((tm, 3 * att_w), BF16)), name=name)(*operands)


def _regroup(arrs, name):
    t = arrs[0].shape[0]
    tm = _rows(t, 512)
    dilated = [dil for dil in DILATIONS if dil > 1]
    n_in = len(arrs)

    def body(*refs):
        ins, outs, stage = refs[:n_in], refs[n_in:-1], refs[-1]
        for j, x_ref in enumerate(ins):
            for h in range(HEADS_PER_GROUP):
                stage[h] = x_ref[:, h * HEAD_DIM:(h + 1) * HEAD_DIM]
            for n, dil in enumerate(dilated):
                _to_groups(stage, outs[n * n_in + j], dil)

    nat = pl.BlockSpec((tm, GROUP_WIDTH), lambda i: (i, 0))
    res = _pallas_call(
        body, out_shape=[_group_shape(t, dil, F32) for dil in dilated for _ in arrs], grid=(t // tm,),
        in_specs=[nat] * n_in, out_specs=[_group_spec(tm, dil) for dil in dilated for _ in arrs], scratch_shapes=[_stage(tm)],
        compiler_params=_cparams(("parallel",), 3 * n_in * _nbytes((tm, GROUP_WIDTH), F32)), name=name)(*arrs)
    return [tuple(res[n * n_in:(n + 1) * n_in]) for n in range(len(dilated))]


def _query_mask(has_prev):
    qi = lax.broadcasted_iota(jnp.int32, (ATT_BLOCK, 2 * ATT_BLOCK), 0)
    col = lax.broadcasted_iota(jnp.int32, (ATT_BLOCK, 2 * ATT_BLOCK), 1)
    prev = jnp.logical_and(jnp.logical_and(col < ATT_BLOCK, col >= qi), has_prev)
    return jnp.logical_or(prev, jnp.logical_and(col >= ATT_BLOCK, col - ATT_BLOCK <= qi))


def _key_mask(has_next):
    row = lax.broadcasted_iota(jnp.int32, (2 * ATT_BLOCK, ATT_BLOCK), 0)
    kj = lax.broadcasted_iota(jnp.int32, (2 * ATT_BLOCK, ATT_BLOCK), 1)
    nxt = jnp.logical_and(jnp.logical_and(row >= ATT_BLOCK, kj >= row - ATT_BLOCK), has_next)
    return jnp.logical_or(nxt, jnp.logical_and(row < ATT_BLOCK, kj <= row))


def _scores(q, k):
    return lax.dot_general(q, k, (((1,), (1,)), ((), ())), preferred_element_type=F32) * (HEAD_DIM ** -0.5)


def _att_fwd(q, k, v, offs, dil, name):
    qo, ko, vo = offs
    length = q.shape[0]
    nb = length // ATT_BLOCK

    def body(q_ref, kp_ref, kc_ref, vp_ref, vc_ref, o_ref, lse_ref):
        mask = _query_mask(pl.program_id(1) > 0)
        heads = [slice(h * HEAD_DIM, (h + 1) * HEAD_DIM) for h in range(HEADS_PER_GROUP)]
        ks = [jnp.concatenate([kp_ref[:, sl], kc_ref[:, sl]], axis=0) for sl in heads]
        vs = [jnp.concatenate([vp_ref[:, sl], vc_ref[:, sl]], axis=0) for sl in heads]
        ss = [jnp.where(mask, _scores(q_ref[:, sl], kv), MASKED) for sl, kv in zip(heads, ks)]
        ms = [jnp.max(s, axis=1, keepdims=True) for s in ss]
        ps = [jnp.exp(s - m) for s, m in zip(ss, ms)]
        ls = [jnp.sum(p, axis=1, keepdims=True) for p in ps]
        accs = [jnp.dot(p.astype(BF16), vv, preferred_element_type=F32) for p, vv in zip(ps, vs)]
        for sl, acc, m, l in zip(heads, accs, ms, ls):
            o_ref[:, sl] = acc / l
            lse_ref[:, sl] = jnp.broadcast_to(m + jnp.log(l), (ATT_BLOCK, HEAD_DIM))

    def spec(off, prev):
        if prev:
            return pl.BlockSpec((ATT_BLOCK, GROUP_WIDTH), lambda r, n: (jnp.maximum(n - 1, 0), off + r))
        return pl.BlockSpec((ATT_BLOCK, GROUP_WIDTH), lambda r, n: (n, off + r))

    out = jax.ShapeDtypeStruct((length, dil * GROUP_WIDTH), F32)
    o_spec = pl.BlockSpec((ATT_BLOCK, GROUP_WIDTH), lambda r, n: (n, r))
    return _pallas_call(
        body, out_shape=(out, out), grid=(dil, nb),
        in_specs=[spec(qo, False), spec(ko, True), spec(ko, False), spec(vo, True), spec(vo, False)],
        out_specs=(o_spec, o_spec),
        compiler_params=_cparams(("parallel", "parallel"), 8 * _nbytes((ATT_BLOCK, GROUP_WIDTH), F32)), name=name)(q, k, k, v, v)


def _att_combine(outs, lses, name):
    t = outs[0].shape[0] * DILATIONS[0]
    tm = _rows(t, 512)

    def body(*refs):
        o_refs, l_refs = refs[:N_GROUPS], refs[N_GROUPS:2 * N_GROUPS]
        ob_ref, of_ref, lse_ref = refs[2 * N_GROUPS:2 * N_GROUPS + 3]
        stages = list(refs[2 * N_GROUPS + 3:])
        staged = []
        for o_ref, l_ref, dil in zip(o_refs, l_refs, DILATIONS):
            if dil > 1:
                so, sl = stages.pop(), stages.pop()
                _from_groups(o_ref, so, dil)
                _from_groups(l_ref, sl, dil)
                staged.append((so, sl))
            else:
                staged.append(None)
        for h in range(HEADS_PER_GROUP):
            hs = slice(h * HEAD_DIM, (h + 1) * HEAD_DIM)
            os_ = [o_ref[:, hs] if st is None else st[0][h] for o_ref, st in zip(o_refs, staged)]
            ls = [l_ref[:, hs] if st is None else st[1][h] for l_ref, st in zip(l_refs, staged)]
            m = functools.reduce(jnp.maximum, ls)
            ws = [jnp.exp(l - m) for l in ls]
            den = functools.reduce(jnp.add, ws)
            num = functools.reduce(jnp.add, [w * o for w, o in zip(ws, os_)])
            o = num / den
            ob_ref[:, hs] = o.astype(BF16)
            of_ref[:, hs] = o
            lse_ref[:, hs] = m + jnp.log(den)

    blk = pl.BlockSpec((tm, GROUP_WIDTH), lambda i: (i, 0))
    specs = [blk if dil == 1 else _group_spec(tm, dil) for dil in DILATIONS]
    f32 = jax.ShapeDtypeStruct((t, GROUP_WIDTH), F32)
    n_stage = 2 * sum(dil > 1 for dil in DILATIONS)
    return _pallas_call(
        body, out_shape=(jax.ShapeDtypeStruct((t, GROUP_WIDTH), BF16), f32, f32), grid=(t // tm,),
        in_specs=specs * 2, out_specs=(blk, blk, blk), scratch_shapes=[_stage(tm)] * n_stage,
        compiler_params=_cparams(("parallel",), 13 * _nbytes((tm, GROUP_WIDTH), F32)), name=name)(*outs, *lses)


def _att_delta(do, o, name):
    t = o.shape[0]
    tm = _rows(t, 512)

    def body(do_ref, o_ref, d_ref):
        for h in range(HEADS_PER_GROUP):
            sl = slice(h * HEAD_DIM, (h + 1) * HEAD_DIM)
            s = jnp.sum(do_ref[:, sl] * o_ref[:, sl], axis=1, keepdims=True)
            d_ref[:, sl] = jnp.broadcast_to(s, (tm, HEAD_DIM))

    blk = pl.BlockSpec((tm, GROUP_WIDTH), lambda i: (i, 0))
    return _pallas_call(
        body, out_shape=jax.ShapeDtypeStruct((t, GROUP_WIDTH), F32), grid=(t // tm,), in_specs=[blk, blk], out_specs=blk,
        compiler_params=_cparams(("parallel",), 3 * _nbytes((tm, GROUP_WIDTH), F32)), name=name)(do, o)


def _att_bwd_dq(q, k, v, do, lse, delta, offs, dil, name):
    qo, ko, vo = offs
    length = q.shape[0]
    nb = length // ATT_BLOCK
    scale = HEAD_DIM ** -0.5

    def body(q_ref, kp_ref, kc_ref, vp_ref, vc_ref, do_ref, lse_ref, dl_ref, dq_ref):
        mask = _query_mask(pl.program_id(1) > 0)
        heads = [slice(h * HEAD_DIM, (h + 1) * HEAD_DIM) for h in range(HEADS_PER_GROUP)]
        wide = lambda ref, sl: jnp.concatenate([ref[:, sl], ref[:, sl]], axis=1)
        ks = [jnp.concatenate([kp_ref[:, sl], kc_ref[:, sl]], axis=0) for sl in heads]
        vs = [jnp.concatenate([vp_ref[:, sl], vc_ref[:, sl]], axis=0) for sl in heads]
        ps = [jnp.exp(jnp.where(mask, _scores(q_ref[:, sl], kv), MASKED) - wide(lse_ref, sl)) for sl, kv in zip(heads, ks)]
        dps = [lax.dot_general(do_ref[:, sl].astype(BF16), vv, (((1,), (1,)), ((), ())), preferred_element_type=F32)
               for sl, vv in zip(heads, vs)]
        dss = [(p * (dp - wide(dl_ref, sl)) * scale).astype(BF16) for sl, p, dp in zip(heads, ps, dps)]
        dqs = [jnp.dot(ds, kv, preferred_element_type=F32) for ds, kv in zip(dss, ks)]
        for sl, dq in zip(heads, dqs):
            dq_ref[:, sl] = dq.astype(BF16)

    def spec(off, prev):
        if prev:
            return pl.BlockSpec((ATT_BLOCK, GROUP_WIDTH), lambda r, n: (jnp.maximum(n - 1, 0), off + r))
        return pl.BlockSpec((ATT_BLOCK, GROUP_WIDTH), lambda r, n: (n, off + r))

    own = pl.BlockSpec((ATT_BLOCK, GROUP_WIDTH), lambda r, n: (n, r))
    return _pallas_call(
        body, out_shape=jax.ShapeDtypeStruct((length, dil * GROUP_WIDTH), BF16), grid=(dil, nb),
        in_specs=[spec(qo, False), spec(ko, True), spec(ko, False), spec(vo, True), spec(vo, False), own, own, own],
        out_specs=own,
        compiler_params=_cparams(("parallel", "parallel"), 10 * _nbytes((ATT_BLOCK, GROUP_WIDTH), F32)),
        name=name)(q, k, k, v, v, do, lse, delta)


def _att_bwd_dkv(q, k, v, do, lse, delta, offs, dil, name):
    qo, ko, vo = offs
    length = q.shape[0]
    nb = length // ATT_BLOCK
    scale = HEAD_DIM ** -0.5

    def body(k_ref, v_ref, qc_ref, qn_ref, doc_ref, don_ref, lsec_ref, lsen_ref, dlc_ref, dln_ref, dk_ref, dv_ref):
        mask = _key_mask(pl.program_id(1) < nb - 1)
        heads = [slice(h * HEAD_DIM, (h + 1) * HEAD_DIM) for h in range(HEADS_PER_GROUP)]
        both = lambda cur, nxt, sl: jnp.concatenate([cur[:, sl], nxt[:, sl]], axis=0)
        qs = [both(qc_ref, qn_ref, sl) for sl in heads]
        dos = [both(doc_ref, don_ref, sl).astype(BF16) for sl in heads]
        ps = [jnp.exp(jnp.where(mask, _scores(qv, k_ref[:, sl]), MASKED) - both(lsec_ref, lsen_ref, sl)) for sl, qv in zip(heads, qs)]
        dps = [lax.dot_general(dov, v_ref[:, sl], (((1,), (1,)), ((), ())), preferred_element_type=F32) for sl, dov in zip(heads, dos)]
        dss = [(p * (dp - both(dlc_ref, dln_ref, sl)) * scale).astype(BF16) for sl, p, dp in zip(heads, ps, dps)]
        dvs = [lax.dot_general(p.astype(BF16), dov, (((0,), (0,)), ((), ())), preferred_element_type=F32) for p, dov in zip(ps, dos)]
        dks = [lax.dot_general(ds, qv, (((0,), (0,)), ((), ())), preferred_element_type=F32) for ds, qv in zip(dss, qs)]
        for sl, dk, dv in zip(heads, dks, dvs):
            dk_ref[:, sl] = dk.astype(BF16)
            dv_ref[:, sl] = dv.astype(BF16)

    def spec(off, nxt):
        if nxt:
            return pl.BlockSpec((ATT_BLOCK, GROUP_WIDTH), lambda r, n: (jnp.minimum(n + 1, nb - 1), off + r))
        return pl.BlockSpec((ATT_BLOCK, GROUP_WIDTH), lambda r, n: (n, off + r))

    own = pl.BlockSpec((ATT_BLOCK, GROUP_WIDTH), lambda r, n: (n, r))
    out = jax.ShapeDtypeStruct((length, dil * GROUP_WIDTH), BF16)
    return _pallas_call(
        body, out_shape=(out, out), grid=(dil, nb),
        in_specs=[spec(ko, False), spec(vo, False), spec(qo, False), spec(qo, True), spec(0, False), spec(0, True),
                  spec(0, False), spec(0, True), spec(0, False), spec(0, True)],
        out_specs=(own, own),
        compiler_params=_cparams(("parallel", "parallel"), 12 * _nbytes((ATT_BLOCK, GROUP_WIDTH), F32)),
        name=name)(k, v, q, q, do, do, lse, lse, delta, delta)


def _gelu(x):
    return 0.5 * x * (1.0 + lax.erf(x * (2.0 ** -0.5)))


def _gelu_grad(x):
    return 0.5 * (1.0 + lax.erf(x * (2.0 ** -0.5))) + x * jnp.exp(-0.5 * x * x) * ((2.0 * jnp.pi) ** -0.5)


def _sg_normed(vs, lg, lb):
    gv = _gelu(vs)
    mu = jnp.mean(gv, axis=1, keepdims=True)
    xc = gv - mu
    rstd = lax.rsqrt(jnp.mean(xc * xc, axis=1, keepdims=True) + LN_EPS)
    z = xc * rstd
    return z, rstd, z * lg + lb


def _sg_tril():
    row = lax.broadcasted_iota(jnp.int32, (SG_CHUNK, SG_CHUNK), 0)
    col = lax.broadcasted_iota(jnp.int32, (SG_CHUNK, SG_CHUNK), 1)
    return row >= col


def _sg_fwd(proj, u_blk, vs_blk, lg, lb, sg_w, bias, name):
    t = proj.shape[0]
    width = SG_GROUPS * SG_GROUP_DIM

    def body(u_ref, vs_ref, lg_ref, lb_ref, w_ref, bias_ref, o_ref):
        _, _, vn = _sg_normed(vs_ref[...].astype(F32), lg_ref[...], lb_ref[...])
        vn = vn.astype(BF16)
        tril = _sg_tril()
        for g in range(SG_GROUPS):
            sl = slice(g * SG_GROUP_DIM, (g + 1) * SG_GROUP_DIM)
            w = jnp.where(tril, w_ref[g], 0.0).astype(BF16)
            sp = jnp.dot(w, vn[:, sl], preferred_element_type=F32) + bias_ref[:, sl]
            o_ref[:, sl] = (_gelu(u_ref[:, sl].astype(F32)) * sp).astype(BF16)

    vec = pl.BlockSpec((1, width), lambda i: (0, 0))
    return _pallas_call(
        body, out_shape=jax.ShapeDtypeStruct((t, width), BF16), grid=(t // SG_CHUNK,),
        in_specs=[pl.BlockSpec((SG_CHUNK, width), lambda i: (i, u_blk)), pl.BlockSpec((SG_CHUNK, width), lambda i: (i, vs_blk)),
                  vec, vec, pl.BlockSpec((SG_GROUPS, SG_CHUNK, SG_CHUNK), lambda i: (0, 0, 0)),
                  pl.BlockSpec((SG_CHUNK, width), lambda i: (0, 0))],
        out_specs=pl.BlockSpec((SG_CHUNK, width), lambda i: (i, 0)),
        compiler_params=_cparams(("parallel",), 8 * _nbytes((SG_CHUNK, width), F32)), name=name)(proj, proj, lg, lb, sg_w, bias)


def _sg_bwd(proj, u_blk, vs_blk, dsu, lg, lb, sg_w, bias, name):
    t = proj.shape[0]
    width = SG_GROUPS * SG_GROUP_DIM

    def body(u_ref, vs_ref, dsu_ref, lg_ref, lb_ref, w_ref, bias_ref, du_ref, dvs_ref, dw_ref, dbias_ref, dlg_ref, dlb_ref):
        @pl.when(pl.program_id(0) == 0)
        def _():
            dw_ref[...] = jnp.zeros_like(dw_ref)
            dbias_ref[...] = jnp.zeros_like(dbias_ref)
            dlg_ref[...] = jnp.zeros_like(dlg_ref)
            dlb_ref[...] = jnp.zeros_like(dlb_ref)

        vs = vs_ref[...].astype(F32)
        z, rstd, vn = _sg_normed(vs, lg_ref[...], lb_ref[...])
        vn = vn.astype(BF16)
        tril = _sg_tril()
        dvn = []
        for g in range(SG_GROUPS):
            sl = slice(g * SG_GROUP_DIM, (g + 1) * SG_GROUP_DIM)
            w = jnp.where(tril, w_ref[g], 0.0).astype(BF16)
            vg = vn[:, sl]
            sp = jnp.dot(w, vg, preferred_element_type=F32) + bias_ref[:, sl]
            uv = u_ref[:, sl].astype(F32)
            dsu_g = dsu_ref[:, sl].astype(F32)
            du_ref[:, sl] = (dsu_g * sp * _gelu_grad(uv)).astype(BF16)
            dsp = dsu_g * _gelu(uv)
            dsp_b = dsp.astype(BF16)
            dw = lax.dot_general(dsp_b, vg, (((1,), (1,)), ((), ())), preferred_element_type=F32)
            dw_ref[g] += jnp.where(tril, dw, 0.0)
            dbias_ref[:, sl] += jnp.broadcast_to(jnp.sum(dsp, axis=1, keepdims=True), (SG_CHUNK, SG_GROUP_DIM))
            dvn.append(lax.dot_general(w, dsp_b, (((0,), (0,)), ((), ())), preferred_element_type=F32))
        dvn = jnp.concatenate(dvn, axis=1)
        dlg_ref[...] += jnp.sum(dvn * z, axis=0, keepdims=True)
        dlb_ref[...] += jnp.sum(dvn, axis=0, keepdims=True)
        dz = dvn * lg_ref[...]
        dgv = rstd * (dz - jnp.mean(dz, axis=1, keepdims=True) - z * jnp.mean(dz * z, axis=1, keepdims=True))
        dvs_ref[...] = (dgv * _gelu_grad(vs)).astype(BF16)

    vec = pl.BlockSpec((1, width), lambda i: (0, 0))
    row = pl.BlockSpec((SG_CHUNK, width), lambda i: (i, 0))
    fixed = pl.BlockSpec((SG_CHUNK, width), lambda i: (0, 0))
    w_spec = pl.BlockSpec((SG_GROUPS, SG_CHUNK, SG_CHUNK), lambda i: (0, 0, 0))
    act = jax.ShapeDtypeStruct((t, width), BF16)
    return _pallas_call(
        body,
        out_shape=(act, act, jax.ShapeDtypeStruct((SG_GROUPS, SG_CHUNK, SG_CHUNK), F32),
                   jax.ShapeDtypeStruct((SG_CHUNK, width), F32), jax.ShapeDtypeStruct((1, width), F32),
                   jax.ShapeDtypeStruct((1, width), F32)),
        grid=(t // SG_CHUNK,),
        in_specs=[pl.BlockSpec((SG_CHUNK, width), lambda i: (i, u_blk)), pl.BlockSpec((SG_CHUNK, width), lambda i: (i, vs_blk)),
                  row, vec, vec, w_spec, fixed],
        out_specs=(row, row, w_spec, fixed, vec, vec),
        compiler_params=_cparams(("arbitrary",), 14 * _nbytes((SG_CHUNK, width), F32)),
        name=name)(proj, proj, dsu, lg, lb, sg_w, bias)


def _gate_fwd(proj, ga_blk, gs_blk, y_att, y_sg, name):
    t, d = y_att.shape
    tm, tn = _rows(t, 512), _tile(d, GROUP_WIDTH)

    def body(ga_ref, gs_ref, ya_ref, ys_ref, o_ref):
        o_ref[...] = (_sigmoid(ga_ref[...].astype(F32)) * ya_ref[...].astype(F32)
                      + _sigmoid(gs_ref[...].astype(F32)) * ys_ref[...].astype(F32)).astype(BF16)

    own = pl.BlockSpec((tm, tn), lambda i, j: (i, j))
    return _pallas_call(
        body, out_shape=jax.ShapeDtypeStruct((t, d), BF16), grid=(t // tm, d // tn),
        in_specs=[pl.BlockSpec((tm, tn), lambda i, j: (i, ga_blk + j)), pl.BlockSpec((tm, tn), lambda i, j: (i, gs_blk + j)),
                  own, own],
        out_specs=own, compiler_params=_cparams(("parallel", "parallel"), 6 * _nbytes((tm, tn), F32)),
        name=name)(proj, proj, y_att, y_sg)


def _gate_bwd(proj, ga_blk, gs_blk, y_att, y_sg, dmerged, name):
    t, d = y_att.shape
    tm, tn = _rows(t, 512), _tile(d, GROUP_WIDTH)

    def body(ga_ref, gs_ref, ya_ref, ys_ref, dm_ref, dya_ref, dys_ref, dga_ref, dgs_ref):
        dm = dm_ref[...].astype(F32)
        for g_ref, y_ref, dy_ref, dg_ref in ((ga_ref, ya_ref, dya_ref, dga_ref), (gs_ref, ys_ref, dys_ref, dgs_ref)):
            sg = _sigmoid(g_ref[...].astype(F32))
            dy_ref[...] = (dm * sg).astype(BF16)
            dg_ref[...] = (dm * y_ref[...].astype(F32) * sg * (1.0 - sg)).astype(BF16)

    own = pl.BlockSpec((tm, tn), lambda i, j: (i, j))
    out = jax.ShapeDtypeStruct((t, d), BF16)
    return _pallas_call(
        body, out_shape=(out, out, out, out), grid=(t // tm, d // tn),
        in_specs=[pl.BlockSpec((tm, tn), lambda i, j: (i, ga_blk + j)), pl.BlockSpec((tm, tn), lambda i, j: (i, gs_blk + j)),
                  own, own, own],
        out_specs=(own, own, own, own), compiler_params=_cparams(("parallel", "parallel"), 10 * _nbytes((tm, tn), F32)),
        name=name)(proj, proj, y_att, y_sg, dmerged)


def _mixer_forward(x, wb, small, in_specs, sg_specs, out_specs):
    t, d = x.shape
    att_w = N_GROUPS * GROUP_WIDTH
    sg_w = SG_GROUPS * SG_GROUP_DIM
    n = _rmsnorm_fwd(x, small['mix_norm'], "mix_norm")
    rider, names = _gather(wb, in_specs)
    proj, got = _matmul([(n, _full(wb, 'w_in'))], 'nn', BF16, "mix_in", b3=True, caps=(1024, 1024, 1024), rider=rider)
    _landed(wb, names, got)
    tables = _rope_tables(t)
    qk0, grouped = _rope_fwd(proj, tables, "mix_rope")
    qkv = [(qk0, qk0, proj, (0, 1, 2 * N_GROUPS))] + [g + ((0, 0, 0),) for g in grouped]
    outs, lses = zip(*[_att_fwd(*args, dil, f"att_fwd{gi}") for gi, (args, dil) in enumerate(zip(qkv, DILATIONS))])
    o_b, o_f, lse = _att_combine(outs, lses, "att_combine")
    y_att = _matmul([(o_b, _full(wb, 'w_att_out'))], 'nn', BF16, "mix_att_out", b3=True)
    bias = jnp.repeat(small['sg_b'].T, SG_GROUP_DIM, axis=1)
    u_blk, vs_blk = 3 * att_w // sg_w, 3 * att_w // sg_w + 1
    su = _sg_fwd(proj, u_blk, vs_blk, small['sg_ln_g'], small['sg_ln_b'], small['sg_w'], bias, "sg_fwd")
    rider, names = _gather(wb, sg_specs)
    y_sg, got = _matmul([(su, _full(wb, 'w_sg_out'))], 'nn', BF16, "mix_sg_out", b3=True, rider=rider)
    _landed(wb, names, got)
    ga_blk = (3 * att_w + 2 * sg_w) // _tile(d, GROUP_WIDTH)
    gs_blk = ga_blk + d // _tile(d, GROUP_WIDTH)
    merged = _gate_fwd(proj, ga_blk, gs_blk, y_att, y_sg, "gate_fwd")
    rider, names = _gather(wb, out_specs)
    x_next, got = _matmul([(merged, _full(wb, 'w_out'))], 'nn', F32, "mix_out", residual=x, rider=rider)
    _landed(wb, names, got)
    saved = (n, proj, qkv, tables, o_b, o_f, lse, y_att, su, y_sg, merged, bias, (u_blk, vs_blk, ga_blk, gs_blk))
    return x_next, saved


def _mixer_backward(x, wb, small, saved, dx_next, dx_next_b, c_idx, first_rider, pending):
    n, proj, qkv, tables, o_b, o_f, lse, y_att, su, y_sg, merged, bias, (u_blk, vs_blk, ga_blk, gs_blk) = saved
    s = N_CHIPS
    dmerged, carried = _matmul([(dx_next_b, _full(wb, 'w_out'))], 'nt', BF16, "mix_out_dx", rider=first_rider)
    g_w_out = _matmul([(merged, dx_next_b)], 'tn', BF16, "mix_out_dw", caps=(1024, 1024, 1024))
    dy_att, dy_sg, dg_att, dg_sg = _gate_bwd(proj, ga_blk, gs_blk, y_att, y_sg, dmerged, "gate_bwd")

    g_w_att_out = _matmul([(o_b, dy_att)], 'tn', BF16, "mix_att_out_dw", out3=s)
    do = _matmul([(dy_att, _full(wb, 'w_att_out'))], 'nt', F32, "mix_att_out_dx", b3=True)
    delta = _att_delta(do, o_f, "att_delta")
    stats = [(do, lse, delta)] + _regroup([do, lse, delta], "att_regroup")
    dqkv = []
    for gi, ((q, k, v, offs), st, dil) in enumerate(zip(qkv, stats, DILATIONS)):
        dq = _att_bwd_dq(q, k, v, *st, offs, dil, f"att_bwd_dq{gi}")
        dk, dv = _att_bwd_dkv(q, k, v, *st, offs, dil, f"att_bwd_dkv{gi}")
        dqkv.append((dq, dk, dv))
    dqkv = _rope_bwd(*dqkv[0], dqkv[1:], tables, "mix_rope_bwd")

    g_w_sg_out = _matmul([(su, dy_sg)], 'tn', BF16, "mix_sg_out_dw", out3=s)
    out_names = ['w_out', 'w_att_out', 'w_sg_out']
    out_g4 = [a.reshape(wb[nm].shape) for a, nm in zip([g_w_out, g_w_att_out, g_w_sg_out], out_names)]
    dsu, from_sibling = _matmul([(dy_sg, _full(wb, 'w_sg_out'))], 'nt', BF16, "mix_sg_out_dx", b3=True, rider=_sibling_rider(out_g4))
    out_parts = [_sibling_sum(a, b, c_idx, f"rs_sum1_{nm}") for a, b, nm in zip(out_g4, from_sibling, out_names)]
    out_names, out_parts = out_names + list(pending), out_parts + list(pending.values())
    du, dvs, g_sg_w, g_bias, g_lg, g_lb = _sg_bwd(proj, u_blk, vs_blk, dsu, small['sg_ln_g'], small['sg_ln_b'],
                                                   small['sg_w'], bias, "sg_bwd")
    gs = {'sg_w': g_sg_w, 'sg_b': g_bias[:, ::SG_GROUP_DIM].T, 'sg_ln_g': g_lg, 'sg_ln_b': g_lb}

    dproj = jnp.concatenate([dqkv, du, dvs, dg_att, dg_sg], axis=1)
    g_w_in, out_recv = _matmul([(n, dproj)], 'tn', BF16, "mix_in_dw", out3=s, caps=(1024, 1024, 1024),
                               rider=_scatter_rider(out_parts))
    (p_w_in,) = _reduce_first([g_w_in], ['w_in'], wb, c_idx)
    dn, (r_w_in,) = _matmul([(dproj, _full(wb, 'w_in'))], 'nt', BF16, "mix_in_dx", b3=True, caps=(1024, 1024, 512),
                            rider=_scatter_rider([p_w_in]))
    dx, dx_b, gs['mix_norm'] = _rmsnorm_bwd(x, small['mix_norm'], dn, dx_next, "mix_norm_bwd")
    g = {nm: (p, r) for nm, p, r in zip(out_names, out_parts, out_recv)}
    g['w_in'] = (p_w_in, r_w_in)
    return dx, dx_b, g, gs, carried


def _step(x, target, wb, small, c_idx, pc_idx):
    def last_stage(g):
        names = list(g)
        return names, _halves_rider([_chip_sum(*g[n], pc_idx, f"rs_sum2_{n}") for n in names])

    wb = dict(wb)
    rider, names = _gather(wb, ['ffn1_w_gate', 'ffn1_w_up'])
    _landed(wb, names, _exchange(rider, "gather_first"))
    half_in = wb['w_in'].shape[2] // 2
    x1, s1 = _ffn_forward(x, small['ffn1_norm'], wb, "ffn1", ['ffn1_w_down', ('w_in', 0, half_in)], [('w_in', half_in, 2 * half_in)])
    up_rows = wb['ffn2_w_up'].shape[2]
    up_cut = up_rows // 32 * 15
    x2, s2 = _mixer_forward(x1, wb, small, ['w_att_out', 'w_sg_out', 'w_out', 'ffn2_w_gate'],
                            [('ffn2_w_up', 0, up_cut)], [('ffn2_w_up', up_cut, up_rows)])
    x3, s3 = _ffn_forward(x2, small['ffn2_norm'], wb, "ffn2", ['ffn2_w_down'], None)
    loss, dx3, dx3_b, g_final = _final_loss(x3, small['final_norm'], target, "final_loss")
    gs = {'final_norm': g_final}
    whole = {}
    dx2, dx2_b, gs['ffn2_norm'], g, pending, _ = _ffn_backward(x2, small['ffn2_norm'], wb, s3, dx3, dx3_b, c_idx, "ffn2", True)
    names, rider = last_stage(g)
    dx1, dx1_b, g, gs_mix, got = _mixer_backward(x1, wb, small, s2, dx2, dx2_b, c_idx, rider, pending)
    whole.update(zip(names, got))
    gs.update(gs_mix)
    names, rider = last_stage(g)
    dx0, _, gs['ffn1_norm'], g, _, got = _ffn_backward(x, small['ffn1_norm'], wb, s1, dx1, dx1_b, c_idx, "ffn1", False,
                                                       dwd_rider=rider)
    whole.update(zip(names, got))
    names, rider = last_stage(g)
    whole.update(zip(names, _exchange(rider, "rs_halves")))
    return loss, dx0, whole, gs


def _cast_into_gathered(wt, p_idx, name):
    r, ccols = wt.shape[0] // 2, wt.shape[1]
    tm = _rows(r, 512)
    nb = r // tm

    def body(p_ref, w_ref, o_ref):
        o_ref[...] = w_ref[...].astype(BF16)

    grid_spec = pltpu.PrefetchScalarGridSpec(
        num_scalar_prefetch=1, grid=(2, nb),
        in_specs=[pl.BlockSpec((tm, ccols), lambda h, i, pr: (h * nb + i, 0))],
        out_specs=pl.BlockSpec((None, None, tm, ccols), lambda h, i, pr: (pr[0], h, i, 0)))
    return pl.pallas_call(body, out_shape=jax.ShapeDtypeStruct((N_CHIPS, 2, r, ccols), BF16), grid_spec=grid_spec,
                          compiler_params=_cparams(("parallel", "parallel"), 2 * _nbytes((tm, ccols), F32)), name=name)(p_idx, wt)


def _sibling_rider(grads):
    n = len(grads)

    def copy(src, dst, sems, i):
        x, y, c, _ = _place()
        return pltpu.make_async_remote_copy(src[i].at[:, 1 - c], dst[i], sems[0].at[i], sems[1].at[i],
                                            device_id=(x, y, 1 - c), device_id_type=MESH)

    def start(src, dst, sems):
        for i in range(n):
            copy(src, dst, sems, i).start()

    def finish(src, dst, sems):
        for i in range(n):
            copy(src, dst, sems, i).wait()

    return _Rider(grads, [jax.ShapeDtypeStruct((g.shape[0],) + g.shape[2:], g.dtype) for g in grads], {},
                  [pltpu.SemaphoreType.DMA((n,))] * 2, start, finish)


def _merge_riders(a, b):
    n_in, n_out, n_sem = len(a.operands), len(a.out_shapes), len(a.sems)

    def both(which):
        def run(ins, outs, sems):
            getattr(a, which)(ins[:n_in], outs[:n_out], sems[:n_sem])
            getattr(b, which)(ins[n_in:], outs[n_out:], sems[n_sem:])
        return run

    aliases = dict(a.aliases)
    aliases.update({n_in + k: n_out + v for k, v in b.aliases.items()})
    return _Rider(list(a.operands) + list(b.operands), list(a.out_shapes) + list(b.out_shapes), aliases,
                  list(a.sems) + list(b.sems), both('start'), both('finish'))


def _halves_rider(bufs):
    n = len(bufs)

    def copy(ref, sems, i, c, x, y):
        return pltpu.make_async_remote_copy(ref, ref, sems[0].at[i], sems[1].at[i], device_id=(x, y, 1 - c), device_id_type=MESH)

    def start(_, buf, sems):
        x, y, c, _ = _place()
        for i in range(n):
            copy(buf[i].at[c], sems, i, c, x, y).start()

    def finish(_, buf, sems):
        x, y, c, _ = _place()
        for i in range(n):
            copy(buf[i].at[c], sems, i, c, x, y).wait_send()
            copy(buf[i].at[1 - c], sems, i, c, x, y).wait_recv()

    return _Rider(bufs, [jax.ShapeDtypeStruct(b.shape, b.dtype) for b in bufs], {i: i for i in range(n)},
                  [pltpu.SemaphoreType.DMA((n,))] * 2, start, finish)


def _sibling_sum(grad, recv, c_idx, name):
    s, _, r, ccols = grad.shape
    tm = _rows(r, 512)

    def body(c_ref, g_ref, r_ref, o_ref):
        o_ref[...] = (g_ref[...].astype(F32) + r_ref[...].astype(F32)).astype(BF16)

    grid_spec = pltpu.PrefetchScalarGridSpec(
        num_scalar_prefetch=1, grid=(s, r // tm),
        in_specs=[pl.BlockSpec((None, None, tm, ccols), lambda q, i, cr: (q, cr[0], i, 0)),
                  pl.BlockSpec((None, tm, ccols), lambda q, i, cr: (q, i, 0))],
        out_specs=pl.BlockSpec((None, tm, ccols), lambda q, i, cr: (q, i, 0)))
    return pl.pallas_call(body, out_shape=jax.ShapeDtypeStruct((s, r, ccols), BF16), grid_spec=grid_spec,
                          compiler_params=_cparams(("parallel", "parallel"), 4 * _nbytes((tm, ccols), F32)), name=name)(c_idx, grad, recv)


def _chip_sum(part, recv, pc_idx, name):
    _, r, ccols = part.shape
    tm = _rows(r, 512)

    def body(pc_ref, own_ref, r0_ref, r1_ref, r2_ref, o_ref):
        acc = own_ref[...].astype(F32) + r0_ref[...].astype(F32)
        acc = acc + r1_ref[...].astype(F32)
        o_ref[...] = acc + r2_ref[...].astype(F32)

    def slot(j):
        return pl.BlockSpec((None, tm, ccols), lambda i, pc: (j, i, 0))

    grid_spec = pltpu.PrefetchScalarGridSpec(
        num_scalar_prefetch=1, grid=(r // tm,),
        in_specs=[pl.BlockSpec((None, tm, ccols), lambda i, pc: (pc[0], i, 0)), slot(0), slot(1), slot(2)],
        out_specs=pl.BlockSpec((None, tm, ccols), lambda i, pc: (pc[1], i, 0)))
    return pl.pallas_call(body, out_shape=jax.ShapeDtypeStruct((2, r, ccols), F32), grid_spec=grid_spec,
                          compiler_params=_cparams(("parallel",), 6 * _nbytes((tm, ccols), F32)), name=name)(pc_idx, part, recv, recv, recv)


def _all_reduce_small(vec):
    _, r, _ = vec.shape

    def body(v_ref, o_ref, parts, send1, recv1, send2, recv2):
        x, y, c, _ = _place()
        me = 4 * x + 2 * y + c
        peers = []
        for k in range(1, N_DEV):
            px, py, pc = (1 - x if k & 4 else x, 1 - y if k & 2 else y, 1 - c if k & 1 else c)
            peers.append(((px, py, pc), 4 * px + 2 * py + pc))
        parts[me] = v_ref[me]
        cps = []
        for k, (peer, peer_id) in enumerate(peers):
            cp = pltpu.make_async_remote_copy(v_ref.at[peer_id], parts.at[me], send1.at[k], recv1.at[k],
                                              device_id=peer, device_id_type=MESH)
            cp.start()
            cps.append(cp)
        for cp in cps:
            cp.wait()
        acc = parts[0]
        for dev in range(1, N_DEV):
            acc = acc + parts[dev]
        o_ref[me] = acc
        cps = []
        for k, (peer, _) in enumerate(peers):
            cp = pltpu.make_async_remote_copy(o_ref.at[me], o_ref.at[me], send2.at[k], recv2.at[k],
                                              device_id=peer, device_id_type=MESH)
            cp.start()
            cps.append(cp)
        for cp in cps:
            cp.wait()

    vm = pl.BlockSpec(memory_space=pltpu.VMEM)
    sems = pltpu.SemaphoreType.DMA((N_DEV - 1,))
    return pl.pallas_call(
        body, out_shape=jax.ShapeDtypeStruct(vec.shape, F32), in_specs=[vm], out_specs=vm,
        scratch_shapes=[pltpu.VMEM((N_DEV, r, LANES), F32), sems, sems, sems, sems],
        compiler_params=pltpu.CompilerParams(vmem_limit_bytes=int(8 * _nbytes((N_DEV, r, LANES), F32))),
        name="all_reduce_small")(vec)


def _adamw(wt, g, m, v, name, rider=None):
    r, ccols = wt.shape
    tm = _rows(r, max(8, (2 * MIB // (4 * ccols)) // 8 * 8))
    blk = pl.BlockSpec((tm, ccols), lambda i: (i, 0))

    def body(w_ref, g_ref, m_ref, v_ref, go_ref, d_ref, mo_ref, vo_ref):
        gv = g_ref[...]
        go_ref[...] = gv
        mv = ADAM_B1 * m_ref[...] + (1.0 - ADAM_B1) * gv
        vv = ADAM_B2 * v_ref[...] + (1.0 - ADAM_B2) * (gv * gv)
        m_hat = mv / (1.0 - ADAM_B1 ** ADAM_STEP)
        v_hat = vv / (1.0 - ADAM_B2 ** ADAM_STEP)
        d_ref[...] = -ADAM_LR * (m_hat / (jnp.sqrt(v_hat) + ADAM_EPS) + ADAM_WD * w_ref[...])
        mo_ref[...] = mv
        vo_ref[...] = vv

    out = jax.ShapeDtypeStruct((r, ccols), F32)
    return _run(body, name=name, grid=(r // tm,), in_specs=[blk] * 4, out_specs=[blk] * 4, out_shape=[out] * 4, scratch_shapes=[],
                operands=[wt, g, m, v], block_bytes=8 * _nbytes((tm, ccols), F32), rider=rider, pinned=False)


def _as_rows(a):
    rows = a.reshape(-1, LANES)
    return jnp.pad(rows, ((0, -rows.shape[0] % 8), (0, 0)))


def kernel(x, ffn1_norm, ffn1_w_gate, ffn1_w_up, ffn1_w_down, mix_norm, w_in, sg_ln_g, sg_ln_b, sg_w, sg_b, w_att_out, w_sg_out, w_out, ffn2_norm, ffn2_w_gate, ffn2_w_up, ffn2_w_down, final_norm, loss_target, m_ffn1_norm, m_ffn1_w_gate, m_ffn1_w_up, m_ffn1_w_down, m_mix_norm, m_w_in, m_sg_ln_g, m_sg_ln_b, m_sg_w, m_sg_b, m_w_att_out, m_w_sg_out, m_w_out, m_ffn2_norm, m_ffn2_w_gate, m_ffn2_w_up, m_ffn2_w_down, m_final_norm, v_ffn1_norm, v_ffn1_w_gate, v_ffn1_w_up, v_ffn1_w_down, v_mix_norm, v_w_in, v_sg_ln_g, v_sg_ln_b, v_sg_w, v_sg_b, v_w_att_out, v_w_sg_out, v_w_out, v_ffn2_norm, v_ffn2_w_gate, v_ffn2_w_up, v_ffn2_w_down, v_final_norm):
    given = dict(locals())
    wts = {n: given[n] for n in WEIGHT_NAMES}
    ms = {n: given["m_" + n] for n in WEIGHT_NAMES}
    vs = {n: given["v_" + n] for n in WEIGHT_NAMES}
    t, d = x.shape[-2], x.shape[-1]
    xc, yc, cc = lax.axis_index("x"), lax.axis_index("y"), lax.axis_index("c")

    shard2d = {n: wts[n].reshape(wts[n].shape[-2:]) for n in BIG_NAMES}
    p_idx = jnp.reshape(2 * xc + yc, (1,)).astype(jnp.int32)
    c_idx = jnp.reshape(cc, (1,)).astype(jnp.int32)
    pc_idx = jnp.stack([2 * xc + yc, cc]).astype(jnp.int32)
    wb = {n: _cast_into_gathered(shard2d[n], p_idx, f"cast_{n}") for n in BIG_NAMES}

    small = {n: wts[n].reshape(-1, wts[n].shape[-1]) for n in SMALL_NAMES}
    small['sg_w'] = wts['sg_w'].reshape(wts['sg_w'].shape[-3:])
    loss, dx, whole, gs = _step(x.reshape(t, d), loss_target.reshape(t, d), wb, small, c_idx, pc_idx)
    loss = lax.psum(loss[0, 0], ("x", "y", "c"))

    def pack(tree):
        rows = jnp.concatenate([_as_rows(tree[n]) for n in SMALL_NAMES], axis=0)
        return jnp.pad(rows, ((0, -rows.shape[0] % (8 * N_DEV)), (0, 0)))

    packed = pack(gs)
    packed = _all_reduce_small(packed.reshape(N_DEV, -1, LANES)).reshape(packed.shape)

    grads, delta, new_m, new_v = {}, {}, {}, {}
    for n in BIG_NAMES:
        shape, flat = wts[n].shape, shard2d[n].shape
        out = _adamw(shard2d[n], whole[n].reshape(flat), ms[n].reshape(flat), vs[n].reshape(flat), f"adamw_{n}")
        grads[n], delta[n], new_m[n], new_v[n] = (a.reshape(shape) for a in out)

    small_out = _adamw(pack(wts), packed, pack(ms), pack(vs), "adamw_small")
    row = 0
    for n in SMALL_NAMES:
        shape = wts[n].shape
        sz = wts[n].size // LANES
        grads[n], delta[n], new_m[n], new_v[n] = (a[row:row + sz].reshape(shape) for a in small_out)
        row += sz + -sz % 8

    return (loss, dx.reshape(x.shape), *[grads[n] for n in WEIGHT_NAMES], *[delta[n] for n in WEIGHT_NAMES],
            *[new_m[n] for n in WEIGHT_NAMES], *[new_v[n] for n in WEIGHT_NAMES])
```
